```python
import math
import jax, jax.numpy as jnp
from jax import lax
import numpy as np

D_MODEL = 1024
BATCH = 2
SEQ = 8192
DEPTH = 1
DEC_BATCH = 128
DEC_SEQ = 4
PAST_LEN = 8192
PAGE_SIZE = 128

SSM_HEADS = 16
SSM_HEAD_DIM = 64
SSM_WIDTH = SSM_HEADS * SSM_HEAD_DIM
SSM_GROUPS = 2
SSM_STATE = 128
CONV_WIDTH = 4
CONV_DIM = SSM_WIDTH + 2 * SSM_GROUPS * SSM_STATE
SSD_CHUNK = 128
DT_MIN = 0.001
DT_MAX = 0.1
ATTN_HEADS = 16
ATTN_KV_HEADS = 4
Q_PER_KV = ATTN_HEADS // ATTN_KV_HEADS
ATTN_HEAD_DIM = 64
ATTN_WIDTH = ATTN_HEADS * ATTN_HEAD_DIM
KV_WIDTH = ATTN_KV_HEADS * ATTN_HEAD_DIM
WINDOW = 128
ATTN_BLOCK = 128
ATTN_SCALE = ATTN_HEAD_DIM ** -0.5
REL_BUCKETS = 32
REL_MAX_DIST = 128
MIX_WIDTH = SSM_WIDTH + ATTN_WIDTH
IN_WIDTH = SSM_WIDTH + CONV_DIM + SSM_HEADS + ATTN_WIDTH + 2 * KV_WIDTH + ATTN_WIDTH
EPS = 1e-6

kernel_name = "hymba_ssd_swa_sink_decode_step"


def rmsnorm(x, w):
    xf = x.astype(jnp.float32)
    y = xf * lax.rsqrt(jnp.mean(xf * xf, axis=-1, keepdims=True) + EPS)
    return (y * w.astype(jnp.float32)).astype(x.dtype)


def rel_bucket(dist):
    max_exact = REL_BUCKETS // 2
    d_f = jnp.maximum(dist, 1).astype(jnp.float32)
    large = max_exact + (jnp.log(d_f / max_exact) / math.log(REL_MAX_DIST / max_exact)
                         * (REL_BUCKETS - max_exact)).astype(jnp.int32)
    return jnp.where(dist < max_exact, dist, jnp.minimum(large, REL_BUCKETS - 1))


def band_bias_mask(dist, rel_table):
    mask = (dist >= 0) & (dist <= WINDOW)
    bias = rel_table[rel_bucket(jnp.clip(dist, 0, WINDOW))].astype(jnp.float32)
    bias = jnp.transpose(bias, (2, 0, 1)).reshape(ATTN_KV_HEADS, Q_PER_KV, dist.shape[0], dist.shape[1])
    return bias, mask


def sink_attention(q, k, v, bias, mask, sinks):
    s = jnp.einsum("...qngd,...snd->...ngqs", q, k).astype(jnp.float32) * ATTN_SCALE + bias
    s = jnp.where(mask, s, -jnp.inf)
    sink = jnp.broadcast_to(sinks.astype(jnp.float32).reshape(ATTN_KV_HEADS, Q_PER_KV, 1, 1),
                            s.shape[:-1] + (1,))
    p = jax.nn.softmax(jnp.concatenate([s, sink], axis=-1), axis=-1)[..., :-1]
    return jnp.einsum("...ngqs,...snd->...qngd", p.astype(v.dtype), v)


def window_attention_prompt(q, k, v, sinks, rel_table):
    b, l = q.shape[:2]
    T = ATTN_BLOCK
    nb = l // T
    qb = q.reshape(b, nb, T, ATTN_KV_HEADS, Q_PER_KV, ATTN_HEAD_DIM)

    def with_prev(t):
        t = t.reshape(b, nb, T, ATTN_KV_HEADS, ATTN_HEAD_DIM)
        prev = jnp.concatenate([jnp.zeros_like(t[:, :1]), t[:, :-1]], axis=1)
        return jnp.concatenate([prev, t], axis=2)

    kpos = jnp.arange(2 * T) - T
    dist = jnp.arange(T)[:, None] - kpos[None, :]
    bias, mask = band_bias_mask(dist, rel_table)
    first_ok = (jnp.arange(nb)[:, None, None] > 0) | (kpos >= 0)[None, None, :]
    mask = (mask[None] & first_ok)[:, None, None]
    o = sink_attention(qb, with_prev(k), with_prev(v), bias, mask, sinks)
    return o.reshape(b, l, ATTN_WIDTH)


def window_attention_sample(q, k, v, k_prev, v_prev, sinks, rel_table):
    b, l = q.shape[:2]
    keys = jnp.concatenate([k_prev, k], axis=1)
    vals = jnp.concatenate([v_prev, v], axis=1)
    dist = (jnp.arange(l) + WINDOW)[:, None] - jnp.arange(WINDOW + l)[None, :]
    bias, mask = band_bias_mask(dist, rel_table)
    o = sink_attention(q.reshape(b, l, ATTN_KV_HEADS, Q_PER_KV, ATTN_HEAD_DIM), keys, vals, bias, mask, sinks)
    return o.reshape(b, l, ATTN_WIDTH), keys[:, -WINDOW:], vals[:, -WINDOW:]


def causal_dwconv(u, prev, w, bias):
    full = jnp.concatenate([prev, u], axis=1)
    l = u.shape[1]
    out = bias + full[:, 0:l] * w[0]
    for tap in range(1, CONV_WIDTH):
        out = out + full[:, tap:tap + l] * w[tap]
    return jax.nn.silu(out), full[:, -(CONV_WIDTH - 1):]


def ssd_chunked(x, dt, A, B, C, init_state, chunk):
    b, l, h, p = x.shape
    g, n = B.shape[-2:]
    e = h // g
    c = l // chunk
    f32 = jnp.float32
    xc = x.reshape(b, c, chunk, g, e, p).astype(f32)
    dtc = dt.reshape(b, c, chunk, g, e)
    Bc = B.reshape(b, c, chunk, g, n).astype(f32)
    Cc = C.reshape(b, c, chunk, g, n).astype(f32)
    a_cum = jnp.cumsum(dtc * A.reshape(g, e), axis=2)
    xdt = xc * dtc[..., None]
    causal = jnp.tril(jnp.ones((chunk, chunk), bool))[None, None, :, :, None, None]
    seg = a_cum[:, :, :, None] - a_cum[:, :, None, :]
    decay = jnp.exp(jnp.where(causal, seg, -jnp.inf))
    cb = jnp.einsum("bclgn,bcsgn->bclsg", Cc, Bc)
    y_diag = jnp.einsum("bclsg,bclsge,bcsgep->bclgep", cb, decay, xdt)
    decay_to_end = jnp.exp(a_cum[:, :, -1:] - a_cum)
    chunk_states = jnp.einsum("bclgn,bclge,bclgep->bcgepn", Bc, decay_to_end, xdt)
    chunk_decay = jnp.exp(a_cum[:, :, -1])

    def step(carry, inp):
        st, dec = inp
        return carry * dec[..., None, None] + st, carry

    final, prev = lax.scan(step, init_state.reshape(b, g, e, p, n).astype(f32),
                           (jnp.swapaxes(chunk_states, 0, 1), jnp.swapaxes(chunk_decay, 0, 1)))
    prev = jnp.swapaxes(prev, 0, 1)
    y_off = jnp.einsum("bclgn,bcgepn,bclge->bclgep", Cc, prev, jnp.exp(a_cum))
    y = (y_diag + y_off).reshape(b, l, h, p)
    return y, final.reshape(b, h, p, n).astype(init_state.dtype)


def hybrid_layer(x, conv_prev, ssm_prev, k_prev, v_prev, norm_w, w_in, conv_w, conv_b, dt_bias,
                 a_log, d_skip, ssm_norm_w, q_norm_w, k_norm_w, sinks, rel_table, w_out):
    is_prompt = k_prev is None
    b, l, _ = x.shape
    f32 = jnp.float32
    h = rmsnorm(x, norm_w)
    u = jnp.einsum("bld,de->ble", h, w_in)
    sizes = (SSM_WIDTH, CONV_DIM, SSM_HEADS, ATTN_WIDTH, KV_WIDTH, KV_WIDTH, ATTN_WIDTH)
    z_ssm, xbc, dt_raw, q, k, v, z_attn = jnp.split(u, np.cumsum(sizes)[:-1].tolist(), axis=-1)

    xbc, conv_new = causal_dwconv(xbc, conv_prev, conv_w, conv_b)
    xs, Bm, Cm = jnp.split(xbc, [SSM_WIDTH, SSM_WIDTH + SSM_GROUPS * SSM_STATE], axis=-1)
    xs = xs.reshape(b, l, SSM_HEADS, SSM_HEAD_DIM)
    Bm = Bm.reshape(b, l, SSM_GROUPS, SSM_STATE)
    Cm = Cm.reshape(b, l, SSM_GROUPS, SSM_STATE)
    dt = jax.nn.softplus(dt_raw.astype(f32) + dt_bias.astype(f32))
    A = -jnp.exp(a_log.astype(f32))
    chunk = SSD_CHUNK if is_prompt else l
    y, ssm_new = ssd_chunked(xs, dt, A, Bm, Cm, ssm_prev, chunk)
    y = y + d_skip.astype(f32)[:, None] * xs.astype(f32)
    gy = (y.reshape(b, l, SSM_WIDTH) * jax.nn.silu(z_ssm.astype(f32)))
    gy = gy.reshape(b, l, SSM_GROUPS, SSM_WIDTH // SSM_GROUPS)
    gy = gy * lax.rsqrt(jnp.mean(gy * gy, axis=-1, keepdims=True) + EPS)
    y_ssm = (gy.reshape(b, l, SSM_WIDTH) * ssm_norm_w.astype(f32)).astype(x.dtype)

    q = rmsnorm(q.reshape(b, l, ATTN_HEADS, ATTN_HEAD_DIM), q_norm_w)
    k = rmsnorm(k.reshape(b, l, ATTN_KV_HEADS, ATTN_HEAD_DIM), k_norm_w)
    v = v.reshape(b, l, ATTN_KV_HEADS, ATTN_HEAD_DIM)
    if is_prompt:
        o = window_attention_prompt(q, k, v, sinks, rel_table)
        k_new, v_new = k[:, -WINDOW:], v[:, -WINDOW:]
    else:
        o, k_new, v_new = window_attention_sample(q, k, v, k_prev, v_prev, sinks, rel_table)
    y_attn = o * jax.nn.silu(z_attn)

    out = jnp.einsum("ble,ed->bld", jnp.concatenate([y_ssm, y_attn], axis=-1), w_out)
    return x + out, conv_new, ssm_new, k_new, v_new


def setup_inputs(seed: int = 0) -> dict:
    key = jax.random.key(seed)
    ks = jax.random.split(key, 20)
    f32 = jnp.float32
    nrm = jax.random.normal
    x_prompt = nrm(ks[0], (BATCH, SEQ, D_MODEL), f32)
    x_sample = nrm(ks[1], (DEC_BATCH, DEC_SEQ, D_MODEL), f32)
    cache_k = nrm(ks[2], (DEPTH, DEC_BATCH, WINDOW, ATTN_KV_HEADS, ATTN_HEAD_DIM), f32)
    cache_v = nrm(ks[3], (DEPTH, DEC_BATCH, WINDOW, ATTN_KV_HEADS, ATTN_HEAD_DIM), f32)
    state_ssm = 0.1 * nrm(ks[4], (DEPTH, DEC_BATCH, SSM_HEADS, SSM_HEAD_DIM, SSM_STATE), f32)
    state_conv = nrm(ks[5], (DEPTH, DEC_BATCH, CONV_WIDTH - 1, CONV_DIM), f32)
    norm_w = 1.0 + 0.02 * nrm(ks[6], (DEPTH, D_MODEL), f32)
    w_in = nrm(ks[7], (DEPTH, D_MODEL, IN_WIDTH), f32) * D_MODEL ** -0.5
    conv_w = nrm(ks[8], (DEPTH, CONV_WIDTH, CONV_DIM), f32) * CONV_WIDTH ** -0.5
    conv_b = 0.02 * nrm(ks[9], (DEPTH, CONV_DIM), f32)
    dt0 = jnp.exp(jax.random.uniform(ks[10], (DEPTH, SSM_HEADS), f32, math.log(DT_MIN), math.log(DT_MAX)))
    dt_bias = dt0 + jnp.log(-jnp.expm1(-dt0))
    a_log = jnp.log(jax.random.uniform(ks[11], (DEPTH, SSM_HEADS), f32, 1.0, 16.0))
    d_skip = 1.0 + 0.1 * nrm(ks[12], (DEPTH, SSM_HEADS), f32)
    ssm_norm_w = 1.0 + 0.02 * nrm(ks[13], (DEPTH, SSM_WIDTH), f32)
    q_norm_w = 1.0 + 0.02 * nrm(ks[14], (DEPTH, ATTN_HEAD_DIM), f32)
    k_norm_w = 1.0 + 0.02 * nrm(ks[15], (DEPTH, ATTN_HEAD_DIM), f32)
    sinks = 0.5 * nrm(ks[16], (DEPTH, ATTN_HEADS), f32)
    rel_table = 0.1 * nrm(ks[17], (REL_BUCKETS, ATTN_HEADS), f32)
    w_out = nrm(ks[18], (DEPTH, MIX_WIDTH, D_MODEL), f32) * MIX_WIDTH ** -0.5
    return {"x_prompt": x_prompt, "x_sample": x_sample, "cache_k": cache_k, "cache_v": cache_v,
            "state_ssm": state_ssm, "state_conv": state_conv, "norm_w": norm_w, "w_in": w_in,
            "conv_w": conv_w, "conv_b": conv_b, "dt_bias": dt_bias, "a_log": a_log, "d_skip": d_skip,
            "ssm_norm_w": ssm_norm_w, "q_norm_w": q_norm_w, "k_norm_w": k_norm_w, "sinks": sinks,
            "rel_table": rel_table, "w_out": w_out}


def reference(x_prompt, x_sample, cache_k, cache_v, state_ssm, state_conv, norm_w, w_in, conv_w, conv_b,
              dt_bias, a_log, d_skip, ssm_norm_w, q_norm_w, k_norm_w, sinks, rel_table, w_out):
    yp, ys = x_prompt, x_sample
    kp_l, vp_l, sp_l, cp_l = [], [], [], []
    ks_l, vs_l, ss_l, cs_l = [], [], [], []
    for layer in range(DEPTH):
        lw = dict(norm_w=norm_w[layer], w_in=w_in[layer], conv_w=conv_w[layer], conv_b=conv_b[layer],
                  dt_bias=dt_bias[layer], a_log=a_log[layer], d_skip=d_skip[layer],
                  ssm_norm_w=ssm_norm_w[layer], q_norm_w=q_norm_w[layer], k_norm_w=k_norm_w[layer],
                  sinks=sinks[layer], rel_table=rel_table, w_out=w_out[layer])
        conv0 = jnp.zeros((yp.shape[0], CONV_WIDTH - 1, CONV_DIM), yp.dtype)
        ssm0 = jnp.zeros((yp.shape[0], SSM_HEADS, SSM_HEAD_DIM, SSM_STATE), yp.dtype)
        yp, cp, sp, kp, vp = hybrid_layer(yp, conv0, ssm0, None, None, **lw)
        ys, cs, ss, kss, vss = hybrid_layer(ys, state_conv[layer], state_ssm[layer],
                                            cache_k[layer], cache_v[layer], **lw)
        kp_l.append(kp); vp_l.append(vp); sp_l.append(sp); cp_l.append(cp)
        ks_l.append(kss); vs_l.append(vss); ss_l.append(ss); cs_l.append(cs)
    return (yp, ys, jnp.stack(kp_l), jnp.stack(vp_l), jnp.stack(sp_l), jnp.stack(cp_l),
            jnp.stack(ks_l), jnp.stack(vs_l), jnp.stack(ss_l), jnp.stack(cs_l))
```

```python
import functools
import math

import numpy as np
import jax
import jax.numpy as jnp
from jax import lax
from jax.experimental import pallas as pl
from jax.experimental.pallas import tpu as pltpu

F32 = jnp.float32
BF16 = jnp.bfloat16

D_MODEL = 1024
SSM_HEADS = 16
SSM_HEAD_DIM = 64
SSM_WIDTH = SSM_HEADS * SSM_HEAD_DIM
SSM_GROUPS = 2
SSM_STATE = 128
GROUP_WIDTH = SSM_WIDTH // SSM_GROUPS
BC_WIDTH = SSM_GROUPS * SSM_STATE
CONV_WIDTH = 4
CONV_DIM = SSM_WIDTH + 2 * BC_WIDTH
CHUNK = 128
ATTN_HEADS = 16
ATTN_KV_HEADS = 4
Q_PER_KV = ATTN_HEADS // ATTN_KV_HEADS
ATTN_HEAD_DIM = 64
ATTN_WIDTH = ATTN_HEADS * ATTN_HEAD_DIM
KV_WIDTH = ATTN_KV_HEADS * ATTN_HEAD_DIM
WINDOW = 128
ATTN_SCALE = ATTN_HEAD_DIM ** -0.5
REL_BUCKETS = 32
REL_MAX_DIST = 128
EPS = 1e-6
NEG = -1e30

LANES = 128
SUBLANES = 8
VMEM_LIMIT = 56 * 1024 * 1024
PROJ_ROWS = 512
SAMPLE_BATCH_BLOCK = 8


def _dot(a, b):
    return jnp.dot(a, b, preferred_element_type=F32)


def _dot_nt(a, b):
    return lax.dot_general(a, b, (((1,), (1,)), ((), ())), preferred_element_type=F32)


def _dot_tn(a, b):
    return lax.dot_general(a, b, (((0,), (0,)), ((), ())), preferred_element_type=F32)


def _split2(v):
    hi = v.astype(BF16)
    lo = (v - hi.astype(F32)).astype(BF16)
    return hi, lo


def _dot_sel(v, m):
    hi, lo = _split2(v)
    return _dot(hi, m) + _dot(lo, m)


def _dot_sel3(m, v):
    hi = v.astype(BF16)
    r1 = v - hi.astype(F32)
    mid = r1.astype(BF16)
    lo = (r1 - mid.astype(F32)).astype(BF16)
    return _dot(m, hi) + _dot(m, mid) + _dot(m, lo)


def _silu(x):
    return x / (1.0 + jnp.exp(-x))


def _softplus(x):
    return jnp.maximum(x, 0.0) + jnp.log1p(jnp.exp(-jnp.abs(x)))


def _params(*sem):
    return pltpu.CompilerParams(dimension_semantics=sem, vmem_limit_bytes=VMEM_LIMIT)


def _const_spec(shape):
    nd = len(shape)
    return pl.BlockSpec(shape, lambda *_: (0,) * nd)


def _inproj_body(x_ref, nw_ref, *refs):
    n = len(refs) // 2
    x = x_ref[...]
    ms = jnp.mean(x * x, axis=-1, keepdims=True)
    h = (x * lax.rsqrt(ms + EPS) * nw_ref[...]).astype(BF16)
    for w_ref, o_ref in zip(refs[:n], refs[n:]):
        o_ref[...] = _dot(h, w_ref[...])


def _inproj(x2d, norm_w, weights):
    rows = x2d.shape[0]
    tm = min(PROJ_ROWS, rows)
    in_specs = [pl.BlockSpec((tm, D_MODEL), lambda i: (i, 0)), _const_spec((1, D_MODEL))]
    in_specs += [pl.BlockSpec(w.shape, lambda i: (0, 0), pipeline_mode=pl.Buffered(1)) for w in weights]
    out_specs = [pl.BlockSpec((tm, w.shape[1]), lambda i: (i, 0)) for w in weights]
    out_shape = [jax.ShapeDtypeStruct((rows, w.shape[1]), F32) for w in weights]
    return pl.pallas_call(
        _inproj_body, grid=(rows // tm,), in_specs=in_specs, out_specs=out_specs, out_shape=out_shape,
        compiler_params=_params("parallel"), name="inproj")(x2d, norm_w, *weights)


def _outproj_body(ys_ref, ya_ref, x_ref, wt_ref, wb_ref, o_ref):
    acc = _dot(ys_ref[...].astype(BF16), wt_ref[...]) + _dot(ya_ref[...].astype(BF16), wb_ref[...])
    o_ref[...] = x_ref[...] + acc


def _outproj(y_ssm, y_attn, x2d, w_top, w_bot):
    rows = x2d.shape[0]
    tm = min(PROJ_ROWS, rows)
    row_spec = pl.BlockSpec((tm, D_MODEL), lambda i: (i, 0))
    w_spec = pl.BlockSpec((SSM_WIDTH, D_MODEL), lambda i: (0, 0), pipeline_mode=pl.Buffered(1))
    return pl.pallas_call(
        _outproj_body, grid=(rows // tm,), in_specs=[row_spec, row_spec, row_spec, w_spec, w_spec],
        out_specs=row_spec, out_shape=jax.ShapeDtypeStruct((rows, D_MODEL), F32),
        compiler_params=_params("parallel"), name="outproj")(y_ssm, y_attn, x2d, w_top, w_bot)


def _group_rmsnorm(gy, norm_w):
    parts = []
    for g in range(SSM_GROUPS):
        blk = gy[:, g * GROUP_WIDTH:(g + 1) * GROUP_WIDTH]
        ms = jnp.mean(blk * blk, axis=-1, keepdims=True)
        parts.append(blk * lax.rsqrt(ms + EPS))
    return jnp.concatenate(parts, axis=1) * norm_w


def _ssd_prompt_body(z_ref, xbc_ref, dt_ref, cw_ref, cb_ref, dtb_ref, alog_ref, dskip_ref, nw_ref, e_ref,
                     y_ref, st_ref, state_sc, ext_sc):
    c = pl.program_id(1)
    tail = CONV_WIDTH - 1

    @pl.when(c == 0)
    def _():
        state_sc[...] = jnp.zeros_like(state_sc)
        ext_sc[0:SUBLANES, :] = jnp.zeros((SUBLANES, CONV_DIM), F32)

    u = xbc_ref[...]
    ext_sc[SUBLANES:SUBLANES + CHUNK, :] = u
    conv = cb_ref[...]
    for tap in range(CONV_WIDTH):
        conv = conv + ext_sc[pl.ds(SUBLANES - tail + tap, CHUNK), :] * cw_ref[tap:tap + 1, :]
    ext_sc[0:SUBLANES, :] = u[CHUNK - SUBLANES:, :]
    act = _silu(conv)
    xs = act[:, :SSM_WIDTH]
    b_bf = act[:, SSM_WIDTH:SSM_WIDTH + BC_WIDTH].astype(BF16)
    c_bf = act[:, SSM_WIDTH + BC_WIDTH:].astype(BF16)
    xs_bf = xs.astype(BF16)

    dt = _softplus(dt_ref[...] + dtb_ref[...])
    a = dt * (-jnp.exp(alog_ref[...]))
    li = lax.broadcasted_iota(jnp.int32, (CHUNK, CHUNK), 0)
    si = lax.broadcasted_iota(jnp.int32, (CHUNK, CHUNK), 1)
    causal = li >= si
    a_cum = _dot_sel3(jnp.where(causal, 1.0, 0.0).astype(BF16), a)
    a_cum_t = a_cum.T
    dt_t = dt.T
    e_mat = e_ref[...]
    ea_full = _dot_sel(jnp.exp(a_cum), e_mat)
    w_full = _dot_sel(dt * jnp.exp(a_cum[CHUNK - 1:CHUNK, :] - a_cum), e_mat)

    cb = [_dot_nt(c_bf[:, g * SSM_STATE:(g + 1) * SSM_STATE], b_bf[:, g * SSM_STATE:(g + 1) * SSM_STATE])
          for g in range(SSM_GROUPS)]
    half = lax.broadcasted_iota(jnp.int32, (CHUNK, LANES), 1) < SSM_HEAD_DIM
    heads_per_group = SSM_HEADS // SSM_GROUPS
    y_parts = []
    for pair in range(SSM_HEADS // 2):
        blocks = []
        for h in (2 * pair, 2 * pair + 1):
            seg = a_cum[:, h:h + 1] - a_cum_t[h:h + 1, :]
            decay = jnp.exp(jnp.where(causal, seg, -jnp.inf))
            blocks.append((cb[h // heads_per_group] * decay * dt_t[h:h + 1, :]).astype(BF16))
        lhs = jnp.concatenate(blocks, axis=1)
        xp = xs_bf[:, pair * LANES:(pair + 1) * LANES]
        zero = jnp.zeros_like(xp)
        rhs = jnp.concatenate([jnp.where(half, xp, zero), jnp.where(half, zero, xp)], axis=0)
        y_parts.append(_dot(lhs, rhs))
    y_diag = jnp.concatenate(y_parts, axis=1)

    state = state_sc[...]
    state_bf = state.astype(BF16)
    xw_bf = (xs * w_full).astype(BF16)
    y_off, upd = [], []
    for g in range(SSM_GROUPS):
        cols = slice(g * GROUP_WIDTH, (g + 1) * GROUP_WIDTH)
        ns = slice(g * SSM_STATE, (g + 1) * SSM_STATE)
        y_off.append(_dot(c_bf[:, ns], state_bf[:, cols]))
        upd.append(_dot_tn(b_bf[:, ns], xw_bf[:, cols]))
    y = y_diag + jnp.concatenate(y_off, axis=1) * ea_full + dskip_ref[...] * xs
    new_state = state * ea_full[CHUNK - 1:CHUNK, :] + jnp.concatenate(upd, axis=1)
    state_sc[...] = new_state

    y_ref[...] = _group_rmsnorm(y * _silu(z_ref[...]), nw_ref[...])

    @pl.when(c == pl.num_programs(1) - 1)
    def _():
        st_ref[0] = new_state.T


def _ssd_prompt(z, xbc, dt, conv_w, conv_b, dtb, alog, dskip, norm_w, e_mat, batch, seq):
    nc = seq // CHUNK
    row = lambda w: pl.BlockSpec((CHUNK, w), lambda b, c: (b * nc + c, 0))
    in_specs = [row(SSM_WIDTH), row(CONV_DIM), row(LANES), _const_spec((CONV_WIDTH, CONV_DIM)),
                _const_spec((1, CONV_DIM)), _const_spec((1, LANES)), _const_spec((1, LANES)),
                _const_spec((1, SSM_WIDTH)), _const_spec((1, SSM_WIDTH)), _const_spec((LANES, SSM_WIDTH))]
    out_specs = [row(SSM_WIDTH), pl.BlockSpec((1, SSM_WIDTH, SSM_STATE), lambda b, c: (b, 0, 0))]
    out_shape = [jax.ShapeDtypeStruct((batch * seq, SSM_WIDTH), F32),
                 jax.ShapeDtypeStruct((batch, SSM_WIDTH, SSM_STATE), F32)]
    scratch = [pltpu.VMEM((SSM_STATE, SSM_WIDTH), F32), pltpu.VMEM((SUBLANES + CHUNK, CONV_DIM), F32)]
    return pl.pallas_call(
        _ssd_prompt_body, grid=(batch, nc), in_specs=in_specs, out_specs=out_specs, out_shape=out_shape,
        scratch_shapes=scratch, compiler_params=_params("arbitrary", "arbitrary"), name="ssd_prompt")(
            z, xbc, dt, conv_w, conv_b, dtb, alog, dskip, norm_w, e_mat)


def _rel_bucket_np(dist):
    max_exact = REL_BUCKETS // 2
    d_f = np.maximum(dist, 1).astype(np.float32)
    large = max_exact + (np.log(d_f / np.float32(max_exact)) / np.float32(math.log(REL_MAX_DIST / max_exact))
                         * np.float32(REL_BUCKETS - max_exact)).astype(np.int32)
    return np.where(dist < max_exact, dist, np.minimum(large, REL_BUCKETS - 1)).astype(np.int32)


def _bucket_or_masked(dist, extra_mask=None):
    ok = (dist >= 0) & (dist <= WINDOW)
    if extra_mask is not None:
        ok = ok & extra_mask
    return np.where(ok, _rel_bucket_np(np.clip(dist, 0, WINDOW)), -1).astype(np.int32)


def _bias_body(rel_ref, bucket_ref, o_ref):
    bucket = bucket_ref[0]

    def per_head(h, carry):
        acc = jnp.full(bucket.shape, NEG, F32)
        for bkt in range(REL_BUCKETS):
            acc = jnp.where(bucket == bkt, rel_ref[bkt * ATTN_HEADS + h], acc)
        o_ref[0, h] = acc
        return carry

    lax.fori_loop(0, ATTN_HEADS, per_head, 0)


def _bias_tables(rel_flat, buckets):
    nv, lq, lk = buckets.shape
    return pl.pallas_call(
        _bias_body, grid=(nv,),
        in_specs=[pl.BlockSpec(memory_space=pltpu.SMEM), pl.BlockSpec((1, lq, lk), lambda v: (v, 0, 0))],
        out_specs=pl.BlockSpec((1, ATTN_HEADS, lq, lk), lambda v: (v, 0, 0, 0)),
        out_shape=jax.ShapeDtypeStruct((nv, ATTN_HEADS, lq, lk), F32),
        compiler_params=_params("arbitrary"), name="rel_bias")(rel_flat, buckets)


def _head_rmsnorm(x, g_mat, e_mat, w):
    ms = _dot_sel(x * x, g_mat) * (1.0 / ATTN_HEAD_DIM)
    return x * _dot_sel(lax.rsqrt(ms + EPS), e_mat) * w


def _lane_head(shape):
    return lax.broadcasted_iota(jnp.int32, shape, 1) // ATTN_HEAD_DIM


def _sink_column(sink_ref, n, rows_per_head):
    return jnp.concatenate(
        [jnp.full((rows_per_head, 1), sink_ref[n * Q_PER_KV + g], F32) for g in range(Q_PER_KV)], axis=0)


def _softmax_with_sink(s, sink):
    m = jnp.maximum(jnp.max(s, axis=-1, keepdims=True), sink)
    p = jnp.exp(s - m)
    denom = jnp.sum(p, axis=-1, keepdims=True) + jnp.exp(sink - m)
    return p / denom


def _attn_prompt_body(sink_ref, q_ref, k_ref, v_ref, z_ref, qw_ref, kw_ref, g_ref, e_ref, bias_ref,
                      y_ref, kn_ref, vn_ref, kcat_sc, vcat_sc):
    T = CHUNK

    @pl.when(pl.program_id(1) == 0)
    def _():
        kcat_sc[0:T, :] = jnp.zeros((T, KV_WIDTH), BF16)
        vcat_sc[0:T, :] = jnp.zeros((T, KV_WIDTH), BF16)

    g_mat = g_ref[...]
    e_mat = e_ref[...]
    qn = (_head_rmsnorm(q_ref[...], g_mat, e_mat, qw_ref[...]) * ATTN_SCALE).astype(BF16)
    kn = _head_rmsnorm(k_ref[...], g_mat[:KV_WIDTH], e_mat[:, :KV_WIDTH], kw_ref[...])
    v = v_ref[...]
    kn_ref[0] = kn
    vn_ref[0] = v
    kcat_sc[T:2 * T, :] = kn.astype(BF16)
    vcat_sc[T:2 * T, :] = v.astype(BF16)
    kcat = kcat_sc[...]
    vcat = vcat_sc[...]

    qstack = jnp.concatenate([qn[:, g * KV_WIDTH:(g + 1) * KV_WIDTH] for g in range(Q_PER_KV)], axis=0)
    lane_head = _lane_head((2 * T, KV_WIDTH))
    zero = jnp.zeros_like(kcat)
    probs, vals = [], []
    for n in range(ATTN_KV_HEADS):
        s = _dot_nt(qstack, jnp.where(lane_head == n, kcat, zero))
        s = s + bias_ref[0, n * Q_PER_KV * T:(n + 1) * Q_PER_KV * T, :]
        probs.append(_softmax_with_sink(s, _sink_column(sink_ref, n, T)).astype(BF16))
        vals.append(jnp.where(lane_head == n, vcat, zero))
    o = _dot(jnp.concatenate(probs, axis=1), jnp.concatenate(vals, axis=0))
    o = jnp.concatenate([o[g * T:(g + 1) * T] for g in range(Q_PER_KV)], axis=1)
    y_ref[...] = o * _silu(z_ref[...])
    kcat_sc[0:T, :] = kcat_sc[T:2 * T, :]
    vcat_sc[0:T, :] = vcat_sc[T:2 * T, :]


def _attn_prompt(sinks, q, k, v, z, qw, kw, g_mat, e_mat, bias, batch, seq):
    nb = seq // CHUNK
    row = lambda w: pl.BlockSpec((CHUNK, w), lambda b, i: (b * nb + i, 0))
    bias_rows = ATTN_HEADS * CHUNK
    in_specs = [pl.BlockSpec(memory_space=pltpu.SMEM), row(ATTN_WIDTH), row(KV_WIDTH), row(KV_WIDTH),
                row(ATTN_WIDTH), _const_spec((1, ATTN_WIDTH)), _const_spec((1, KV_WIDTH)),
                _const_spec((ATTN_WIDTH, LANES)), _const_spec((LANES, ATTN_WIDTH)),
                pl.BlockSpec((1, bias_rows, 2 * CHUNK), lambda b, i: (jnp.minimum(i, 1), 0, 0))]
    kv_out = pl.BlockSpec((1, CHUNK, KV_WIDTH), lambda b, i: (b, 0, 0))
    out_specs = [row(ATTN_WIDTH), kv_out, kv_out]
    out_shape = [jax.ShapeDtypeStruct((batch * seq, ATTN_WIDTH), F32),
                 jax.ShapeDtypeStruct((batch, CHUNK, KV_WIDTH), F32),
                 jax.ShapeDtypeStruct((batch, CHUNK, KV_WIDTH), F32)]
    scratch = [pltpu.VMEM((2 * CHUNK, KV_WIDTH), BF16), pltpu.VMEM((2 * CHUNK, KV_WIDTH), BF16)]
    return pl.pallas_call(
        _attn_prompt_body, grid=(batch, nb), in_specs=in_specs, out_specs=out_specs, out_shape=out_shape,
        scratch_shapes=scratch, compiler_params=_params("arbitrary", "arbitrary"), name="attn_prompt")(
            sinks, q, k, v, z, qw, kw, g_mat, e_mat, bias)


def _ssd_sample_vec_body(z_ref, xbc_ref, dt_ref, sconv_ref, cw_ref, cb_ref, dtb_ref, alog_ref, dskip_ref,
                         gh_ref, e_ref, ypart_ref, ea_ref, xw_ref, b_ref, c_ref, cdec_ref, convnew_ref):
    steps = xbc_ref.shape[0]
    tail = CONV_WIDTH - 1
    full = [sconv_ref[j] for j in range(tail)] + [xbc_ref[l] for l in range(steps)]
    for j in range(tail):
        convnew_ref[j] = full[steps + j]
    gh = gh_ref[...]
    e_mat = e_ref[...]
    a_neg = -jnp.exp(alog_ref[...])
    xs, bm, cm, dts, acum = [], [], [], [], []
    run = None
    for l in range(steps):
        conv = cb_ref[...]
        for tap in range(CONV_WIDTH):
            conv = conv + full[l + tap] * cw_ref[tap:tap + 1, :]
        act = _silu(conv)
        xs.append(act[:, :SSM_WIDTH])
        bm.append(act[:, SSM_WIDTH:SSM_WIDTH + BC_WIDTH])
        cm.append(act[:, SSM_WIDTH + BC_WIDTH:])
        d = _softplus(dt_ref[l] + dtb_ref[...])
        dts.append(d)
        run = d * a_neg if run is None else run + d * a_neg
        acum.append(run)
        b_ref[l] = bm[l]
        c_ref[l] = cm[l]
    for l in range(steps):
        y = dskip_ref[...] * xs[l]
        for s in range(l + 1):
            cb_h = _dot_sel(cm[l] * bm[s], gh)
            coef = cb_h * jnp.exp(acum[l] - acum[s]) * dts[s]
            y = y + _dot_sel(coef, e_mat) * xs[s]
        ypart_ref[l] = y
        ea_ref[l] = _dot_sel(jnp.exp(acum[l]), e_mat)
        xw_ref[l] = xs[l] * _dot_sel(dts[l] * jnp.exp(acum[steps - 1] - acum[l]), e_mat)
    cdec_ref[...] = jnp.exp(acum[steps - 1])


def _ssd_sample_vec(z3, xbc3, dt3, sconv3, conv_w, conv_b, dtb, alog, dskip, gh_mat, e_mat):
    steps, nb = z3.shape[0], z3.shape[1]
    f = lambda *s: jax.ShapeDtypeStruct(s, F32)
    out_shape = [f(steps, nb, SSM_WIDTH), f(steps, nb, SSM_WIDTH), f(steps, nb, SSM_WIDTH),
                 f(steps, nb, BC_WIDTH), f(steps, nb, BC_WIDTH), f(nb, LANES), f(CONV_WIDTH - 1, nb, CONV_DIM)]
    return pl.pallas_call(_ssd_sample_vec_body, out_shape=out_shape,
                          compiler_params=pltpu.CompilerParams(vmem_limit_bytes=VMEM_LIMIT),
                          name="ssd_sample_vec")(
        z3, xbc3, dt3, sconv3, conv_w, conv_b, dtb, alog, dskip, gh_mat, e_mat)


def _ssd_sample_state_body(cdec_ref, st_ref, c_ref, b_ref, xw_ref, new_ref, yoff_ref):
    i = pl.program_id(0)
    heads_per_group = SSM_HEADS // SSM_GROUPS
    for j in range(SAMPLE_BATCH_BLOCK):
        st = st_ref[j]
        cb_bf = c_ref[:, j, :].astype(BF16)
        bb_bf = b_ref[:, j, :].astype(BF16)
        xw_bf = xw_ref[:, j, :].astype(BF16)
        y_parts = []
        for g in range(SSM_GROUPS):
            rows = slice(g * GROUP_WIDTH, (g + 1) * GROUP_WIDTH)
            ns = slice(g * SSM_STATE, (g + 1) * SSM_STATE)
            y_parts.append(_dot_nt(cb_bf[:, ns], st[rows].astype(BF16)))
            upd = _dot_tn(xw_bf[:, rows], bb_bf[:, ns])
            for hh in range(heads_per_group):
                h = g * heads_per_group + hh
                r = slice(h * SSM_HEAD_DIM, (h + 1) * SSM_HEAD_DIM)
                dec = cdec_ref[(i * SAMPLE_BATCH_BLOCK + j) * SSM_HEADS + h]
                new_ref[j, r, :] = st[r] * dec + upd[hh * SSM_HEAD_DIM:(hh + 1) * SSM_HEAD_DIM]
        yoff_ref[:, j, :] = jnp.concatenate(y_parts, axis=1)


def _ssd_sample_state(cdec_flat, state, c3, b3, xw3):
    steps, nb = c3.shape[0], c3.shape[1]
    bb = SAMPLE_BATCH_BLOCK
    tok = lambda w: pl.BlockSpec((steps, bb, w), lambda i: (0, i, 0))
    st_spec = pl.BlockSpec((bb, SSM_WIDTH, SSM_STATE), lambda i: (i, 0, 0))
    return pl.pallas_call(
        _ssd_sample_state_body, grid=(nb // bb,),
        in_specs=[pl.BlockSpec(memory_space=pltpu.SMEM), st_spec, tok(BC_WIDTH), tok(BC_WIDTH), tok(SSM_WIDTH)],
        out_specs=[st_spec, tok(SSM_WIDTH)],
        out_shape=[jax.ShapeDtypeStruct(state.shape, F32), jax.ShapeDtypeStruct((steps, nb, SSM_WIDTH), F32)],
        compiler_params=_params("parallel"), name="ssd_sample_state")(cdec_flat, state, c3, b3, xw3)


def _sample_finish_body(ypart_ref, yoff_ref, ea_ref, z_ref, nw_ref, q_ref, k_ref, qw_ref, kw_ref, g_ref, e_ref,
                        yssm_ref, qn_ref, kn_ref):
    y = ypart_ref[...] + yoff_ref[...] * ea_ref[...]
    yssm_ref[...] = _group_rmsnorm(y * _silu(z_ref[...]), nw_ref[...])
    g_mat = g_ref[...]
    e_mat = e_ref[...]
    qn_ref[...] = _head_rmsnorm(q_ref[...], g_mat, e_mat, qw_ref[...]) * ATTN_SCALE
    kn_ref[...] = _head_rmsnorm(k_ref[...], g_mat[:KV_WIDTH], e_mat[:, :KV_WIDTH], kw_ref[...])


def _sample_finish(ypart, yoff, ea, z, norm_w, q, k, qw, kw, g_mat, e_mat):
    rows = z.shape[0]
    f = lambda w: jax.ShapeDtypeStruct((rows, w), F32)
    return pl.pallas_call(_sample_finish_body, out_shape=[f(SSM_WIDTH), f(ATTN_WIDTH), f(KV_WIDTH)],
                          compiler_params=pltpu.CompilerParams(vmem_limit_bytes=VMEM_LIMIT),
                          name="sample_finish")(ypart, yoff, ea, z, norm_w, q, k, qw, kw, g_mat, e_mat)


def _attn_sample_body(sink_ref, q_ref, kn_ref, vn_ref, z_ref, ck_ref, cv_ref, biasc_ref, biasn_ref,
                      y_ref, ko_ref, vo_ref):
    steps = q_ref.shape[0]
    rows_per_head = steps
    lane_head = _lane_head((Q_PER_KV * steps, KV_WIDTH))
    sink = jnp.concatenate([_sink_column(sink_ref, n, rows_per_head) for n in range(ATTN_KV_HEADS)], axis=0)
    pad = SUBLANES - steps
    for j in range(SAMPLE_BATCH_BLOCK):
        q = q_ref[:, j, :]
        qg = jnp.concatenate([q[:, g * KV_WIDTH:(g + 1) * KV_WIDTH] for g in range(Q_PER_KV)], axis=0)
        zero = jnp.zeros_like(qg)
        qx = jnp.concatenate([jnp.where(lane_head == n, qg, zero) for n in range(ATTN_KV_HEADS)], axis=0)
        qx = qx.astype(BF16)
        ck = ck_ref[j]
        cv = cv_ref[j]
        k_new = kn_ref[:, j, :]
        v_new = vn_ref[:, j, :]
        k_new8 = jnp.concatenate([k_new, jnp.zeros((pad, KV_WIDTH), F32)], axis=0)
        v_new8 = jnp.concatenate([v_new, jnp.zeros((pad, KV_WIDTH), F32)], axis=0)
        s_c = _dot_nt(qx, ck.astype(BF16)) + biasc_ref[...]
        s_n = _dot_nt(qx, k_new8.astype(BF16)) + biasn_ref[...]
        m = jnp.maximum(jnp.maximum(jnp.max(s_c, axis=-1, keepdims=True), jnp.max(s_n, axis=-1, keepdims=True)),
                        sink)
        p_c = jnp.exp(s_c - m)
        p_n = jnp.exp(s_n - m)
        denom = (jnp.sum(p_c, axis=-1, keepdims=True) + jnp.sum(p_n, axis=-1, keepdims=True)
                 + jnp.exp(sink - m))
        o = (_dot((p_c / denom).astype(BF16), cv.astype(BF16))
             + _dot((p_n / denom).astype(BF16), v_new8.astype(BF16)))
        blk = Q_PER_KV * steps
        lane_rows = _lane_head((blk, KV_WIDTH))
        og = jnp.zeros((blk, KV_WIDTH), F32)
        for n in range(ATTN_KV_HEADS):
            og = og + jnp.where(lane_rows == n, o[n * blk:(n + 1) * blk], 0.0)
        y = jnp.concatenate([og[g * steps:(g + 1) * steps] for g in range(Q_PER_KV)], axis=1)
        y_ref[:, j, :] = y * _silu(z_ref[:, j, :])
        ko_ref[j, 0:WINDOW - steps, :] = ck[steps:, :]
        ko_ref[j, WINDOW - steps:WINDOW, :] = k_new
        vo_ref[j, 0:WINDOW - steps, :] = cv[steps:, :]
        vo_ref[j, WINDOW - steps:WINDOW, :] = v_new


def _attn_sample(sinks, q3, kn3, vn3, z3, cache_k, cache_v, bias_c, bias_n):
    steps, nb = q3.shape[0], q3.shape[1]
    bb = SAMPLE_BATCH_BLOCK
    tok = lambda w: pl.BlockSpec((steps, bb, w), lambda i: (0, i, 0))
    cache_spec = pl.BlockSpec((bb, WINDOW, KV_WIDTH), lambda i: (i, 0, 0))
    return pl.pallas_call(
        _attn_sample_body, grid=(nb // bb,),
        in_specs=[pl.BlockSpec(memory_space=pltpu.SMEM), tok(ATTN_WIDTH), tok(KV_WIDTH), tok(KV_WIDTH),
                  tok(ATTN_WIDTH), cache_spec, cache_spec, _const_spec(bias_c.shape), _const_spec(bias_n.shape)],
        out_specs=[tok(ATTN_WIDTH), cache_spec, cache_spec],
        out_shape=[jax.ShapeDtypeStruct((steps, nb, ATTN_WIDTH), F32),
                   jax.ShapeDtypeStruct(cache_k.shape, F32), jax.ShapeDtypeStruct(cache_v.shape, F32)],
        compiler_params=_params("parallel"), name="attn_sample")(
            sinks, q3, kn3, vn3, z3, cache_k, cache_v, bias_c, bias_n)


def _static_tables(steps):
    lanes = np.arange(ATTN_WIDTH)
    g_mat = np.zeros((ATTN_WIDTH, LANES), np.float32)
    g_mat[lanes, lanes // ATTN_HEAD_DIM] = 1.0
    e_mat = g_mat.T.copy()
    bc = np.arange(BC_WIDTH)
    gh_mat = np.zeros((BC_WIDTH, LANES), np.float32)
    for h in range(SSM_HEADS):
        gh_mat[bc // SSM_STATE == h // (SSM_HEADS // SSM_GROUPS), h] = 1.0
    g_i, n_i, d_i = np.meshgrid(np.arange(Q_PER_KV), np.arange(ATTN_KV_HEADS), np.arange(ATTN_HEAD_DIM),
                                indexing="ij")
    perm = ((n_i * Q_PER_KV + g_i) * ATTN_HEAD_DIM + d_i).reshape(-1)
    T = CHUNK
    dist = np.arange(T)[:, None] - (np.arange(2 * T) - T)[None, :]
    first = np.broadcast_to((np.arange(2 * T) >= T)[None, :], dist.shape)
    prompt_buckets = np.stack([_bucket_or_masked(dist, first), _bucket_or_masked(dist)])
    dist_c = (np.arange(steps) + WINDOW)[:, None] - np.arange(WINDOW)[None, :]
    dist_n = np.arange(steps)[:, None] - np.arange(SUBLANES)[None, :]
    real = np.broadcast_to((np.arange(SUBLANES) < steps)[None, :], dist_n.shape)
    return dict(g=g_mat, e=e_mat, gh=gh_mat, perm=perm, prompt_buckets=prompt_buckets,
                cache_buckets=_bucket_or_masked(dist_c)[None], new_buckets=_bucket_or_masked(dist_n, real)[None])


def kernel(x_prompt, x_sample, cache_k, cache_v, state_ssm, state_conv, norm_w, w_in, conv_w, conv_b, dt_bias,
           a_log, d_skip, ssm_norm_w, q_norm_w, k_norm_w, sinks, rel_table, w_out):
    assert w_in.shape[0] == 1, "single-layer kernel"
    batch, seq, _ = x_prompt.shape
    nb, steps, _ = x_sample.shape
    tab = _static_tables(steps)
    perm = tab["perm"]
    g_mat = jnp.asarray(tab["g"], BF16)
    e_mat = jnp.asarray(tab["e"], BF16)
    gh_mat = jnp.asarray(tab["gh"], BF16)

    w = w_in[0]
    edges = np.cumsum([0, SSM_WIDTH, CONV_DIM, SSM_HEADS, ATTN_WIDTH, KV_WIDTH, KV_WIDTH, ATTN_WIDTH])
    wz, wxbc, wdt, wq, wk, wv, wza = [w[:, a:b] for a, b in zip(edges[:-1], edges[1:])]
    wdt = jnp.pad(wdt, ((0, 0), (0, LANES - SSM_HEADS)))
    weights = [m.astype(BF16) for m in (wz, wxbc, wdt, wq[:, perm], wk, wv, wza[:, perm])]
    wo = w_out[0]
    wo_top = wo[:SSM_WIDTH].astype(BF16)
    wo_bot = wo[SSM_WIDTH:][perm].astype(BF16)

    row = lambda v, width: jnp.pad(v.reshape(1, -1), ((0, 0), (0, width - v.size)))
    nw = row(norm_w[0], D_MODEL)
    cw = conv_w[0]
    cb = row(conv_b[0], CONV_DIM)
    dtb = row(dt_bias[0], LANES)
    alog = row(a_log[0], LANES)
    dskip = jnp.repeat(d_skip[0], SSM_HEAD_DIM).reshape(1, SSM_WIDTH)
    snw = row(ssm_norm_w[0], SSM_WIDTH)
    qw = jnp.tile(q_norm_w[0], ATTN_HEADS).reshape(1, ATTN_WIDTH)
    kw = jnp.tile(k_norm_w[0], ATTN_KV_HEADS).reshape(1, KV_WIDTH)
    sink = sinks[0]
    rel_flat = rel_table.reshape(-1)

    xp = x_prompt.reshape(batch * seq, D_MODEL)
    z, xbc, dt, q, k, v, za = _inproj(xp, nw, weights)
    y_ssm, st_p = _ssd_prompt(z, xbc, dt, cw, cb, dtb, alog, dskip, snw, e_mat, batch, seq)
    bias_p = _bias_tables(rel_flat, jnp.asarray(tab["prompt_buckets"]))
    bias_p = bias_p.reshape(2, ATTN_HEADS * CHUNK, 2 * CHUNK)
    y_attn, k_p, v_p = _attn_prompt(sink, q, k, v, za, qw, kw, g_mat, e_mat, bias_p, batch, seq)
    y_p = _outproj(y_ssm, y_attn, xp, wo_top, wo_bot).reshape(batch, seq, D_MODEL)
    conv_p = xbc.reshape(batch, seq, CONV_DIM)[:, seq - (CONV_WIDTH - 1):, :]

    xs = jnp.swapaxes(x_sample, 0, 1).reshape(steps * nb, D_MODEL)
    z, xbc, dt, q, k, v, za = _inproj(xs, nw, weights)
    t3 = lambda a: a.reshape(steps, nb, a.shape[-1])
    sconv3 = jnp.swapaxes(state_conv[0], 0, 1)
    ypart, ea, xw, b3, c3, cdec, conv_s3 = _ssd_sample_vec(
        t3(z), t3(xbc), t3(dt), sconv3, cw, cb, dtb, alog, dskip, gh_mat, e_mat)
    state_in = state_ssm[0].reshape(nb, SSM_WIDTH, SSM_STATE)
    st_s, yoff = _ssd_sample_state(cdec[:, :SSM_HEADS].reshape(-1), state_in, c3, b3, xw)
    f2 = lambda a: a.reshape(steps * nb, a.shape[-1])
    y_ssm, qn, kn = _sample_finish(f2(ypart), f2(yoff), f2(ea), z, snw, q, k, qw, kw, g_mat, e_mat)
    bias_c = _bias_tables(rel_flat, jnp.asarray(tab["cache_buckets"])).reshape(ATTN_HEADS * steps, WINDOW)
    bias_n = _bias_tables(rel_flat, jnp.asarray(tab["new_buckets"])).reshape(ATTN_HEADS * steps, SUBLANES)
    ck = cache_k[0].reshape(nb, WINDOW, KV_WIDTH)
    cv = cache_v[0].reshape(nb, WINDOW, KV_WIDTH)
    y_attn3, k_s, v_s = _attn_sample(sink, t3(qn), t3(kn), t3(v), t3(za), ck, cv, bias_c, bias_n)
    y_s = _outproj(y_ssm, f2(y_attn3), xs, wo_top, wo_bot)
    y_s = jnp.swapaxes(y_s.reshape(steps, nb, D_MODEL), 0, 1)

    kv5 = lambda a: a.reshape(1, a.shape[0], WINDOW, ATTN_KV_HEADS, ATTN_HEAD_DIM)
    st5 = lambda a: a.reshape(1, a.shape[0], SSM_HEADS, SSM_HEAD_DIM, SSM_STATE)
    return (y_p, y_s, kv5(k_p), kv5(v_p), st5(st_p), conv_p[None],
            kv5(k_s), kv5(v_s), st5(st_s), jnp.swapaxes(conv_s3, 0, 1)[None])
```

```python
import functools
import math

import numpy as np
import jax
import jax.numpy as jnp
from jax import lax
from jax.experimental import pallas as pl
from jax.experimental.pallas import tpu as pltpu

F32 = jnp.float32
BF16 = jnp.bfloat16

D_MODEL = 1024
SSM_HEADS = 16
SSM_HEAD_DIM = 64
SSM_WIDTH = SSM_HEADS * SSM_HEAD_DIM
SSM_GROUPS = 2
SSM_STATE = 128
GROUP_WIDTH = SSM_WIDTH // SSM_GROUPS
BC_WIDTH = SSM_GROUPS * SSM_STATE
CONV_WIDTH = 4
CONV_DIM = SSM_WIDTH + 2 * BC_WIDTH
CHUNK = 128
ATTN_HEADS = 16
ATTN_KV_HEADS = 4
Q_PER_KV = ATTN_HEADS // ATTN_KV_HEADS
ATTN_HEAD_DIM = 64
ATTN_WIDTH = ATTN_HEADS * ATTN_HEAD_DIM
KV_WIDTH = ATTN_KV_HEADS * ATTN_HEAD_DIM
WINDOW = 128
ATTN_SCALE = ATTN_HEAD_DIM ** -0.5
REL_BUCKETS = 32
REL_MAX_DIST = 128
EPS = 1e-6
NEG = -1e30

LANES = 128
SUBLANES = 8
VMEM_LIMIT = 56 * 1024 * 1024
PROJ_ROWS = 512
SAMPLE_BATCH_BLOCK = 8


def _dot(a, b):
    return jnp.dot(a, b, preferred_element_type=F32)


def _dot_nt(a, b):
    return lax.dot_general(a, b, (((1,), (1,)), ((), ())), preferred_element_type=F32)


def _dot_tn(a, b):
    return lax.dot_general(a, b, (((0,), (0,)), ((), ())), preferred_element_type=F32)


def _split2(v):
    hi = v.astype(BF16)
    lo = (v - hi.astype(F32)).astype(BF16)
    return hi, lo


def _dot_sel(v, m):
    hi, lo = _split2(v)
    return _dot(hi, m) + _dot(lo, m)


def _dot_sel3(m, v):
    hi = v.astype(BF16)
    r1 = v - hi.astype(F32)
    mid = r1.astype(BF16)
    lo = (r1 - mid.astype(F32)).astype(BF16)
    return _dot(m, hi) + _dot(m, mid) + _dot(m, lo)


def _silu(x):
    return x / (1.0 + jnp.exp(-x))


def _softplus(x):
    return jnp.maximum(x, 0.0) + jnp.log1p(jnp.exp(-jnp.abs(x)))


def _params(*sem):
    return pltpu.CompilerParams(dimension_semantics=sem, vmem_limit_bytes=VMEM_LIMIT)


def _const_spec(shape):
    nd = len(shape)
    return pl.BlockSpec(shape, lambda *_: (0,) * nd)


def _inproj_body(x_ref, nw_ref, *refs):
    n = len(refs) // 2
    x = x_ref[...]
    ms = jnp.mean(x * x, axis=-1, keepdims=True)
    h = (x * lax.rsqrt(ms + EPS) * nw_ref[...]).astype(BF16)
    for w_ref, o_ref in zip(refs[:n], refs[n:]):
        o_ref[...] = _dot(h, w_ref[...])


def _inproj(x2d, norm_w, weights):
    rows = x2d.shape[0]
    tm = min(PROJ_ROWS, rows)
    in_specs = [pl.BlockSpec((tm, D_MODEL), lambda i: (i, 0)), _const_spec((1, D_MODEL))]
    in_specs += [pl.BlockSpec(w.shape, lambda i: (0, 0), pipeline_mode=pl.Buffered(1)) for w in weights]
    out_specs = [pl.BlockSpec((tm, w.shape[1]), lambda i: (i, 0)) for w in weights]
    out_shape = [jax.ShapeDtypeStruct((rows, w.shape[1]), F32) for w in weights]
    return pl.pallas_call(
        _inproj_body, grid=(rows // tm,), in_specs=in_specs, out_specs=out_specs, out_shape=out_shape,
        compiler_params=_params("parallel"), name="inproj")(x2d, norm_w, *weights)


def _outproj_body(ys_ref, ya_ref, x_ref, wt_ref, wb_ref, o_ref):
    acc = _dot(ys_ref[...].astype(BF16), wt_ref[...]) + _dot(ya_ref[...].astype(BF16), wb_ref[...])
    o_ref[...] = x_ref[...] + acc


def _outproj(y_ssm, y_attn, x2d, w_top, w_bot):
    rows = x2d.shape[0]
    tm = min(PROJ_ROWS, rows)
    row_spec = pl.BlockSpec((tm, D_MODEL), lambda i: (i, 0))
    w_spec = pl.BlockSpec((SSM_WIDTH, D_MODEL), lambda i: (0, 0), pipeline_mode=pl.Buffered(1))
    return pl.pallas_call(
        _outproj_body, grid=(rows // tm,), in_specs=[row_spec, row_spec, row_spec, w_spec, w_spec],
        out_specs=row_spec, out_shape=jax.ShapeDtypeStruct((rows, D_MODEL), F32),
        compiler_params=_params("parallel"), name="outproj")(y_ssm, y_attn, x2d, w_top, w_bot)


def _group_rmsnorm(gy, norm_w):
    parts = []
    for g in range(SSM_GROUPS):
        blk = gy[:, g * GROUP_WIDTH:(g + 1) * GROUP_WIDTH]
        ms = jnp.mean(blk * blk, axis=-1, keepdims=True)
        parts.append(blk * lax.rsqrt(ms + EPS))
    return jnp.concatenate(parts, axis=1) * norm_w


def _ssd_prompt_body(z_ref, xbc_ref, dt_ref, cw_ref, cb_ref, dtb_ref, alog_ref, dskip_ref, nw_ref, e_ref,
                     y_ref, st_ref, state_sc, ext_sc):
    c = pl.program_id(1)
    tail = CONV_WIDTH - 1

    @pl.when(c == 0)
    def _():
        state_sc[...] = jnp.zeros_like(state_sc)
        ext_sc[0:SUBLANES, :] = jnp.zeros((SUBLANES, CONV_DIM), F32)

    u = xbc_ref[...]
    ext_sc[SUBLANES:SUBLANES + CHUNK, :] = u
    conv = cb_ref[...]
    for tap in range(CONV_WIDTH):
        conv = conv + ext_sc[pl.ds(SUBLANES - tail + tap, CHUNK), :] * cw_ref[tap:tap + 1, :]
    ext_sc[0:SUBLANES, :] = u[CHUNK - SUBLANES:, :]
    act = _silu(conv)
    xs = act[:, :SSM_WIDTH]
    b_bf = act[:, SSM_WIDTH:SSM_WIDTH + BC_WIDTH].astype(BF16)
    c_bf = act[:, SSM_WIDTH + BC_WIDTH:].astype(BF16)
    xs_bf = xs.astype(BF16)

    dt = _softplus(dt_ref[...] + dtb_ref[...])
    a = dt * (-jnp.exp(alog_ref[...]))
    li = lax.broadcasted_iota(jnp.int32, (CHUNK, CHUNK), 0)
    si = lax.broadcasted_iota(jnp.int32, (CHUNK, CHUNK), 1)
    causal = li >= si
    a_cum = _dot_sel3(jnp.where(causal, 1.0, 0.0).astype(BF16), a)
    a_cum_t = a_cum.T
    dt_t = dt.T
    e_mat = e_ref[...]
    ea_full = _dot_sel(jnp.exp(a_cum), e_mat)
    w_full = _dot_sel(dt * jnp.exp(a_cum[CHUNK - 1:CHUNK, :] - a_cum), e_mat)

    cb = [_dot_nt(c_bf[:, g * SSM_STATE:(g + 1) * SSM_STATE], b_bf[:, g * SSM_STATE:(g + 1) * SSM_STATE])
          for g in range(SSM_GROUPS)]
    half = lax.broadcasted_iota(jnp.int32, (CHUNK, LANES), 1) < SSM_HEAD_DIM
    heads_per_group = SSM_HEADS // SSM_GROUPS
    y_parts = []
    for pair in range(SSM_HEADS // 2):
        blocks = []
        for h in (2 * pair, 2 * pair + 1):
            seg = a_cum[:, h:h + 1] - a_cum_t[h:h + 1, :]
            decay = jnp.exp(jnp.where(causal, seg, -jnp.inf))
            blocks.append((cb[h // heads_per_group] * decay * dt_t[h:h + 1, :]).astype(BF16))
        lhs = jnp.concatenate(blocks, axis=1)
        xp = xs_bf[:, pair * LANES:(pair + 1) * LANES]
        zero = jnp.zeros_like(xp)
        rhs = jnp.concatenate([jnp.where(half, xp, zero), jnp.where(half, zero, xp)], axis=0)
        y_parts.append(_dot(lhs, rhs))
    y_diag = jnp.concatenate(y_parts, axis=1)

    state = state_sc[...]
    state_bf = state.astype(BF16)
    xw_bf = (xs * w_full).astype(BF16)
    y_off, upd = [], []
    for g in range(SSM_GROUPS):
        cols = slice(g * GROUP_WIDTH, (g + 1) * GROUP_WIDTH)
        ns = slice(g * SSM_STATE, (g + 1) * SSM_STATE)
        y_off.append(_dot(c_bf[:, ns], state_bf[:, cols]))
        upd.append(_dot_tn(b_bf[:, ns], xw_bf[:, cols]))
    y = y_diag + jnp.concatenate(y_off, axis=1) * ea_full + dskip_ref[...] * xs
    new_state = state * ea_full[CHUNK - 1:CHUNK, :] + jnp.concatenate(upd, axis=1)
    state_sc[...] = new_state

    y_ref[...] = _group_rmsnorm(y * _silu(z_ref[...]), nw_ref[...])

    @pl.when(c == pl.num_programs(1) - 1)
    def _():
        st_ref[0] = new_state.T


def _ssd_prompt(z, xbc, dt, conv_w, conv_b, dtb, alog, dskip, norm_w, e_mat, batch, seq):
    nc = seq // CHUNK
    row = lambda w: pl.BlockSpec((CHUNK, w), lambda b, c: (b * nc + c, 0))
    in_specs = [row(SSM_WIDTH), row(CONV_DIM), row(LANES), _const_spec((CONV_WIDTH, CONV_DIM)),
                _const_spec((1, CONV_DIM)), _const_spec((1, LANES)), _const_spec((1, LANES)),
                _const_spec((1, SSM_WIDTH)), _const_spec((1, SSM_WIDTH)), _const_spec((LANES, SSM_WIDTH))]
    out_specs = [row(SSM_WIDTH), pl.BlockSpec((1, SSM_WIDTH, SSM_STATE), lambda b, c: (b, 0, 0))]
    out_shape = [jax.ShapeDtypeStruct((batch * seq, SSM_WIDTH), F32),
                 jax.ShapeDtypeStruct((batch, SSM_WIDTH, SSM_STATE), F32)]
    scratch = [pltpu.VMEM((SSM_STATE, SSM_WIDTH), F32), pltpu.VMEM((SUBLANES + CHUNK, CONV_DIM), F32)]
    return pl.pallas_call(
        _ssd_prompt_body, grid=(batch, nc), in_specs=in_specs, out_specs=out_specs, out_shape=out_shape,
        scratch_shapes=scratch, compiler_params=_params("arbitrary", "arbitrary"), name="ssd_prompt")(
            z, xbc, dt, conv_w, conv_b, dtb, alog, dskip, norm_w, e_mat)


def _rel_bucket_np(dist):
    max_exact = REL_BUCKETS // 2
    d_f = np.maximum(dist, 1).astype(np.float32)
    large = max_exact + (np.log(d_f / np.float32(max_exact)) / np.float32(math.log(REL_MAX_DIST / max_exact))
                         * np.float32(REL_BUCKETS - max_exact)).astype(np.int32)
    return np.where(dist < max_exact, dist, np.minimum(large, REL_BUCKETS - 1)).astype(np.int32)


def _bucket_or_masked(dist, extra_mask=None):
    ok = (dist >= 0) & (dist <= WINDOW)
    if extra_mask is not None:
        ok = ok & extra_mask
    return np.where(ok, _rel_bucket_np(np.clip(dist, 0, WINDOW)), -1).astype(np.int32)


def _bias_body(rel_ref, bucket_ref, o_ref):
    bucket = bucket_ref[0]

    def per_head(h, carry):
        acc = jnp.full(bucket.shape, NEG, F32)
        for bkt in range(REL_BUCKETS):
            acc = jnp.where(bucket == bkt, rel_ref[bkt * ATTN_HEADS + h], acc)
        o_ref[0, h] = acc
        return carry

    lax.fori_loop(0, ATTN_HEADS, per_head, 0)


def _bias_tables(rel_flat, buckets):
    nv, lq, lk = buckets.shape
    return pl.pallas_call(
        _bias_body, grid=(nv,),
        in_specs=[pl.BlockSpec(memory_space=pltpu.SMEM), pl.BlockSpec((1, lq, lk), lambda v: (v, 0, 0))],
        out_specs=pl.BlockSpec((1, ATTN_HEADS, lq, lk), lambda v: (v, 0, 0, 0)),
        out_shape=jax.ShapeDtypeStruct((nv, ATTN_HEADS, lq, lk), F32),
        compiler_params=_params("arbitrary"), name="rel_bias")(rel_flat, buckets)


def _head_rmsnorm(x, g_mat, e_mat, w):
    ms = _dot_sel(x * x, g_mat) * (1.0 / ATTN_HEAD_DIM)
    return x * _dot_sel(lax.rsqrt(ms + EPS), e_mat) * w


def _lane_head(shape):
    return lax.broadcasted_iota(jnp.int32, shape, 1) // ATTN_HEAD_DIM


def _sink_column(sink_ref, n, rows_per_head):
    return jnp.concatenate(
        [jnp.full((rows_per_head, 1), sink_ref[n * Q_PER_KV + g], F32) for g in range(Q_PER_KV)], axis=0)


def _softmax_with_sink(s, sink):
    m = jnp.maximum(jnp.max(s, axis=-1, keepdims=True), sink)
    p = jnp.exp(s - m)
    denom = jnp.sum(p, axis=-1, keepdims=True) + jnp.exp(sink - m)
    return p / denom


def _attn_prompt_body(sink_ref, q_ref, k_ref, v_ref, z_ref, qw_ref, kw_ref, g_ref, e_ref, bias_ref,
                      y_ref, kn_ref, vn_ref, kcat_sc, vcat_sc):
    T = CHUNK

    @pl.when(pl.program_id(1) == 0)
    def _():
        kcat_sc[0:T, :] = jnp.zeros((T, KV_WIDTH), BF16)
        vcat_sc[0:T, :] = jnp.zeros((T, KV_WIDTH), BF16)

    g_mat = g_ref[...]
    e_mat = e_ref[...]
    qn = (_head_rmsnorm(q_ref[...], g_mat, e_mat, qw_ref[...]) * ATTN_SCALE).astype(BF16)
    kn = _head_rmsnorm(k_ref[...], g_mat[:KV_WIDTH], e_mat[:, :KV_WIDTH], kw_ref[...])
    v = v_ref[...]
    kn_ref[0] = kn
    vn_ref[0] = v
    kcat_sc[T:2 * T, :] = kn.astype(BF16)
    vcat_sc[T:2 * T, :] = v.astype(BF16)
    kcat = kcat_sc[...]
    vcat = vcat_sc[...]

    lane_head = _lane_head((2 * T, KV_WIDTH))
    zero = jnp.zeros_like(kcat)
    k_heads = [jnp.where(lane_head == n, kcat, zero) for n in range(ATTN_KV_HEADS)]
    v_stack = jnp.concatenate([jnp.where(lane_head == n, vcat, zero) for n in range(ATTN_KV_HEADS)], axis=0)
    outs = []
    for g in range(Q_PER_KV):
        qg = qn[:, g * KV_WIDTH:(g + 1) * KV_WIDTH]
        probs = []
        for n in range(ATTN_KV_HEADS):
            h = n * Q_PER_KV + g
            s = _dot_nt(qg, k_heads[n]) + bias_ref[0, h * T:(h + 1) * T, :]
            probs.append(_softmax_with_sink(s, sink_ref[h]).astype(BF16))
        outs.append(_dot(jnp.concatenate(probs, axis=1), v_stack))
    y_ref[...] = jnp.concatenate(outs, axis=1) * _silu(z_ref[...])
    kcat_sc[0:T, :] = kcat_sc[T:2 * T, :]
    vcat_sc[0:T, :] = vcat_sc[T:2 * T, :]


def _attn_prompt(sinks, q, k, v, z, qw, kw, g_mat, e_mat, bias, batch, seq):
    nb = seq // CHUNK
    row = lambda w: pl.BlockSpec((CHUNK, w), lambda b, i: (b * nb + i, 0))
    bias_rows = ATTN_HEADS * CHUNK
    in_specs = [pl.BlockSpec(memory_space=pltpu.SMEM), row(ATTN_WIDTH), row(KV_WIDTH), row(KV_WIDTH),
                row(ATTN_WIDTH), _const_spec((1, ATTN_WIDTH)), _const_spec((1, KV_WIDTH)),
                _const_spec((ATTN_WIDTH, LANES)), _const_spec((LANES, ATTN_WIDTH)),
                pl.BlockSpec((1, bias_rows, 2 * CHUNK), lambda b, i: (jnp.minimum(i, 1), 0, 0))]
    kv_out = pl.BlockSpec((1, CHUNK, KV_WIDTH), lambda b, i: (b, 0, 0))
    out_specs = [row(ATTN_WIDTH), kv_out, kv_out]
    out_shape = [jax.ShapeDtypeStruct((batch * seq, ATTN_WIDTH), F32),
                 jax.ShapeDtypeStruct((batch, CHUNK, KV_WIDTH), F32),
                 jax.ShapeDtypeStruct((batch, CHUNK, KV_WIDTH), F32)]
    scratch = [pltpu.VMEM((2 * CHUNK, KV_WIDTH), BF16), pltpu.VMEM((2 * CHUNK, KV_WIDTH), BF16)]
    return pl.pallas_call(
        _attn_prompt_body, grid=(batch, nb), in_specs=in_specs, out_specs=out_specs, out_shape=out_shape,
        scratch_shapes=scratch, compiler_params=_params("arbitrary", "arbitrary"), name="attn_prompt")(
            sinks, q, k, v, z, qw, kw, g_mat, e_mat, bias)


def _ssd_sample_vec_body(z_ref, xbc_ref, dt_ref, sconv_ref, cw_ref, cb_ref, dtb_ref, alog_ref, dskip_ref,
                         gh_ref, e_ref, ypart_ref, ea_ref, xw_ref, b_ref, c_ref, cdec_ref, convnew_ref):
    steps = xbc_ref.shape[0]
    tail = CONV_WIDTH - 1
    full = [sconv_ref[j] for j in range(tail)] + [xbc_ref[l] for l in range(steps)]
    for j in range(tail):
        convnew_ref[j] = full[steps + j]
    gh = gh_ref[...]
    e_mat = e_ref[...]
    a_neg = -jnp.exp(alog_ref[...])
    xs, bm, cm, dts, acum = [], [], [], [], []
    run = None
    for l in range(steps):
        conv = cb_ref[...]
        for tap in range(CONV_WIDTH):
            conv = conv + full[l + tap] * cw_ref[tap:tap + 1, :]
        act = _silu(conv)
        xs.append(act[:, :SSM_WIDTH])
        bm.append(act[:, SSM_WIDTH:SSM_WIDTH + BC_WIDTH])
        cm.append(act[:, SSM_WIDTH + BC_WIDTH:])
        d = _softplus(dt_ref[l] + dtb_ref[...])
        dts.append(d)
        run = d * a_neg if run is None else run + d * a_neg
        acum.append(run)
        b_ref[l] = bm[l]
        c_ref[l] = cm[l]
    for l in range(steps):
        y = dskip_ref[...] * xs[l]
        for s in range(l + 1):
            cb_h = _dot_sel(cm[l] * bm[s], gh)
            coef = cb_h * jnp.exp(acum[l] - acum[s]) * dts[s]
            y = y + _dot_sel(coef, e_mat) * xs[s]
        ypart_ref[l] = y
        ea_ref[l] = _dot_sel(jnp.exp(acum[l]), e_mat)
        xw_ref[l] = xs[l] * _dot_sel(dts[l] * jnp.exp(acum[steps - 1] - acum[l]), e_mat)
    cdec_ref[...] = jnp.exp(acum[steps - 1])


def _ssd_sample_vec(z3, xbc3, dt3, sconv3, conv_w, conv_b, dtb, alog, dskip, gh_mat, e_mat):
    steps, nb = z3.shape[0], z3.shape[1]
    f = lambda *s: jax.ShapeDtypeStruct(s, F32)
    out_shape = [f(steps, nb, SSM_WIDTH), f(steps, nb, SSM_WIDTH), f(steps, nb, SSM_WIDTH),
                 f(steps, nb, BC_WIDTH), f(steps, nb, BC_WIDTH), f(nb, LANES), f(CONV_WIDTH - 1, nb, CONV_DIM)]
    return pl.pallas_call(_ssd_sample_vec_body, out_shape=out_shape,
                          compiler_params=pltpu.CompilerParams(vmem_limit_bytes=VMEM_LIMIT),
                          name="ssd_sample_vec")(
        z3, xbc3, dt3, sconv3, conv_w, conv_b, dtb, alog, dskip, gh_mat, e_mat)


def _ssd_sample_state_body(cdec_ref, st_ref, c_ref, b_ref, xw_ref, new_ref, yoff_ref):
    i = pl.program_id(0)
    heads_per_group = SSM_HEADS // SSM_GROUPS
    for j in range(SAMPLE_BATCH_BLOCK):
        st = st_ref[j]
        cb_bf = c_ref[:, j, :].astype(BF16)
        bb_bf = b_ref[:, j, :].astype(BF16)
        xw_bf = xw_ref[:, j, :].astype(BF16)
        y_parts = []
        for g in range(SSM_GROUPS):
            rows = slice(g * GROUP_WIDTH, (g + 1) * GROUP_WIDTH)
            ns = slice(g * SSM_STATE, (g + 1) * SSM_STATE)
            y_parts.append(_dot_nt(cb_bf[:, ns], st[rows].astype(BF16)))
            upd = _dot_tn(xw_bf[:, rows], bb_bf[:, ns])
            for hh in range(heads_per_group):
                h = g * heads_per_group + hh
                r = slice(h * SSM_HEAD_DIM, (h + 1) * SSM_HEAD_DIM)
                dec = cdec_ref[(i * SAMPLE_BATCH_BLOCK + j) * SSM_HEADS + h]
                new_ref[j, r, :] = st[r] * dec + upd[hh * SSM_HEAD_DIM:(hh + 1) * SSM_HEAD_DIM]
        yoff_ref[:, j, :] = jnp.concatenate(y_parts, axis=1)


def _ssd_sample_state(cdec_flat, state, c3, b3, xw3):
    steps, nb = c3.shape[0], c3.shape[1]
    bb = SAMPLE_BATCH_BLOCK
    tok = lambda w: pl.BlockSpec((steps, bb, w), lambda i: (0, i, 0))
    st_spec = pl.BlockSpec((bb, SSM_WIDTH, SSM_STATE), lambda i: (i, 0, 0))
    return pl.pallas_call(
        _ssd_sample_state_body, grid=(nb // bb,),
        in_specs=[pl.BlockSpec(memory_space=pltpu.SMEM), st_spec, tok(BC_WIDTH), tok(BC_WIDTH), tok(SSM_WIDTH)],
        out_specs=[st_spec, tok(SSM_WIDTH)],
        out_shape=[jax.ShapeDtypeStruct(state.shape, F32), jax.ShapeDtypeStruct((steps, nb, SSM_WIDTH), F32)],
        compiler_params=_params("parallel"), name="ssd_sample_state")(cdec_flat, state, c3, b3, xw3)


def _sample_finish_body(ypart_ref, yoff_ref, ea_ref, z_ref, nw_ref, q_ref, k_ref, qw_ref, kw_ref, g_ref, e_ref,
                        yssm_ref, qn_ref, kn_ref):
    y = ypart_ref[...] + yoff_ref[...] * ea_ref[...]
    yssm_ref[...] = _group_rmsnorm(y * _silu(z_ref[...]), nw_ref[...])
    g_mat = g_ref[...]
    e_mat = e_ref[...]
    qn_ref[...] = _head_rmsnorm(q_ref[...], g_mat, e_mat, qw_ref[...]) * ATTN_SCALE
    kn_ref[...] = _head_rmsnorm(k_ref[...], g_mat[:KV_WIDTH], e_mat[:, :KV_WIDTH], kw_ref[...])


def _sample_finish(ypart, yoff, ea, z, norm_w, q, k, qw, kw, g_mat, e_mat):
    rows = z.shape[0]
    f = lambda w: jax.ShapeDtypeStruct((rows, w), F32)
    return pl.pallas_call(_sample_finish_body, out_shape=[f(SSM_WIDTH), f(ATTN_WIDTH), f(KV_WIDTH)],
                          compiler_params=pltpu.CompilerParams(vmem_limit_bytes=VMEM_LIMIT),
                          name="sample_finish")(ypart, yoff, ea, z, norm_w, q, k, qw, kw, g_mat, e_mat)


def _attn_sample_body(sink_ref, q_ref, kn_ref, vn_ref, z_ref, ck_ref, cv_ref, biasc_ref, biasn_ref,
                      y_ref, ko_ref, vo_ref):
    steps = q_ref.shape[0]
    rows_per_head = steps
    lane_head = _lane_head((Q_PER_KV * steps, KV_WIDTH))
    sink = jnp.concatenate([_sink_column(sink_ref, n, rows_per_head) for n in range(ATTN_KV_HEADS)], axis=0)
    pad = SUBLANES - steps
    for j in range(SAMPLE_BATCH_BLOCK):
        q = q_ref[:, j, :]
        qg = jnp.concatenate([q[:, g * KV_WIDTH:(g + 1) * KV_WIDTH] for g in range(Q_PER_KV)], axis=0)
        zero = jnp.zeros_like(qg)
        qx = jnp.concatenate([jnp.where(lane_head == n, qg, zero) for n in range(ATTN_KV_HEADS)], axis=0)
        qx = qx.astype(BF16)
        ck = ck_ref[j]
        cv = cv_ref[j]
        k_new = kn_ref[:, j, :]
        v_new = vn_ref[:, j, :]
        k_new8 = jnp.concatenate([k_new, jnp.zeros((pad, KV_WIDTH), F32)], axis=0)
        v_new8 = jnp.concatenate([v_new, jnp.zeros((pad, KV_WIDTH), F32)], axis=0)
        s_c = _dot_nt(qx, ck.astype(BF16)) + biasc_ref[...]
        s_n = _dot_nt(qx, k_new8.astype(BF16)) + biasn_ref[...]
        m = jnp.maximum(jnp.maximum(jnp.max(s_c, axis=-1, keepdims=True), jnp.max(s_n, axis=-1, keepdims=True)),
                        sink)
        p_c = jnp.exp(s_c - m)
        p_n = jnp.exp(s_n - m)
        denom = (jnp.sum(p_c, axis=-1, keepdims=True) + jnp.sum(p_n, axis=-1, keepdims=True)
                 + jnp.exp(sink - m))
        o = (_dot((p_c / denom).astype(BF16), cv.astype(BF16))
             + _dot((p_n / denom).astype(BF16), v_new8.astype(BF16)))
        blk = Q_PER_KV * steps
        lane_rows = _lane_head((blk, KV_WIDTH))
        og = jnp.zeros((blk, KV_WIDTH), F32)
        for n in range(ATTN_KV_HEADS):
            og = og + jnp.where(lane_rows == n, o[n * blk:(n + 1) * blk], 0.0)
        y = jnp.concatenate([og[g * steps:(g + 1) * steps] for g in range(Q_PER_KV)], axis=1)
        y_ref[:, j, :] = y * _silu(z_ref[:, j, :])
        ko_ref[j, 0:WINDOW - steps, :] = ck[steps:, :]
        ko_ref[j, WINDOW - steps:WINDOW, :] = k_new
        vo_ref[j, 0:WINDOW - steps, :] = cv[steps:, :]
        vo_ref[j, WINDOW - steps:WINDOW, :] = v_new


def _attn_sample(sinks, q3, kn3, vn3, z3, cache_k, cache_v, bias_c, bias_n):
    steps, nb = q3.shape[0], q3.shape[1]
    bb = SAMPLE_BATCH_BLOCK
    tok = lambda w: pl.BlockSpec((steps, bb, w), lambda i: (0, i, 0))
    cache_spec = pl.BlockSpec((bb, WINDOW, KV_WIDTH), lambda i: (i, 0, 0))
    return pl.pallas_call(
        _attn_sample_body, grid=(nb // bb,),
        in_specs=[pl.BlockSpec(memory_space=pltpu.SMEM), tok(ATTN_WIDTH), tok(KV_WIDTH), tok(KV_WIDTH),
                  tok(ATTN_WIDTH), cache_spec, cache_spec, _const_spec(bias_c.shape), _const_spec(bias_n.shape)],
        out_specs=[tok(ATTN_WIDTH), cache_spec, cache_spec],
        out_shape=[jax.ShapeDtypeStruct((steps, nb, ATTN_WIDTH), F32),
                   jax.ShapeDtypeStruct(cache_k.shape, F32), jax.ShapeDtypeStruct(cache_v.shape, F32)],
        compiler_params=_params("parallel"), name="attn_sample")(
            sinks, q3, kn3, vn3, z3, cache_k, cache_v, bias_c, bias_n)


def _static_tables(steps):
    lanes = np.arange(ATTN_WIDTH)
    g_mat = np.zeros((ATTN_WIDTH, LANES), np.float32)
    g_mat[lanes, lanes // ATTN_HEAD_DIM] = 1.0
    e_mat = g_mat.T.copy()
    bc = np.arange(BC_WIDTH)
    gh_mat = np.zeros((BC_WIDTH, LANES), np.float32)
    for h in range(SSM_HEADS):
        gh_mat[bc // SSM_STATE == h // (SSM_HEADS // SSM_GROUPS), h] = 1.0
    g_i, n_i, d_i = np.meshgrid(np.arange(Q_PER_KV), np.arange(ATTN_KV_HEADS), np.arange(ATTN_HEAD_DIM),
                                indexing="ij")
    perm = ((n_i * Q_PER_KV + g_i) * ATTN_HEAD_DIM + d_i).reshape(-1)
    T = CHUNK
    dist = np.arange(T)[:, None] - (np.arange(2 * T) - T)[None, :]
    first = np.broadcast_to((np.arange(2 * T) >= T)[None, :], dist.shape)
    prompt_buckets = np.stack([_bucket_or_masked(dist, first), _bucket_or_masked(dist)])
    dist_c = (np.arange(steps) + WINDOW)[:, None] - np.arange(WINDOW)[None, :]
    dist_n = np.arange(steps)[:, None] - np.arange(SUBLANES)[None, :]
    real = np.broadcast_to((np.arange(SUBLANES) < steps)[None, :], dist_n.shape)
    return dict(g=g_mat, e=e_mat, gh=gh_mat, perm=perm, prompt_buckets=prompt_buckets,
                cache_buckets=_bucket_or_masked(dist_c)[None], new_buckets=_bucket_or_masked(dist_n, real)[None])


def kernel(x_prompt, x_sample, cache_k, cache_v, state_ssm, state_conv, norm_w, w_in, conv_w, conv_b, dt_bias,
           a_log, d_skip, ssm_norm_w, q_norm_w, k_norm_w, sinks, rel_table, w_out):
    assert w_in.shape[0] == 1, "single-layer kernel"
    batch, seq, _ = x_prompt.shape
    nb, steps, _ = x_sample.shape
    tab = _static_tables(steps)
    perm = tab["perm"]
    g_mat = jnp.asarray(tab["g"], BF16)
    e_mat = jnp.asarray(tab["e"], BF16)
    gh_mat = jnp.asarray(tab["gh"], BF16)

    w = w_in[0]
    edges = np.cumsum([0, SSM_WIDTH, CONV_DIM, SSM_HEADS, ATTN_WIDTH, KV_WIDTH, KV_WIDTH, ATTN_WIDTH])
    wz, wxbc, wdt, wq, wk, wv, wza = [w[:, a:b] for a, b in zip(edges[:-1], edges[1:])]
    wdt = jnp.pad(wdt, ((0, 0), (0, LANES - SSM_HEADS)))
    weights = [m.astype(BF16) for m in (wz, wxbc, wdt, wq[:, perm], wk, wv, wza[:, perm])]
    wo = w_out[0]
    wo_top = wo[:SSM_WIDTH].astype(BF16)
    wo_bot = wo[SSM_WIDTH:][perm].astype(BF16)

    row = lambda v, width: jnp.pad(v.reshape(1, -1), ((0, 0), (0, width - v.size)))
    nw = row(norm_w[0], D_MODEL)
    cw = conv_w[0]
    cb = row(conv_b[0], CONV_DIM)
    dtb = row(dt_bias[0], LANES)
    alog = row(a_log[0], LANES)
    dskip = jnp.repeat(d_skip[0], SSM_HEAD_DIM).reshape(1, SSM_WIDTH)
    snw = row(ssm_norm_w[0], SSM_WIDTH)
    qw = jnp.tile(q_norm_w[0], ATTN_HEADS).reshape(1, ATTN_WIDTH)
    kw = jnp.tile(k_norm_w[0], ATTN_KV_HEADS).reshape(1, KV_WIDTH)
    sink = sinks[0]
    rel_flat = rel_table.reshape(-1)

    xp = x_prompt.reshape(batch * seq, D_MODEL)
    z, xbc, dt, q, k, v, za = _inproj(xp, nw, weights)
    y_ssm, st_p = _ssd_prompt(z, xbc, dt, cw, cb, dtb, alog, dskip, snw, e_mat, batch, seq)
    bias_p = _bias_tables(rel_flat, jnp.asarray(tab["prompt_buckets"]))
    bias_p = bias_p.reshape(2, ATTN_HEADS * CHUNK, 2 * CHUNK)
    y_attn, k_p, v_p = _attn_prompt(sink, q, k, v, za, qw, kw, g_mat, e_mat, bias_p, batch, seq)
    y_p = _outproj(y_ssm, y_attn, xp, wo_top, wo_bot).reshape(batch, seq, D_MODEL)
    conv_p = xbc.reshape(batch, seq, CONV_DIM)[:, seq - (CONV_WIDTH - 1):, :]

    xs = jnp.swapaxes(x_sample, 0, 1).reshape(steps * nb, D_MODEL)
    z, xbc, dt, q, k, v, za = _inproj(xs, nw, weights)
    t3 = lambda a: a.reshape(steps, nb, a.shape[-1])
    sconv3 = jnp.swapaxes(state_conv[0], 0, 1)
    ypart, ea, xw, b3, c3, cdec, conv_s3 = _ssd_sample_vec(
        t3(z), t3(xbc), t3(dt), sconv3, cw, cb, dtb, alog, dskip, gh_mat, e_mat)
    state_in = state_ssm[0].reshape(nb, SSM_WIDTH, SSM_STATE)
    st_s, yoff = _ssd_sample_state(cdec[:, :SSM_HEADS].reshape(-1), state_in, c3, b3, xw)
    f2 = lambda a: a.reshape(steps * nb, a.shape[-1])
    y_ssm, qn, kn = _sample_finish(f2(ypart), f2(yoff), f2(ea), z, snw, q, k, qw, kw, g_mat, e_mat)
    bias_c = _bias_tables(rel_flat, jnp.asarray(tab["cache_buckets"])).reshape(ATTN_HEADS * steps, WINDOW)
    bias_n = _bias_tables(rel_flat, jnp.asarray(tab["new_buckets"])).reshape(ATTN_HEADS * steps, SUBLANES)
    ck = cache_k[0].reshape(nb, WINDOW, KV_WIDTH)
    cv = cache_v[0].reshape(nb, WINDOW, KV_WIDTH)
    y_attn3, k_s, v_s = _attn_sample(sink, t3(qn), t3(kn), t3(v), t3(za), ck, cv, bias_c, bias_n)
    y_s = _outproj(y_ssm, f2(y_attn3), xs, wo_top, wo_bot)
    y_s = jnp.swapaxes(y_s.reshape(steps, nb, D_MODEL), 0, 1)

    kv5 = lambda a: a.reshape(1, a.shape[0], WINDOW, ATTN_KV_HEADS, ATTN_HEAD_DIM)
    st5 = lambda a: a.reshape(1, a.shape[0], SSM_HEADS, SSM_HEAD_DIM, SSM_STATE)
    return (y_p, y_s, kv5(k_p), kv5(v_p), st5(st_p), conv_p[None],
            kv5(k_s), kv5(v_s), st5(st_s), jnp.swapaxes(conv_s3, 0, 1)[None])
```

```python
import functools
import math

import numpy as np
import jax
import jax.numpy as jnp
from jax import lax
from jax.experimental import pallas as pl
from jax.experimental.pallas import tpu as pltpu

F32 = jnp.float32
BF16 = jnp.bfloat16

D_MODEL = 1024
SSM_HEADS = 16
SSM_HEAD_DIM = 64
SSM_WIDTH = SSM_HEADS * SSM_HEAD_DIM
SSM_GROUPS = 2
SSM_STATE = 128
GROUP_WIDTH = SSM_WIDTH // SSM_GROUPS
BC_WIDTH = SSM_GROUPS * SSM_STATE
CONV_WIDTH = 4
CONV_DIM = SSM_WIDTH + 2 * BC_WIDTH
CHUNK = 128
ATTN_HEADS = 16
ATTN_KV_HEADS = 4
Q_PER_KV = ATTN_HEADS // ATTN_KV_HEADS
ATTN_HEAD_DIM = 64
ATTN_WIDTH = ATTN_HEADS * ATTN_HEAD_DIM
KV_WIDTH = ATTN_KV_HEADS * ATTN_HEAD_DIM
WINDOW = 128
ATTN_SCALE = ATTN_HEAD_DIM ** -0.5
REL_BUCKETS = 32
REL_MAX_DIST = 128
EPS = 1e-6
NEG = -1e30

LANES = 128
SUBLANES = 8
VMEM_LIMIT = 56 * 1024 * 1024
PROJ_ROWS = 512
SAMPLE_BATCH_BLOCK = 8


def _dot(a, b):
    return jnp.dot(a, b, preferred_element_type=F32)


def _dot_nt(a, b):
    return lax.dot_general(a, b, (((1,), (1,)), ((), ())), preferred_element_type=F32)


def _dot_tn(a, b):
    return lax.dot_general(a, b, (((0,), (0,)), ((), ())), preferred_element_type=F32)


def _split2(v):
    hi = v.astype(BF16)
    lo = (v - hi.astype(F32)).astype(BF16)
    return hi, lo


def _dot_sel(v, m):
    hi, lo = _split2(v)
    return _dot(hi, m) + _dot(lo, m)


def _dot_sel3(m, v):
    hi = v.astype(BF16)
    r1 = v - hi.astype(F32)
    mid = r1.astype(BF16)
    lo = (r1 - mid.astype(F32)).astype(BF16)
    return _dot(m, hi) + _dot(m, mid) + _dot(m, lo)


def _silu(x):
    return x / (1.0 + jnp.exp(-x))


def _softplus(x):
    return jnp.maximum(x, 0.0) + jnp.log1p(jnp.exp(-jnp.abs(x)))


def _params(*sem):
    return pltpu.CompilerParams(dimension_semantics=sem, vmem_limit_bytes=VMEM_LIMIT)


def _const_spec(shape):
    nd = len(shape)
    return pl.BlockSpec(shape, lambda *_: (0,) * nd)


def _inproj_body(transposed, x_ref, nw_ref, *refs):
    n = len(refs) // 2
    x = x_ref[...]
    ms = jnp.mean(x * x, axis=-1, keepdims=True)
    h = (x * lax.rsqrt(ms + EPS) * nw_ref[...]).astype(BF16)
    for t, w_ref, o_ref in zip(transposed, refs[:n], refs[n:]):
        o_ref[...] = _dot_nt(w_ref[...], h) if t else _dot(h, w_ref[...])


def _inproj(x2d, norm_w, weights, transposed):
    rows = x2d.shape[0]
    tm = min(PROJ_ROWS, rows)
    in_specs = [pl.BlockSpec((tm, D_MODEL), lambda i: (i, 0)), _const_spec((1, D_MODEL))]
    in_specs += [pl.BlockSpec(w.shape, lambda i: (0, 0), pipeline_mode=pl.Buffered(1)) for w in weights]
    out_specs, out_shape = [], []
    for t, w in zip(transposed, weights):
        if t:
            out_specs.append(pl.BlockSpec((w.shape[0], tm), lambda i: (0, i)))
            out_shape.append(jax.ShapeDtypeStruct((w.shape[0], rows), F32))
        else:
            out_specs.append(pl.BlockSpec((tm, w.shape[1]), lambda i: (i, 0)))
            out_shape.append(jax.ShapeDtypeStruct((rows, w.shape[1]), F32))
    return pl.pallas_call(
        functools.partial(_inproj_body, transposed), grid=(rows // tm,), in_specs=in_specs, out_specs=out_specs,
        out_shape=out_shape, compiler_params=_params("parallel"), name="inproj")(x2d, norm_w, *weights)


def _outproj_body(attn_transposed, ys_ref, ya_ref, x_ref, wt_ref, wb_ref, o_ref):
    ya = ya_ref[...].astype(BF16)
    attn = _dot_tn(ya, wb_ref[...]) if attn_transposed else _dot(ya, wb_ref[...])
    o_ref[...] = x_ref[...] + _dot(ys_ref[...].astype(BF16), wt_ref[...]) + attn


def _outproj(y_ssm, y_attn, x2d, w_top, w_bot, attn_transposed):
    rows = x2d.shape[0]
    tm = min(PROJ_ROWS, rows)
    row_spec = pl.BlockSpec((tm, D_MODEL), lambda i: (i, 0))
    ya_spec = pl.BlockSpec((ATTN_WIDTH, tm), lambda i: (0, i)) if attn_transposed else row_spec
    w_spec = pl.BlockSpec((SSM_WIDTH, D_MODEL), lambda i: (0, 0), pipeline_mode=pl.Buffered(1))
    return pl.pallas_call(
        functools.partial(_outproj_body, attn_transposed), grid=(rows // tm,),
        in_specs=[row_spec, ya_spec, row_spec, w_spec, w_spec],
        out_specs=row_spec, out_shape=jax.ShapeDtypeStruct((rows, D_MODEL), F32),
        compiler_params=_params("parallel"), name="outproj")(y_ssm, y_attn, x2d, w_top, w_bot)


def _group_rmsnorm(gy, norm_w):
    parts = []
    for g in range(SSM_GROUPS):
        blk = gy[:, g * GROUP_WIDTH:(g + 1) * GROUP_WIDTH]
        ms = jnp.mean(blk * blk, axis=-1, keepdims=True)
        parts.append(blk * lax.rsqrt(ms + EPS))
    return jnp.concatenate(parts, axis=1) * norm_w


def _ssd_prompt_body(z_ref, xbc_ref, dt_ref, cw_ref, cb_ref, dtb_ref, alog_ref, dskip_ref, nw_ref, e_ref,
                     y_ref, st_ref, state_sc, ext_sc):
    c = pl.program_id(1)
    tail = CONV_WIDTH - 1

    @pl.when(c == 0)
    def _():
        state_sc[...] = jnp.zeros_like(state_sc)
        ext_sc[0:SUBLANES, :] = jnp.zeros((SUBLANES, CONV_DIM), F32)

    u = xbc_ref[...]
    ext_sc[SUBLANES:SUBLANES + CHUNK, :] = u
    conv = cb_ref[...]
    for tap in range(CONV_WIDTH):
        conv = conv + ext_sc[pl.ds(SUBLANES - tail + tap, CHUNK), :] * cw_ref[tap:tap + 1, :]
    ext_sc[0:SUBLANES, :] = u[CHUNK - SUBLANES:, :]
    act = _silu(conv)
    xs = act[:, :SSM_WIDTH]
    b_bf = act[:, SSM_WIDTH:SSM_WIDTH + BC_WIDTH].astype(BF16)
    c_bf = act[:, SSM_WIDTH + BC_WIDTH:].astype(BF16)
    xs_bf = xs.astype(BF16)

    dt = _softplus(dt_ref[...] + dtb_ref[...])
    a = dt * (-jnp.exp(alog_ref[...]))
    li = lax.broadcasted_iota(jnp.int32, (CHUNK, CHUNK), 0)
    si = lax.broadcasted_iota(jnp.int32, (CHUNK, CHUNK), 1)
    causal = li >= si
    a_cum = _dot_sel3(jnp.where(causal, 1.0, 0.0).astype(BF16), a)
    a_cum_t = a_cum.T
    dt_t = dt.T
    e_mat = e_ref[...]
    ea_full = _dot_sel(jnp.exp(a_cum), e_mat)
    w_full = _dot_sel(dt * jnp.exp(a_cum[CHUNK - 1:CHUNK, :] - a_cum), e_mat)

    cb = [_dot_nt(c_bf[:, g * SSM_STATE:(g + 1) * SSM_STATE], b_bf[:, g * SSM_STATE:(g + 1) * SSM_STATE])
          for g in range(SSM_GROUPS)]
    half = lax.broadcasted_iota(jnp.int32, (CHUNK, LANES), 1) < SSM_HEAD_DIM
    heads_per_group = SSM_HEADS // SSM_GROUPS
    y_parts = []
    for pair in range(SSM_HEADS // 2):
        blocks = []
        for h in (2 * pair, 2 * pair + 1):
            seg = a_cum[:, h:h + 1] - a_cum_t[h:h + 1, :]
            decay = jnp.exp(jnp.where(causal, seg, -jnp.inf))
            blocks.append((cb[h // heads_per_group] * decay * dt_t[h:h + 1, :]).astype(BF16))
        lhs = jnp.concatenate(blocks, axis=1)
        xp = xs_bf[:, pair * LANES:(pair + 1) * LANES]
        zero = jnp.zeros_like(xp)
        rhs = jnp.concatenate([jnp.where(half, xp, zero), jnp.where(half, zero, xp)], axis=0)
        y_parts.append(_dot(lhs, rhs))
    y_diag = jnp.concatenate(y_parts, axis=1)

    state = state_sc[...]
    state_bf = state.astype(BF16)
    xw_bf = (xs * w_full).astype(BF16)
    y_off, upd = [], []
    for g in range(SSM_GROUPS):
        cols = slice(g * GROUP_WIDTH, (g + 1) * GROUP_WIDTH)
        ns = slice(g * SSM_STATE, (g + 1) * SSM_STATE)
        y_off.append(_dot(c_bf[:, ns], state_bf[:, cols]))
        upd.append(_dot_tn(b_bf[:, ns], xw_bf[:, cols]))
    y = y_diag + jnp.concatenate(y_off, axis=1) * ea_full + dskip_ref[...] * xs
    new_state = state * ea_full[CHUNK - 1:CHUNK, :] + jnp.concatenate(upd, axis=1)
    state_sc[...] = new_state

    y_ref[...] = _group_rmsnorm(y * _silu(z_ref[...]), nw_ref[...])

    @pl.when(c == pl.num_programs(1) - 1)
    def _():
        st_ref[0] = new_state.T


def _ssd_prompt(z, xbc, dt, conv_w, conv_b, dtb, alog, dskip, norm_w, e_mat, batch, seq):
    nc = seq // CHUNK
    row = lambda w: pl.BlockSpec((CHUNK, w), lambda b, c: (b * nc + c, 0))
    in_specs = [row(SSM_WIDTH), row(CONV_DIM), row(LANES), _const_spec((CONV_WIDTH, CONV_DIM)),
                _const_spec((1, CONV_DIM)), _const_spec((1, LANES)), _const_spec((1, LANES)),
                _const_spec((1, SSM_WIDTH)), _const_spec((1, SSM_WIDTH)), _const_spec((LANES, SSM_WIDTH))]
    out_specs = [row(SSM_WIDTH), pl.BlockSpec((1, SSM_WIDTH, SSM_STATE), lambda b, c: (b, 0, 0))]
    out_shape = [jax.ShapeDtypeStruct((batch * seq, SSM_WIDTH), F32),
                 jax.ShapeDtypeStruct((batch, SSM_WIDTH, SSM_STATE), F32)]
    scratch = [pltpu.VMEM((SSM_STATE, SSM_WIDTH), F32), pltpu.VMEM((SUBLANES + CHUNK, CONV_DIM), F32)]
    return pl.pallas_call(
        _ssd_prompt_body, grid=(batch, nc), in_specs=in_specs, out_specs=out_specs, out_shape=out_shape,
        scratch_shapes=scratch, compiler_params=_params("arbitrary", "arbitrary"), name="ssd_prompt")(
            z, xbc, dt, conv_w, conv_b, dtb, alog, dskip, norm_w, e_mat)


def _rel_bucket_np(dist):
    max_exact = REL_BUCKETS // 2
    d_f = np.maximum(dist, 1).astype(np.float32)
    large = max_exact + (np.log(d_f / np.float32(max_exact)) / np.float32(math.log(REL_MAX_DIST / max_exact))
                         * np.float32(REL_BUCKETS - max_exact)).astype(np.int32)
    return np.where(dist < max_exact, dist, np.minimum(large, REL_BUCKETS - 1)).astype(np.int32)


def _bucket_or_masked(dist, extra_mask=None):
    ok = (dist >= 0) & (dist <= WINDOW)
    if extra_mask is not None:
        ok = ok & extra_mask
    return np.where(ok, _rel_bucket_np(np.clip(dist, 0, WINDOW)), -1).astype(np.int32)


def _bias_body(rel_ref, bucket_ref, o_ref):
    bucket = bucket_ref[0]

    def per_head(h, carry):
        acc = jnp.full(bucket.shape, NEG, F32)
        for bkt in range(REL_BUCKETS):
            acc = jnp.where(bucket == bkt, rel_ref[bkt * ATTN_HEADS + h], acc)
        o_ref[0, h] = acc
        return carry

    lax.fori_loop(0, ATTN_HEADS, per_head, 0)


def _bias_tables(rel_flat, buckets):
    nv, lq, lk = buckets.shape
    return pl.pallas_call(
        _bias_body, grid=(nv,),
        in_specs=[pl.BlockSpec(memory_space=pltpu.SMEM), pl.BlockSpec((1, lq, lk), lambda v: (v, 0, 0))],
        out_specs=pl.BlockSpec((1, ATTN_HEADS, lq, lk), lambda v: (v, 0, 0, 0)),
        out_shape=jax.ShapeDtypeStruct((nv, ATTN_HEADS, lq, lk), F32),
        compiler_params=_params("arbitrary"), name="rel_bias")(rel_flat, buckets)


def _bias_t_body(rel_ref, bucket_ref, o_ref):
    bucket = bucket_ref[0]
    lq = bucket.shape[1]

    def per_kv_head(n, carry):
        for g in range(Q_PER_KV):
            acc = jnp.full(bucket.shape, NEG, F32)
            for bkt in range(REL_BUCKETS):
                acc = jnp.where(bucket == bkt, rel_ref[bkt * ATTN_HEADS + n * Q_PER_KV + g], acc)
            o_ref[0, n, :, g * lq:(g + 1) * lq] = acc
        return carry

    lax.fori_loop(0, ATTN_KV_HEADS, per_kv_head, 0)


def _bias_tables_t(rel_flat, buckets_t):
    nv, lk, lq = buckets_t.shape
    out_dims = (ATTN_KV_HEADS, lk, Q_PER_KV * lq)
    return pl.pallas_call(
        _bias_t_body, grid=(nv,),
        in_specs=[pl.BlockSpec(memory_space=pltpu.SMEM), pl.BlockSpec((1, lk, lq), lambda v: (v, 0, 0))],
        out_specs=pl.BlockSpec((1,) + out_dims, lambda v: (v, 0, 0, 0)),
        out_shape=jax.ShapeDtypeStruct((nv,) + out_dims, F32),
        compiler_params=_params("arbitrary"), name="rel_bias_t")(rel_flat, buckets_t)


def _head_rmsnorm(x, g_mat, e_mat, w):
    ms = _dot_sel(x * x, g_mat) * (1.0 / ATTN_HEAD_DIM)
    return x * _dot_sel(lax.rsqrt(ms + EPS), e_mat) * w


def _lane_head(shape):
    return lax.broadcasted_iota(jnp.int32, shape, 1) // ATTN_HEAD_DIM


def _sink_column(sink_ref, n, rows_per_head):
    return jnp.concatenate(
        [jnp.full((rows_per_head, 1), sink_ref[n * Q_PER_KV + g], F32) for g in range(Q_PER_KV)], axis=0)


def _softmax_with_sink(s, sink):
    m = jnp.maximum(jnp.max(s, axis=-1, keepdims=True), sink)
    p = jnp.exp(s - m)
    denom = jnp.sum(p, axis=-1, keepdims=True) + jnp.exp(sink - m)
    return p / denom


def _attn_prompt_body(qt_ref, k_ref, v_ref, zt_ref, qwt_ref, kw_ref, g_ref, e_ref, bias_ref, sink_ref,
                      yt_ref, kn_ref, vn_ref, kcat_sc, vcat_t_sc):
    T = CHUNK

    @pl.when(pl.program_id(1) == 0)
    def _():
        kcat_sc[0:T, :] = jnp.zeros((T, KV_WIDTH), BF16)
        vcat_t_sc[:, 0:T] = jnp.zeros((KV_WIDTH, T), BF16)

    q3 = qt_ref[...].reshape(ATTN_HEADS, ATTN_HEAD_DIM, T)
    ms = jnp.mean(q3 * q3, axis=1, keepdims=True)
    qn = ((q3 * lax.rsqrt(ms + EPS)).reshape(ATTN_WIDTH, T) * qwt_ref[...]).astype(BF16)
    q_cols = jnp.concatenate([qn[g * KV_WIDTH:(g + 1) * KV_WIDTH] for g in range(Q_PER_KV)], axis=1)

    kn = _head_rmsnorm(k_ref[...], g_ref[...], e_ref[...], kw_ref[...])
    v = v_ref[...]
    kn_ref[0] = kn
    vn_ref[0] = v
    kcat_sc[T:2 * T, :] = kn.astype(BF16)
    vcat_t_sc[:, T:2 * T] = v.T.astype(BF16)
    kcat = kcat_sc[...]
    vcat_t = vcat_t_sc[...]

    lane_head = _lane_head((2 * T, KV_WIDTH))
    row_head = lax.broadcasted_iota(jnp.int32, (KV_WIDTH, 2 * T), 0) // ATTN_HEAD_DIM
    zero = jnp.zeros_like(kcat)
    probs, vals = [], []
    for n in range(ATTN_KV_HEADS):
        s = _dot(jnp.where(lane_head == n, kcat, zero), q_cols)
        sink = sink_ref[n]
        cols = []
        for g in range(Q_PER_KV):
            c = slice(g * T, (g + 1) * T)
            sg = s[:, c] + bias_ref[0, n, :, c]
            m = jnp.maximum(jnp.max(sg, axis=0, keepdims=True), sink[:, c])
            p = jnp.exp(sg - m)
            denom = jnp.sum(p, axis=0, keepdims=True) + jnp.exp(sink[:, c] - m)
            cols.append((p * (1.0 / denom)).astype(BF16))
        probs.append(jnp.concatenate(cols, axis=1))
        vals.append(jnp.where(row_head == n, vcat_t, zero))
    o_t = _dot(jnp.concatenate(vals, axis=1), jnp.concatenate(probs, axis=0))
    y_t = jnp.concatenate([o_t[:, g * T:(g + 1) * T] for g in range(Q_PER_KV)], axis=0)
    yt_ref[...] = y_t * _silu(zt_ref[...])
    kcat_sc[0:T, :] = kcat_sc[T:2 * T, :]
    vcat_t_sc[:, 0:T] = vcat_t_sc[:, T:2 * T]


def _attn_prompt(q_t, k, v, z_t, qw_t, kw, g_mat, e_mat, bias_t, sink_rows, batch, seq):
    nb = seq // CHUNK
    row = lambda w: pl.BlockSpec((CHUNK, w), lambda b, i: (b * nb + i, 0))
    col = pl.BlockSpec((ATTN_WIDTH, CHUNK), lambda b, i: (0, b * nb + i))
    in_specs = [col, row(KV_WIDTH), row(KV_WIDTH), col, _const_spec((ATTN_WIDTH, CHUNK)),
                _const_spec((1, KV_WIDTH)), _const_spec((KV_WIDTH, LANES)), _const_spec((LANES, KV_WIDTH)),
                pl.BlockSpec((1,) + bias_t.shape[1:], lambda b, i: (jnp.minimum(i, 1), 0, 0, 0)),
                _const_spec(sink_rows.shape)]
    kv_out = pl.BlockSpec((1, CHUNK, KV_WIDTH), lambda b, i: (b, 0, 0))
    out_specs = [col, kv_out, kv_out]
    out_shape = [jax.ShapeDtypeStruct((ATTN_WIDTH, batch * seq), F32),
                 jax.ShapeDtypeStruct((batch, CHUNK, KV_WIDTH), F32),
                 jax.ShapeDtypeStruct((batch, CHUNK, KV_WIDTH), F32)]
    scratch = [pltpu.VMEM((2 * CHUNK, KV_WIDTH), BF16), pltpu.VMEM((KV_WIDTH, 2 * CHUNK), BF16)]
    return pl.pallas_call(
        _attn_prompt_body, grid=(batch, nb), in_specs=in_specs, out_specs=out_specs, out_shape=out_shape,
        scratch_shapes=scratch, compiler_params=_params("arbitrary", "arbitrary"), name="attn_prompt")(
            q_t, k, v, z_t, qw_t, kw, g_mat, e_mat, bias_t, sink_rows)


def _ssd_sample_vec_body(z_ref, xbc_ref, dt_ref, sconv_ref, cw_ref, cb_ref, dtb_ref, alog_ref, dskip_ref,
                         gh_ref, e_ref, ypart_ref, ea_ref, xw_ref, b_ref, c_ref, cdec_ref, convnew_ref):
    steps = xbc_ref.shape[0]
    tail = CONV_WIDTH - 1
    full = [sconv_ref[j] for j in range(tail)] + [xbc_ref[l] for l in range(steps)]
    for j in range(tail):
        convnew_ref[j] = full[steps + j]
    gh = gh_ref[...]
    e_mat = e_ref[...]
    a_neg = -jnp.exp(alog_ref[...])
    xs, bm, cm, dts, acum = [], [], [], [], []
    run = None
    for l in range(steps):
        conv = cb_ref[...]
        for tap in range(CONV_WIDTH):
            conv = conv + full[l + tap] * cw_ref[tap:tap + 1, :]
        act = _silu(conv)
        xs.append(act[:, :SSM_WIDTH])
        bm.append(act[:, SSM_WIDTH:SSM_WIDTH + BC_WIDTH])
        cm.append(act[:, SSM_WIDTH + BC_WIDTH:])
        d = _softplus(dt_ref[l] + dtb_ref[...])
        dts.append(d)
        run = d * a_neg if run is None else run + d * a_neg
        acum.append(run)
        b_ref[l] = bm[l]
        c_ref[l] = cm[l]
    for l in range(steps):
        y = dskip_ref[...] * xs[l]
        for s in range(l + 1):
            cb_h = _dot_sel(cm[l] * bm[s], gh)
            coef = cb_h * jnp.exp(acum[l] - acum[s]) * dts[s]
            y = y + _dot_sel(coef, e_mat) * xs[s]
        ypart_ref[l] = y
        ea_ref[l] = _dot_sel(jnp.exp(acum[l]), e_mat)
        xw_ref[l] = xs[l] * _dot_sel(dts[l] * jnp.exp(acum[steps - 1] - acum[l]), e_mat)
    cdec_ref[...] = jnp.exp(acum[steps - 1])


def _ssd_sample_vec(z3, xbc3, dt3, sconv3, conv_w, conv_b, dtb, alog, dskip, gh_mat, e_mat):
    steps, nb = z3.shape[0], z3.shape[1]
    f = lambda *s: jax.ShapeDtypeStruct(s, F32)
    out_shape = [f(steps, nb, SSM_WIDTH), f(steps, nb, SSM_WIDTH), f(steps, nb, SSM_WIDTH),
                 f(steps, nb, BC_WIDTH), f(steps, nb, BC_WIDTH), f(nb, LANES), f(CONV_WIDTH - 1, nb, CONV_DIM)]
    return pl.pallas_call(_ssd_sample_vec_body, out_shape=out_shape,
                          compiler_params=pltpu.CompilerParams(vmem_limit_bytes=VMEM_LIMIT),
                          name="ssd_sample_vec")(
        z3, xbc3, dt3, sconv3, conv_w, conv_b, dtb, alog, dskip, gh_mat, e_mat)


def _ssd_sample_state_body(cdec_ref, st_ref, c_ref, b_ref, xw_ref, new_ref, yoff_ref):
    i = pl.program_id(0)
    heads_per_group = SSM_HEADS // SSM_GROUPS
    for j in range(SAMPLE_BATCH_BLOCK):
        st = st_ref[j]
        cb_bf = c_ref[:, j, :].astype(BF16)
        bb_bf = b_ref[:, j, :].astype(BF16)
        xw_bf = xw_ref[:, j, :].astype(BF16)
        y_parts = []
        for g in range(SSM_GROUPS):
            rows = slice(g * GROUP_WIDTH, (g + 1) * GROUP_WIDTH)
            ns = slice(g * SSM_STATE, (g + 1) * SSM_STATE)
            y_parts.append(_dot_nt(cb_bf[:, ns], st[rows].astype(BF16)))
            upd = _dot_tn(xw_bf[:, rows], bb_bf[:, ns])
            for hh in range(heads_per_group):
                h = g * heads_per_group + hh
                r = slice(h * SSM_HEAD_DIM, (h + 1) * SSM_HEAD_DIM)
                dec = cdec_ref[(i * SAMPLE_BATCH_BLOCK + j) * SSM_HEADS + h]
                new_ref[j, r, :] = st[r] * dec + upd[hh * SSM_HEAD_DIM:(hh + 1) * SSM_HEAD_DIM]
        yoff_ref[:, j, :] = jnp.concatenate(y_parts, axis=1)


def _ssd_sample_state(cdec_flat, state, c3, b3, xw3):
    steps, nb = c3.shape[0], c3.shape[1]
    bb = SAMPLE_BATCH_BLOCK
    tok = lambda w: pl.BlockSpec((steps, bb, w), lambda i: (0, i, 0))
    st_spec = pl.BlockSpec((bb, SSM_WIDTH, SSM_STATE), lambda i: (i, 0, 0))
    return pl.pallas_call(
        _ssd_sample_state_body, grid=(nb // bb,),
        in_specs=[pl.BlockSpec(memory_space=pltpu.SMEM), st_spec, tok(BC_WIDTH), tok(BC_WIDTH), tok(SSM_WIDTH)],
        out_specs=[st_spec, tok(SSM_WIDTH)],
        out_shape=[jax.ShapeDtypeStruct(state.shape, F32), jax.ShapeDtypeStruct((steps, nb, SSM_WIDTH), F32)],
        compiler_params=_params("parallel"), name="ssd_sample_state")(cdec_flat, state, c3, b3, xw3)


def _sample_finish_body(ypart_ref, yoff_ref, ea_ref, z_ref, nw_ref, q_ref, k_ref, qw_ref, kw_ref, g_ref, e_ref,
                        yssm_ref, qn_ref, kn_ref):
    y = ypart_ref[...] + yoff_ref[...] * ea_ref[...]
    yssm_ref[...] = _group_rmsnorm(y * _silu(z_ref[...]), nw_ref[...])
    g_mat = g_ref[...]
    e_mat = e_ref[...]
    qn_ref[...] = _head_rmsnorm(q_ref[...], g_mat, e_mat, qw_ref[...]) * ATTN_SCALE
    kn_ref[...] = _head_rmsnorm(k_ref[...], g_mat[:KV_WIDTH], e_mat[:, :KV_WIDTH], kw_ref[...])


def _sample_finish(ypart, yoff, ea, z, norm_w, q, k, qw, kw, g_mat, e_mat):
    rows = z.shape[0]
    f = lambda w: jax.ShapeDtypeStruct((rows, w), F32)
    return pl.pallas_call(_sample_finish_body, out_shape=[f(SSM_WIDTH), f(ATTN_WIDTH), f(KV_WIDTH)],
                          compiler_params=pltpu.CompilerParams(vmem_limit_bytes=VMEM_LIMIT),
                          name="sample_finish")(ypart, yoff, ea, z, norm_w, q, k, qw, kw, g_mat, e_mat)


def _attn_sample_body(sink_ref, q_ref, kn_ref, vn_ref, z_ref, ck_ref, cv_ref, biasc_ref, biasn_ref,
                      y_ref, ko_ref, vo_ref):
    steps = q_ref.shape[0]
    rows_per_head = steps
    lane_head = _lane_head((Q_PER_KV * steps, KV_WIDTH))
    sink = jnp.concatenate([_sink_column(sink_ref, n, rows_per_head) for n in range(ATTN_KV_HEADS)], axis=0)
    pad = SUBLANES - steps
    for j in range(SAMPLE_BATCH_BLOCK):
        q = q_ref[:, j, :]
        qg = jnp.concatenate([q[:, g * KV_WIDTH:(g + 1) * KV_WIDTH] for g in range(Q_PER_KV)], axis=0)
        zero = jnp.zeros_like(qg)
        qx = jnp.concatenate([jnp.where(lane_head == n, qg, zero) for n in range(ATTN_KV_HEADS)], axis=0)
        qx = qx.astype(BF16)
        ck = ck_ref[j]
        cv = cv_ref[j]
        k_new = kn_ref[:, j, :]
        v_new = vn_ref[:, j, :]
        k_new8 = jnp.concatenate([k_new, jnp.zeros((pad, KV_WIDTH), F32)], axis=0)
        v_new8 = jnp.concatenate([v_new, jnp.zeros((pad, KV_WIDTH), F32)], axis=0)
        s_c = _dot_nt(qx, ck.astype(BF16)) + biasc_ref[...]
        s_n = _dot_nt(qx, k_new8.astype(BF16)) + biasn_ref[...]
        m = jnp.maximum(jnp.maximum(jnp.max(s_c, axis=-1, keepdims=True), jnp.max(s_n, axis=-1, keepdims=True)),
                        sink)
        p_c = jnp.exp(s_c - m)
        p_n = jnp.exp(s_n - m)
        denom = (jnp.sum(p_c, axis=-1, keepdims=True) + jnp.sum(p_n, axis=-1, keepdims=True)
                 + jnp.exp(sink - m))
        o = (_dot((p_c / denom).astype(BF16), cv.astype(BF16))
             + _dot((p_n / denom).astype(BF16), v_new8.astype(BF16)))
        blk = Q_PER_KV * steps
        lane_rows = _lane_head((blk, KV_WIDTH))
        og = jnp.zeros((blk, KV_WIDTH), F32)
        for n in range(ATTN_KV_HEADS):
            og = og + jnp.where(lane_rows == n, o[n * blk:(n + 1) * blk], 0.0)
        y = jnp.concatenate([og[g * steps:(g + 1) * steps] for g in range(Q_PER_KV)], axis=1)
        y_ref[:, j, :] = y * _silu(z_ref[:, j, :])
        ko_ref[j, 0:WINDOW - steps, :] = ck[steps:, :]
        ko_ref[j, WINDOW - steps:WINDOW, :] = k_new
        vo_ref[j, 0:WINDOW - steps, :] = cv[steps:, :]
        vo_ref[j, WINDOW - steps:WINDOW, :] = v_new


def _attn_sample(sinks, q3, kn3, vn3, z3, cache_k, cache_v, bias_c, bias_n):
    steps, nb = q3.shape[0], q3.shape[1]
    bb = SAMPLE_BATCH_BLOCK
    tok = lambda w: pl.BlockSpec((steps, bb, w), lambda i: (0, i, 0))
    cache_spec = pl.BlockSpec((bb, WINDOW, KV_WIDTH), lambda i: (i, 0, 0))
    return pl.pallas_call(
        _attn_sample_body, grid=(nb // bb,),
        in_specs=[pl.BlockSpec(memory_space=pltpu.SMEM), tok(ATTN_WIDTH), tok(KV_WIDTH), tok(KV_WIDTH),
                  tok(ATTN_WIDTH), cache_spec, cache_spec, _const_spec(bias_c.shape), _const_spec(bias_n.shape)],
        out_specs=[tok(ATTN_WIDTH), cache_spec, cache_spec],
        out_shape=[jax.ShapeDtypeStruct((steps, nb, ATTN_WIDTH), F32),
                   jax.ShapeDtypeStruct(cache_k.shape, F32), jax.ShapeDtypeStruct(cache_v.shape, F32)],
        compiler_params=_params("parallel"), name="attn_sample")(
            sinks, q3, kn3, vn3, z3, cache_k, cache_v, bias_c, bias_n)


def _static_tables(steps):
    lanes = np.arange(ATTN_WIDTH)
    g_mat = np.zeros((ATTN_WIDTH, LANES), np.float32)
    g_mat[lanes, lanes // ATTN_HEAD_DIM] = 1.0
    e_mat = g_mat.T.copy()
    bc = np.arange(BC_WIDTH)
    gh_mat = np.zeros((BC_WIDTH, LANES), np.float32)
    for h in range(SSM_HEADS):
        gh_mat[bc // SSM_STATE == h // (SSM_HEADS // SSM_GROUPS), h] = 1.0
    g_i, n_i, d_i = np.meshgrid(np.arange(Q_PER_KV), np.arange(ATTN_KV_HEADS), np.arange(ATTN_HEAD_DIM),
                                indexing="ij")
    perm = ((n_i * Q_PER_KV + g_i) * ATTN_HEAD_DIM + d_i).reshape(-1)
    T = CHUNK
    dist = np.arange(T)[:, None] - (np.arange(2 * T) - T)[None, :]
    first = np.broadcast_to((np.arange(2 * T) >= T)[None, :], dist.shape)
    prompt_buckets = np.stack([_bucket_or_masked(dist, first), _bucket_or_masked(dist)])
    dist_c = (np.arange(steps) + WINDOW)[:, None] - np.arange(WINDOW)[None, :]
    dist_n = np.arange(steps)[:, None] - np.arange(SUBLANES)[None, :]
    real = np.broadcast_to((np.arange(SUBLANES) < steps)[None, :], dist_n.shape)
    return dict(g=g_mat, e=e_mat, gh=gh_mat, perm=perm, prompt_buckets=prompt_buckets,
                cache_buckets=_bucket_or_masked(dist_c)[None], new_buckets=_bucket_or_masked(dist_n, real)[None])


def kernel(x_prompt, x_sample, cache_k, cache_v, state_ssm, state_conv, norm_w, w_in, conv_w, conv_b, dt_bias,
           a_log, d_skip, ssm_norm_w, q_norm_w, k_norm_w, sinks, rel_table, w_out):
    assert w_in.shape[0] == 1, "single-layer kernel"
    batch, seq, _ = x_prompt.shape
    nb, steps, _ = x_sample.shape
    tab = _static_tables(steps)
    perm = tab["perm"]
    g_mat = jnp.asarray(tab["g"], BF16)
    e_mat = jnp.asarray(tab["e"], BF16)
    gh_mat = jnp.asarray(tab["gh"], BF16)

    w = w_in[0]
    edges = np.cumsum([0, SSM_WIDTH, CONV_DIM, SSM_HEADS, ATTN_WIDTH, KV_WIDTH, KV_WIDTH, ATTN_WIDTH])
    wz, wxbc, wdt, wq, wk, wv, wza = [w[:, a:b] for a, b in zip(edges[:-1], edges[1:])]
    wdt = jnp.pad(wdt, ((0, 0), (0, LANES - SSM_HEADS)))
    weights = [m.astype(BF16) for m in (wz, wxbc, wdt, wq[:, perm], wk, wv, wza[:, perm])]
    wo = w_out[0]
    wo_top = wo[:SSM_WIDTH].astype(BF16)
    wo_bot = wo[SSM_WIDTH:][perm].astype(BF16)

    row = lambda v, width: jnp.pad(v.reshape(1, -1), ((0, 0), (0, width - v.size)))
    nw = row(norm_w[0], D_MODEL)
    cw = conv_w[0]
    cb = row(conv_b[0], CONV_DIM)
    dtb = row(dt_bias[0], LANES)
    alog = row(a_log[0], LANES)
    dskip = jnp.repeat(d_skip[0], SSM_HEAD_DIM).reshape(1, SSM_WIDTH)
    snw = row(ssm_norm_w[0], SSM_WIDTH)
    qw = jnp.tile(q_norm_w[0], ATTN_HEADS).reshape(1, ATTN_WIDTH)
    kw = jnp.tile(k_norm_w[0], ATTN_KV_HEADS).reshape(1, KV_WIDTH)
    sink = sinks[0]
    rel_flat = rel_table.reshape(-1)

    xp = x_prompt.reshape(batch * seq, D_MODEL)
    natural = (False,) * len(weights)
    feature_major = (False, False, False, True, False, False, True)
    weights_p = [m.T if t else m for t, m in zip(feature_major, weights)]
    z, xbc, dt, q_t, k, v, za_t = _inproj(xp, nw, weights_p, feature_major)
    y_ssm, st_p = _ssd_prompt(z, xbc, dt, cw, cb, dtb, alog, dskip, snw, e_mat, batch, seq)
    bias_t = _bias_tables_t(rel_flat, jnp.asarray(tab["prompt_buckets"].transpose(0, 2, 1)))
    qw_t = jnp.broadcast_to((qw * ATTN_SCALE).reshape(ATTN_WIDTH, 1), (ATTN_WIDTH, CHUNK))
    sink_rows = jnp.repeat(sink.reshape(ATTN_KV_HEADS, Q_PER_KV), CHUNK, axis=1).reshape(ATTN_KV_HEADS, 1, -1)
    y_attn_t, k_p, v_p = _attn_prompt(q_t, k, v, za_t, qw_t, kw, g_mat[:KV_WIDTH], e_mat[:, :KV_WIDTH], bias_t,
                                      sink_rows, batch, seq)
    y_p = _outproj(y_ssm, y_attn_t, xp, wo_top, wo_bot, True).reshape(batch, seq, D_MODEL)
    conv_p = xbc.reshape(batch, seq, CONV_DIM)[:, seq - (CONV_WIDTH - 1):, :]

    xs = jnp.swapaxes(x_sample, 0, 1).reshape(steps * nb, D_MODEL)
    z, xbc, dt, q, k, v, za = _inproj(xs, nw, weights, natural)
    t3 = lambda a: a.reshape(steps, nb, a.shape[-1])
    sconv3 = jnp.swapaxes(state_conv[0], 0, 1)
    ypart, ea, xw, b3, c3, cdec, conv_s3 = _ssd_sample_vec(
        t3(z), t3(xbc), t3(dt), sconv3, cw, cb, dtb, alog, dskip, gh_mat, e_mat)
    state_in = state_ssm[0].reshape(nb, SSM_WIDTH, SSM_STATE)
    st_s, yoff = _ssd_sample_state(cdec[:, :SSM_HEADS].reshape(-1), state_in, c3, b3, xw)
    f2 = lambda a: a.reshape(steps * nb, a.shape[-1])
    y_ssm, qn, kn = _sample_finish(f2(ypart), f2(yoff), f2(ea), z, snw, q, k, qw, kw, g_mat, e_mat)
    bias_c = _bias_tables(rel_flat, jnp.asarray(tab["cache_buckets"])).reshape(ATTN_HEADS * steps, WINDOW)
    bias_n = _bias_tables(rel_flat, jnp.asarray(tab["new_buckets"])).reshape(ATTN_HEADS * steps, SUBLANES)
    ck = cache_k[0].reshape(nb, WINDOW, KV_WIDTH)
    cv = cache_v[0].reshape(nb, WINDOW, KV_WIDTH)
    y_attn3, k_s, v_s = _attn_sample(sink, t3(qn), t3(kn), t3(v), t3(za), ck, cv, bias_c, bias_n)
    y_s = _outproj(y_ssm, f2(y_attn3), xs, wo_top, wo_bot, False)
    y_s = jnp.swapaxes(y_s.reshape(steps, nb, D_MODEL), 0, 1)

    kv5 = lambda a: a.reshape(1, a.shape[0], WINDOW, ATTN_KV_HEADS, ATTN_HEAD_DIM)
    st5 = lambda a: a.reshape(1, a.shape[0], SSM_HEADS, SSM_HEAD_DIM, SSM_STATE)
    return (y_p, y_s, kv5(k_p), kv5(v_p), st5(st_p), conv_p[None],
            kv5(k_s), kv5(v_s), st5(st_s), jnp.swapaxes(conv_s3, 0, 1)[None])
```

```python
import functools
import math

import numpy as np
import jax
import jax.numpy as jnp
from jax import lax
from jax.experimental import pallas as pl
from jax.experimental.pallas import tpu as pltpu

F32 = jnp.float32
BF16 = jnp.bfloat16

D_MODEL = 1024
SSM_HEADS = 16
SSM_HEAD_DIM = 64
SSM_WIDTH = SSM_HEADS * SSM_HEAD_DIM
SSM_GROUPS = 2
SSM_STATE = 128
GROUP_WIDTH = SSM_WIDTH // SSM_GROUPS
BC_WIDTH = SSM_GROUPS * SSM_STATE
CONV_WIDTH = 4
CONV_DIM = SSM_WIDTH + 2 * BC_WIDTH
CHUNK = 128
ATTN_HEADS = 16
ATTN_KV_HEADS = 4
Q_PER_KV = ATTN_HEADS // ATTN_KV_HEADS
ATTN_HEAD_DIM = 64
ATTN_WIDTH = ATTN_HEADS * ATTN_HEAD_DIM
KV_WIDTH = ATTN_KV_HEADS * ATTN_HEAD_DIM
WINDOW = 128
ATTN_SCALE = ATTN_HEAD_DIM ** -0.5
REL_BUCKETS = 32
REL_MAX_DIST = 128
EPS = 1e-6
NEG = -1e30

LANES = 128
SUBLANES = 8
MXU_WIDTH = 256
VMEM_LIMIT = 56 * 1024 * 1024
PROJ_ROWS = 512
SAMPLE_BATCH_BLOCK = 8


def _dot(a, b):
    return jnp.dot(a, b, preferred_element_type=F32)


def _dot_nt(a, b):
    return lax.dot_general(a, b, (((1,), (1,)), ((), ())), preferred_element_type=F32)


def _dot_tn(a, b):
    return lax.dot_general(a, b, (((0,), (0,)), ((), ())), preferred_element_type=F32)


def _split2(v):
    hi = v.astype(BF16)
    lo = (v - hi.astype(F32)).astype(BF16)
    return hi, lo


def _dot_sel(v, m):
    hi, lo = _split2(v)
    return _dot(hi, m) + _dot(lo, m)


def _dot_sel3(m, v):
    hi = v.astype(BF16)
    r1 = v - hi.astype(F32)
    mid = r1.astype(BF16)
    lo = (r1 - mid.astype(F32)).astype(BF16)
    return _dot(m, hi) + _dot(m, mid) + _dot(m, lo)


def _silu(x):
    return x / (1.0 + jnp.exp(-x))


def _softplus(x):
    return jnp.maximum(x, 0.0) + jnp.log1p(jnp.exp(-jnp.abs(x)))


def _params(*sem):
    return pltpu.CompilerParams(dimension_semantics=sem, vmem_limit_bytes=VMEM_LIMIT)


def _const_spec(shape):
    nd = len(shape)
    return pl.BlockSpec(shape, lambda *_: (0,) * nd)


def _inproj_body(x_ref, nw_ref, *refs):
    n = len(refs) // 2
    x = x_ref[...]
    ms = jnp.mean(x * x, axis=-1, keepdims=True)
    h = (x * lax.rsqrt(ms + EPS) * nw_ref[...]).astype(BF16)
    for w_ref, o_ref in zip(refs[:n], refs[n:]):
        o_ref[...] = _dot(h, w_ref[...])


def _inproj(x2d, norm_w, weights):
    rows = x2d.shape[0]
    tm = min(PROJ_ROWS, rows)
    in_specs = [pl.BlockSpec((tm, D_MODEL), lambda i: (i, 0)), _const_spec((1, D_MODEL))]
    in_specs += [pl.BlockSpec(w.shape, lambda i: (0, 0), pipeline_mode=pl.Buffered(1)) for w in weights]
    out_specs = [pl.BlockSpec((tm, w.shape[1]), lambda i: (i, 0)) for w in weights]
    out_shape = [jax.ShapeDtypeStruct((rows, w.shape[1]), F32) for w in weights]
    return pl.pallas_call(
        _inproj_body, grid=(rows // tm,), in_specs=in_specs, out_specs=out_specs, out_shape=out_shape,
        compiler_params=_params("parallel"), name="inproj")(x2d, norm_w, *weights)


def _shift_rows(u, prev_tail, k):
    rows, width = u.shape
    tiles = jnp.concatenate([prev_tail, u], axis=0).reshape(rows // SUBLANES + 1, SUBLANES, width)
    rot = pltpu.roll(tiles, k, axis=1)
    first = lax.broadcasted_iota(jnp.int32, (1, SUBLANES, width), 1) < k
    return jnp.where(first, rot[:-1], rot[1:]).reshape(rows, width)


def _inproj_prompt_body(steps_per_seq, x_ref, nw_ref, wz_ref, wxbc_ref, wdt_ref, wqt_ref, wk_ref, wv_ref,
                        wzat_ref, cw_ref, cb_ref, dtb_ref,
                        gz_ref, xs_ref, b_ref, c_ref, dt_ref, qt_ref, k_ref, v_ref, gzat_ref, tail_ref, tail_sc):
    @pl.when(pl.program_id(0) % steps_per_seq == 0)
    def _():
        tail_sc[...] = jnp.zeros_like(tail_sc)

    x = x_ref[...]
    ms = jnp.mean(x * x, axis=-1, keepdims=True)
    h = (x * lax.rsqrt(ms + EPS) * nw_ref[...]).astype(BF16)
    rows = x.shape[0]
    for j in range(CONV_DIM // MXU_WIDTH):
        cols = slice(j * MXU_WIDTH, (j + 1) * MXU_WIDTH)
        u = _dot(h, wxbc_ref[:, cols])
        prev_tail = tail_sc[:, cols]
        conv = cb_ref[:, cols] + u * cw_ref[CONV_WIDTH - 1:CONV_WIDTH, cols]
        for k in range(1, CONV_WIDTH):
            tap = CONV_WIDTH - 1 - k
            conv = conv + _shift_rows(u, prev_tail, k) * cw_ref[tap:tap + 1, cols]
        new_tail = u[rows - SUBLANES:, :]
        tail_sc[:, cols] = new_tail
        tail_ref[0, :, cols] = new_tail
        act = _silu(conv)
        if j < SSM_WIDTH // MXU_WIDTH:
            xs_ref[:, cols] = act
        elif j == SSM_WIDTH // MXU_WIDTH:
            b_ref[...] = act.astype(BF16)
        else:
            c_ref[...] = act.astype(BF16)
    for j in range(SSM_WIDTH // MXU_WIDTH):
        cols = slice(j * MXU_WIDTH, (j + 1) * MXU_WIDTH)
        gz_ref[:, cols] = _silu(_dot(h, wz_ref[:, cols]))
    for j in range(ATTN_WIDTH // MXU_WIDTH):
        feats = slice(j * MXU_WIDTH, (j + 1) * MXU_WIDTH)
        gzat_ref[feats, :] = _silu(_dot_nt(wzat_ref[feats, :], h))
    dt_ref[...] = _softplus(_dot(h, wdt_ref[...]) + dtb_ref[...])
    qt_ref[...] = _dot_nt(wqt_ref[...], h)
    k_ref[...] = _dot(h, wk_ref[...])
    v_ref[...] = _dot(h, wv_ref[...])


def _inproj_prompt(x2d, norm_w, weights, conv_w, conv_b, dtb, batch, seq):
    rows = x2d.shape[0]
    tm = PROJ_ROWS
    steps_per_seq = seq // tm
    resident = lambda a: pl.BlockSpec(a.shape, lambda i: (0, 0), pipeline_mode=pl.Buffered(1))
    rowblk = lambda w: pl.BlockSpec((tm, w), lambda i: (i, 0))
    colblk = pl.BlockSpec((ATTN_WIDTH, tm), lambda i: (0, i))
    in_specs = ([rowblk(D_MODEL), _const_spec((1, D_MODEL))] + [resident(w) for w in weights]
                + [_const_spec(conv_w.shape), _const_spec(conv_b.shape), _const_spec(dtb.shape)])
    out_specs = [rowblk(SSM_WIDTH), rowblk(SSM_WIDTH), rowblk(BC_WIDTH), rowblk(BC_WIDTH), rowblk(LANES),
                 colblk, rowblk(KV_WIDTH), rowblk(KV_WIDTH), colblk,
                 pl.BlockSpec((1, SUBLANES, CONV_DIM), lambda i: (i // steps_per_seq, 0, 0))]
    f = lambda r, c, dt=F32: jax.ShapeDtypeStruct((r, c), dt)
    out_shape = [f(rows, SSM_WIDTH), f(rows, SSM_WIDTH), f(rows, BC_WIDTH, BF16), f(rows, BC_WIDTH, BF16),
                 f(rows, LANES), f(ATTN_WIDTH, rows), f(rows, KV_WIDTH), f(rows, KV_WIDTH), f(ATTN_WIDTH, rows),
                 jax.ShapeDtypeStruct((batch, SUBLANES, CONV_DIM), F32)]
    return pl.pallas_call(
        functools.partial(_inproj_prompt_body, steps_per_seq), grid=(rows // tm,), in_specs=in_specs,
        out_specs=out_specs, out_shape=out_shape, scratch_shapes=[pltpu.VMEM((SUBLANES, CONV_DIM), F32)],
        compiler_params=_params("arbitrary"), name="inproj_prompt")(
            x2d, norm_w, *weights, conv_w, conv_b, dtb)


def _outproj_body(attn_transposed, ys_ref, ya_ref, x_ref, wt_ref, wb_ref, o_ref):
    ya = ya_ref[...].astype(BF16)
    attn = _dot_tn(ya, wb_ref[...]) if attn_transposed else _dot(ya, wb_ref[...])
    o_ref[...] = x_ref[...] + _dot(ys_ref[...].astype(BF16), wt_ref[...]) + attn


def _outproj(y_ssm, y_attn, x2d, w_top, w_bot, attn_transposed):
    rows = x2d.shape[0]
    tm = min(PROJ_ROWS, rows)
    row_spec = pl.BlockSpec((tm, D_MODEL), lambda i: (i, 0))
    ya_spec = pl.BlockSpec((ATTN_WIDTH, tm), lambda i: (0, i)) if attn_transposed else row_spec
    w_spec = pl.BlockSpec((SSM_WIDTH, D_MODEL), lambda i: (0, 0), pipeline_mode=pl.Buffered(1))
    return pl.pallas_call(
        functools.partial(_outproj_body, attn_transposed), grid=(rows // tm,),
        in_specs=[row_spec, ya_spec, row_spec, w_spec, w_spec],
        out_specs=row_spec, out_shape=jax.ShapeDtypeStruct((rows, D_MODEL), F32),
        compiler_params=_params("parallel"), name="outproj")(y_ssm, y_attn, x2d, w_top, w_bot)


def _group_rmsnorm(gy, norm_w):
    parts = []
    for g in range(SSM_GROUPS):
        blk = gy[:, g * GROUP_WIDTH:(g + 1) * GROUP_WIDTH]
        ms = jnp.mean(blk * blk, axis=-1, keepdims=True)
        parts.append(blk * lax.rsqrt(ms + EPS))
    return jnp.concatenate(parts, axis=1) * norm_w


def _ssd_prompt_body(gz_ref, xs_ref, b_ref, c_ref, dt_ref, alog_ref, dskip_ref, nw_ref, e_ref,
                     y_ref, st_ref, state_sc):
    c = pl.program_id(1)

    @pl.when(c == 0)
    def _():
        state_sc[...] = jnp.zeros_like(state_sc)

    xs = xs_ref[...]
    b_bf = b_ref[...]
    c_bf = c_ref[...]
    xs_bf = xs.astype(BF16)

    dt = dt_ref[...]
    a = dt * (-jnp.exp(alog_ref[...]))
    li = lax.broadcasted_iota(jnp.int32, (CHUNK, CHUNK), 0)
    si = lax.broadcasted_iota(jnp.int32, (CHUNK, CHUNK), 1)
    causal = li >= si
    a_cum = _dot_sel3(jnp.where(causal, 1.0, 0.0).astype(BF16), a)
    a_cum_t = a_cum.T
    dt_t = dt.T
    e_mat = e_ref[...]
    ea_full = _dot_sel(jnp.exp(a_cum), e_mat)
    w_full = _dot_sel(dt * jnp.exp(a_cum[CHUNK - 1:CHUNK, :] - a_cum), e_mat)

    cb = [_dot_nt(c_bf[:, g * SSM_STATE:(g + 1) * SSM_STATE], b_bf[:, g * SSM_STATE:(g + 1) * SSM_STATE])
          for g in range(SSM_GROUPS)]
    half = lax.broadcasted_iota(jnp.int32, (CHUNK, LANES), 1) < SSM_HEAD_DIM
    heads_per_group = SSM_HEADS // SSM_GROUPS
    y_parts = []
    for pair in range(SSM_HEADS // 2):
        blocks = []
        for h in (2 * pair, 2 * pair + 1):
            seg = a_cum[:, h:h + 1] - a_cum_t[h:h + 1, :]
            decay = jnp.exp(jnp.where(causal, seg, -jnp.inf))
            blocks.append((cb[h // heads_per_group] * decay * dt_t[h:h + 1, :]).astype(BF16))
        lhs = jnp.concatenate(blocks, axis=1)
        xp = xs_bf[:, pair * LANES:(pair + 1) * LANES]
        zero = jnp.zeros_like(xp)
        rhs = jnp.concatenate([jnp.where(half, xp, zero), jnp.where(half, zero, xp)], axis=0)
        y_parts.append(_dot(lhs, rhs))
    y_diag = jnp.concatenate(y_parts, axis=1)

    state = state_sc[...]
    state_bf = state.astype(BF16)
    xw_bf = (xs * w_full).astype(BF16)
    y_off, upd = [], []
    for g in range(SSM_GROUPS):
        cols = slice(g * GROUP_WIDTH, (g + 1) * GROUP_WIDTH)
        ns = slice(g * SSM_STATE, (g + 1) * SSM_STATE)
        y_off.append(_dot(c_bf[:, ns], state_bf[:, cols]))
        upd.append(_dot_tn(b_bf[:, ns], xw_bf[:, cols]))
    y = y_diag + jnp.concatenate(y_off, axis=1) * ea_full + dskip_ref[...] * xs
    new_state = state * ea_full[CHUNK - 1:CHUNK, :] + jnp.concatenate(upd, axis=1)
    state_sc[...] = new_state

    y_ref[...] = _group_rmsnorm(y * gz_ref[...], nw_ref[...])

    @pl.when(c == pl.num_programs(1) - 1)
    def _():
        st_ref[0] = new_state.T


def _ssd_prompt(gz, xs, b, c, dt, alog, dskip, norm_w, e_mat, batch, seq):
    nc = seq // CHUNK
    row = lambda w: pl.BlockSpec((CHUNK, w), lambda b, c: (b * nc + c, 0))
    in_specs = [row(SSM_WIDTH), row(SSM_WIDTH), row(BC_WIDTH), row(BC_WIDTH), row(LANES),
                _const_spec((1, LANES)), _const_spec((1, SSM_WIDTH)), _const_spec((1, SSM_WIDTH)),
                _const_spec((LANES, SSM_WIDTH))]
    out_specs = [row(SSM_WIDTH), pl.BlockSpec((1, SSM_WIDTH, SSM_STATE), lambda b, c: (b, 0, 0))]
    out_shape = [jax.ShapeDtypeStruct((batch * seq, SSM_WIDTH), F32),
                 jax.ShapeDtypeStruct((batch, SSM_WIDTH, SSM_STATE), F32)]
    scratch = [pltpu.VMEM((SSM_STATE, SSM_WIDTH), F32)]
    return pl.pallas_call(
        _ssd_prompt_body, grid=(batch, nc), in_specs=in_specs, out_specs=out_specs, out_shape=out_shape,
        scratch_shapes=scratch, compiler_params=_params("arbitrary", "arbitrary"), name="ssd_prompt")(
            gz, xs, b, c, dt, alog, dskip, norm_w, e_mat)


def _rel_bucket_np(dist):
    max_exact = REL_BUCKETS // 2
    d_f = np.maximum(dist, 1).astype(np.float32)
    large = max_exact + (np.log(d_f / np.float32(max_exact)) / np.float32(math.log(REL_MAX_DIST / max_exact))
                         * np.float32(REL_BUCKETS - max_exact)).astype(np.int32)
    return np.where(dist < max_exact, dist, np.minimum(large, REL_BUCKETS - 1)).astype(np.int32)


def _bucket_or_masked(dist, extra_mask=None):
    ok = (dist >= 0) & (dist <= WINDOW)
    if extra_mask is not None:
        ok = ok & extra_mask
    return np.where(ok, _rel_bucket_np(np.clip(dist, 0, WINDOW)), -1).astype(np.int32)


def _bias_body(rel_ref, bucket_ref, o_ref):
    bucket = bucket_ref[0]

    def per_head(h, carry):
        acc = jnp.full(bucket.shape, NEG, F32)
        for bkt in range(REL_BUCKETS):
            acc = jnp.where(bucket == bkt, rel_ref[bkt * ATTN_HEADS + h], acc)
        o_ref[0, h] = acc
        return carry

    lax.fori_loop(0, ATTN_HEADS, per_head, 0)


def _bias_tables(rel_flat, buckets):
    nv, lq, lk = buckets.shape
    return pl.pallas_call(
        _bias_body, grid=(nv,),
        in_specs=[pl.BlockSpec(memory_space=pltpu.SMEM), pl.BlockSpec((1, lq, lk), lambda v: (v, 0, 0))],
        out_specs=pl.BlockSpec((1, ATTN_HEADS, lq, lk), lambda v: (v, 0, 0, 0)),
        out_shape=jax.ShapeDtypeStruct((nv, ATTN_HEADS, lq, lk), F32),
        compiler_params=_params("arbitrary"), name="rel_bias")(rel_flat, buckets)


def _bias_t_body(rel_ref, bucket_ref, o_ref):
    bucket = bucket_ref[0]
    lq = bucket.shape[1]

    def per_kv_head(n, carry):
        for g in range(Q_PER_KV):
            acc = jnp.full(bucket.shape, NEG, F32)
            for bkt in range(REL_BUCKETS):
                acc = jnp.where(bucket == bkt, rel_ref[bkt * ATTN_HEADS + n * Q_PER_KV + g], acc)
            o_ref[0, n, :, g * lq:(g + 1) * lq] = acc
        return carry

    lax.fori_loop(0, ATTN_KV_HEADS, per_kv_head, 0)


def _bias_tables_t(rel_flat, buckets_t):
    nv, lk, lq = buckets_t.shape
    out_dims = (ATTN_KV_HEADS, lk, Q_PER_KV * lq)
    return pl.pallas_call(
        _bias_t_body, grid=(nv,),
        in_specs=[pl.BlockSpec(memory_space=pltpu.SMEM), pl.BlockSpec((1, lk, lq), lambda v: (v, 0, 0))],
        out_specs=pl.BlockSpec((1,) + out_dims, lambda v: (v, 0, 0, 0)),
        out_shape=jax.ShapeDtypeStruct((nv,) + out_dims, F32),
        compiler_params=_params("arbitrary"), name="rel_bias_t")(rel_flat, buckets_t)


def _head_rmsnorm(x, g_mat, e_mat, w):
    ms = _dot_sel(x * x, g_mat) * (1.0 / ATTN_HEAD_DIM)
    return x * _dot_sel(lax.rsqrt(ms + EPS), e_mat) * w


def _lane_head(shape):
    return lax.broadcasted_iota(jnp.int32, shape, 1) // ATTN_HEAD_DIM


def _sink_column(sink_ref, n, rows_per_head):
    return jnp.concatenate(
        [jnp.full((rows_per_head, 1), sink_ref[n * Q_PER_KV + g], F32) for g in range(Q_PER_KV)], axis=0)


def _softmax_with_sink(s, sink):
    m = jnp.maximum(jnp.max(s, axis=-1, keepdims=True), sink)
    p = jnp.exp(s - m)
    denom = jnp.sum(p, axis=-1, keepdims=True) + jnp.exp(sink - m)
    return p / denom


def _attn_prompt_body(qt_ref, k_ref, v_ref, gzt_ref, qwt_ref, kw_ref, g_ref, e_ref, bias_ref, sink_ref,
                      yt_ref, kn_ref, vn_ref, kcat_sc, vcat_t_sc):
    T = CHUNK

    @pl.when(pl.program_id(1) == 0)
    def _():
        kcat_sc[0:T, :] = jnp.zeros((T, KV_WIDTH), BF16)
        vcat_t_sc[:, 0:T] = jnp.zeros((KV_WIDTH, T), BF16)

    q3 = qt_ref[...].reshape(ATTN_HEADS, ATTN_HEAD_DIM, T)
    ms = jnp.mean(q3 * q3, axis=1, keepdims=True)
    qn = ((q3 * lax.rsqrt(ms + EPS)).reshape(ATTN_WIDTH, T) * qwt_ref[...]).astype(BF16)
    q_cols = jnp.concatenate([qn[g * KV_WIDTH:(g + 1) * KV_WIDTH] for g in range(Q_PER_KV)], axis=1)

    kn = _head_rmsnorm(k_ref[...], g_ref[...], e_ref[...], kw_ref[...])
    v = v_ref[...]
    kn_ref[0] = kn
    vn_ref[0] = v
    kcat_sc[T:2 * T, :] = kn.astype(BF16)
    vcat_t_sc[:, T:2 * T] = v.T.astype(BF16)
    kcat = kcat_sc[...]
    vcat_t = vcat_t_sc[...]

    lane_head = _lane_head((2 * T, KV_WIDTH))
    row_head = lax.broadcasted_iota(jnp.int32, (KV_WIDTH, 2 * T), 0) // ATTN_HEAD_DIM
    zero = jnp.zeros_like(kcat)
    probs, vals = [], []
    for n in range(ATTN_KV_HEADS):
        s = _dot(jnp.where(lane_head == n, kcat, zero), q_cols)
        sink = sink_ref[n]
        cols = []
        for g in range(Q_PER_KV):
            c = slice(g * T, (g + 1) * T)
            sg = s[:, c] + bias_ref[0, n, :, c]
            m = jnp.maximum(jnp.max(sg, axis=0, keepdims=True), sink[:, c])
            p = jnp.exp(sg - m)
            denom = jnp.sum(p, axis=0, keepdims=True) + jnp.exp(sink[:, c] - m)
            cols.append((p * (1.0 / denom)).astype(BF16))
        probs.append(jnp.concatenate(cols, axis=1))
        vals.append(jnp.where(row_head == n, vcat_t, zero))
    o_t = _dot(jnp.concatenate(vals, axis=1), jnp.concatenate(probs, axis=0))
    y_t = jnp.concatenate([o_t[:, g * T:(g + 1) * T] for g in range(Q_PER_KV)], axis=0)
    yt_ref[...] = y_t * gzt_ref[...]
    kcat_sc[0:T, :] = kcat_sc[T:2 * T, :]
    vcat_t_sc[:, 0:T] = vcat_t_sc[:, T:2 * T]


def _attn_prompt(q_t, k, v, z_t, qw_t, kw, g_mat, e_mat, bias_t, sink_rows, batch, seq):
    nb = seq // CHUNK
    row = lambda w: pl.BlockSpec((CHUNK, w), lambda b, i: (b * nb + i, 0))
    col = pl.BlockSpec((ATTN_WIDTH, CHUNK), lambda b, i: (0, b * nb + i))
    in_specs = [col, row(KV_WIDTH), row(KV_WIDTH), col, _const_spec((ATTN_WIDTH, CHUNK)),
                _const_spec((1, KV_WIDTH)), _const_spec((KV_WIDTH, LANES)), _const_spec((LANES, KV_WIDTH)),
                pl.BlockSpec((1,) + bias_t.shape[1:], lambda b, i: (jnp.minimum(i, 1), 0, 0, 0)),
                _const_spec(sink_rows.shape)]
    kv_out = pl.BlockSpec((1, CHUNK, KV_WIDTH), lambda b, i: (b, 0, 0))
    out_specs = [col, kv_out, kv_out]
    out_shape = [jax.ShapeDtypeStruct((ATTN_WIDTH, batch * seq), F32),
                 jax.ShapeDtypeStruct((batch, CHUNK, KV_WIDTH), F32),
                 jax.ShapeDtypeStruct((batch, CHUNK, KV_WIDTH), F32)]
    scratch = [pltpu.VMEM((2 * CHUNK, KV_WIDTH), BF16), pltpu.VMEM((KV_WIDTH, 2 * CHUNK), BF16)]
    return pl.pallas_call(
        _attn_prompt_body, grid=(batch, nb), in_specs=in_specs, out_specs=out_specs, out_shape=out_shape,
        scratch_shapes=scratch, compiler_params=_params("arbitrary", "arbitrary"), name="attn_prompt")(
            q_t, k, v, z_t, qw_t, kw, g_mat, e_mat, bias_t, sink_rows)


def _ssd_sample_vec_body(z_ref, xbc_ref, dt_ref, sconv_ref, cw_ref, cb_ref, dtb_ref, alog_ref, dskip_ref,
                         gh_ref, e_ref, ypart_ref, ea_ref, xw_ref, b_ref, c_ref, cdec_ref, convnew_ref):
    steps = xbc_ref.shape[0]
    tail = CONV_WIDTH - 1
    full = [sconv_ref[j] for j in range(tail)] + [xbc_ref[l] for l in range(steps)]
    for j in range(tail):
        convnew_ref[j] = full[steps + j]
    gh = gh_ref[...]
    e_mat = e_ref[...]
    a_neg = -jnp.exp(alog_ref[...])
    xs, bm, cm, dts, acum = [], [], [], [], []
    run = None
    for l in range(steps):
        conv = cb_ref[...]
        for tap in range(CONV_WIDTH):
            conv = conv + full[l + tap] * cw_ref[tap:tap + 1, :]
        act = _silu(conv)
        xs.append(act[:, :SSM_WIDTH])
        bm.append(act[:, SSM_WIDTH:SSM_WIDTH + BC_WIDTH])
        cm.append(act[:, SSM_WIDTH + BC_WIDTH:])
        d = _softplus(dt_ref[l] + dtb_ref[...])
        dts.append(d)
        run = d * a_neg if run is None else run + d * a_neg
        acum.append(run)
        b_ref[l] = bm[l]
        c_ref[l] = cm[l]
    for l in range(steps):
        y = dskip_ref[...] * xs[l]
        for s in range(l + 1):
            cb_h = _dot_sel(cm[l] * bm[s], gh)
            coef = cb_h * jnp.exp(acum[l] - acum[s]) * dts[s]
            y = y + _dot_sel(coef, e_mat) * xs[s]
        ypart_ref[l] = y
        ea_ref[l] = _dot_sel(jnp.exp(acum[l]), e_mat)
        xw_ref[l] = xs[l] * _dot_sel(dts[l] * jnp.exp(acum[steps - 1] - acum[l]), e_mat)
    cdec_ref[...] = jnp.exp(acum[steps - 1])


def _ssd_sample_vec(z3, xbc3, dt3, sconv3, conv_w, conv_b, dtb, alog, dskip, gh_mat, e_mat):
    steps, nb = z3.shape[0], z3.shape[1]
    f = lambda *s: jax.ShapeDtypeStruct(s, F32)
    out_shape = [f(steps, nb, SSM_WIDTH), f(steps, nb, SSM_WIDTH), f(steps, nb, SSM_WIDTH),
                 f(steps, nb, BC_WIDTH), f(steps, nb, BC_WIDTH), f(nb, LANES), f(CONV_WIDTH - 1, nb, CONV_DIM)]
    return pl.pallas_call(_ssd_sample_vec_body, out_shape=out_shape,
                          compiler_params=pltpu.CompilerParams(vmem_limit_bytes=VMEM_LIMIT),
                          name="ssd_sample_vec")(
        z3, xbc3, dt3, sconv3, conv_w, conv_b, dtb, alog, dskip, gh_mat, e_mat)


def _ssd_sample_state_body(cdec_ref, st_ref, c_ref, b_ref, xw_ref, new_ref, yoff_ref):
    i = pl.program_id(0)
    heads_per_group = SSM_HEADS // SSM_GROUPS
    for j in range(SAMPLE_BATCH_BLOCK):
        st = st_ref[j]
        cb_bf = c_ref[:, j, :].astype(BF16)
        bb_bf = b_ref[:, j, :].astype(BF16)
        xw_bf = xw_ref[:, j, :].astype(BF16)
        y_parts = []
        for g in range(SSM_GROUPS):
            rows = slice(g * GROUP_WIDTH, (g + 1) * GROUP_WIDTH)
            ns = slice(g * SSM_STATE, (g + 1) * SSM_STATE)
            y_parts.append(_dot_nt(cb_bf[:, ns], st[rows].astype(BF16)))
            upd = _dot_tn(xw_bf[:, rows], bb_bf[:, ns])
            for hh in range(heads_per_group):
                h = g * heads_per_group + hh
                r = slice(h * SSM_HEAD_DIM, (h + 1) * SSM_HEAD_DIM)
                dec = cdec_ref[(i * SAMPLE_BATCH_BLOCK + j) * SSM_HEADS + h]
                new_ref[j, r, :] = st[r] * dec + upd[hh * SSM_HEAD_DIM:(hh + 1) * SSM_HEAD_DIM]
        yoff_ref[:, j, :] = jnp.concatenate(y_parts, axis=1)


def _ssd_sample_state(cdec_flat, state, c3, b3, xw3):
    steps, nb = c3.shape[0], c3.shape[1]
    bb = SAMPLE_BATCH_BLOCK
    tok = lambda w: pl.BlockSpec((steps, bb, w), lambda i: (0, i, 0))
    st_spec = pl.BlockSpec((bb, SSM_WIDTH, SSM_STATE), lambda i: (i, 0, 0))
    return pl.pallas_call(
        _ssd_sample_state_body, grid=(nb // bb,),
        in_specs=[pl.BlockSpec(memory_space=pltpu.SMEM), st_spec, tok(BC_WIDTH), tok(BC_WIDTH), tok(SSM_WIDTH)],
        out_specs=[st_spec, tok(SSM_WIDTH)],
        out_shape=[jax.ShapeDtypeStruct(state.shape, F32), jax.ShapeDtypeStruct((steps, nb, SSM_WIDTH), F32)],
        compiler_params=_params("parallel"), name="ssd_sample_state")(cdec_flat, state, c3, b3, xw3)


def _sample_finish_body(ypart_ref, yoff_ref, ea_ref, z_ref, nw_ref, q_ref, k_ref, qw_ref, kw_ref, g_ref, e_ref,
                        yssm_ref, qn_ref, kn_ref):
    y = ypart_ref[...] + yoff_ref[...] * ea_ref[...]
    yssm_ref[...] = _group_rmsnorm(y * _silu(z_ref[...]), nw_ref[...])
    g_mat = g_ref[...]
    e_mat = e_ref[...]
    qn_ref[...] = _head_rmsnorm(q_ref[...], g_mat, e_mat, qw_ref[...]) * ATTN_SCALE
    kn_ref[...] = _head_rmsnorm(k_ref[...], g_mat[:KV_WIDTH], e_mat[:, :KV_WIDTH], kw_ref[...])


def _sample_finish(ypart, yoff, ea, z, norm_w, q, k, qw, kw, g_mat, e_mat):
    rows = z.shape[0]
    f = lambda w: jax.ShapeDtypeStruct((rows, w), F32)
    return pl.pallas_call(_sample_finish_body, out_shape=[f(SSM_WIDTH), f(ATTN_WIDTH), f(KV_WIDTH)],
                          compiler_params=pltpu.CompilerParams(vmem_limit_bytes=VMEM_LIMIT),
                          name="sample_finish")(ypart, yoff, ea, z, norm_w, q, k, qw, kw, g_mat, e_mat)


def _attn_sample_body(sink_ref, q_ref, kn_ref, vn_ref, z_ref, ck_ref, cv_ref, biasc_ref, biasn_ref,
                      y_ref, ko_ref, vo_ref):
    steps = q_ref.shape[0]
    rows_per_head = steps
    lane_head = _lane_head((Q_PER_KV * steps, KV_WIDTH))
    sink = jnp.concatenate([_sink_column(sink_ref, n, rows_per_head) for n in range(ATTN_KV_HEADS)], axis=0)
    pad = SUBLANES - steps
    for j in range(SAMPLE_BATCH_BLOCK):
        q = q_ref[:, j, :]
        qg = jnp.concatenate([q[:, g * KV_WIDTH:(g + 1) * KV_WIDTH] for g in range(Q_PER_KV)], axis=0)
        zero = jnp.zeros_like(qg)
        qx = jnp.concatenate([jnp.where(lane_head == n, qg, zero) for n in range(ATTN_KV_HEADS)], axis=0)
        qx = qx.astype(BF16)
        ck = ck_ref[j]
        cv = cv_ref[j]
        k_new = kn_ref[:, j, :]
        v_new = vn_ref[:, j, :]
        k_new8 = jnp.concatenate([k_new, jnp.zeros((pad, KV_WIDTH), F32)], axis=0)
        v_new8 = jnp.concatenate([v_new, jnp.zeros((pad, KV_WIDTH), F32)], axis=0)
        s_c = _dot_nt(qx, ck.astype(BF16)) + biasc_ref[...]
        s_n = _dot_nt(qx, k_new8.astype(BF16)) + biasn_ref[...]
        m = jnp.maximum(jnp.maximum(jnp.max(s_c, axis=-1, keepdims=True), jnp.max(s_n, axis=-1, keepdims=True)),
                        sink)
        p_c = jnp.exp(s_c - m)
        p_n = jnp.exp(s_n - m)
        denom = (jnp.sum(p_c, axis=-1, keepdims=True) + jnp.sum(p_n, axis=-1, keepdims=True)
                 + jnp.exp(sink - m))
        o = (_dot((p_c / denom).astype(BF16), cv.astype(BF16))
             + _dot((p_n / denom).astype(BF16), v_new8.astype(BF16)))
        blk = Q_PER_KV * steps
        lane_rows = _lane_head((blk, KV_WIDTH))
        og = jnp.zeros((blk, KV_WIDTH), F32)
        for n in range(ATTN_KV_HEADS):
            og = og + jnp.where(lane_rows == n, o[n * blk:(n + 1) * blk], 0.0)
        y = jnp.concatenate([og[g * steps:(g + 1) * steps] for g in range(Q_PER_KV)], axis=1)
        y_ref[:, j, :] = y * _silu(z_ref[:, j, :])
        ko_ref[j, 0:WINDOW - steps, :] = ck[steps:, :]
        ko_ref[j, WINDOW - steps:WINDOW, :] = k_new
        vo_ref[j, 0:WINDOW - steps, :] = cv[steps:, :]
        vo_ref[j, WINDOW - steps:WINDOW, :] = v_new


def _attn_sample(sinks, q3, kn3, vn3, z3, cache_k, cache_v, bias_c, bias_n):
    steps, nb = q3.shape[0], q3.shape[1]
    bb = SAMPLE_BATCH_BLOCK
    tok = lambda w: pl.BlockSpec((steps, bb, w), lambda i: (0, i, 0))
    cache_spec = pl.BlockSpec((bb, WINDOW, KV_WIDTH), lambda i: (i, 0, 0))
    return pl.pallas_call(
        _attn_sample_body, grid=(nb // bb,),
        in_specs=[pl.BlockSpec(memory_space=pltpu.SMEM), tok(ATTN_WIDTH), tok(KV_WIDTH), tok(KV_WIDTH),
                  tok(ATTN_WIDTH), cache_spec, cache_spec, _const_spec(bias_c.shape), _const_spec(bias_n.shape)],
        out_specs=[tok(ATTN_WIDTH), cache_spec, cache_spec],
        out_shape=[jax.ShapeDtypeStruct((steps, nb, ATTN_WIDTH), F32),
                   jax.ShapeDtypeStruct(cache_k.shape, F32), jax.ShapeDtypeStruct(cache_v.shape, F32)],
        compiler_params=_params("parallel"), name="attn_sample")(
            sinks, q3, kn3, vn3, z3, cache_k, cache_v, bias_c, bias_n)


def _static_tables(steps):
    lanes = np.arange(ATTN_WIDTH)
    g_mat = np.zeros((ATTN_WIDTH, LANES), np.float32)
    g_mat[lanes, lanes // ATTN_HEAD_DIM] = 1.0
    e_mat = g_mat.T.copy()
    bc = np.arange(BC_WIDTH)
    gh_mat = np.zeros((BC_WIDTH, LANES), np.float32)
    for h in range(SSM_HEADS):
        gh_mat[bc // SSM_STATE == h // (SSM_HEADS // SSM_GROUPS), h] = 1.0
    g_i, n_i, d_i = np.meshgrid(np.arange(Q_PER_KV), np.arange(ATTN_KV_HEADS), np.arange(ATTN_HEAD_DIM),
                                indexing="ij")
    perm = ((n_i * Q_PER_KV + g_i) * ATTN_HEAD_DIM + d_i).reshape(-1)
    T = CHUNK
    dist = np.arange(T)[:, None] - (np.arange(2 * T) - T)[None, :]
    first = np.broadcast_to((np.arange(2 * T) >= T)[None, :], dist.shape)
    prompt_buckets = np.stack([_bucket_or_masked(dist, first), _bucket_or_masked(dist)])
    dist_c = (np.arange(steps) + WINDOW)[:, None] - np.arange(WINDOW)[None, :]
    dist_n = np.arange(steps)[:, None] - np.arange(SUBLANES)[None, :]
    real = np.broadcast_to((np.arange(SUBLANES) < steps)[None, :], dist_n.shape)
    return dict(g=g_mat, e=e_mat, gh=gh_mat, perm=perm, prompt_buckets=prompt_buckets,
                cache_buckets=_bucket_or_masked(dist_c)[None], new_buckets=_bucket_or_masked(dist_n, real)[None])


def kernel(x_prompt, x_sample, cache_k, cache_v, state_ssm, state_conv, norm_w, w_in, conv_w, conv_b, dt_bias,
           a_log, d_skip, ssm_norm_w, q_norm_w, k_norm_w, sinks, rel_table, w_out):
    assert w_in.shape[0] == 1, "single-layer kernel"
    batch, seq, _ = x_prompt.shape
    nb, steps, _ = x_sample.shape
    tab = _static_tables(steps)
    perm = tab["perm"]
    g_mat = jnp.asarray(tab["g"], BF16)
    e_mat = jnp.asarray(tab["e"], BF16)
    gh_mat = jnp.asarray(tab["gh"], BF16)

    w = w_in[0]
    edges = np.cumsum([0, SSM_WIDTH, CONV_DIM, SSM_HEADS, ATTN_WIDTH, KV_WIDTH, KV_WIDTH, ATTN_WIDTH])
    wz, wxbc, wdt, wq, wk, wv, wza = [w[:, a:b] for a, b in zip(edges[:-1], edges[1:])]
    wdt = jnp.pad(wdt, ((0, 0), (0, LANES - SSM_HEADS)))
    weights = [m.astype(BF16) for m in (wz, wxbc, wdt, wq[:, perm], wk, wv, wza[:, perm])]
    wo = w_out[0]
    wo_top = wo[:SSM_WIDTH].astype(BF16)
    wo_bot = wo[SSM_WIDTH:][perm].astype(BF16)

    row = lambda v, width: jnp.pad(v.reshape(1, -1), ((0, 0), (0, width - v.size)))
    nw = row(norm_w[0], D_MODEL)
    cw = conv_w[0]
    cb = row(conv_b[0], CONV_DIM)
    dtb = row(dt_bias[0], LANES)
    alog = row(a_log[0], LANES)
    dskip = jnp.repeat(d_skip[0], SSM_HEAD_DIM).reshape(1, SSM_WIDTH)
    snw = row(ssm_norm_w[0], SSM_WIDTH)
    qw = jnp.tile(q_norm_w[0], ATTN_HEADS).reshape(1, ATTN_WIDTH)
    kw = jnp.tile(k_norm_w[0], ATTN_KV_HEADS).reshape(1, KV_WIDTH)
    sink = sinks[0]
    rel_flat = rel_table.reshape(-1)

    xp = x_prompt.reshape(batch * seq, D_MODEL)
    feature_major = (False, False, False, True, False, False, True)
    weights_p = [m.T if t else m for t, m in zip(feature_major, weights)]
    gz, xs_p, b_p, c_p, dt_p, q_t, k, v, gza_t, tail_p = _inproj_prompt(xp, nw, weights_p, cw, cb, dtb, batch, seq)
    y_ssm, st_p = _ssd_prompt(gz, xs_p, b_p, c_p, dt_p, alog, dskip, snw, e_mat, batch, seq)
    bias_t = _bias_tables_t(rel_flat, jnp.asarray(tab["prompt_buckets"].transpose(0, 2, 1)))
    qw_t = jnp.broadcast_to((qw * ATTN_SCALE).reshape(ATTN_WIDTH, 1), (ATTN_WIDTH, CHUNK))
    sink_rows = jnp.repeat(sink.reshape(ATTN_KV_HEADS, Q_PER_KV), CHUNK, axis=1).reshape(ATTN_KV_HEADS, 1, -1)
    y_attn_t, k_p, v_p = _attn_prompt(q_t, k, v, gza_t, qw_t, kw, g_mat[:KV_WIDTH], e_mat[:, :KV_WIDTH], bias_t,
                                      sink_rows, batch, seq)
    y_p = _outproj(y_ssm, y_attn_t, xp, wo_top, wo_bot, True).reshape(batch, seq, D_MODEL)
    conv_p = tail_p[:, SUBLANES - (CONV_WIDTH - 1):, :]

    xs = jnp.swapaxes(x_sample, 0, 1).reshape(steps * nb, D_MODEL)
    z, xbc, dt, q, k, v, za = _inproj(xs, nw, weights)
    t3 = lambda a: a.reshape(steps, nb, a.shape[-1])
    sconv3 = jnp.swapaxes(state_conv[0], 0, 1)
    ypart, ea, xw, b3, c3, cdec, conv_s3 = _ssd_sample_vec(
        t3(z), t3(xbc), t3(dt), sconv3, cw, cb, dtb, alog, dskip, gh_mat, e_mat)
    state_in = state_ssm[0].reshape(nb, SSM_WIDTH, SSM_STATE)
    st_s, yoff = _ssd_sample_state(cdec[:, :SSM_HEADS].reshape(-1), state_in, c3, b3, xw)
    f2 = lambda a: a.reshape(steps * nb, a.shape[-1])
    y_ssm, qn, kn = _sample_finish(f2(ypart), f2(yoff), f2(ea), z, snw, q, k, qw, kw, g_mat, e_mat)
    bias_c = _bias_tables(rel_flat, jnp.asarray(tab["cache_buckets"])).reshape(ATTN_HEADS * steps, WINDOW)
    bias_n = _bias_tables(rel_flat, jnp.asarray(tab["new_buckets"])).reshape(ATTN_HEADS * steps, SUBLANES)
    ck = cache_k[0].reshape(nb, WINDOW, KV_WIDTH)
    cv = cache_v[0].reshape(nb, WINDOW, KV_WIDTH)
    y_attn3, k_s, v_s = _attn_sample(sink, t3(qn), t3(kn), t3(v), t3(za), ck, cv, bias_c, bias_n)
    y_s = _outproj(y_ssm, f2(y_attn3), xs, wo_top, wo_bot, False)
    y_s = jnp.swapaxes(y_s.reshape(steps, nb, D_MODEL), 0, 1)

    kv5 = lambda a: a.reshape(1, a.shape[0], WINDOW, ATTN_KV_HEADS, ATTN_HEAD_DIM)
    st5 = lambda a: a.reshape(1, a.shape[0], SSM_HEADS, SSM_HEAD_DIM, SSM_STATE)
    return (y_p, y_s, kv5(k_p), kv5(v_p), st5(st_p), conv_p[None],
            kv5(k_s), kv5(v_s), st5(st_s), jnp.swapaxes(conv_s3, 0, 1)[None])
```

```python
import functools
import math

import numpy as np
import jax
import jax.numpy as jnp
from jax import lax
from jax.experimental import pallas as pl
from jax.experimental.pallas import tpu as pltpu

F32 = jnp.float32
BF16 = jnp.bfloat16

D_MODEL = 1024
SSM_HEADS = 16
SSM_HEAD_DIM = 64
SSM_WIDTH = SSM_HEADS * SSM_HEAD_DIM
SSM_GROUPS = 2
SSM_STATE = 128
GROUP_WIDTH = SSM_WIDTH // SSM_GROUPS
BC_WIDTH = SSM_GROUPS * SSM_STATE
CONV_WIDTH = 4
CONV_DIM = SSM_WIDTH + 2 * BC_WIDTH
CHUNK = 128
ATTN_HEADS = 16
ATTN_KV_HEADS = 4
Q_PER_KV = ATTN_HEADS // ATTN_KV_HEADS
ATTN_HEAD_DIM = 64
ATTN_WIDTH = ATTN_HEADS * ATTN_HEAD_DIM
KV_WIDTH = ATTN_KV_HEADS * ATTN_HEAD_DIM
WINDOW = 128
ATTN_SCALE = ATTN_HEAD_DIM ** -0.5
REL_BUCKETS = 32
REL_MAX_DIST = 128
EPS = 1e-6
NEG = -1e30

LANES = 128
SUBLANES = 8
MXU_WIDTH = 256
VMEM_LIMIT = 56 * 1024 * 1024
PROJ_ROWS = 512
CHUNKS_PER_STEP = 4
SAMPLE_BATCH_BLOCK = 8


def _dot(a, b):
    return jnp.dot(a, b, preferred_element_type=F32)


def _dot_nt(a, b):
    return lax.dot_general(a, b, (((1,), (1,)), ((), ())), preferred_element_type=F32)


def _dot_tn(a, b):
    return lax.dot_general(a, b, (((0,), (0,)), ((), ())), preferred_element_type=F32)


def _split2(v):
    hi = v.astype(BF16)
    lo = (v - hi.astype(F32)).astype(BF16)
    return hi, lo


def _dot_sel(v, m):
    hi, lo = _split2(v)
    return _dot(hi, m) + _dot(lo, m)


def _dot_sel3(m, v):
    hi = v.astype(BF16)
    r1 = v - hi.astype(F32)
    mid = r1.astype(BF16)
    lo = (r1 - mid.astype(F32)).astype(BF16)
    return _dot(m, hi) + _dot(m, mid) + _dot(m, lo)


def _silu(x):
    return x / (1.0 + jnp.exp(-x))


def _softplus(x):
    return jnp.maximum(x, 0.0) + jnp.log1p(jnp.exp(-jnp.abs(x)))


def _params(*sem):
    return pltpu.CompilerParams(dimension_semantics=sem, vmem_limit_bytes=VMEM_LIMIT)


def _const_spec(shape):
    nd = len(shape)
    return pl.BlockSpec(shape, lambda *_: (0,) * nd)


def _inproj_body(x_ref, nw_ref, *refs):
    n = len(refs) // 2
    x = x_ref[...]
    ms = jnp.mean(x * x, axis=-1, keepdims=True)
    h = (x * lax.rsqrt(ms + EPS) * nw_ref[...]).astype(BF16)
    for w_ref, o_ref in zip(refs[:n], refs[n:]):
        o_ref[...] = _dot(h, w_ref[...])


def _inproj(x2d, norm_w, weights):
    rows = x2d.shape[0]
    tm = min(PROJ_ROWS, rows)
    in_specs = [pl.BlockSpec((tm, D_MODEL), lambda i: (i, 0)), _const_spec((1, D_MODEL))]
    in_specs += [pl.BlockSpec(w.shape, lambda i: (0, 0), pipeline_mode=pl.Buffered(1)) for w in weights]
    out_specs = [pl.BlockSpec((tm, w.shape[1]), lambda i: (i, 0)) for w in weights]
    out_shape = [jax.ShapeDtypeStruct((rows, w.shape[1]), F32) for w in weights]
    return pl.pallas_call(
        _inproj_body, grid=(rows // tm,), in_specs=in_specs, out_specs=out_specs, out_shape=out_shape,
        compiler_params=_params("parallel"), name="inproj")(x2d, norm_w, *weights)


def _shift_rows(u, prev_tail, k):
    rows, width = u.shape
    tiles = jnp.concatenate([prev_tail, u], axis=0).reshape(rows // SUBLANES + 1, SUBLANES, width)
    rot = pltpu.roll(tiles, k, axis=1)
    first = lax.broadcasted_iota(jnp.int32, (1, SUBLANES, width), 1) < k
    return jnp.where(first, rot[:-1], rot[1:]).reshape(rows, width)


def _inproj_prompt_body(steps_per_seq, x_ref, nw_ref, wz_ref, wxbc_ref, wdt_ref, wqt_ref, wk_ref, wv_ref,
                        wzat_ref, cw_ref, cb_ref, dtb_ref,
                        gz_ref, xs_ref, b_ref, c_ref, dt_ref, qt_ref, k_ref, v_ref, gzat_ref, tail_ref, tail_sc):
    @pl.when(pl.program_id(0) % steps_per_seq == 0)
    def _():
        tail_sc[...] = jnp.zeros_like(tail_sc)

    x = x_ref[...]
    ms = jnp.mean(x * x, axis=-1, keepdims=True)
    h = (x * lax.rsqrt(ms + EPS) * nw_ref[...]).astype(BF16)
    rows = x.shape[0]
    for j in range(CONV_DIM // MXU_WIDTH):
        cols = slice(j * MXU_WIDTH, (j + 1) * MXU_WIDTH)
        u = _dot(h, wxbc_ref[:, cols])
        prev_tail = tail_sc[:, cols]
        conv = cb_ref[:, cols] + u * cw_ref[CONV_WIDTH - 1:CONV_WIDTH, cols]
        for k in range(1, CONV_WIDTH):
            tap = CONV_WIDTH - 1 - k
            conv = conv + _shift_rows(u, prev_tail, k) * cw_ref[tap:tap + 1, cols]
        new_tail = u[rows - SUBLANES:, :]
        tail_sc[:, cols] = new_tail
        tail_ref[0, :, cols] = new_tail
        act = _silu(conv)
        if j < SSM_WIDTH // MXU_WIDTH:
            xs_ref[:, cols] = act
        elif j == SSM_WIDTH // MXU_WIDTH:
            b_ref[...] = act.astype(BF16)
        else:
            c_ref[...] = act.astype(BF16)

    for j in range(SSM_WIDTH // MXU_WIDTH):
        cols = slice(j * MXU_WIDTH, (j + 1) * MXU_WIDTH)
        gz_ref[:, cols] = _silu(_dot(h, wz_ref[:, cols]))
    for j in range(ATTN_WIDTH // MXU_WIDTH):
        feats = slice(j * MXU_WIDTH, (j + 1) * MXU_WIDTH)
        gzat_ref[feats, :] = _silu(_dot_nt(wzat_ref[feats, :], h))
    dt_ref[...] = _softplus(_dot(h, wdt_ref[...]) + dtb_ref[...])
    qt_ref[...] = _dot_nt(wqt_ref[...], h)
    k_ref[...] = _dot(h, wk_ref[...])
    v_ref[...] = _dot(h, wv_ref[...])


def _inproj_prompt(x2d, norm_w, weights, conv_w, conv_b, dtb, batch, seq):
    rows = x2d.shape[0]
    tm = PROJ_ROWS
    steps_per_seq = seq // tm
    resident = lambda a: pl.BlockSpec(a.shape, lambda i: (0, 0), pipeline_mode=pl.Buffered(1))
    rowblk = lambda w: pl.BlockSpec((tm, w), lambda i: (i, 0))
    colblk = pl.BlockSpec((ATTN_WIDTH, tm), lambda i: (0, i))
    in_specs = ([rowblk(D_MODEL), _const_spec((1, D_MODEL))] + [resident(w) for w in weights]
                + [_const_spec(conv_w.shape), _const_spec(conv_b.shape), _const_spec(dtb.shape)])
    out_specs = [rowblk(SSM_WIDTH), rowblk(SSM_WIDTH), rowblk(BC_WIDTH), rowblk(BC_WIDTH), rowblk(LANES),
                 colblk, rowblk(KV_WIDTH), rowblk(KV_WIDTH), colblk,
                 pl.BlockSpec((1, SUBLANES, CONV_DIM), lambda i: (i // steps_per_seq, 0, 0))]
    f = lambda r, c, dt=F32: jax.ShapeDtypeStruct((r, c), dt)
    out_shape = [f(rows, SSM_WIDTH), f(rows, SSM_WIDTH), f(rows, BC_WIDTH, BF16), f(rows, BC_WIDTH, BF16),
                 f(rows, LANES), f(ATTN_WIDTH, rows), f(rows, KV_WIDTH), f(rows, KV_WIDTH), f(ATTN_WIDTH, rows),
                 jax.ShapeDtypeStruct((batch, SUBLANES, CONV_DIM), F32)]
    return pl.pallas_call(
        functools.partial(_inproj_prompt_body, steps_per_seq), grid=(rows // tm,), in_specs=in_specs,
        out_specs=out_specs, out_shape=out_shape, scratch_shapes=[pltpu.VMEM((SUBLANES, CONV_DIM), F32)],
        compiler_params=_params("arbitrary"), name="inproj_prompt")(
            x2d, norm_w, *weights, conv_w, conv_b, dtb)


def _outproj_body(attn_transposed, ys_ref, ya_ref, x_ref, wt_ref, wb_ref, o_ref):
    ya = ya_ref[...].astype(BF16)
    attn = _dot_tn(ya, wb_ref[...]) if attn_transposed else _dot(ya, wb_ref[...])
    o_ref[...] = x_ref[...] + _dot(ys_ref[...].astype(BF16), wt_ref[...]) + attn


def _outproj(y_ssm, y_attn, x2d, w_top, w_bot, attn_transposed):
    rows = x2d.shape[0]
    tm = min(PROJ_ROWS, rows)
    row_spec = pl.BlockSpec((tm, D_MODEL), lambda i: (i, 0))
    ya_spec = pl.BlockSpec((ATTN_WIDTH, tm), lambda i: (0, i)) if attn_transposed else row_spec
    w_spec = pl.BlockSpec((SSM_WIDTH, D_MODEL), lambda i: (0, 0), pipeline_mode=pl.Buffered(1))
    return pl.pallas_call(
        functools.partial(_outproj_body, attn_transposed), grid=(rows // tm,),
        in_specs=[row_spec, ya_spec, row_spec, w_spec, w_spec],
        out_specs=row_spec, out_shape=jax.ShapeDtypeStruct((rows, D_MODEL), F32),
        compiler_params=_params("parallel"), name="outproj")(y_ssm, y_attn, x2d, w_top, w_bot)


def _group_rmsnorm(gy, norm_w):
    parts = []
    for g in range(SSM_GROUPS):
        blk = gy[:, g * GROUP_WIDTH:(g + 1) * GROUP_WIDTH]
        ms = jnp.mean(blk * blk, axis=-1, keepdims=True)
        parts.append(blk * lax.rsqrt(ms + EPS))
    return jnp.concatenate(parts, axis=1) * norm_w


def _ssd_chunk(gz, xs, b_bf, c_bf, dt, a_neg, dskip, norm_w, e_mat, state):
    xs_bf = xs.astype(BF16)

    a = dt * a_neg
    li = lax.broadcasted_iota(jnp.int32, (CHUNK, CHUNK), 0)
    si = lax.broadcasted_iota(jnp.int32, (CHUNK, CHUNK), 1)
    causal = li >= si
    a_cum = _dot_sel3(jnp.where(causal, 1.0, 0.0).astype(BF16), a)
    a_cum_t = a_cum.T
    dt_t = dt.T
    ea_full = _dot_sel(jnp.exp(a_cum), e_mat)
    w_full = _dot_sel(dt * jnp.exp(a_cum[CHUNK - 1:CHUNK, :] - a_cum), e_mat)

    cb = [_dot_nt(c_bf[:, g * SSM_STATE:(g + 1) * SSM_STATE], b_bf[:, g * SSM_STATE:(g + 1) * SSM_STATE])
          for g in range(SSM_GROUPS)]
    half = lax.broadcasted_iota(jnp.int32, (CHUNK, LANES), 1) < SSM_HEAD_DIM
    heads_per_group = SSM_HEADS // SSM_GROUPS
    y_parts = []
    for pair in range(SSM_HEADS // 2):
        blocks = []
        for h in (2 * pair, 2 * pair + 1):
            seg = a_cum[:, h:h + 1] - a_cum_t[h:h + 1, :]
            decay = jnp.exp(jnp.where(causal, seg, -jnp.inf))
            blocks.append((cb[h // heads_per_group] * decay * dt_t[h:h + 1, :]).astype(BF16))
        lhs = jnp.concatenate(blocks, axis=1)
        xp = xs_bf[:, pair * LANES:(pair + 1) * LANES]
        zero = jnp.zeros_like(xp)
        rhs = jnp.concatenate([jnp.where(half, xp, zero), jnp.where(half, zero, xp)], axis=0)
        y_parts.append(_dot(lhs, rhs))
    y_diag = jnp.concatenate(y_parts, axis=1)

    state_bf = state.astype(BF16)
    xw_bf = (xs * w_full).astype(BF16)
    y_off, upd = [], []
    for g in range(SSM_GROUPS):
        cols = slice(g * GROUP_WIDTH, (g + 1) * GROUP_WIDTH)
        ns = slice(g * SSM_STATE, (g + 1) * SSM_STATE)
        y_off.append(_dot(c_bf[:, ns], state_bf[:, cols]))
        upd.append(_dot_tn(b_bf[:, ns], xw_bf[:, cols]))
    y = y_diag + jnp.concatenate(y_off, axis=1) * ea_full + dskip * xs
    new_state = state * ea_full[CHUNK - 1:CHUNK, :] + jnp.concatenate(upd, axis=1)
    return _group_rmsnorm(y * gz, norm_w), new_state


def _ssd_prompt_body(gz_ref, xs_ref, b_ref, c_ref, dt_ref, alog_ref, dskip_ref, nw_ref, e_ref,
                     y_ref, st_ref, state_sc):
    c = pl.program_id(1)

    @pl.when(c == 0)
    def _():
        state_sc[...] = jnp.zeros_like(state_sc)

    a_neg = -jnp.exp(alog_ref[...])
    state = state_sc[...]
    for j in range(CHUNKS_PER_STEP):
        r = slice(j * CHUNK, (j + 1) * CHUNK)
        y, state = _ssd_chunk(gz_ref[r, :], xs_ref[r, :], b_ref[r, :], c_ref[r, :], dt_ref[r, :], a_neg,
                              dskip_ref[...], nw_ref[...], e_ref[...], state)
        y_ref[r, :] = y.astype(y_ref.dtype)
    state_sc[...] = state

    @pl.when(c == pl.num_programs(1) - 1)
    def _():
        st_ref[0] = state.T


def _ssd_prompt(gz, xs, b, c, dt, alog, dskip, norm_w, e_mat, batch, seq):
    step_rows = CHUNKS_PER_STEP * CHUNK
    nc = seq // step_rows
    row = lambda w: pl.BlockSpec((step_rows, w), lambda b, c: (b * nc + c, 0))
    in_specs = [row(SSM_WIDTH), row(SSM_WIDTH), row(BC_WIDTH), row(BC_WIDTH), row(LANES),
                _const_spec((1, LANES)), _const_spec((1, SSM_WIDTH)), _const_spec((1, SSM_WIDTH)),
                _const_spec((LANES, SSM_WIDTH))]
    out_specs = [row(SSM_WIDTH), pl.BlockSpec((1, SSM_WIDTH, SSM_STATE), lambda b, c: (b, 0, 0))]
    out_shape = [jax.ShapeDtypeStruct((batch * seq, SSM_WIDTH), BF16),
                 jax.ShapeDtypeStruct((batch, SSM_WIDTH, SSM_STATE), F32)]
    scratch = [pltpu.VMEM((SSM_STATE, SSM_WIDTH), F32)]
    return pl.pallas_call(
        _ssd_prompt_body, grid=(batch, nc), in_specs=in_specs, out_specs=out_specs, out_shape=out_shape,
        scratch_shapes=scratch, compiler_params=_params("arbitrary", "arbitrary"), name="ssd_prompt")(
            gz, xs, b, c, dt, alog, dskip, norm_w, e_mat)


def _rel_bucket_np(dist):
    max_exact = REL_BUCKETS // 2
    d_f = np.maximum(dist, 1).astype(np.float32)
    large = max_exact + (np.log(d_f / np.float32(max_exact)) / np.float32(math.log(REL_MAX_DIST / max_exact))
                         * np.float32(REL_BUCKETS - max_exact)).astype(np.int32)
    return np.where(dist < max_exact, dist, np.minimum(large, REL_BUCKETS - 1)).astype(np.int32)


def _bucket_or_masked(dist, extra_mask=None):
    ok = (dist >= 0) & (dist <= WINDOW)
    if extra_mask is not None:
        ok = ok & extra_mask
    return np.where(ok, _rel_bucket_np(np.clip(dist, 0, WINDOW)), -1).astype(np.int32)


def _bias_body(rel_ref, bucket_ref, o_ref):
    bucket = bucket_ref[0]

    def per_head(h, carry):
        acc = jnp.full(bucket.shape, NEG, F32)
        for bkt in range(REL_BUCKETS):
            acc = jnp.where(bucket == bkt, rel_ref[bkt * ATTN_HEADS + h], acc)
        o_ref[0, h] = acc
        return carry

    lax.fori_loop(0, ATTN_HEADS, per_head, 0)


def _bias_tables(rel_flat, buckets):
    nv, lq, lk = buckets.shape
    return pl.pallas_call(
        _bias_body, grid=(nv,),
        in_specs=[pl.BlockSpec(memory_space=pltpu.SMEM), pl.BlockSpec((1, lq, lk), lambda v: (v, 0, 0))],
        out_specs=pl.BlockSpec((1, ATTN_HEADS, lq, lk), lambda v: (v, 0, 0, 0)),
        out_shape=jax.ShapeDtypeStruct((nv, ATTN_HEADS, lq, lk), F32),
        compiler_params=_params("arbitrary"), name="rel_bias")(rel_flat, buckets)


def _bias_t_body(rel_ref, bucket_ref, o_ref):
    bucket = bucket_ref[0]
    lq = bucket.shape[1]

    def per_kv_head(n, carry):
        for g in range(Q_PER_KV):
            acc = jnp.full(bucket.shape, NEG, F32)
            for bkt in range(REL_BUCKETS):
                acc = jnp.where(bucket == bkt, rel_ref[bkt * ATTN_HEADS + n * Q_PER_KV + g], acc)
            o_ref[0, n, :, g * lq:(g + 1) * lq] = acc
        return carry

    lax.fori_loop(0, ATTN_KV_HEADS, per_kv_head, 0)


def _bias_tables_t(rel_flat, buckets_t):
    nv, lk, lq = buckets_t.shape
    out_dims = (ATTN_KV_HEADS, lk, Q_PER_KV * lq)
    return pl.pallas_call(
        _bias_t_body, grid=(nv,),
        in_specs=[pl.BlockSpec(memory_space=pltpu.SMEM), pl.BlockSpec((1, lk, lq), lambda v: (v, 0, 0))],
        out_specs=pl.BlockSpec((1,) + out_dims, lambda v: (v, 0, 0, 0)),
        out_shape=jax.ShapeDtypeStruct((nv,) + out_dims, F32),
        compiler_params=_params("arbitrary"), name="rel_bias_t")(rel_flat, buckets_t)


def _head_rmsnorm(x, g_mat, e_mat, w):
    ms = _dot_sel(x * x, g_mat) * (1.0 / ATTN_HEAD_DIM)
    return x * _dot_sel(lax.rsqrt(ms + EPS), e_mat) * w


def _lane_head(shape):
    return lax.broadcasted_iota(jnp.int32, shape, 1) // ATTN_HEAD_DIM


def _sink_column(sink_ref, n, rows_per_head):
    return jnp.concatenate(
        [jnp.full((rows_per_head, 1), sink_ref[n * Q_PER_KV + g], F32) for g in range(Q_PER_KV)], axis=0)


def _softmax_with_sink(s, sink):
    m = jnp.maximum(jnp.max(s, axis=-1, keepdims=True), sink)
    p = jnp.exp(s - m)
    denom = jnp.sum(p, axis=-1, keepdims=True) + jnp.exp(sink - m)
    return p / denom


def _attn_prompt_body(qt_ref, k_ref, v_ref, gzt_ref, qwt_ref, kw_ref, g_ref, e_ref, bias_ref, sink_ref,
                      yt_ref, kn_ref, vn_ref, kcat_sc, vcat_t_sc):
    T = CHUNK
    step = pl.program_id(1)

    @pl.when(step == 0)
    def _():
        kcat_sc[0:T, :] = jnp.zeros((T, KV_WIDTH), BF16)
        vcat_t_sc[:, 0:T] = jnp.zeros((KV_WIDTH, T), BF16)

    cols_step = CHUNKS_PER_STEP * T
    q3 = qt_ref[...].reshape(ATTN_HEADS, ATTN_HEAD_DIM, cols_step)
    ms = jnp.mean(q3 * q3, axis=1, keepdims=True)
    qn = ((q3 * lax.rsqrt(ms + EPS)).reshape(ATTN_WIDTH, cols_step) * qwt_ref[...]).astype(BF16)

    kn = _head_rmsnorm(k_ref[...], g_ref[...], e_ref[...], kw_ref[...])
    v = v_ref[...]
    kn_ref[0] = kn[cols_step - T:]
    vn_ref[0] = v[cols_step - T:]
    kcat_sc[T:, :] = kn.astype(BF16)
    vcat_t_sc[:, T:] = v.T.astype(BF16)

    lane_head = _lane_head((2 * T, KV_WIDTH))
    row_head = lax.broadcasted_iota(jnp.int32, (KV_WIDTH, 2 * T), 0) // ATTN_HEAD_DIM
    zero = jnp.zeros((2 * T, KV_WIDTH), BF16)
    first_variant = jnp.minimum(step, 1)
    for j in range(CHUNKS_PER_STEP):
        kcat = kcat_sc[j * T:(j + 2) * T, :]
        vcat_t = vcat_t_sc[:, j * T:(j + 2) * T]
        variant = first_variant if j == 0 else 1
        q_blk = qn[:, j * T:(j + 1) * T]
        q_cols = jnp.concatenate([q_blk[g * KV_WIDTH:(g + 1) * KV_WIDTH] for g in range(Q_PER_KV)], axis=1)
        probs, vals = [], []
        for n in range(ATTN_KV_HEADS):
            s = _dot(jnp.where(lane_head == n, kcat, zero), q_cols)
            sink = sink_ref[n]
            cols = []
            for g in range(Q_PER_KV):
                c = slice(g * T, (g + 1) * T)
                sg = s[:, c] + bias_ref[variant, n, :, c]
                m = jnp.maximum(jnp.max(sg, axis=0, keepdims=True), sink[:, c])
                p = jnp.exp(sg - m)
                denom = jnp.sum(p, axis=0, keepdims=True) + jnp.exp(sink[:, c] - m)
                cols.append((p * (1.0 / denom)).astype(BF16))
            probs.append(jnp.concatenate(cols, axis=1))
            vals.append(jnp.where(row_head == n, vcat_t, zero.T))
        o_t = _dot(jnp.concatenate(vals, axis=1), jnp.concatenate(probs, axis=0))
        y_t = jnp.concatenate([o_t[:, g * T:(g + 1) * T] for g in range(Q_PER_KV)], axis=0)
        yt_ref[:, j * T:(j + 1) * T] = (y_t * gzt_ref[:, j * T:(j + 1) * T]).astype(yt_ref.dtype)
    kcat_sc[0:T, :] = kcat_sc[cols_step:, :]
    vcat_t_sc[:, 0:T] = vcat_t_sc[:, cols_step:]


def _attn_prompt(q_t, k, v, z_t, qw_t, kw, g_mat, e_mat, bias_t, sink_rows, batch, seq):
    step_rows = CHUNKS_PER_STEP * CHUNK
    nb = seq // step_rows
    row = lambda w: pl.BlockSpec((step_rows, w), lambda b, i: (b * nb + i, 0))
    col = pl.BlockSpec((ATTN_WIDTH, step_rows), lambda b, i: (0, b * nb + i))
    in_specs = [col, row(KV_WIDTH), row(KV_WIDTH), col, _const_spec((ATTN_WIDTH, step_rows)),
                _const_spec((1, KV_WIDTH)), _const_spec((KV_WIDTH, LANES)), _const_spec((LANES, KV_WIDTH)),
                pl.BlockSpec(bias_t.shape, lambda b, i: (0, 0, 0, 0), pipeline_mode=pl.Buffered(1)),
                _const_spec(sink_rows.shape)]
    kv_out = pl.BlockSpec((1, CHUNK, KV_WIDTH), lambda b, i: (b, 0, 0))
    out_specs = [col, kv_out, kv_out]
    out_shape = [jax.ShapeDtypeStruct((ATTN_WIDTH, batch * seq), BF16),
                 jax.ShapeDtypeStruct((batch, CHUNK, KV_WIDTH), F32),
                 jax.ShapeDtypeStruct((batch, CHUNK, KV_WIDTH), F32)]
    scratch = [pltpu.VMEM((step_rows + CHUNK, KV_WIDTH), BF16), pltpu.VMEM((KV_WIDTH, step_rows + CHUNK), BF16)]
    return pl.pallas_call(
        _attn_prompt_body, grid=(batch, nb), in_specs=in_specs, out_specs=out_specs, out_shape=out_shape,
        scratch_shapes=scratch, compiler_params=_params("arbitrary", "arbitrary"), name="attn_prompt")(
            q_t, k, v, z_t, qw_t, kw, g_mat, e_mat, bias_t, sink_rows)


def _ssd_sample_vec_body(z_ref, xbc_ref, dt_ref, sconv_ref, cw_ref, cb_ref, dtb_ref, alog_ref, dskip_ref,
                         gh_ref, e_ref, ypart_ref, ea_ref, xw_ref, b_ref, c_ref, cdec_ref, convnew_ref):
    steps = xbc_ref.shape[0]
    tail = CONV_WIDTH - 1
    full = [sconv_ref[j] for j in range(tail)] + [xbc_ref[l] for l in range(steps)]
    for j in range(tail):
        convnew_ref[j] = full[steps + j]
    gh = gh_ref[...]
    e_mat = e_ref[...]
    a_neg = -jnp.exp(alog_ref[...])
    xs, bm, cm, dts, acum = [], [], [], [], []
    run = None
    for l in range(steps):
        conv = cb_ref[...]
        for tap in range(CONV_WIDTH):
            conv = conv + full[l + tap] * cw_ref[tap:tap + 1, :]
        act = _silu(conv)
        xs.append(act[:, :SSM_WIDTH])
        bm.append(act[:, SSM_WIDTH:SSM_WIDTH + BC_WIDTH])
        cm.append(act[:, SSM_WIDTH + BC_WIDTH:])
        d = _softplus(dt_ref[l] + dtb_ref[...])
        dts.append(d)
        run = d * a_neg if run is None else run + d * a_neg
        acum.append(run)
        b_ref[l] = bm[l]
        c_ref[l] = cm[l]
    for l in range(steps):
        y = dskip_ref[...] * xs[l]
        for s in range(l + 1):
            cb_h = _dot_sel(cm[l] * bm[s], gh)
            coef = cb_h * jnp.exp(acum[l] - acum[s]) * dts[s]
            y = y + _dot_sel(coef, e_mat) * xs[s]
        ypart_ref[l] = y
        ea_ref[l] = _dot_sel(jnp.exp(acum[l]), e_mat)
        xw_ref[l] = xs[l] * _dot_sel(dts[l] * jnp.exp(acum[steps - 1] - acum[l]), e_mat)
    cdec_ref[...] = jnp.exp(acum[steps - 1])


def _ssd_sample_vec(z3, xbc3, dt3, sconv3, conv_w, conv_b, dtb, alog, dskip, gh_mat, e_mat):
    steps, nb = z3.shape[0], z3.shape[1]
    f = lambda *s: jax.ShapeDtypeStruct(s, F32)
    out_shape = [f(steps, nb, SSM_WIDTH), f(steps, nb, SSM_WIDTH), f(steps, nb, SSM_WIDTH),
                 f(steps, nb, BC_WIDTH), f(steps, nb, BC_WIDTH), f(nb, LANES), f(CONV_WIDTH - 1, nb, CONV_DIM)]
    return pl.pallas_call(_ssd_sample_vec_body, out_shape=out_shape,
                          compiler_params=pltpu.CompilerParams(vmem_limit_bytes=VMEM_LIMIT),
                          name="ssd_sample_vec")(
        z3, xbc3, dt3, sconv3, conv_w, conv_b, dtb, alog, dskip, gh_mat, e_mat)


def _ssd_sample_state_body(cdec_ref, st_ref, c_ref, b_ref, xw_ref, new_ref, yoff_ref):
    i = pl.program_id(0)
    heads_per_group = SSM_HEADS // SSM_GROUPS
    for j in range(SAMPLE_BATCH_BLOCK):
        st = st_ref[j]
        cb_bf = c_ref[:, j, :].astype(BF16)
        bb_bf = b_ref[:, j, :].astype(BF16)
        xw_bf = xw_ref[:, j, :].astype(BF16)
        y_parts = []
        for g in range(SSM_GROUPS):
            rows = slice(g * GROUP_WIDTH, (g + 1) * GROUP_WIDTH)
            ns = slice(g * SSM_STATE, (g + 1) * SSM_STATE)
            y_parts.append(_dot_nt(cb_bf[:, ns], st[rows].astype(BF16)))
            upd = _dot_tn(xw_bf[:, rows], bb_bf[:, ns])
            for hh in range(heads_per_group):
                h = g * heads_per_group + hh
                r = slice(h * SSM_HEAD_DIM, (h + 1) * SSM_HEAD_DIM)
                dec = cdec_ref[(i * SAMPLE_BATCH_BLOCK + j) * SSM_HEADS + h]
                new_ref[j, r, :] = st[r] * dec + upd[hh * SSM_HEAD_DIM:(hh + 1) * SSM_HEAD_DIM]
        yoff_ref[:, j, :] = jnp.concatenate(y_parts, axis=1)


def _ssd_sample_state(cdec_flat, state, c3, b3, xw3):
    steps, nb = c3.shape[0], c3.shape[1]
    bb = SAMPLE_BATCH_BLOCK
    tok = lambda w: pl.BlockSpec((steps, bb, w), lambda i: (0, i, 0))
    st_spec = pl.BlockSpec((bb, SSM_WIDTH, SSM_STATE), lambda i: (i, 0, 0))
    return pl.pallas_call(
        _ssd_sample_state_body, grid=(nb // bb,),
        in_specs=[pl.BlockSpec(memory_space=pltpu.SMEM), st_spec, tok(BC_WIDTH), tok(BC_WIDTH), tok(SSM_WIDTH)],
        out_specs=[st_spec, tok(SSM_WIDTH)],
        out_shape=[jax.ShapeDtypeStruct(state.shape, F32), jax.ShapeDtypeStruct((steps, nb, SSM_WIDTH), F32)],
        compiler_params=_params("parallel"), name="ssd_sample_state")(cdec_flat, state, c3, b3, xw3)


def _sample_finish_body(ypart_ref, yoff_ref, ea_ref, z_ref, nw_ref, q_ref, k_ref, qw_ref, kw_ref, g_ref, e_ref,
                        yssm_ref, qn_ref, kn_ref):
    y = ypart_ref[...] + yoff_ref[...] * ea_ref[...]
    yssm_ref[...] = _group_rmsnorm(y * _silu(z_ref[...]), nw_ref[...])
    g_mat = g_ref[...]
    e_mat = e_ref[...]
    qn_ref[...] = _head_rmsnorm(q_ref[...], g_mat, e_mat, qw_ref[...]) * ATTN_SCALE
    kn_ref[...] = _head_rmsnorm(k_ref[...], g_mat[:KV_WIDTH], e_mat[:, :KV_WIDTH], kw_ref[...])


def _sample_finish(ypart, yoff, ea, z, norm_w, q, k, qw, kw, g_mat, e_mat):
    rows = z.shape[0]
    f = lambda w: jax.ShapeDtypeStruct((rows, w), F32)
    return pl.pallas_call(_sample_finish_body, out_shape=[f(SSM_WIDTH), f(ATTN_WIDTH), f(KV_WIDTH)],
                          compiler_params=pltpu.CompilerParams(vmem_limit_bytes=VMEM_LIMIT),
                          name="sample_finish")(ypart, yoff, ea, z, norm_w, q, k, qw, kw, g_mat, e_mat)


def _attn_sample_body(sink_ref, q_ref, kn_ref, vn_ref, z_ref, ck_ref, cv_ref, biasc_ref, biasn_ref,
                      y_ref, ko_ref, vo_ref):
    steps = q_ref.shape[0]
    rows_per_head = steps
    lane_head = _lane_head((Q_PER_KV * steps, KV_WIDTH))
    sink = jnp.concatenate([_sink_column(sink_ref, n, rows_per_head) for n in range(ATTN_KV_HEADS)], axis=0)
    pad = SUBLANES - steps
    for j in range(SAMPLE_BATCH_BLOCK):
        q = q_ref[:, j, :]
        qg = jnp.concatenate([q[:, g * KV_WIDTH:(g + 1) * KV_WIDTH] for g in range(Q_PER_KV)], axis=0)
        zero = jnp.zeros_like(qg)
        qx = jnp.concatenate([jnp.where(lane_head == n, qg, zero) for n in range(ATTN_KV_HEADS)], axis=0)
        qx = qx.astype(BF16)
        ck = ck_ref[j]
        cv = cv_ref[j]
        k_new = kn_ref[:, j, :]
        v_new = vn_ref[:, j, :]
        k_new8 = jnp.concatenate([k_new, jnp.zeros((pad, KV_WIDTH), F32)], axis=0)
        v_new8 = jnp.concatenate([v_new, jnp.zeros((pad, KV_WIDTH), F32)], axis=0)
        s_c = _dot_nt(qx, ck.astype(BF16)) + biasc_ref[...]
        s_n = _dot_nt(qx, k_new8.astype(BF16)) + biasn_ref[...]
        m = jnp.maximum(jnp.maximum(jnp.max(s_c, axis=-1, keepdims=True), jnp.max(s_n, axis=-1, keepdims=True)),
                        sink)
        p_c = jnp.exp(s_c - m)
        p_n = jnp.exp(s_n - m)
        denom = (jnp.sum(p_c, axis=-1, keepdims=True) + jnp.sum(p_n, axis=-1, keepdims=True)
                 + jnp.exp(sink - m))
        o = (_dot((p_c / denom).astype(BF16), cv.astype(BF16))
             + _dot((p_n / denom).astype(BF16), v_new8.astype(BF16)))
        blk = Q_PER_KV * steps
        lane_rows = _lane_head((blk, KV_WIDTH))
        og = jnp.zeros((blk, KV_WIDTH), F32)
        for n in range(ATTN_KV_HEADS):
            og = og + jnp.where(lane_rows == n, o[n * blk:(n + 1) * blk], 0.0)
        y = jnp.concatenate([og[g * steps:(g + 1) * steps] for g in range(Q_PER_KV)], axis=1)
        y_ref[:, j, :] = y * _silu(z_ref[:, j, :])
        ko_ref[j, 0:WINDOW - steps, :] = ck[steps:, :]
        ko_ref[j, WINDOW - steps:WINDOW, :] = k_new
        vo_ref[j, 0:WINDOW - steps, :] = cv[steps:, :]
        vo_ref[j, WINDOW - steps:WINDOW, :] = v_new


def _attn_sample(sinks, q3, kn3, vn3, z3, cache_k, cache_v, bias_c, bias_n):
    steps, nb = q3.shape[0], q3.shape[1]
    bb = SAMPLE_BATCH_BLOCK
    tok = lambda w: pl.BlockSpec((steps, bb, w), lambda i: (0, i, 0))
    cache_spec = pl.BlockSpec((bb, WINDOW, KV_WIDTH), lambda i: (i, 0, 0))
    return pl.pallas_call(
        _attn_sample_body, grid=(nb // bb,),
        in_specs=[pl.BlockSpec(memory_space=pltpu.SMEM), tok(ATTN_WIDTH), tok(KV_WIDTH), tok(KV_WIDTH),
                  tok(ATTN_WIDTH), cache_spec, cache_spec, _const_spec(bias_c.shape), _const_spec(bias_n.shape)],
        out_specs=[tok(ATTN_WIDTH), cache_spec, cache_spec],
        out_shape=[jax.ShapeDtypeStruct((steps, nb, ATTN_WIDTH), F32),
                   jax.ShapeDtypeStruct(cache_k.shape, F32), jax.ShapeDtypeStruct(cache_v.shape, F32)],
        compiler_params=_params("parallel"), name="attn_sample")(
            sinks, q3, kn3, vn3, z3, cache_k, cache_v, bias_c, bias_n)


def _static_tables(steps):
    lanes = np.arange(ATTN_WIDTH)
    g_mat = np.zeros((ATTN_WIDTH, LANES), np.float32)
    g_mat[lanes, lanes // ATTN_HEAD_DIM] = 1.0
    e_mat = g_mat.T.copy()
    bc = np.arange(BC_WIDTH)
    gh_mat = np.zeros((BC_WIDTH, LANES), np.float32)
    for h in range(SSM_HEADS):
        gh_mat[bc // SSM_STATE == h // (SSM_HEADS // SSM_GROUPS), h] = 1.0
    g_i, n_i, d_i = np.meshgrid(np.arange(Q_PER_KV), np.arange(ATTN_KV_HEADS), np.arange(ATTN_HEAD_DIM),
                                indexing="ij")
    perm = ((n_i * Q_PER_KV + g_i) * ATTN_HEAD_DIM + d_i).reshape(-1)
    T = CHUNK
    dist = np.arange(T)[:, None] - (np.arange(2 * T) - T)[None, :]
    first = np.broadcast_to((np.arange(2 * T) >= T)[None, :], dist.shape)
    prompt_buckets = np.stack([_bucket_or_masked(dist, first), _bucket_or_masked(dist)])
    dist_c = (np.arange(steps) + WINDOW)[:, None] - np.arange(WINDOW)[None, :]
    dist_n = np.arange(steps)[:, None] - np.arange(SUBLANES)[None, :]
    real = np.broadcast_to((np.arange(SUBLANES) < steps)[None, :], dist_n.shape)
    return dict(g=g_mat, e=e_mat, gh=gh_mat, perm=perm, prompt_buckets=prompt_buckets,
                cache_buckets=_bucket_or_masked(dist_c)[None], new_buckets=_bucket_or_masked(dist_n, real)[None])


def kernel(x_prompt, x_sample, cache_k, cache_v, state_ssm, state_conv, norm_w, w_in, conv_w, conv_b, dt_bias,
           a_log, d_skip, ssm_norm_w, q_norm_w, k_norm_w, sinks, rel_table, w_out):
    assert w_in.shape[0] == 1, "single-layer kernel"
    batch, seq, _ = x_prompt.shape
    nb, steps, _ = x_sample.shape
    tab = _static_tables(steps)
    perm = tab["perm"]
    g_mat = jnp.asarray(tab["g"], BF16)
    e_mat = jnp.asarray(tab["e"], BF16)
    gh_mat = jnp.asarray(tab["gh"], BF16)

    w = w_in[0]
    edges = np.cumsum([0, SSM_WIDTH, CONV_DIM, SSM_HEADS, ATTN_WIDTH, KV_WIDTH, KV_WIDTH, ATTN_WIDTH])
    wz, wxbc, wdt, wq, wk, wv, wza = [w[:, a:b] for a, b in zip(edges[:-1], edges[1:])]
    wdt = jnp.pad(wdt, ((0, 0), (0, LANES - SSM_HEADS)))
    weights = [m.astype(BF16) for m in (wz, wxbc, wdt, wq[:, perm], wk, wv, wza[:, perm])]
    wo = w_out[0]
    wo_top = wo[:SSM_WIDTH].astype(BF16)
    wo_bot = wo[SSM_WIDTH:][perm].astype(BF16)

    row = lambda v, width: jnp.pad(v.reshape(1, -1), ((0, 0), (0, width - v.size)))
    nw = row(norm_w[0], D_MODEL)
    cw = conv_w[0]
    cb = row(conv_b[0], CONV_DIM)
    dtb = row(dt_bias[0], LANES)
    alog = row(a_log[0], LANES)
    dskip = jnp.repeat(d_skip[0], SSM_HEAD_DIM).reshape(1, SSM_WIDTH)
    snw = row(ssm_norm_w[0], SSM_WIDTH)
    qw = jnp.tile(q_norm_w[0], ATTN_HEADS).reshape(1, ATTN_WIDTH)
    kw = jnp.tile(k_norm_w[0], ATTN_KV_HEADS).reshape(1, KV_WIDTH)
    sink = sinks[0]
    rel_flat = rel_table.reshape(-1)

    xp = x_prompt.reshape(batch * seq, D_MODEL)
    feature_major = (False, False, False, True, False, False, True)
    weights_p = [m.T if t else m for t, m in zip(feature_major, weights)]
    gz, xs_p, b_p, c_p, dt_p, q_t, k, v, gza_t, tail_p = _inproj_prompt(xp, nw, weights_p, cw, cb, dtb, batch, seq)
    y_ssm, st_p = _ssd_prompt(gz, xs_p, b_p, c_p, dt_p, alog, dskip, snw, e_mat, batch, seq)
    bias_t = _bias_tables_t(rel_flat, jnp.asarray(tab["prompt_buckets"].transpose(0, 2, 1)))
    qw_t = jnp.broadcast_to((qw * ATTN_SCALE).reshape(ATTN_WIDTH, 1), (ATTN_WIDTH, CHUNKS_PER_STEP * CHUNK))
    sink_rows = jnp.repeat(sink.reshape(ATTN_KV_HEADS, Q_PER_KV), CHUNK, axis=1).reshape(ATTN_KV_HEADS, 1, -1)
    y_attn_t, k_p, v_p = _attn_prompt(q_t, k, v, gza_t, qw_t, kw, g_mat[:KV_WIDTH], e_mat[:, :KV_WIDTH], bias_t,
                                      sink_rows, batch, seq)
    y_p = _outproj(y_ssm, y_attn_t, xp, wo_top, wo_bot, True).reshape(batch, seq, D_MODEL)
    conv_p = tail_p[:, SUBLANES - (CONV_WIDTH - 1):, :]

    xs = jnp.swapaxes(x_sample, 0, 1).reshape(steps * nb, D_MODEL)
    z, xbc, dt, q, k, v, za = _inproj(xs, nw, weights)
    t3 = lambda a: a.reshape(steps, nb, a.shape[-1])
    sconv3 = jnp.swapaxes(state_conv[0], 0, 1)
    ypart, ea, xw, b3, c3, cdec, conv_s3 = _ssd_sample_vec(
        t3(z), t3(xbc), t3(dt), sconv3, cw, cb, dtb, alog, dskip, gh_mat, e_mat)
    state_in = state_ssm[0].reshape(nb, SSM_WIDTH, SSM_STATE)
    st_s, yoff = _ssd_sample_state(cdec[:, :SSM_HEADS].reshape(-1), state_in, c3, b3, xw)
    f2 = lambda a: a.reshape(steps * nb, a.shape[-1])
    y_ssm, qn, kn = _sample_finish(f2(ypart), f2(yoff), f2(ea), z, snw, q, k, qw, kw, g_mat, e_mat)
    bias_c = _bias_tables(rel_flat, jnp.asarray(tab["cache_buckets"])).reshape(ATTN_HEADS * steps, WINDOW)
    bias_n = _bias_tables(rel_flat, jnp.asarray(tab["new_buckets"])).reshape(ATTN_HEADS * steps, SUBLANES)
    ck = cache_k[0].reshape(nb, WINDOW, KV_WIDTH)
    cv = cache_v[0].reshape(nb, WINDOW, KV_WIDTH)
    y_attn3, k_s, v_s = _attn_sample(sink, t3(qn), t3(kn), t3(v), t3(za), ck, cv, bias_c, bias_n)
    y_s = _outproj(y_ssm, f2(y_attn3), xs, wo_top, wo_bot, False)
    y_s = jnp.swapaxes(y_s.reshape(steps, nb, D_MODEL), 0, 1)

    kv5 = lambda a: a.reshape(1, a.shape[0], WINDOW, ATTN_KV_HEADS, ATTN_HEAD_DIM)
    st5 = lambda a: a.reshape(1, a.shape[0], SSM_HEADS, SSM_HEAD_DIM, SSM_STATE)
    return (y_p, y_s, kv5(k_p), kv5(v_p), st5(st_p), conv_p[None],
            kv5(k_s), kv5(v_s), st5(st_s), jnp.swapaxes(conv_s3, 0, 1)[None])
```

```python
import functools
import math

import numpy as np
import jax
import jax.numpy as jnp
from jax import lax
from jax.experimental import pallas as pl
from jax.experimental.pallas import tpu as pltpu

F32 = jnp.float32
BF16 = jnp.bfloat16

D_MODEL = 1024
SSM_HEADS = 16
SSM_HEAD_DIM = 64
SSM_WIDTH = SSM_HEADS * SSM_HEAD_DIM
SSM_GROUPS = 2
SSM_STATE = 128
GROUP_WIDTH = SSM_WIDTH // SSM_GROUPS
BC_WIDTH = SSM_GROUPS * SSM_STATE
CONV_WIDTH = 4
CONV_DIM = SSM_WIDTH + 2 * BC_WIDTH
CHUNK = 128
ATTN_HEADS = 16
ATTN_KV_HEADS = 4
Q_PER_KV = ATTN_HEADS // ATTN_KV_HEADS
ATTN_HEAD_DIM = 64
ATTN_WIDTH = ATTN_HEADS * ATTN_HEAD_DIM
KV_WIDTH = ATTN_KV_HEADS * ATTN_HEAD_DIM
WINDOW = 128
ATTN_SCALE = ATTN_HEAD_DIM ** -0.5
REL_BUCKETS = 32
REL_MAX_DIST = 128
EPS = 1e-6
NEG = -1e30

LANES = 128
SUBLANES = 8
MXU_WIDTH = 256
VMEM_LIMIT = 56 * 1024 * 1024
PROJ_ROWS = 512
CHUNKS_PER_STEP = 4
SAMPLE_BATCH_BLOCK = 8


def _dot(a, b):
    return jnp.dot(a, b, preferred_element_type=F32)


def _dot_nt(a, b):
    return lax.dot_general(a, b, (((1,), (1,)), ((), ())), preferred_element_type=F32)


def _dot_tn(a, b):
    return lax.dot_general(a, b, (((0,), (0,)), ((), ())), preferred_element_type=F32)


def _split2(v):
    hi = v.astype(BF16)
    lo = (v - hi.astype(F32)).astype(BF16)
    return hi, lo


def _dot_sel(v, m):
    hi, lo = _split2(v)
    return _dot(hi, m) + _dot(lo, m)


def _dot_sel3(m, v):
    hi = v.astype(BF16)
    r1 = v - hi.astype(F32)
    mid = r1.astype(BF16)
    lo = (r1 - mid.astype(F32)).astype(BF16)
    return _dot(m, hi) + _dot(m, mid) + _dot(m, lo)


def _silu(x):
    return x / (1.0 + jnp.exp(-x))


def _softplus(x):
    return jnp.maximum(x, 0.0) + jnp.log1p(jnp.exp(-jnp.abs(x)))


def _params(*sem):
    return pltpu.CompilerParams(dimension_semantics=sem, vmem_limit_bytes=VMEM_LIMIT)


def _const_spec(shape):
    nd = len(shape)
    return pl.BlockSpec(shape, lambda *_: (0,) * nd)


def _inproj_body(x_ref, nw_ref, *refs):
    n = len(refs) // 2
    x = x_ref[...]
    ms = jnp.mean(x * x, axis=-1, keepdims=True)
    h = (x * lax.rsqrt(ms + EPS) * nw_ref[...]).astype(BF16)
    for w_ref, o_ref in zip(refs[:n], refs[n:]):
        o_ref[...] = _dot(h, w_ref[...])


def _inproj(x2d, norm_w, weights):
    rows = x2d.shape[0]
    tm = min(PROJ_ROWS, rows)
    in_specs = [pl.BlockSpec((tm, D_MODEL), lambda i: (i, 0)), _const_spec((1, D_MODEL))]
    in_specs += [pl.BlockSpec(w.shape, lambda i: (0, 0), pipeline_mode=pl.Buffered(1)) for w in weights]
    out_specs = [pl.BlockSpec((tm, w.shape[1]), lambda i: (i, 0)) for w in weights]
    out_shape = [jax.ShapeDtypeStruct((rows, w.shape[1]), F32) for w in weights]
    return pl.pallas_call(
        _inproj_body, grid=(rows // tm,), in_specs=in_specs, out_specs=out_specs, out_shape=out_shape,
        compiler_params=_params("parallel"), name="inproj")(x2d, norm_w, *weights)


def _shift_rows(u, prev_tail, k):
    rows, width = u.shape
    tiles = jnp.concatenate([prev_tail, u], axis=0).reshape(rows // SUBLANES + 1, SUBLANES, width)
    rot = pltpu.roll(tiles, k, axis=1)
    first = lax.broadcasted_iota(jnp.int32, (1, SUBLANES, width), 1) < k
    return jnp.where(first, rot[:-1], rot[1:]).reshape(rows, width)


def _inproj_prompt_body(steps_per_seq, x_ref, nw_ref, wz_ref, wxbc_ref, wdt_ref, wqt_ref, wk_ref, wv_ref,
                        wzat_ref, cw_ref, cb_ref, dtb_ref,
                        gz_ref, xs_ref, b_ref, c_ref, dt_ref, qt_ref, k_ref, v_ref, gzat_ref, tail_ref, tail_sc):
    @pl.when(pl.program_id(0) % steps_per_seq == 0)
    def _():
        tail_sc[...] = jnp.zeros_like(tail_sc)

    x = x_ref[...]
    ms = jnp.mean(x * x, axis=-1, keepdims=True)
    h = (x * lax.rsqrt(ms + EPS) * nw_ref[...]).astype(BF16)
    rows = x.shape[0]
    for j in range(CONV_DIM // MXU_WIDTH):
        cols = slice(j * MXU_WIDTH, (j + 1) * MXU_WIDTH)
        u = _dot(h, wxbc_ref[:, cols])
        prev_tail = tail_sc[:, cols]
        conv = cb_ref[:, cols] + u * cw_ref[CONV_WIDTH - 1:CONV_WIDTH, cols]
        for k in range(1, CONV_WIDTH):
            tap = CONV_WIDTH - 1 - k
            conv = conv + _shift_rows(u, prev_tail, k) * cw_ref[tap:tap + 1, cols]
        new_tail = u[rows - SUBLANES:, :]
        tail_sc[:, cols] = new_tail
        tail_ref[0, :, cols] = new_tail
        act = _silu(conv)
        if j < SSM_WIDTH // MXU_WIDTH:
            xs_ref[:, cols] = act
        elif j == SSM_WIDTH // MXU_WIDTH:
            b_ref[...] = act.astype(BF16)
        else:
            c_ref[...] = act.astype(BF16)

    for j in range(SSM_WIDTH // MXU_WIDTH):
        cols = slice(j * MXU_WIDTH, (j + 1) * MXU_WIDTH)
        gz_ref[:, cols] = _silu(_dot(h, wz_ref[:, cols]))
    for j in range(ATTN_WIDTH // MXU_WIDTH):
        feats = slice(j * MXU_WIDTH, (j + 1) * MXU_WIDTH)
        gzat_ref[feats, :] = _silu(_dot_nt(wzat_ref[feats, :], h))
    dt_ref[...] = _softplus(_dot(h, wdt_ref[...]) + dtb_ref[...])
    qt_ref[...] = _dot_nt(wqt_ref[...], h)
    k_ref[...] = _dot(h, wk_ref[...])
    v_ref[...] = _dot(h, wv_ref[...])


def _inproj_prompt(x2d, norm_w, weights, conv_w, conv_b, dtb, batch, seq):
    rows = x2d.shape[0]
    tm = PROJ_ROWS
    steps_per_seq = seq // tm
    resident = lambda a: pl.BlockSpec(a.shape, lambda i: (0, 0), pipeline_mode=pl.Buffered(1))
    rowblk = lambda w: pl.BlockSpec((tm, w), lambda i: (i, 0))
    colblk = pl.BlockSpec((ATTN_WIDTH, tm), lambda i: (0, i))
    in_specs = ([rowblk(D_MODEL), _const_spec((1, D_MODEL))] + [resident(w) for w in weights]
                + [_const_spec(conv_w.shape), _const_spec(conv_b.shape), _const_spec(dtb.shape)])
    out_specs = [rowblk(SSM_WIDTH), rowblk(SSM_WIDTH), rowblk(BC_WIDTH), rowblk(BC_WIDTH), rowblk(LANES),
                 colblk, rowblk(KV_WIDTH), rowblk(KV_WIDTH), colblk,
                 pl.BlockSpec((1, SUBLANES, CONV_DIM), lambda i: (i // steps_per_seq, 0, 0))]
    f = lambda r, c, dt=F32: jax.ShapeDtypeStruct((r, c), dt)
    out_shape = [f(rows, SSM_WIDTH), f(rows, SSM_WIDTH), f(rows, BC_WIDTH, BF16), f(rows, BC_WIDTH, BF16),
                 f(rows, LANES), f(ATTN_WIDTH, rows), f(rows, KV_WIDTH), f(rows, KV_WIDTH), f(ATTN_WIDTH, rows),
                 jax.ShapeDtypeStruct((batch, SUBLANES, CONV_DIM), F32)]
    return pl.pallas_call(
        functools.partial(_inproj_prompt_body, steps_per_seq), grid=(rows // tm,), in_specs=in_specs,
        out_specs=out_specs, out_shape=out_shape, scratch_shapes=[pltpu.VMEM((SUBLANES, CONV_DIM), F32)],
        compiler_params=_params("arbitrary"), name="inproj_prompt")(
            x2d, norm_w, *weights, conv_w, conv_b, dtb)


def _outproj_body(attn_transposed, ys_ref, ya_ref, x_ref, wt_ref, wb_ref, o_ref):
    ya = ya_ref[...].astype(BF16)
    attn = _dot_tn(ya, wb_ref[...]) if attn_transposed else _dot(ya, wb_ref[...])
    o_ref[...] = x_ref[...] + _dot(ys_ref[...].astype(BF16), wt_ref[...]) + attn


def _outproj(y_ssm, y_attn, x2d, w_top, w_bot, attn_transposed):
    rows = x2d.shape[0]
    tm = min(PROJ_ROWS, rows)
    row_spec = pl.BlockSpec((tm, D_MODEL), lambda i: (i, 0))
    ya_spec = pl.BlockSpec((ATTN_WIDTH, tm), lambda i: (0, i)) if attn_transposed else row_spec
    w_spec = pl.BlockSpec((SSM_WIDTH, D_MODEL), lambda i: (0, 0), pipeline_mode=pl.Buffered(1))
    return pl.pallas_call(
        functools.partial(_outproj_body, attn_transposed), grid=(rows // tm,),
        in_specs=[row_spec, ya_spec, row_spec, w_spec, w_spec],
        out_specs=row_spec, out_shape=jax.ShapeDtypeStruct((rows, D_MODEL), F32),
        compiler_params=_params("parallel"), name="outproj")(y_ssm, y_attn, x2d, w_top, w_bot)


def _group_rmsnorm(gy, norm_w):
    parts = []
    for g in range(SSM_GROUPS):
        blk = gy[:, g * GROUP_WIDTH:(g + 1) * GROUP_WIDTH]
        ms = jnp.mean(blk * blk, axis=-1, keepdims=True)
        parts.append(blk * lax.rsqrt(ms + EPS))
    return jnp.concatenate(parts, axis=1) * norm_w


def _ssd_chunk(gz, xs, b_bf, c_bf, dt, a_neg, dskip, norm_w, e_mat, state):
    xs_bf = xs.astype(BF16)

    a = dt * a_neg
    li = lax.broadcasted_iota(jnp.int32, (CHUNK, CHUNK), 0)
    si = lax.broadcasted_iota(jnp.int32, (CHUNK, CHUNK), 1)
    causal = li >= si
    a_cum = _dot_sel3(jnp.where(causal, 1.0, 0.0).astype(BF16), a)
    a_cum_t = a_cum.T
    dt_t = dt.T
    ea_full = _dot_sel(jnp.exp(a_cum), e_mat)
    w_full = _dot_sel(dt * jnp.exp(a_cum[CHUNK - 1:CHUNK, :] - a_cum), e_mat)

    cb = [_dot_nt(c_bf[:, g * SSM_STATE:(g + 1) * SSM_STATE], b_bf[:, g * SSM_STATE:(g + 1) * SSM_STATE])
          for g in range(SSM_GROUPS)]
    half = lax.broadcasted_iota(jnp.int32, (CHUNK, LANES), 1) < SSM_HEAD_DIM
    heads_per_group = SSM_HEADS // SSM_GROUPS
    y_parts = []
    for pair in range(SSM_HEADS // 2):
        blocks = []
        for h in (2 * pair, 2 * pair + 1):
            seg = a_cum[:, h:h + 1] - a_cum_t[h:h + 1, :]
            decay = jnp.exp(jnp.where(causal, seg, -jnp.inf))
            blocks.append((cb[h // heads_per_group] * decay * dt_t[h:h + 1, :]).astype(BF16))
        lhs = jnp.concatenate(blocks, axis=1)
        xp = xs_bf[:, pair * LANES:(pair + 1) * LANES]
        zero = jnp.zeros_like(xp)
        rhs = jnp.concatenate([jnp.where(half, xp, zero), jnp.where(half, zero, xp)], axis=0)
        y_parts.append(_dot(lhs, rhs))
    y_diag = jnp.concatenate(y_parts, axis=1)

    state_bf = state.astype(BF16)
    xw_bf = (xs * w_full).astype(BF16)
    y_off, upd = [], []
    for g in range(SSM_GROUPS):
        cols = slice(g * GROUP_WIDTH, (g + 1) * GROUP_WIDTH)
        ns = slice(g * SSM_STATE, (g + 1) * SSM_STATE)
        y_off.append(_dot(c_bf[:, ns], state_bf[:, cols]))
        upd.append(_dot_tn(b_bf[:, ns], xw_bf[:, cols]))
    y = y_diag + jnp.concatenate(y_off, axis=1) * ea_full + dskip * xs
    new_state = state * ea_full[CHUNK - 1:CHUNK, :] + jnp.concatenate(upd, axis=1)
    return _group_rmsnorm(y * gz, norm_w), new_state


def _ssd_prompt_body(gz_ref, xs_ref, b_ref, c_ref, dt_ref, alog_ref, dskip_ref, nw_ref, e_ref,
                     y_ref, st_ref, state_sc):
    c = pl.program_id(1)

    @pl.when(c == 0)
    def _():
        state_sc[...] = jnp.zeros_like(state_sc)

    a_neg = -jnp.exp(alog_ref[...])
    state = state_sc[...]
    for j in range(CHUNKS_PER_STEP):
        r = slice(j * CHUNK, (j + 1) * CHUNK)
        y, state = _ssd_chunk(gz_ref[r, :], xs_ref[r, :], b_ref[r, :], c_ref[r, :], dt_ref[r, :], a_neg,
                              dskip_ref[...], nw_ref[...], e_ref[...], state)
        y_ref[r, :] = y.astype(y_ref.dtype)
    state_sc[...] = state

    @pl.when(c == pl.num_programs(1) - 1)
    def _():
        st_ref[0] = state.T


def _ssd_prompt(gz, xs, b, c, dt, alog, dskip, norm_w, e_mat, batch, seq):
    step_rows = CHUNKS_PER_STEP * CHUNK
    nc = seq // step_rows
    row = lambda w: pl.BlockSpec((step_rows, w), lambda b, c: (b * nc + c, 0))
    in_specs = [row(SSM_WIDTH), row(SSM_WIDTH), row(BC_WIDTH), row(BC_WIDTH), row(LANES),
                _const_spec((1, LANES)), _const_spec((1, SSM_WIDTH)), _const_spec((1, SSM_WIDTH)),
                _const_spec((LANES, SSM_WIDTH))]
    out_specs = [row(SSM_WIDTH), pl.BlockSpec((1, SSM_WIDTH, SSM_STATE), lambda b, c: (b, 0, 0))]
    out_shape = [jax.ShapeDtypeStruct((batch * seq, SSM_WIDTH), BF16),
                 jax.ShapeDtypeStruct((batch, SSM_WIDTH, SSM_STATE), F32)]
    scratch = [pltpu.VMEM((SSM_STATE, SSM_WIDTH), F32)]
    return pl.pallas_call(
        _ssd_prompt_body, grid=(batch, nc), in_specs=in_specs, out_specs=out_specs, out_shape=out_shape,
        scratch_shapes=scratch, compiler_params=_params("arbitrary", "arbitrary"), name="ssd_prompt")(
            gz, xs, b, c, dt, alog, dskip, norm_w, e_mat)


def _rel_bucket_np(dist):
    max_exact = REL_BUCKETS // 2
    d_f = np.maximum(dist, 1).astype(np.float32)
    large = max_exact + (np.log(d_f / np.float32(max_exact)) / np.float32(math.log(REL_MAX_DIST / max_exact))
                         * np.float32(REL_BUCKETS - max_exact)).astype(np.int32)
    return np.where(dist < max_exact, dist, np.minimum(large, REL_BUCKETS - 1)).astype(np.int32)


def _bucket_or_masked(dist, extra_mask=None):
    ok = (dist >= 0) & (dist <= WINDOW)
    if extra_mask is not None:
        ok = ok & extra_mask
    return np.where(ok, _rel_bucket_np(np.clip(dist, 0, WINDOW)), -1).astype(np.int32)


def _bias_body(rel_ref, bucket_ref, o_ref):
    bucket = bucket_ref[0]

    def per_head(h, carry):
        acc = jnp.full(bucket.shape, NEG, F32)
        for bkt in range(REL_BUCKETS):
            acc = jnp.where(bucket == bkt, rel_ref[bkt * ATTN_HEADS + h], acc)
        o_ref[0, h] = acc
        return carry

    lax.fori_loop(0, ATTN_HEADS, per_head, 0)


def _bias_tables(rel_flat, buckets):
    nv, lq, lk = buckets.shape
    return pl.pallas_call(
        _bias_body, grid=(nv,),
        in_specs=[pl.BlockSpec(memory_space=pltpu.SMEM), pl.BlockSpec((1, lq, lk), lambda v: (v, 0, 0))],
        out_specs=pl.BlockSpec((1, ATTN_HEADS, lq, lk), lambda v: (v, 0, 0, 0)),
        out_shape=jax.ShapeDtypeStruct((nv, ATTN_HEADS, lq, lk), F32),
        compiler_params=_params("arbitrary"), name="rel_bias")(rel_flat, buckets)


def _bias_t_body(rel_ref, bucket_ref, o_ref):
    bucket = bucket_ref[0]
    lq = bucket.shape[1]

    def per_kv_head(n, carry):
        for g in range(Q_PER_KV):
            acc = jnp.full(bucket.shape, NEG, F32)
            for bkt in range(REL_BUCKETS):
                acc = jnp.where(bucket == bkt, rel_ref[bkt * ATTN_HEADS + n * Q_PER_KV + g], acc)
            o_ref[0, n, :, g * lq:(g + 1) * lq] = acc
        return carry

    lax.fori_loop(0, ATTN_KV_HEADS, per_kv_head, 0)


def _bias_tables_t(rel_flat, buckets_t):
    nv, lk, lq = buckets_t.shape
    out_dims = (ATTN_KV_HEADS, lk, Q_PER_KV * lq)
    return pl.pallas_call(
        _bias_t_body, grid=(nv,),
        in_specs=[pl.BlockSpec(memory_space=pltpu.SMEM), pl.BlockSpec((1, lk, lq), lambda v: (v, 0, 0))],
        out_specs=pl.BlockSpec((1,) + out_dims, lambda v: (v, 0, 0, 0)),
        out_shape=jax.ShapeDtypeStruct((nv,) + out_dims, F32),
        compiler_params=_params("arbitrary"), name="rel_bias_t")(rel_flat, buckets_t)


def _head_rmsnorm(x, g_mat, e_mat, w):
    ms = _dot_sel(x * x, g_mat) * (1.0 / ATTN_HEAD_DIM)
    return x * _dot_sel(lax.rsqrt(ms + EPS), e_mat) * w


def _lane_head(shape):
    return lax.broadcasted_iota(jnp.int32, shape, 1) // ATTN_HEAD_DIM


def _sink_column(sink_ref, n, rows_per_head):
    return jnp.concatenate(
        [jnp.full((rows_per_head, 1), sink_ref[n * Q_PER_KV + g], F32) for g in range(Q_PER_KV)], axis=0)


def _softmax_with_sink(s, sink):
    m = jnp.maximum(jnp.max(s, axis=-1, keepdims=True), sink)
    p = jnp.exp(s - m)
    denom = jnp.sum(p, axis=-1, keepdims=True) + jnp.exp(sink - m)
    return p / denom


def _attn_prompt_body(qt_ref, k_ref, v_ref, gzt_ref, qwt_ref, kw_ref, g_ref, e_ref, bias_ref, sink_ref,
                      yt_ref, kn_ref, vn_ref, kcat_sc, vcat_t_sc):
    T = CHUNK
    step = pl.program_id(1)

    @pl.when(step == 0)
    def _():
        kcat_sc[0:T, :] = jnp.zeros((T, KV_WIDTH), BF16)
        vcat_t_sc[:, 0:T] = jnp.zeros((KV_WIDTH, T), BF16)

    cols_step = CHUNKS_PER_STEP * T
    q3 = qt_ref[...].reshape(ATTN_HEADS, ATTN_HEAD_DIM, cols_step)
    ms = jnp.mean(q3 * q3, axis=1, keepdims=True)
    qn = ((q3 * lax.rsqrt(ms + EPS)).reshape(ATTN_WIDTH, cols_step) * qwt_ref[...]).astype(BF16)

    kn = _head_rmsnorm(k_ref[...], g_ref[...], e_ref[...], kw_ref[...])
    v = v_ref[...]
    kn_ref[0] = kn[cols_step - T:]
    vn_ref[0] = v[cols_step - T:]
    kcat_sc[T:, :] = kn.astype(BF16)
    vcat_t_sc[:, T:] = v.T.astype(BF16)

    lane_head = _lane_head((2 * T, KV_WIDTH))
    row_head = lax.broadcasted_iota(jnp.int32, (KV_WIDTH, 2 * T), 0) // ATTN_HEAD_DIM
    zero = jnp.zeros((2 * T, KV_WIDTH), BF16)
    first_variant = jnp.minimum(step, 1)
    for j in range(CHUNKS_PER_STEP):
        kcat = kcat_sc[j * T:(j + 2) * T, :]
        vcat_t = vcat_t_sc[:, j * T:(j + 2) * T]
        variant = first_variant if j == 0 else 1
        q_blk = qn[:, j * T:(j + 1) * T]
        q_cols = jnp.concatenate([q_blk[g * KV_WIDTH:(g + 1) * KV_WIDTH] for g in range(Q_PER_KV)], axis=1)
        probs, vals = [], []
        for n in range(ATTN_KV_HEADS):
            s = _dot(jnp.where(lane_head == n, kcat, zero), q_cols)
            sink = sink_ref[n]
            cols = []
            for g in range(Q_PER_KV):
                c = slice(g * T, (g + 1) * T)
                sg = s[:, c] + bias_ref[variant, n, :, c]
                m = jnp.maximum(jnp.max(sg, axis=0, keepdims=True), sink[:, c])
                p = jnp.exp(sg - m)
                denom = jnp.sum(p, axis=0, keepdims=True) + jnp.exp(sink[:, c] - m)
                cols.append((p * (1.0 / denom)).astype(BF16))
            probs.append(jnp.concatenate(cols, axis=1))
            vals.append(jnp.where(row_head == n, vcat_t, zero.T))
        o_t = _dot(jnp.concatenate(vals, axis=1), jnp.concatenate(probs, axis=0))
        y_t = jnp.concatenate([o_t[:, g * T:(g + 1) * T] for g in range(Q_PER_KV)], axis=0)
        yt_ref[:, j * T:(j + 1) * T] = (y_t * gzt_ref[:, j * T:(j + 1) * T]).astype(yt_ref.dtype)
    kcat_sc[0:T, :] = kcat_sc[cols_step:, :]
    vcat_t_sc[:, 0:T] = vcat_t_sc[:, cols_step:]


def _attn_prompt(q_t, k, v, z_t, qw_t, kw, g_mat, e_mat, bias_t, sink_rows, batch, seq):
    step_rows = CHUNKS_PER_STEP * CHUNK
    nb = seq // step_rows
    row = lambda w: pl.BlockSpec((step_rows, w), lambda b, i: (b * nb + i, 0))
    col = pl.BlockSpec((ATTN_WIDTH, step_rows), lambda b, i: (0, b * nb + i))
    in_specs = [col, row(KV_WIDTH), row(KV_WIDTH), col, _const_spec((ATTN_WIDTH, step_rows)),
                _const_spec((1, KV_WIDTH)), _const_spec((KV_WIDTH, LANES)), _const_spec((LANES, KV_WIDTH)),
                pl.BlockSpec(bias_t.shape, lambda b, i: (0, 0, 0, 0), pipeline_mode=pl.Buffered(1)),
                _const_spec(sink_rows.shape)]
    kv_out = pl.BlockSpec((1, CHUNK, KV_WIDTH), lambda b, i: (b, 0, 0))
    out_specs = [col, kv_out, kv_out]
    out_shape = [jax.ShapeDtypeStruct((ATTN_WIDTH, batch * seq), BF16),
                 jax.ShapeDtypeStruct((batch, CHUNK, KV_WIDTH), F32),
                 jax.ShapeDtypeStruct((batch, CHUNK, KV_WIDTH), F32)]
    scratch = [pltpu.VMEM((step_rows + CHUNK, KV_WIDTH), BF16), pltpu.VMEM((KV_WIDTH, step_rows + CHUNK), BF16)]
    return pl.pallas_call(
        _attn_prompt_body, grid=(batch, nb), in_specs=in_specs, out_specs=out_specs, out_shape=out_shape,
        scratch_shapes=scratch, compiler_params=_params("arbitrary", "arbitrary"), name="attn_prompt")(
            q_t, k, v, z_t, qw_t, kw, g_mat, e_mat, bias_t, sink_rows)


def _ssd_sample_vec_body(z_ref, xbc_ref, dt_ref, sconv_ref, cw_ref, cb_ref, dtb_ref, alog_ref, dskip_ref,
                         gh_ref, e_ref, ypart_ref, ea_ref, xw_ref, b_ref, c_ref, cdec_ref, convnew_ref):
    steps = xbc_ref.shape[0]
    tail = CONV_WIDTH - 1
    full = [sconv_ref[j] for j in range(tail)] + [xbc_ref[l] for l in range(steps)]
    for j in range(tail):
        convnew_ref[j] = full[steps + j]
    gh = gh_ref[...]
    e_mat = e_ref[...]
    a_neg = -jnp.exp(alog_ref[...])
    xs, bm, cm, dts, acum = [], [], [], [], []
    run = None
    for l in range(steps):
        conv = cb_ref[...]
        for tap in range(CONV_WIDTH):
            conv = conv + full[l + tap] * cw_ref[tap:tap + 1, :]
        act = _silu(conv)
        xs.append(act[:, :SSM_WIDTH])
        bm.append(act[:, SSM_WIDTH:SSM_WIDTH + BC_WIDTH])
        cm.append(act[:, SSM_WIDTH + BC_WIDTH:])
        d = _softplus(dt_ref[l] + dtb_ref[...])
        dts.append(d)
        run = d * a_neg if run is None else run + d * a_neg
        acum.append(run)
        b_ref[l] = bm[l]
        c_ref[l] = cm[l]
    for l in range(steps):
        y = dskip_ref[...] * xs[l]
        for s in range(l + 1):
            cb_h = _dot_sel(cm[l] * bm[s], gh)
            coef = cb_h * jnp.exp(acum[l] - acum[s]) * dts[s]
            y = y + _dot_sel(coef, e_mat) * xs[s]
        ypart_ref[l] = y
        ea_ref[l] = _dot_sel(jnp.exp(acum[l]), e_mat)
        xw_ref[l] = xs[l] * _dot_sel(dts[l] * jnp.exp(acum[steps - 1] - acum[l]), e_mat)
    cdec_ref[...] = jnp.exp(acum[steps - 1])


def _ssd_sample_vec(z3, xbc3, dt3, sconv3, conv_w, conv_b, dtb, alog, dskip, gh_mat, e_mat):
    steps, nb = z3.shape[0], z3.shape[1]
    f = lambda *s: jax.ShapeDtypeStruct(s, F32)
    out_shape = [f(steps, nb, SSM_WIDTH), f(steps, nb, SSM_WIDTH), f(steps, nb, SSM_WIDTH),
                 f(steps, nb, BC_WIDTH), f(steps, nb, BC_WIDTH), f(nb, LANES), f(CONV_WIDTH - 1, nb, CONV_DIM)]
    return pl.pallas_call(_ssd_sample_vec_body, out_shape=out_shape,
                          compiler_params=pltpu.CompilerParams(vmem_limit_bytes=VMEM_LIMIT),
                          name="ssd_sample_vec")(
        z3, xbc3, dt3, sconv3, conv_w, conv_b, dtb, alog, dskip, gh_mat, e_mat)


def _ssd_sample_state_body(cdec_ref, st_ref, c_ref, b_ref, xw_ref, new_ref, yoff_ref):
    i = pl.program_id(0)
    heads_per_group = SSM_HEADS // SSM_GROUPS
    for j in range(SAMPLE_BATCH_BLOCK):
        st = st_ref[j]
        cb_bf = c_ref[:, j, :].astype(BF16)
        bb_bf = b_ref[:, j, :].astype(BF16)
        xw_bf = xw_ref[:, j, :].astype(BF16)
        y_parts = []
        for g in range(SSM_GROUPS):
            rows = slice(g * GROUP_WIDTH, (g + 1) * GROUP_WIDTH)
            ns = slice(g * SSM_STATE, (g + 1) * SSM_STATE)
            y_parts.append(_dot_nt(cb_bf[:, ns], st[rows].astype(BF16)))
            upd = _dot_tn(xw_bf[:, rows], bb_bf[:, ns])
            for hh in range(heads_per_group):
                h = g * heads_per_group + hh
                r = slice(h * SSM_HEAD_DIM, (h + 1) * SSM_HEAD_DIM)
                dec = cdec_ref[(i * SAMPLE_BATCH_BLOCK + j) * SSM_HEADS + h]
                new_ref[j, r, :] = st[r] * dec + upd[hh * SSM_HEAD_DIM:(hh + 1) * SSM_HEAD_DIM]
        yoff_ref[:, j, :] = jnp.concatenate(y_parts, axis=1)


def _ssd_sample_state(cdec_flat, state, c3, b3, xw3):
    steps, nb = c3.shape[0], c3.shape[1]
    bb = SAMPLE_BATCH_BLOCK
    tok = lambda w: pl.BlockSpec((steps, bb, w), lambda i: (0, i, 0))
    st_spec = pl.BlockSpec((bb, SSM_WIDTH, SSM_STATE), lambda i: (i, 0, 0))
    return pl.pallas_call(
        _ssd_sample_state_body, grid=(nb // bb,),
        in_specs=[pl.BlockSpec(memory_space=pltpu.SMEM), st_spec, tok(BC_WIDTH), tok(BC_WIDTH), tok(SSM_WIDTH)],
        out_specs=[st_spec, tok(SSM_WIDTH)],
        out_shape=[jax.ShapeDtypeStruct(state.shape, F32), jax.ShapeDtypeStruct((steps, nb, SSM_WIDTH), F32)],
        compiler_params=_params("parallel"), name="ssd_sample_state")(cdec_flat, state, c3, b3, xw3)


def _sample_finish_body(ypart_ref, yoff_ref, ea_ref, z_ref, nw_ref, q_ref, k_ref, qw_ref, kw_ref, g_ref, e_ref,
                        yssm_ref, qn_ref, kn_ref):
    y = ypart_ref[...] + yoff_ref[...] * ea_ref[...]
    yssm_ref[...] = _group_rmsnorm(y * _silu(z_ref[...]), nw_ref[...])
    g_mat = g_ref[...]
    e_mat = e_ref[...]
    qn_ref[...] = _head_rmsnorm(q_ref[...], g_mat, e_mat, qw_ref[...]) * ATTN_SCALE
    kn_ref[...] = _head_rmsnorm(k_ref[...], g_mat[:KV_WIDTH], e_mat[:, :KV_WIDTH], kw_ref[...])


def _sample_finish(ypart, yoff, ea, z, norm_w, q, k, qw, kw, g_mat, e_mat):
    rows = z.shape[0]
    f = lambda w: jax.ShapeDtypeStruct((rows, w), F32)
    return pl.pallas_call(_sample_finish_body, out_shape=[f(SSM_WIDTH), f(ATTN_WIDTH), f(KV_WIDTH)],
                          compiler_params=pltpu.CompilerParams(vmem_limit_bytes=VMEM_LIMIT),
                          name="sample_finish")(ypart, yoff, ea, z, norm_w, q, k, qw, kw, g_mat, e_mat)


def _attn_sample_body(q_ref, kn_ref, vn_ref, z_ref, ckt_ref, cvt_ref, biasc_ref, biasn_ref,
                      y_ref, kot_ref, vot_ref):
    steps = q_ref.shape[0]
    bb = SAMPLE_BATCH_BLOCK
    blk = Q_PER_KV * steps
    rows = ATTN_KV_HEADS * blk
    pad = jnp.zeros((SUBLANES - steps, KV_WIDTH), F32)
    lane_head = _lane_head((blk, KV_WIDTH))
    zero = jnp.zeros((blk, KV_WIDTH), F32)

    s_c, s_n, k8, v8 = [], [], [], []
    for j in range(bb):
        q = q_ref[:, j, :]
        qg = jnp.concatenate([q[:, g * KV_WIDTH:(g + 1) * KV_WIDTH] for g in range(Q_PER_KV)], axis=0)
        qx = jnp.concatenate([jnp.where(lane_head == n, qg, zero) for n in range(ATTN_KV_HEADS)], axis=0)
        qx = qx.astype(BF16)
        k8.append(jnp.concatenate([kn_ref[:, j, :], pad], axis=0))
        v8.append(jnp.concatenate([vn_ref[:, j, :], pad], axis=0))
        s_c.append(_dot(qx, ckt_ref[j].astype(BF16)))
        s_n.append(_dot_nt(qx, k8[j].astype(BF16)))
    s_c = jnp.concatenate(s_c, axis=0) + biasc_ref[...]
    s_n = jnp.concatenate(s_n, axis=0) + biasn_ref[...]
    m = jnp.maximum(jnp.max(s_c, axis=-1, keepdims=True), jnp.max(s_n, axis=-1, keepdims=True))
    p_c = jnp.exp(s_c - m)
    p_n = jnp.exp(s_n - m)
    inv = 1.0 / (jnp.sum(p_c, axis=-1, keepdims=True) + jnp.sum(p_n, axis=-1, keepdims=True))
    p_c = (p_c * inv).astype(BF16)
    p_n = (p_n * inv).astype(BF16)

    lane = lax.broadcasted_iota(jnp.int32, (KV_WIDTH, WINDOW), 1)
    for j in range(bb):
        r = slice(j * rows, (j + 1) * rows)
        o = _dot_nt(p_c[r], cvt_ref[j].astype(BF16)) + _dot(p_n[r], v8[j].astype(BF16))
        og = zero
        for n in range(ATTN_KV_HEADS):
            og = og + jnp.where(lane_head == n, o[n * blk:(n + 1) * blk], zero)
        y = jnp.concatenate([og[g * steps:(g + 1) * steps] for g in range(Q_PER_KV)], axis=1)
        y_ref[:, j, :] = y * _silu(z_ref[:, j, :])

        for new8, old_ref, out_ref in ((k8[j], ckt_ref, kot_ref), (v8[j], cvt_ref, vot_ref)):
            tail_rows = jnp.concatenate([new8[steps:], new8[:steps]], axis=0)
            block = jnp.concatenate([jnp.zeros((WINDOW - SUBLANES, KV_WIDTH), F32), tail_rows], axis=0)
            shifted = pltpu.roll(old_ref[j], WINDOW - steps, axis=1)
            out_ref[j] = jnp.where(lane >= WINDOW - steps, block.T, shifted)


def _attn_sample(q3, kn3, vn3, z3, cache_kt, cache_vt, bias_c, bias_n):
    steps, nb = q3.shape[0], q3.shape[1]
    bb = SAMPLE_BATCH_BLOCK
    tok = lambda w: pl.BlockSpec((steps, bb, w), lambda i: (0, i, 0))
    cache_spec = pl.BlockSpec((bb, KV_WIDTH, WINDOW), lambda i: (i, 0, 0))
    return pl.pallas_call(
        _attn_sample_body, grid=(nb // bb,),
        in_specs=[tok(ATTN_WIDTH), tok(KV_WIDTH), tok(KV_WIDTH), tok(ATTN_WIDTH), cache_spec, cache_spec,
                  _const_spec(bias_c.shape), _const_spec(bias_n.shape)],
        out_specs=[tok(ATTN_WIDTH), cache_spec, cache_spec],
        out_shape=[jax.ShapeDtypeStruct((steps, nb, ATTN_WIDTH), F32),
                   jax.ShapeDtypeStruct(cache_kt.shape, F32), jax.ShapeDtypeStruct(cache_vt.shape, F32)],
        compiler_params=_params("parallel"), name="attn_sample")(
            q3, kn3, vn3, z3, cache_kt, cache_vt, bias_c, bias_n)


def _static_tables(steps):
    lanes = np.arange(ATTN_WIDTH)
    g_mat = np.zeros((ATTN_WIDTH, LANES), np.float32)
    g_mat[lanes, lanes // ATTN_HEAD_DIM] = 1.0
    e_mat = g_mat.T.copy()
    bc = np.arange(BC_WIDTH)
    gh_mat = np.zeros((BC_WIDTH, LANES), np.float32)
    for h in range(SSM_HEADS):
        gh_mat[bc // SSM_STATE == h // (SSM_HEADS // SSM_GROUPS), h] = 1.0
    g_i, n_i, d_i = np.meshgrid(np.arange(Q_PER_KV), np.arange(ATTN_KV_HEADS), np.arange(ATTN_HEAD_DIM),
                                indexing="ij")
    perm = ((n_i * Q_PER_KV + g_i) * ATTN_HEAD_DIM + d_i).reshape(-1)
    T = CHUNK
    dist = np.arange(T)[:, None] - (np.arange(2 * T) - T)[None, :]
    first = np.broadcast_to((np.arange(2 * T) >= T)[None, :], dist.shape)
    prompt_buckets = np.stack([_bucket_or_masked(dist, first), _bucket_or_masked(dist)])
    dist_c = (np.arange(steps) + WINDOW)[:, None] - np.arange(WINDOW)[None, :]
    dist_n = np.arange(steps)[:, None] - np.arange(SUBLANES)[None, :]
    real = np.broadcast_to((np.arange(SUBLANES) < steps)[None, :], dist_n.shape)
    return dict(g=g_mat, e=e_mat, gh=gh_mat, perm=perm, prompt_buckets=prompt_buckets,
                cache_buckets=_bucket_or_masked(dist_c)[None], new_buckets=_bucket_or_masked(dist_n, real)[None])


def kernel(x_prompt, x_sample, cache_k, cache_v, state_ssm, state_conv, norm_w, w_in, conv_w, conv_b, dt_bias,
           a_log, d_skip, ssm_norm_w, q_norm_w, k_norm_w, sinks, rel_table, w_out):
    assert w_in.shape[0] == 1, "single-layer kernel"
    batch, seq, _ = x_prompt.shape
    nb, steps, _ = x_sample.shape
    tab = _static_tables(steps)
    perm = tab["perm"]
    g_mat = jnp.asarray(tab["g"], BF16)
    e_mat = jnp.asarray(tab["e"], BF16)
    gh_mat = jnp.asarray(tab["gh"], BF16)

    w = w_in[0]
    edges = np.cumsum([0, SSM_WIDTH, CONV_DIM, SSM_HEADS, ATTN_WIDTH, KV_WIDTH, KV_WIDTH, ATTN_WIDTH])
    wz, wxbc, wdt, wq, wk, wv, wza = [w[:, a:b] for a, b in zip(edges[:-1], edges[1:])]
    wdt = jnp.pad(wdt, ((0, 0), (0, LANES - SSM_HEADS)))
    weights = [m.astype(BF16) for m in (wz, wxbc, wdt, wq[:, perm], wk, wv, wza[:, perm])]
    wo = w_out[0]
    wo_top = wo[:SSM_WIDTH].astype(BF16)
    wo_bot = wo[SSM_WIDTH:][perm].astype(BF16)

    row = lambda v, width: jnp.pad(v.reshape(1, -1), ((0, 0), (0, width - v.size)))
    nw = row(norm_w[0], D_MODEL)
    cw = conv_w[0]
    cb = row(conv_b[0], CONV_DIM)
    dtb = row(dt_bias[0], LANES)
    alog = row(a_log[0], LANES)
    dskip = jnp.repeat(d_skip[0], SSM_HEAD_DIM).reshape(1, SSM_WIDTH)
    snw = row(ssm_norm_w[0], SSM_WIDTH)
    qw = jnp.tile(q_norm_w[0], ATTN_HEADS).reshape(1, ATTN_WIDTH)
    kw = jnp.tile(k_norm_w[0], ATTN_KV_HEADS).reshape(1, KV_WIDTH)
    sink = sinks[0]
    rel_flat = rel_table.reshape(-1)

    xp = x_prompt.reshape(batch * seq, D_MODEL)
    feature_major = (False, False, False, True, False, False, True)
    weights_p = [m.T if t else m for t, m in zip(feature_major, weights)]
    gz, xs_p, b_p, c_p, dt_p, q_t, k, v, gza_t, tail_p = _inproj_prompt(xp, nw, weights_p, cw, cb, dtb, batch, seq)
    y_ssm, st_p = _ssd_prompt(gz, xs_p, b_p, c_p, dt_p, alog, dskip, snw, e_mat, batch, seq)
    bias_t = _bias_tables_t(rel_flat, jnp.asarray(tab["prompt_buckets"].transpose(0, 2, 1)))
    qw_t = jnp.broadcast_to((qw * ATTN_SCALE).reshape(ATTN_WIDTH, 1), (ATTN_WIDTH, CHUNKS_PER_STEP * CHUNK))
    sink_rows = jnp.repeat(sink.reshape(ATTN_KV_HEADS, Q_PER_KV), CHUNK, axis=1).reshape(ATTN_KV_HEADS, 1, -1)
    y_attn_t, k_p, v_p = _attn_prompt(q_t, k, v, gza_t, qw_t, kw, g_mat[:KV_WIDTH], e_mat[:, :KV_WIDTH], bias_t,
                                      sink_rows, batch, seq)
    y_p = _outproj(y_ssm, y_attn_t, xp, wo_top, wo_bot, True).reshape(batch, seq, D_MODEL)
    conv_p = tail_p[:, SUBLANES - (CONV_WIDTH - 1):, :]

    xs = jnp.swapaxes(x_sample, 0, 1).reshape(steps * nb, D_MODEL)
    z, xbc, dt, q, k, v, za = _inproj(xs, nw, weights)
    t3 = lambda a: a.reshape(steps, nb, a.shape[-1])
    sconv3 = jnp.swapaxes(state_conv[0], 0, 1)
    ypart, ea, xw, b3, c3, cdec, conv_s3 = _ssd_sample_vec(
        t3(z), t3(xbc), t3(dt), sconv3, cw, cb, dtb, alog, dskip, gh_mat, e_mat)
    state_in = state_ssm[0].reshape(nb, SSM_WIDTH, SSM_STATE)
    st_s, yoff = _ssd_sample_state(cdec[:, :SSM_HEADS].reshape(-1), state_in, c3, b3, xw)
    f2 = lambda a: a.reshape(steps * nb, a.shape[-1])
    y_ssm, qn, kn = _sample_finish(f2(ypart), f2(yoff), f2(ea), z, snw, q, k, qw, kw, g_mat, e_mat)
    bias_c = _bias_tables(rel_flat, jnp.asarray(tab["cache_buckets"])).reshape(ATTN_HEADS * steps, WINDOW)
    bias_n = _bias_tables(rel_flat, jnp.asarray(tab["new_buckets"])).reshape(ATTN_HEADS * steps, SUBLANES)
    bias_n = bias_n.at[:, steps].set(jnp.repeat(sink, steps))
    bias_c = jnp.tile(bias_c, (SAMPLE_BATCH_BLOCK, 1))
    bias_n = jnp.tile(bias_n, (SAMPLE_BATCH_BLOCK, 1))
    to_t = lambda a: jnp.transpose(a[0], (0, 2, 3, 1)).reshape(nb, KV_WIDTH, WINDOW)
    from_t = lambda a: jnp.transpose(a.reshape(nb, ATTN_KV_HEADS, ATTN_HEAD_DIM, WINDOW), (0, 3, 1, 2))[None]
    y_attn3, k_st, v_st = _attn_sample(t3(qn), t3(kn), t3(v), t3(za), to_t(cache_k), to_t(cache_v),
                                       bias_c, bias_n)
    k_s, v_s = from_t(k_st), from_t(v_st)
    y_s = _outproj(y_ssm, f2(y_attn3), xs, wo_top, wo_bot, False)
    y_s = jnp.swapaxes(y_s.reshape(steps, nb, D_MODEL), 0, 1)

    kv5 = lambda a: a.reshape(1, a.shape[0], WINDOW, ATTN_KV_HEADS, ATTN_HEAD_DIM)
    st5 = lambda a: a.reshape(1, a.shape[0], SSM_HEADS, SSM_HEAD_DIM, SSM_STATE)
    return (y_p, y_s, kv5(k_p), kv5(v_p), st5(st_p), conv_p[None],
            k_s, v_s, st5(st_s), jnp.swapaxes(conv_s3, 0, 1)[None])
```

```python
import functools
import math

import numpy as np
import jax
import jax.numpy as jnp
from jax import lax
from jax.experimental import pallas as pl
from jax.experimental.pallas import tpu as pltpu

F32 = jnp.float32
BF16 = jnp.bfloat16

D_MODEL = 1024
SSM_HEADS = 16
SSM_HEAD_DIM = 64
SSM_WIDTH = SSM_HEADS * SSM_HEAD_DIM
SSM_GROUPS = 2
SSM_STATE = 128
GROUP_WIDTH = SSM_WIDTH // SSM_GROUPS
BC_WIDTH = SSM_GROUPS * SSM_STATE
CONV_WIDTH = 4
CONV_DIM = SSM_WIDTH + 2 * BC_WIDTH
CHUNK = 128
ATTN_HEADS = 16
ATTN_KV_HEADS = 4
Q_PER_KV = ATTN_HEADS // ATTN_KV_HEADS
ATTN_HEAD_DIM = 64
ATTN_WIDTH = ATTN_HEADS * ATTN_HEAD_DIM
KV_WIDTH = ATTN_KV_HEADS * ATTN_HEAD_DIM
WINDOW = 128
ATTN_SCALE = ATTN_HEAD_DIM ** -0.5
REL_BUCKETS = 32
REL_MAX_DIST = 128
EPS = 1e-6
LOG2E = 1.0 / math.log(2.0)
NEG = -1e30

LANES = 128
SUBLANES = 8
MXU_WIDTH = 256
VMEM_LIMIT = 56 * 1024 * 1024
def _in_proj_columns():
    widths = (("z", SSM_WIDTH), ("xbc", CONV_DIM), ("dt", LANES), ("q", ATTN_WIDTH), ("k", KV_WIDTH),
              ("v", KV_WIDTH), ("za", ATTN_WIDTH))
    cols, start = {}, 0
    for name, width in widths:
        cols[name] = slice(start, start + width)
        start += width
    return cols, start


IN_COLS, IN_WIDTH_PADDED = _in_proj_columns()

PROJ_ROWS = 512
CHUNKS_PER_STEP = 4
SAMPLE_BATCH_BLOCK = 8


def _dot(a, b):
    return jnp.dot(a, b, preferred_element_type=F32)


def _dot_nt(a, b):
    return lax.dot_general(a, b, (((1,), (1,)), ((), ())), preferred_element_type=F32)


def _dot_tn(a, b):
    return lax.dot_general(a, b, (((0,), (0,)), ((), ())), preferred_element_type=F32)


def _split2(v):
    hi = v.astype(BF16)
    lo = (v - hi.astype(F32)).astype(BF16)
    return hi, lo


def _dot_sel(v, m):
    hi, lo = _split2(v)
    return _dot(hi, m) + _dot(lo, m)


def _dot_sel3(m, v):
    hi = v.astype(BF16)
    r1 = v - hi.astype(F32)
    mid = r1.astype(BF16)
    lo = (r1 - mid.astype(F32)).astype(BF16)
    return _dot(m, hi) + _dot(m, mid) + _dot(m, lo)


def _silu(x):
    return x / (1.0 + jnp.exp(-x))


def _softplus(x):
    return jnp.maximum(x, 0.0) + jnp.log1p(jnp.exp(-jnp.abs(x)))


def _params(*sem):
    return pltpu.CompilerParams(dimension_semantics=sem, vmem_limit_bytes=VMEM_LIMIT)


def _const_spec(shape):
    nd = len(shape)
    return pl.BlockSpec(shape, lambda *_: (0,) * nd)


def _inproj_body(x_ref, nw_ref, w_ref, *out_refs):
    x = x_ref[...]
    ms = jnp.mean(x * x, axis=-1, keepdims=True)
    h = (x * lax.rsqrt(ms + EPS) * nw_ref[...]).astype(BF16)
    for cols, o_ref in zip(IN_COLS.values(), out_refs):
        o_ref[...] = _dot(h, w_ref[:, cols])


def _inproj(x2d, norm_w, w_all):
    rows = x2d.shape[0]
    tm = min(PROJ_ROWS, rows)
    widths = [c.stop - c.start for c in IN_COLS.values()]
    in_specs = [pl.BlockSpec((tm, D_MODEL), lambda i: (i, 0)), _const_spec((1, D_MODEL)),
                pl.BlockSpec(w_all.shape, lambda i: (0, 0), pipeline_mode=pl.Buffered(1))]
    out_specs = [pl.BlockSpec((tm, w), lambda i: (i, 0)) for w in widths]
    out_shape = [jax.ShapeDtypeStruct((rows, w), F32) for w in widths]
    return pl.pallas_call(
        _inproj_body, grid=(rows // tm,), in_specs=in_specs, out_specs=out_specs, out_shape=out_shape,
        compiler_params=_params("parallel"), name="inproj")(x2d, norm_w, w_all)


def _shift_rows(u, prev_tail, k):
    rows, width = u.shape
    tiles = jnp.concatenate([prev_tail, u], axis=0).reshape(rows // SUBLANES + 1, SUBLANES, width)
    rot = pltpu.roll(tiles, k, axis=1)
    first = lax.broadcasted_iota(jnp.int32, (1, SUBLANES, width), 1) < k
    return jnp.where(first, rot[:-1], rot[1:]).reshape(rows, width)


def _inproj_prompt_body(steps_per_seq, x_ref, nw_ref, w_ref, wqt_ref, wzat_ref, cw_ref, cb_ref, dtb_ref,
                        gz_ref, xs_ref, b_ref, c_ref, dt_ref, qt_ref, k_ref, v_ref, gzat_ref, tail_ref, tail_sc):
    @pl.when(pl.program_id(0) % steps_per_seq == 0)
    def _():
        tail_sc[...] = jnp.zeros_like(tail_sc)

    x = x_ref[...]
    ms = jnp.mean(x * x, axis=-1, keepdims=True)
    h = (x * lax.rsqrt(ms + EPS) * nw_ref[...]).astype(BF16)
    rows = x.shape[0]
    col0 = {name: c.start for name, c in IN_COLS.items()}
    for j in range(CONV_DIM // MXU_WIDTH):
        cols = slice(j * MXU_WIDTH, (j + 1) * MXU_WIDTH)
        u = _dot(h, w_ref[:, col0["xbc"] + cols.start:col0["xbc"] + cols.stop])
        prev_tail = tail_sc[:, cols]
        conv = cb_ref[:, cols] + u * cw_ref[CONV_WIDTH - 1:CONV_WIDTH, cols]
        for k in range(1, CONV_WIDTH):
            tap = CONV_WIDTH - 1 - k
            conv = conv + _shift_rows(u, prev_tail, k) * cw_ref[tap:tap + 1, cols]
        new_tail = u[rows - SUBLANES:, :]
        tail_sc[:, cols] = new_tail
        tail_ref[0, :, cols] = new_tail
        act = _silu(conv)
        if j < SSM_WIDTH // MXU_WIDTH:
            xs_ref[:, cols] = act
        elif j == SSM_WIDTH // MXU_WIDTH:
            b_ref[...] = act.astype(BF16)
        else:
            c_ref[...] = act.astype(BF16)

    for j in range(SSM_WIDTH // MXU_WIDTH):
        cols = slice(j * MXU_WIDTH, (j + 1) * MXU_WIDTH)
        gz_ref[:, cols] = _silu(_dot(h, w_ref[:, col0["z"] + cols.start:col0["z"] + cols.stop]))
    for j in range(ATTN_WIDTH // MXU_WIDTH):
        feats = slice(j * MXU_WIDTH, (j + 1) * MXU_WIDTH)
        gzat_ref[feats, :] = _silu(_dot_nt(wzat_ref[feats, :], h))
    dt_ref[...] = _softplus(_dot(h, w_ref[:, IN_COLS["dt"]]) + dtb_ref[...])
    qt_ref[...] = _dot_nt(wqt_ref[...], h)
    k_ref[...] = _dot(h, w_ref[:, IN_COLS["k"]])
    v_ref[...] = _dot(h, w_ref[:, IN_COLS["v"]])


def _inproj_prompt(x2d, norm_w, w_all, wq_t, wza_t, conv_w, conv_b, dtb, batch, seq):
    rows = x2d.shape[0]
    tm = PROJ_ROWS
    steps_per_seq = seq // tm
    resident = lambda a: pl.BlockSpec(a.shape, lambda i: (0, 0), pipeline_mode=pl.Buffered(1))
    rowblk = lambda w: pl.BlockSpec((tm, w), lambda i: (i, 0))
    colblk = pl.BlockSpec((ATTN_WIDTH, tm), lambda i: (0, i))
    in_specs = ([rowblk(D_MODEL), _const_spec((1, D_MODEL)), resident(w_all), resident(wq_t), resident(wza_t)]
                + [_const_spec(conv_w.shape), _const_spec(conv_b.shape), _const_spec(dtb.shape)])
    out_specs = [rowblk(SSM_WIDTH), rowblk(SSM_WIDTH), rowblk(BC_WIDTH), rowblk(BC_WIDTH), rowblk(LANES),
                 colblk, rowblk(KV_WIDTH), rowblk(KV_WIDTH), colblk,
                 pl.BlockSpec((1, SUBLANES, CONV_DIM), lambda i: (i // steps_per_seq, 0, 0))]
    f = lambda r, c, dt=F32: jax.ShapeDtypeStruct((r, c), dt)
    out_shape = [f(rows, SSM_WIDTH), f(rows, SSM_WIDTH), f(rows, BC_WIDTH, BF16), f(rows, BC_WIDTH, BF16),
                 f(rows, LANES), f(ATTN_WIDTH, rows), f(rows, KV_WIDTH), f(rows, KV_WIDTH), f(ATTN_WIDTH, rows),
                 jax.ShapeDtypeStruct((batch, SUBLANES, CONV_DIM), F32)]
    return pl.pallas_call(
        functools.partial(_inproj_prompt_body, steps_per_seq), grid=(rows // tm,), in_specs=in_specs,
        out_specs=out_specs, out_shape=out_shape, scratch_shapes=[pltpu.VMEM((SUBLANES, CONV_DIM), F32)],
        compiler_params=_params("arbitrary"), name="inproj_prompt")(
            x2d, norm_w, w_all, wq_t, wza_t, conv_w, conv_b, dtb)


def _outproj_body(attn_transposed, ys_ref, ya_ref, x_ref, w_ref, o_ref):
    ya = ya_ref[...].astype(BF16)
    w_attn = w_ref[SSM_WIDTH:, :]
    attn = _dot_tn(ya, w_attn) if attn_transposed else _dot(ya, w_attn)
    o_ref[...] = x_ref[...] + _dot(ys_ref[...].astype(BF16), w_ref[:SSM_WIDTH, :]) + attn


def _outproj(y_ssm, y_attn, x2d, w_out, attn_transposed):
    rows = x2d.shape[0]
    tm = min(PROJ_ROWS, rows)
    row_spec = pl.BlockSpec((tm, D_MODEL), lambda i: (i, 0))
    ya_spec = pl.BlockSpec((ATTN_WIDTH, tm), lambda i: (0, i)) if attn_transposed else row_spec
    w_spec = pl.BlockSpec(w_out.shape, lambda i: (0, 0), pipeline_mode=pl.Buffered(1))
    return pl.pallas_call(
        functools.partial(_outproj_body, attn_transposed), grid=(rows // tm,),
        in_specs=[row_spec, ya_spec, row_spec, w_spec],
        out_specs=row_spec, out_shape=jax.ShapeDtypeStruct((rows, D_MODEL), F32),
        compiler_params=_params("parallel"), name="outproj")(y_ssm, y_attn, x2d, w_out)


def _group_rmsnorm(gy, norm_w):
    parts = []
    for g in range(SSM_GROUPS):
        blk = gy[:, g * GROUP_WIDTH:(g + 1) * GROUP_WIDTH]
        ms = jnp.mean(blk * blk, axis=-1, keepdims=True)
        parts.append(blk * lax.rsqrt(ms + EPS))
    return jnp.concatenate(parts, axis=1) * norm_w


def _ssd_chunk(gz, xs, b_bf, c_bf, dt, a_neg, dskip, norm_w, e_mat, state):
    xs_bf = xs.astype(BF16)

    a = dt * a_neg
    li = lax.broadcasted_iota(jnp.int32, (CHUNK, CHUNK), 0)
    si = lax.broadcasted_iota(jnp.int32, (CHUNK, CHUNK), 1)
    causal = li >= si
    a_cum = _dot_sel3(jnp.where(causal, 1.0, 0.0).astype(BF16), a)
    a2 = a_cum * LOG2E
    row_term = a2.T - jnp.log2(dt.T)
    ea_full = _dot_sel(jnp.exp(a_cum), e_mat)
    w_full = _dot_sel(dt * jnp.exp(a_cum[CHUNK - 1:CHUNK, :] - a_cum), e_mat)

    cb = [_dot_nt(c_bf[:, g * SSM_STATE:(g + 1) * SSM_STATE], b_bf[:, g * SSM_STATE:(g + 1) * SSM_STATE])
          for g in range(SSM_GROUPS)]
    half = lax.broadcasted_iota(jnp.int32, (CHUNK, LANES), 1) < SSM_HEAD_DIM
    heads_per_group = SSM_HEADS // SSM_GROUPS
    y_parts = []
    for pair in range(SSM_HEADS // 2):
        blocks = []
        for h in (2 * pair, 2 * pair + 1):
            seg = a2[:, h:h + 1] - row_term[h:h + 1, :]
            decay_dt = jnp.exp2(jnp.where(causal, seg, -jnp.inf))
            blocks.append((cb[h // heads_per_group] * decay_dt).astype(BF16))
        lhs = jnp.concatenate(blocks, axis=1)
        xp = xs_bf[:, pair * LANES:(pair + 1) * LANES]
        zero = jnp.zeros_like(xp)
        rhs = jnp.concatenate([jnp.where(half, xp, zero), jnp.where(half, zero, xp)], axis=0)
        y_parts.append(_dot(lhs, rhs))
    y_diag = jnp.concatenate(y_parts, axis=1)

    state_bf = state.astype(BF16)
    xw_bf = (xs * w_full).astype(BF16)
    y_off, upd = [], []
    for g in range(SSM_GROUPS):
        cols = slice(g * GROUP_WIDTH, (g + 1) * GROUP_WIDTH)
        ns = slice(g * SSM_STATE, (g + 1) * SSM_STATE)
        y_off.append(_dot(c_bf[:, ns], state_bf[:, cols]))
        upd.append(_dot_tn(b_bf[:, ns], xw_bf[:, cols]))
    y = y_diag + jnp.concatenate(y_off, axis=1) * ea_full + dskip * xs
    new_state = state * ea_full[CHUNK - 1:CHUNK, :] + jnp.concatenate(upd, axis=1)
    return _group_rmsnorm(y * gz, norm_w), new_state


def _ssd_prompt_body(gz_ref, xs_ref, b_ref, c_ref, dt_ref, alog_ref, dskip_ref, nw_ref, e_ref,
                     y_ref, st_ref, state_sc):
    c = pl.program_id(1)

    @pl.when(c == 0)
    def _():
        state_sc[...] = jnp.zeros_like(state_sc)

    a_neg = -jnp.exp(alog_ref[...])
    state = state_sc[...]
    for j in range(CHUNKS_PER_STEP):
        r = slice(j * CHUNK, (j + 1) * CHUNK)
        y, state = _ssd_chunk(gz_ref[r, :], xs_ref[r, :], b_ref[r, :], c_ref[r, :], dt_ref[r, :], a_neg,
                              dskip_ref[...], nw_ref[...], e_ref[...], state)
        y_ref[r, :] = y.astype(y_ref.dtype)
    state_sc[...] = state

    @pl.when(c == pl.num_programs(1) - 1)
    def _():
        st_ref[0] = state.T


def _ssd_prompt(gz, xs, b, c, dt, alog, dskip, norm_w, e_mat, batch, seq):
    step_rows = CHUNKS_PER_STEP * CHUNK
    nc = seq // step_rows
    row = lambda w: pl.BlockSpec((step_rows, w), lambda b, c: (b * nc + c, 0))
    in_specs = [row(SSM_WIDTH), row(SSM_WIDTH), row(BC_WIDTH), row(BC_WIDTH), row(LANES),
                _const_spec((1, LANES)), _const_spec((1, SSM_WIDTH)), _const_spec((1, SSM_WIDTH)),
                _const_spec((LANES, SSM_WIDTH))]
    out_specs = [row(SSM_WIDTH), pl.BlockSpec((1, SSM_WIDTH, SSM_STATE), lambda b, c: (b, 0, 0))]
    out_shape = [jax.ShapeDtypeStruct((batch * seq, SSM_WIDTH), BF16),
                 jax.ShapeDtypeStruct((batch, SSM_WIDTH, SSM_STATE), F32)]
    scratch = [pltpu.VMEM((SSM_STATE, SSM_WIDTH), F32)]
    return pl.pallas_call(
        _ssd_prompt_body, grid=(batch, nc), in_specs=in_specs, out_specs=out_specs, out_shape=out_shape,
        scratch_shapes=scratch, compiler_params=_params("arbitrary", "arbitrary"), name="ssd_prompt")(
            gz, xs, b, c, dt, alog, dskip, norm_w, e_mat)


def _rel_bucket_np(dist):
    max_exact = REL_BUCKETS // 2
    d_f = np.maximum(dist, 1).astype(np.float32)
    large = max_exact + (np.log(d_f / np.float32(max_exact)) / np.float32(math.log(REL_MAX_DIST / max_exact))
                         * np.float32(REL_BUCKETS - max_exact)).astype(np.int32)
    return np.where(dist < max_exact, dist, np.minimum(large, REL_BUCKETS - 1)).astype(np.int32)


def _bucket_or_masked(dist, extra_mask=None):
    ok = (dist >= 0) & (dist <= WINDOW)
    if extra_mask is not None:
        ok = ok & extra_mask
    return np.where(ok, _rel_bucket_np(np.clip(dist, 0, WINDOW)), -1).astype(np.int32)


def _bias_body(rel_ref, bucket_ref, o_ref):
    bucket = bucket_ref[0]

    def per_head(h, carry):
        acc = jnp.full(bucket.shape, NEG, F32)
        for bkt in range(REL_BUCKETS):
            acc = jnp.where(bucket == bkt, rel_ref[bkt * ATTN_HEADS + h], acc)
        o_ref[0, h] = acc
        return carry

    lax.fori_loop(0, ATTN_HEADS, per_head, 0)


def _bias_tables(rel_flat, buckets):
    nv, lq, lk = buckets.shape
    return pl.pallas_call(
        _bias_body, grid=(nv,),
        in_specs=[pl.BlockSpec(memory_space=pltpu.SMEM), pl.BlockSpec((1, lq, lk), lambda v: (v, 0, 0))],
        out_specs=pl.BlockSpec((1, ATTN_HEADS, lq, lk), lambda v: (v, 0, 0, 0)),
        out_shape=jax.ShapeDtypeStruct((nv, ATTN_HEADS, lq, lk), F32),
        compiler_params=_params("arbitrary"), name="rel_bias")(rel_flat, buckets)


def _bias_t_body(rel_ref, bucket_ref, o_ref):
    variants = [bucket_ref[v] for v in range(bucket_ref.shape[0])]
    union = functools.reduce(jnp.maximum, variants)
    lq = union.shape[1]

    def per_kv_head(n, carry):
        for g in range(Q_PER_KV):
            acc = jnp.full(union.shape, NEG, F32)
            for bkt in range(REL_BUCKETS):
                acc = jnp.where(union == bkt, rel_ref[bkt * ATTN_HEADS + n * Q_PER_KV + g], acc)
            for v, bucket in enumerate(variants):
                o_ref[v, n, :, g * lq:(g + 1) * lq] = jnp.where(bucket >= 0, acc, NEG)
        return carry

    lax.fori_loop(0, ATTN_KV_HEADS, per_kv_head, 0)


def _bias_tables_t(rel_flat, buckets_t):
    nv, lk, lq = buckets_t.shape
    out_dims = (nv, ATTN_KV_HEADS, lk, Q_PER_KV * lq)
    return pl.pallas_call(
        _bias_t_body,
        in_specs=[pl.BlockSpec(memory_space=pltpu.SMEM), pl.BlockSpec(memory_space=pltpu.VMEM)],
        out_specs=pl.BlockSpec(memory_space=pltpu.VMEM),
        out_shape=jax.ShapeDtypeStruct(out_dims, F32),
        compiler_params=pltpu.CompilerParams(vmem_limit_bytes=VMEM_LIMIT), name="rel_bias_t")(rel_flat, buckets_t)


def _head_rmsnorm(x, g_mat, e_mat, w):
    ms = _dot_sel(x * x, g_mat) * (1.0 / ATTN_HEAD_DIM)
    return x * _dot_sel(lax.rsqrt(ms + EPS), e_mat) * w


def _lane_head(shape):
    return lax.broadcasted_iota(jnp.int32, shape, 1) // ATTN_HEAD_DIM


def _sink_column(sink_ref, n, rows_per_head):
    return jnp.concatenate(
        [jnp.full((rows_per_head, 1), sink_ref[n * Q_PER_KV + g], F32) for g in range(Q_PER_KV)], axis=0)


def _softmax_with_sink(s, sink):
    m = jnp.maximum(jnp.max(s, axis=-1, keepdims=True), sink)
    p = jnp.exp(s - m)
    denom = jnp.sum(p, axis=-1, keepdims=True) + jnp.exp(sink - m)
    return p / denom


def _attn_prompt_body(qt_ref, k_ref, v_ref, gzt_ref, qwt_ref, kw_ref, g_ref, e_ref, bias_ref, sink_ref,
                      yt_ref, kn_ref, vn_ref, kcat_sc, vcat_t_sc):
    T = CHUNK
    step = pl.program_id(1)

    @pl.when(step == 0)
    def _():
        kcat_sc[0:T, :] = jnp.zeros((T, KV_WIDTH), BF16)
        vcat_t_sc[:, 0:T] = jnp.zeros((KV_WIDTH, T), BF16)

    cols_step = CHUNKS_PER_STEP * T
    q3 = qt_ref[...].reshape(ATTN_HEADS, ATTN_HEAD_DIM, cols_step)
    ms = jnp.mean(q3 * q3, axis=1, keepdims=True)
    qn = ((q3 * lax.rsqrt(ms + EPS)).reshape(ATTN_WIDTH, cols_step) * qwt_ref[...]).astype(BF16)

    kn = _head_rmsnorm(k_ref[...], g_ref[...], e_ref[...], kw_ref[...])
    v = v_ref[...]
    kn_ref[0] = kn[cols_step - T:]
    vn_ref[0] = v[cols_step - T:]
    kcat_sc[T:, :] = kn.astype(BF16)
    vcat_t_sc[:, T:] = v.T.astype(BF16)

    lane_head = _lane_head((2 * T, KV_WIDTH))
    row_head = lax.broadcasted_iota(jnp.int32, (KV_WIDTH, 2 * T), 0) // ATTN_HEAD_DIM
    zero = jnp.zeros((2 * T, KV_WIDTH), BF16)
    first_variant = jnp.minimum(step, 1)
    for j in range(CHUNKS_PER_STEP):
        kcat = kcat_sc[j * T:(j + 2) * T, :]
        vcat_t = vcat_t_sc[:, j * T:(j + 2) * T]
        variant = first_variant if j == 0 else 1
        q_blk = qn[:, j * T:(j + 1) * T]
        head = lambda h: q_blk[h * ATTN_HEAD_DIM:(h + 1) * ATTN_HEAD_DIM]
        q_cols = jnp.concatenate(
            [jnp.concatenate([head(n * Q_PER_KV + g) for n in range(ATTN_KV_HEADS)], axis=0)
             for g in range(Q_PER_KV)], axis=1)
        probs, vals = [], []
        for n in range(ATTN_KV_HEADS):
            s = _dot(jnp.where(lane_head == n, kcat, zero), q_cols)
            sink = sink_ref[n]
            cols = []
            for g in range(Q_PER_KV):
                c = slice(g * T, (g + 1) * T)
                sg = s[:, c] + bias_ref[variant, n, :, c]
                m = jnp.maximum(jnp.max(sg, axis=0, keepdims=True), sink[:, c])
                p = jnp.exp(sg - m)
                denom = jnp.sum(p, axis=0, keepdims=True) + jnp.exp(sink[:, c] - m)
                cols.append((p * (1.0 / denom)).astype(BF16))
            probs.append(jnp.concatenate(cols, axis=1))
            vals.append(jnp.where(row_head == n, vcat_t, zero.T))
        o_t = _dot(jnp.concatenate(vals, axis=1), jnp.concatenate(probs, axis=0))
        y_t = jnp.concatenate(
            [o_t[n * ATTN_HEAD_DIM:(n + 1) * ATTN_HEAD_DIM, g * T:(g + 1) * T]
             for n in range(ATTN_KV_HEADS) for g in range(Q_PER_KV)], axis=0)
        yt_ref[:, j * T:(j + 1) * T] = (y_t * gzt_ref[:, j * T:(j + 1) * T]).astype(yt_ref.dtype)
    kcat_sc[0:T, :] = kcat_sc[cols_step:, :]
    vcat_t_sc[:, 0:T] = vcat_t_sc[:, cols_step:]


def _attn_prompt(q_t, k, v, z_t, qw_t, kw, g_mat, e_mat, bias_t, sink_rows, batch, seq):
    step_rows = CHUNKS_PER_STEP * CHUNK
    nb = seq // step_rows
    row = lambda w: pl.BlockSpec((step_rows, w), lambda b, i: (b * nb + i, 0))
    col = pl.BlockSpec((ATTN_WIDTH, step_rows), lambda b, i: (0, b * nb + i))
    in_specs = [col, row(KV_WIDTH), row(KV_WIDTH), col, _const_spec((ATTN_WIDTH, step_rows)),
                _const_spec((1, KV_WIDTH)), _const_spec((KV_WIDTH, LANES)), _const_spec((LANES, KV_WIDTH)),
                pl.BlockSpec(bias_t.shape, lambda b, i: (0, 0, 0, 0), pipeline_mode=pl.Buffered(1)),
                _const_spec(sink_rows.shape)]
    kv_out = pl.BlockSpec((1, CHUNK, KV_WIDTH), lambda b, i: (b, 0, 0))
    out_specs = [col, kv_out, kv_out]
    out_shape = [jax.ShapeDtypeStruct((ATTN_WIDTH, batch * seq), BF16),
                 jax.ShapeDtypeStruct((batch, CHUNK, KV_WIDTH), F32),
                 jax.ShapeDtypeStruct((batch, CHUNK, KV_WIDTH), F32)]
    scratch = [pltpu.VMEM((step_rows + CHUNK, KV_WIDTH), BF16), pltpu.VMEM((KV_WIDTH, step_rows + CHUNK), BF16)]
    return pl.pallas_call(
        _attn_prompt_body, grid=(batch, nb), in_specs=in_specs, out_specs=out_specs, out_shape=out_shape,
        scratch_shapes=scratch, compiler_params=_params("arbitrary", "arbitrary"), name="attn_prompt")(
            q_t, k, v, z_t, qw_t, kw, g_mat, e_mat, bias_t, sink_rows)


def _ssd_sample_vec_body(z_ref, xbc_ref, dt_ref, sconv_ref, cw_ref, cb_ref, dtb_ref, alog_ref, dskip_ref,
                         gh_ref, e_ref, ypart_ref, ea_ref, xw_ref, b_ref, c_ref, cdec_ref, convnew_ref):
    steps = xbc_ref.shape[0]
    tail = CONV_WIDTH - 1
    full = [sconv_ref[j] for j in range(tail)] + [xbc_ref[l] for l in range(steps)]
    for j in range(tail):
        convnew_ref[j] = full[steps + j]
    gh = gh_ref[...]
    e_mat = e_ref[...]
    a_neg = -jnp.exp(alog_ref[...])
    xs, bm, cm, dts, acum = [], [], [], [], []
    run = None
    for l in range(steps):
        conv = cb_ref[...]
        for tap in range(CONV_WIDTH):
            conv = conv + full[l + tap] * cw_ref[tap:tap + 1, :]
        act = _silu(conv)
        xs.append(act[:, :SSM_WIDTH])
        bm.append(act[:, SSM_WIDTH:SSM_WIDTH + BC_WIDTH])
        cm.append(act[:, SSM_WIDTH + BC_WIDTH:])
        d = _softplus(dt_ref[l] + dtb_ref[...])
        dts.append(d)
        run = d * a_neg if run is None else run + d * a_neg
        acum.append(run)
        b_ref[l] = bm[l]
        c_ref[l] = cm[l]
    for l in range(steps):
        y = dskip_ref[...] * xs[l]
        for s in range(l + 1):
            cb_h = _dot_sel(cm[l] * bm[s], gh)
            coef = cb_h * jnp.exp(acum[l] - acum[s]) * dts[s]
            y = y + _dot_sel(coef, e_mat) * xs[s]
        ypart_ref[l] = y
        ea_ref[l] = _dot_sel(jnp.exp(acum[l]), e_mat)
        xw_ref[l] = xs[l] * _dot_sel(dts[l] * jnp.exp(acum[steps - 1] - acum[l]), e_mat)
    cdec_ref[...] = jnp.exp(acum[steps - 1])


def _ssd_sample_vec(z3, xbc3, dt3, sconv3, conv_w, conv_b, dtb, alog, dskip, gh_mat, e_mat):
    steps, nb = z3.shape[0], z3.shape[1]
    f = lambda *s: jax.ShapeDtypeStruct(s, F32)
    out_shape = [f(steps, nb, SSM_WIDTH), f(steps, nb, SSM_WIDTH), f(steps, nb, SSM_WIDTH),
                 f(steps, nb, BC_WIDTH), f(steps, nb, BC_WIDTH), f(nb, LANES), f(CONV_WIDTH - 1, nb, CONV_DIM)]
    return pl.pallas_call(_ssd_sample_vec_body, out_shape=out_shape,
                          compiler_params=pltpu.CompilerParams(vmem_limit_bytes=VMEM_LIMIT),
                          name="ssd_sample_vec")(
        z3, xbc3, dt3, sconv3, conv_w, conv_b, dtb, alog, dskip, gh_mat, e_mat)


def _ssd_sample_state_body(cdec_ref, st_ref, c_ref, b_ref, xw_ref, new_ref, yoff_ref):
    i = pl.program_id(0)
    heads_per_group = SSM_HEADS // SSM_GROUPS
    for j in range(SAMPLE_BATCH_BLOCK):
        st = st_ref[j]
        cb_bf = c_ref[:, j, :].astype(BF16)
        bb_bf = b_ref[:, j, :].astype(BF16)
        xw_bf = xw_ref[:, j, :].astype(BF16)
        y_parts = []
        for g in range(SSM_GROUPS):
            rows = slice(g * GROUP_WIDTH, (g + 1) * GROUP_WIDTH)
            ns = slice(g * SSM_STATE, (g + 1) * SSM_STATE)
            y_parts.append(_dot_nt(cb_bf[:, ns], st[rows].astype(BF16)))
            upd = _dot_tn(xw_bf[:, rows], bb_bf[:, ns])
            for hh in range(heads_per_group):
                h = g * heads_per_group + hh
                r = slice(h * SSM_HEAD_DIM, (h + 1) * SSM_HEAD_DIM)
                dec = cdec_ref[(i * SAMPLE_BATCH_BLOCK + j) * SSM_HEADS + h]
                new_ref[j, r, :] = st[r] * dec + upd[hh * SSM_HEAD_DIM:(hh + 1) * SSM_HEAD_DIM]
        yoff_ref[:, j, :] = jnp.concatenate(y_parts, axis=1)


def _ssd_sample_state(cdec_flat, state, c3, b3, xw3):
    steps, nb = c3.shape[0], c3.shape[1]
    bb = SAMPLE_BATCH_BLOCK
    tok = lambda w: pl.BlockSpec((steps, bb, w), lambda i: (0, i, 0))
    st_spec = pl.BlockSpec((bb, SSM_WIDTH, SSM_STATE), lambda i: (i, 0, 0))
    return pl.pallas_call(
        _ssd_sample_state_body, grid=(nb // bb,),
        in_specs=[pl.BlockSpec(memory_space=pltpu.SMEM), st_spec, tok(BC_WIDTH), tok(BC_WIDTH), tok(SSM_WIDTH)],
        out_specs=[st_spec, tok(SSM_WIDTH)],
        out_shape=[jax.ShapeDtypeStruct(state.shape, F32), jax.ShapeDtypeStruct((steps, nb, SSM_WIDTH), F32)],
        compiler_params=_params("parallel"), name="ssd_sample_state")(cdec_flat, state, c3, b3, xw3)


def _sample_finish_body(ypart_ref, yoff_ref, ea_ref, z_ref, nw_ref, q_ref, k_ref, qw_ref, kw_ref, g_ref, e_ref,
                        yssm_ref, qn_ref, kn_ref):
    y = ypart_ref[...] + yoff_ref[...] * ea_ref[...]
    yssm_ref[...] = _group_rmsnorm(y * _silu(z_ref[...]), nw_ref[...])
    g_mat = g_ref[...]
    e_mat = e_ref[...]
    qn = _head_rmsnorm(q_ref[...], g_mat, e_mat, qw_ref[...]) * ATTN_SCALE
    head = lambda h: qn[:, h * ATTN_HEAD_DIM:(h + 1) * ATTN_HEAD_DIM]
    qn_ref[...] = jnp.concatenate(
        [head(n * Q_PER_KV + g) for g in range(Q_PER_KV) for n in range(ATTN_KV_HEADS)], axis=1)
    kn_ref[...] = _head_rmsnorm(k_ref[...], g_mat[:KV_WIDTH], e_mat[:, :KV_WIDTH], kw_ref[...])


def _sample_finish(ypart, yoff, ea, z, norm_w, q, k, qw, kw, g_mat, e_mat):
    rows = z.shape[0]
    f = lambda w: jax.ShapeDtypeStruct((rows, w), F32)
    return pl.pallas_call(_sample_finish_body, out_shape=[f(SSM_WIDTH), f(ATTN_WIDTH), f(KV_WIDTH)],
                          compiler_params=pltpu.CompilerParams(vmem_limit_bytes=VMEM_LIMIT),
                          name="sample_finish")(ypart, yoff, ea, z, norm_w, q, k, qw, kw, g_mat, e_mat)


def _attn_sample_body(q_ref, kn_ref, vn_ref, z_ref, ckt_ref, cvt_ref, biasc_ref, biasn_ref,
                      y_ref, kot_ref, vot_ref):
    steps = q_ref.shape[0]
    bb = SAMPLE_BATCH_BLOCK
    blk = Q_PER_KV * steps
    rows = ATTN_KV_HEADS * blk
    pad = jnp.zeros((SUBLANES - steps, KV_WIDTH), F32)
    lane_head = _lane_head((blk, KV_WIDTH))
    zero = jnp.zeros((blk, KV_WIDTH), F32)

    s_c, s_n, k8, v8 = [], [], [], []
    for j in range(bb):
        q = q_ref[:, j, :]
        qg = jnp.concatenate([q[:, g * KV_WIDTH:(g + 1) * KV_WIDTH] for g in range(Q_PER_KV)], axis=0)
        qx = jnp.concatenate([jnp.where(lane_head == n, qg, zero) for n in range(ATTN_KV_HEADS)], axis=0)
        qx = qx.astype(BF16)
        k8.append(jnp.concatenate([kn_ref[:, j, :], pad], axis=0))
        v8.append(jnp.concatenate([vn_ref[:, j, :], pad], axis=0))
        s_c.append(_dot(qx, ckt_ref[j].astype(BF16)))
        s_n.append(_dot_nt(qx, k8[j].astype(BF16)))
    s_c = jnp.concatenate(s_c, axis=0) + biasc_ref[...]
    s_n = jnp.concatenate(s_n, axis=0) + biasn_ref[...]
    m = jnp.maximum(jnp.max(s_c, axis=-1, keepdims=True), jnp.max(s_n, axis=-1, keepdims=True))
    p_c = jnp.exp(s_c - m)
    p_n = jnp.exp(s_n - m)
    inv = 1.0 / (jnp.sum(p_c, axis=-1, keepdims=True) + jnp.sum(p_n, axis=-1, keepdims=True))
    p_c = (p_c * inv).astype(BF16)
    p_n = (p_n * inv).astype(BF16)

    lane = lax.broadcasted_iota(jnp.int32, (KV_WIDTH, WINDOW), 1)
    for j in range(bb):
        r = slice(j * rows, (j + 1) * rows)
        o = _dot_nt(p_c[r], cvt_ref[j].astype(BF16)) + _dot(p_n[r], v8[j].astype(BF16))
        og = zero
        for n in range(ATTN_KV_HEADS):
            og = og + jnp.where(lane_head == n, o[n * blk:(n + 1) * blk], zero)
        y = jnp.concatenate(
            [og[g * steps:(g + 1) * steps, n * ATTN_HEAD_DIM:(n + 1) * ATTN_HEAD_DIM]
             for n in range(ATTN_KV_HEADS) for g in range(Q_PER_KV)], axis=1)
        y_ref[:, j, :] = y * _silu(z_ref[:, j, :])

        for new8, old_ref, out_ref in ((k8[j], ckt_ref, kot_ref), (v8[j], cvt_ref, vot_ref)):
            tail_rows = jnp.concatenate([new8[steps:], new8[:steps]], axis=0)
            block = jnp.concatenate([jnp.zeros((WINDOW - SUBLANES, KV_WIDTH), F32), tail_rows], axis=0)
            shifted = pltpu.roll(old_ref[j], WINDOW - steps, axis=1)
            out_ref[j] = jnp.where(lane >= WINDOW - steps, block.T, shifted)


def _attn_sample(q3, kn3, vn3, z3, cache_kt, cache_vt, bias_c, bias_n):
    steps, nb = q3.shape[0], q3.shape[1]
    bb = SAMPLE_BATCH_BLOCK
    tok = lambda w: pl.BlockSpec((steps, bb, w), lambda i: (0, i, 0))
    cache_spec = pl.BlockSpec((bb, KV_WIDTH, WINDOW), lambda i: (i, 0, 0))
    return pl.pallas_call(
        _attn_sample_body, grid=(nb // bb,),
        in_specs=[tok(ATTN_WIDTH), tok(KV_WIDTH), tok(KV_WIDTH), tok(ATTN_WIDTH), cache_spec, cache_spec,
                  _const_spec(bias_c.shape), _const_spec(bias_n.shape)],
        out_specs=[tok(ATTN_WIDTH), cache_spec, cache_spec],
        out_shape=[jax.ShapeDtypeStruct((steps, nb, ATTN_WIDTH), F32),
                   jax.ShapeDtypeStruct(cache_kt.shape, F32), jax.ShapeDtypeStruct(cache_vt.shape, F32)],
        compiler_params=_params("parallel"), name="attn_sample")(
            q3, kn3, vn3, z3, cache_kt, cache_vt, bias_c, bias_n)


def _static_tables(steps):
    lanes = np.arange(ATTN_WIDTH)
    g_mat = np.zeros((ATTN_WIDTH, LANES), np.float32)
    g_mat[lanes, lanes // ATTN_HEAD_DIM] = 1.0
    e_mat = g_mat.T.copy()
    bc = np.arange(BC_WIDTH)
    gh_mat = np.zeros((BC_WIDTH, LANES), np.float32)
    for h in range(SSM_HEADS):
        gh_mat[bc // SSM_STATE == h // (SSM_HEADS // SSM_GROUPS), h] = 1.0
    T = CHUNK
    dist = np.arange(T)[:, None] - (np.arange(2 * T) - T)[None, :]
    first = np.broadcast_to((np.arange(2 * T) >= T)[None, :], dist.shape)
    prompt_buckets = np.stack([_bucket_or_masked(dist, first), _bucket_or_masked(dist)])
    dist_c = (np.arange(steps) + WINDOW)[:, None] - np.arange(WINDOW)[None, :]
    dist_n = np.arange(steps)[:, None] - np.arange(SUBLANES)[None, :]
    real = np.broadcast_to((np.arange(SUBLANES) < steps)[None, :], dist_n.shape)
    return dict(g=g_mat, e=e_mat, gh=gh_mat, prompt_buckets=prompt_buckets,
                cache_buckets=_bucket_or_masked(dist_c)[None], new_buckets=_bucket_or_masked(dist_n, real)[None])


def kernel(x_prompt, x_sample, cache_k, cache_v, state_ssm, state_conv, norm_w, w_in, conv_w, conv_b, dt_bias,
           a_log, d_skip, ssm_norm_w, q_norm_w, k_norm_w, sinks, rel_table, w_out):
    assert w_in.shape[0] == 1, "single-layer kernel"
    batch, seq, _ = x_prompt.shape
    nb, steps, _ = x_sample.shape
    tab = _static_tables(steps)
    g_mat = jnp.asarray(tab["g"], BF16)
    e_mat = jnp.asarray(tab["e"], BF16)
    gh_mat = jnp.asarray(tab["gh"], BF16)

    w = w_in[0]
    dt_end = SSM_WIDTH + CONV_DIM + SSM_HEADS
    w_all = jnp.concatenate([w[:, :dt_end], jnp.zeros((D_MODEL, LANES - SSM_HEADS), w.dtype), w[:, dt_end:]],
                            axis=1).astype(BF16)
    wq_t = w_all[:, IN_COLS["q"]].T
    wza_t = w_all[:, IN_COLS["za"]].T
    wo_all = w_out[0].astype(BF16)

    row = lambda v, width: jnp.pad(v.reshape(1, -1), ((0, 0), (0, width - v.size)))
    nw = row(norm_w[0], D_MODEL)
    cw = conv_w[0]
    cb = row(conv_b[0], CONV_DIM)
    dtb = row(dt_bias[0], LANES)
    alog = row(a_log[0], LANES)
    dskip = jnp.repeat(d_skip[0], SSM_HEAD_DIM).reshape(1, SSM_WIDTH)
    snw = row(ssm_norm_w[0], SSM_WIDTH)
    qw = jnp.tile(q_norm_w[0], ATTN_HEADS).reshape(1, ATTN_WIDTH)
    kw = jnp.tile(k_norm_w[0], ATTN_KV_HEADS).reshape(1, KV_WIDTH)
    sink = sinks[0]
    rel_flat = rel_table.reshape(-1)

    xp = x_prompt.reshape(batch * seq, D_MODEL)
    gz, xs_p, b_p, c_p, dt_p, q_t, k, v, gza_t, tail_p = _inproj_prompt(
        xp, nw, w_all, wq_t, wza_t, cw, cb, dtb, batch, seq)
    y_ssm, st_p = _ssd_prompt(gz, xs_p, b_p, c_p, dt_p, alog, dskip, snw, e_mat, batch, seq)
    bias_t = _bias_tables_t(rel_flat, jnp.asarray(tab["prompt_buckets"].transpose(0, 2, 1)))
    qw_t = jnp.broadcast_to((qw * ATTN_SCALE).reshape(ATTN_WIDTH, 1), (ATTN_WIDTH, CHUNKS_PER_STEP * CHUNK))
    sink_rows = jnp.repeat(sink.reshape(ATTN_KV_HEADS, Q_PER_KV), CHUNK, axis=1).reshape(ATTN_KV_HEADS, 1, -1)
    y_attn_t, k_p, v_p = _attn_prompt(q_t, k, v, gza_t, qw_t, kw, g_mat[:KV_WIDTH], e_mat[:, :KV_WIDTH], bias_t,
                                      sink_rows, batch, seq)
    y_p = _outproj(y_ssm, y_attn_t, xp, wo_all, True).reshape(batch, seq, D_MODEL)
    conv_p = tail_p[:, SUBLANES - (CONV_WIDTH - 1):, :]

    xs = jnp.swapaxes(x_sample, 0, 1).reshape(steps * nb, D_MODEL)
    z, xbc, dt, q, k, v, za = _inproj(xs, nw, w_all)
    t3 = lambda a: a.reshape(steps, nb, a.shape[-1])
    sconv3 = jnp.swapaxes(state_conv[0], 0, 1)
    ypart, ea, xw, b3, c3, cdec, conv_s3 = _ssd_sample_vec(
        t3(z), t3(xbc), t3(dt), sconv3, cw, cb, dtb, alog, dskip, gh_mat, e_mat)
    state_in = state_ssm[0].reshape(nb, SSM_WIDTH, SSM_STATE)
    st_s, yoff = _ssd_sample_state(cdec[:, :SSM_HEADS].reshape(-1), state_in, c3, b3, xw)
    f2 = lambda a: a.reshape(steps * nb, a.shape[-1])
    y_ssm, qn, kn = _sample_finish(f2(ypart), f2(yoff), f2(ea), z, snw, q, k, qw, kw, g_mat, e_mat)
    bias_c = _bias_tables(rel_flat, jnp.asarray(tab["cache_buckets"])).reshape(ATTN_HEADS * steps, WINDOW)
    bias_n = _bias_tables(rel_flat, jnp.asarray(tab["new_buckets"])).reshape(ATTN_HEADS * steps, SUBLANES)
    bias_n = bias_n.at[:, steps].set(jnp.repeat(sink, steps))
    bias_c = jnp.tile(bias_c, (SAMPLE_BATCH_BLOCK, 1))
    bias_n = jnp.tile(bias_n, (SAMPLE_BATCH_BLOCK, 1))
    to_t = lambda a: jnp.transpose(a[0], (0, 2, 3, 1)).reshape(nb, KV_WIDTH, WINDOW)
    from_t = lambda a: jnp.transpose(a.reshape(nb, ATTN_KV_HEADS, ATTN_HEAD_DIM, WINDOW), (0, 3, 1, 2))[None]
    y_attn3, k_st, v_st = _attn_sample(t3(qn), t3(kn), t3(v), t3(za), to_t(cache_k), to_t(cache_v),
                                       bias_c, bias_n)
    k_s, v_s = from_t(k_st), from_t(v_st)
    y_s = _outproj(y_ssm, f2(y_attn3), xs, wo_all, False)
    y_s = jnp.swapaxes(y_s.reshape(steps, nb, D_MODEL), 0, 1)

    kv5 = lambda a: a.reshape(1, a.shape[0], WINDOW, ATTN_KV_HEADS, ATTN_HEAD_DIM)
    st5 = lambda a: a.reshape(1, a.shape[0], SSM_HEADS, SSM_HEAD_DIM, SSM_STATE)
    return (y_p, y_s, kv5(k_p), kv5(v_p), st5(st_p), conv_p[None],
            k_s, v_s, st5(st_s), jnp.swapaxes(conv_s3, 0, 1)[None])
```

```python
import functools
import math

import numpy as np
import jax
import jax.numpy as jnp
from jax import lax
from jax.experimental import pallas as pl
from jax.experimental.pallas import tpu as pltpu

F32 = jnp.float32
BF16 = jnp.bfloat16

D_MODEL = 1024
SSM_HEADS = 16
SSM_HEAD_DIM = 64
SSM_WIDTH = SSM_HEADS * SSM_HEAD_DIM
SSM_GROUPS = 2
SSM_STATE = 128
GROUP_WIDTH = SSM_WIDTH // SSM_GROUPS
BC_WIDTH = SSM_GROUPS * SSM_STATE
CONV_WIDTH = 4
CONV_DIM = SSM_WIDTH + 2 * BC_WIDTH
CHUNK = 128
ATTN_HEADS = 16
ATTN_KV_HEADS = 4
Q_PER_KV = ATTN_HEADS // ATTN_KV_HEADS
ATTN_HEAD_DIM = 64
ATTN_WIDTH = ATTN_HEADS * ATTN_HEAD_DIM
KV_WIDTH = ATTN_KV_HEADS * ATTN_HEAD_DIM
WINDOW = 128
ATTN_SCALE = ATTN_HEAD_DIM ** -0.5
REL_BUCKETS = 32
REL_MAX_DIST = 128
EPS = 1e-6
LOG2E = 1.0 / math.log(2.0)
NEG = -1e30

LANES = 128
SUBLANES = 8
MXU_WIDTH = 256
VMEM_LIMIT = 56 * 1024 * 1024
def _in_proj_rows():
    widths = (("z", SSM_WIDTH), ("xbc", CONV_DIM), ("dt", SSM_HEADS), ("q", ATTN_WIDTH), ("k", KV_WIDTH),
              ("v", KV_WIDTH), ("za", ATTN_WIDTH))
    rows, start = {}, 0
    for name, width in widths:
        rows[name] = slice(start, start + width)
        start += width
    return rows


IN_ROWS = _in_proj_rows()
DT_ROWS = slice(IN_ROWS["dt"].start, IN_ROWS["dt"].start + LANES)

PROJ_ROWS = 512
CHUNKS_PER_STEP = 4
SAMPLE_BATCH_BLOCK = 8


def _dot(a, b):
    return jnp.dot(a, b, preferred_element_type=F32)


def _dot_nt(a, b):
    return lax.dot_general(a, b, (((1,), (1,)), ((), ())), preferred_element_type=F32)


def _dot_tn(a, b):
    return lax.dot_general(a, b, (((0,), (0,)), ((), ())), preferred_element_type=F32)


def _split2(v):
    hi = v.astype(BF16)
    lo = (v - hi.astype(F32)).astype(BF16)
    return hi, lo


def _dot_sel(v, m):
    hi, lo = _split2(v)
    return _dot(hi, m) + _dot(lo, m)


def _dot_sel3(m, v):
    hi = v.astype(BF16)
    r1 = v - hi.astype(F32)
    mid = r1.astype(BF16)
    lo = (r1 - mid.astype(F32)).astype(BF16)
    return _dot(m, hi) + _dot(m, mid) + _dot(m, lo)


def _silu(x):
    return x / (1.0 + jnp.exp(-x))


def _softplus(x):
    return jnp.maximum(x, 0.0) + jnp.log1p(jnp.exp(-jnp.abs(x)))


def _params(*sem):
    return pltpu.CompilerParams(dimension_semantics=sem, vmem_limit_bytes=VMEM_LIMIT)


def _const_spec(shape):
    nd = len(shape)
    return pl.BlockSpec(shape, lambda *_: (0,) * nd)


def _normed_input(x_ref, nw_ref):
    x = x_ref[...]
    ms = jnp.mean(x * x, axis=-1, keepdims=True)
    return (x * lax.rsqrt(ms + EPS) * nw_ref[...]).astype(BF16)


def _dt_projection(h, wt_ref):
    raw = _dot_nt(h, wt_ref[DT_ROWS, :])
    return jnp.where(lax.broadcasted_iota(jnp.int32, raw.shape, 1) < SSM_HEADS, raw, 0.0)


def _inproj_body(x_ref, nw_ref, wt_ref, *out_refs):
    h = _normed_input(x_ref, nw_ref)
    for name, o_ref in zip(IN_ROWS, out_refs):
        o_ref[...] = _dt_projection(h, wt_ref) if name == "dt" else _dot_nt(h, wt_ref[IN_ROWS[name], :])


def _inproj(x2d, norm_w, w_t):
    rows = x2d.shape[0]
    tm = min(PROJ_ROWS, rows)
    widths = [LANES if name == "dt" else r.stop - r.start for name, r in IN_ROWS.items()]
    in_specs = [pl.BlockSpec((tm, D_MODEL), lambda i: (i, 0)), _const_spec((1, D_MODEL)),
                pl.BlockSpec(w_t.shape, lambda i: (0, 0), pipeline_mode=pl.Buffered(1))]
    out_specs = [pl.BlockSpec((tm, w), lambda i: (i, 0)) for w in widths]
    out_shape = [jax.ShapeDtypeStruct((rows, w), F32) for w in widths]
    return pl.pallas_call(
        _inproj_body, grid=(rows // tm,), in_specs=in_specs, out_specs=out_specs, out_shape=out_shape,
        compiler_params=_params("parallel"), name="inproj")(x2d, norm_w, w_t)


def _shift_rows(u, prev_tail, k):
    rows, width = u.shape
    tiles = jnp.concatenate([prev_tail, u], axis=0).reshape(rows // SUBLANES + 1, SUBLANES, width)
    rot = pltpu.roll(tiles, k, axis=1)
    first = lax.broadcasted_iota(jnp.int32, (1, SUBLANES, width), 1) < k
    return jnp.where(first, rot[:-1], rot[1:]).reshape(rows, width)


def _inproj_prompt_body(steps_per_seq, x_ref, nw_ref, wt_ref, cw_ref, cb_ref, dtb_ref,
                        gz_ref, xs_ref, b_ref, c_ref, dt_ref, qt_ref, k_ref, v_ref, gzat_ref, tail_ref, tail_sc):
    @pl.when(pl.program_id(0) % steps_per_seq == 0)
    def _():
        tail_sc[...] = jnp.zeros_like(tail_sc)

    h = _normed_input(x_ref, nw_ref)
    rows = h.shape[0]
    w_tile = lambda name, j: wt_ref[IN_ROWS[name].start + j * MXU_WIDTH:IN_ROWS[name].start + (j + 1) * MXU_WIDTH, :]
    for j in range(CONV_DIM // MXU_WIDTH):
        cols = slice(j * MXU_WIDTH, (j + 1) * MXU_WIDTH)
        u = _dot_nt(h, w_tile("xbc", j))
        prev_tail = tail_sc[:, cols]
        conv = cb_ref[:, cols] + u * cw_ref[CONV_WIDTH - 1:CONV_WIDTH, cols]
        for k in range(1, CONV_WIDTH):
            tap = CONV_WIDTH - 1 - k
            conv = conv + _shift_rows(u, prev_tail, k) * cw_ref[tap:tap + 1, cols]
        new_tail = u[rows - SUBLANES:, :]
        tail_sc[:, cols] = new_tail
        tail_ref[0, :, cols] = new_tail
        act = _silu(conv)
        if j < SSM_WIDTH // MXU_WIDTH:
            xs_ref[:, cols] = act
        elif j == SSM_WIDTH // MXU_WIDTH:
            b_ref[...] = act.astype(BF16)
        else:
            c_ref[...] = act.astype(BF16)

    for j in range(SSM_WIDTH // MXU_WIDTH):
        cols = slice(j * MXU_WIDTH, (j + 1) * MXU_WIDTH)
        gz_ref[:, cols] = _silu(_dot_nt(h, w_tile("z", j)))
    for j in range(ATTN_WIDTH // MXU_WIDTH):
        feats = slice(j * MXU_WIDTH, (j + 1) * MXU_WIDTH)
        gzat_ref[feats, :] = _silu(_dot_nt(w_tile("za", j), h))
    dt_ref[...] = _softplus(_dt_projection(h, wt_ref) + dtb_ref[...])
    qt_ref[...] = _dot_nt(wt_ref[IN_ROWS["q"], :], h)
    k_ref[...] = _dot_nt(h, wt_ref[IN_ROWS["k"], :])
    v_ref[...] = _dot_nt(h, wt_ref[IN_ROWS["v"], :])


def _inproj_prompt(x2d, norm_w, w_t, conv_w, conv_b, dtb, batch, seq):
    rows = x2d.shape[0]
    tm = PROJ_ROWS
    steps_per_seq = seq // tm
    resident = lambda a: pl.BlockSpec(a.shape, lambda i: (0, 0), pipeline_mode=pl.Buffered(1))
    rowblk = lambda w: pl.BlockSpec((tm, w), lambda i: (i, 0))
    colblk = pl.BlockSpec((ATTN_WIDTH, tm), lambda i: (0, i))
    in_specs = ([rowblk(D_MODEL), _const_spec((1, D_MODEL)), resident(w_t)]
                + [_const_spec(conv_w.shape), _const_spec(conv_b.shape), _const_spec(dtb.shape)])
    out_specs = [rowblk(SSM_WIDTH), rowblk(SSM_WIDTH), rowblk(BC_WIDTH), rowblk(BC_WIDTH), rowblk(LANES),
                 colblk, rowblk(KV_WIDTH), rowblk(KV_WIDTH), colblk,
                 pl.BlockSpec((1, SUBLANES, CONV_DIM), lambda i: (i // steps_per_seq, 0, 0))]
    f = lambda r, c, dt=F32: jax.ShapeDtypeStruct((r, c), dt)
    out_shape = [f(rows, SSM_WIDTH), f(rows, SSM_WIDTH), f(rows, BC_WIDTH, BF16), f(rows, BC_WIDTH, BF16),
                 f(rows, LANES), f(ATTN_WIDTH, rows), f(rows, KV_WIDTH), f(rows, KV_WIDTH), f(ATTN_WIDTH, rows),
                 jax.ShapeDtypeStruct((batch, SUBLANES, CONV_DIM), F32)]
    return pl.pallas_call(
        functools.partial(_inproj_prompt_body, steps_per_seq), grid=(rows // tm,), in_specs=in_specs,
        out_specs=out_specs, out_shape=out_shape, scratch_shapes=[pltpu.VMEM((SUBLANES, CONV_DIM), F32)],
        compiler_params=_params("arbitrary"), name="inproj_prompt")(
            x2d, norm_w, w_t, conv_w, conv_b, dtb)


def _outproj_body(attn_transposed, ys_ref, ya_ref, x_ref, w_ref, o_ref):
    ya = ya_ref[...].astype(BF16)
    w_attn = w_ref[SSM_WIDTH:, :]
    attn = _dot_tn(ya, w_attn) if attn_transposed else _dot(ya, w_attn)
    o_ref[...] = x_ref[...] + _dot(ys_ref[...].astype(BF16), w_ref[:SSM_WIDTH, :]) + attn


def _outproj(y_ssm, y_attn, x2d, w_out, attn_transposed):
    rows = x2d.shape[0]
    tm = min(PROJ_ROWS, rows)
    row_spec = pl.BlockSpec((tm, D_MODEL), lambda i: (i, 0))
    ya_spec = pl.BlockSpec((ATTN_WIDTH, tm), lambda i: (0, i)) if attn_transposed else row_spec
    w_spec = pl.BlockSpec(w_out.shape, lambda i: (0, 0), pipeline_mode=pl.Buffered(1))
    return pl.pallas_call(
        functools.partial(_outproj_body, attn_transposed), grid=(rows // tm,),
        in_specs=[row_spec, ya_spec, row_spec, w_spec],
        out_specs=row_spec, out_shape=jax.ShapeDtypeStruct((rows, D_MODEL), F32),
        compiler_params=_params("parallel"), name="outproj")(y_ssm, y_attn, x2d, w_out)


def _group_rmsnorm(gy, norm_w):
    parts = []
    for g in range(SSM_GROUPS):
        blk = gy[:, g * GROUP_WIDTH:(g + 1) * GROUP_WIDTH]
        ms = jnp.mean(blk * blk, axis=-1, keepdims=True)
        parts.append(blk * lax.rsqrt(ms + EPS))
    return jnp.concatenate(parts, axis=1) * norm_w


def _ssd_chunk(gz, xs, b_bf, c_bf, dt, a_neg, dskip, norm_w, e_mat, state):
    xs_bf = xs.astype(BF16)

    a = dt * a_neg
    li = lax.broadcasted_iota(jnp.int32, (CHUNK, CHUNK), 0)
    si = lax.broadcasted_iota(jnp.int32, (CHUNK, CHUNK), 1)
    causal = li >= si
    a_cum = _dot_sel3(jnp.where(causal, 1.0, 0.0).astype(BF16), a)
    a2 = a_cum * LOG2E
    row_term = a2.T - jnp.log2(dt.T)
    ea_full = _dot_sel(jnp.exp(a_cum), e_mat)
    w_full = _dot_sel(dt * jnp.exp(a_cum[CHUNK - 1:CHUNK, :] - a_cum), e_mat)

    cb = [_dot_nt(c_bf[:, g * SSM_STATE:(g + 1) * SSM_STATE], b_bf[:, g * SSM_STATE:(g + 1) * SSM_STATE])
          for g in range(SSM_GROUPS)]
    half = lax.broadcasted_iota(jnp.int32, (CHUNK, LANES), 1) < SSM_HEAD_DIM
    heads_per_group = SSM_HEADS // SSM_GROUPS
    y_parts = []
    for pair in range(SSM_HEADS // 2):
        blocks = []
        for h in (2 * pair, 2 * pair + 1):
            seg = a2[:, h:h + 1] - row_term[h:h + 1, :]
            decay_dt = jnp.exp2(jnp.where(causal, seg, -jnp.inf))
            blocks.append((cb[h // heads_per_group] * decay_dt).astype(BF16))
        lhs = jnp.concatenate(blocks, axis=1)
        xp = xs_bf[:, pair * LANES:(pair + 1) * LANES]
        zero = jnp.zeros_like(xp)
        rhs = jnp.concatenate([jnp.where(half, xp, zero), jnp.where(half, zero, xp)], axis=0)
        y_parts.append(_dot(lhs, rhs))
    y_diag = jnp.concatenate(y_parts, axis=1)

    state_bf = state.astype(BF16)
    xw_bf = (xs * w_full).astype(BF16)
    y_off, upd = [], []
    for g in range(SSM_GROUPS):
        cols = slice(g * GROUP_WIDTH, (g + 1) * GROUP_WIDTH)
        ns = slice(g * SSM_STATE, (g + 1) * SSM_STATE)
        y_off.append(_dot(c_bf[:, ns], state_bf[:, cols]))
        upd.append(_dot_tn(b_bf[:, ns], xw_bf[:, cols]))
    y = y_diag + jnp.concatenate(y_off, axis=1) * ea_full + dskip * xs
    new_state = state * ea_full[CHUNK - 1:CHUNK, :] + jnp.concatenate(upd, axis=1)
    return _group_rmsnorm(y * gz, norm_w), new_state


def _ssd_prompt_body(gz_ref, xs_ref, b_ref, c_ref, dt_ref, alog_ref, dskip_ref, nw_ref, e_ref,
                     y_ref, st_ref, state_sc):
    c = pl.program_id(1)

    @pl.when(c == 0)
    def _():
        state_sc[...] = jnp.zeros_like(state_sc)

    a_neg = -jnp.exp(alog_ref[...])
    state = state_sc[...]
    for j in range(CHUNKS_PER_STEP):
        r = slice(j * CHUNK, (j + 1) * CHUNK)
        y, state = _ssd_chunk(gz_ref[r, :], xs_ref[r, :], b_ref[r, :], c_ref[r, :], dt_ref[r, :], a_neg,
                              dskip_ref[...], nw_ref[...], e_ref[...], state)
        y_ref[r, :] = y.astype(y_ref.dtype)
    state_sc[...] = state

    @pl.when(c == pl.num_programs(1) - 1)
    def _():
        st_ref[0] = state.T


def _ssd_prompt(gz, xs, b, c, dt, alog, dskip, norm_w, e_mat, batch, seq):
    step_rows = CHUNKS_PER_STEP * CHUNK
    nc = seq // step_rows
    row = lambda w: pl.BlockSpec((step_rows, w), lambda b, c: (b * nc + c, 0))
    in_specs = [row(SSM_WIDTH), row(SSM_WIDTH), row(BC_WIDTH), row(BC_WIDTH), row(LANES),
                _const_spec((1, LANES)), _const_spec((1, SSM_WIDTH)), _const_spec((1, SSM_WIDTH)),
                _const_spec((LANES, SSM_WIDTH))]
    out_specs = [row(SSM_WIDTH), pl.BlockSpec((1, SSM_WIDTH, SSM_STATE), lambda b, c: (b, 0, 0))]
    out_shape = [jax.ShapeDtypeStruct((batch * seq, SSM_WIDTH), BF16),
                 jax.ShapeDtypeStruct((batch, SSM_WIDTH, SSM_STATE), F32)]
    scratch = [pltpu.VMEM((SSM_STATE, SSM_WIDTH), F32)]
    return pl.pallas_call(
        _ssd_prompt_body, grid=(batch, nc), in_specs=in_specs, out_specs=out_specs, out_shape=out_shape,
        scratch_shapes=scratch, compiler_params=_params("arbitrary", "arbitrary"), name="ssd_prompt")(
            gz, xs, b, c, dt, alog, dskip, norm_w, e_mat)


def _rel_bucket_np(dist):
    max_exact = REL_BUCKETS // 2
    d_f = np.maximum(dist, 1).astype(np.float32)
    large = max_exact + (np.log(d_f / np.float32(max_exact)) / np.float32(math.log(REL_MAX_DIST / max_exact))
                         * np.float32(REL_BUCKETS - max_exact)).astype(np.int32)
    return np.where(dist < max_exact, dist, np.minimum(large, REL_BUCKETS - 1)).astype(np.int32)


def _bucket_or_masked(dist, extra_mask=None):
    ok = (dist >= 0) & (dist <= WINDOW)
    if extra_mask is not None:
        ok = ok & extra_mask
    return np.where(ok, _rel_bucket_np(np.clip(dist, 0, WINDOW)), -1).astype(np.int32)


def _bias_body(rel_ref, bucket_ref, o_ref):
    bucket = bucket_ref[0]

    def per_head(h, carry):
        acc = jnp.full(bucket.shape, NEG, F32)
        for bkt in range(REL_BUCKETS):
            acc = jnp.where(bucket == bkt, rel_ref[bkt * ATTN_HEADS + h], acc)
        o_ref[0, h] = acc
        return carry

    lax.fori_loop(0, ATTN_HEADS, per_head, 0)


def _bias_tables(rel_flat, buckets):
    nv, lq, lk = buckets.shape
    return pl.pallas_call(
        _bias_body, grid=(nv,),
        in_specs=[pl.BlockSpec(memory_space=pltpu.SMEM), pl.BlockSpec((1, lq, lk), lambda v: (v, 0, 0))],
        out_specs=pl.BlockSpec((1, ATTN_HEADS, lq, lk), lambda v: (v, 0, 0, 0)),
        out_shape=jax.ShapeDtypeStruct((nv, ATTN_HEADS, lq, lk), F32),
        compiler_params=_params("arbitrary"), name="rel_bias")(rel_flat, buckets)


def _bias_t_body(rel_ref, bucket_ref, o_ref):
    variants = [bucket_ref[v] for v in range(bucket_ref.shape[0])]
    union = functools.reduce(jnp.maximum, variants)
    lq = union.shape[1]

    def per_kv_head(n, carry):
        for g in range(Q_PER_KV):
            acc = jnp.full(union.shape, NEG, F32)
            for bkt in range(REL_BUCKETS):
                acc = jnp.where(union == bkt, rel_ref[bkt * ATTN_HEADS + n * Q_PER_KV + g], acc)
            for v, bucket in enumerate(variants):
                o_ref[v, n, :, g * lq:(g + 1) * lq] = jnp.where(bucket >= 0, acc, NEG)
        return carry

    lax.fori_loop(0, ATTN_KV_HEADS, per_kv_head, 0)


def _bias_tables_t(rel_flat, buckets_t):
    nv, lk, lq = buckets_t.shape
    out_dims = (nv, ATTN_KV_HEADS, lk, Q_PER_KV * lq)
    return pl.pallas_call(
        _bias_t_body,
        in_specs=[pl.BlockSpec(memory_space=pltpu.SMEM), pl.BlockSpec(memory_space=pltpu.VMEM)],
        out_specs=pl.BlockSpec(memory_space=pltpu.VMEM),
        out_shape=jax.ShapeDtypeStruct(out_dims, F32),
        compiler_params=pltpu.CompilerParams(vmem_limit_bytes=VMEM_LIMIT), name="rel_bias_t")(rel_flat, buckets_t)


def _head_rmsnorm(x, g_mat, e_mat, w):
    ms = _dot_sel(x * x, g_mat) * (1.0 / ATTN_HEAD_DIM)
    return x * _dot_sel(lax.rsqrt(ms + EPS), e_mat) * w


def _lane_head(shape):
    return lax.broadcasted_iota(jnp.int32, shape, 1) // ATTN_HEAD_DIM


def _sink_column(sink_ref, n, rows_per_head):
    return jnp.concatenate(
        [jnp.full((rows_per_head, 1), sink_ref[n * Q_PER_KV + g], F32) for g in range(Q_PER_KV)], axis=0)


def _softmax_with_sink(s, sink):
    m = jnp.maximum(jnp.max(s, axis=-1, keepdims=True), sink)
    p = jnp.exp(s - m)
    denom = jnp.sum(p, axis=-1, keepdims=True) + jnp.exp(sink - m)
    return p / denom


def _attn_prompt_body(qt_ref, k_ref, v_ref, gzt_ref, qwt_ref, kw_ref, g_ref, e_ref, bias_ref, sink_ref,
                      yt_ref, kn_ref, vn_ref, kcat_sc, vcat_t_sc):
    T = CHUNK
    step = pl.program_id(1)

    @pl.when(step == 0)
    def _():
        kcat_sc[0:T, :] = jnp.zeros((T, KV_WIDTH), BF16)
        vcat_t_sc[:, 0:T] = jnp.zeros((KV_WIDTH, T), BF16)

    cols_step = CHUNKS_PER_STEP * T
    q3 = qt_ref[...].reshape(ATTN_HEADS, ATTN_HEAD_DIM, cols_step)
    ms = jnp.mean(q3 * q3, axis=1, keepdims=True)
    qn = ((q3 * lax.rsqrt(ms + EPS)).reshape(ATTN_WIDTH, cols_step) * qwt_ref[...]).astype(BF16)

    kn = _head_rmsnorm(k_ref[...], g_ref[...], e_ref[...], kw_ref[...])
    v = v_ref[...]
    kn_ref[0] = kn[cols_step - T:]
    vn_ref[0] = v[cols_step - T:]
    kcat_sc[T:, :] = kn.astype(BF16)
    vcat_t_sc[:, T:] = v.T.astype(BF16)

    lane_head = _lane_head((2 * T, KV_WIDTH))
    row_head = lax.broadcasted_iota(jnp.int32, (KV_WIDTH, 2 * T), 0) // ATTN_HEAD_DIM
    zero = jnp.zeros((2 * T, KV_WIDTH), BF16)
    first_variant = jnp.minimum(step, 1)
    for j in range(CHUNKS_PER_STEP):
        kcat = kcat_sc[j * T:(j + 2) * T, :]
        vcat_t = vcat_t_sc[:, j * T:(j + 2) * T]
        variant = first_variant if j == 0 else 1
        q_blk = qn[:, j * T:(j + 1) * T]
        head = lambda h: q_blk[h * ATTN_HEAD_DIM:(h + 1) * ATTN_HEAD_DIM]
        q_cols = jnp.concatenate(
            [jnp.concatenate([head(n * Q_PER_KV + g) for n in range(ATTN_KV_HEADS)], axis=0)
             for g in range(Q_PER_KV)], axis=1)
        probs, vals = [], []
        for n in range(ATTN_KV_HEADS):
            s = _dot(jnp.where(lane_head == n, kcat, zero), q_cols)
            sink = sink_ref[n]
            cols = []
            for g in range(Q_PER_KV):
                c = slice(g * T, (g + 1) * T)
                sg = s[:, c] + bias_ref[variant, n, :, c]
                m = jnp.maximum(jnp.max(sg, axis=0, keepdims=True), sink[:, c])
                p = jnp.exp(sg - m)
                denom = jnp.sum(p, axis=0, keepdims=True) + jnp.exp(sink[:, c] - m)
                cols.append((p * (1.0 / denom)).astype(BF16))
            probs.append(jnp.concatenate(cols, axis=1))
            vals.append(jnp.where(row_head == n, vcat_t, zero.T))
        o_t = _dot(jnp.concatenate(vals, axis=1), jnp.concatenate(probs, axis=0))
        y_t = jnp.concatenate(
            [o_t[n * ATTN_HEAD_DIM:(n + 1) * ATTN_HEAD_DIM, g * T:(g + 1) * T]
             for n in range(ATTN_KV_HEADS) for g in range(Q_PER_KV)], axis=0)
        yt_ref[:, j * T:(j + 1) * T] = (y_t * gzt_ref[:, j * T:(j + 1) * T]).astype(yt_ref.dtype)
    kcat_sc[0:T, :] = kcat_sc[cols_step:, :]
    vcat_t_sc[:, 0:T] = vcat_t_sc[:, cols_step:]


def _attn_prompt(q_t, k, v, z_t, qw_t, kw, g_mat, e_mat, bias_t, sink_rows, batch, seq):
    step_rows = CHUNKS_PER_STEP * CHUNK
    nb = seq // step_rows
    row = lambda w: pl.BlockSpec((step_rows, w), lambda b, i: (b * nb + i, 0))
    col = pl.BlockSpec((ATTN_WIDTH, step_rows), lambda b, i: (0, b * nb + i))
    in_specs = [col, row(KV_WIDTH), row(KV_WIDTH), col, _const_spec((ATTN_WIDTH, step_rows)),
                _const_spec((1, KV_WIDTH)), _const_spec((KV_WIDTH, LANES)), _const_spec((LANES, KV_WIDTH)),
                pl.BlockSpec(bias_t.shape, lambda b, i: (0, 0, 0, 0), pipeline_mode=pl.Buffered(1)),
                _const_spec(sink_rows.shape)]
    kv_out = pl.BlockSpec((1, CHUNK, KV_WIDTH), lambda b, i: (b, 0, 0))
    out_specs = [col, kv_out, kv_out]
    out_shape = [jax.ShapeDtypeStruct((ATTN_WIDTH, batch * seq), BF16),
                 jax.ShapeDtypeStruct((batch, CHUNK, KV_WIDTH), F32),
                 jax.ShapeDtypeStruct((batch, CHUNK, KV_WIDTH), F32)]
    scratch = [pltpu.VMEM((step_rows + CHUNK, KV_WIDTH), BF16), pltpu.VMEM((KV_WIDTH, step_rows + CHUNK), BF16)]
    return pl.pallas_call(
        _attn_prompt_body, grid=(batch, nb), in_specs=in_specs, out_specs=out_specs, out_shape=out_shape,
        scratch_shapes=scratch, compiler_params=_params("arbitrary", "arbitrary"), name="attn_prompt")(
            q_t, k, v, z_t, qw_t, kw, g_mat, e_mat, bias_t, sink_rows)


def _ssd_sample_vec_body(z_ref, xbc_ref, dt_ref, sconv_ref, cw_ref, cb_ref, dtb_ref, alog_ref, dskip_ref,
                         gh_ref, e_ref, ypart_ref, ea_ref, xw_ref, b_ref, c_ref, cdec_ref, convnew_ref):
    steps = xbc_ref.shape[0]
    tail = CONV_WIDTH - 1
    full = [sconv_ref[j] for j in range(tail)] + [xbc_ref[l] for l in range(steps)]
    for j in range(tail):
        convnew_ref[j] = full[steps + j]
    gh = gh_ref[...]
    e_mat = e_ref[...]
    a_neg = -jnp.exp(alog_ref[...])
    xs, bm, cm, dts, acum = [], [], [], [], []
    run = None
    for l in range(steps):
        conv = cb_ref[...]
        for tap in range(CONV_WIDTH):
            conv = conv + full[l + tap] * cw_ref[tap:tap + 1, :]
        act = _silu(conv)
        xs.append(act[:, :SSM_WIDTH])
        bm.append(act[:, SSM_WIDTH:SSM_WIDTH + BC_WIDTH])
        cm.append(act[:, SSM_WIDTH + BC_WIDTH:])
        d = _softplus(dt_ref[l] + dtb_ref[...])
        dts.append(d)
        run = d * a_neg if run is None else run + d * a_neg
        acum.append(run)
        b_ref[l] = bm[l]
        c_ref[l] = cm[l]
    for l in range(steps):
        y = dskip_ref[...] * xs[l]
        for s in range(l + 1):
            cb_h = _dot_sel(cm[l] * bm[s], gh)
            coef = cb_h * jnp.exp(acum[l] - acum[s]) * dts[s]
            y = y + _dot_sel(coef, e_mat) * xs[s]
        ypart_ref[l] = y
        ea_ref[l] = _dot_sel(jnp.exp(acum[l]), e_mat)
        xw_ref[l] = xs[l] * _dot_sel(dts[l] * jnp.exp(acum[steps - 1] - acum[l]), e_mat)
    cdec_ref[...] = jnp.exp(acum[steps - 1])


def _ssd_sample_vec(z3, xbc3, dt3, sconv3, conv_w, conv_b, dtb, alog, dskip, gh_mat, e_mat):
    steps, nb = z3.shape[0], z3.shape[1]
    f = lambda *s: jax.ShapeDtypeStruct(s, F32)
    out_shape = [f(steps, nb, SSM_WIDTH), f(steps, nb, SSM_WIDTH), f(steps, nb, SSM_WIDTH),
                 f(steps, nb, BC_WIDTH), f(steps, nb, BC_WIDTH), f(nb, LANES), f(CONV_WIDTH - 1, nb, CONV_DIM)]
    return pl.pallas_call(_ssd_sample_vec_body, out_shape=out_shape,
                          compiler_params=pltpu.CompilerParams(vmem_limit_bytes=VMEM_LIMIT),
                          name="ssd_sample_vec")(
        z3, xbc3, dt3, sconv3, conv_w, conv_b, dtb, alog, dskip, gh_mat, e_mat)


def _ssd_sample_state_body(cdec_ref, st_ref, c_ref, b_ref, xw_ref, new_ref, yoff_ref):
    i = pl.program_id(0)
    heads_per_group = SSM_HEADS // SSM_GROUPS
    for j in range(SAMPLE_BATCH_BLOCK):
        st = st_ref[j]
        cb_bf = c_ref[:, j, :].astype(BF16)
        bb_bf = b_ref[:, j, :].astype(BF16)
        xw_bf = xw_ref[:, j, :].astype(BF16)
        y_parts = []
        for g in range(SSM_GROUPS):
            rows = slice(g * GROUP_WIDTH, (g + 1) * GROUP_WIDTH)
            ns = slice(g * SSM_STATE, (g + 1) * SSM_STATE)
            y_parts.append(_dot_nt(cb_bf[:, ns], st[rows].astype(BF16)))
            upd = _dot_tn(xw_bf[:, rows], bb_bf[:, ns])
            for hh in range(heads_per_group):
                h = g * heads_per_group + hh
                r = slice(h * SSM_HEAD_DIM, (h + 1) * SSM_HEAD_DIM)
                dec = cdec_ref[(i * SAMPLE_BATCH_BLOCK + j) * SSM_HEADS + h]
                new_ref[j, r, :] = st[r] * dec + upd[hh * SSM_HEAD_DIM:(hh + 1) * SSM_HEAD_DIM]
        yoff_ref[:, j, :] = jnp.concatenate(y_parts, axis=1)


def _ssd_sample_state(cdec_flat, state, c3, b3, xw3):
    steps, nb = c3.shape[0], c3.shape[1]
    bb = SAMPLE_BATCH_BLOCK
    tok = lambda w: pl.BlockSpec((steps, bb, w), lambda i: (0, i, 0))
    st_spec = pl.BlockSpec((bb, SSM_WIDTH, SSM_STATE), lambda i: (i, 0, 0))
    return pl.pallas_call(
        _ssd_sample_state_body, grid=(nb // bb,),
        in_specs=[pl.BlockSpec(memory_space=pltpu.SMEM), st_spec, tok(BC_WIDTH), tok(BC_WIDTH), tok(SSM_WIDTH)],
        out_specs=[st_spec, tok(SSM_WIDTH)],
        out_shape=[jax.ShapeDtypeStruct(state.shape, F32), jax.ShapeDtypeStruct((steps, nb, SSM_WIDTH), F32)],
        compiler_params=_params("parallel"), name="ssd_sample_state")(cdec_flat, state, c3, b3, xw3)


def _sample_finish_body(ypart_ref, yoff_ref, ea_ref, z_ref, nw_ref, q_ref, k_ref, qw_ref, kw_ref, g_ref, e_ref,
                        yssm_ref, qn_ref, kn_ref):
    y = ypart_ref[...] + yoff_ref[...] * ea_ref[...]
    yssm_ref[...] = _group_rmsnorm(y * _silu(z_ref[...]), nw_ref[...])
    g_mat = g_ref[...]
    e_mat = e_ref[...]
    qn = _head_rmsnorm(q_ref[...], g_mat, e_mat, qw_ref[...]) * ATTN_SCALE
    head = lambda h: qn[:, h * ATTN_HEAD_DIM:(h + 1) * ATTN_HEAD_DIM]
    qn_ref[...] = jnp.concatenate(
        [head(n * Q_PER_KV + g) for g in range(Q_PER_KV) for n in range(ATTN_KV_HEADS)], axis=1)
    kn_ref[...] = _head_rmsnorm(k_ref[...], g_mat[:KV_WIDTH], e_mat[:, :KV_WIDTH], kw_ref[...])


def _sample_finish(ypart, yoff, ea, z, norm_w, q, k, qw, kw, g_mat, e_mat):
    rows = z.shape[0]
    f = lambda w: jax.ShapeDtypeStruct((rows, w), F32)
    return pl.pallas_call(_sample_finish_body, out_shape=[f(SSM_WIDTH), f(ATTN_WIDTH), f(KV_WIDTH)],
                          compiler_params=pltpu.CompilerParams(vmem_limit_bytes=VMEM_LIMIT),
                          name="sample_finish")(ypart, yoff, ea, z, norm_w, q, k, qw, kw, g_mat, e_mat)


def _attn_sample_body(q_ref, kn_ref, vn_ref, z_ref, ckt_ref, cvt_ref, biasc_ref, biasn_ref,
                      y_ref, kot_ref, vot_ref):
    steps = q_ref.shape[0]
    bb = SAMPLE_BATCH_BLOCK
    blk = Q_PER_KV * steps
    rows = ATTN_KV_HEADS * blk
    pad = jnp.zeros((SUBLANES - steps, KV_WIDTH), F32)
    lane_head = _lane_head((blk, KV_WIDTH))
    zero = jnp.zeros((blk, KV_WIDTH), F32)

    s_c, s_n, k8, v8 = [], [], [], []
    for j in range(bb):
        q = q_ref[:, j, :]
        qg = jnp.concatenate([q[:, g * KV_WIDTH:(g + 1) * KV_WIDTH] for g in range(Q_PER_KV)], axis=0)
        qx = jnp.concatenate([jnp.where(lane_head == n, qg, zero) for n in range(ATTN_KV_HEADS)], axis=0)
        qx = qx.astype(BF16)
        k8.append(jnp.concatenate([kn_ref[:, j, :], pad], axis=0))
        v8.append(jnp.concatenate([vn_ref[:, j, :], pad], axis=0))
        s_c.append(_dot(qx, ckt_ref[j].astype(BF16)))
        s_n.append(_dot_nt(qx, k8[j].astype(BF16)))
    s_c = jnp.concatenate(s_c, axis=0) + biasc_ref[...]
    s_n = jnp.concatenate(s_n, axis=0) + biasn_ref[...]
    m = jnp.maximum(jnp.max(s_c, axis=-1, keepdims=True), jnp.max(s_n, axis=-1, keepdims=True))
    p_c = jnp.exp(s_c - m)
    p_n = jnp.exp(s_n - m)
    inv = 1.0 / (jnp.sum(p_c, axis=-1, keepdims=True) + jnp.sum(p_n, axis=-1, keepdims=True))
    p_c = (p_c * inv).astype(BF16)
    p_n = (p_n * inv).astype(BF16)

    lane = lax.broadcasted_iota(jnp.int32, (KV_WIDTH, WINDOW), 1)
    for j in range(bb):
        r = slice(j * rows, (j + 1) * rows)
        o = _dot_nt(p_c[r], cvt_ref[j].astype(BF16)) + _dot(p_n[r], v8[j].astype(BF16))
        og = zero
        for n in range(ATTN_KV_HEADS):
            og = og + jnp.where(lane_head == n, o[n * blk:(n + 1) * blk], zero)
        y = jnp.concatenate(
            [og[g * steps:(g + 1) * steps, n * ATTN_HEAD_DIM:(n + 1) * ATTN_HEAD_DIM]
             for n in range(ATTN_KV_HEADS) for g in range(Q_PER_KV)], axis=1)
        y_ref[:, j, :] = y * _silu(z_ref[:, j, :])

        for new8, old_ref, out_ref in ((k8[j], ckt_ref, kot_ref), (v8[j], cvt_ref, vot_ref)):
            tail_rows = jnp.concatenate([new8[steps:], new8[:steps]], axis=0)
            block = jnp.concatenate([jnp.zeros((WINDOW - SUBLANES, KV_WIDTH), F32), tail_rows], axis=0)
            shifted = pltpu.roll(old_ref[j], WINDOW - steps, axis=1)
            out_ref[j] = jnp.where(lane >= WINDOW - steps, block.T, shifted)


def _attn_sample(q3, kn3, vn3, z3, cache_kt, cache_vt, bias_c, bias_n):
    steps, nb = q3.shape[0], q3.shape[1]
    bb = SAMPLE_BATCH_BLOCK
    tok = lambda w: pl.BlockSpec((steps, bb, w), lambda i: (0, i, 0))
    cache_spec = pl.BlockSpec((bb, KV_WIDTH, WINDOW), lambda i: (i, 0, 0))
    return pl.pallas_call(
        _attn_sample_body, grid=(nb // bb,),
        in_specs=[tok(ATTN_WIDTH), tok(KV_WIDTH), tok(KV_WIDTH), tok(ATTN_WIDTH), cache_spec, cache_spec,
                  _const_spec(bias_c.shape), _const_spec(bias_n.shape)],
        out_specs=[tok(ATTN_WIDTH), cache_spec, cache_spec],
        out_shape=[jax.ShapeDtypeStruct((steps, nb, ATTN_WIDTH), F32),
                   jax.ShapeDtypeStruct(cache_kt.shape, F32), jax.ShapeDtypeStruct(cache_vt.shape, F32)],
        compiler_params=_params("parallel"), name="attn_sample")(
            q3, kn3, vn3, z3, cache_kt, cache_vt, bias_c, bias_n)


def _static_tables(steps):
    lanes = np.arange(ATTN_WIDTH)
    g_mat = np.zeros((ATTN_WIDTH, LANES), np.float32)
    g_mat[lanes, lanes // ATTN_HEAD_DIM] = 1.0
    e_mat = g_mat.T.copy()
    bc = np.arange(BC_WIDTH)
    gh_mat = np.zeros((BC_WIDTH, LANES), np.float32)
    for h in range(SSM_HEADS):
        gh_mat[bc // SSM_STATE == h // (SSM_HEADS // SSM_GROUPS), h] = 1.0
    T = CHUNK
    dist = np.arange(T)[:, None] - (np.arange(2 * T) - T)[None, :]
    first = np.broadcast_to((np.arange(2 * T) >= T)[None, :], dist.shape)
    prompt_buckets = np.stack([_bucket_or_masked(dist, first), _bucket_or_masked(dist)])
    dist_c = (np.arange(steps) + WINDOW)[:, None] - np.arange(WINDOW)[None, :]
    dist_n = np.arange(steps)[:, None] - np.arange(SUBLANES)[None, :]
    real = np.broadcast_to((np.arange(SUBLANES) < steps)[None, :], dist_n.shape)
    return dict(g=g_mat, e=e_mat, gh=gh_mat, prompt_buckets=prompt_buckets,
                cache_buckets=_bucket_or_masked(dist_c)[None], new_buckets=_bucket_or_masked(dist_n, real)[None])


def kernel(x_prompt, x_sample, cache_k, cache_v, state_ssm, state_conv, norm_w, w_in, conv_w, conv_b, dt_bias,
           a_log, d_skip, ssm_norm_w, q_norm_w, k_norm_w, sinks, rel_table, w_out):
    assert w_in.shape[0] == 1, "single-layer kernel"
    batch, seq, _ = x_prompt.shape
    nb, steps, _ = x_sample.shape
    tab = _static_tables(steps)
    g_mat = jnp.asarray(tab["g"], BF16)
    e_mat = jnp.asarray(tab["e"], BF16)
    gh_mat = jnp.asarray(tab["gh"], BF16)

    w_t = jnp.transpose(w_in[0]).astype(BF16)
    wo_all = w_out[0].astype(BF16)

    row = lambda v, width: jnp.pad(v.reshape(1, -1), ((0, 0), (0, width - v.size)))
    nw = row(norm_w[0], D_MODEL)
    cw = conv_w[0]
    cb = row(conv_b[0], CONV_DIM)
    dtb = row(dt_bias[0], LANES)
    alog = row(a_log[0], LANES)
    dskip = jnp.repeat(d_skip[0], SSM_HEAD_DIM).reshape(1, SSM_WIDTH)
    snw = row(ssm_norm_w[0], SSM_WIDTH)
    qw = jnp.tile(q_norm_w[0], ATTN_HEADS).reshape(1, ATTN_WIDTH)
    kw = jnp.tile(k_norm_w[0], ATTN_KV_HEADS).reshape(1, KV_WIDTH)
    sink = sinks[0]
    rel_flat = rel_table.reshape(-1)

    xp = x_prompt.reshape(batch * seq, D_MODEL)
    gz, xs_p, b_p, c_p, dt_p, q_t, k, v, gza_t, tail_p = _inproj_prompt(
        xp, nw, w_t, cw, cb, dtb, batch, seq)
    y_ssm, st_p = _ssd_prompt(gz, xs_p, b_p, c_p, dt_p, alog, dskip, snw, e_mat, batch, seq)
    bias_t = _bias_tables_t(rel_flat, jnp.asarray(tab["prompt_buckets"].transpose(0, 2, 1)))
    qw_t = jnp.broadcast_to((qw * ATTN_SCALE).reshape(ATTN_WIDTH, 1), (ATTN_WIDTH, CHUNKS_PER_STEP * CHUNK))
    sink_rows = jnp.repeat(sink.reshape(ATTN_KV_HEADS, Q_PER_KV), CHUNK, axis=1).reshape(ATTN_KV_HEADS, 1, -1)
    y_attn_t, k_p, v_p = _attn_prompt(q_t, k, v, gza_t, qw_t, kw, g_mat[:KV_WIDTH], e_mat[:, :KV_WIDTH], bias_t,
                                      sink_rows, batch, seq)
    y_p = _outproj(y_ssm, y_attn_t, xp, wo_all, True).reshape(batch, seq, D_MODEL)
    conv_p = tail_p[:, SUBLANES - (CONV_WIDTH - 1):, :]

    xs = jnp.swapaxes(x_sample, 0, 1).reshape(steps * nb, D_MODEL)
    z, xbc, dt, q, k, v, za = _inproj(xs, nw, w_t)
    t3 = lambda a: a.reshape(steps, nb, a.shape[-1])
    sconv3 = jnp.swapaxes(state_conv[0], 0, 1)
    ypart, ea, xw, b3, c3, cdec, conv_s3 = _ssd_sample_vec(
        t3(z), t3(xbc), t3(dt), sconv3, cw, cb, dtb, alog, dskip, gh_mat, e_mat)
    state_in = state_ssm[0].reshape(nb, SSM_WIDTH, SSM_STATE)
    st_s, yoff = _ssd_sample_state(cdec[:, :SSM_HEADS].reshape(-1), state_in, c3, b3, xw)
    f2 = lambda a: a.reshape(steps * nb, a.shape[-1])
    y_ssm, qn, kn = _sample_finish(f2(ypart), f2(yoff), f2(ea), z, snw, q, k, qw, kw, g_mat, e_mat)
    bias_c = _bias_tables(rel_flat, jnp.asarray(tab["cache_buckets"])).reshape(ATTN_HEADS * steps, WINDOW)
    bias_n = _bias_tables(rel_flat, jnp.asarray(tab["new_buckets"])).reshape(ATTN_HEADS * steps, SUBLANES)
    bias_n = bias_n.at[:, steps].set(jnp.repeat(sink, steps))
    bias_c = jnp.tile(bias_c, (SAMPLE_BATCH_BLOCK, 1))
    bias_n = jnp.tile(bias_n, (SAMPLE_BATCH_BLOCK, 1))
    to_t = lambda a: jnp.transpose(a[0], (0, 2, 3, 1)).reshape(nb, KV_WIDTH, WINDOW)
    from_t = lambda a: jnp.transpose(a.reshape(nb, ATTN_KV_HEADS, ATTN_HEAD_DIM, WINDOW), (0, 3, 1, 2))[None]
    y_attn3, k_st, v_st = _attn_sample(t3(qn), t3(kn), t3(v), t3(za), to_t(cache_k), to_t(cache_v),
                                       bias_c, bias_n)
    k_s, v_s = from_t(k_st), from_t(v_st)
    y_s = _outproj(y_ssm, f2(y_attn3), xs, wo_all, False)
    y_s = jnp.swapaxes(y_s.reshape(steps, nb, D_MODEL), 0, 1)

    kv5 = lambda a: a.reshape(1, a.shape[0], WINDOW, ATTN_KV_HEADS, ATTN_HEAD_DIM)
    st5 = lambda a: a.reshape(1, a.shape[0], SSM_HEADS, SSM_HEAD_DIM, SSM_STATE)
    return (y_p, y_s, kv5(k_p), kv5(v_p), st5(st_p), conv_p[None],
            k_s, v_s, st5(st_s), jnp.swapaxes(conv_s3, 0, 1)[None])
```

```python
import functools
import math

import numpy as np
import jax
import jax.numpy as jnp
from jax import lax
from jax.experimental import pallas as pl
from jax.experimental.pallas import tpu as pltpu

F32 = jnp.float32
BF16 = jnp.bfloat16

D_MODEL = 1024
SSM_HEADS = 16
SSM_HEAD_DIM = 64
SSM_WIDTH = SSM_HEADS * SSM_HEAD_DIM
SSM_GROUPS = 2
SSM_STATE = 128
GROUP_WIDTH = SSM_WIDTH // SSM_GROUPS
BC_WIDTH = SSM_GROUPS * SSM_STATE
CONV_WIDTH = 4
CONV_DIM = SSM_WIDTH + 2 * BC_WIDTH
CHUNK = 128
ATTN_HEADS = 16
ATTN_KV_HEADS = 4
Q_PER_KV = ATTN_HEADS // ATTN_KV_HEADS
ATTN_HEAD_DIM = 64
ATTN_WIDTH = ATTN_HEADS * ATTN_HEAD_DIM
KV_WIDTH = ATTN_KV_HEADS * ATTN_HEAD_DIM
WINDOW = 128
ATTN_SCALE = ATTN_HEAD_DIM ** -0.5
REL_BUCKETS = 32
REL_MAX_DIST = 128
EPS = 1e-6
LOG2E = 1.0 / math.log(2.0)
NEG = -1e30

LANES = 128
SUBLANES = 8
MXU_WIDTH = 256
VMEM_LIMIT = 56 * 1024 * 1024
def _in_proj_rows():
    widths = (("z", SSM_WIDTH), ("xbc", CONV_DIM), ("dt", SSM_HEADS), ("q", ATTN_WIDTH), ("k", KV_WIDTH),
              ("v", KV_WIDTH), ("za", ATTN_WIDTH))
    rows, start = {}, 0
    for name, width in widths:
        rows[name] = slice(start, start + width)
        start += width
    return rows


IN_ROWS = _in_proj_rows()
DT_ROWS = slice(IN_ROWS["dt"].start, IN_ROWS["dt"].start + LANES)

PROJ_ROWS = 512
CHUNKS_PER_STEP = 4
SAMPLE_BATCH_BLOCK = 8


def _dot(a, b):
    return jnp.dot(a, b, preferred_element_type=F32)


def _dot_nt(a, b):
    return lax.dot_general(a, b, (((1,), (1,)), ((), ())), preferred_element_type=F32)


def _dot_tn(a, b):
    return lax.dot_general(a, b, (((0,), (0,)), ((), ())), preferred_element_type=F32)


def _split2(v):
    hi = v.astype(BF16)
    lo = (v - hi.astype(F32)).astype(BF16)
    return hi, lo


def _dot_sel(v, m):
    hi, lo = _split2(v)
    return _dot(hi, m) + _dot(lo, m)


def _dot_sel3(m, v):
    hi = v.astype(BF16)
    r1 = v - hi.astype(F32)
    mid = r1.astype(BF16)
    lo = (r1 - mid.astype(F32)).astype(BF16)
    return _dot(m, hi) + _dot(m, mid) + _dot(m, lo)


def _silu(x):
    return x / (1.0 + jnp.exp(-x))


def _softplus(x):
    return jnp.maximum(x, 0.0) + jnp.log1p(jnp.exp(-jnp.abs(x)))


def _params(*sem):
    return pltpu.CompilerParams(dimension_semantics=sem, vmem_limit_bytes=VMEM_LIMIT)


def _const_spec(shape):
    nd = len(shape)
    return pl.BlockSpec(shape, lambda *_: (0,) * nd)


def _normed_input(x_ref, nw_ref):
    x = x_ref[...]
    ms = jnp.mean(x * x, axis=-1, keepdims=True)
    return (x * lax.rsqrt(ms + EPS) * nw_ref[...]).astype(BF16)


def _dt_projection(h, wt_ref):
    raw = _dot_nt(h, wt_ref[DT_ROWS, :])
    return jnp.where(lax.broadcasted_iota(jnp.int32, raw.shape, 1) < SSM_HEADS, raw, 0.0)


def _inproj_body(x_ref, nw_ref, wt_ref, *out_refs):
    h = _normed_input(x_ref, nw_ref)
    for name, o_ref in zip(IN_ROWS, out_refs):
        o_ref[...] = _dt_projection(h, wt_ref) if name == "dt" else _dot_nt(h, wt_ref[IN_ROWS[name], :])


def _inproj(x2d, norm_w, w_t):
    rows = x2d.shape[0]
    tm = min(PROJ_ROWS, rows)
    widths = [LANES if name == "dt" else r.stop - r.start for name, r in IN_ROWS.items()]
    in_specs = [pl.BlockSpec((tm, D_MODEL), lambda i: (i, 0)), _const_spec((1, D_MODEL)),
                pl.BlockSpec(w_t.shape, lambda i: (0, 0), pipeline_mode=pl.Buffered(1))]
    out_specs = [pl.BlockSpec((tm, w), lambda i: (i, 0)) for w in widths]
    out_shape = [jax.ShapeDtypeStruct((rows, w), F32) for w in widths]
    return pl.pallas_call(
        _inproj_body, grid=(rows // tm,), in_specs=in_specs, out_specs=out_specs, out_shape=out_shape,
        compiler_params=_params("parallel"), name="inproj")(x2d, norm_w, w_t)


def _shift_rows(u, prev_tail, k):
    rows, width = u.shape
    tiles = jnp.concatenate([prev_tail, u], axis=0).reshape(rows // SUBLANES + 1, SUBLANES, width)
    rot = pltpu.roll(tiles, k, axis=1)
    first = lax.broadcasted_iota(jnp.int32, (1, SUBLANES, width), 1) < k
    return jnp.where(first, rot[:-1], rot[1:]).reshape(rows, width)


def _inproj_prompt_body(steps_per_seq, x_ref, nw_ref, wt_ref, cw_ref, cb_ref, dtb_ref,
                        gz_ref, xs_ref, b_ref, c_ref, dt_ref, qt_ref, k_ref, v_ref, gzat_ref, tail_ref, tail_sc):
    @pl.when(pl.program_id(0) % steps_per_seq == 0)
    def _():
        tail_sc[...] = jnp.zeros_like(tail_sc)

    h = _normed_input(x_ref, nw_ref)
    rows = h.shape[0]
    w_tile = lambda name, j: wt_ref[IN_ROWS[name].start + j * MXU_WIDTH:IN_ROWS[name].start + (j + 1) * MXU_WIDTH, :]
    for j in range(CONV_DIM // MXU_WIDTH):
        cols = slice(j * MXU_WIDTH, (j + 1) * MXU_WIDTH)
        u = _dot_nt(h, w_tile("xbc", j))
        prev_tail = tail_sc[:, cols]
        conv = cb_ref[:, cols] + u * cw_ref[CONV_WIDTH - 1:CONV_WIDTH, cols]
        for k in range(1, CONV_WIDTH):
            tap = CONV_WIDTH - 1 - k
            conv = conv + _shift_rows(u, prev_tail, k) * cw_ref[tap:tap + 1, cols]
        new_tail = u[rows - SUBLANES:, :]
        tail_sc[:, cols] = new_tail
        tail_ref[0, :, cols] = new_tail
        act = _silu(conv)
        if j < SSM_WIDTH // MXU_WIDTH:
            xs_ref[:, cols] = act
        elif j == SSM_WIDTH // MXU_WIDTH:
            b_ref[...] = act.astype(BF16)
        else:
            c_ref[...] = act.astype(BF16)

    for j in range(SSM_WIDTH // MXU_WIDTH):
        cols = slice(j * MXU_WIDTH, (j + 1) * MXU_WIDTH)
        gz_ref[:, cols] = _silu(_dot_nt(h, w_tile("z", j)))
    for j in range(ATTN_WIDTH // MXU_WIDTH):
        feats = slice(j * MXU_WIDTH, (j + 1) * MXU_WIDTH)
        gzat_ref[feats, :] = _silu(_dot_nt(w_tile("za", j), h))
    dt_ref[...] = _softplus(_dt_projection(h, wt_ref) + dtb_ref[...])
    qt_ref[...] = _dot_nt(wt_ref[IN_ROWS["q"], :], h)
    k_ref[...] = _dot_nt(h, wt_ref[IN_ROWS["k"], :])
    v_ref[...] = _dot_nt(h, wt_ref[IN_ROWS["v"], :])


def _inproj_prompt(x2d, norm_w, w_t, conv_w, conv_b, dtb, batch, seq):
    rows = x2d.shape[0]
    tm = PROJ_ROWS
    steps_per_seq = seq // tm
    resident = lambda a: pl.BlockSpec(a.shape, lambda i: (0, 0), pipeline_mode=pl.Buffered(1))
    rowblk = lambda w: pl.BlockSpec((tm, w), lambda i: (i, 0))
    colblk = pl.BlockSpec((ATTN_WIDTH, tm), lambda i: (0, i))
    in_specs = ([rowblk(D_MODEL), _const_spec((1, D_MODEL)), resident(w_t)]
                + [_const_spec(conv_w.shape), _const_spec(conv_b.shape), _const_spec(dtb.shape)])
    out_specs = [rowblk(SSM_WIDTH), rowblk(SSM_WIDTH), rowblk(BC_WIDTH), rowblk(BC_WIDTH), rowblk(LANES),
                 colblk, rowblk(KV_WIDTH), rowblk(KV_WIDTH), colblk,
                 pl.BlockSpec((1, SUBLANES, CONV_DIM), lambda i: (i // steps_per_seq, 0, 0))]
    f = lambda r, c, dt=F32: jax.ShapeDtypeStruct((r, c), dt)
    out_shape = [f(rows, SSM_WIDTH), f(rows, SSM_WIDTH), f(rows, BC_WIDTH, BF16), f(rows, BC_WIDTH, BF16),
                 f(rows, LANES), f(ATTN_WIDTH, rows), f(rows, KV_WIDTH), f(rows, KV_WIDTH), f(ATTN_WIDTH, rows),
                 jax.ShapeDtypeStruct((batch, SUBLANES, CONV_DIM), F32)]
    return pl.pallas_call(
        functools.partial(_inproj_prompt_body, steps_per_seq), grid=(rows // tm,), in_specs=in_specs,
        out_specs=out_specs, out_shape=out_shape, scratch_shapes=[pltpu.VMEM((SUBLANES, CONV_DIM), F32)],
        compiler_params=_params("arbitrary"), name="inproj_prompt")(
            x2d, norm_w, w_t, conv_w, conv_b, dtb)


def _outproj_body(ys_ref, ya_ref, x_ref, w_ref, o_ref):
    o_ref[...] = (x_ref[...] + _dot(ys_ref[...].astype(BF16), w_ref[:SSM_WIDTH, :])
                  + _dot(ya_ref[...].astype(BF16), w_ref[SSM_WIDTH:, :]))


def _outproj(y_ssm, y_attn, x2d, w_out):
    rows = x2d.shape[0]
    tm = min(PROJ_ROWS, rows)
    row_spec = pl.BlockSpec((tm, D_MODEL), lambda i: (i, 0))
    w_spec = pl.BlockSpec(w_out.shape, lambda i: (0, 0), pipeline_mode=pl.Buffered(1))
    return pl.pallas_call(
        _outproj_body, grid=(rows // tm,), in_specs=[row_spec, row_spec, row_spec, w_spec],
        out_specs=row_spec, out_shape=jax.ShapeDtypeStruct((rows, D_MODEL), F32),
        compiler_params=_params("parallel"), name="outproj")(y_ssm, y_attn, x2d, w_out)


def _group_rmsnorm(gy, norm_w):
    parts = []
    for g in range(SSM_GROUPS):
        blk = gy[:, g * GROUP_WIDTH:(g + 1) * GROUP_WIDTH]
        ms = jnp.mean(blk * blk, axis=-1, keepdims=True)
        parts.append(blk * lax.rsqrt(ms + EPS))
    return jnp.concatenate(parts, axis=1) * norm_w


def _ssd_chunk(gz, xs, b_bf, c_bf, dt, a_neg, dskip, norm_w, e_mat, state):
    xs_bf = xs.astype(BF16)

    a = dt * a_neg
    li = lax.broadcasted_iota(jnp.int32, (CHUNK, CHUNK), 0)
    si = lax.broadcasted_iota(jnp.int32, (CHUNK, CHUNK), 1)
    causal = li >= si
    a_cum = _dot_sel3(jnp.where(causal, 1.0, 0.0).astype(BF16), a)
    a2 = a_cum * LOG2E
    row_term = a2.T - jnp.log2(dt.T)
    ea_full = _dot_sel(jnp.exp(a_cum), e_mat)
    w_full = _dot_sel(dt * jnp.exp(a_cum[CHUNK - 1:CHUNK, :] - a_cum), e_mat)

    cb = [_dot_nt(c_bf[:, g * SSM_STATE:(g + 1) * SSM_STATE], b_bf[:, g * SSM_STATE:(g + 1) * SSM_STATE])
          for g in range(SSM_GROUPS)]
    half = lax.broadcasted_iota(jnp.int32, (CHUNK, LANES), 1) < SSM_HEAD_DIM
    heads_per_group = SSM_HEADS // SSM_GROUPS
    y_parts = []
    for pair in range(SSM_HEADS // 2):
        blocks = []
        for h in (2 * pair, 2 * pair + 1):
            seg = a2[:, h:h + 1] - row_term[h:h + 1, :]
            decay_dt = jnp.exp2(jnp.where(causal, seg, -jnp.inf))
            blocks.append((cb[h // heads_per_group] * decay_dt).astype(BF16))
        lhs = jnp.concatenate(blocks, axis=1)
        xp = xs_bf[:, pair * LANES:(pair + 1) * LANES]
        zero = jnp.zeros_like(xp)
        rhs = jnp.concatenate([jnp.where(half, xp, zero), jnp.where(half, zero, xp)], axis=0)
        y_parts.append(_dot(lhs, rhs))
    y_diag = jnp.concatenate(y_parts, axis=1)

    state_bf = state.astype(BF16)
    xw_bf = (xs * w_full).astype(BF16)
    y_off, upd = [], []
    for g in range(SSM_GROUPS):
        cols = slice(g * GROUP_WIDTH, (g + 1) * GROUP_WIDTH)
        ns = slice(g * SSM_STATE, (g + 1) * SSM_STATE)
        y_off.append(_dot(c_bf[:, ns], state_bf[:, cols]))
        upd.append(_dot_tn(b_bf[:, ns], xw_bf[:, cols]))
    y = y_diag + jnp.concatenate(y_off, axis=1) * ea_full + dskip * xs
    new_state = state * ea_full[CHUNK - 1:CHUNK, :] + jnp.concatenate(upd, axis=1)
    return _group_rmsnorm(y * gz, norm_w), new_state


def _rel_bucket_np(dist):
    max_exact = REL_BUCKETS // 2
    d_f = np.maximum(dist, 1).astype(np.float32)
    large = max_exact + (np.log(d_f / np.float32(max_exact)) / np.float32(math.log(REL_MAX_DIST / max_exact))
                         * np.float32(REL_BUCKETS - max_exact)).astype(np.int32)
    return np.where(dist < max_exact, dist, np.minimum(large, REL_BUCKETS - 1)).astype(np.int32)


def _bucket_or_masked(dist, extra_mask=None):
    ok = (dist >= 0) & (dist <= WINDOW)
    if extra_mask is not None:
        ok = ok & extra_mask
    return np.where(ok, _rel_bucket_np(np.clip(dist, 0, WINDOW)), -1).astype(np.int32)


def _bias_body(rel_ref, bucket_ref, o_ref):
    bucket = bucket_ref[0]

    def per_head(h, carry):
        acc = jnp.full(bucket.shape, NEG, F32)
        for bkt in range(REL_BUCKETS):
            acc = jnp.where(bucket == bkt, rel_ref[bkt * ATTN_HEADS + h], acc)
        o_ref[0, h] = acc
        return carry

    lax.fori_loop(0, ATTN_HEADS, per_head, 0)


def _bias_tables(rel_flat, buckets):
    nv, lq, lk = buckets.shape
    return pl.pallas_call(
        _bias_body, grid=(nv,),
        in_specs=[pl.BlockSpec(memory_space=pltpu.SMEM), pl.BlockSpec((1, lq, lk), lambda v: (v, 0, 0))],
        out_specs=pl.BlockSpec((1, ATTN_HEADS, lq, lk), lambda v: (v, 0, 0, 0)),
        out_shape=jax.ShapeDtypeStruct((nv, ATTN_HEADS, lq, lk), F32),
        compiler_params=_params("arbitrary"), name="rel_bias")(rel_flat, buckets)


def _bias_t_body(rel_ref, bucket_ref, o_ref):
    variants = [bucket_ref[v] for v in range(bucket_ref.shape[0])]
    union = functools.reduce(jnp.maximum, variants)
    lq = union.shape[1]

    def per_kv_head(n, carry):
        for g in range(Q_PER_KV):
            acc = jnp.full(union.shape, NEG, F32)
            for bkt in range(REL_BUCKETS):
                acc = jnp.where(union == bkt, rel_ref[bkt * ATTN_HEADS + n * Q_PER_KV + g], acc)
            for v, bucket in enumerate(variants):
                o_ref[v, n, :, g * lq:(g + 1) * lq] = jnp.where(bucket >= 0, acc, NEG)
        return carry

    lax.fori_loop(0, ATTN_KV_HEADS, per_kv_head, 0)


def _bias_tables_t(rel_flat, buckets_t):
    nv, lk, lq = buckets_t.shape
    out_dims = (nv, ATTN_KV_HEADS, lk, Q_PER_KV * lq)
    return pl.pallas_call(
        _bias_t_body,
        in_specs=[pl.BlockSpec(memory_space=pltpu.SMEM), pl.BlockSpec(memory_space=pltpu.VMEM)],
        out_specs=pl.BlockSpec(memory_space=pltpu.VMEM),
        out_shape=jax.ShapeDtypeStruct(out_dims, F32),
        compiler_params=pltpu.CompilerParams(vmem_limit_bytes=VMEM_LIMIT), name="rel_bias_t")(rel_flat, buckets_t)


def _head_rmsnorm(x, g_mat, e_mat, w):
    ms = _dot_sel(x * x, g_mat) * (1.0 / ATTN_HEAD_DIM)
    return x * _dot_sel(lax.rsqrt(ms + EPS), e_mat) * w


def _lane_head(shape):
    return lax.broadcasted_iota(jnp.int32, shape, 1) // ATTN_HEAD_DIM


def _sink_column(sink_ref, n, rows_per_head):
    return jnp.concatenate(
        [jnp.full((rows_per_head, 1), sink_ref[n * Q_PER_KV + g], F32) for g in range(Q_PER_KV)], axis=0)


def _softmax_with_sink(s, sink):
    m = jnp.maximum(jnp.max(s, axis=-1, keepdims=True), sink)
    p = jnp.exp(s - m)
    denom = jnp.sum(p, axis=-1, keepdims=True) + jnp.exp(sink - m)
    return p / denom


def _attn_block(q_blk, kcat, vcat_t, bias_at, sink_ref):
    T = CHUNK
    lane_head = _lane_head((2 * T, KV_WIDTH))
    row_head = lax.broadcasted_iota(jnp.int32, (KV_WIDTH, 2 * T), 0) // ATTN_HEAD_DIM
    zero = jnp.zeros((2 * T, KV_WIDTH), BF16)
    head = lambda h: q_blk[h * ATTN_HEAD_DIM:(h + 1) * ATTN_HEAD_DIM]
    q_cols = jnp.concatenate(
        [jnp.concatenate([head(n * Q_PER_KV + g) for n in range(ATTN_KV_HEADS)], axis=0)
         for g in range(Q_PER_KV)], axis=1)
    probs, vals = [], []
    for n in range(ATTN_KV_HEADS):
        s = _dot(jnp.where(lane_head == n, kcat, zero), q_cols)
        sink = sink_ref[n]
        cols = []
        for g in range(Q_PER_KV):
            c = slice(g * T, (g + 1) * T)
            sg = s[:, c] + bias_at(n, c)
            m = jnp.maximum(jnp.max(sg, axis=0, keepdims=True), sink[:, c])
            p = jnp.exp(sg - m)
            denom = jnp.sum(p, axis=0, keepdims=True) + jnp.exp(sink[:, c] - m)
            cols.append((p * (1.0 / denom)).astype(BF16))
        probs.append(jnp.concatenate(cols, axis=1))
        vals.append(jnp.where(row_head == n, vcat_t, zero.T))
    o_t = _dot(jnp.concatenate(vals, axis=1), jnp.concatenate(probs, axis=0))
    return jnp.concatenate(
        [o_t[n * ATTN_HEAD_DIM:(n + 1) * ATTN_HEAD_DIM, g * T:(g + 1) * T]
         for n in range(ATTN_KV_HEADS) for g in range(Q_PER_KV)], axis=0)


def _mixer_body(gz_ref, xs_ref, b_ref, c_ref, dt_ref, alog_ref, dskip_ref, nw_ref, e_ref,
                qt_ref, k_ref, v_ref, gzt_ref, qwt_ref, kw_ref, g_ref, bias_ref, sink_ref, x_ref, wo_ref,
                y_ref, st_ref, kn_ref, vn_ref,
                state_sc, kcat_sc, vcat_t_sc, yssm_sc, yattn_t_sc):
    T = CHUNK
    step = pl.program_id(1)
    cols_step = CHUNKS_PER_STEP * T

    @pl.when(step == 0)
    def _():
        state_sc[...] = jnp.zeros_like(state_sc)
        kcat_sc[0:T, :] = jnp.zeros((T, KV_WIDTH), BF16)
        vcat_t_sc[:, 0:T] = jnp.zeros((KV_WIDTH, T), BF16)

    q3 = qt_ref[...].reshape(ATTN_HEADS, ATTN_HEAD_DIM, cols_step)
    ms = jnp.mean(q3 * q3, axis=1, keepdims=True)
    qn = ((q3 * lax.rsqrt(ms + EPS)).reshape(ATTN_WIDTH, cols_step) * qwt_ref[...]).astype(BF16)
    e_mat = e_ref[...]
    kn = _head_rmsnorm(k_ref[...], g_ref[...], e_mat[:, :KV_WIDTH], kw_ref[...])
    v = v_ref[...]
    kn_ref[0] = kn[cols_step - T:]
    vn_ref[0] = v[cols_step - T:]
    kcat_sc[T:, :] = kn.astype(BF16)
    vcat_t_sc[:, T:] = v.T.astype(BF16)

    a_neg = -jnp.exp(alog_ref[...])
    state = state_sc[...]
    first_variant = jnp.minimum(step, 1)
    for j in range(CHUNKS_PER_STEP):
        r = slice(j * T, (j + 1) * T)
        y, state = _ssd_chunk(gz_ref[r, :], xs_ref[r, :], b_ref[r, :], c_ref[r, :], dt_ref[r, :], a_neg,
                              dskip_ref[...], nw_ref[...], e_mat, state)
        yssm_sc[r, :] = y.astype(BF16)
        variant = first_variant if j == 0 else 1
        y_t = _attn_block(qn[:, r], kcat_sc[j * T:(j + 2) * T, :], vcat_t_sc[:, j * T:(j + 2) * T],
                          lambda n, c, variant=variant: bias_ref[variant, n, :, c], sink_ref)
        yattn_t_sc[:, r] = (y_t * gzt_ref[:, r]).astype(BF16)
    state_sc[...] = state
    kcat_sc[0:T, :] = kcat_sc[cols_step:, :]
    vcat_t_sc[:, 0:T] = vcat_t_sc[:, cols_step:]

    y_ref[...] = (x_ref[...] + _dot(yssm_sc[...], wo_ref[:SSM_WIDTH, :])
                  + _dot_tn(yattn_t_sc[...], wo_ref[SSM_WIDTH:, :]))

    @pl.when(step == pl.num_programs(1) - 1)
    def _():
        st_ref[0] = state.T


def _mixer(gz, xs, b, c, dt, alog, dskip, norm_w, e_mat, q_t, k, v, gz_t, qw_t, kw, g_mat, bias_t, sink_rows,
           x2d, w_out, batch, seq):
    step_rows = CHUNKS_PER_STEP * CHUNK
    ns = seq // step_rows
    row = lambda w: pl.BlockSpec((step_rows, w), lambda b, i: (b * ns + i, 0))
    col = pl.BlockSpec((ATTN_WIDTH, step_rows), lambda b, i: (0, b * ns + i))
    resident = lambda a: pl.BlockSpec(a.shape, lambda b, i: (0,) * a.ndim, pipeline_mode=pl.Buffered(1))
    in_specs = [row(SSM_WIDTH), row(SSM_WIDTH), row(BC_WIDTH), row(BC_WIDTH), row(LANES),
                _const_spec((1, LANES)), _const_spec((1, SSM_WIDTH)), _const_spec((1, SSM_WIDTH)), resident(e_mat),
                col, row(KV_WIDTH), row(KV_WIDTH), col, resident(qw_t), _const_spec((1, KV_WIDTH)),
                resident(g_mat), resident(bias_t), _const_spec(sink_rows.shape), row(D_MODEL), resident(w_out)]
    kv_out = pl.BlockSpec((1, CHUNK, KV_WIDTH), lambda b, i: (b, 0, 0))
    out_specs = [row(D_MODEL), pl.BlockSpec((1, SSM_WIDTH, SSM_STATE), lambda b, i: (b, 0, 0)), kv_out, kv_out]
    out_shape = [jax.ShapeDtypeStruct((batch * seq, D_MODEL), F32),
                 jax.ShapeDtypeStruct((batch, SSM_WIDTH, SSM_STATE), F32),
                 jax.ShapeDtypeStruct((batch, CHUNK, KV_WIDTH), F32),
                 jax.ShapeDtypeStruct((batch, CHUNK, KV_WIDTH), F32)]
    scratch = [pltpu.VMEM((SSM_STATE, SSM_WIDTH), F32),
               pltpu.VMEM((step_rows + CHUNK, KV_WIDTH), BF16), pltpu.VMEM((KV_WIDTH, step_rows + CHUNK), BF16),
               pltpu.VMEM((step_rows, SSM_WIDTH), BF16), pltpu.VMEM((ATTN_WIDTH, step_rows), BF16)]
    return pl.pallas_call(
        _mixer_body, grid=(batch, ns), in_specs=in_specs, out_specs=out_specs, out_shape=out_shape,
        scratch_shapes=scratch, compiler_params=_params("arbitrary", "arbitrary"), name="mixer")(
            gz, xs, b, c, dt, alog, dskip, norm_w, e_mat, q_t, k, v, gz_t, qw_t, kw, g_mat, bias_t, sink_rows,
            x2d, w_out)


def _ssd_sample_vec_body(z_ref, xbc_ref, dt_ref, sconv_ref, cw_ref, cb_ref, dtb_ref, alog_ref, dskip_ref,
                         gh_ref, e_ref, ypart_ref, ea_ref, xw_ref, b_ref, c_ref, cdec_ref, convnew_ref):
    steps = xbc_ref.shape[0]
    tail = CONV_WIDTH - 1
    full = [sconv_ref[j] for j in range(tail)] + [xbc_ref[l] for l in range(steps)]
    for j in range(tail):
        convnew_ref[j] = full[steps + j]
    gh = gh_ref[...]
    e_mat = e_ref[...]
    a_neg = -jnp.exp(alog_ref[...])
    xs, bm, cm, dts, acum = [], [], [], [], []
    run = None
    for l in range(steps):
        conv = cb_ref[...]
        for tap in range(CONV_WIDTH):
            conv = conv + full[l + tap] * cw_ref[tap:tap + 1, :]
        act = _silu(conv)
        xs.append(act[:, :SSM_WIDTH])
        bm.append(act[:, SSM_WIDTH:SSM_WIDTH + BC_WIDTH])
        cm.append(act[:, SSM_WIDTH + BC_WIDTH:])
        d = _softplus(dt_ref[l] + dtb_ref[...])
        dts.append(d)
        run = d * a_neg if run is None else run + d * a_neg
        acum.append(run)
        b_ref[l] = bm[l]
        c_ref[l] = cm[l]
    for l in range(steps):
        y = dskip_ref[...] * xs[l]
        for s in range(l + 1):
            cb_h = _dot_sel(cm[l] * bm[s], gh)
            coef = cb_h * jnp.exp(acum[l] - acum[s]) * dts[s]
            y = y + _dot_sel(coef, e_mat) * xs[s]
        ypart_ref[l] = y
        ea_ref[l] = _dot_sel(jnp.exp(acum[l]), e_mat)
        xw_ref[l] = xs[l] * _dot_sel(dts[l] * jnp.exp(acum[steps - 1] - acum[l]), e_mat)
    cdec_ref[...] = jnp.exp(acum[steps - 1])


def _ssd_sample_vec(z3, xbc3, dt3, sconv3, conv_w, conv_b, dtb, alog, dskip, gh_mat, e_mat):
    steps, nb = z3.shape[0], z3.shape[1]
    f = lambda *s: jax.ShapeDtypeStruct(s, F32)
    out_shape = [f(steps, nb, SSM_WIDTH), f(steps, nb, SSM_WIDTH), f(steps, nb, SSM_WIDTH),
                 f(steps, nb, BC_WIDTH), f(steps, nb, BC_WIDTH), f(nb, LANES), f(CONV_WIDTH - 1, nb, CONV_DIM)]
    return pl.pallas_call(_ssd_sample_vec_body, out_shape=out_shape,
                          compiler_params=pltpu.CompilerParams(vmem_limit_bytes=VMEM_LIMIT),
                          name="ssd_sample_vec")(
        z3, xbc3, dt3, sconv3, conv_w, conv_b, dtb, alog, dskip, gh_mat, e_mat)


def _ssd_sample_state_body(cdec_ref, st_ref, c_ref, b_ref, xw_ref, new_ref, yoff_ref):
    i = pl.program_id(0)
    heads_per_group = SSM_HEADS // SSM_GROUPS
    for j in range(SAMPLE_BATCH_BLOCK):
        st = st_ref[j]
        cb_bf = c_ref[:, j, :].astype(BF16)
        bb_bf = b_ref[:, j, :].astype(BF16)
        xw_bf = xw_ref[:, j, :].astype(BF16)
        y_parts = []
        for g in range(SSM_GROUPS):
            rows = slice(g * GROUP_WIDTH, (g + 1) * GROUP_WIDTH)
            ns = slice(g * SSM_STATE, (g + 1) * SSM_STATE)
            y_parts.append(_dot_nt(cb_bf[:, ns], st[rows].astype(BF16)))
            upd = _dot_tn(xw_bf[:, rows], bb_bf[:, ns])
            for hh in range(heads_per_group):
                h = g * heads_per_group + hh
                r = slice(h * SSM_HEAD_DIM, (h + 1) * SSM_HEAD_DIM)
                dec = cdec_ref[(i * SAMPLE_BATCH_BLOCK + j) * SSM_HEADS + h]
                new_ref[j, r, :] = st[r] * dec + upd[hh * SSM_HEAD_DIM:(hh + 1) * SSM_HEAD_DIM]
        yoff_ref[:, j, :] = jnp.concatenate(y_parts, axis=1)


def _ssd_sample_state(cdec_flat, state, c3, b3, xw3):
    steps, nb = c3.shape[0], c3.shape[1]
    bb = SAMPLE_BATCH_BLOCK
    tok = lambda w: pl.BlockSpec((steps, bb, w), lambda i: (0, i, 0))
    st_spec = pl.BlockSpec((bb, SSM_WIDTH, SSM_STATE), lambda i: (i, 0, 0))
    return pl.pallas_call(
        _ssd_sample_state_body, grid=(nb // bb,),
        in_specs=[pl.BlockSpec(memory_space=pltpu.SMEM), st_spec, tok(BC_WIDTH), tok(BC_WIDTH), tok(SSM_WIDTH)],
        out_specs=[st_spec, tok(SSM_WIDTH)],
        out_shape=[jax.ShapeDtypeStruct(state.shape, F32), jax.ShapeDtypeStruct((steps, nb, SSM_WIDTH), F32)],
        compiler_params=_params("parallel"), name="ssd_sample_state")(cdec_flat, state, c3, b3, xw3)


def _sample_finish_body(ypart_ref, yoff_ref, ea_ref, z_ref, nw_ref, q_ref, k_ref, qw_ref, kw_ref, g_ref, e_ref,
                        yssm_ref, qn_ref, kn_ref):
    y = ypart_ref[...] + yoff_ref[...] * ea_ref[...]
    yssm_ref[...] = _group_rmsnorm(y * _silu(z_ref[...]), nw_ref[...])
    g_mat = g_ref[...]
    e_mat = e_ref[...]
    qn = _head_rmsnorm(q_ref[...], g_mat, e_mat, qw_ref[...]) * ATTN_SCALE
    head = lambda h: qn[:, h * ATTN_HEAD_DIM:(h + 1) * ATTN_HEAD_DIM]
    qn_ref[...] = jnp.concatenate(
        [head(n * Q_PER_KV + g) for g in range(Q_PER_KV) for n in range(ATTN_KV_HEADS)], axis=1)
    kn_ref[...] = _head_rmsnorm(k_ref[...], g_mat[:KV_WIDTH], e_mat[:, :KV_WIDTH], kw_ref[...])


def _sample_finish(ypart, yoff, ea, z, norm_w, q, k, qw, kw, g_mat, e_mat):
    rows = z.shape[0]
    f = lambda w: jax.ShapeDtypeStruct((rows, w), F32)
    return pl.pallas_call(_sample_finish_body, out_shape=[f(SSM_WIDTH), f(ATTN_WIDTH), f(KV_WIDTH)],
                          compiler_params=pltpu.CompilerParams(vmem_limit_bytes=VMEM_LIMIT),
                          name="sample_finish")(ypart, yoff, ea, z, norm_w, q, k, qw, kw, g_mat, e_mat)


def _attn_sample_body(q_ref, kn_ref, vn_ref, z_ref, ckt_ref, cvt_ref, biasc_ref, biasn_ref,
                      y_ref, kot_ref, vot_ref):
    steps = q_ref.shape[0]
    bb = SAMPLE_BATCH_BLOCK
    blk = Q_PER_KV * steps
    rows = ATTN_KV_HEADS * blk
    pad = jnp.zeros((SUBLANES - steps, KV_WIDTH), F32)
    lane_head = _lane_head((blk, KV_WIDTH))
    zero = jnp.zeros((blk, KV_WIDTH), F32)

    s_c, s_n, k8, v8 = [], [], [], []
    for j in range(bb):
        q = q_ref[:, j, :]
        qg = jnp.concatenate([q[:, g * KV_WIDTH:(g + 1) * KV_WIDTH] for g in range(Q_PER_KV)], axis=0)
        qx = jnp.concatenate([jnp.where(lane_head == n, qg, zero) for n in range(ATTN_KV_HEADS)], axis=0)
        qx = qx.astype(BF16)
        k8.append(jnp.concatenate([kn_ref[:, j, :], pad], axis=0))
        v8.append(jnp.concatenate([vn_ref[:, j, :], pad], axis=0))
        s_c.append(_dot(qx, ckt_ref[j].astype(BF16)))
        s_n.append(_dot_nt(qx, k8[j].astype(BF16)))
    s_c = jnp.concatenate(s_c, axis=0) + biasc_ref[...]
    s_n = jnp.concatenate(s_n, axis=0) + biasn_ref[...]
    m = jnp.maximum(jnp.max(s_c, axis=-1, keepdims=True), jnp.max(s_n, axis=-1, keepdims=True))
    p_c = jnp.exp(s_c - m)
    p_n = jnp.exp(s_n - m)
    inv = 1.0 / (jnp.sum(p_c, axis=-1, keepdims=True) + jnp.sum(p_n, axis=-1, keepdims=True))
    p_c = (p_c * inv).astype(BF16)
    p_n = (p_n * inv).astype(BF16)

    lane = lax.broadcasted_iota(jnp.int32, (KV_WIDTH, WINDOW), 1)
    for j in range(bb):
        r = slice(j * rows, (j + 1) * rows)
        o = _dot_nt(p_c[r], cvt_ref[j].astype(BF16)) + _dot(p_n[r], v8[j].astype(BF16))
        og = zero
        for n in range(ATTN_KV_HEADS):
            og = og + jnp.where(lane_head == n, o[n * blk:(n + 1) * blk], zero)
        y = jnp.concatenate(
            [og[g * steps:(g + 1) * steps, n * ATTN_HEAD_DIM:(n + 1) * ATTN_HEAD_DIM]
             for n in range(ATTN_KV_HEADS) for g in range(Q_PER_KV)], axis=1)
        y_ref[:, j, :] = y * _silu(z_ref[:, j, :])

        for new8, old_ref, out_ref in ((k8[j], ckt_ref, kot_ref), (v8[j], cvt_ref, vot_ref)):
            tail_rows = jnp.concatenate([new8[steps:], new8[:steps]], axis=0)
            block = jnp.concatenate([jnp.zeros((WINDOW - SUBLANES, KV_WIDTH), F32), tail_rows], axis=0)
            shifted = pltpu.roll(old_ref[j], WINDOW - steps, axis=1)
            out_ref[j] = jnp.where(lane >= WINDOW - steps, block.T, shifted)


def _attn_sample(q3, kn3, vn3, z3, cache_kt, cache_vt, bias_c, bias_n):
    steps, nb = q3.shape[0], q3.shape[1]
    bb = SAMPLE_BATCH_BLOCK
    tok = lambda w: pl.BlockSpec((steps, bb, w), lambda i: (0, i, 0))
    cache_spec = pl.BlockSpec((bb, KV_WIDTH, WINDOW), lambda i: (i, 0, 0))
    return pl.pallas_call(
        _attn_sample_body, grid=(nb // bb,),
        in_specs=[tok(ATTN_WIDTH), tok(KV_WIDTH), tok(KV_WIDTH), tok(ATTN_WIDTH), cache_spec, cache_spec,
                  _const_spec(bias_c.shape), _const_spec(bias_n.shape)],
        out_specs=[tok(ATTN_WIDTH), cache_spec, cache_spec],
        out_shape=[jax.ShapeDtypeStruct((steps, nb, ATTN_WIDTH), F32),
                   jax.ShapeDtypeStruct(cache_kt.shape, F32), jax.ShapeDtypeStruct(cache_vt.shape, F32)],
        compiler_params=_params("parallel"), name="attn_sample")(
            q3, kn3, vn3, z3, cache_kt, cache_vt, bias_c, bias_n)


def _static_tables(steps):
    lanes = np.arange(ATTN_WIDTH)
    g_mat = np.zeros((ATTN_WIDTH, LANES), np.float32)
    g_mat[lanes, lanes // ATTN_HEAD_DIM] = 1.0
    e_mat = g_mat.T.copy()
    bc = np.arange(BC_WIDTH)
    gh_mat = np.zeros((BC_WIDTH, LANES), np.float32)
    for h in range(SSM_HEADS):
        gh_mat[bc // SSM_STATE == h // (SSM_HEADS // SSM_GROUPS), h] = 1.0
    T = CHUNK
    dist = np.arange(T)[:, None] - (np.arange(2 * T) - T)[None, :]
    first = np.broadcast_to((np.arange(2 * T) >= T)[None, :], dist.shape)
    prompt_buckets = np.stack([_bucket_or_masked(dist, first), _bucket_or_masked(dist)])
    dist_c = (np.arange(steps) + WINDOW)[:, None] - np.arange(WINDOW)[None, :]
    dist_n = np.arange(steps)[:, None] - np.arange(SUBLANES)[None, :]
    real = np.broadcast_to((np.arange(SUBLANES) < steps)[None, :], dist_n.shape)
    return dict(g=g_mat, e=e_mat, gh=gh_mat, prompt_buckets=prompt_buckets,
                cache_buckets=_bucket_or_masked(dist_c)[None], new_buckets=_bucket_or_masked(dist_n, real)[None])


def kernel(x_prompt, x_sample, cache_k, cache_v, state_ssm, state_conv, norm_w, w_in, conv_w, conv_b, dt_bias,
           a_log, d_skip, ssm_norm_w, q_norm_w, k_norm_w, sinks, rel_table, w_out):
    assert w_in.shape[0] == 1, "single-layer kernel"
    batch, seq, _ = x_prompt.shape
    nb, steps, _ = x_sample.shape
    tab = _static_tables(steps)
    g_mat = jnp.asarray(tab["g"], BF16)
    e_mat = jnp.asarray(tab["e"], BF16)
    gh_mat = jnp.asarray(tab["gh"], BF16)

    w_t = jnp.transpose(w_in[0]).astype(BF16)
    wo_all = w_out[0].astype(BF16)

    row = lambda v, width: jnp.pad(v.reshape(1, -1), ((0, 0), (0, width - v.size)))
    nw = row(norm_w[0], D_MODEL)
    cw = conv_w[0]
    cb = row(conv_b[0], CONV_DIM)
    dtb = row(dt_bias[0], LANES)
    alog = row(a_log[0], LANES)
    dskip = jnp.repeat(d_skip[0], SSM_HEAD_DIM).reshape(1, SSM_WIDTH)
    snw = row(ssm_norm_w[0], SSM_WIDTH)
    qw = jnp.tile(q_norm_w[0], ATTN_HEADS).reshape(1, ATTN_WIDTH)
    kw = jnp.tile(k_norm_w[0], ATTN_KV_HEADS).reshape(1, KV_WIDTH)
    sink = sinks[0]
    rel_flat = rel_table.reshape(-1)

    xp = x_prompt.reshape(batch * seq, D_MODEL)
    gz, xs_p, b_p, c_p, dt_p, q_t, k, v, gza_t, tail_p = _inproj_prompt(
        xp, nw, w_t, cw, cb, dtb, batch, seq)
    bias_t = _bias_tables_t(rel_flat, jnp.asarray(tab["prompt_buckets"].transpose(0, 2, 1)))
    qw_t = jnp.broadcast_to((qw * ATTN_SCALE).reshape(ATTN_WIDTH, 1), (ATTN_WIDTH, CHUNKS_PER_STEP * CHUNK))
    sink_rows = jnp.repeat(sink.reshape(ATTN_KV_HEADS, Q_PER_KV), CHUNK, axis=1).reshape(ATTN_KV_HEADS, 1, -1)
    y_p, st_p, k_p, v_p = _mixer(gz, xs_p, b_p, c_p, dt_p, alog, dskip, snw, e_mat, q_t, k, v, gza_t, qw_t, kw,
                                 g_mat[:KV_WIDTH], bias_t, sink_rows, xp, wo_all, batch, seq)
    y_p = y_p.reshape(batch, seq, D_MODEL)
    conv_p = tail_p[:, SUBLANES - (CONV_WIDTH - 1):, :]

    xs = jnp.swapaxes(x_sample, 0, 1).reshape(steps * nb, D_MODEL)
    z, xbc, dt, q, k, v, za = _inproj(xs, nw, w_t)
    t3 = lambda a: a.reshape(steps, nb, a.shape[-1])
    sconv3 = jnp.swapaxes(state_conv[0], 0, 1)
    ypart, ea, xw, b3, c3, cdec, conv_s3 = _ssd_sample_vec(
        t3(z), t3(xbc), t3(dt), sconv3, cw, cb, dtb, alog, dskip, gh_mat, e_mat)
    state_in = state_ssm[0].reshape(nb, SSM_WIDTH, SSM_STATE)
    st_s, yoff = _ssd_sample_state(cdec[:, :SSM_HEADS].reshape(-1), state_in, c3, b3, xw)
    f2 = lambda a: a.reshape(steps * nb, a.shape[-1])
    y_ssm, qn, kn = _sample_finish(f2(ypart), f2(yoff), f2(ea), z, snw, q, k, qw, kw, g_mat, e_mat)
    bias_c = _bias_tables(rel_flat, jnp.asarray(tab["cache_buckets"])).reshape(ATTN_HEADS * steps, WINDOW)
    bias_n = _bias_tables(rel_flat, jnp.asarray(tab["new_buckets"])).reshape(ATTN_HEADS * steps, SUBLANES)
    bias_n = bias_n.at[:, steps].set(jnp.repeat(sink, steps))
    bias_c = jnp.tile(bias_c, (SAMPLE_BATCH_BLOCK, 1))
    bias_n = jnp.tile(bias_n, (SAMPLE_BATCH_BLOCK, 1))
    to_t = lambda a: jnp.transpose(a[0], (0, 2, 3, 1)).reshape(nb, KV_WIDTH, WINDOW)
    from_t = lambda a: jnp.transpose(a.reshape(nb, ATTN_KV_HEADS, ATTN_HEAD_DIM, WINDOW), (0, 3, 1, 2))[None]
    y_attn3, k_st, v_st = _attn_sample(t3(qn), t3(kn), t3(v), t3(za), to_t(cache_k), to_t(cache_v),
                                       bias_c, bias_n)
    k_s, v_s = from_t(k_st), from_t(v_st)
    y_s = _outproj(y_ssm, f2(y_attn3), xs, wo_all)
    y_s = jnp.swapaxes(y_s.reshape(steps, nb, D_MODEL), 0, 1)

    kv5 = lambda a: a.reshape(1, a.shape[0], WINDOW, ATTN_KV_HEADS, ATTN_HEAD_DIM)
    st5 = lambda a: a.reshape(1, a.shape[0], SSM_HEADS, SSM_HEAD_DIM, SSM_STATE)
    return (y_p, y_s, kv5(k_p), kv5(v_p), st5(st_p), conv_p[None],
            k_s, v_s, st5(st_s), jnp.swapaxes(conv_s3, 0, 1)[None])
```

```python
import functools
import math

import numpy as np
import jax
import jax.numpy as jnp
from jax import lax
from jax.experimental import pallas as pl
from jax.experimental.pallas import tpu as pltpu

F32 = jnp.float32
BF16 = jnp.bfloat16

D_MODEL = 1024
SSM_HEADS = 16
SSM_HEAD_DIM = 64
SSM_WIDTH = SSM_HEADS * SSM_HEAD_DIM
SSM_GROUPS = 2
SSM_STATE = 128
GROUP_WIDTH = SSM_WIDTH // SSM_GROUPS
BC_WIDTH = SSM_GROUPS * SSM_STATE
CONV_WIDTH = 4
CONV_DIM = SSM_WIDTH + 2 * BC_WIDTH
CHUNK = 128
ATTN_HEADS = 16
ATTN_KV_HEADS = 4
Q_PER_KV = ATTN_HEADS // ATTN_KV_HEADS
ATTN_HEAD_DIM = 64
ATTN_WIDTH = ATTN_HEADS * ATTN_HEAD_DIM
KV_WIDTH = ATTN_KV_HEADS * ATTN_HEAD_DIM
WINDOW = 128
ATTN_SCALE = ATTN_HEAD_DIM ** -0.5
REL_BUCKETS = 32
REL_MAX_DIST = 128
EPS = 1e-6
LOG2E = 1.0 / math.log(2.0)
NEG = -1e30

LANES = 128
SUBLANES = 8
MXU_WIDTH = 256
VMEM_LIMIT = 56 * 1024 * 1024
def _in_proj_rows():
    widths = (("z", SSM_WIDTH), ("xbc", CONV_DIM), ("dt", SSM_HEADS), ("q", ATTN_WIDTH), ("k", KV_WIDTH),
              ("v", KV_WIDTH), ("za", ATTN_WIDTH))
    rows, start = {}, 0
    for name, width in widths:
        rows[name] = slice(start, start + width)
        start += width
    return rows


IN_ROWS = _in_proj_rows()
DT_ROWS = slice(IN_ROWS["dt"].start, IN_ROWS["dt"].start + LANES)

PROJ_ROWS = 512
CHUNKS_PER_STEP = 4
SAMPLE_BATCH_BLOCK = 8


def _dot(a, b):
    return jnp.dot(a, b, preferred_element_type=F32)


def _dot_nt(a, b):
    return lax.dot_general(a, b, (((1,), (1,)), ((), ())), preferred_element_type=F32)


def _dot_tn(a, b):
    return lax.dot_general(a, b, (((0,), (0,)), ((), ())), preferred_element_type=F32)


def _split2(v):
    hi = v.astype(BF16)
    lo = (v - hi.astype(F32)).astype(BF16)
    return hi, lo


def _dot_sel(v, m):
    hi, lo = _split2(v)
    return _dot(hi, m) + _dot(lo, m)


def _dot_sel3(m, v):
    hi = v.astype(BF16)
    r1 = v - hi.astype(F32)
    mid = r1.astype(BF16)
    lo = (r1 - mid.astype(F32)).astype(BF16)
    return _dot(m, hi) + _dot(m, mid) + _dot(m, lo)


def _silu(x):
    return x / (1.0 + jnp.exp(-x))


def _softplus(x):
    return jnp.maximum(x, 0.0) + jnp.log1p(jnp.exp(-jnp.abs(x)))


def _params(*sem):
    return pltpu.CompilerParams(dimension_semantics=sem, vmem_limit_bytes=VMEM_LIMIT)


def _const_spec(shape):
    nd = len(shape)
    return pl.BlockSpec(shape, lambda *_: (0,) * nd)


def _normed_input(x_ref, nw_ref):
    x = x_ref[...]
    ms = jnp.mean(x * x, axis=-1, keepdims=True)
    return (x * lax.rsqrt(ms + EPS) * nw_ref[...]).astype(BF16)


def _dt_projection(h, wt_ref):
    raw = _dot_nt(h, wt_ref[DT_ROWS, :])
    return jnp.where(lax.broadcasted_iota(jnp.int32, raw.shape, 1) < SSM_HEADS, raw, 0.0)


def _inproj_body(x_ref, nw_ref, wt_ref, *out_refs):
    h = _normed_input(x_ref, nw_ref)
    for name, o_ref in zip(IN_ROWS, out_refs):
        o_ref[...] = _dt_projection(h, wt_ref) if name == "dt" else _dot_nt(h, wt_ref[IN_ROWS[name], :])


def _inproj(x2d, norm_w, w_t):
    rows = x2d.shape[0]
    tm = min(PROJ_ROWS, rows)
    widths = [LANES if name == "dt" else r.stop - r.start for name, r in IN_ROWS.items()]
    in_specs = [pl.BlockSpec((tm, D_MODEL), lambda i: (i, 0)), _const_spec((1, D_MODEL)),
                pl.BlockSpec(w_t.shape, lambda i: (0, 0), pipeline_mode=pl.Buffered(1))]
    out_specs = [pl.BlockSpec((tm, w), lambda i: (i, 0)) for w in widths]
    out_shape = [jax.ShapeDtypeStruct((rows, w), F32) for w in widths]
    return pl.pallas_call(
        _inproj_body, grid=(rows // tm,), in_specs=in_specs, out_specs=out_specs, out_shape=out_shape,
        compiler_params=_params("parallel"), name="inproj")(x2d, norm_w, w_t)


def _shift_rows(u, prev_tail, k):
    rows, width = u.shape
    tiles = jnp.concatenate([prev_tail, u], axis=0).reshape(rows // SUBLANES + 1, SUBLANES, width)
    rot = pltpu.roll(tiles, k, axis=1)
    first = lax.broadcasted_iota(jnp.int32, (1, SUBLANES, width), 1) < k
    return jnp.where(first, rot[:-1], rot[1:]).reshape(rows, width)


def _inproj_prompt_body(steps_per_seq, x_ref, nw_ref, wt_ref, cw_ref, cb_ref, dtb_ref,
                        gz_ref, xs_ref, b_ref, c_ref, dt_ref, qt_ref, k_ref, v_ref, gzat_ref, tail_ref, tail_sc):
    @pl.when(pl.program_id(0) % steps_per_seq == 0)
    def _():
        tail_sc[...] = jnp.zeros_like(tail_sc)

    h = _normed_input(x_ref, nw_ref)
    rows = h.shape[0]
    w_tile = lambda name, j: wt_ref[IN_ROWS[name].start + j * MXU_WIDTH:IN_ROWS[name].start + (j + 1) * MXU_WIDTH, :]
    n_side = SSM_WIDTH // MXU_WIDTH
    for j in range(CONV_DIM // MXU_WIDTH):
        cols = slice(j * MXU_WIDTH, (j + 1) * MXU_WIDTH)
        partner = w_tile("z", j) if j < n_side else wt_ref[IN_ROWS["k" if j == n_side else "v"], :]
        both = _dot_nt(h, jnp.concatenate([w_tile("xbc", j), partner], axis=0))
        u = both[:, :MXU_WIDTH]
        if j < n_side:
            gz_ref[:, cols] = _silu(both[:, MXU_WIDTH:])
        elif j == n_side:
            k_ref[...] = both[:, MXU_WIDTH:]
        else:
            v_ref[...] = both[:, MXU_WIDTH:]
        prev_tail = tail_sc[:, cols]
        conv = cb_ref[:, cols] + u * cw_ref[CONV_WIDTH - 1:CONV_WIDTH, cols]
        for k in range(1, CONV_WIDTH):
            tap = CONV_WIDTH - 1 - k
            conv = conv + _shift_rows(u, prev_tail, k) * cw_ref[tap:tap + 1, cols]
        new_tail = u[rows - SUBLANES:, :]
        tail_sc[:, cols] = new_tail
        tail_ref[0, :, cols] = new_tail
        act = _silu(conv)
        if j < SSM_WIDTH // MXU_WIDTH:
            xs_ref[:, cols] = act
        elif j == SSM_WIDTH // MXU_WIDTH:
            b_ref[...] = act.astype(BF16)
        else:
            c_ref[...] = act.astype(BF16)

    for j in range(ATTN_WIDTH // MXU_WIDTH):
        feats = slice(j * MXU_WIDTH, (j + 1) * MXU_WIDTH)
        gzat_ref[feats, :] = _silu(_dot_nt(w_tile("za", j), h))
    dt_ref[...] = _softplus(_dt_projection(h, wt_ref) + dtb_ref[...])
    qt_ref[...] = _dot_nt(wt_ref[IN_ROWS["q"], :], h)


def _inproj_prompt(x2d, norm_w, w_t, conv_w, conv_b, dtb, batch, seq):
    rows = x2d.shape[0]
    tm = PROJ_ROWS
    steps_per_seq = seq // tm
    resident = lambda a: pl.BlockSpec(a.shape, lambda i: (0, 0), pipeline_mode=pl.Buffered(1))
    rowblk = lambda w: pl.BlockSpec((tm, w), lambda i: (i, 0))
    colblk = pl.BlockSpec((ATTN_WIDTH, tm), lambda i: (0, i))
    in_specs = ([rowblk(D_MODEL), _const_spec((1, D_MODEL)), resident(w_t)]
                + [_const_spec(conv_w.shape), _const_spec(conv_b.shape), _const_spec(dtb.shape)])
    out_specs = [rowblk(SSM_WIDTH), rowblk(SSM_WIDTH), rowblk(BC_WIDTH), rowblk(BC_WIDTH), rowblk(LANES),
                 colblk, rowblk(KV_WIDTH), rowblk(KV_WIDTH), colblk,
                 pl.BlockSpec((1, SUBLANES, CONV_DIM), lambda i: (i // steps_per_seq, 0, 0))]
    f = lambda r, c, dt=F32: jax.ShapeDtypeStruct((r, c), dt)
    out_shape = [f(rows, SSM_WIDTH), f(rows, SSM_WIDTH), f(rows, BC_WIDTH, BF16), f(rows, BC_WIDTH, BF16),
                 f(rows, LANES), f(ATTN_WIDTH, rows), f(rows, KV_WIDTH), f(rows, KV_WIDTH), f(ATTN_WIDTH, rows),
                 jax.ShapeDtypeStruct((batch, SUBLANES, CONV_DIM), F32)]
    return pl.pallas_call(
        functools.partial(_inproj_prompt_body, steps_per_seq), grid=(rows // tm,), in_specs=in_specs,
        out_specs=out_specs, out_shape=out_shape, scratch_shapes=[pltpu.VMEM((SUBLANES, CONV_DIM), F32)],
        compiler_params=_params("arbitrary"), name="inproj_prompt")(
            x2d, norm_w, w_t, conv_w, conv_b, dtb)


def _outproj_body(ys_ref, ya_ref, x_ref, w_ref, o_ref):
    o_ref[...] = (x_ref[...] + _dot(ys_ref[...].astype(BF16), w_ref[:SSM_WIDTH, :])
                  + _dot(ya_ref[...].astype(BF16), w_ref[SSM_WIDTH:, :]))


def _outproj(y_ssm, y_attn, x2d, w_out):
    rows = x2d.shape[0]
    tm = min(PROJ_ROWS, rows)
    row_spec = pl.BlockSpec((tm, D_MODEL), lambda i: (i, 0))
    w_spec = pl.BlockSpec(w_out.shape, lambda i: (0, 0), pipeline_mode=pl.Buffered(1))
    return pl.pallas_call(
        _outproj_body, grid=(rows // tm,), in_specs=[row_spec, row_spec, row_spec, w_spec],
        out_specs=row_spec, out_shape=jax.ShapeDtypeStruct((rows, D_MODEL), F32),
        compiler_params=_params("parallel"), name="outproj")(y_ssm, y_attn, x2d, w_out)


def _group_rmsnorm(gy, norm_w):
    parts = []
    for g in range(SSM_GROUPS):
        blk = gy[:, g * GROUP_WIDTH:(g + 1) * GROUP_WIDTH]
        ms = jnp.mean(blk * blk, axis=-1, keepdims=True)
        parts.append(blk * lax.rsqrt(ms + EPS))
    return jnp.concatenate(parts, axis=1) * norm_w


def _ssd_chunk(gz, xs, b_bf, c_bf, dt, a_neg, dskip, norm_w, e_mat, state):
    xs_bf = xs.astype(BF16)

    a = dt * a_neg
    li = lax.broadcasted_iota(jnp.int32, (CHUNK, CHUNK), 0)
    si = lax.broadcasted_iota(jnp.int32, (CHUNK, CHUNK), 1)
    causal = li >= si
    a_cum = _dot_sel3(jnp.where(causal, 1.0, 0.0).astype(BF16), a)
    a2 = a_cum * LOG2E
    row_term = a2.T - jnp.log2(dt.T)
    ea_full = _dot_sel(jnp.exp(a_cum), e_mat)
    w_full = _dot((dt * jnp.exp(a_cum[CHUNK - 1:CHUNK, :] - a_cum)).astype(BF16), e_mat)

    cb = [_dot_nt(c_bf[:, g * SSM_STATE:(g + 1) * SSM_STATE], b_bf[:, g * SSM_STATE:(g + 1) * SSM_STATE])
          for g in range(SSM_GROUPS)]
    half = lax.broadcasted_iota(jnp.int32, (CHUNK, LANES), 1) < SSM_HEAD_DIM
    heads_per_group = SSM_HEADS // SSM_GROUPS
    y_parts = []
    for pair in range(SSM_HEADS // 2):
        blocks = []
        for h in (2 * pair, 2 * pair + 1):
            seg = a2[:, h:h + 1] - row_term[h:h + 1, :]
            decay_dt = jnp.exp2(jnp.where(causal, seg, -jnp.inf))
            blocks.append((cb[h // heads_per_group] * decay_dt).astype(BF16))
        lhs = jnp.concatenate(blocks, axis=1)
        xp = xs_bf[:, pair * LANES:(pair + 1) * LANES]
        zero = jnp.zeros_like(xp)
        rhs = jnp.concatenate([jnp.where(half, xp, zero), jnp.where(half, zero, xp)], axis=0)
        y_parts.append(_dot(lhs, rhs))
    y_diag = jnp.concatenate(y_parts, axis=1)

    state_bf = state.astype(BF16)
    xw_bf = (xs * w_full).astype(BF16)
    y_off, upd = [], []
    for g in range(SSM_GROUPS):
        cols = slice(g * GROUP_WIDTH, (g + 1) * GROUP_WIDTH)
        ns = slice(g * SSM_STATE, (g + 1) * SSM_STATE)
        y_off.append(_dot(c_bf[:, ns], state_bf[:, cols]))
        upd.append(_dot_tn(b_bf[:, ns], xw_bf[:, cols]))
    y = y_diag + jnp.concatenate(y_off, axis=1) * ea_full + dskip * xs
    new_state = state * ea_full[CHUNK - 1:CHUNK, :] + jnp.concatenate(upd, axis=1)
    return _group_rmsnorm(y * gz, norm_w), new_state


def _rel_bucket_np(dist):
    max_exact = REL_BUCKETS // 2
    d_f = np.maximum(dist, 1).astype(np.float32)
    large = max_exact + (np.log(d_f / np.float32(max_exact)) / np.float32(math.log(REL_MAX_DIST / max_exact))
                         * np.float32(REL_BUCKETS - max_exact)).astype(np.int32)
    return np.where(dist < max_exact, dist, np.minimum(large, REL_BUCKETS - 1)).astype(np.int32)


def _bucket_or_masked(dist, extra_mask=None):
    ok = (dist >= 0) & (dist <= WINDOW)
    if extra_mask is not None:
        ok = ok & extra_mask
    return np.where(ok, _rel_bucket_np(np.clip(dist, 0, WINDOW)), -1).astype(np.int32)


def _bias_body(rel_ref, bucket_ref, o_ref):
    bucket = bucket_ref[0]

    def per_head(h, carry):
        acc = jnp.full(bucket.shape, NEG, F32)
        for bkt in range(REL_BUCKETS):
            acc = jnp.where(bucket == bkt, rel_ref[bkt * ATTN_HEADS + h], acc)
        o_ref[0, h] = acc
        return carry

    lax.fori_loop(0, ATTN_HEADS, per_head, 0)


def _bias_tables(rel_flat, buckets):
    nv, lq, lk = buckets.shape
    return pl.pallas_call(
        _bias_body, grid=(nv,),
        in_specs=[pl.BlockSpec(memory_space=pltpu.SMEM), pl.BlockSpec((1, lq, lk), lambda v: (v, 0, 0))],
        out_specs=pl.BlockSpec((1, ATTN_HEADS, lq, lk), lambda v: (v, 0, 0, 0)),
        out_shape=jax.ShapeDtypeStruct((nv, ATTN_HEADS, lq, lk), F32),
        compiler_params=_params("arbitrary"), name="rel_bias")(rel_flat, buckets)


def _bias_t_body(rel_ref, bucket_ref, o_ref):
    variants = [bucket_ref[v] for v in range(bucket_ref.shape[0])]
    union = functools.reduce(jnp.maximum, variants)
    lq = union.shape[1]

    def per_kv_head(n, carry):
        for g in range(Q_PER_KV):
            acc = jnp.full(union.shape, NEG, F32)
            for bkt in range(REL_BUCKETS):
                acc = jnp.where(union == bkt, rel_ref[bkt * ATTN_HEADS + n * Q_PER_KV + g], acc)
            for v, bucket in enumerate(variants):
                o_ref[v, n, :, g * lq:(g + 1) * lq] = jnp.where(bucket >= 0, acc, NEG)
        return carry

    lax.fori_loop(0, ATTN_KV_HEADS, per_kv_head, 0)


def _bias_tables_t(rel_flat, buckets_t):
    nv, lk, lq = buckets_t.shape
    out_dims = (nv, ATTN_KV_HEADS, lk, Q_PER_KV * lq)
    return pl.pallas_call(
        _bias_t_body,
        in_specs=[pl.BlockSpec(memory_space=pltpu.SMEM), pl.BlockSpec(memory_space=pltpu.VMEM)],
        out_specs=pl.BlockSpec(memory_space=pltpu.VMEM),
        out_shape=jax.ShapeDtypeStruct(out_dims, F32),
        compiler_params=pltpu.CompilerParams(vmem_limit_bytes=VMEM_LIMIT), name="rel_bias_t")(rel_flat, buckets_t)


def _head_rmsnorm(x, g_mat, e_mat, w):
    ms = _dot_sel(x * x, g_mat) * (1.0 / ATTN_HEAD_DIM)
    return x * _dot_sel(lax.rsqrt(ms + EPS), e_mat) * w


def _lane_head(shape):
    return lax.broadcasted_iota(jnp.int32, shape, 1) // ATTN_HEAD_DIM


def _sink_column(sink_ref, n, rows_per_head):
    return jnp.concatenate(
        [jnp.full((rows_per_head, 1), sink_ref[n * Q_PER_KV + g], F32) for g in range(Q_PER_KV)], axis=0)


def _softmax_with_sink(s, sink):
    m = jnp.maximum(jnp.max(s, axis=-1, keepdims=True), sink)
    p = jnp.exp(s - m)
    denom = jnp.sum(p, axis=-1, keepdims=True) + jnp.exp(sink - m)
    return p / denom


def _attn_block(q_blk, kcat, vcat_t, bias_at, sink_ref):
    T = CHUNK
    lane_head = _lane_head((2 * T, KV_WIDTH))
    row_head = lax.broadcasted_iota(jnp.int32, (KV_WIDTH, 2 * T), 0) // ATTN_HEAD_DIM
    zero = jnp.zeros((2 * T, KV_WIDTH), BF16)
    head = lambda h: q_blk[h * ATTN_HEAD_DIM:(h + 1) * ATTN_HEAD_DIM]
    q_cols = jnp.concatenate(
        [jnp.concatenate([head(n * Q_PER_KV + g) for n in range(ATTN_KV_HEADS)], axis=0)
         for g in range(Q_PER_KV)], axis=1)
    probs, vals = [], []
    for n in range(ATTN_KV_HEADS):
        s = _dot(jnp.where(lane_head == n, kcat, zero), q_cols)
        sink = sink_ref[n]
        cols = []
        for g in range(Q_PER_KV):
            c = slice(g * T, (g + 1) * T)
            sg = s[:, c] + bias_at(n, c)
            m = jnp.maximum(jnp.max(sg, axis=0, keepdims=True), sink[:, c])
            p = jnp.exp2(sg - m)
            denom = jnp.sum(p, axis=0, keepdims=True) + jnp.exp2(sink[:, c] - m)
            cols.append((p * (1.0 / denom)).astype(BF16))
        probs.append(jnp.concatenate(cols, axis=1))
        vals.append(jnp.where(row_head == n, vcat_t, zero.T))
    o_t = _dot(jnp.concatenate(vals, axis=1), jnp.concatenate(probs, axis=0))
    return jnp.concatenate(
        [o_t[n * ATTN_HEAD_DIM:(n + 1) * ATTN_HEAD_DIM, g * T:(g + 1) * T]
         for n in range(ATTN_KV_HEADS) for g in range(Q_PER_KV)], axis=0)


def _mixer_body(gz_ref, xs_ref, b_ref, c_ref, dt_ref, alog_ref, dskip_ref, nw_ref, e_ref,
                qt_ref, k_ref, v_ref, gzt_ref, qwt_ref, kw_ref, g_ref, bias_ref, sink_ref, x_ref, wo_ref,
                y_ref, st_ref, kn_ref, vn_ref,
                state_sc, kcat_sc, vcat_t_sc, yssm_sc, yattn_t_sc):
    T = CHUNK
    step = pl.program_id(1)
    cols_step = CHUNKS_PER_STEP * T

    @pl.when(step == 0)
    def _():
        state_sc[...] = jnp.zeros_like(state_sc)
        kcat_sc[0:T, :] = jnp.zeros((T, KV_WIDTH), BF16)
        vcat_t_sc[:, 0:T] = jnp.zeros((KV_WIDTH, T), BF16)

    q3 = qt_ref[...].reshape(ATTN_HEADS, ATTN_HEAD_DIM, cols_step)
    ms = jnp.mean(q3 * q3, axis=1, keepdims=True)
    qn = ((q3 * lax.rsqrt(ms + EPS)).reshape(ATTN_WIDTH, cols_step) * qwt_ref[...]).astype(BF16)
    e_mat = e_ref[...]
    kn = _head_rmsnorm(k_ref[...], g_ref[...], e_mat[:, :KV_WIDTH], kw_ref[...])
    v = v_ref[...]
    kn_ref[0] = kn[cols_step - T:]
    vn_ref[0] = v[cols_step - T:]
    kcat_sc[T:, :] = kn.astype(BF16)
    vcat_t_sc[:, T:] = v.T.astype(BF16)

    a_neg = -jnp.exp(alog_ref[...])
    state = state_sc[...]
    first_variant = jnp.minimum(step, 1)
    for j in range(CHUNKS_PER_STEP):
        r = slice(j * T, (j + 1) * T)
        y, state = _ssd_chunk(gz_ref[r, :], xs_ref[r, :], b_ref[r, :], c_ref[r, :], dt_ref[r, :], a_neg,
                              dskip_ref[...], nw_ref[...], e_mat, state)
        yssm_sc[r, :] = y.astype(BF16)
        variant = first_variant if j == 0 else 1
        y_t = _attn_block(qn[:, r], kcat_sc[j * T:(j + 2) * T, :], vcat_t_sc[:, j * T:(j + 2) * T],
                          lambda n, c, variant=variant: bias_ref[variant, n, :, c], sink_ref)
        yattn_t_sc[:, r] = (y_t * gzt_ref[:, r]).astype(BF16)
    state_sc[...] = state
    kcat_sc[0:T, :] = kcat_sc[cols_step:, :]
    vcat_t_sc[:, 0:T] = vcat_t_sc[:, cols_step:]

    y_ref[...] = (x_ref[...] + _dot(yssm_sc[...], wo_ref[:SSM_WIDTH, :])
                  + _dot_tn(yattn_t_sc[...], wo_ref[SSM_WIDTH:, :]))

    @pl.when(step == pl.num_programs(1) - 1)
    def _():
        st_ref[0] = state.T


def _mixer(gz, xs, b, c, dt, alog, dskip, norm_w, e_mat, q_t, k, v, gz_t, qw_t, kw, g_mat, bias_t, sink_rows,
           x2d, w_out, batch, seq):
    step_rows = CHUNKS_PER_STEP * CHUNK
    ns = seq // step_rows
    row = lambda w: pl.BlockSpec((step_rows, w), lambda b, i: (b * ns + i, 0))
    col = pl.BlockSpec((ATTN_WIDTH, step_rows), lambda b, i: (0, b * ns + i))
    resident = lambda a: pl.BlockSpec(a.shape, lambda b, i: (0,) * a.ndim, pipeline_mode=pl.Buffered(1))
    in_specs = [row(SSM_WIDTH), row(SSM_WIDTH), row(BC_WIDTH), row(BC_WIDTH), row(LANES),
                _const_spec((1, LANES)), _const_spec((1, SSM_WIDTH)), _const_spec((1, SSM_WIDTH)), resident(e_mat),
                col, row(KV_WIDTH), row(KV_WIDTH), col, resident(qw_t), _const_spec((1, KV_WIDTH)),
                resident(g_mat), resident(bias_t), _const_spec(sink_rows.shape), row(D_MODEL), resident(w_out)]
    kv_out = pl.BlockSpec((1, CHUNK, KV_WIDTH), lambda b, i: (b, 0, 0))
    out_specs = [row(D_MODEL), pl.BlockSpec((1, SSM_WIDTH, SSM_STATE), lambda b, i: (b, 0, 0)), kv_out, kv_out]
    out_shape = [jax.ShapeDtypeStruct((batch * seq, D_MODEL), F32),
                 jax.ShapeDtypeStruct((batch, SSM_WIDTH, SSM_STATE), F32),
                 jax.ShapeDtypeStruct((batch, CHUNK, KV_WIDTH), F32),
                 jax.ShapeDtypeStruct((batch, CHUNK, KV_WIDTH), F32)]
    scratch = [pltpu.VMEM((SSM_STATE, SSM_WIDTH), F32),
               pltpu.VMEM((step_rows + CHUNK, KV_WIDTH), BF16), pltpu.VMEM((KV_WIDTH, step_rows + CHUNK), BF16),
               pltpu.VMEM((step_rows, SSM_WIDTH), BF16), pltpu.VMEM((ATTN_WIDTH, step_rows), BF16)]
    return pl.pallas_call(
        _mixer_body, grid=(batch, ns), in_specs=in_specs, out_specs=out_specs, out_shape=out_shape,
        scratch_shapes=scratch, compiler_params=_params("arbitrary", "arbitrary"), name="mixer")(
            gz, xs, b, c, dt, alog, dskip, norm_w, e_mat, q_t, k, v, gz_t, qw_t, kw, g_mat, bias_t, sink_rows,
            x2d, w_out)


def _ssd_sample_vec_body(z_ref, xbc_ref, dt_ref, sconv_ref, cw_ref, cb_ref, dtb_ref, alog_ref, dskip_ref,
                         gh_ref, e_ref, ypart_ref, ea_ref, xw_ref, b_ref, c_ref, cdec_ref, convnew_ref):
    steps = xbc_ref.shape[0]
    tail = CONV_WIDTH - 1
    full = [sconv_ref[j] for j in range(tail)] + [xbc_ref[l] for l in range(steps)]
    for j in range(tail):
        convnew_ref[j] = full[steps + j]
    gh = gh_ref[...]
    e_mat = e_ref[...]
    a_neg = -jnp.exp(alog_ref[...])
    xs, bm, cm, dts, acum = [], [], [], [], []
    run = None
    for l in range(steps):
        conv = cb_ref[...]
        for tap in range(CONV_WIDTH):
            conv = conv + full[l + tap] * cw_ref[tap:tap + 1, :]
        act = _silu(conv)
        xs.append(act[:, :SSM_WIDTH])
        bm.append(act[:, SSM_WIDTH:SSM_WIDTH + BC_WIDTH])
        cm.append(act[:, SSM_WIDTH + BC_WIDTH:])
        d = _softplus(dt_ref[l] + dtb_ref[...])
        dts.append(d)
        run = d * a_neg if run is None else run + d * a_neg
        acum.append(run)
        b_ref[l] = bm[l]
        c_ref[l] = cm[l]
    for l in range(steps):
        y = dskip_ref[...] * xs[l]
        for s in range(l + 1):
            cb_h = _dot_sel(cm[l] * bm[s], gh)
            coef = cb_h * jnp.exp(acum[l] - acum[s]) * dts[s]
            y = y + _dot_sel(coef, e_mat) * xs[s]
        ypart_ref[l] = y
        ea_ref[l] = _dot_sel(jnp.exp(acum[l]), e_mat)
        xw_ref[l] = xs[l] * _dot_sel(dts[l] * jnp.exp(acum[steps - 1] - acum[l]), e_mat)
    cdec_ref[...] = jnp.exp(acum[steps - 1])


def _ssd_sample_vec(z3, xbc3, dt3, sconv3, conv_w, conv_b, dtb, alog, dskip, gh_mat, e_mat):
    steps, nb = z3.shape[0], z3.shape[1]
    f = lambda *s: jax.ShapeDtypeStruct(s, F32)
    out_shape = [f(steps, nb, SSM_WIDTH), f(steps, nb, SSM_WIDTH), f(steps, nb, SSM_WIDTH),
                 f(steps, nb, BC_WIDTH), f(steps, nb, BC_WIDTH), f(nb, LANES), f(CONV_WIDTH - 1, nb, CONV_DIM)]
    return pl.pallas_call(_ssd_sample_vec_body, out_shape=out_shape,
                          compiler_params=pltpu.CompilerParams(vmem_limit_bytes=VMEM_LIMIT),
                          name="ssd_sample_vec")(
        z3, xbc3, dt3, sconv3, conv_w, conv_b, dtb, alog, dskip, gh_mat, e_mat)


def _ssd_sample_state_body(cdec_ref, st_ref, c_ref, b_ref, xw_ref, new_ref, yoff_ref):
    i = pl.program_id(0)
    heads_per_group = SSM_HEADS // SSM_GROUPS
    for j in range(SAMPLE_BATCH_BLOCK):
        st = st_ref[j]
        cb_bf = c_ref[:, j, :].astype(BF16)
        bb_bf = b_ref[:, j, :].astype(BF16)
        xw_bf = xw_ref[:, j, :].astype(BF16)
        y_parts = []
        for g in range(SSM_GROUPS):
            rows = slice(g * GROUP_WIDTH, (g + 1) * GROUP_WIDTH)
            ns = slice(g * SSM_STATE, (g + 1) * SSM_STATE)
            y_parts.append(_dot_nt(cb_bf[:, ns], st[rows].astype(BF16)))
            upd = _dot_tn(xw_bf[:, rows], bb_bf[:, ns])
            for hh in range(heads_per_group):
                h = g * heads_per_group + hh
                r = slice(h * SSM_HEAD_DIM, (h + 1) * SSM_HEAD_DIM)
                dec = cdec_ref[(i * SAMPLE_BATCH_BLOCK + j) * SSM_HEADS + h]
                new_ref[j, r, :] = st[r] * dec + upd[hh * SSM_HEAD_DIM:(hh + 1) * SSM_HEAD_DIM]
        yoff_ref[:, j, :] = jnp.concatenate(y_parts, axis=1)


def _ssd_sample_state(cdec_flat, state, c3, b3, xw3):
    steps, nb = c3.shape[0], c3.shape[1]
    bb = SAMPLE_BATCH_BLOCK
    tok = lambda w: pl.BlockSpec((steps, bb, w), lambda i: (0, i, 0))
    st_spec = pl.BlockSpec((bb, SSM_WIDTH, SSM_STATE), lambda i: (i, 0, 0))
    return pl.pallas_call(
        _ssd_sample_state_body, grid=(nb // bb,),
        in_specs=[pl.BlockSpec(memory_space=pltpu.SMEM), st_spec, tok(BC_WIDTH), tok(BC_WIDTH), tok(SSM_WIDTH)],
        out_specs=[st_spec, tok(SSM_WIDTH)],
        out_shape=[jax.ShapeDtypeStruct(state.shape, F32), jax.ShapeDtypeStruct((steps, nb, SSM_WIDTH), F32)],
        compiler_params=_params("parallel"), name="ssd_sample_state")(cdec_flat, state, c3, b3, xw3)


def _sample_finish_body(ypart_ref, yoff_ref, ea_ref, z_ref, nw_ref, q_ref, k_ref, qw_ref, kw_ref, g_ref, e_ref,
                        yssm_ref, qn_ref, kn_ref):
    y = ypart_ref[...] + yoff_ref[...] * ea_ref[...]
    yssm_ref[...] = _group_rmsnorm(y * _silu(z_ref[...]), nw_ref[...])
    g_mat = g_ref[...]
    e_mat = e_ref[...]
    qn = _head_rmsnorm(q_ref[...], g_mat, e_mat, qw_ref[...]) * ATTN_SCALE
    head = lambda h: qn[:, h * ATTN_HEAD_DIM:(h + 1) * ATTN_HEAD_DIM]
    qn_ref[...] = jnp.concatenate(
        [head(n * Q_PER_KV + g) for g in range(Q_PER_KV) for n in range(ATTN_KV_HEADS)], axis=1)
    kn_ref[...] = _head_rmsnorm(k_ref[...], g_mat[:KV_WIDTH], e_mat[:, :KV_WIDTH], kw_ref[...])


def _sample_finish(ypart, yoff, ea, z, norm_w, q, k, qw, kw, g_mat, e_mat):
    rows = z.shape[0]
    f = lambda w: jax.ShapeDtypeStruct((rows, w), F32)
    return pl.pallas_call(_sample_finish_body, out_shape=[f(SSM_WIDTH), f(ATTN_WIDTH), f(KV_WIDTH)],
                          compiler_params=pltpu.CompilerParams(vmem_limit_bytes=VMEM_LIMIT),
                          name="sample_finish")(ypart, yoff, ea, z, norm_w, q, k, qw, kw, g_mat, e_mat)


def _attn_sample_body(q_ref, kn_ref, vn_ref, z_ref, ckt_ref, cvt_ref, biasc_ref, biasn_ref,
                      y_ref, kot_ref, vot_ref):
    steps = q_ref.shape[0]
    bb = SAMPLE_BATCH_BLOCK
    blk = Q_PER_KV * steps
    rows = ATTN_KV_HEADS * blk
    pad = jnp.zeros((SUBLANES - steps, KV_WIDTH), F32)
    lane_head = _lane_head((blk, KV_WIDTH))
    zero = jnp.zeros((blk, KV_WIDTH), F32)

    s_c, s_n, k8, v8 = [], [], [], []
    for j in range(bb):
        q = q_ref[:, j, :]
        qg = jnp.concatenate([q[:, g * KV_WIDTH:(g + 1) * KV_WIDTH] for g in range(Q_PER_KV)], axis=0)
        qx = jnp.concatenate([jnp.where(lane_head == n, qg, zero) for n in range(ATTN_KV_HEADS)], axis=0)
        qx = qx.astype(BF16)
        k8.append(jnp.concatenate([kn_ref[:, j, :], pad], axis=0))
        v8.append(jnp.concatenate([vn_ref[:, j, :], pad], axis=0))
        s_c.append(_dot(qx, ckt_ref[j].astype(BF16)))
        s_n.append(_dot_nt(qx, k8[j].astype(BF16)))
    s_c = jnp.concatenate(s_c, axis=0) + biasc_ref[...]
    s_n = jnp.concatenate(s_n, axis=0) + biasn_ref[...]
    m = jnp.maximum(jnp.max(s_c, axis=-1, keepdims=True), jnp.max(s_n, axis=-1, keepdims=True))
    p_c = jnp.exp(s_c - m)
    p_n = jnp.exp(s_n - m)
    inv = 1.0 / (jnp.sum(p_c, axis=-1, keepdims=True) + jnp.sum(p_n, axis=-1, keepdims=True))
    p_c = (p_c * inv).astype(BF16)
    p_n = (p_n * inv).astype(BF16)

    lane = lax.broadcasted_iota(jnp.int32, (KV_WIDTH, WINDOW), 1)
    for j in range(bb):
        r = slice(j * rows, (j + 1) * rows)
        o = _dot_nt(p_c[r], cvt_ref[j].astype(BF16)) + _dot(p_n[r], v8[j].astype(BF16))
        og = zero
        for n in range(ATTN_KV_HEADS):
            og = og + jnp.where(lane_head == n, o[n * blk:(n + 1) * blk], zero)
        y = jnp.concatenate(
            [og[g * steps:(g + 1) * steps, n * ATTN_HEAD_DIM:(n + 1) * ATTN_HEAD_DIM]
             for n in range(ATTN_KV_HEADS) for g in range(Q_PER_KV)], axis=1)
        y_ref[:, j, :] = y * _silu(z_ref[:, j, :])

        for new8, old_ref, out_ref in ((k8[j], ckt_ref, kot_ref), (v8[j], cvt_ref, vot_ref)):
            tail_rows = jnp.concatenate([new8[steps:], new8[:steps]], axis=0)
            block = jnp.concatenate([jnp.zeros((WINDOW - SUBLANES, KV_WIDTH), F32), tail_rows], axis=0)
            shifted = pltpu.roll(old_ref[j], WINDOW - steps, axis=1)
            out_ref[j] = jnp.where(lane >= WINDOW - steps, block.T, shifted)


def _attn_sample(q3, kn3, vn3, z3, cache_kt, cache_vt, bias_c, bias_n):
    steps, nb = q3.shape[0], q3.shape[1]
    bb = SAMPLE_BATCH_BLOCK
    tok = lambda w: pl.BlockSpec((steps, bb, w), lambda i: (0, i, 0))
    cache_spec = pl.BlockSpec((bb, KV_WIDTH, WINDOW), lambda i: (i, 0, 0))
    return pl.pallas_call(
        _attn_sample_body, grid=(nb // bb,),
        in_specs=[tok(ATTN_WIDTH), tok(KV_WIDTH), tok(KV_WIDTH), tok(ATTN_WIDTH), cache_spec, cache_spec,
                  _const_spec(bias_c.shape), _const_spec(bias_n.shape)],
        out_specs=[tok(ATTN_WIDTH), cache_spec, cache_spec],
        out_shape=[jax.ShapeDtypeStruct((steps, nb, ATTN_WIDTH), F32),
                   jax.ShapeDtypeStruct(cache_kt.shape, F32), jax.ShapeDtypeStruct(cache_vt.shape, F32)],
        compiler_params=_params("parallel"), name="attn_sample")(
            q3, kn3, vn3, z3, cache_kt, cache_vt, bias_c, bias_n)


def _static_tables(steps):
    lanes = np.arange(ATTN_WIDTH)
    g_mat = np.zeros((ATTN_WIDTH, LANES), np.float32)
    g_mat[lanes, lanes // ATTN_HEAD_DIM] = 1.0
    e_mat = g_mat.T.copy()
    bc = np.arange(BC_WIDTH)
    gh_mat = np.zeros((BC_WIDTH, LANES), np.float32)
    for h in range(SSM_HEADS):
        gh_mat[bc // SSM_STATE == h // (SSM_HEADS // SSM_GROUPS), h] = 1.0
    T = CHUNK
    dist = np.arange(T)[:, None] - (np.arange(2 * T) - T)[None, :]
    first = np.broadcast_to((np.arange(2 * T) >= T)[None, :], dist.shape)
    prompt_buckets = np.stack([_bucket_or_masked(dist, first), _bucket_or_masked(dist)])
    dist_c = (np.arange(steps) + WINDOW)[:, None] - np.arange(WINDOW)[None, :]
    dist_n = np.arange(steps)[:, None] - np.arange(SUBLANES)[None, :]
    real = np.broadcast_to((np.arange(SUBLANES) < steps)[None, :], dist_n.shape)
    return dict(g=g_mat, e=e_mat, gh=gh_mat, prompt_buckets=prompt_buckets,
                cache_buckets=_bucket_or_masked(dist_c)[None], new_buckets=_bucket_or_masked(dist_n, real)[None])


def kernel(x_prompt, x_sample, cache_k, cache_v, state_ssm, state_conv, norm_w, w_in, conv_w, conv_b, dt_bias,
           a_log, d_skip, ssm_norm_w, q_norm_w, k_norm_w, sinks, rel_table, w_out):
    assert w_in.shape[0] == 1, "single-layer kernel"
    batch, seq, _ = x_prompt.shape
    nb, steps, _ = x_sample.shape
    tab = _static_tables(steps)
    g_mat = jnp.asarray(tab["g"], BF16)
    e_mat = jnp.asarray(tab["e"], BF16)
    gh_mat = jnp.asarray(tab["gh"], BF16)

    w_t = jnp.transpose(w_in[0]).astype(BF16)
    wo_all = w_out[0].astype(BF16)

    row = lambda v, width: jnp.pad(v.reshape(1, -1), ((0, 0), (0, width - v.size)))
    nw = row(norm_w[0], D_MODEL)
    cw = conv_w[0]
    cb = row(conv_b[0], CONV_DIM)
    dtb = row(dt_bias[0], LANES)
    alog = row(a_log[0], LANES)
    dskip = jnp.repeat(d_skip[0], SSM_HEAD_DIM).reshape(1, SSM_WIDTH)
    snw = row(ssm_norm_w[0], SSM_WIDTH)
    qw = jnp.tile(q_norm_w[0], ATTN_HEADS).reshape(1, ATTN_WIDTH)
    kw = jnp.tile(k_norm_w[0], ATTN_KV_HEADS).reshape(1, KV_WIDTH)
    sink = sinks[0]
    rel_flat = rel_table.reshape(-1)

    xp = x_prompt.reshape(batch * seq, D_MODEL)
    gz, xs_p, b_p, c_p, dt_p, q_t, k, v, gza_t, tail_p = _inproj_prompt(
        xp, nw, w_t, cw, cb, dtb, batch, seq)
    bias_t = _bias_tables_t(rel_flat * LOG2E, jnp.asarray(tab["prompt_buckets"].transpose(0, 2, 1)))
    qw_t = jnp.broadcast_to((qw * (ATTN_SCALE * LOG2E)).reshape(ATTN_WIDTH, 1),
                            (ATTN_WIDTH, CHUNKS_PER_STEP * CHUNK))
    sink_rows = jnp.repeat(sink.reshape(ATTN_KV_HEADS, Q_PER_KV) * LOG2E, CHUNK, axis=1)
    sink_rows = sink_rows.reshape(ATTN_KV_HEADS, 1, -1)
    y_p, st_p, k_p, v_p = _mixer(gz, xs_p, b_p, c_p, dt_p, alog, dskip, snw, e_mat, q_t, k, v, gza_t, qw_t, kw,
                                 g_mat[:KV_WIDTH], bias_t, sink_rows, xp, wo_all, batch, seq)
    y_p = y_p.reshape(batch, seq, D_MODEL)
    conv_p = tail_p[:, SUBLANES - (CONV_WIDTH - 1):, :]

    xs = jnp.swapaxes(x_sample, 0, 1).reshape(steps * nb, D_MODEL)
    z, xbc, dt, q, k, v, za = _inproj(xs, nw, w_t)
    t3 = lambda a: a.reshape(steps, nb, a.shape[-1])
    sconv3 = jnp.swapaxes(state_conv[0], 0, 1)
    ypart, ea, xw, b3, c3, cdec, conv_s3 = _ssd_sample_vec(
        t3(z), t3(xbc), t3(dt), sconv3, cw, cb, dtb, alog, dskip, gh_mat, e_mat)
    state_in = state_ssm[0].reshape(nb, SSM_WIDTH, SSM_STATE)
    st_s, yoff = _ssd_sample_state(cdec[:, :SSM_HEADS].reshape(-1), state_in, c3, b3, xw)
    f2 = lambda a: a.reshape(steps * nb, a.shape[-1])
    y_ssm, qn, kn = _sample_finish(f2(ypart), f2(yoff), f2(ea), z, snw, q, k, qw, kw, g_mat, e_mat)
    bias_c = _bias_tables(rel_flat, jnp.asarray(tab["cache_buckets"])).reshape(ATTN_HEADS * steps, WINDOW)
    bias_n = _bias_tables(rel_flat, jnp.asarray(tab["new_buckets"])).reshape(ATTN_HEADS * steps, SUBLANES)
    bias_n = bias_n.at[:, steps].set(jnp.repeat(sink, steps))
    bias_c = jnp.tile(bias_c, (SAMPLE_BATCH_BLOCK, 1))
    bias_n = jnp.tile(bias_n, (SAMPLE_BATCH_BLOCK, 1))
    to_t = lambda a: jnp.transpose(a[0], (0, 2, 3, 1)).reshape(nb, KV_WIDTH, WINDOW)
    from_t = lambda a: jnp.transpose(a.reshape(nb, ATTN_KV_HEADS, ATTN_HEAD_DIM, WINDOW), (0, 3, 1, 2))[None]
    y_attn3, k_st, v_st = _attn_sample(t3(qn), t3(kn), t3(v), t3(za), to_t(cache_k), to_t(cache_v),
                                       bias_c, bias_n)
    k_s, v_s = from_t(k_st), from_t(v_st)
    y_s = _outproj(y_ssm, f2(y_attn3), xs, wo_all)
    y_s = jnp.swapaxes(y_s.reshape(steps, nb, D_MODEL), 0, 1)

    kv5 = lambda a: a.reshape(1, a.shape[0], WINDOW, ATTN_KV_HEADS, ATTN_HEAD_DIM)
    st5 = lambda a: a.reshape(1, a.shape[0], SSM_HEADS, SSM_HEAD_DIM, SSM_STATE)
    return (y_p, y_s, kv5(k_p), kv5(v_p), st5(st_p), conv_p[None],
            k_s, v_s, st5(st_s), jnp.swapaxes(conv_s3, 0, 1)[None])
```

```python
import functools
import math

import numpy as np
import jax
import jax.numpy as jnp
from jax import lax
from jax.experimental import pallas as pl
from jax.experimental.pallas import tpu as pltpu

F32 = jnp.float32
BF16 = jnp.bfloat16

D_MODEL = 1024
SSM_HEADS = 16
SSM_HEAD_DIM = 64
SSM_WIDTH = SSM_HEADS * SSM_HEAD_DIM
SSM_GROUPS = 2
SSM_STATE = 128
GROUP_WIDTH = SSM_WIDTH // SSM_GROUPS
BC_WIDTH = SSM_GROUPS * SSM_STATE
CONV_WIDTH = 4
CONV_DIM = SSM_WIDTH + 2 * BC_WIDTH
CHUNK = 128
ATTN_HEADS = 16
ATTN_KV_HEADS = 4
Q_PER_KV = ATTN_HEADS // ATTN_KV_HEADS
ATTN_HEAD_DIM = 64
ATTN_WIDTH = ATTN_HEADS * ATTN_HEAD_DIM
KV_WIDTH = ATTN_KV_HEADS * ATTN_HEAD_DIM
WINDOW = 128
ATTN_SCALE = ATTN_HEAD_DIM ** -0.5
REL_BUCKETS = 32
REL_MAX_DIST = 128
EPS = 1e-6
LOG2E = 1.0 / math.log(2.0)
NEG = -1e30

LANES = 128
SUBLANES = 8
MXU_WIDTH = 256
VMEM_LIMIT = 56 * 1024 * 1024
def _in_proj_rows():
    widths = (("z", SSM_WIDTH), ("xbc", CONV_DIM), ("dt", SSM_HEADS), ("q", ATTN_WIDTH), ("k", KV_WIDTH),
              ("v", KV_WIDTH), ("za", ATTN_WIDTH))
    rows, start = {}, 0
    for name, width in widths:
        rows[name] = slice(start, start + width)
        start += width
    return rows


IN_ROWS = _in_proj_rows()
DT_ROWS = slice(IN_ROWS["dt"].start, IN_ROWS["dt"].start + LANES)

PROJ_ROWS = 512
CHUNKS_PER_STEP = 4
SAMPLE_BATCH_BLOCK = 8


def _dot(a, b):
    return jnp.dot(a, b, preferred_element_type=F32)


def _dot_nt(a, b):
    return lax.dot_general(a, b, (((1,), (1,)), ((), ())), preferred_element_type=F32)


def _dot_tn(a, b):
    return lax.dot_general(a, b, (((0,), (0,)), ((), ())), preferred_element_type=F32)


def _split2(v):
    hi = v.astype(BF16)
    lo = (v - hi.astype(F32)).astype(BF16)
    return hi, lo


def _dot_sel(v, m):
    hi, lo = _split2(v)
    if 2 * v.shape[1] <= MXU_WIDTH:
        return _dot(jnp.concatenate([hi, lo], axis=1), jnp.concatenate([m, m], axis=0))
    return _dot(hi, m) + _dot(lo, m)


def _dot_sel3(m, v):
    hi = v.astype(BF16)
    r1 = v - hi.astype(F32)
    mid = r1.astype(BF16)
    lo = (r1 - mid.astype(F32)).astype(BF16)
    return _dot(m, hi) + _dot(m, mid) + _dot(m, lo)


def _silu(x):
    return x / (1.0 + jnp.exp(-x))


def _softplus(x):
    return jnp.maximum(x, 0.0) + jnp.log1p(jnp.exp(-jnp.abs(x)))


def _params(*sem):
    return pltpu.CompilerParams(dimension_semantics=sem, vmem_limit_bytes=VMEM_LIMIT)


def _const_spec(shape):
    nd = len(shape)
    return pl.BlockSpec(shape, lambda *_: (0,) * nd)


def _normed_input(x_ref, nw_ref):
    x = x_ref[...]
    ms = jnp.mean(x * x, axis=-1, keepdims=True)
    return (x * lax.rsqrt(ms + EPS) * nw_ref[...]).astype(BF16)


def _dt_projection(h, wt_ref):
    raw = _dot_nt(h, wt_ref[DT_ROWS, :])
    return jnp.where(lax.broadcasted_iota(jnp.int32, raw.shape, 1) < SSM_HEADS, raw, 0.0)


def _inproj_body(x_ref, nw_ref, wt_ref, *out_refs):
    h = _normed_input(x_ref, nw_ref)
    for name, o_ref in zip(IN_ROWS, out_refs):
        o_ref[...] = _dt_projection(h, wt_ref) if name == "dt" else _dot_nt(h, wt_ref[IN_ROWS[name], :])


def _inproj(x2d, norm_w, w_t):
    rows = x2d.shape[0]
    tm = min(PROJ_ROWS, rows)
    widths = [LANES if name == "dt" else r.stop - r.start for name, r in IN_ROWS.items()]
    in_specs = [pl.BlockSpec((tm, D_MODEL), lambda i: (i, 0)), _const_spec((1, D_MODEL)),
                pl.BlockSpec(w_t.shape, lambda i: (0, 0), pipeline_mode=pl.Buffered(1))]
    out_specs = [pl.BlockSpec((tm, w), lambda i: (i, 0)) for w in widths]
    out_shape = [jax.ShapeDtypeStruct((rows, w), F32) for w in widths]
    return pl.pallas_call(
        _inproj_body, grid=(rows // tm,), in_specs=in_specs, out_specs=out_specs, out_shape=out_shape,
        compiler_params=_params("parallel"), name="inproj")(x2d, norm_w, w_t)


def _shift_rows(u, prev_tail, k):
    rows, width = u.shape
    tiles = jnp.concatenate([prev_tail, u], axis=0).reshape(rows // SUBLANES + 1, SUBLANES, width)
    rot = pltpu.roll(tiles, k, axis=1)
    first = lax.broadcasted_iota(jnp.int32, (1, SUBLANES, width), 1) < k
    return jnp.where(first, rot[:-1], rot[1:]).reshape(rows, width)


def _inproj_prompt_body(steps_per_seq, x_ref, nw_ref, wt_ref, cw_ref, cb_ref, dtb_ref,
                        gz_ref, xs_ref, b_ref, c_ref, dt_ref, qt_ref, k_ref, v_ref, gzat_ref, tail_ref, tail_sc):
    @pl.when(pl.program_id(0) % steps_per_seq == 0)
    def _():
        tail_sc[...] = jnp.zeros_like(tail_sc)

    h = _normed_input(x_ref, nw_ref)
    rows = h.shape[0]
    w_tile = lambda name, j: wt_ref[IN_ROWS[name].start + j * MXU_WIDTH:IN_ROWS[name].start + (j + 1) * MXU_WIDTH, :]
    n_side = SSM_WIDTH // MXU_WIDTH
    for j in range(CONV_DIM // MXU_WIDTH):
        cols = slice(j * MXU_WIDTH, (j + 1) * MXU_WIDTH)
        partner = w_tile("z", j) if j < n_side else wt_ref[IN_ROWS["k" if j == n_side else "v"], :]
        both = _dot_nt(h, jnp.concatenate([w_tile("xbc", j), partner], axis=0))
        u = both[:, :MXU_WIDTH]
        if j < n_side:
            gz_ref[:, cols] = _silu(both[:, MXU_WIDTH:])
        elif j == n_side:
            k_ref[...] = both[:, MXU_WIDTH:]
        else:
            v_ref[...] = both[:, MXU_WIDTH:]
        prev_tail = tail_sc[:, cols]
        conv = cb_ref[:, cols] + u * cw_ref[CONV_WIDTH - 1:CONV_WIDTH, cols]
        for k in range(1, CONV_WIDTH):
            tap = CONV_WIDTH - 1 - k
            conv = conv + _shift_rows(u, prev_tail, k) * cw_ref[tap:tap + 1, cols]
        new_tail = u[rows - SUBLANES:, :]
        tail_sc[:, cols] = new_tail
        tail_ref[0, :, cols] = new_tail
        act = _silu(conv)
        if j < SSM_WIDTH // MXU_WIDTH:
            xs_ref[:, cols] = act
        elif j == SSM_WIDTH // MXU_WIDTH:
            b_ref[...] = act.astype(BF16)
        else:
            c_ref[...] = act.astype(BF16)

    for j in range(ATTN_WIDTH // MXU_WIDTH):
        feats = slice(j * MXU_WIDTH, (j + 1) * MXU_WIDTH)
        gzat_ref[feats, :] = _silu(_dot_nt(w_tile("za", j), h))
    dt_ref[...] = _softplus(_dt_projection(h, wt_ref) + dtb_ref[...])
    qt_ref[...] = _dot_nt(wt_ref[IN_ROWS["q"], :], h)


def _inproj_prompt(x2d, norm_w, w_t, conv_w, conv_b, dtb, batch, seq):
    rows = x2d.shape[0]
    tm = PROJ_ROWS
    steps_per_seq = seq // tm
    resident = lambda a: pl.BlockSpec(a.shape, lambda i: (0, 0), pipeline_mode=pl.Buffered(1))
    rowblk = lambda w: pl.BlockSpec((tm, w), lambda i: (i, 0))
    colblk = pl.BlockSpec((ATTN_WIDTH, tm), lambda i: (0, i))
    in_specs = ([rowblk(D_MODEL), _const_spec((1, D_MODEL)), resident(w_t)]
                + [_const_spec(conv_w.shape), _const_spec(conv_b.shape), _const_spec(dtb.shape)])
    out_specs = [rowblk(SSM_WIDTH), rowblk(SSM_WIDTH), rowblk(BC_WIDTH), rowblk(BC_WIDTH), rowblk(LANES),
                 colblk, rowblk(KV_WIDTH), rowblk(KV_WIDTH), colblk,
                 pl.BlockSpec((1, SUBLANES, CONV_DIM), lambda i: (i // steps_per_seq, 0, 0))]
    f = lambda r, c, dt=F32: jax.ShapeDtypeStruct((r, c), dt)
    out_shape = [f(rows, SSM_WIDTH), f(rows, SSM_WIDTH), f(rows, BC_WIDTH, BF16), f(rows, BC_WIDTH, BF16),
                 f(rows, LANES), f(ATTN_WIDTH, rows), f(rows, KV_WIDTH), f(rows, KV_WIDTH), f(ATTN_WIDTH, rows),
                 jax.ShapeDtypeStruct((batch, SUBLANES, CONV_DIM), F32)]
    return pl.pallas_call(
        functools.partial(_inproj_prompt_body, steps_per_seq), grid=(rows // tm,), in_specs=in_specs,
        out_specs=out_specs, out_shape=out_shape, scratch_shapes=[pltpu.VMEM((SUBLANES, CONV_DIM), F32)],
        compiler_params=_params("arbitrary"), name="inproj_prompt")(
            x2d, norm_w, w_t, conv_w, conv_b, dtb)


def _outproj_body(ys_ref, ya_ref, x_ref, w_ref, o_ref):
    o_ref[...] = (x_ref[...] + _dot(ys_ref[...].astype(BF16), w_ref[:SSM_WIDTH, :])
                  + _dot(ya_ref[...].astype(BF16), w_ref[SSM_WIDTH:, :]))


def _outproj(y_ssm, y_attn, x2d, w_out):
    rows = x2d.shape[0]
    tm = min(PROJ_ROWS, rows)
    row_spec = pl.BlockSpec((tm, D_MODEL), lambda i: (i, 0))
    w_spec = pl.BlockSpec(w_out.shape, lambda i: (0, 0), pipeline_mode=pl.Buffered(1))
    return pl.pallas_call(
        _outproj_body, grid=(rows // tm,), in_specs=[row_spec, row_spec, row_spec, w_spec],
        out_specs=row_spec, out_shape=jax.ShapeDtypeStruct((rows, D_MODEL), F32),
        compiler_params=_params("parallel"), name="outproj")(y_ssm, y_attn, x2d, w_out)


def _group_rmsnorm(gy, norm_w):
    parts = []
    for g in range(SSM_GROUPS):
        blk = gy[:, g * GROUP_WIDTH:(g + 1) * GROUP_WIDTH]
        ms = jnp.mean(blk * blk, axis=-1, keepdims=True)
        parts.append(blk * lax.rsqrt(ms + EPS))
    return jnp.concatenate(parts, axis=1) * norm_w


def _ssd_chunk(gz, xs, b_bf, c_bf, dt, a_neg, dskip, norm_w, e_mat, state):
    xs_bf = xs.astype(BF16)

    a = dt * a_neg
    li = lax.broadcasted_iota(jnp.int32, (CHUNK, CHUNK), 0)
    si = lax.broadcasted_iota(jnp.int32, (CHUNK, CHUNK), 1)
    causal = li >= si
    a_cum = _dot_sel3(jnp.where(causal, 1.0, 0.0).astype(BF16), a)
    a2 = a_cum * LOG2E
    row_term = a2.T - jnp.log2(dt.T)
    ea_full = _dot_sel(jnp.exp(a_cum), e_mat)
    w_full = _dot((dt * jnp.exp(a_cum[CHUNK - 1:CHUNK, :] - a_cum)).astype(BF16), e_mat)

    cb = [_dot_nt(c_bf[:, g * SSM_STATE:(g + 1) * SSM_STATE], b_bf[:, g * SSM_STATE:(g + 1) * SSM_STATE])
          for g in range(SSM_GROUPS)]
    half = lax.broadcasted_iota(jnp.int32, (CHUNK, LANES), 1) < SSM_HEAD_DIM
    heads_per_group = SSM_HEADS // SSM_GROUPS
    y_parts = []
    for pair in range(SSM_HEADS // 2):
        blocks = []
        for h in (2 * pair, 2 * pair + 1):
            seg = a2[:, h:h + 1] - row_term[h:h + 1, :]
            decay_dt = jnp.exp2(jnp.where(causal, seg, -jnp.inf))
            blocks.append((cb[h // heads_per_group] * decay_dt).astype(BF16))
        lhs = jnp.concatenate(blocks, axis=1)
        xp = xs_bf[:, pair * LANES:(pair + 1) * LANES]
        zero = jnp.zeros_like(xp)
        rhs = jnp.concatenate([jnp.where(half, xp, zero), jnp.where(half, zero, xp)], axis=0)
        y_parts.append(_dot(lhs, rhs))
    y_diag = jnp.concatenate(y_parts, axis=1)

    state_bf = state.astype(BF16)
    xw_bf = (xs * w_full).astype(BF16)
    y_off, upd = [], []
    for g in range(SSM_GROUPS):
        cols = slice(g * GROUP_WIDTH, (g + 1) * GROUP_WIDTH)
        ns = slice(g * SSM_STATE, (g + 1) * SSM_STATE)
        y_off.append(_dot(c_bf[:, ns], state_bf[:, cols]))
        upd.append(_dot_tn(b_bf[:, ns], xw_bf[:, cols]))
    y = y_diag + jnp.concatenate(y_off, axis=1) * ea_full + dskip * xs
    new_state = state * ea_full[CHUNK - 1:CHUNK, :] + jnp.concatenate(upd, axis=1)
    return _group_rmsnorm(y * gz, norm_w), new_state


def _rel_bucket_np(dist):
    max_exact = REL_BUCKETS // 2
    d_f = np.maximum(dist, 1).astype(np.float32)
    large = max_exact + (np.log(d_f / np.float32(max_exact)) / np.float32(math.log(REL_MAX_DIST / max_exact))
                         * np.float32(REL_BUCKETS - max_exact)).astype(np.int32)
    return np.where(dist < max_exact, dist, np.minimum(large, REL_BUCKETS - 1)).astype(np.int32)


def _bucket_or_masked(dist, extra_mask=None):
    ok = (dist >= 0) & (dist <= WINDOW)
    if extra_mask is not None:
        ok = ok & extra_mask
    return np.where(ok, _rel_bucket_np(np.clip(dist, 0, WINDOW)), -1).astype(np.int32)


def _bias_body(rel_ref, bucket_ref, o_ref):
    bucket = bucket_ref[0]

    def per_head(h, carry):
        acc = jnp.full(bucket.shape, NEG, F32)
        for bkt in range(REL_BUCKETS):
            acc = jnp.where(bucket == bkt, rel_ref[bkt * ATTN_HEADS + h], acc)
        o_ref[0, h] = acc
        return carry

    lax.fori_loop(0, ATTN_HEADS, per_head, 0)


def _bias_tables(rel_flat, buckets):
    nv, lq, lk = buckets.shape
    return pl.pallas_call(
        _bias_body, grid=(nv,),
        in_specs=[pl.BlockSpec(memory_space=pltpu.SMEM), pl.BlockSpec((1, lq, lk), lambda v: (v, 0, 0))],
        out_specs=pl.BlockSpec((1, ATTN_HEADS, lq, lk), lambda v: (v, 0, 0, 0)),
        out_shape=jax.ShapeDtypeStruct((nv, ATTN_HEADS, lq, lk), F32),
        compiler_params=_params("arbitrary"), name="rel_bias")(rel_flat, buckets)


def _bias_t_body(rel_ref, bucket_ref, o_ref):
    variants = [bucket_ref[v] for v in range(bucket_ref.shape[0])]
    union = functools.reduce(jnp.maximum, variants)
    lq = union.shape[1]

    def per_kv_head(n, carry):
        for g in range(Q_PER_KV):
            acc = jnp.full(union.shape, NEG, F32)
            for bkt in range(REL_BUCKETS):
                acc = jnp.where(union == bkt, rel_ref[bkt * ATTN_HEADS + n * Q_PER_KV + g], acc)
            for v, bucket in enumerate(variants):
                o_ref[v, n, :, g * lq:(g + 1) * lq] = jnp.where(bucket >= 0, acc, NEG)
        return carry

    lax.fori_loop(0, ATTN_KV_HEADS, per_kv_head, 0)


def _bias_tables_t(rel_flat, buckets_t):
    nv, lk, lq = buckets_t.shape
    out_dims = (nv, ATTN_KV_HEADS, lk, Q_PER_KV * lq)
    return pl.pallas_call(
        _bias_t_body,
        in_specs=[pl.BlockSpec(memory_space=pltpu.SMEM), pl.BlockSpec(memory_space=pltpu.VMEM)],
        out_specs=pl.BlockSpec(memory_space=pltpu.VMEM),
        out_shape=jax.ShapeDtypeStruct(out_dims, F32),
        compiler_params=pltpu.CompilerParams(vmem_limit_bytes=VMEM_LIMIT), name="rel_bias_t")(rel_flat, buckets_t)


def _head_rmsnorm(x, g_mat, e_mat, w):
    ms = _dot_sel(x * x, g_mat) * (1.0 / ATTN_HEAD_DIM)
    return x * _dot_sel(lax.rsqrt(ms + EPS), e_mat) * w


def _lane_head(shape):
    return lax.broadcasted_iota(jnp.int32, shape, 1) // ATTN_HEAD_DIM


def _sink_column(sink_ref, n, rows_per_head):
    return jnp.concatenate(
        [jnp.full((rows_per_head, 1), sink_ref[n * Q_PER_KV + g], F32) for g in range(Q_PER_KV)], axis=0)


def _softmax_with_sink(s, sink):
    m = jnp.maximum(jnp.max(s, axis=-1, keepdims=True), sink)
    p = jnp.exp(s - m)
    denom = jnp.sum(p, axis=-1, keepdims=True) + jnp.exp(sink - m)
    return p / denom


def _attn_block(q_blk, kcat, vcat_t, bias_at, sink_ref):
    T = CHUNK
    lane_head = _lane_head((2 * T, KV_WIDTH))
    row_head = lax.broadcasted_iota(jnp.int32, (KV_WIDTH, 2 * T), 0) // ATTN_HEAD_DIM
    zero = jnp.zeros((2 * T, KV_WIDTH), BF16)
    head = lambda h: q_blk[h * ATTN_HEAD_DIM:(h + 1) * ATTN_HEAD_DIM]
    q_cols = jnp.concatenate(
        [jnp.concatenate([head(n * Q_PER_KV + g) for n in range(ATTN_KV_HEADS)], axis=0)
         for g in range(Q_PER_KV)], axis=1)
    probs, vals = [], []
    for n in range(ATTN_KV_HEADS):
        s = _dot(jnp.where(lane_head == n, kcat, zero), q_cols)
        sink = sink_ref[n]
        cols = []
        for g in range(Q_PER_KV):
            c = slice(g * T, (g + 1) * T)
            sg = s[:, c] + bias_at(n, c)
            m = jnp.maximum(jnp.max(sg, axis=0, keepdims=True), sink[:, c])
            p = jnp.exp2(sg - m)
            denom = jnp.sum(p, axis=0, keepdims=True) + jnp.exp2(sink[:, c] - m)
            cols.append((p * (1.0 / denom)).astype(BF16))
        probs.append(jnp.concatenate(cols, axis=1))
        vals.append(jnp.where(row_head == n, vcat_t, zero.T))
    o_t = _dot(jnp.concatenate(vals, axis=1), jnp.concatenate(probs, axis=0))
    return jnp.concatenate(
        [o_t[n * ATTN_HEAD_DIM:(n + 1) * ATTN_HEAD_DIM, g * T:(g + 1) * T]
         for n in range(ATTN_KV_HEADS) for g in range(Q_PER_KV)], axis=0)


def _mixer_body(gz_ref, xs_ref, b_ref, c_ref, dt_ref, alog_ref, dskip_ref, nw_ref, e_ref,
                qt_ref, k_ref, v_ref, gzt_ref, qwt_ref, kw_ref, g_ref, bias_ref, sink_ref, x_ref, wo_ref,
                y_ref, st_ref, kn_ref, vn_ref,
                state_sc, kcat_sc, vcat_t_sc, yssm_sc, yattn_t_sc):
    T = CHUNK
    step = pl.program_id(1)
    cols_step = CHUNKS_PER_STEP * T

    @pl.when(step == 0)
    def _():
        state_sc[...] = jnp.zeros_like(state_sc)
        kcat_sc[0:T, :] = jnp.zeros((T, KV_WIDTH), BF16)
        vcat_t_sc[:, 0:T] = jnp.zeros((KV_WIDTH, T), BF16)

    q3 = qt_ref[...].reshape(ATTN_HEADS, ATTN_HEAD_DIM, cols_step)
    ms = jnp.mean(q3 * q3, axis=1, keepdims=True)
    qn = ((q3 * lax.rsqrt(ms + EPS)).reshape(ATTN_WIDTH, cols_step) * qwt_ref[...]).astype(BF16)
    e_mat = e_ref[...]
    kn = _head_rmsnorm(k_ref[...], g_ref[...], e_mat[:, :KV_WIDTH], kw_ref[...])
    v = v_ref[...]
    kn_ref[0] = kn[cols_step - T:]
    vn_ref[0] = v[cols_step - T:]
    kcat_sc[T:, :] = kn.astype(BF16)
    vcat_t_sc[:, T:] = v.T.astype(BF16)

    a_neg = -jnp.exp(alog_ref[...])
    state = state_sc[...]
    first_variant = jnp.minimum(step, 1)
    for j in range(CHUNKS_PER_STEP):
        r = slice(j * T, (j + 1) * T)
        y, state = _ssd_chunk(gz_ref[r, :], xs_ref[r, :], b_ref[r, :], c_ref[r, :], dt_ref[r, :], a_neg,
                              dskip_ref[...], nw_ref[...], e_mat, state)
        yssm_sc[r, :] = y.astype(BF16)
        variant = first_variant if j == 0 else 1
        y_t = _attn_block(qn[:, r], kcat_sc[j * T:(j + 2) * T, :], vcat_t_sc[:, j * T:(j + 2) * T],
                          lambda n, c, variant=variant: bias_ref[variant, n, :, c], sink_ref)
        yattn_t_sc[:, r] = (y_t * gzt_ref[:, r]).astype(BF16)
    state_sc[...] = state
    kcat_sc[0:T, :] = kcat_sc[cols_step:, :]
    vcat_t_sc[:, 0:T] = vcat_t_sc[:, cols_step:]

    y_ref[...] = (x_ref[...] + _dot(yssm_sc[...], wo_ref[:SSM_WIDTH, :])
                  + _dot_tn(yattn_t_sc[...], wo_ref[SSM_WIDTH:, :]))

    @pl.when(step == pl.num_programs(1) - 1)
    def _():
        st_ref[0] = state.T


def _mixer(gz, xs, b, c, dt, alog, dskip, norm_w, e_mat, q_t, k, v, gz_t, qw_t, kw, g_mat, bias_t, sink_rows,
           x2d, w_out, batch, seq):
    step_rows = CHUNKS_PER_STEP * CHUNK
    ns = seq // step_rows
    row = lambda w: pl.BlockSpec((step_rows, w), lambda b, i: (b * ns + i, 0))
    col = pl.BlockSpec((ATTN_WIDTH, step_rows), lambda b, i: (0, b * ns + i))
    resident = lambda a: pl.BlockSpec(a.shape, lambda b, i: (0,) * a.ndim, pipeline_mode=pl.Buffered(1))
    in_specs = [row(SSM_WIDTH), row(SSM_WIDTH), row(BC_WIDTH), row(BC_WIDTH), row(LANES),
                _const_spec((1, LANES)), _const_spec((1, SSM_WIDTH)), _const_spec((1, SSM_WIDTH)), resident(e_mat),
                col, row(KV_WIDTH), row(KV_WIDTH), col, resident(qw_t), _const_spec((1, KV_WIDTH)),
                resident(g_mat), resident(bias_t), _const_spec(sink_rows.shape), row(D_MODEL), resident(w_out)]
    kv_out = pl.BlockSpec((1, CHUNK, KV_WIDTH), lambda b, i: (b, 0, 0))
    out_specs = [row(D_MODEL), pl.BlockSpec((1, SSM_WIDTH, SSM_STATE), lambda b, i: (b, 0, 0)), kv_out, kv_out]
    out_shape = [jax.ShapeDtypeStruct((batch * seq, D_MODEL), F32),
                 jax.ShapeDtypeStruct((batch, SSM_WIDTH, SSM_STATE), F32),
                 jax.ShapeDtypeStruct((batch, CHUNK, KV_WIDTH), F32),
                 jax.ShapeDtypeStruct((batch, CHUNK, KV_WIDTH), F32)]
    scratch = [pltpu.VMEM((SSM_STATE, SSM_WIDTH), F32),
               pltpu.VMEM((step_rows + CHUNK, KV_WIDTH), BF16), pltpu.VMEM((KV_WIDTH, step_rows + CHUNK), BF16),
               pltpu.VMEM((step_rows, SSM_WIDTH), BF16), pltpu.VMEM((ATTN_WIDTH, step_rows), BF16)]
    return pl.pallas_call(
        _mixer_body, grid=(batch, ns), in_specs=in_specs, out_specs=out_specs, out_shape=out_shape,
        scratch_shapes=scratch, compiler_params=_params("arbitrary", "arbitrary"), name="mixer")(
            gz, xs, b, c, dt, alog, dskip, norm_w, e_mat, q_t, k, v, gz_t, qw_t, kw, g_mat, bias_t, sink_rows,
            x2d, w_out)


def _ssd_sample_vec_body(z_ref, xbc_ref, dt_ref, sconv_ref, cw_ref, cb_ref, dtb_ref, alog_ref, dskip_ref,
                         gh_ref, e_ref, ypart_ref, ea_ref, xw_ref, b_ref, c_ref, cdec_ref, convnew_ref):
    steps = xbc_ref.shape[0]
    tail = CONV_WIDTH - 1
    full = [sconv_ref[j] for j in range(tail)] + [xbc_ref[l] for l in range(steps)]
    for j in range(tail):
        convnew_ref[j] = full[steps + j]
    gh = gh_ref[...]
    e_mat = e_ref[...]
    a_neg = -jnp.exp(alog_ref[...])
    xs, bm, cm, dts, acum = [], [], [], [], []
    run = None
    for l in range(steps):
        conv = cb_ref[...]
        for tap in range(CONV_WIDTH):
            conv = conv + full[l + tap] * cw_ref[tap:tap + 1, :]
        act = _silu(conv)
        xs.append(act[:, :SSM_WIDTH])
        bm.append(act[:, SSM_WIDTH:SSM_WIDTH + BC_WIDTH])
        cm.append(act[:, SSM_WIDTH + BC_WIDTH:])
        d = _softplus(dt_ref[l] + dtb_ref[...])
        dts.append(d)
        run = d * a_neg if run is None else run + d * a_neg
        acum.append(run)
        b_ref[l] = bm[l]
        c_ref[l] = cm[l]
    for l in range(steps):
        y = dskip_ref[...] * xs[l]
        for s in range(l + 1):
            cb_h = _dot_sel(cm[l] * bm[s], gh)
            coef = cb_h * jnp.exp(acum[l] - acum[s]) * dts[s]
            y = y + _dot_sel(coef, e_mat) * xs[s]
        ypart_ref[l] = y
        ea_ref[l] = _dot_sel(jnp.exp(acum[l]), e_mat)
        xw_ref[l] = xs[l] * _dot_sel(dts[l] * jnp.exp(acum[steps - 1] - acum[l]), e_mat)
    cdec_ref[...] = jnp.exp(acum[steps - 1])


def _ssd_sample_vec(z3, xbc3, dt3, sconv3, conv_w, conv_b, dtb, alog, dskip, gh_mat, e_mat):
    steps, nb = z3.shape[0], z3.shape[1]
    f = lambda *s: jax.ShapeDtypeStruct(s, F32)
    out_shape = [f(steps, nb, SSM_WIDTH), f(steps, nb, SSM_WIDTH), f(steps, nb, SSM_WIDTH),
                 f(steps, nb, BC_WIDTH), f(steps, nb, BC_WIDTH), f(nb, LANES), f(CONV_WIDTH - 1, nb, CONV_DIM)]
    return pl.pallas_call(_ssd_sample_vec_body, out_shape=out_shape,
                          compiler_params=pltpu.CompilerParams(vmem_limit_bytes=VMEM_LIMIT),
                          name="ssd_sample_vec")(
        z3, xbc3, dt3, sconv3, conv_w, conv_b, dtb, alog, dskip, gh_mat, e_mat)


def _ssd_sample_state_body(cdec_ref, st_ref, c_ref, b_ref, xw_ref, new_ref, yoff_ref):
    i = pl.program_id(0)
    heads_per_group = SSM_HEADS // SSM_GROUPS
    for j in range(SAMPLE_BATCH_BLOCK):
        st = st_ref[j]
        cb_bf = c_ref[:, j, :].astype(BF16)
        bb_bf = b_ref[:, j, :].astype(BF16)
        xw_bf = xw_ref[:, j, :].astype(BF16)
        y_parts = []
        for g in range(SSM_GROUPS):
            rows = slice(g * GROUP_WIDTH, (g + 1) * GROUP_WIDTH)
            ns = slice(g * SSM_STATE, (g + 1) * SSM_STATE)
            y_parts.append(_dot_nt(cb_bf[:, ns], st[rows].astype(BF16)))
            upd = _dot_tn(xw_bf[:, rows], bb_bf[:, ns])
            for hh in range(heads_per_group):
                h = g * heads_per_group + hh
                r = slice(h * SSM_HEAD_DIM, (h + 1) * SSM_HEAD_DIM)
                dec = cdec_ref[(i * SAMPLE_BATCH_BLOCK + j) * SSM_HEADS + h]
                new_ref[j, r, :] = st[r] * dec + upd[hh * SSM_HEAD_DIM:(hh + 1) * SSM_HEAD_DIM]
        yoff_ref[:, j, :] = jnp.concatenate(y_parts, axis=1)


def _ssd_sample_state(cdec_flat, state, c3, b3, xw3):
    steps, nb = c3.shape[0], c3.shape[1]
    bb = SAMPLE_BATCH_BLOCK
    tok = lambda w: pl.BlockSpec((steps, bb, w), lambda i: (0, i, 0))
    st_spec = pl.BlockSpec((bb, SSM_WIDTH, SSM_STATE), lambda i: (i, 0, 0))
    return pl.pallas_call(
        _ssd_sample_state_body, grid=(nb // bb,),
        in_specs=[pl.BlockSpec(memory_space=pltpu.SMEM), st_spec, tok(BC_WIDTH), tok(BC_WIDTH), tok(SSM_WIDTH)],
        out_specs=[st_spec, tok(SSM_WIDTH)],
        out_shape=[jax.ShapeDtypeStruct(state.shape, F32), jax.ShapeDtypeStruct((steps, nb, SSM_WIDTH), F32)],
        compiler_params=_params("parallel"), name="ssd_sample_state")(cdec_flat, state, c3, b3, xw3)


def _sample_finish_body(ypart_ref, yoff_ref, ea_ref, z_ref, nw_ref, q_ref, k_ref, qw_ref, kw_ref, g_ref, e_ref,
                        yssm_ref, qn_ref, kn_ref):
    y = ypart_ref[...] + yoff_ref[...] * ea_ref[...]
    yssm_ref[...] = _group_rmsnorm(y * _silu(z_ref[...]), nw_ref[...])
    g_mat = g_ref[...]
    e_mat = e_ref[...]
    qn = _head_rmsnorm(q_ref[...], g_mat, e_mat, qw_ref[...]) * ATTN_SCALE
    head = lambda h: qn[:, h * ATTN_HEAD_DIM:(h + 1) * ATTN_HEAD_DIM]
    qn_ref[...] = jnp.concatenate(
        [head(n * Q_PER_KV + g) for g in range(Q_PER_KV) for n in range(ATTN_KV_HEADS)], axis=1)
    kn_ref[...] = _head_rmsnorm(k_ref[...], g_mat[:KV_WIDTH], e_mat[:, :KV_WIDTH], kw_ref[...])


def _sample_finish(ypart, yoff, ea, z, norm_w, q, k, qw, kw, g_mat, e_mat):
    rows = z.shape[0]
    f = lambda w: jax.ShapeDtypeStruct((rows, w), F32)
    return pl.pallas_call(_sample_finish_body, out_shape=[f(SSM_WIDTH), f(ATTN_WIDTH), f(KV_WIDTH)],
                          compiler_params=pltpu.CompilerParams(vmem_limit_bytes=VMEM_LIMIT),
                          name="sample_finish")(ypart, yoff, ea, z, norm_w, q, k, qw, kw, g_mat, e_mat)


def _attn_sample_body(q_ref, kn_ref, vn_ref, z_ref, ckt_ref, cvt_ref, biasc_ref, biasn_ref,
                      y_ref, kot_ref, vot_ref):
    steps = q_ref.shape[0]
    bb = SAMPLE_BATCH_BLOCK
    blk = Q_PER_KV * steps
    rows = ATTN_KV_HEADS * blk
    pad = jnp.zeros((SUBLANES - steps, KV_WIDTH), F32)
    lane_head = _lane_head((blk, KV_WIDTH))
    zero = jnp.zeros((blk, KV_WIDTH), F32)

    s_c, s_n, k8, v8 = [], [], [], []
    for j in range(bb):
        q = q_ref[:, j, :]
        qg = jnp.concatenate([q[:, g * KV_WIDTH:(g + 1) * KV_WIDTH] for g in range(Q_PER_KV)], axis=0)
        qx = jnp.concatenate([jnp.where(lane_head == n, qg, zero) for n in range(ATTN_KV_HEADS)], axis=0)
        qx = qx.astype(BF16)
        k8.append(jnp.concatenate([kn_ref[:, j, :], pad], axis=0))
        v8.append(jnp.concatenate([vn_ref[:, j, :], pad], axis=0))
        s_c.append(_dot(qx, ckt_ref[j].astype(BF16)))
        s_n.append(_dot_nt(qx, k8[j].astype(BF16)))
    s_c = jnp.concatenate(s_c, axis=0) + biasc_ref[...]
    s_n = jnp.concatenate(s_n, axis=0) + biasn_ref[...]
    m = jnp.maximum(jnp.max(s_c, axis=-1, keepdims=True), jnp.max(s_n, axis=-1, keepdims=True))
    p_c = jnp.exp(s_c - m)
    p_n = jnp.exp(s_n - m)
    inv = 1.0 / (jnp.sum(p_c, axis=-1, keepdims=True) + jnp.sum(p_n, axis=-1, keepdims=True))
    p_c = (p_c * inv).astype(BF16)
    p_n = (p_n * inv).astype(BF16)

    lane = lax.broadcasted_iota(jnp.int32, (KV_WIDTH, WINDOW), 1)
    for j in range(bb):
        r = slice(j * rows, (j + 1) * rows)
        o = _dot_nt(p_c[r], cvt_ref[j].astype(BF16)) + _dot(p_n[r], v8[j].astype(BF16))
        og = zero
        for n in range(ATTN_KV_HEADS):
            og = og + jnp.where(lane_head == n, o[n * blk:(n + 1) * blk], zero)
        y = jnp.concatenate(
            [og[g * steps:(g + 1) * steps, n * ATTN_HEAD_DIM:(n + 1) * ATTN_HEAD_DIM]
             for n in range(ATTN_KV_HEADS) for g in range(Q_PER_KV)], axis=1)
        y_ref[:, j, :] = y * _silu(z_ref[:, j, :])

        for new8, old_ref, out_ref in ((k8[j], ckt_ref, kot_ref), (v8[j], cvt_ref, vot_ref)):
            tail_rows = jnp.concatenate([new8[steps:], new8[:steps]], axis=0)
            block = jnp.concatenate([jnp.zeros((WINDOW - SUBLANES, KV_WIDTH), F32), tail_rows], axis=0)
            shifted = pltpu.roll(old_ref[j], WINDOW - steps, axis=1)
            out_ref[j] = jnp.where(lane >= WINDOW - steps, block.T, shifted)


def _attn_sample(q3, kn3, vn3, z3, cache_kt, cache_vt, bias_c, bias_n):
    steps, nb = q3.shape[0], q3.shape[1]
    bb = SAMPLE_BATCH_BLOCK
    tok = lambda w: pl.BlockSpec((steps, bb, w), lambda i: (0, i, 0))
    cache_spec = pl.BlockSpec((bb, KV_WIDTH, WINDOW), lambda i: (i, 0, 0))
    return pl.pallas_call(
        _attn_sample_body, grid=(nb // bb,),
        in_specs=[tok(ATTN_WIDTH), tok(KV_WIDTH), tok(KV_WIDTH), tok(ATTN_WIDTH), cache_spec, cache_spec,
                  _const_spec(bias_c.shape), _const_spec(bias_n.shape)],
        out_specs=[tok(ATTN_WIDTH), cache_spec, cache_spec],
        out_shape=[jax.ShapeDtypeStruct((steps, nb, ATTN_WIDTH), F32),
                   jax.ShapeDtypeStruct(cache_kt.shape, F32), jax.ShapeDtypeStruct(cache_vt.shape, F32)],
        compiler_params=_params("parallel"), name="attn_sample")(
            q3, kn3, vn3, z3, cache_kt, cache_vt, bias_c, bias_n)


def _static_tables(steps):
    lanes = np.arange(ATTN_WIDTH)
    g_mat = np.zeros((ATTN_WIDTH, LANES), np.float32)
    g_mat[lanes, lanes // ATTN_HEAD_DIM] = 1.0
    e_mat = g_mat.T.copy()
    bc = np.arange(BC_WIDTH)
    gh_mat = np.zeros((BC_WIDTH, LANES), np.float32)
    for h in range(SSM_HEADS):
        gh_mat[bc // SSM_STATE == h // (SSM_HEADS // SSM_GROUPS), h] = 1.0
    T = CHUNK
    dist = np.arange(T)[:, None] - (np.arange(2 * T) - T)[None, :]
    first = np.broadcast_to((np.arange(2 * T) >= T)[None, :], dist.shape)
    prompt_buckets = np.stack([_bucket_or_masked(dist, first), _bucket_or_masked(dist)])
    dist_c = (np.arange(steps) + WINDOW)[:, None] - np.arange(WINDOW)[None, :]
    dist_n = np.arange(steps)[:, None] - np.arange(SUBLANES)[None, :]
    real = np.broadcast_to((np.arange(SUBLANES) < steps)[None, :], dist_n.shape)
    return dict(g=g_mat, e=e_mat, gh=gh_mat, prompt_buckets=prompt_buckets,
                cache_buckets=_bucket_or_masked(dist_c)[None], new_buckets=_bucket_or_masked(dist_n, real)[None])


def kernel(x_prompt, x_sample, cache_k, cache_v, state_ssm, state_conv, norm_w, w_in, conv_w, conv_b, dt_bias,
           a_log, d_skip, ssm_norm_w, q_norm_w, k_norm_w, sinks, rel_table, w_out):
    assert w_in.shape[0] == 1, "single-layer kernel"
    batch, seq, _ = x_prompt.shape
    nb, steps, _ = x_sample.shape
    tab = _static_tables(steps)
    g_mat = jnp.asarray(tab["g"], BF16)
    e_mat = jnp.asarray(tab["e"], BF16)
    gh_mat = jnp.asarray(tab["gh"], BF16)

    w_t = jnp.transpose(w_in[0]).astype(BF16)
    wo_all = w_out[0].astype(BF16)

    row = lambda v, width: jnp.pad(v.reshape(1, -1), ((0, 0), (0, width - v.size)))
    nw = row(norm_w[0], D_MODEL)
    cw = conv_w[0]
    cb = row(conv_b[0], CONV_DIM)
    dtb = row(dt_bias[0], LANES)
    alog = row(a_log[0], LANES)
    dskip = jnp.repeat(d_skip[0], SSM_HEAD_DIM).reshape(1, SSM_WIDTH)
    snw = row(ssm_norm_w[0], SSM_WIDTH)
    qw = jnp.tile(q_norm_w[0], ATTN_HEADS).reshape(1, ATTN_WIDTH)
    kw = jnp.tile(k_norm_w[0], ATTN_KV_HEADS).reshape(1, KV_WIDTH)
    sink = sinks[0]
    rel_flat = rel_table.reshape(-1)

    xp = x_prompt.reshape(batch * seq, D_MODEL)
    gz, xs_p, b_p, c_p, dt_p, q_t, k, v, gza_t, tail_p = _inproj_prompt(
        xp, nw, w_t, cw, cb, dtb, batch, seq)
    bias_t = _bias_tables_t(rel_flat * LOG2E, jnp.asarray(tab["prompt_buckets"].transpose(0, 2, 1)))
    qw_t = jnp.broadcast_to((qw * (ATTN_SCALE * LOG2E)).reshape(ATTN_WIDTH, 1),
                            (ATTN_WIDTH, CHUNKS_PER_STEP * CHUNK))
    sink_rows = jnp.repeat(sink.reshape(ATTN_KV_HEADS, Q_PER_KV) * LOG2E, CHUNK, axis=1)
    sink_rows = sink_rows.reshape(ATTN_KV_HEADS, 1, -1)
    y_p, st_p, k_p, v_p = _mixer(gz, xs_p, b_p, c_p, dt_p, alog, dskip, snw, e_mat, q_t, k, v, gza_t, qw_t, kw,
                                 g_mat[:KV_WIDTH], bias_t, sink_rows, xp, wo_all, batch, seq)
    y_p = y_p.reshape(batch, seq, D_MODEL)
    conv_p = tail_p[:, SUBLANES - (CONV_WIDTH - 1):, :]

    xs = jnp.swapaxes(x_sample, 0, 1).reshape(steps * nb, D_MODEL)
    z, xbc, dt, q, k, v, za = _inproj(xs, nw, w_t)
    t3 = lambda a: a.reshape(steps, nb, a.shape[-1])
    sconv3 = jnp.swapaxes(state_conv[0], 0, 1)
    ypart, ea, xw, b3, c3, cdec, conv_s3 = _ssd_sample_vec(
        t3(z), t3(xbc), t3(dt), sconv3, cw, cb, dtb, alog, dskip, gh_mat, e_mat)
    state_in = state_ssm[0].reshape(nb, SSM_WIDTH, SSM_STATE)
    st_s, yoff = _ssd_sample_state(cdec[:, :SSM_HEADS].reshape(-1), state_in, c3, b3, xw)
    f2 = lambda a: a.reshape(steps * nb, a.shape[-1])
    y_ssm, qn, kn = _sample_finish(f2(ypart), f2(yoff), f2(ea), z, snw, q, k, qw, kw, g_mat, e_mat)
    bias_c = _bias_tables(rel_flat, jnp.asarray(tab["cache_buckets"])).reshape(ATTN_HEADS * steps, WINDOW)
    bias_n = _bias_tables(rel_flat, jnp.asarray(tab["new_buckets"])).reshape(ATTN_HEADS * steps, SUBLANES)
    bias_n = bias_n.at[:, steps].set(jnp.repeat(sink, steps))
    bias_c = jnp.tile(bias_c, (SAMPLE_BATCH_BLOCK, 1))
    bias_n = jnp.tile(bias_n, (SAMPLE_BATCH_BLOCK, 1))
    to_t = lambda a: jnp.transpose(a[0], (0, 2, 3, 1)).reshape(nb, KV_WIDTH, WINDOW)
    from_t = lambda a: jnp.transpose(a.reshape(nb, ATTN_KV_HEADS, ATTN_HEAD_DIM, WINDOW), (0, 3, 1, 2))[None]
    y_attn3, k_st, v_st = _attn_sample(t3(qn), t3(kn), t3(v), t3(za), to_t(cache_k), to_t(cache_v),
                                       bias_c, bias_n)
    k_s, v_s = from_t(k_st), from_t(v_st)
    y_s = _outproj(y_ssm, f2(y_attn3), xs, wo_all)
    y_s = jnp.swapaxes(y_s.reshape(steps, nb, D_MODEL), 0, 1)

    kv5 = lambda a: a.reshape(1, a.shape[0], WINDOW, ATTN_KV_HEADS, ATTN_HEAD_DIM)
    st5 = lambda a: a.reshape(1, a.shape[0], SSM_HEADS, SSM_HEAD_DIM, SSM_STATE)
    return (y_p, y_s, kv5(k_p), kv5(v_p), st5(st_p), conv_p[None],
            k_s, v_s, st5(st_s), jnp.swapaxes(conv_s3, 0, 1)[None])
```

```python
import functools
import math

import numpy as np
import jax
import jax.numpy as jnp
from jax import lax
from jax.experimental import pallas as pl
from jax.experimental.pallas import tpu as pltpu

F32 = jnp.float32
BF16 = jnp.bfloat16

D_MODEL = 1024
SSM_HEADS = 16
SSM_HEAD_DIM = 64
SSM_WIDTH = SSM_HEADS * SSM_HEAD_DIM
SSM_GROUPS = 2
SSM_STATE = 128
GROUP_WIDTH = SSM_WIDTH // SSM_GROUPS
BC_WIDTH = SSM_GROUPS * SSM_STATE
CONV_WIDTH = 4
CONV_DIM = SSM_WIDTH + 2 * BC_WIDTH
CHUNK = 128
ATTN_HEADS = 16
ATTN_KV_HEADS = 4
Q_PER_KV = ATTN_HEADS // ATTN_KV_HEADS
ATTN_HEAD_DIM = 64
ATTN_WIDTH = ATTN_HEADS * ATTN_HEAD_DIM
KV_WIDTH = ATTN_KV_HEADS * ATTN_HEAD_DIM
WINDOW = 128
ATTN_SCALE = ATTN_HEAD_DIM ** -0.5
REL_BUCKETS = 32
REL_MAX_DIST = 128
EPS = 1e-6
LOG2E = 1.0 / math.log(2.0)
NEG = -1e30

LANES = 128
SUBLANES = 8
MXU_WIDTH = 256
VMEM_LIMIT = 56 * 1024 * 1024
def _in_proj_rows():
    widths = (("z", SSM_WIDTH), ("xbc", CONV_DIM), ("dt", SSM_HEADS), ("q", ATTN_WIDTH), ("k", KV_WIDTH),
              ("v", KV_WIDTH), ("za", ATTN_WIDTH))
    rows, start = {}, 0
    for name, width in widths:
        rows[name] = slice(start, start + width)
        start += width
    return rows


IN_ROWS = _in_proj_rows()
DT_ROWS = slice(IN_ROWS["dt"].start, IN_ROWS["dt"].start + LANES)

PROJ_ROWS = 512
CHUNKS_PER_STEP = 4
SAMPLE_BATCH_BLOCK = 8


def _dot(a, b):
    return jnp.dot(a, b, preferred_element_type=F32)


def _dot_nt(a, b):
    return lax.dot_general(a, b, (((1,), (1,)), ((), ())), preferred_element_type=F32)


def _dot_tn(a, b):
    return lax.dot_general(a, b, (((0,), (0,)), ((), ())), preferred_element_type=F32)


def _split2(v):
    hi = v.astype(BF16)
    lo = (v - hi.astype(F32)).astype(BF16)
    return hi, lo


def _dot_sel(v, m):
    hi, lo = _split2(v)
    if 2 * v.shape[1] <= MXU_WIDTH:
        return _dot(jnp.concatenate([hi, lo], axis=1), jnp.concatenate([m, m], axis=0))
    return _dot(hi, m) + _dot(lo, m)


def _dot_sel3(m, v):
    hi = v.astype(BF16)
    r1 = v - hi.astype(F32)
    mid = r1.astype(BF16)
    lo = (r1 - mid.astype(F32)).astype(BF16)
    return _dot(m, hi) + _dot(m, mid) + _dot(m, lo)


def _silu(x):
    return x / (1.0 + jnp.exp(-x))


def _softplus(x):
    return jnp.maximum(x, 0.0) + jnp.log1p(jnp.exp(-jnp.abs(x)))


def _params(*sem):
    return pltpu.CompilerParams(dimension_semantics=sem, vmem_limit_bytes=VMEM_LIMIT)


def _const_spec(shape):
    nd = len(shape)
    return pl.BlockSpec(shape, lambda *_: (0,) * nd)


def _normed_input(x_ref, nw_ref):
    x = x_ref[...]
    ms = jnp.mean(x * x, axis=-1, keepdims=True)
    return (x * lax.rsqrt(ms + EPS) * nw_ref[...]).astype(BF16)


def _dt_projection(h, wt_ref):
    raw = _dot_nt(h, wt_ref[DT_ROWS, :])
    return jnp.where(lax.broadcasted_iota(jnp.int32, raw.shape, 1) < SSM_HEADS, raw, 0.0)


def _inproj_body(x_ref, nw_ref, wt_ref, *out_refs):
    h = _normed_input(x_ref, nw_ref)
    for name, o_ref in zip(IN_ROWS, out_refs):
        o_ref[...] = _dt_projection(h, wt_ref) if name == "dt" else _dot_nt(h, wt_ref[IN_ROWS[name], :])


def _inproj(x2d, norm_w, w_t):
    rows = x2d.shape[0]
    tm = min(PROJ_ROWS, rows)
    widths = [LANES if name == "dt" else r.stop - r.start for name, r in IN_ROWS.items()]
    in_specs = [pl.BlockSpec((tm, D_MODEL), lambda i: (i, 0)), _const_spec((1, D_MODEL)),
                pl.BlockSpec(w_t.shape, lambda i: (0, 0), pipeline_mode=pl.Buffered(1))]
    out_specs = [pl.BlockSpec((tm, w), lambda i: (i, 0)) for w in widths]
    out_shape = [jax.ShapeDtypeStruct((rows, w), F32) for w in widths]
    return pl.pallas_call(
        _inproj_body, grid=(rows // tm,), in_specs=in_specs, out_specs=out_specs, out_shape=out_shape,
        compiler_params=_params("parallel"), name="inproj")(x2d, norm_w, w_t)


def _shift_rows(u, prev_tail, k):
    rows, width = u.shape
    tiles = jnp.concatenate([prev_tail, u], axis=0).reshape(rows // SUBLANES + 1, SUBLANES, width)
    rot = jnp.concatenate([tiles[:, SUBLANES - k:], tiles[:, :SUBLANES - k]], axis=1)
    first = lax.broadcasted_iota(jnp.int32, (1, SUBLANES, width), 1) < k
    return jnp.where(first, rot[:-1], rot[1:]).reshape(rows, width)


def _inproj_prompt_body(steps_per_seq, x_ref, nw_ref, wt_ref, cw_ref, cb_ref, dtb_ref,
                        gz_ref, xs_ref, b_ref, c_ref, dt_ref, qt_ref, k_ref, v_ref, gzat_ref, tail_ref, tail_sc):
    @pl.when(pl.program_id(0) % steps_per_seq == 0)
    def _():
        tail_sc[...] = jnp.zeros_like(tail_sc)

    h = _normed_input(x_ref, nw_ref)
    rows = h.shape[0]
    w_tile = lambda name, j: wt_ref[IN_ROWS[name].start + j * MXU_WIDTH:IN_ROWS[name].start + (j + 1) * MXU_WIDTH, :]
    n_side = SSM_WIDTH // MXU_WIDTH
    for j in range(CONV_DIM // MXU_WIDTH):
        cols = slice(j * MXU_WIDTH, (j + 1) * MXU_WIDTH)
        partner = w_tile("z", j) if j < n_side else wt_ref[IN_ROWS["k" if j == n_side else "v"], :]
        both = _dot_nt(h, jnp.concatenate([w_tile("xbc", j), partner], axis=0))
        u = both[:, :MXU_WIDTH]
        if j < n_side:
            gz_ref[:, cols] = _silu(both[:, MXU_WIDTH:])
        elif j == n_side:
            k_ref[...] = both[:, MXU_WIDTH:]
        else:
            v_ref[...] = both[:, MXU_WIDTH:]
        prev_tail = tail_sc[:, cols]
        conv = cb_ref[:, cols] + u * cw_ref[CONV_WIDTH - 1:CONV_WIDTH, cols]
        for k in range(1, CONV_WIDTH):
            tap = CONV_WIDTH - 1 - k
            conv = conv + _shift_rows(u, prev_tail, k) * cw_ref[tap:tap + 1, cols]
        new_tail = u[rows - SUBLANES:, :]
        tail_sc[:, cols] = new_tail
        tail_ref[0, :, cols] = new_tail
        act = _silu(conv)
        if j < SSM_WIDTH // MXU_WIDTH:
            xs_ref[:, cols] = act
        elif j == SSM_WIDTH // MXU_WIDTH:
            b_ref[...] = act.astype(BF16)
        else:
            c_ref[...] = act.astype(BF16)

    for j in range(ATTN_WIDTH // MXU_WIDTH):
        feats = slice(j * MXU_WIDTH, (j + 1) * MXU_WIDTH)
        gzat_ref[feats, :] = _silu(_dot_nt(w_tile("za", j), h))
    dt_ref[...] = _softplus(_dt_projection(h, wt_ref) + dtb_ref[...])
    qt_ref[...] = _dot_nt(wt_ref[IN_ROWS["q"], :], h)


def _inproj_prompt(x2d, norm_w, w_t, conv_w, conv_b, dtb, batch, seq):
    rows = x2d.shape[0]
    tm = PROJ_ROWS
    steps_per_seq = seq // tm
    resident = lambda a: pl.BlockSpec(a.shape, lambda i: (0, 0), pipeline_mode=pl.Buffered(1))
    rowblk = lambda w: pl.BlockSpec((tm, w), lambda i: (i, 0))
    colblk = pl.BlockSpec((ATTN_WIDTH, tm), lambda i: (0, i))
    in_specs = ([rowblk(D_MODEL), _const_spec((1, D_MODEL)), resident(w_t)]
                + [_const_spec(conv_w.shape), _const_spec(conv_b.shape), _const_spec(dtb.shape)])
    out_specs = [rowblk(SSM_WIDTH), rowblk(SSM_WIDTH), rowblk(BC_WIDTH), rowblk(BC_WIDTH), rowblk(LANES),
                 colblk, rowblk(KV_WIDTH), rowblk(KV_WIDTH), colblk,
                 pl.BlockSpec((1, SUBLANES, CONV_DIM), lambda i: (i // steps_per_seq, 0, 0))]
    f = lambda r, c, dt=F32: jax.ShapeDtypeStruct((r, c), dt)
    out_shape = [f(rows, SSM_WIDTH), f(rows, SSM_WIDTH), f(rows, BC_WIDTH, BF16), f(rows, BC_WIDTH, BF16),
                 f(rows, LANES), f(ATTN_WIDTH, rows), f(rows, KV_WIDTH), f(rows, KV_WIDTH), f(ATTN_WIDTH, rows),
                 jax.ShapeDtypeStruct((batch, SUBLANES, CONV_DIM), F32)]
    return pl.pallas_call(
        functools.partial(_inproj_prompt_body, steps_per_seq), grid=(rows // tm,), in_specs=in_specs,
        out_specs=out_specs, out_shape=out_shape, scratch_shapes=[pltpu.VMEM((SUBLANES, CONV_DIM), F32)],
        compiler_params=_params("arbitrary"), name="inproj_prompt")(
            x2d, norm_w, w_t, conv_w, conv_b, dtb)


def _outproj_body(ys_ref, ya_ref, x_ref, w_ref, o_ref):
    o_ref[...] = (x_ref[...] + _dot(ys_ref[...].astype(BF16), w_ref[:SSM_WIDTH, :])
                  + _dot(ya_ref[...].astype(BF16), w_ref[SSM_WIDTH:, :]))


def _outproj(y_ssm, y_attn, x2d, w_out):
    rows = x2d.shape[0]
    tm = min(PROJ_ROWS, rows)
    row_spec = pl.BlockSpec((tm, D_MODEL), lambda i: (i, 0))
    w_spec = pl.BlockSpec(w_out.shape, lambda i: (0, 0), pipeline_mode=pl.Buffered(1))
    return pl.pallas_call(
        _outproj_body, grid=(rows // tm,), in_specs=[row_spec, row_spec, row_spec, w_spec],
        out_specs=row_spec, out_shape=jax.ShapeDtypeStruct((rows, D_MODEL), F32),
        compiler_params=_params("parallel"), name="outproj")(y_ssm, y_attn, x2d, w_out)


def _group_rmsnorm(gy, norm_w):
    parts = []
    for g in range(SSM_GROUPS):
        blk = gy[:, g * GROUP_WIDTH:(g + 1) * GROUP_WIDTH]
        ms = jnp.mean(blk * blk, axis=-1, keepdims=True)
        parts.append(blk * lax.rsqrt(ms + EPS))
    return jnp.concatenate(parts, axis=1) * norm_w


def _ssd_chunk(gz, xs, b_bf, c_bf, dt, a_neg, dskip, norm_w, e_mat, state):
    xs_bf = xs.astype(BF16)

    a = dt * a_neg
    li = lax.broadcasted_iota(jnp.int32, (CHUNK, CHUNK), 0)
    si = lax.broadcasted_iota(jnp.int32, (CHUNK, CHUNK), 1)
    causal = li >= si
    a_cum = _dot_sel3(jnp.where(causal, 1.0, 0.0).astype(BF16), a)
    a2 = a_cum * LOG2E
    row_term = a2.T - jnp.log2(dt.T)
    ea_full = _dot_sel(jnp.exp(a_cum), e_mat)
    w_full = _dot((dt * jnp.exp(a_cum[CHUNK - 1:CHUNK, :] - a_cum)).astype(BF16), e_mat)

    cb = [_dot_nt(c_bf[:, g * SSM_STATE:(g + 1) * SSM_STATE], b_bf[:, g * SSM_STATE:(g + 1) * SSM_STATE])
          for g in range(SSM_GROUPS)]
    half = lax.broadcasted_iota(jnp.int32, (CHUNK, LANES), 1) < SSM_HEAD_DIM
    heads_per_group = SSM_HEADS // SSM_GROUPS
    y_parts = []
    for pair in range(SSM_HEADS // 2):
        blocks = []
        for h in (2 * pair, 2 * pair + 1):
            seg = a2[:, h:h + 1] - row_term[h:h + 1, :]
            decay_dt = jnp.exp2(jnp.where(causal, seg, -jnp.inf))
            blocks.append((cb[h // heads_per_group] * decay_dt).astype(BF16))
        lhs = jnp.concatenate(blocks, axis=1)
        xp = xs_bf[:, pair * LANES:(pair + 1) * LANES]
        zero = jnp.zeros_like(xp)
        rhs = jnp.concatenate([jnp.where(half, xp, zero), jnp.where(half, zero, xp)], axis=0)
        y_parts.append(_dot(lhs, rhs))
    y_diag = jnp.concatenate(y_parts, axis=1)

    state_bf = state.astype(BF16)
    xw_bf = (xs * w_full).astype(BF16)
    y_off, upd = [], []
    for g in range(SSM_GROUPS):
        cols = slice(g * GROUP_WIDTH, (g + 1) * GROUP_WIDTH)
        ns = slice(g * SSM_STATE, (g + 1) * SSM_STATE)
        y_off.append(_dot(c_bf[:, ns], state_bf[:, cols]))
        upd.append(_dot_tn(b_bf[:, ns], xw_bf[:, cols]))
    y = y_diag + jnp.concatenate(y_off, axis=1) * ea_full + dskip * xs
    new_state = state * ea_full[CHUNK - 1:CHUNK, :] + jnp.concatenate(upd, axis=1)
    return _group_rmsnorm(y * gz, norm_w), new_state


def _rel_bucket_np(dist):
    max_exact = REL_BUCKETS // 2
    d_f = np.maximum(dist, 1).astype(np.float32)
    large = max_exact + (np.log(d_f / np.float32(max_exact)) / np.float32(math.log(REL_MAX_DIST / max_exact))
                         * np.float32(REL_BUCKETS - max_exact)).astype(np.int32)
    return np.where(dist < max_exact, dist, np.minimum(large, REL_BUCKETS - 1)).astype(np.int32)


def _bucket_or_masked(dist, extra_mask=None):
    ok = (dist >= 0) & (dist <= WINDOW)
    if extra_mask is not None:
        ok = ok & extra_mask
    return np.where(ok, _rel_bucket_np(np.clip(dist, 0, WINDOW)), -1).astype(np.int32)


def _bias_body(rel_ref, bucket_ref, o_ref):
    bucket = bucket_ref[0]

    def per_head(h, carry):
        acc = jnp.full(bucket.shape, NEG, F32)
        for bkt in range(REL_BUCKETS):
            acc = jnp.where(bucket == bkt, rel_ref[bkt * ATTN_HEADS + h], acc)
        o_ref[0, h] = acc
        return carry

    lax.fori_loop(0, ATTN_HEADS, per_head, 0)


def _bias_tables(rel_flat, buckets):
    nv, lq, lk = buckets.shape
    return pl.pallas_call(
        _bias_body, grid=(nv,),
        in_specs=[pl.BlockSpec(memory_space=pltpu.SMEM), pl.BlockSpec((1, lq, lk), lambda v: (v, 0, 0))],
        out_specs=pl.BlockSpec((1, ATTN_HEADS, lq, lk), lambda v: (v, 0, 0, 0)),
        out_shape=jax.ShapeDtypeStruct((nv, ATTN_HEADS, lq, lk), F32),
        compiler_params=_params("arbitrary"), name="rel_bias")(rel_flat, buckets)


def _bias_t_body(rel_ref, bucket_ref, o_ref):
    variants = [bucket_ref[v] for v in range(bucket_ref.shape[0])]
    union = functools.reduce(jnp.maximum, variants)
    lq = union.shape[1]

    def per_kv_head(n, carry):
        for g in range(Q_PER_KV):
            acc = jnp.full(union.shape, NEG, F32)
            for bkt in range(REL_BUCKETS):
                acc = jnp.where(union == bkt, rel_ref[bkt * ATTN_HEADS + n * Q_PER_KV + g], acc)
            for v, bucket in enumerate(variants):
                o_ref[v, n, :, g * lq:(g + 1) * lq] = jnp.where(bucket >= 0, acc, NEG)
        return carry

    lax.fori_loop(0, ATTN_KV_HEADS, per_kv_head, 0)


def _bias_tables_t(rel_flat, buckets_t):
    nv, lk, lq = buckets_t.shape
    out_dims = (nv, ATTN_KV_HEADS, lk, Q_PER_KV * lq)
    return pl.pallas_call(
        _bias_t_body,
        in_specs=[pl.BlockSpec(memory_space=pltpu.SMEM), pl.BlockSpec(memory_space=pltpu.VMEM)],
        out_specs=pl.BlockSpec(memory_space=pltpu.VMEM),
        out_shape=jax.ShapeDtypeStruct(out_dims, F32),
        compiler_params=pltpu.CompilerParams(vmem_limit_bytes=VMEM_LIMIT), name="rel_bias_t")(rel_flat, buckets_t)


def _head_rmsnorm(x, g_mat, e_mat, w):
    ms = _dot_sel(x * x, g_mat) * (1.0 / ATTN_HEAD_DIM)
    return x * _dot_sel(lax.rsqrt(ms + EPS), e_mat) * w


def _lane_head(shape):
    return lax.broadcasted_iota(jnp.int32, shape, 1) // ATTN_HEAD_DIM


def _sink_column(sink_ref, n, rows_per_head):
    return jnp.concatenate(
        [jnp.full((rows_per_head, 1), sink_ref[n * Q_PER_KV + g], F32) for g in range(Q_PER_KV)], axis=0)


def _softmax_with_sink(s, sink):
    m = jnp.maximum(jnp.max(s, axis=-1, keepdims=True), sink)
    p = jnp.exp(s - m)
    denom = jnp.sum(p, axis=-1, keepdims=True) + jnp.exp(sink - m)
    return p / denom


def _attn_block(q_blk, kcat, vcat_t, bias_at, sink_ref):
    T = CHUNK
    lane_head = _lane_head((2 * T, KV_WIDTH))
    zero = jnp.zeros((2 * T, KV_WIDTH), BF16)
    head = lambda h: q_blk[h * ATTN_HEAD_DIM:(h + 1) * ATTN_HEAD_DIM]
    q_cols = jnp.concatenate(
        [jnp.concatenate([head(n * Q_PER_KV + g) for n in range(ATTN_KV_HEADS)], axis=0)
         for g in range(Q_PER_KV)], axis=1)
    row_head = lax.broadcasted_iota(jnp.int32, (KV_WIDTH, 2 * T), 0) // ATTN_HEAD_DIM
    probs, vals, inv = [], [], {}
    for n in range(ATTN_KV_HEADS):
        s = _dot(jnp.where(lane_head == n, kcat, zero), q_cols)
        sink = sink_ref[n]
        cols = []
        for g in range(Q_PER_KV):
            c = slice(g * T, (g + 1) * T)
            sg = s[:, c] + bias_at(n, c)
            m = jnp.maximum(jnp.max(sg, axis=0, keepdims=True), sink[:, c])
            p = jnp.exp2(sg - m)
            inv[n, g] = 1.0 / (jnp.sum(p, axis=0, keepdims=True) + jnp.exp2(sink[:, c] - m))
            cols.append(p.astype(BF16))
        probs.append(jnp.concatenate(cols, axis=1))
        vals.append(jnp.where(row_head == n, vcat_t, zero.T))
    o_t = _dot(jnp.concatenate(vals, axis=1), jnp.concatenate(probs, axis=0))
    return jnp.concatenate(
        [o_t[n * ATTN_HEAD_DIM:(n + 1) * ATTN_HEAD_DIM, g * T:(g + 1) * T] * inv[n, g]
         for n in range(ATTN_KV_HEADS) for g in range(Q_PER_KV)], axis=0)


def _mixer_body(gz_ref, xs_ref, b_ref, c_ref, dt_ref, alog_ref, dskip_ref, nw_ref, e_ref,
                qt_ref, k_ref, v_ref, gzt_ref, qwt_ref, kw_ref, g_ref, bias_ref, sink_ref, x_ref, wo_ref,
                y_ref, st_ref, kn_ref, vn_ref,
                state_sc, kcat_sc, vcat_t_sc, yssm_sc, yattn_t_sc):
    T = CHUNK
    step = pl.program_id(1)
    cols_step = CHUNKS_PER_STEP * T

    @pl.when(step == 0)
    def _():
        state_sc[...] = jnp.zeros_like(state_sc)
        kcat_sc[0:T, :] = jnp.zeros((T, KV_WIDTH), BF16)
        vcat_t_sc[:, 0:T] = jnp.zeros((KV_WIDTH, T), BF16)

    q3 = qt_ref[...].reshape(ATTN_HEADS, ATTN_HEAD_DIM, cols_step)
    ms = jnp.mean(q3 * q3, axis=1, keepdims=True)
    qn = ((q3 * lax.rsqrt(ms + EPS)).reshape(ATTN_WIDTH, cols_step) * qwt_ref[...]).astype(BF16)
    e_mat = e_ref[...]
    kn = _head_rmsnorm(k_ref[...], g_ref[...], e_mat[:, :KV_WIDTH], kw_ref[...])
    v = v_ref[...]
    kn_ref[0] = kn[cols_step - T:]
    vn_ref[0] = v[cols_step - T:]
    kcat_sc[T:, :] = kn.astype(BF16)
    vcat_t_sc[:, T:] = v.T.astype(BF16)

    a_neg = -jnp.exp(alog_ref[...])
    state = state_sc[...]
    first_variant = jnp.minimum(step, 1)
    for j in range(CHUNKS_PER_STEP):
        r = slice(j * T, (j + 1) * T)
        y, state = _ssd_chunk(gz_ref[r, :], xs_ref[r, :], b_ref[r, :], c_ref[r, :], dt_ref[r, :], a_neg,
                              dskip_ref[...], nw_ref[...], e_mat, state)
        yssm_sc[r, :] = y.astype(BF16)
        variant = first_variant if j == 0 else 1
        y_t = _attn_block(qn[:, r], kcat_sc[j * T:(j + 2) * T, :], vcat_t_sc[:, j * T:(j + 2) * T],
                          lambda n, c, variant=variant: bias_ref[variant, n, :, c], sink_ref)
        yattn_t_sc[:, r] = (y_t * gzt_ref[:, r]).astype(BF16)
    state_sc[...] = state
    kcat_sc[0:T, :] = kcat_sc[cols_step:, :]
    vcat_t_sc[:, 0:T] = vcat_t_sc[:, cols_step:]

    y_ref[...] = (x_ref[...] + _dot(yssm_sc[...], wo_ref[:SSM_WIDTH, :])
                  + _dot_tn(yattn_t_sc[...], wo_ref[SSM_WIDTH:, :]))

    @pl.when(step == pl.num_programs(1) - 1)
    def _():
        st_ref[0] = state.T


def _mixer(gz, xs, b, c, dt, alog, dskip, norm_w, e_mat, q_t, k, v, gz_t, qw_t, kw, g_mat, bias_t, sink_rows,
           x2d, w_out, batch, seq):
    step_rows = CHUNKS_PER_STEP * CHUNK
    ns = seq // step_rows
    row = lambda w: pl.BlockSpec((step_rows, w), lambda b, i: (b * ns + i, 0))
    col = pl.BlockSpec((ATTN_WIDTH, step_rows), lambda b, i: (0, b * ns + i))
    resident = lambda a: pl.BlockSpec(a.shape, lambda b, i: (0,) * a.ndim, pipeline_mode=pl.Buffered(1))
    in_specs = [row(SSM_WIDTH), row(SSM_WIDTH), row(BC_WIDTH), row(BC_WIDTH), row(LANES),
                _const_spec((1, LANES)), _const_spec((1, SSM_WIDTH)), _const_spec((1, SSM_WIDTH)), resident(e_mat),
                col, row(KV_WIDTH), row(KV_WIDTH), col, resident(qw_t), _const_spec((1, KV_WIDTH)),
                resident(g_mat), resident(bias_t), _const_spec(sink_rows.shape), row(D_MODEL), resident(w_out)]
    kv_out = pl.BlockSpec((1, CHUNK, KV_WIDTH), lambda b, i: (b, 0, 0))
    out_specs = [row(D_MODEL), pl.BlockSpec((1, SSM_WIDTH, SSM_STATE), lambda b, i: (b, 0, 0)), kv_out, kv_out]
    out_shape = [jax.ShapeDtypeStruct((batch * seq, D_MODEL), F32),
                 jax.ShapeDtypeStruct((batch, SSM_WIDTH, SSM_STATE), F32),
                 jax.ShapeDtypeStruct((batch, CHUNK, KV_WIDTH), F32),
                 jax.ShapeDtypeStruct((batch, CHUNK, KV_WIDTH), F32)]
    scratch = [pltpu.VMEM((SSM_STATE, SSM_WIDTH), F32),
               pltpu.VMEM((step_rows + CHUNK, KV_WIDTH), BF16), pltpu.VMEM((KV_WIDTH, step_rows + CHUNK), BF16),
               pltpu.VMEM((step_rows, SSM_WIDTH), BF16), pltpu.VMEM((ATTN_WIDTH, step_rows), BF16)]
    return pl.pallas_call(
        _mixer_body, grid=(batch, ns), in_specs=in_specs, out_specs=out_specs, out_shape=out_shape,
        scratch_shapes=scratch, compiler_params=_params("arbitrary", "arbitrary"), name="mixer")(
            gz, xs, b, c, dt, alog, dskip, norm_w, e_mat, q_t, k, v, gz_t, qw_t, kw, g_mat, bias_t, sink_rows,
            x2d, w_out)


def _ssd_sample_vec_body(z_ref, xbc_ref, dt_ref, sconv_ref, cw_ref, cb_ref, dtb_ref, alog_ref, dskip_ref,
                         gh_ref, e_ref, ypart_ref, ea_ref, xw_ref, b_ref, c_ref, cdec_ref, convnew_ref):
    steps = xbc_ref.shape[0]
    tail = CONV_WIDTH - 1
    full = [sconv_ref[j] for j in range(tail)] + [xbc_ref[l] for l in range(steps)]
    for j in range(tail):
        convnew_ref[j] = full[steps + j]
    gh = gh_ref[...]
    e_mat = e_ref[...]
    a_neg = -jnp.exp(alog_ref[...])
    xs, bm, cm, dts, acum = [], [], [], [], []
    run = None
    for l in range(steps):
        conv = cb_ref[...]
        for tap in range(CONV_WIDTH):
            conv = conv + full[l + tap] * cw_ref[tap:tap + 1, :]
        act = _silu(conv)
        xs.append(act[:, :SSM_WIDTH])
        bm.append(act[:, SSM_WIDTH:SSM_WIDTH + BC_WIDTH])
        cm.append(act[:, SSM_WIDTH + BC_WIDTH:])
        d = _softplus(dt_ref[l] + dtb_ref[...])
        dts.append(d)
        run = d * a_neg if run is None else run + d * a_neg
        acum.append(run)
        b_ref[l] = bm[l]
        c_ref[l] = cm[l]
    for l in range(steps):
        y = dskip_ref[...] * xs[l]
        for s in range(l + 1):
            cb_h = _dot_sel(cm[l] * bm[s], gh)
            coef = cb_h * jnp.exp(acum[l] - acum[s]) * dts[s]
            y = y + _dot_sel(coef, e_mat) * xs[s]
        ypart_ref[l] = y
        ea_ref[l] = _dot_sel(jnp.exp(acum[l]), e_mat)
        xw_ref[l] = xs[l] * _dot_sel(dts[l] * jnp.exp(acum[steps - 1] - acum[l]), e_mat)
    cdec_ref[...] = jnp.exp(acum[steps - 1])


def _ssd_sample_vec(z3, xbc3, dt3, sconv3, conv_w, conv_b, dtb, alog, dskip, gh_mat, e_mat):
    steps, nb = z3.shape[0], z3.shape[1]
    f = lambda *s: jax.ShapeDtypeStruct(s, F32)
    out_shape = [f(steps, nb, SSM_WIDTH), f(steps, nb, SSM_WIDTH), f(steps, nb, SSM_WIDTH),
                 f(steps, nb, BC_WIDTH), f(steps, nb, BC_WIDTH), f(nb, LANES), f(CONV_WIDTH - 1, nb, CONV_DIM)]
    return pl.pallas_call(_ssd_sample_vec_body, out_shape=out_shape,
                          compiler_params=pltpu.CompilerParams(vmem_limit_bytes=VMEM_LIMIT),
                          name="ssd_sample_vec")(
        z3, xbc3, dt3, sconv3, conv_w, conv_b, dtb, alog, dskip, gh_mat, e_mat)


def _ssd_sample_state_body(cdec_ref, st_ref, c_ref, b_ref, xw_ref, new_ref, yoff_ref):
    i = pl.program_id(0)
    heads_per_group = SSM_HEADS // SSM_GROUPS
    for j in range(SAMPLE_BATCH_BLOCK):
        st = st_ref[j]
        cb_bf = c_ref[:, j, :].astype(BF16)
        bb_bf = b_ref[:, j, :].astype(BF16)
        xw_bf = xw_ref[:, j, :].astype(BF16)
        y_parts = []
        for g in range(SSM_GROUPS):
            rows = slice(g * GROUP_WIDTH, (g + 1) * GROUP_WIDTH)
            ns = slice(g * SSM_STATE, (g + 1) * SSM_STATE)
            y_parts.append(_dot_nt(cb_bf[:, ns], st[rows].astype(BF16)))
            upd = _dot_tn(xw_bf[:, rows], bb_bf[:, ns])
            for hh in range(heads_per_group):
                h = g * heads_per_group + hh
                r = slice(h * SSM_HEAD_DIM, (h + 1) * SSM_HEAD_DIM)
                dec = cdec_ref[(i * SAMPLE_BATCH_BLOCK + j) * SSM_HEADS + h]
                new_ref[j, r, :] = st[r] * dec + upd[hh * SSM_HEAD_DIM:(hh + 1) * SSM_HEAD_DIM]
        yoff_ref[:, j, :] = jnp.concatenate(y_parts, axis=1)


def _ssd_sample_state(cdec_flat, state, c3, b3, xw3):
    steps, nb = c3.shape[0], c3.shape[1]
    bb = SAMPLE_BATCH_BLOCK
    tok = lambda w: pl.BlockSpec((steps, bb, w), lambda i: (0, i, 0))
    st_spec = pl.BlockSpec((bb, SSM_WIDTH, SSM_STATE), lambda i: (i, 0, 0))
    return pl.pallas_call(
        _ssd_sample_state_body, grid=(nb // bb,),
        in_specs=[pl.BlockSpec(memory_space=pltpu.SMEM), st_spec, tok(BC_WIDTH), tok(BC_WIDTH), tok(SSM_WIDTH)],
        out_specs=[st_spec, tok(SSM_WIDTH)],
        out_shape=[jax.ShapeDtypeStruct(state.shape, F32), jax.ShapeDtypeStruct((steps, nb, SSM_WIDTH), F32)],
        compiler_params=_params("parallel"), name="ssd_sample_state")(cdec_flat, state, c3, b3, xw3)


def _sample_finish_body(ypart_ref, yoff_ref, ea_ref, z_ref, nw_ref, q_ref, k_ref, qw_ref, kw_ref, g_ref, e_ref,
                        yssm_ref, qn_ref, kn_ref):
    y = ypart_ref[...] + yoff_ref[...] * ea_ref[...]
    yssm_ref[...] = _group_rmsnorm(y * _silu(z_ref[...]), nw_ref[...])
    g_mat = g_ref[...]
    e_mat = e_ref[...]
    qn = _head_rmsnorm(q_ref[...], g_mat, e_mat, qw_ref[...]) * ATTN_SCALE
    head = lambda h: qn[:, h * ATTN_HEAD_DIM:(h + 1) * ATTN_HEAD_DIM]
    qn_ref[...] = jnp.concatenate(
        [head(n * Q_PER_KV + g) for g in range(Q_PER_KV) for n in range(ATTN_KV_HEADS)], axis=1)
    kn_ref[...] = _head_rmsnorm(k_ref[...], g_mat[:KV_WIDTH], e_mat[:, :KV_WIDTH], kw_ref[...])


def _sample_finish(ypart, yoff, ea, z, norm_w, q, k, qw, kw, g_mat, e_mat):
    rows = z.shape[0]
    f = lambda w: jax.ShapeDtypeStruct((rows, w), F32)
    return pl.pallas_call(_sample_finish_body, out_shape=[f(SSM_WIDTH), f(ATTN_WIDTH), f(KV_WIDTH)],
                          compiler_params=pltpu.CompilerParams(vmem_limit_bytes=VMEM_LIMIT),
                          name="sample_finish")(ypart, yoff, ea, z, norm_w, q, k, qw, kw, g_mat, e_mat)


def _attn_sample_body(q_ref, kn_ref, vn_ref, z_ref, ckt_ref, cvt_ref, biasc_ref, biasn_ref,
                      y_ref, kot_ref, vot_ref):
    steps = q_ref.shape[0]
    bb = SAMPLE_BATCH_BLOCK
    blk = Q_PER_KV * steps
    rows = ATTN_KV_HEADS * blk
    pad = jnp.zeros((SUBLANES - steps, KV_WIDTH), F32)
    lane_head = _lane_head((blk, KV_WIDTH))
    zero = jnp.zeros((blk, KV_WIDTH), F32)

    s_c, s_n, k8, v8 = [], [], [], []
    for j in range(bb):
        q = q_ref[:, j, :]
        qg = jnp.concatenate([q[:, g * KV_WIDTH:(g + 1) * KV_WIDTH] for g in range(Q_PER_KV)], axis=0)
        qx = jnp.concatenate([jnp.where(lane_head == n, qg, zero) for n in range(ATTN_KV_HEADS)], axis=0)
        qx = qx.astype(BF16)
        k8.append(jnp.concatenate([kn_ref[:, j, :], pad], axis=0))
        v8.append(jnp.concatenate([vn_ref[:, j, :], pad], axis=0))
        s_c.append(_dot(qx, ckt_ref[j].astype(BF16)))
        s_n.append(_dot_nt(qx, k8[j].astype(BF16)))
    s_c = jnp.concatenate(s_c, axis=0) + biasc_ref[...]
    s_n = jnp.concatenate(s_n, axis=0) + biasn_ref[...]
    m = jnp.maximum(jnp.max(s_c, axis=-1, keepdims=True), jnp.max(s_n, axis=-1, keepdims=True))
    p_c = jnp.exp(s_c - m)
    p_n = jnp.exp(s_n - m)
    inv = 1.0 / (jnp.sum(p_c, axis=-1, keepdims=True) + jnp.sum(p_n, axis=-1, keepdims=True))
    p_c = (p_c * inv).astype(BF16)
    p_n = (p_n * inv).astype(BF16)

    lane = lax.broadcasted_iota(jnp.int32, (KV_WIDTH, WINDOW), 1)
    for j in range(bb):
        r = slice(j * rows, (j + 1) * rows)
        o = _dot_nt(p_c[r], cvt_ref[j].astype(BF16)) + _dot(p_n[r], v8[j].astype(BF16))
        og = zero
        for n in range(ATTN_KV_HEADS):
            og = og + jnp.where(lane_head == n, o[n * blk:(n + 1) * blk], zero)
        y = jnp.concatenate(
            [og[g * steps:(g + 1) * steps, n * ATTN_HEAD_DIM:(n + 1) * ATTN_HEAD_DIM]
             for n in range(ATTN_KV_HEADS) for g in range(Q_PER_KV)], axis=1)
        y_ref[:, j, :] = y * _silu(z_ref[:, j, :])

        for new8, old_ref, out_ref in ((k8[j], ckt_ref, kot_ref), (v8[j], cvt_ref, vot_ref)):
            tail_rows = jnp.concatenate([new8[steps:], new8[:steps]], axis=0)
            block = jnp.concatenate([jnp.zeros((WINDOW - SUBLANES, KV_WIDTH), F32), tail_rows], axis=0)
            shifted = pltpu.roll(old_ref[j], WINDOW - steps, axis=1)
            out_ref[j] = jnp.where(lane >= WINDOW - steps, block.T, shifted)


def _attn_sample(q3, kn3, vn3, z3, cache_kt, cache_vt, bias_c, bias_n):
    steps, nb = q3.shape[0], q3.shape[1]
    bb = SAMPLE_BATCH_BLOCK
    tok = lambda w: pl.BlockSpec((steps, bb, w), lambda i: (0, i, 0))
    cache_spec = pl.BlockSpec((bb, KV_WIDTH, WINDOW), lambda i: (i, 0, 0))
    return pl.pallas_call(
        _attn_sample_body, grid=(nb // bb,),
        in_specs=[tok(ATTN_WIDTH), tok(KV_WIDTH), tok(KV_WIDTH), tok(ATTN_WIDTH), cache_spec, cache_spec,
                  _const_spec(bias_c.shape), _const_spec(bias_n.shape)],
        out_specs=[tok(ATTN_WIDTH), cache_spec, cache_spec],
        out_shape=[jax.ShapeDtypeStruct((steps, nb, ATTN_WIDTH), F32),
                   jax.ShapeDtypeStruct(cache_kt.shape, F32), jax.ShapeDtypeStruct(cache_vt.shape, F32)],
        compiler_params=_params("parallel"), name="attn_sample")(
            q3, kn3, vn3, z3, cache_kt, cache_vt, bias_c, bias_n)


def _static_tables(steps):
    lanes = np.arange(ATTN_WIDTH)
    g_mat = np.zeros((ATTN_WIDTH, LANES), np.float32)
    g_mat[lanes, lanes // ATTN_HEAD_DIM] = 1.0
    e_mat = g_mat.T.copy()
    bc = np.arange(BC_WIDTH)
    gh_mat = np.zeros((BC_WIDTH, LANES), np.float32)
    for h in range(SSM_HEADS):
        gh_mat[bc // SSM_STATE == h // (SSM_HEADS // SSM_GROUPS), h] = 1.0
    T = CHUNK
    dist = np.arange(T)[:, None] - (np.arange(2 * T) - T)[None, :]
    first = np.broadcast_to((np.arange(2 * T) >= T)[None, :], dist.shape)
    prompt_buckets = np.stack([_bucket_or_masked(dist, first), _bucket_or_masked(dist)])
    dist_c = (np.arange(steps) + WINDOW)[:, None] - np.arange(WINDOW)[None, :]
    dist_n = np.arange(steps)[:, None] - np.arange(SUBLANES)[None, :]
    real = np.broadcast_to((np.arange(SUBLANES) < steps)[None, :], dist_n.shape)
    return dict(g=g_mat, e=e_mat, gh=gh_mat, prompt_buckets=prompt_buckets,
                cache_buckets=_bucket_or_masked(dist_c)[None], new_buckets=_bucket_or_masked(dist_n, real)[None])


def kernel(x_prompt, x_sample, cache_k, cache_v, state_ssm, state_conv, norm_w, w_in, conv_w, conv_b, dt_bias,
           a_log, d_skip, ssm_norm_w, q_norm_w, k_norm_w, sinks, rel_table, w_out):
    assert w_in.shape[0] == 1, "single-layer kernel"
    batch, seq, _ = x_prompt.shape
    nb, steps, _ = x_sample.shape
    tab = _static_tables(steps)
    g_mat = jnp.asarray(tab["g"], BF16)
    e_mat = jnp.asarray(tab["e"], BF16)
    gh_mat = jnp.asarray(tab["gh"], BF16)

    w_t = jnp.transpose(w_in[0]).astype(BF16)
    wo_all = w_out[0].astype(BF16)

    row = lambda v, width: jnp.pad(v.reshape(1, -1), ((0, 0), (0, width - v.size)))
    nw = row(norm_w[0], D_MODEL)
    cw = conv_w[0]
    cb = row(conv_b[0], CONV_DIM)
    dtb = row(dt_bias[0], LANES)
    alog = row(a_log[0], LANES)
    dskip = jnp.repeat(d_skip[0], SSM_HEAD_DIM).reshape(1, SSM_WIDTH)
    snw = row(ssm_norm_w[0], SSM_WIDTH)
    qw = jnp.tile(q_norm_w[0], ATTN_HEADS).reshape(1, ATTN_WIDTH)
    kw = jnp.tile(k_norm_w[0], ATTN_KV_HEADS).reshape(1, KV_WIDTH)
    sink = sinks[0]
    rel_flat = rel_table.reshape(-1)

    xp = x_prompt.reshape(batch * seq, D_MODEL)
    gz, xs_p, b_p, c_p, dt_p, q_t, k, v, gza_t, tail_p = _inproj_prompt(
        xp, nw, w_t, cw, cb, dtb, batch, seq)
    bias_t = _bias_tables_t(rel_flat * LOG2E, jnp.asarray(tab["prompt_buckets"].transpose(0, 2, 1)))
    qw_t = jnp.broadcast_to((qw * (ATTN_SCALE * LOG2E)).reshape(ATTN_WIDTH, 1),
                            (ATTN_WIDTH, CHUNKS_PER_STEP * CHUNK))
    sink_rows = jnp.repeat(sink.reshape(ATTN_KV_HEADS, Q_PER_KV) * LOG2E, CHUNK, axis=1)
    sink_rows = sink_rows.reshape(ATTN_KV_HEADS, 1, -1)
    y_p, st_p, k_p, v_p = _mixer(gz, xs_p, b_p, c_p, dt_p, alog, dskip, snw, e_mat, q_t, k, v, gza_t, qw_t, kw,
                                 g_mat[:KV_WIDTH], bias_t, sink_rows, xp, wo_all, batch, seq)
    y_p = y_p.reshape(batch, seq, D_MODEL)
    conv_p = tail_p[:, SUBLANES - (CONV_WIDTH - 1):, :]

    xs = jnp.swapaxes(x_sample, 0, 1).reshape(steps * nb, D_MODEL)
    z, xbc, dt, q, k, v, za = _inproj(xs, nw, w_t)
    t3 = lambda a: a.reshape(steps, nb, a.shape[-1])
    sconv3 = jnp.swapaxes(state_conv[0], 0, 1)
    ypart, ea, xw, b3, c3, cdec, conv_s3 = _ssd_sample_vec(
        t3(z), t3(xbc), t3(dt), sconv3, cw, cb, dtb, alog, dskip, gh_mat, e_mat)
    state_in = state_ssm[0].reshape(nb, SSM_WIDTH, SSM_STATE)
    st_s, yoff = _ssd_sample_state(cdec[:, :SSM_HEADS].reshape(-1), state_in, c3, b3, xw)
    f2 = lambda a: a.reshape(steps * nb, a.shape[-1])
    y_ssm, qn, kn = _sample_finish(f2(ypart), f2(yoff), f2(ea), z, snw, q, k, qw, kw, g_mat, e_mat)
    bias_c = _bias_tables(rel_flat, jnp.asarray(tab["cache_buckets"])).reshape(ATTN_HEADS * steps, WINDOW)
    bias_n = _bias_tables(rel_flat, jnp.asarray(tab["new_buckets"])).reshape(ATTN_HEADS * steps, SUBLANES)
    bias_n = bias_n.at[:, steps].set(jnp.repeat(sink, steps))
    bias_c = jnp.tile(bias_c, (SAMPLE_BATCH_BLOCK, 1))
    bias_n = jnp.tile(bias_n, (SAMPLE_BATCH_BLOCK, 1))
    to_t = lambda a: jnp.transpose(a[0], (0, 2, 3, 1)).reshape(nb, KV_WIDTH, WINDOW)
    from_t = lambda a: jnp.transpose(a.reshape(nb, ATTN_KV_HEADS, ATTN_HEAD_DIM, WINDOW), (0, 3, 1, 2))[None]
    y_attn3, k_st, v_st = _attn_sample(t3(qn), t3(kn), t3(v), t3(za), to_t(cache_k), to_t(cache_v),
                                       bias_c, bias_n)
    k_s, v_s = from_t(k_st), from_t(v_st)
    y_s = _outproj(y_ssm, f2(y_attn3), xs, wo_all)
    y_s = jnp.swapaxes(y_s.reshape(steps, nb, D_MODEL), 0, 1)

    kv5 = lambda a: a.reshape(1, a.shape[0], WINDOW, ATTN_KV_HEADS, ATTN_HEAD_DIM)
    st5 = lambda a: a.reshape(1, a.shape[0], SSM_HEADS, SSM_HEAD_DIM, SSM_STATE)
    return (y_p, y_s, kv5(k_p), kv5(v_p), st5(st_p), conv_p[None],
            k_s, v_s, st5(st_s), jnp.swapaxes(conv_s3, 0, 1)[None])
```

```python
import functools
import math

import numpy as np
import jax
import jax.numpy as jnp
from jax import lax
from jax.experimental import pallas as pl
from jax.experimental.pallas import tpu as pltpu

F32 = jnp.float32
BF16 = jnp.bfloat16

D_MODEL = 1024
SSM_HEADS = 16
SSM_HEAD_DIM = 64
SSM_WIDTH = SSM_HEADS * SSM_HEAD_DIM
SSM_GROUPS = 2
SSM_STATE = 128
GROUP_WIDTH = SSM_WIDTH // SSM_GROUPS
BC_WIDTH = SSM_GROUPS * SSM_STATE
CONV_WIDTH = 4
CONV_DIM = SSM_WIDTH + 2 * BC_WIDTH
CHUNK = 128
ATTN_HEADS = 16
ATTN_KV_HEADS = 4
Q_PER_KV = ATTN_HEADS // ATTN_KV_HEADS
ATTN_HEAD_DIM = 64
ATTN_WIDTH = ATTN_HEADS * ATTN_HEAD_DIM
KV_WIDTH = ATTN_KV_HEADS * ATTN_HEAD_DIM
WINDOW = 128
ATTN_SCALE = ATTN_HEAD_DIM ** -0.5
REL_BUCKETS = 32
REL_MAX_DIST = 128
EPS = 1e-6
LOG2E = 1.0 / math.log(2.0)
NEG = -1e30

LANES = 128
SUBLANES = 8
MXU_WIDTH = 256
VMEM_LIMIT = 56 * 1024 * 1024
def _in_proj_rows():
    widths = (("z", SSM_WIDTH), ("xbc", CONV_DIM), ("dt", SSM_HEADS), ("q", ATTN_WIDTH), ("k", KV_WIDTH),
              ("v", KV_WIDTH), ("za", ATTN_WIDTH))
    rows, start = {}, 0
    for name, width in widths:
        rows[name] = slice(start, start + width)
        start += width
    return rows


IN_ROWS = _in_proj_rows()
DT_ROWS = slice(IN_ROWS["dt"].start, IN_ROWS["dt"].start + LANES)

PROJ_ROWS = 512
CHUNKS_PER_STEP = 4
SAMPLE_BATCH_BLOCK = 8


def _dot(a, b):
    return jnp.dot(a, b, preferred_element_type=F32)


def _dot_nt(a, b):
    return lax.dot_general(a, b, (((1,), (1,)), ((), ())), preferred_element_type=F32)


def _dot_tn(a, b):
    return lax.dot_general(a, b, (((0,), (0,)), ((), ())), preferred_element_type=F32)


def _split2(v):
    hi = v.astype(BF16)
    lo = (v - hi.astype(F32)).astype(BF16)
    return hi, lo


def _dot_sel(v, m):
    hi, lo = _split2(v)
    if 2 * v.shape[1] <= MXU_WIDTH:
        return _dot(jnp.concatenate([hi, lo], axis=1), jnp.concatenate([m, m], axis=0))
    return _dot(hi, m) + _dot(lo, m)


def _dot_sel3(m, v):
    hi = v.astype(BF16)
    r1 = v - hi.astype(F32)
    mid = r1.astype(BF16)
    lo = (r1 - mid.astype(F32)).astype(BF16)
    return _dot(m, hi) + _dot(m, mid) + _dot(m, lo)


def _silu(x):
    return x / (1.0 + jnp.exp(-x))


def _softplus(x):
    return jnp.maximum(x, 0.0) + jnp.log1p(jnp.exp(-jnp.abs(x)))


def _params(*sem):
    return pltpu.CompilerParams(dimension_semantics=sem, vmem_limit_bytes=VMEM_LIMIT)


def _const_spec(shape):
    nd = len(shape)
    return pl.BlockSpec(shape, lambda *_: (0,) * nd)


def _normed_input(x_ref, nw_ref):
    x = x_ref[...]
    ms = jnp.mean(x * x, axis=-1, keepdims=True)
    return (x * lax.rsqrt(ms + EPS) * nw_ref[...]).astype(BF16)


def _dt_projection(h, wt_ref):
    raw = _dot_nt(h, wt_ref[DT_ROWS, :])
    return jnp.where(lax.broadcasted_iota(jnp.int32, raw.shape, 1) < SSM_HEADS, raw, 0.0)


def _inproj_body(x_ref, nw_ref, wt_ref, *out_refs):
    h = _normed_input(x_ref, nw_ref)
    for name, o_ref in zip(IN_ROWS, out_refs):
        o_ref[...] = _dt_projection(h, wt_ref) if name == "dt" else _dot_nt(h, wt_ref[IN_ROWS[name], :])


def _inproj(x2d, norm_w, w_t):
    rows = x2d.shape[0]
    tm = min(PROJ_ROWS, rows)
    widths = [LANES if name == "dt" else r.stop - r.start for name, r in IN_ROWS.items()]
    in_specs = [pl.BlockSpec((tm, D_MODEL), lambda i: (i, 0)), _const_spec((1, D_MODEL)),
                pl.BlockSpec(w_t.shape, lambda i: (0, 0), pipeline_mode=pl.Buffered(1))]
    out_specs = [pl.BlockSpec((tm, w), lambda i: (i, 0)) for w in widths]
    out_shape = [jax.ShapeDtypeStruct((rows, w), F32) for w in widths]
    return pl.pallas_call(
        _inproj_body, grid=(rows // tm,), in_specs=in_specs, out_specs=out_specs, out_shape=out_shape,
        compiler_params=_params("parallel"), name="inproj")(x2d, norm_w, w_t)


def _shift_rows(u, prev_tail, k):
    rows, width = u.shape
    tiles = jnp.concatenate([prev_tail, u], axis=0).reshape(rows // SUBLANES + 1, SUBLANES, width)
    rot = jnp.concatenate([tiles[:, SUBLANES - k:], tiles[:, :SUBLANES - k]], axis=1)
    first = lax.broadcasted_iota(jnp.int32, (1, SUBLANES, width), 1) < k
    return jnp.where(first, rot[:-1], rot[1:]).reshape(rows, width)


def _inproj_prompt_body(steps_per_seq, n_state_blocks, x_ref, nw_ref, wt_ref, cw_ref, cb_ref, dtb_ref,
                        cdec_ref, st_ref, sc_ref, sb_ref, sxw_ref,
                        gz_ref, xs_ref, b_ref, c_ref, dt_ref, qt_ref, k_ref, v_ref, gzat_ref, tail_ref,
                        newst_ref, yoff_ref, tail_sc):
    step = pl.program_id(0)

    @pl.when(step < n_state_blocks)
    def _():
        _ssd_sample_state_block(step, cdec_ref, st_ref, sc_ref, sb_ref, sxw_ref, newst_ref, yoff_ref)

    @pl.when(step % steps_per_seq == 0)
    def _():
        tail_sc[...] = jnp.zeros_like(tail_sc)

    h = _normed_input(x_ref, nw_ref)
    rows = h.shape[0]
    w_tile = lambda name, j: wt_ref[IN_ROWS[name].start + j * MXU_WIDTH:IN_ROWS[name].start + (j + 1) * MXU_WIDTH, :]
    n_side = SSM_WIDTH // MXU_WIDTH
    for j in range(CONV_DIM // MXU_WIDTH):
        cols = slice(j * MXU_WIDTH, (j + 1) * MXU_WIDTH)
        partner = w_tile("z", j) if j < n_side else wt_ref[IN_ROWS["k" if j == n_side else "v"], :]
        both = _dot_nt(h, jnp.concatenate([w_tile("xbc", j), partner], axis=0))
        u = both[:, :MXU_WIDTH]
        if j < n_side:
            gz_ref[:, cols] = _silu(both[:, MXU_WIDTH:])
        elif j == n_side:
            k_ref[...] = both[:, MXU_WIDTH:]
        else:
            v_ref[...] = both[:, MXU_WIDTH:]
        prev_tail = tail_sc[:, cols]
        conv = cb_ref[:, cols] + u * cw_ref[CONV_WIDTH - 1:CONV_WIDTH, cols]
        for k in range(1, CONV_WIDTH):
            tap = CONV_WIDTH - 1 - k
            conv = conv + _shift_rows(u, prev_tail, k) * cw_ref[tap:tap + 1, cols]
        new_tail = u[rows - SUBLANES:, :]
        tail_sc[:, cols] = new_tail
        tail_ref[0, :, cols] = new_tail
        act = _silu(conv)
        if j < SSM_WIDTH // MXU_WIDTH:
            xs_ref[:, cols] = act
        elif j == SSM_WIDTH // MXU_WIDTH:
            b_ref[...] = act.astype(BF16)
        else:
            c_ref[...] = act.astype(BF16)

    for j in range(ATTN_WIDTH // MXU_WIDTH):
        feats = slice(j * MXU_WIDTH, (j + 1) * MXU_WIDTH)
        gzat_ref[feats, :] = _silu(_dot_nt(w_tile("za", j), h))
    dt_ref[...] = _softplus(_dt_projection(h, wt_ref) + dtb_ref[...])
    qt_ref[...] = _dot_nt(wt_ref[IN_ROWS["q"], :], h)


def _inproj_prompt(x2d, norm_w, w_t, conv_w, conv_b, dtb, batch, seq, cdec_flat, state, c3, b3, xw3):
    rows = x2d.shape[0]
    tm = PROJ_ROWS
    steps_per_seq = seq // tm
    steps, nb = c3.shape[0], c3.shape[1]
    bb = SAMPLE_BATCH_BLOCK
    n_state_blocks = nb // bb
    assert n_state_blocks <= rows // tm, "state blocks ride on the in-proj grid steps"
    last = n_state_blocks - 1
    resident = lambda a: pl.BlockSpec(a.shape, lambda i: (0, 0), pipeline_mode=pl.Buffered(1))
    rowblk = lambda w: pl.BlockSpec((tm, w), lambda i: (i, 0))
    colblk = pl.BlockSpec((ATTN_WIDTH, tm), lambda i: (0, i))
    tok = lambda w: pl.BlockSpec((steps, bb, w), lambda i: (0, jnp.minimum(i, last), 0))
    st_spec = pl.BlockSpec((bb, SSM_WIDTH, SSM_STATE), lambda i: (jnp.minimum(i, last), 0, 0))
    in_specs = ([rowblk(D_MODEL), _const_spec((1, D_MODEL)), resident(w_t)]
                + [_const_spec(conv_w.shape), _const_spec(conv_b.shape), _const_spec(dtb.shape)]
                + [pl.BlockSpec(memory_space=pltpu.SMEM), st_spec, tok(BC_WIDTH), tok(BC_WIDTH), tok(SSM_WIDTH)])
    out_specs = [rowblk(SSM_WIDTH), rowblk(SSM_WIDTH), rowblk(BC_WIDTH), rowblk(BC_WIDTH), rowblk(LANES),
                 colblk, rowblk(KV_WIDTH), rowblk(KV_WIDTH), colblk,
                 pl.BlockSpec((1, SUBLANES, CONV_DIM), lambda i: (i // steps_per_seq, 0, 0)),
                 st_spec, tok(SSM_WIDTH)]
    f = lambda r, c, dt=F32: jax.ShapeDtypeStruct((r, c), dt)
    out_shape = [f(rows, SSM_WIDTH), f(rows, SSM_WIDTH), f(rows, BC_WIDTH, BF16), f(rows, BC_WIDTH, BF16),
                 f(rows, LANES), f(ATTN_WIDTH, rows), f(rows, KV_WIDTH), f(rows, KV_WIDTH), f(ATTN_WIDTH, rows),
                 jax.ShapeDtypeStruct((batch, SUBLANES, CONV_DIM), F32),
                 jax.ShapeDtypeStruct(state.shape, F32), jax.ShapeDtypeStruct((steps, nb, SSM_WIDTH), F32)]
    return pl.pallas_call(
        functools.partial(_inproj_prompt_body, steps_per_seq, n_state_blocks), grid=(rows // tm,),
        in_specs=in_specs, out_specs=out_specs, out_shape=out_shape,
        scratch_shapes=[pltpu.VMEM((SUBLANES, CONV_DIM), F32)],
        compiler_params=_params("arbitrary"), name="inproj_prompt")(
            x2d, norm_w, w_t, conv_w, conv_b, dtb, cdec_flat, state, c3, b3, xw3)


def _outproj_body(ys_ref, ya_ref, x_ref, w_ref, o_ref):
    o_ref[...] = (x_ref[...] + _dot(ys_ref[...].astype(BF16), w_ref[:SSM_WIDTH, :])
                  + _dot(ya_ref[...].astype(BF16), w_ref[SSM_WIDTH:, :]))


def _outproj(y_ssm, y_attn, x2d, w_out):
    rows = x2d.shape[0]
    tm = min(PROJ_ROWS, rows)
    row_spec = pl.BlockSpec((tm, D_MODEL), lambda i: (i, 0))
    w_spec = pl.BlockSpec(w_out.shape, lambda i: (0, 0), pipeline_mode=pl.Buffered(1))
    return pl.pallas_call(
        _outproj_body, grid=(rows // tm,), in_specs=[row_spec, row_spec, row_spec, w_spec],
        out_specs=row_spec, out_shape=jax.ShapeDtypeStruct((rows, D_MODEL), F32),
        compiler_params=_params("parallel"), name="outproj")(y_ssm, y_attn, x2d, w_out)


def _group_rmsnorm(gy, norm_w):
    parts = []
    for g in range(SSM_GROUPS):
        blk = gy[:, g * GROUP_WIDTH:(g + 1) * GROUP_WIDTH]
        ms = jnp.mean(blk * blk, axis=-1, keepdims=True)
        parts.append(blk * lax.rsqrt(ms + EPS))
    return jnp.concatenate(parts, axis=1) * norm_w


def _ssd_chunk(gz, xs, b_bf, c_bf, dt, a_neg, dskip, norm_w, e_mat, state):
    xs_bf = xs.astype(BF16)

    a = dt * a_neg
    li = lax.broadcasted_iota(jnp.int32, (CHUNK, CHUNK), 0)
    si = lax.broadcasted_iota(jnp.int32, (CHUNK, CHUNK), 1)
    causal = li >= si
    a_cum = _dot_sel3(jnp.where(causal, 1.0, 0.0).astype(BF16), a)
    a2 = a_cum * LOG2E
    row_term = a2.T - jnp.log2(dt.T)
    ea_full = _dot_sel(jnp.exp(a_cum), e_mat)
    w_full = _dot((dt * jnp.exp(a_cum[CHUNK - 1:CHUNK, :] - a_cum)).astype(BF16), e_mat)

    cb = [_dot_nt(c_bf[:, g * SSM_STATE:(g + 1) * SSM_STATE], b_bf[:, g * SSM_STATE:(g + 1) * SSM_STATE])
          for g in range(SSM_GROUPS)]
    half = lax.broadcasted_iota(jnp.int32, (CHUNK, LANES), 1) < SSM_HEAD_DIM
    heads_per_group = SSM_HEADS // SSM_GROUPS
    y_parts = []
    for pair in range(SSM_HEADS // 2):
        blocks = []
        for h in (2 * pair, 2 * pair + 1):
            seg = a2[:, h:h + 1] - row_term[h:h + 1, :]
            decay_dt = jnp.exp2(jnp.where(causal, seg, -jnp.inf))
            blocks.append((cb[h // heads_per_group] * decay_dt).astype(BF16))
        lhs = jnp.concatenate(blocks, axis=1)
        xp = xs_bf[:, pair * LANES:(pair + 1) * LANES]
        zero = jnp.zeros_like(xp)
        rhs = jnp.concatenate([jnp.where(half, xp, zero), jnp.where(half, zero, xp)], axis=0)
        y_parts.append(_dot(lhs, rhs))
    y_diag = jnp.concatenate(y_parts, axis=1)

    state_bf = state.astype(BF16)
    xw_bf = (xs * w_full).astype(BF16)
    y_off, upd = [], []
    for g in range(SSM_GROUPS):
        cols = slice(g * GROUP_WIDTH, (g + 1) * GROUP_WIDTH)
        ns = slice(g * SSM_STATE, (g + 1) * SSM_STATE)
        y_off.append(_dot(c_bf[:, ns], state_bf[:, cols]))
        upd.append(_dot_tn(b_bf[:, ns], xw_bf[:, cols]))
    y = y_diag + jnp.concatenate(y_off, axis=1) * ea_full + dskip * xs
    new_state = state * ea_full[CHUNK - 1:CHUNK, :] + jnp.concatenate(upd, axis=1)
    return _group_rmsnorm(y * gz, norm_w), new_state


def _rel_bucket_np(dist):
    max_exact = REL_BUCKETS // 2
    d_f = np.maximum(dist, 1).astype(np.float32)
    large = max_exact + (np.log(d_f / np.float32(max_exact)) / np.float32(math.log(REL_MAX_DIST / max_exact))
                         * np.float32(REL_BUCKETS - max_exact)).astype(np.int32)
    return np.where(dist < max_exact, dist, np.minimum(large, REL_BUCKETS - 1)).astype(np.int32)


def _bucket_or_masked(dist, extra_mask=None):
    ok = (dist >= 0) & (dist <= WINDOW)
    if extra_mask is not None:
        ok = ok & extra_mask
    return np.where(ok, _rel_bucket_np(np.clip(dist, 0, WINDOW)), -1).astype(np.int32)


def _bias_body(rel_ref, bucket_ref, o_ref):
    bucket = bucket_ref[0]

    def per_head(h, carry):
        acc = jnp.full(bucket.shape, NEG, F32)
        for bkt in range(REL_BUCKETS):
            acc = jnp.where(bucket == bkt, rel_ref[bkt * ATTN_HEADS + h], acc)
        o_ref[0, h] = acc
        return carry

    lax.fori_loop(0, ATTN_HEADS, per_head, 0)


def _bias_tables(rel_flat, buckets):
    nv, lq, lk = buckets.shape
    return pl.pallas_call(
        _bias_body, grid=(nv,),
        in_specs=[pl.BlockSpec(memory_space=pltpu.SMEM), pl.BlockSpec((1, lq, lk), lambda v: (v, 0, 0))],
        out_specs=pl.BlockSpec((1, ATTN_HEADS, lq, lk), lambda v: (v, 0, 0, 0)),
        out_shape=jax.ShapeDtypeStruct((nv, ATTN_HEADS, lq, lk), F32),
        compiler_params=_params("arbitrary"), name="rel_bias")(rel_flat, buckets)


def _bias_t_body(rel_ref, bucket_ref, o_ref):
    variants = [bucket_ref[v] for v in range(bucket_ref.shape[0])]
    union = functools.reduce(jnp.maximum, variants)
    lq = union.shape[1]

    def per_kv_head(n, carry):
        for g in range(Q_PER_KV):
            acc = jnp.full(union.shape, NEG, F32)
            for bkt in range(REL_BUCKETS):
                acc = jnp.where(union == bkt, rel_ref[bkt * ATTN_HEADS + n * Q_PER_KV + g], acc)
            for v, bucket in enumerate(variants):
                o_ref[v, n, :, g * lq:(g + 1) * lq] = jnp.where(bucket >= 0, acc, NEG)
        return carry

    lax.fori_loop(0, ATTN_KV_HEADS, per_kv_head, 0)


def _bias_tables_t(rel_flat, buckets_t):
    nv, lk, lq = buckets_t.shape
    out_dims = (nv, ATTN_KV_HEADS, lk, Q_PER_KV * lq)
    return pl.pallas_call(
        _bias_t_body,
        in_specs=[pl.BlockSpec(memory_space=pltpu.SMEM), pl.BlockSpec(memory_space=pltpu.VMEM)],
        out_specs=pl.BlockSpec(memory_space=pltpu.VMEM),
        out_shape=jax.ShapeDtypeStruct(out_dims, F32),
        compiler_params=pltpu.CompilerParams(vmem_limit_bytes=VMEM_LIMIT), name="rel_bias_t")(rel_flat, buckets_t)


def _head_rmsnorm(x, g_mat, e_mat, w):
    ms = _dot_sel(x * x, g_mat) * (1.0 / ATTN_HEAD_DIM)
    return x * _dot_sel(lax.rsqrt(ms + EPS), e_mat) * w


def _lane_head(shape):
    return lax.broadcasted_iota(jnp.int32, shape, 1) // ATTN_HEAD_DIM


def _sink_column(sink_ref, n, rows_per_head):
    return jnp.concatenate(
        [jnp.full((rows_per_head, 1), sink_ref[n * Q_PER_KV + g], F32) for g in range(Q_PER_KV)], axis=0)


def _softmax_with_sink(s, sink):
    m = jnp.maximum(jnp.max(s, axis=-1, keepdims=True), sink)
    p = jnp.exp(s - m)
    denom = jnp.sum(p, axis=-1, keepdims=True) + jnp.exp(sink - m)
    return p / denom


def _attn_block(q_blk, kcat, vcat_t, bias_at, sink_ref):
    T = CHUNK
    lane_head = _lane_head((2 * T, KV_WIDTH))
    zero = jnp.zeros((2 * T, KV_WIDTH), BF16)
    head = lambda h: q_blk[h * ATTN_HEAD_DIM:(h + 1) * ATTN_HEAD_DIM]
    q_cols = jnp.concatenate(
        [jnp.concatenate([head(n * Q_PER_KV + g) for n in range(ATTN_KV_HEADS)], axis=0)
         for g in range(Q_PER_KV)], axis=1)
    row_head = lax.broadcasted_iota(jnp.int32, (KV_WIDTH, 2 * T), 0) // ATTN_HEAD_DIM
    probs, vals, inv = [], [], {}
    for n in range(ATTN_KV_HEADS):
        s = _dot(jnp.where(lane_head == n, kcat, zero), q_cols)
        sink = sink_ref[n]
        cols = []
        for g in range(Q_PER_KV):
            c = slice(g * T, (g + 1) * T)
            sg = s[:, c] + bias_at(n, c)
            m = jnp.maximum(jnp.max(sg, axis=0, keepdims=True), sink[:, c])
            p = jnp.exp2(sg - m)
            inv[n, g] = 1.0 / (jnp.sum(p, axis=0, keepdims=True) + jnp.exp2(sink[:, c] - m))
            cols.append(p.astype(BF16))
        probs.append(jnp.concatenate(cols, axis=1))
        vals.append(jnp.where(row_head == n, vcat_t, zero.T))
    o_t = _dot(jnp.concatenate(vals, axis=1), jnp.concatenate(probs, axis=0))
    return jnp.concatenate(
        [o_t[n * ATTN_HEAD_DIM:(n + 1) * ATTN_HEAD_DIM, g * T:(g + 1) * T] * inv[n, g]
         for n in range(ATTN_KV_HEADS) for g in range(Q_PER_KV)], axis=0)


def _mixer_body(gz_ref, xs_ref, b_ref, c_ref, dt_ref, alog_ref, dskip_ref, nw_ref, e_ref,
                qt_ref, k_ref, v_ref, gzt_ref, qwt_ref, kw_ref, g_ref, bias_ref, sink_ref, x_ref, wo_ref,
                y_ref, st_ref, kn_ref, vn_ref,
                state_sc, kcat_sc, vcat_t_sc, yssm_sc, yattn_t_sc):
    T = CHUNK
    step = pl.program_id(1)
    cols_step = CHUNKS_PER_STEP * T

    @pl.when(step == 0)
    def _():
        state_sc[...] = jnp.zeros_like(state_sc)
        kcat_sc[0:T, :] = jnp.zeros((T, KV_WIDTH), BF16)
        vcat_t_sc[:, 0:T] = jnp.zeros((KV_WIDTH, T), BF16)

    q3 = qt_ref[...].reshape(ATTN_HEADS, ATTN_HEAD_DIM, cols_step)
    ms = jnp.mean(q3 * q3, axis=1, keepdims=True)
    qn = ((q3 * lax.rsqrt(ms + EPS)).reshape(ATTN_WIDTH, cols_step) * qwt_ref[...]).astype(BF16)
    e_mat = e_ref[...]
    kn = _head_rmsnorm(k_ref[...], g_ref[...], e_mat[:, :KV_WIDTH], kw_ref[...])
    v = v_ref[...]
    kn_ref[0] = kn[cols_step - T:]
    vn_ref[0] = v[cols_step - T:]
    kcat_sc[T:, :] = kn.astype(BF16)
    vcat_t_sc[:, T:] = v.T.astype(BF16)

    a_neg = -jnp.exp(alog_ref[...])
    state = state_sc[...]
    first_variant = jnp.minimum(step, 1)
    for j in range(CHUNKS_PER_STEP):
        r = slice(j * T, (j + 1) * T)
        y, state = _ssd_chunk(gz_ref[r, :], xs_ref[r, :], b_ref[r, :], c_ref[r, :], dt_ref[r, :], a_neg,
                              dskip_ref[...], nw_ref[...], e_mat, state)
        yssm_sc[r, :] = y.astype(BF16)
        variant = first_variant if j == 0 else 1
        y_t = _attn_block(qn[:, r], kcat_sc[j * T:(j + 2) * T, :], vcat_t_sc[:, j * T:(j + 2) * T],
                          lambda n, c, variant=variant: bias_ref[variant, n, :, c], sink_ref)
        yattn_t_sc[:, r] = (y_t * gzt_ref[:, r]).astype(BF16)
    state_sc[...] = state
    kcat_sc[0:T, :] = kcat_sc[cols_step:, :]
    vcat_t_sc[:, 0:T] = vcat_t_sc[:, cols_step:]

    y_ref[...] = (x_ref[...] + _dot(yssm_sc[...], wo_ref[:SSM_WIDTH, :])
                  + _dot_tn(yattn_t_sc[...], wo_ref[SSM_WIDTH:, :]))

    @pl.when(step == pl.num_programs(1) - 1)
    def _():
        st_ref[0] = state.T


def _mixer(gz, xs, b, c, dt, alog, dskip, norm_w, e_mat, q_t, k, v, gz_t, qw_t, kw, g_mat, bias_t, sink_rows,
           x2d, w_out, batch, seq):
    step_rows = CHUNKS_PER_STEP * CHUNK
    ns = seq // step_rows
    row = lambda w: pl.BlockSpec((step_rows, w), lambda b, i: (b * ns + i, 0))
    col = pl.BlockSpec((ATTN_WIDTH, step_rows), lambda b, i: (0, b * ns + i))
    resident = lambda a: pl.BlockSpec(a.shape, lambda b, i: (0,) * a.ndim, pipeline_mode=pl.Buffered(1))
    in_specs = [row(SSM_WIDTH), row(SSM_WIDTH), row(BC_WIDTH), row(BC_WIDTH), row(LANES),
                _const_spec((1, LANES)), _const_spec((1, SSM_WIDTH)), _const_spec((1, SSM_WIDTH)), resident(e_mat),
                col, row(KV_WIDTH), row(KV_WIDTH), col, resident(qw_t), _const_spec((1, KV_WIDTH)),
                resident(g_mat), resident(bias_t), _const_spec(sink_rows.shape), row(D_MODEL), resident(w_out)]
    kv_out = pl.BlockSpec((1, CHUNK, KV_WIDTH), lambda b, i: (b, 0, 0))
    out_specs = [row(D_MODEL), pl.BlockSpec((1, SSM_WIDTH, SSM_STATE), lambda b, i: (b, 0, 0)), kv_out, kv_out]
    out_shape = [jax.ShapeDtypeStruct((batch * seq, D_MODEL), F32),
                 jax.ShapeDtypeStruct((batch, SSM_WIDTH, SSM_STATE), F32),
                 jax.ShapeDtypeStruct((batch, CHUNK, KV_WIDTH), F32),
                 jax.ShapeDtypeStruct((batch, CHUNK, KV_WIDTH), F32)]
    scratch = [pltpu.VMEM((SSM_STATE, SSM_WIDTH), F32),
               pltpu.VMEM((step_rows + CHUNK, KV_WIDTH), BF16), pltpu.VMEM((KV_WIDTH, step_rows + CHUNK), BF16),
               pltpu.VMEM((step_rows, SSM_WIDTH), BF16), pltpu.VMEM((ATTN_WIDTH, step_rows), BF16)]
    return pl.pallas_call(
        _mixer_body, grid=(batch, ns), in_specs=in_specs, out_specs=out_specs, out_shape=out_shape,
        scratch_shapes=scratch, compiler_params=_params("arbitrary", "arbitrary"), name="mixer")(
            gz, xs, b, c, dt, alog, dskip, norm_w, e_mat, q_t, k, v, gz_t, qw_t, kw, g_mat, bias_t, sink_rows,
            x2d, w_out)


def _ssd_sample_vec_body(z_ref, xbc_ref, dt_ref, sconv_ref, cw_ref, cb_ref, dtb_ref, alog_ref, dskip_ref,
                         gh_ref, e_ref, ypart_ref, ea_ref, xw_ref, b_ref, c_ref, cdec_ref, convnew_ref):
    steps = xbc_ref.shape[0]
    tail = CONV_WIDTH - 1
    full = [sconv_ref[j] for j in range(tail)] + [xbc_ref[l] for l in range(steps)]
    for j in range(tail):
        convnew_ref[j] = full[steps + j]
    gh = gh_ref[...]
    e_mat = e_ref[...]
    a_neg = -jnp.exp(alog_ref[...])
    xs, bm, cm, dts, acum = [], [], [], [], []
    run = None
    for l in range(steps):
        conv = cb_ref[...]
        for tap in range(CONV_WIDTH):
            conv = conv + full[l + tap] * cw_ref[tap:tap + 1, :]
        act = _silu(conv)
        xs.append(act[:, :SSM_WIDTH])
        bm.append(act[:, SSM_WIDTH:SSM_WIDTH + BC_WIDTH])
        cm.append(act[:, SSM_WIDTH + BC_WIDTH:])
        d = _softplus(dt_ref[l] + dtb_ref[...])
        dts.append(d)
        run = d * a_neg if run is None else run + d * a_neg
        acum.append(run)
        b_ref[l] = bm[l]
        c_ref[l] = cm[l]
    for l in range(steps):
        y = dskip_ref[...] * xs[l]
        for s in range(l + 1):
            cb_h = _dot_sel(cm[l] * bm[s], gh)
            coef = cb_h * jnp.exp(acum[l] - acum[s]) * dts[s]
            y = y + _dot_sel(coef, e_mat) * xs[s]
        ypart_ref[l] = y
        ea_ref[l] = _dot_sel(jnp.exp(acum[l]), e_mat)
        xw_ref[l] = xs[l] * _dot_sel(dts[l] * jnp.exp(acum[steps - 1] - acum[l]), e_mat)
    cdec_ref[...] = jnp.exp(acum[steps - 1])


def _ssd_sample_vec(z3, xbc3, dt3, sconv3, conv_w, conv_b, dtb, alog, dskip, gh_mat, e_mat):
    steps, nb = z3.shape[0], z3.shape[1]
    f = lambda *s: jax.ShapeDtypeStruct(s, F32)
    out_shape = [f(steps, nb, SSM_WIDTH), f(steps, nb, SSM_WIDTH), f(steps, nb, SSM_WIDTH),
                 f(steps, nb, BC_WIDTH), f(steps, nb, BC_WIDTH), f(nb, LANES), f(CONV_WIDTH - 1, nb, CONV_DIM)]
    return pl.pallas_call(_ssd_sample_vec_body, out_shape=out_shape,
                          compiler_params=pltpu.CompilerParams(vmem_limit_bytes=VMEM_LIMIT),
                          name="ssd_sample_vec")(
        z3, xbc3, dt3, sconv3, conv_w, conv_b, dtb, alog, dskip, gh_mat, e_mat)


def _ssd_sample_state_block(i, cdec_ref, st_ref, c_ref, b_ref, xw_ref, new_ref, yoff_ref):
    heads_per_group = SSM_HEADS // SSM_GROUPS
    for j in range(SAMPLE_BATCH_BLOCK):
        st = st_ref[j]
        cb_bf = c_ref[:, j, :].astype(BF16)
        bb_bf = b_ref[:, j, :].astype(BF16)
        xw_bf = xw_ref[:, j, :].astype(BF16)
        y_parts = []
        for g in range(SSM_GROUPS):
            rows = slice(g * GROUP_WIDTH, (g + 1) * GROUP_WIDTH)
            ns = slice(g * SSM_STATE, (g + 1) * SSM_STATE)
            y_parts.append(_dot_nt(cb_bf[:, ns], st[rows].astype(BF16)))
            upd = _dot_tn(xw_bf[:, rows], bb_bf[:, ns])
            for hh in range(heads_per_group):
                h = g * heads_per_group + hh
                r = slice(h * SSM_HEAD_DIM, (h + 1) * SSM_HEAD_DIM)
                dec = cdec_ref[(i * SAMPLE_BATCH_BLOCK + j) * SSM_HEADS + h]
                new_ref[j, r, :] = st[r] * dec + upd[hh * SSM_HEAD_DIM:(hh + 1) * SSM_HEAD_DIM]
        yoff_ref[:, j, :] = jnp.concatenate(y_parts, axis=1)


def _sample_finish_body(ypart_ref, yoff_ref, ea_ref, z_ref, nw_ref, q_ref, k_ref, qw_ref, kw_ref, g_ref, e_ref,
                        yssm_ref, qn_ref, kn_ref):
    y = ypart_ref[...] + yoff_ref[...] * ea_ref[...]
    yssm_ref[...] = _group_rmsnorm(y * _silu(z_ref[...]), nw_ref[...])
    g_mat = g_ref[...]
    e_mat = e_ref[...]
    qn = _head_rmsnorm(q_ref[...], g_mat, e_mat, qw_ref[...]) * ATTN_SCALE
    head = lambda h: qn[:, h * ATTN_HEAD_DIM:(h + 1) * ATTN_HEAD_DIM]
    qn_ref[...] = jnp.concatenate(
        [head(n * Q_PER_KV + g) for g in range(Q_PER_KV) for n in range(ATTN_KV_HEADS)], axis=1)
    kn_ref[...] = _head_rmsnorm(k_ref[...], g_mat[:KV_WIDTH], e_mat[:, :KV_WIDTH], kw_ref[...])


def _sample_finish(ypart, yoff, ea, z, norm_w, q, k, qw, kw, g_mat, e_mat):
    rows = z.shape[0]
    f = lambda w: jax.ShapeDtypeStruct((rows, w), F32)
    return pl.pallas_call(_sample_finish_body, out_shape=[f(SSM_WIDTH), f(ATTN_WIDTH), f(KV_WIDTH)],
                          compiler_params=pltpu.CompilerParams(vmem_limit_bytes=VMEM_LIMIT),
                          name="sample_finish")(ypart, yoff, ea, z, norm_w, q, k, qw, kw, g_mat, e_mat)


def _attn_sample_body(q_ref, kn_ref, vn_ref, z_ref, ckt_ref, cvt_ref, biasc_ref, biasn_ref,
                      y_ref, kot_ref, vot_ref):
    steps = q_ref.shape[0]
    bb = SAMPLE_BATCH_BLOCK
    blk = Q_PER_KV * steps
    rows = ATTN_KV_HEADS * blk
    pad = jnp.zeros((SUBLANES - steps, KV_WIDTH), F32)
    lane_head = _lane_head((blk, KV_WIDTH))
    zero = jnp.zeros((blk, KV_WIDTH), F32)

    s_c, s_n, k8, v8 = [], [], [], []
    for j in range(bb):
        q = q_ref[:, j, :]
        qg = jnp.concatenate([q[:, g * KV_WIDTH:(g + 1) * KV_WIDTH] for g in range(Q_PER_KV)], axis=0)
        qx = jnp.concatenate([jnp.where(lane_head == n, qg, zero) for n in range(ATTN_KV_HEADS)], axis=0)
        qx = qx.astype(BF16)
        k8.append(jnp.concatenate([kn_ref[:, j, :], pad], axis=0))
        v8.append(jnp.concatenate([vn_ref[:, j, :], pad], axis=0))
        s_c.append(_dot(qx, ckt_ref[j].astype(BF16)))
        s_n.append(_dot_nt(qx, k8[j].astype(BF16)))
    s_c = jnp.concatenate(s_c, axis=0) + biasc_ref[...]
    s_n = jnp.concatenate(s_n, axis=0) + biasn_ref[...]
    m = jnp.maximum(jnp.max(s_c, axis=-1, keepdims=True), jnp.max(s_n, axis=-1, keepdims=True))
    p_c = jnp.exp(s_c - m)
    p_n = jnp.exp(s_n - m)
    inv = 1.0 / (jnp.sum(p_c, axis=-1, keepdims=True) + jnp.sum(p_n, axis=-1, keepdims=True))
    p_c = (p_c * inv).astype(BF16)
    p_n = (p_n * inv).astype(BF16)

    lane = lax.broadcasted_iota(jnp.int32, (KV_WIDTH, WINDOW), 1)
    for j in range(bb):
        r = slice(j * rows, (j + 1) * rows)
        o = _dot_nt(p_c[r], cvt_ref[j].astype(BF16)) + _dot(p_n[r], v8[j].astype(BF16))
        og = zero
        for n in range(ATTN_KV_HEADS):
            og = og + jnp.where(lane_head == n, o[n * blk:(n + 1) * blk], zero)
        y = jnp.concatenate(
            [og[g * steps:(g + 1) * steps, n * ATTN_HEAD_DIM:(n + 1) * ATTN_HEAD_DIM]
             for n in range(ATTN_KV_HEADS) for g in range(Q_PER_KV)], axis=1)
        y_ref[:, j, :] = y * _silu(z_ref[:, j, :])

        for new8, old_ref, out_ref in ((k8[j], ckt_ref, kot_ref), (v8[j], cvt_ref, vot_ref)):
            tail_rows = jnp.concatenate([new8[steps:], new8[:steps]], axis=0)
            block = jnp.concatenate([jnp.zeros((WINDOW - SUBLANES, KV_WIDTH), F32), tail_rows], axis=0)
            shifted = pltpu.roll(old_ref[j], WINDOW - steps, axis=1)
            out_ref[j] = jnp.where(lane >= WINDOW - steps, block.T, shifted)


def _attn_sample(q3, kn3, vn3, z3, cache_kt, cache_vt, bias_c, bias_n):
    steps, nb = q3.shape[0], q3.shape[1]
    bb = SAMPLE_BATCH_BLOCK
    tok = lambda w: pl.BlockSpec((steps, bb, w), lambda i: (0, i, 0))
    cache_spec = pl.BlockSpec((bb, KV_WIDTH, WINDOW), lambda i: (i, 0, 0))
    return pl.pallas_call(
        _attn_sample_body, grid=(nb // bb,),
        in_specs=[tok(ATTN_WIDTH), tok(KV_WIDTH), tok(KV_WIDTH), tok(ATTN_WIDTH), cache_spec, cache_spec,
                  _const_spec(bias_c.shape), _const_spec(bias_n.shape)],
        out_specs=[tok(ATTN_WIDTH), cache_spec, cache_spec],
        out_shape=[jax.ShapeDtypeStruct((steps, nb, ATTN_WIDTH), F32),
                   jax.ShapeDtypeStruct(cache_kt.shape, F32), jax.ShapeDtypeStruct(cache_vt.shape, F32)],
        compiler_params=_params("parallel"), name="attn_sample")(
            q3, kn3, vn3, z3, cache_kt, cache_vt, bias_c, bias_n)


def _static_tables(steps):
    lanes = np.arange(ATTN_WIDTH)
    g_mat = np.zeros((ATTN_WIDTH, LANES), np.float32)
    g_mat[lanes, lanes // ATTN_HEAD_DIM] = 1.0
    e_mat = g_mat.T.copy()
    bc = np.arange(BC_WIDTH)
    gh_mat = np.zeros((BC_WIDTH, LANES), np.float32)
    for h in range(SSM_HEADS):
        gh_mat[bc // SSM_STATE == h // (SSM_HEADS // SSM_GROUPS), h] = 1.0
    T = CHUNK
    dist = np.arange(T)[:, None] - (np.arange(2 * T) - T)[None, :]
    first = np.broadcast_to((np.arange(2 * T) >= T)[None, :], dist.shape)
    prompt_buckets = np.stack([_bucket_or_masked(dist, first), _bucket_or_masked(dist)])
    dist_c = (np.arange(steps) + WINDOW)[:, None] - np.arange(WINDOW)[None, :]
    dist_n = np.arange(steps)[:, None] - np.arange(SUBLANES)[None, :]
    real = np.broadcast_to((np.arange(SUBLANES) < steps)[None, :], dist_n.shape)
    return dict(g=g_mat, e=e_mat, gh=gh_mat, prompt_buckets=prompt_buckets,
                cache_buckets=_bucket_or_masked(dist_c)[None], new_buckets=_bucket_or_masked(dist_n, real)[None])


def kernel(x_prompt, x_sample, cache_k, cache_v, state_ssm, state_conv, norm_w, w_in, conv_w, conv_b, dt_bias,
           a_log, d_skip, ssm_norm_w, q_norm_w, k_norm_w, sinks, rel_table, w_out):
    assert w_in.shape[0] == 1, "single-layer kernel"
    batch, seq, _ = x_prompt.shape
    nb, steps, _ = x_sample.shape
    tab = _static_tables(steps)
    g_mat = jnp.asarray(tab["g"], BF16)
    e_mat = jnp.asarray(tab["e"], BF16)
    gh_mat = jnp.asarray(tab["gh"], BF16)

    w_t = jnp.transpose(w_in[0]).astype(BF16)
    wo_all = w_out[0].astype(BF16)

    row = lambda v, width: jnp.pad(v.reshape(1, -1), ((0, 0), (0, width - v.size)))
    nw = row(norm_w[0], D_MODEL)
    cw = conv_w[0]
    cb = row(conv_b[0], CONV_DIM)
    dtb = row(dt_bias[0], LANES)
    alog = row(a_log[0], LANES)
    dskip = jnp.repeat(d_skip[0], SSM_HEAD_DIM).reshape(1, SSM_WIDTH)
    snw = row(ssm_norm_w[0], SSM_WIDTH)
    qw = jnp.tile(q_norm_w[0], ATTN_HEADS).reshape(1, ATTN_WIDTH)
    kw = jnp.tile(k_norm_w[0], ATTN_KV_HEADS).reshape(1, KV_WIDTH)
    sink = sinks[0]
    rel_flat = rel_table.reshape(-1)

    xs = jnp.swapaxes(x_sample, 0, 1).reshape(steps * nb, D_MODEL)
    z, xbc, dt, q, k_smp, v_smp, za = _inproj(xs, nw, w_t)
    t3 = lambda a: a.reshape(steps, nb, a.shape[-1])
    sconv3 = jnp.swapaxes(state_conv[0], 0, 1)
    ypart, ea, xw, b3, c3, cdec, conv_s3 = _ssd_sample_vec(
        t3(z), t3(xbc), t3(dt), sconv3, cw, cb, dtb, alog, dskip, gh_mat, e_mat)
    state_in = state_ssm[0].reshape(nb, SSM_WIDTH, SSM_STATE)

    xp = x_prompt.reshape(batch * seq, D_MODEL)
    gz, xs_p, b_p, c_p, dt_p, q_t, k, v, gza_t, tail_p, st_s, yoff = _inproj_prompt(
        xp, nw, w_t, cw, cb, dtb, batch, seq, cdec[:, :SSM_HEADS].reshape(-1), state_in, c3, b3, xw)
    bias_t = _bias_tables_t(rel_flat * LOG2E, jnp.asarray(tab["prompt_buckets"].transpose(0, 2, 1)))
    qw_t = jnp.broadcast_to((qw * (ATTN_SCALE * LOG2E)).reshape(ATTN_WIDTH, 1),
                            (ATTN_WIDTH, CHUNKS_PER_STEP * CHUNK))
    sink_rows = jnp.repeat(sink.reshape(ATTN_KV_HEADS, Q_PER_KV) * LOG2E, CHUNK, axis=1)
    sink_rows = sink_rows.reshape(ATTN_KV_HEADS, 1, -1)
    y_p, st_p, k_p, v_p = _mixer(gz, xs_p, b_p, c_p, dt_p, alog, dskip, snw, e_mat, q_t, k, v, gza_t, qw_t, kw,
                                 g_mat[:KV_WIDTH], bias_t, sink_rows, xp, wo_all, batch, seq)
    y_p = y_p.reshape(batch, seq, D_MODEL)
    conv_p = tail_p[:, SUBLANES - (CONV_WIDTH - 1):, :]

    f2 = lambda a: a.reshape(steps * nb, a.shape[-1])
    y_ssm, qn, kn = _sample_finish(f2(ypart), f2(yoff), f2(ea), z, snw, q, k_smp, qw, kw, g_mat, e_mat)
    bias_c = _bias_tables(rel_flat, jnp.asarray(tab["cache_buckets"])).reshape(ATTN_HEADS * steps, WINDOW)
    bias_n = _bias_tables(rel_flat, jnp.asarray(tab["new_buckets"])).reshape(ATTN_HEADS * steps, SUBLANES)
    bias_n = bias_n.at[:, steps].set(jnp.repeat(sink, steps))
    bias_c = jnp.tile(bias_c, (SAMPLE_BATCH_BLOCK, 1))
    bias_n = jnp.tile(bias_n, (SAMPLE_BATCH_BLOCK, 1))
    to_t = lambda a: jnp.transpose(a[0], (0, 2, 3, 1)).reshape(nb, KV_WIDTH, WINDOW)
    from_t = lambda a: jnp.transpose(a.reshape(nb, ATTN_KV_HEADS, ATTN_HEAD_DIM, WINDOW), (0, 3, 1, 2))[None]
    y_attn3, k_st, v_st = _attn_sample(t3(qn), t3(kn), t3(v_smp), t3(za), to_t(cache_k), to_t(cache_v),
                                       bias_c, bias_n)
    k_s, v_s = from_t(k_st), from_t(v_st)
    y_s = _outproj(y_ssm, f2(y_attn3), xs, wo_all)
    y_s = jnp.swapaxes(y_s.reshape(steps, nb, D_MODEL), 0, 1)

    kv5 = lambda a: a.reshape(1, a.shape[0], WINDOW, ATTN_KV_HEADS, ATTN_HEAD_DIM)
    st5 = lambda a: a.reshape(1, a.shape[0], SSM_HEADS, SSM_HEAD_DIM, SSM_STATE)
    return (y_p, y_s, kv5(k_p), kv5(v_p), st5(st_p), conv_p[None],
            k_s, v_s, st5(st_s), jnp.swapaxes(conv_s3, 0, 1)[None])
```

```python
import functools
import math

import numpy as np
import jax
import jax.numpy as jnp
from jax import lax
from jax.experimental import pallas as pl
from jax.experimental.pallas import tpu as pltpu

F32 = jnp.float32
BF16 = jnp.bfloat16

D_MODEL = 1024
SSM_HEADS = 16
SSM_HEAD_DIM = 64
SSM_WIDTH = SSM_HEADS * SSM_HEAD_DIM
SSM_GROUPS = 2
SSM_STATE = 128
GROUP_WIDTH = SSM_WIDTH // SSM_GROUPS
BC_WIDTH = SSM_GROUPS * SSM_STATE
CONV_WIDTH = 4
CONV_DIM = SSM_WIDTH + 2 * BC_WIDTH
CHUNK = 128
ATTN_HEADS = 16
ATTN_KV_HEADS = 4
Q_PER_KV = ATTN_HEADS // ATTN_KV_HEADS
ATTN_HEAD_DIM = 64
ATTN_WIDTH = ATTN_HEADS * ATTN_HEAD_DIM
KV_WIDTH = ATTN_KV_HEADS * ATTN_HEAD_DIM
WINDOW = 128
ATTN_SCALE = ATTN_HEAD_DIM ** -0.5
REL_BUCKETS = 32
REL_MAX_DIST = 128
EPS = 1e-6
LOG2E = 1.0 / math.log(2.0)
NEG = -1e30

LANES = 128
SUBLANES = 8
MXU_WIDTH = 256
VMEM_LIMIT = 56 * 1024 * 1024


def _in_proj_rows():
    widths = (("z", SSM_WIDTH), ("xbc", CONV_DIM), ("dt", SSM_HEADS), ("q", ATTN_WIDTH), ("k", KV_WIDTH),
              ("v", KV_WIDTH), ("za", ATTN_WIDTH))
    rows, start = {}, 0
    for name, width in widths:
        rows[name] = slice(start, start + width)
        start += width
    return rows


IN_ROWS = _in_proj_rows()
DT_ROWS = slice(IN_ROWS["dt"].start, IN_ROWS["dt"].start + LANES)

PROJ_ROWS = 512
CHUNKS_PER_STEP = 4
SAMPLE_BATCH_BLOCK = 8


def _dot(a, b):
    return jnp.dot(a, b, preferred_element_type=F32)


def _dot_nt(a, b):
    return lax.dot_general(a, b, (((1,), (1,)), ((), ())), preferred_element_type=F32)


def _dot_tn(a, b):
    return lax.dot_general(a, b, (((0,), (0,)), ((), ())), preferred_element_type=F32)


def _split2(v):
    hi = v.astype(BF16)
    lo = (v - hi.astype(F32)).astype(BF16)
    return hi, lo


def _dot_sel(v, m):
    hi, lo = _split2(v)
    if 2 * v.shape[1] <= MXU_WIDTH:
        return _dot(jnp.concatenate([hi, lo], axis=1), jnp.concatenate([m, m], axis=0))
    return _dot(hi, m) + _dot(lo, m)


def _dot_sel3(m, v):
    hi = v.astype(BF16)
    r1 = v - hi.astype(F32)
    mid = r1.astype(BF16)
    lo = (r1 - mid.astype(F32)).astype(BF16)
    return _dot(m, hi) + _dot(m, mid) + _dot(m, lo)


def _silu(x):
    return x / (1.0 + jnp.exp(-x))


def _softplus(x):
    return jnp.maximum(x, 0.0) + jnp.log1p(jnp.exp(-jnp.abs(x)))


def _params(*sem):
    return pltpu.CompilerParams(dimension_semantics=sem, vmem_limit_bytes=VMEM_LIMIT)


def _const_spec(shape):
    nd = len(shape)
    return pl.BlockSpec(shape, lambda *_: (0,) * nd)


def _normed_input(x_ref, nw_ref):
    x = x_ref[...]
    ms = jnp.mean(x * x, axis=-1, keepdims=True)
    return (x * lax.rsqrt(ms + EPS) * nw_ref[...]).astype(BF16)


def _dt_projection(h, wt_ref):
    raw = _dot_nt(h, wt_ref[DT_ROWS, :])
    return jnp.where(lax.broadcasted_iota(jnp.int32, raw.shape, 1) < SSM_HEADS, raw, 0.0)


def _shift_rows(u, prev_tail, k):
    rows, width = u.shape
    tiles = jnp.concatenate([prev_tail, u], axis=0).reshape(rows // SUBLANES + 1, SUBLANES, width)
    rot = jnp.concatenate([tiles[:, SUBLANES - k:], tiles[:, :SUBLANES - k]], axis=1)
    first = lax.broadcasted_iota(jnp.int32, (1, SUBLANES, width), 1) < k
    return jnp.where(first, rot[:-1], rot[1:]).reshape(rows, width)


def _inproj_prompt_body(steps_per_seq, n_state_blocks, x_ref, nw_ref, wt_ref, cw_ref, cb_ref, dtb_ref,
                        cdec_ref, st_ref, sc_ref, sb_ref, sxw_ref,
                        gz_ref, xs_ref, b_ref, c_ref, dt_ref, qt_ref, k_ref, v_ref, gzat_ref, tail_ref,
                        newst_ref, yoff_ref, tail_sc):
    step = pl.program_id(0)

    @pl.when(step < n_state_blocks)
    def _():
        _ssd_sample_state_block(step, cdec_ref, st_ref, sc_ref, sb_ref, sxw_ref, newst_ref, yoff_ref)

    @pl.when(step % steps_per_seq == 0)
    def _():
        tail_sc[...] = jnp.zeros_like(tail_sc)

    h = _normed_input(x_ref, nw_ref)
    rows = h.shape[0]
    w_tile = lambda name, j: wt_ref[IN_ROWS[name].start + j * MXU_WIDTH:IN_ROWS[name].start + (j + 1) * MXU_WIDTH, :]
    for j in range(CONV_DIM // MXU_WIDTH):
        cols = slice(j * MXU_WIDTH, (j + 1) * MXU_WIDTH)
        u = _dot_nt(h, w_tile("xbc", j))
        prev_tail = tail_sc[:, cols]
        conv = cb_ref[:, cols] + u * cw_ref[CONV_WIDTH - 1:CONV_WIDTH, cols]
        for k in range(1, CONV_WIDTH):
            tap = CONV_WIDTH - 1 - k
            conv = conv + _shift_rows(u, prev_tail, k) * cw_ref[tap:tap + 1, cols]
        new_tail = u[rows - SUBLANES:, :]
        tail_sc[:, cols] = new_tail
        tail_ref[0, :, cols] = new_tail
        act = _silu(conv)
        if j < SSM_WIDTH // MXU_WIDTH:
            xs_ref[:, cols] = act
        elif j == SSM_WIDTH // MXU_WIDTH:
            b_ref[...] = act.astype(BF16)
        else:
            c_ref[...] = act.astype(BF16)

    for j in range(SSM_WIDTH // MXU_WIDTH):
        cols = slice(j * MXU_WIDTH, (j + 1) * MXU_WIDTH)
        gz_ref[:, cols] = _silu(_dot_nt(h, w_tile("z", j)))
    for j in range(ATTN_WIDTH // MXU_WIDTH):
        feats = slice(j * MXU_WIDTH, (j + 1) * MXU_WIDTH)
        gzat_ref[feats, :] = _silu(_dot_nt(w_tile("za", j), h))
    dt_ref[...] = _softplus(_dt_projection(h, wt_ref) + dtb_ref[...])
    qt_ref[...] = _dot_nt(wt_ref[IN_ROWS["q"], :], h)
    k_ref[...] = _dot_nt(h, wt_ref[IN_ROWS["k"], :])
    v_ref[...] = _dot_nt(h, wt_ref[IN_ROWS["v"], :])


def _inproj_prompt(x2d, norm_w, w_t, conv_w, conv_b, dtb, batch, seq, cdec_flat, state, c3, b3, xw3):
    rows = x2d.shape[0]
    tm = PROJ_ROWS
    steps_per_seq = seq // tm
    steps, nb = c3.shape[0], c3.shape[1]
    bb = SAMPLE_BATCH_BLOCK
    n_state_blocks = nb // bb
    assert n_state_blocks <= rows // tm, "state blocks ride on the in-proj grid steps"
    last = n_state_blocks - 1
    resident = lambda a: pl.BlockSpec(a.shape, lambda i: (0, 0), pipeline_mode=pl.Buffered(1))
    rowblk = lambda w: pl.BlockSpec((tm, w), lambda i: (i, 0))
    colblk = pl.BlockSpec((ATTN_WIDTH, tm), lambda i: (0, i))
    tok = lambda w: pl.BlockSpec((steps, bb, w), lambda i: (0, jnp.minimum(i, last), 0))
    st_spec = pl.BlockSpec((bb, SSM_WIDTH, SSM_STATE), lambda i: (jnp.minimum(i, last), 0, 0))
    in_specs = ([rowblk(D_MODEL), _const_spec((1, D_MODEL)), resident(w_t)]
                + [_const_spec(conv_w.shape), _const_spec(conv_b.shape), _const_spec(dtb.shape)]
                + [pl.BlockSpec(memory_space=pltpu.SMEM), st_spec, tok(BC_WIDTH), tok(BC_WIDTH), tok(SSM_WIDTH)])
    out_specs = [rowblk(SSM_WIDTH), rowblk(SSM_WIDTH), rowblk(BC_WIDTH), rowblk(BC_WIDTH), rowblk(LANES),
                 colblk, rowblk(KV_WIDTH), rowblk(KV_WIDTH), colblk,
                 pl.BlockSpec((1, SUBLANES, CONV_DIM), lambda i: (i // steps_per_seq, 0, 0)),
                 st_spec, tok(SSM_WIDTH)]
    f = lambda r, c, dt=F32: jax.ShapeDtypeStruct((r, c), dt)
    out_shape = [f(rows, SSM_WIDTH), f(rows, SSM_WIDTH), f(rows, BC_WIDTH, BF16), f(rows, BC_WIDTH, BF16),
                 f(rows, LANES), f(ATTN_WIDTH, rows), f(rows, KV_WIDTH), f(rows, KV_WIDTH), f(ATTN_WIDTH, rows),
                 jax.ShapeDtypeStruct((batch, SUBLANES, CONV_DIM), F32),
                 jax.ShapeDtypeStruct(state.shape, F32), jax.ShapeDtypeStruct((steps, nb, SSM_WIDTH), F32)]
    return pl.pallas_call(
        functools.partial(_inproj_prompt_body, steps_per_seq, n_state_blocks), grid=(rows // tm,),
        in_specs=in_specs, out_specs=out_specs, out_shape=out_shape,
        scratch_shapes=[pltpu.VMEM((SUBLANES, CONV_DIM), F32)],
        compiler_params=_params("arbitrary"), name="inproj_prompt")(
            x2d, norm_w, w_t, conv_w, conv_b, dtb, cdec_flat, state, c3, b3, xw3)


def _group_rmsnorm(gy, norm_w):
    parts = []
    for g in range(SSM_GROUPS):
        blk = gy[:, g * GROUP_WIDTH:(g + 1) * GROUP_WIDTH]
        ms = jnp.mean(blk * blk, axis=-1, keepdims=True)
        parts.append(blk * lax.rsqrt(ms + EPS))
    return jnp.concatenate(parts, axis=1) * norm_w


def _ssd_chunk(gz, xs, b_bf, c_bf, dt, a_neg, dskip, norm_w, e_mat, state):
    xs_bf = xs.astype(BF16)

    a = dt * a_neg
    li = lax.broadcasted_iota(jnp.int32, (CHUNK, CHUNK), 0)
    si = lax.broadcasted_iota(jnp.int32, (CHUNK, CHUNK), 1)
    causal = li >= si
    a_cum = _dot_sel3(jnp.where(causal, 1.0, 0.0).astype(BF16), a)
    a2 = a_cum * LOG2E
    row_term = a2.T - jnp.log2(dt.T)
    ea_full = _dot_sel(jnp.exp(a_cum), e_mat)
    w_full = _dot((dt * jnp.exp(a_cum[CHUNK - 1:CHUNK, :] - a_cum)).astype(BF16), e_mat)

    cb = [_dot_nt(c_bf[:, g * SSM_STATE:(g + 1) * SSM_STATE], b_bf[:, g * SSM_STATE:(g + 1) * SSM_STATE])
          for g in range(SSM_GROUPS)]
    half = lax.broadcasted_iota(jnp.int32, (CHUNK, LANES), 1) < SSM_HEAD_DIM
    heads_per_group = SSM_HEADS // SSM_GROUPS
    y_parts = []
    for pair in range(SSM_HEADS // 2):
        blocks = []
        for h in (2 * pair, 2 * pair + 1):
            seg = a2[:, h:h + 1] - row_term[h:h + 1, :]
            decay_dt = jnp.exp2(jnp.where(causal, seg, -jnp.inf))
            blocks.append((cb[h // heads_per_group] * decay_dt).astype(BF16))
        lhs = jnp.concatenate(blocks, axis=1)
        xp = xs_bf[:, pair * LANES:(pair + 1) * LANES]
        zero = jnp.zeros_like(xp)
        rhs = jnp.concatenate([jnp.where(half, xp, zero), jnp.where(half, zero, xp)], axis=0)
        y_parts.append(_dot(lhs, rhs))
    y_diag = jnp.concatenate(y_parts, axis=1)

    state_bf = state.astype(BF16)
    xw_bf = (xs * w_full).astype(BF16)
    y_off, upd = [], []
    for g in range(SSM_GROUPS):
        cols = slice(g * GROUP_WIDTH, (g + 1) * GROUP_WIDTH)
        ns = slice(g * SSM_STATE, (g + 1) * SSM_STATE)
        y_off.append(_dot(c_bf[:, ns], state_bf[:, cols]))
        upd.append(_dot_tn(b_bf[:, ns], xw_bf[:, cols]))
    y = y_diag + jnp.concatenate(y_off, axis=1) * ea_full + dskip * xs
    new_state = state * ea_full[CHUNK - 1:CHUNK, :] + jnp.concatenate(upd, axis=1)
    return _group_rmsnorm(y * gz, norm_w), new_state


def _rel_bucket_np(dist):
    max_exact = REL_BUCKETS // 2
    d_f = np.maximum(dist, 1).astype(np.float32)
    large = max_exact + (np.log(d_f / np.float32(max_exact)) / np.float32(math.log(REL_MAX_DIST / max_exact))
                         * np.float32(REL_BUCKETS - max_exact)).astype(np.int32)
    return np.where(dist < max_exact, dist, np.minimum(large, REL_BUCKETS - 1)).astype(np.int32)


def _bucket_or_masked(dist, extra_mask=None):
    ok = (dist >= 0) & (dist <= WINDOW)
    if extra_mask is not None:
        ok = ok & extra_mask
    return np.where(ok, _rel_bucket_np(np.clip(dist, 0, WINDOW)), -1).astype(np.int32)


def _bias_body(rel_ref, bucket_ref, o_ref):
    bucket = bucket_ref[0]

    def per_head(h, carry):
        acc = jnp.full(bucket.shape, NEG, F32)
        for bkt in range(REL_BUCKETS):
            acc = jnp.where(bucket == bkt, rel_ref[bkt * ATTN_HEADS + h], acc)
        o_ref[0, h] = acc
        return carry

    lax.fori_loop(0, ATTN_HEADS, per_head, 0)


def _bias_tables(rel_flat, buckets):
    nv, lq, lk = buckets.shape
    return pl.pallas_call(
        _bias_body, grid=(nv,),
        in_specs=[pl.BlockSpec(memory_space=pltpu.SMEM), pl.BlockSpec((1, lq, lk), lambda v: (v, 0, 0))],
        out_specs=pl.BlockSpec((1, ATTN_HEADS, lq, lk), lambda v: (v, 0, 0, 0)),
        out_shape=jax.ShapeDtypeStruct((nv, ATTN_HEADS, lq, lk), F32),
        compiler_params=_params("arbitrary"), name="rel_bias")(rel_flat, buckets)


def _bias_t_body(rel_ref, bucket_ref, o_ref):
    variants = [bucket_ref[v] for v in range(bucket_ref.shape[0])]
    union = functools.reduce(jnp.maximum, variants)
    lq = union.shape[1]

    def per_kv_head(n, carry):
        for g in range(Q_PER_KV):
            acc = jnp.full(union.shape, NEG, F32)
            for bkt in range(REL_BUCKETS):
                acc = jnp.where(union == bkt, rel_ref[bkt * ATTN_HEADS + n * Q_PER_KV + g], acc)
            for v, bucket in enumerate(variants):
                o_ref[v, n, :, g * lq:(g + 1) * lq] = jnp.where(bucket >= 0, acc, NEG)
        return carry

    lax.fori_loop(0, ATTN_KV_HEADS, per_kv_head, 0)


def _bias_tables_t(rel_flat, buckets_t):
    nv, lk, lq = buckets_t.shape
    out_dims = (nv, ATTN_KV_HEADS, lk, Q_PER_KV * lq)
    return pl.pallas_call(
        _bias_t_body,
        in_specs=[pl.BlockSpec(memory_space=pltpu.SMEM), pl.BlockSpec(memory_space=pltpu.VMEM)],
        out_specs=pl.BlockSpec(memory_space=pltpu.VMEM),
        out_shape=jax.ShapeDtypeStruct(out_dims, F32),
        compiler_params=pltpu.CompilerParams(vmem_limit_bytes=VMEM_LIMIT), name="rel_bias_t")(rel_flat, buckets_t)


def _head_rmsnorm(x, g_mat, e_mat, w):
    ms = _dot_sel(x * x, g_mat) * (1.0 / ATTN_HEAD_DIM)
    return x * _dot_sel(lax.rsqrt(ms + EPS), e_mat) * w


def _lane_head(shape):
    return lax.broadcasted_iota(jnp.int32, shape, 1) // ATTN_HEAD_DIM


def _attn_block(q_blk, kcat, vcat_t, bias_at, sink_ref):
    T = CHUNK
    lane_head = _lane_head((2 * T, KV_WIDTH))
    zero = jnp.zeros((2 * T, KV_WIDTH), BF16)
    head = lambda h: q_blk[h * ATTN_HEAD_DIM:(h + 1) * ATTN_HEAD_DIM]
    q_cols = jnp.concatenate(
        [jnp.concatenate([head(n * Q_PER_KV + g) for n in range(ATTN_KV_HEADS)], axis=0)
         for g in range(Q_PER_KV)], axis=1)
    row_head = lax.broadcasted_iota(jnp.int32, (KV_WIDTH, 2 * T), 0) // ATTN_HEAD_DIM
    probs, vals, inv = [], [], {}
    for n in range(ATTN_KV_HEADS):
        s = _dot(jnp.where(lane_head == n, kcat, zero), q_cols)
        sink = sink_ref[n]
        cols = []
        for g in range(Q_PER_KV):
            c = slice(g * T, (g + 1) * T)
            sg = s[:, c] + bias_at(n, c)
            m = jnp.maximum(jnp.max(sg, axis=0, keepdims=True), sink[:, c])
            p = jnp.exp2(sg - m)
            inv[n, g] = 1.0 / (jnp.sum(p, axis=0, keepdims=True) + jnp.exp2(sink[:, c] - m))
            cols.append(p.astype(BF16))
        probs.append(jnp.concatenate(cols, axis=1))
        vals.append(jnp.where(row_head == n, vcat_t, zero.T))
    o_t = _dot(jnp.concatenate(vals, axis=1), jnp.concatenate(probs, axis=0))
    return jnp.concatenate(
        [o_t[n * ATTN_HEAD_DIM:(n + 1) * ATTN_HEAD_DIM, g * T:(g + 1) * T] * inv[n, g]
         for n in range(ATTN_KV_HEADS) for g in range(Q_PER_KV)], axis=0)


def _mixer_body(gz_ref, xs_ref, b_ref, c_ref, dt_ref, alog_ref, dskip_ref, nw_ref, e_ref,
                qt_ref, k_ref, v_ref, gzt_ref, qwt_ref, kw_ref, g_ref, bias_ref, sink_ref, x_ref, wo_ref,
                y_ref, st_ref, kn_ref, vn_ref,
                state_sc, kcat_sc, vcat_t_sc, yssm_sc, yattn_t_sc):
    T = CHUNK
    step = pl.program_id(1)
    cols_step = CHUNKS_PER_STEP * T

    @pl.when(step == 0)
    def _():
        state_sc[...] = jnp.zeros_like(state_sc)
        kcat_sc[0:T, :] = jnp.zeros((T, KV_WIDTH), BF16)
        vcat_t_sc[:, 0:T] = jnp.zeros((KV_WIDTH, T), BF16)

    q3 = qt_ref[...].reshape(ATTN_HEADS, ATTN_HEAD_DIM, cols_step)
    ms = jnp.mean(q3 * q3, axis=1, keepdims=True)
    qn = ((q3 * lax.rsqrt(ms + EPS)).reshape(ATTN_WIDTH, cols_step) * qwt_ref[...]).astype(BF16)
    e_mat = e_ref[...]
    kn = _head_rmsnorm(k_ref[...], g_ref[...], e_mat[:, :KV_WIDTH], kw_ref[...])
    v = v_ref[...]
    kn_ref[0] = kn[cols_step - T:]
    vn_ref[0] = v[cols_step - T:]
    kcat_sc[T:, :] = kn.astype(BF16)
    vcat_t_sc[:, T:] = v.T.astype(BF16)

    a_neg = -jnp.exp(alog_ref[...])
    state = state_sc[...]
    first_variant = jnp.minimum(step, 1)
    for j in range(CHUNKS_PER_STEP):
        r = slice(j * T, (j + 1) * T)
        y, state = _ssd_chunk(gz_ref[r, :], xs_ref[r, :], b_ref[r, :], c_ref[r, :], dt_ref[r, :], a_neg,
                              dskip_ref[...], nw_ref[...], e_mat, state)
        yssm_sc[r, :] = y.astype(BF16)
        variant = first_variant if j == 0 else 1
        y_t = _attn_block(qn[:, r], kcat_sc[j * T:(j + 2) * T, :], vcat_t_sc[:, j * T:(j + 2) * T],
                          lambda n, c, variant=variant: bias_ref[variant, n, :, c], sink_ref)
        yattn_t_sc[:, r] = (y_t * gzt_ref[:, r]).astype(BF16)
    state_sc[...] = state
    kcat_sc[0:T, :] = kcat_sc[cols_step:, :]
    vcat_t_sc[:, 0:T] = vcat_t_sc[:, cols_step:]

    y_ref[...] = (x_ref[...] + _dot(yssm_sc[...], wo_ref[:SSM_WIDTH, :])
                  + _dot_tn(yattn_t_sc[...], wo_ref[SSM_WIDTH:, :]))

    @pl.when(step == pl.num_programs(1) - 1)
    def _():
        st_ref[0] = state.T


def _mixer(gz, xs, b, c, dt, alog, dskip, norm_w, e_mat, q_t, k, v, gz_t, qw_t, kw, g_mat, bias_t, sink_rows,
           x2d, w_out, batch, seq):
    step_rows = CHUNKS_PER_STEP * CHUNK
    ns = seq // step_rows
    row = lambda w: pl.BlockSpec((step_rows, w), lambda b, i: (b * ns + i, 0))
    col = pl.BlockSpec((ATTN_WIDTH, step_rows), lambda b, i: (0, b * ns + i))
    resident = lambda a: pl.BlockSpec(a.shape, lambda b, i: (0,) * a.ndim, pipeline_mode=pl.Buffered(1))
    in_specs = [row(SSM_WIDTH), row(SSM_WIDTH), row(BC_WIDTH), row(BC_WIDTH), row(LANES),
                _const_spec((1, LANES)), _const_spec((1, SSM_WIDTH)), _const_spec((1, SSM_WIDTH)), resident(e_mat),
                col, row(KV_WIDTH), row(KV_WIDTH), col, resident(qw_t), _const_spec((1, KV_WIDTH)),
                resident(g_mat), resident(bias_t), _const_spec(sink_rows.shape), row(D_MODEL), resident(w_out)]
    kv_out = pl.BlockSpec((1, CHUNK, KV_WIDTH), lambda b, i: (b, 0, 0))
    out_specs = [row(D_MODEL), pl.BlockSpec((1, SSM_WIDTH, SSM_STATE), lambda b, i: (b, 0, 0)), kv_out, kv_out]
    out_shape = [jax.ShapeDtypeStruct((batch * seq, D_MODEL), F32),
                 jax.ShapeDtypeStruct((batch, SSM_WIDTH, SSM_STATE), F32),
                 jax.ShapeDtypeStruct((batch, CHUNK, KV_WIDTH), F32),
                 jax.ShapeDtypeStruct((batch, CHUNK, KV_WIDTH), F32)]
    scratch = [pltpu.VMEM((SSM_STATE, SSM_WIDTH), F32),
               pltpu.VMEM((step_rows + CHUNK, KV_WIDTH), BF16), pltpu.VMEM((KV_WIDTH, step_rows + CHUNK), BF16),
               pltpu.VMEM((step_rows, SSM_WIDTH), BF16), pltpu.VMEM((ATTN_WIDTH, step_rows), BF16)]
    return pl.pallas_call(
        _mixer_body, grid=(batch, ns), in_specs=in_specs, out_specs=out_specs, out_shape=out_shape,
        scratch_shapes=scratch, compiler_params=_params("arbitrary", "arbitrary"), name="mixer")(
            gz, xs, b, c, dt, alog, dskip, norm_w, e_mat, q_t, k, v, gz_t, qw_t, kw, g_mat, bias_t, sink_rows,
            x2d, w_out)


def _sample_front_body(x_ref, nw_ref, wt_ref, sconv_ref, cw_ref, cb_ref, dtb_ref, alog_ref, dskip_ref,
                       gh_ref, e_ref, qw_ref, kw_ref, g_ref,
                       ypart_ref, ea_ref, xw_ref, b_ref, c_ref, cdec_ref, convnew_ref,
                       gz_ref, qn_ref, kn_ref, v_ref, za_ref):
    steps, nb = ypart_ref.shape[0], ypart_ref.shape[1]
    tail = CONV_WIDTH - 1
    h = _normed_input(x_ref, nw_ref)
    proj = lambda name: _dot_nt(h, wt_ref[IN_ROWS[name], :])
    slab = lambda a, l: a[l * nb:(l + 1) * nb]
    xbc = proj("xbc")
    dt_raw = _dt_projection(h, wt_ref)
    gz_ref[...] = _silu(proj("z"))
    v_ref[...] = proj("v")
    za_ref[...] = proj("za")
    g_mat = g_ref[...]
    e_mat = e_ref[...]
    qn = _head_rmsnorm(proj("q"), g_mat, e_mat, qw_ref[...]) * ATTN_SCALE
    head = lambda hd: qn[:, hd * ATTN_HEAD_DIM:(hd + 1) * ATTN_HEAD_DIM]
    qn_ref[...] = jnp.concatenate(
        [head(n * Q_PER_KV + g) for g in range(Q_PER_KV) for n in range(ATTN_KV_HEADS)], axis=1)
    kn_ref[...] = _head_rmsnorm(proj("k"), g_mat[:KV_WIDTH], e_mat[:, :KV_WIDTH], kw_ref[...])

    full = [sconv_ref[j] for j in range(tail)] + [slab(xbc, l) for l in range(steps)]
    for j in range(tail):
        convnew_ref[j] = full[steps + j]
    gh = gh_ref[...]
    a_neg = -jnp.exp(alog_ref[...])
    xs, bm, cm, dts, acum = [], [], [], [], []
    run = None
    for l in range(steps):
        conv = cb_ref[...]
        for tap in range(CONV_WIDTH):
            conv = conv + full[l + tap] * cw_ref[tap:tap + 1, :]
        act = _silu(conv)
        xs.append(act[:, :SSM_WIDTH])
        bm.append(act[:, SSM_WIDTH:SSM_WIDTH + BC_WIDTH])
        cm.append(act[:, SSM_WIDTH + BC_WIDTH:])
        d = _softplus(slab(dt_raw, l) + dtb_ref[...])
        dts.append(d)
        run = d * a_neg if run is None else run + d * a_neg
        acum.append(run)
        b_ref[l] = bm[l]
        c_ref[l] = cm[l]
    for l in range(steps):
        y = dskip_ref[...] * xs[l]
        for s in range(l + 1):
            cb_h = _dot_sel(cm[l] * bm[s], gh)
            coef = cb_h * jnp.exp(acum[l] - acum[s]) * dts[s]
            y = y + _dot_sel(coef, e_mat) * xs[s]
        ypart_ref[l] = y
        ea_ref[l] = _dot_sel(jnp.exp(acum[l]), e_mat)
        xw_ref[l] = xs[l] * _dot_sel(dts[l] * jnp.exp(acum[steps - 1] - acum[l]), e_mat)
    cdec_ref[...] = jnp.exp(acum[steps - 1])


def _sample_front(x2d, norm_w, w_t, sconv3, conv_w, conv_b, dtb, alog, dskip, gh_mat, e_mat, qw, kw, g_mat,
                  steps, nb):
    rows = steps * nb
    f = lambda *s: jax.ShapeDtypeStruct(s, F32)
    out_shape = [f(steps, nb, SSM_WIDTH), f(steps, nb, SSM_WIDTH), f(steps, nb, SSM_WIDTH),
                 f(steps, nb, BC_WIDTH), f(steps, nb, BC_WIDTH), f(nb, LANES), f(CONV_WIDTH - 1, nb, CONV_DIM),
                 f(rows, SSM_WIDTH), f(rows, ATTN_WIDTH), f(rows, KV_WIDTH), f(rows, KV_WIDTH), f(rows, ATTN_WIDTH)]
    return pl.pallas_call(_sample_front_body, out_shape=out_shape,
                          compiler_params=pltpu.CompilerParams(vmem_limit_bytes=VMEM_LIMIT),
                          name="sample_front")(
        x2d, norm_w, w_t, sconv3, conv_w, conv_b, dtb, alog, dskip, gh_mat, e_mat, qw, kw, g_mat)


def _ssd_sample_state_block(i, cdec_ref, st_ref, c_ref, b_ref, xw_ref, new_ref, yoff_ref):
    heads_per_group = SSM_HEADS // SSM_GROUPS
    for j in range(SAMPLE_BATCH_BLOCK):
        st = st_ref[j]
        cb_bf = c_ref[:, j, :].astype(BF16)
        bb_bf = b_ref[:, j, :].astype(BF16)
        xw_bf = xw_ref[:, j, :].astype(BF16)
        y_parts = []
        for g in range(SSM_GROUPS):
            rows = slice(g * GROUP_WIDTH, (g + 1) * GROUP_WIDTH)
            ns = slice(g * SSM_STATE, (g + 1) * SSM_STATE)
            y_parts.append(_dot_nt(cb_bf[:, ns], st[rows].astype(BF16)))
            upd = _dot_tn(xw_bf[:, rows], bb_bf[:, ns])
            for hh in range(heads_per_group):
                h = g * heads_per_group + hh
                r = slice(h * SSM_HEAD_DIM, (h + 1) * SSM_HEAD_DIM)
                dec = cdec_ref[(i * SAMPLE_BATCH_BLOCK + j) * SSM_HEADS + h]
                new_ref[j, r, :] = st[r] * dec + upd[hh * SSM_HEAD_DIM:(hh + 1) * SSM_HEAD_DIM]
        yoff_ref[:, j, :] = jnp.concatenate(y_parts, axis=1)


def _sample_back_body(ypart_ref, yoff_ref, ea_ref, gz_ref, nw_ref, ya_ref, x_ref, wo_ref, o_ref):
    y = ypart_ref[...] + yoff_ref[...] * ea_ref[...]
    y_ssm = _group_rmsnorm(y * gz_ref[...], nw_ref[...])
    o_ref[...] = (x_ref[...] + _dot(y_ssm.astype(BF16), wo_ref[:SSM_WIDTH, :])
                  + _dot(ya_ref[...].astype(BF16), wo_ref[SSM_WIDTH:, :]))


def _sample_back(ypart, yoff, ea, gz, norm_w, y_attn, x2d, w_out):
    return pl.pallas_call(_sample_back_body, out_shape=jax.ShapeDtypeStruct(x2d.shape, F32),
                          compiler_params=pltpu.CompilerParams(vmem_limit_bytes=VMEM_LIMIT),
                          name="sample_back")(ypart, yoff, ea, gz, norm_w, y_attn, x2d, w_out)


def _attn_sample_body(q_ref, kn_ref, vn_ref, z_ref, ckt_ref, cvt_ref, biasc_ref, biasn_ref,
                      y_ref, kot_ref, vot_ref):
    steps = q_ref.shape[0]
    bb = SAMPLE_BATCH_BLOCK
    blk = Q_PER_KV * steps
    rows = ATTN_KV_HEADS * blk
    pad = jnp.zeros((SUBLANES - steps, KV_WIDTH), F32)
    lane_head = _lane_head((blk, KV_WIDTH))
    zero = jnp.zeros((blk, KV_WIDTH), F32)

    s_c, s_n, k8, v8 = [], [], [], []
    for j in range(bb):
        q = q_ref[:, j, :]
        qg = jnp.concatenate([q[:, g * KV_WIDTH:(g + 1) * KV_WIDTH] for g in range(Q_PER_KV)], axis=0)
        qx = jnp.concatenate([jnp.where(lane_head == n, qg, zero) for n in range(ATTN_KV_HEADS)], axis=0)
        qx = qx.astype(BF16)
        k8.append(jnp.concatenate([kn_ref[:, j, :], pad], axis=0))
        v8.append(jnp.concatenate([vn_ref[:, j, :], pad], axis=0))
        s_c.append(_dot(qx, ckt_ref[j].astype(BF16)))
        s_n.append(_dot_nt(qx, k8[j].astype(BF16)))
    s_c = jnp.concatenate(s_c, axis=0) + biasc_ref[...]
    s_n = jnp.concatenate(s_n, axis=0) + biasn_ref[...]
    m = jnp.maximum(jnp.max(s_c, axis=-1, keepdims=True), jnp.max(s_n, axis=-1, keepdims=True))
    p_c = jnp.exp(s_c - m)
    p_n = jnp.exp(s_n - m)
    inv = 1.0 / (jnp.sum(p_c, axis=-1, keepdims=True) + jnp.sum(p_n, axis=-1, keepdims=True))
    p_c = (p_c * inv).astype(BF16)
    p_n = (p_n * inv).astype(BF16)

    lane = lax.broadcasted_iota(jnp.int32, (KV_WIDTH, WINDOW), 1)
    for j in range(bb):
        r = slice(j * rows, (j + 1) * rows)
        o = _dot_nt(p_c[r], cvt_ref[j].astype(BF16)) + _dot(p_n[r], v8[j].astype(BF16))
        og = zero
        for n in range(ATTN_KV_HEADS):
            og = og + jnp.where(lane_head == n, o[n * blk:(n + 1) * blk], zero)
        y = jnp.concatenate(
            [og[g * steps:(g + 1) * steps, n * ATTN_HEAD_DIM:(n + 1) * ATTN_HEAD_DIM]
             for n in range(ATTN_KV_HEADS) for g in range(Q_PER_KV)], axis=1)
        y_ref[:, j, :] = y * _silu(z_ref[:, j, :])

        for new8, old_ref, out_ref in ((k8[j], ckt_ref, kot_ref), (v8[j], cvt_ref, vot_ref)):
            tail_rows = jnp.concatenate([new8[steps:], new8[:steps]], axis=0)
            block = jnp.concatenate([jnp.zeros((WINDOW - SUBLANES, KV_WIDTH), F32), tail_rows], axis=0)
            shifted = pltpu.roll(old_ref[j], WINDOW - steps, axis=1)
            out_ref[j] = jnp.where(lane >= WINDOW - steps, block.T, shifted)


def _attn_sample(q3, kn3, vn3, z3, cache_kt, cache_vt, bias_c, bias_n):
    steps, nb = q3.shape[0], q3.shape[1]
    bb = SAMPLE_BATCH_BLOCK
    tok = lambda w: pl.BlockSpec((steps, bb, w), lambda i: (0, i, 0))
    cache_spec = pl.BlockSpec((bb, KV_WIDTH, WINDOW), lambda i: (i, 0, 0))
    return pl.pallas_call(
        _attn_sample_body, grid=(nb // bb,),
        in_specs=[tok(ATTN_WIDTH), tok(KV_WIDTH), tok(KV_WIDTH), tok(ATTN_WIDTH), cache_spec, cache_spec,
                  _const_spec(bias_c.shape), _const_spec(bias_n.shape)],
        out_specs=[tok(ATTN_WIDTH), cache_spec, cache_spec],
        out_shape=[jax.ShapeDtypeStruct((steps, nb, ATTN_WIDTH), F32),
                   jax.ShapeDtypeStruct(cache_kt.shape, F32), jax.ShapeDtypeStruct(cache_vt.shape, F32)],
        compiler_params=_params("parallel"), name="attn_sample")(
            q3, kn3, vn3, z3, cache_kt, cache_vt, bias_c, bias_n)


def _static_tables(steps):
    lanes = np.arange(ATTN_WIDTH)
    g_mat = np.zeros((ATTN_WIDTH, LANES), np.float32)
    g_mat[lanes, lanes // ATTN_HEAD_DIM] = 1.0
    e_mat = g_mat.T.copy()
    bc = np.arange(BC_WIDTH)
    gh_mat = np.zeros((BC_WIDTH, LANES), np.float32)
    for h in range(SSM_HEADS):
        gh_mat[bc // SSM_STATE == h // (SSM_HEADS // SSM_GROUPS), h] = 1.0
    T = CHUNK
    dist = np.arange(T)[:, None] - (np.arange(2 * T) - T)[None, :]
    first = np.broadcast_to((np.arange(2 * T) >= T)[None, :], dist.shape)
    prompt_buckets = np.stack([_bucket_or_masked(dist, first), _bucket_or_masked(dist)])
    dist_c = (np.arange(steps) + WINDOW)[:, None] - np.arange(WINDOW)[None, :]
    dist_n = np.arange(steps)[:, None] - np.arange(SUBLANES)[None, :]
    real = np.broadcast_to((np.arange(SUBLANES) < steps)[None, :], dist_n.shape)
    return dict(g=g_mat, e=e_mat, gh=gh_mat, prompt_buckets=prompt_buckets,
                cache_buckets=_bucket_or_masked(dist_c)[None], new_buckets=_bucket_or_masked(dist_n, real)[None])


def kernel(x_prompt, x_sample, cache_k, cache_v, state_ssm, state_conv, norm_w, w_in, conv_w, conv_b, dt_bias,
           a_log, d_skip, ssm_norm_w, q_norm_w, k_norm_w, sinks, rel_table, w_out):
    assert w_in.shape[0] == 1, "single-layer kernel"
    batch, seq, _ = x_prompt.shape
    nb, steps, _ = x_sample.shape
    tab = _static_tables(steps)
    g_mat = jnp.asarray(tab["g"], BF16)
    e_mat = jnp.asarray(tab["e"], BF16)
    gh_mat = jnp.asarray(tab["gh"], BF16)

    w_t = jnp.transpose(w_in[0]).astype(BF16)
    wo_all = w_out[0].astype(BF16)

    row = lambda v, width: jnp.pad(v.reshape(1, -1), ((0, 0), (0, width - v.size)))
    nw = row(norm_w[0], D_MODEL)
    cw = conv_w[0]
    cb = row(conv_b[0], CONV_DIM)
    dtb = row(dt_bias[0], LANES)
    alog = row(a_log[0], LANES)
    dskip = jnp.repeat(d_skip[0], SSM_HEAD_DIM).reshape(1, SSM_WIDTH)
    snw = row(ssm_norm_w[0], SSM_WIDTH)
    qw = jnp.tile(q_norm_w[0], ATTN_HEADS).reshape(1, ATTN_WIDTH)
    kw = jnp.tile(k_norm_w[0], ATTN_KV_HEADS).reshape(1, KV_WIDTH)
    sink = sinks[0]
    rel_flat = rel_table.reshape(-1)

    xs = jnp.swapaxes(x_sample, 0, 1).reshape(steps * nb, D_MODEL)
    t3 = lambda a: a.reshape(steps, nb, a.shape[-1])
    sconv3 = jnp.swapaxes(state_conv[0], 0, 1)
    ypart, ea, xw, b3, c3, cdec, conv_s3, gz_smp, qn, kn, v_smp, za = _sample_front(
        xs, nw, w_t, sconv3, cw, cb, dtb, alog, dskip, gh_mat, e_mat, qw, kw, g_mat, steps, nb)
    state_in = state_ssm[0].reshape(nb, SSM_WIDTH, SSM_STATE)

    xp = x_prompt.reshape(batch * seq, D_MODEL)
    gz, xs_p, b_p, c_p, dt_p, q_t, k, v, gza_t, tail_p, st_s, yoff = _inproj_prompt(
        xp, nw, w_t, cw, cb, dtb, batch, seq, cdec[:, :SSM_HEADS].reshape(-1), state_in, c3, b3, xw)
    bias_t = _bias_tables_t(rel_flat * LOG2E, jnp.asarray(tab["prompt_buckets"].transpose(0, 2, 1)))
    qw_t = jnp.broadcast_to((qw * (ATTN_SCALE * LOG2E)).reshape(ATTN_WIDTH, 1),
                            (ATTN_WIDTH, CHUNKS_PER_STEP * CHUNK))
    sink_rows = jnp.repeat(sink.reshape(ATTN_KV_HEADS, Q_PER_KV) * LOG2E, CHUNK, axis=1)
    sink_rows = sink_rows.reshape(ATTN_KV_HEADS, 1, -1)
    y_p, st_p, k_p, v_p = _mixer(gz, xs_p, b_p, c_p, dt_p, alog, dskip, snw, e_mat, q_t, k, v, gza_t, qw_t, kw,
                                 g_mat[:KV_WIDTH], bias_t, sink_rows, xp, wo_all, batch, seq)
    y_p = y_p.reshape(batch, seq, D_MODEL)
    conv_p = tail_p[:, SUBLANES - (CONV_WIDTH - 1):, :]

    f2 = lambda a: a.reshape(steps * nb, a.shape[-1])
    bias_c = _bias_tables(rel_flat, jnp.asarray(tab["cache_buckets"])).reshape(ATTN_HEADS * steps, WINDOW)
    bias_n = _bias_tables(rel_flat, jnp.asarray(tab["new_buckets"])).reshape(ATTN_HEADS * steps, SUBLANES)
    bias_n = bias_n.at[:, steps].set(jnp.repeat(sink, steps))
    bias_c = jnp.tile(bias_c, (SAMPLE_BATCH_BLOCK, 1))
    bias_n = jnp.tile(bias_n, (SAMPLE_BATCH_BLOCK, 1))
    to_t = lambda a: jnp.transpose(a[0], (0, 2, 3, 1)).reshape(nb, KV_WIDTH, WINDOW)
    from_t = lambda a: jnp.transpose(a.reshape(nb, ATTN_KV_HEADS, ATTN_HEAD_DIM, WINDOW), (0, 3, 1, 2))[None]
    y_attn3, k_st, v_st = _attn_sample(t3(qn), t3(kn), t3(v_smp), t3(za), to_t(cache_k), to_t(cache_v),
                                       bias_c, bias_n)
    k_s, v_s = from_t(k_st), from_t(v_st)
    y_s = _sample_back(f2(ypart), f2(yoff), f2(ea), gz_smp, snw, f2(y_attn3), xs, wo_all)
    y_s = jnp.swapaxes(y_s.reshape(steps, nb, D_MODEL), 0, 1)

    kv5 = lambda a: a.reshape(1, a.shape[0], WINDOW, ATTN_KV_HEADS, ATTN_HEAD_DIM)
    st5 = lambda a: a.reshape(1, a.shape[0], SSM_HEADS, SSM_HEAD_DIM, SSM_STATE)
    return (y_p, y_s, kv5(k_p), kv5(v_p), st5(st_p), conv_p[None],
            k_s, v_s, st5(st_s), jnp.swapaxes(conv_s3, 0, 1)[None])
```

```python
import functools
import math

import numpy as np
import jax
import jax.numpy as jnp
from jax import lax
from jax.experimental import pallas as pl
from jax.experimental.pallas import tpu as pltpu

F32 = jnp.float32
BF16 = jnp.bfloat16

D_MODEL = 1024
SSM_HEADS = 16
SSM_HEAD_DIM = 64
SSM_WIDTH = SSM_HEADS * SSM_HEAD_DIM
SSM_GROUPS = 2
SSM_STATE = 128
GROUP_WIDTH = SSM_WIDTH // SSM_GROUPS
BC_WIDTH = SSM_GROUPS * SSM_STATE
CONV_WIDTH = 4
CONV_DIM = SSM_WIDTH + 2 * BC_WIDTH
CHUNK = 128
ATTN_HEADS = 16
ATTN_KV_HEADS = 4
Q_PER_KV = ATTN_HEADS // ATTN_KV_HEADS
ATTN_HEAD_DIM = 64
ATTN_WIDTH = ATTN_HEADS * ATTN_HEAD_DIM
KV_WIDTH = ATTN_KV_HEADS * ATTN_HEAD_DIM
WINDOW = 128
ATTN_SCALE = ATTN_HEAD_DIM ** -0.5
REL_BUCKETS = 32
REL_MAX_DIST = 128
EPS = 1e-6
LOG2E = 1.0 / math.log(2.0)
NEG = -1e30

LANES = 128
SUBLANES = 8
MXU_WIDTH = 256
VMEM_LIMIT = 56 * 1024 * 1024


def _in_proj_rows():
    widths = (("z", SSM_WIDTH), ("xbc", CONV_DIM), ("dt", SSM_HEADS), ("q", ATTN_WIDTH), ("k", KV_WIDTH),
              ("v", KV_WIDTH), ("za", ATTN_WIDTH))
    rows, start = {}, 0
    for name, width in widths:
        rows[name] = slice(start, start + width)
        start += width
    return rows


IN_ROWS = _in_proj_rows()
DT_ROWS = slice(IN_ROWS["dt"].start, IN_ROWS["dt"].start + LANES)

PROJ_ROWS = 512
CHUNKS_PER_STEP = 4
SAMPLE_BATCH_BLOCK = 8


def _dot(a, b):
    return jnp.dot(a, b, preferred_element_type=F32)


def _dot_nt(a, b):
    return lax.dot_general(a, b, (((1,), (1,)), ((), ())), preferred_element_type=F32)


def _dot_tn(a, b):
    return lax.dot_general(a, b, (((0,), (0,)), ((), ())), preferred_element_type=F32)


def _split2(v):
    hi = v.astype(BF16)
    lo = (v - hi.astype(F32)).astype(BF16)
    return hi, lo


def _dot_sel(v, m):
    hi, lo = _split2(v)
    if 2 * v.shape[1] <= MXU_WIDTH:
        return _dot(jnp.concatenate([hi, lo], axis=1), jnp.concatenate([m, m], axis=0))
    return _dot(hi, m) + _dot(lo, m)


def _dot_sel3(m, v):
    hi = v.astype(BF16)
    r1 = v - hi.astype(F32)
    mid = r1.astype(BF16)
    lo = (r1 - mid.astype(F32)).astype(BF16)
    return _dot(m, hi) + _dot(m, mid) + _dot(m, lo)


def _silu(x):
    return x / (1.0 + jnp.exp(-x))


def _softplus(x):
    return jnp.maximum(x, 0.0) + jnp.log1p(jnp.exp(-jnp.abs(x)))


def _params(*sem):
    return pltpu.CompilerParams(dimension_semantics=sem, vmem_limit_bytes=VMEM_LIMIT)


def _const_spec(shape):
    nd = len(shape)
    return pl.BlockSpec(shape, lambda *_: (0,) * nd)


def _normed_input(x_ref, nw_ref):
    x = x_ref[...]
    ms = jnp.mean(x * x, axis=-1, keepdims=True)
    return (x * lax.rsqrt(ms + EPS) * nw_ref[...]).astype(BF16)


def _dt_projection(h, wt_ref):
    raw = _dot_nt(h, wt_ref[DT_ROWS, :])
    return jnp.where(lax.broadcasted_iota(jnp.int32, raw.shape, 1) < SSM_HEADS, raw, 0.0)


def _shift_rows(u, prev_tail, k):
    rows, width = u.shape
    tiles = jnp.concatenate([prev_tail, u], axis=0).reshape(rows // SUBLANES + 1, SUBLANES, width)
    rot = jnp.concatenate([tiles[:, SUBLANES - k:], tiles[:, :SUBLANES - k]], axis=1)
    first = lax.broadcasted_iota(jnp.int32, (1, SUBLANES, width), 1) < k
    return jnp.where(first, rot[:-1], rot[1:]).reshape(rows, width)


def _inproj_prompt_body(steps_per_seq, n_state_blocks, x_ref, nw_ref, wt_ref, cw_ref, cb_ref, dtb_ref,
                        cdec_ref, st_ref, sc_ref, sb_ref, sxw_ref,
                        gz_ref, xs_ref, b_ref, c_ref, dt_ref, qt_ref, kt_ref, vt_ref, gzat_ref, tail_ref,
                        newst_ref, yoff_ref, tail_sc):
    step = pl.program_id(0)

    @pl.when(step < n_state_blocks)
    def _():
        _ssd_sample_state_block(step, cdec_ref, st_ref, sc_ref, sb_ref, sxw_ref, newst_ref, yoff_ref)

    @pl.when(step % steps_per_seq == 0)
    def _():
        tail_sc[...] = jnp.zeros_like(tail_sc)

    h = _normed_input(x_ref, nw_ref)
    rows = h.shape[0]
    w_tile = lambda name, j: wt_ref[IN_ROWS[name].start + j * MXU_WIDTH:IN_ROWS[name].start + (j + 1) * MXU_WIDTH, :]
    n_side = SSM_WIDTH // MXU_WIDTH
    for j in range(CONV_DIM // MXU_WIDTH):
        cols = slice(j * MXU_WIDTH, (j + 1) * MXU_WIDTH)
        if j < n_side:
            both = _dot_nt(h, jnp.concatenate([w_tile("xbc", j), w_tile("z", j)], axis=0))
            u = both[:, :MXU_WIDTH]
            gz_ref[:, cols] = _silu(both[:, MXU_WIDTH:])
        else:
            u = _dot_nt(h, w_tile("xbc", j))
        prev_tail = tail_sc[:, cols]
        conv = cb_ref[:, cols] + u * cw_ref[CONV_WIDTH - 1:CONV_WIDTH, cols]
        for k in range(1, CONV_WIDTH):
            tap = CONV_WIDTH - 1 - k
            conv = conv + _shift_rows(u, prev_tail, k) * cw_ref[tap:tap + 1, cols]
        new_tail = u[rows - SUBLANES:, :]
        tail_sc[:, cols] = new_tail
        tail_ref[0, :, cols] = new_tail
        act = _silu(conv)
        if j < SSM_WIDTH // MXU_WIDTH:
            xs_ref[:, cols] = act
        elif j == SSM_WIDTH // MXU_WIDTH:
            b_ref[...] = act.astype(BF16)
        else:
            c_ref[...] = act.astype(BF16)

    for j in range(ATTN_WIDTH // MXU_WIDTH):
        feats = slice(j * MXU_WIDTH, (j + 1) * MXU_WIDTH)
        gzat_ref[feats, :] = _silu(_dot_nt(w_tile("za", j), h))
    dt_ref[...] = _softplus(_dt_projection(h, wt_ref) + dtb_ref[...])
    qt_ref[...] = _dot_nt(wt_ref[IN_ROWS["q"], :], h)
    kt_ref[...] = _dot_nt(wt_ref[IN_ROWS["k"], :], h)
    vt_ref[...] = _dot_nt(wt_ref[IN_ROWS["v"], :], h)


def _inproj_prompt(x2d, norm_w, w_t, conv_w, conv_b, dtb, batch, seq, cdec_flat, state, c3, b3, xw3):
    rows = x2d.shape[0]
    tm = PROJ_ROWS
    steps_per_seq = seq // tm
    steps, nb = c3.shape[0], c3.shape[1]
    bb = SAMPLE_BATCH_BLOCK
    n_state_blocks = nb // bb
    assert n_state_blocks <= rows // tm, "state blocks ride on the in-proj grid steps"
    last = n_state_blocks - 1
    resident = lambda a: pl.BlockSpec(a.shape, lambda i: (0, 0), pipeline_mode=pl.Buffered(1))
    rowblk = lambda w: pl.BlockSpec((tm, w), lambda i: (i, 0))
    colblk = pl.BlockSpec((ATTN_WIDTH, tm), lambda i: (0, i))
    kvblk = pl.BlockSpec((KV_WIDTH, tm), lambda i: (0, i))
    tok = lambda w: pl.BlockSpec((steps, bb, w), lambda i: (0, jnp.minimum(i, last), 0))
    st_spec = pl.BlockSpec((bb, SSM_WIDTH, SSM_STATE), lambda i: (jnp.minimum(i, last), 0, 0))
    in_specs = ([rowblk(D_MODEL), _const_spec((1, D_MODEL)), resident(w_t)]
                + [_const_spec(conv_w.shape), _const_spec(conv_b.shape), _const_spec(dtb.shape)]
                + [pl.BlockSpec(memory_space=pltpu.SMEM), st_spec, tok(BC_WIDTH), tok(BC_WIDTH), tok(SSM_WIDTH)])
    out_specs = [rowblk(SSM_WIDTH), rowblk(SSM_WIDTH), rowblk(BC_WIDTH), rowblk(BC_WIDTH), rowblk(LANES),
                 colblk, kvblk, kvblk, colblk,
                 pl.BlockSpec((1, SUBLANES, CONV_DIM), lambda i: (i // steps_per_seq, 0, 0)),
                 st_spec, tok(SSM_WIDTH)]
    f = lambda r, c, dt=F32: jax.ShapeDtypeStruct((r, c), dt)
    out_shape = [f(rows, SSM_WIDTH), f(rows, SSM_WIDTH), f(rows, BC_WIDTH, BF16), f(rows, BC_WIDTH, BF16),
                 f(rows, LANES), f(ATTN_WIDTH, rows), f(KV_WIDTH, rows), f(KV_WIDTH, rows), f(ATTN_WIDTH, rows),
                 jax.ShapeDtypeStruct((batch, SUBLANES, CONV_DIM), F32),
                 jax.ShapeDtypeStruct(state.shape, F32), jax.ShapeDtypeStruct((steps, nb, SSM_WIDTH), F32)]
    return pl.pallas_call(
        functools.partial(_inproj_prompt_body, steps_per_seq, n_state_blocks), grid=(rows // tm,),
        in_specs=in_specs, out_specs=out_specs, out_shape=out_shape,
        scratch_shapes=[pltpu.VMEM((SUBLANES, CONV_DIM), F32)],
        compiler_params=_params("arbitrary"), name="inproj_prompt")(
            x2d, norm_w, w_t, conv_w, conv_b, dtb, cdec_flat, state, c3, b3, xw3)


def _group_rmsnorm(gy, norm_w):
    parts = []
    for g in range(SSM_GROUPS):
        blk = gy[:, g * GROUP_WIDTH:(g + 1) * GROUP_WIDTH]
        ms = jnp.mean(blk * blk, axis=-1, keepdims=True)
        parts.append(blk * lax.rsqrt(ms + EPS))
    return jnp.concatenate(parts, axis=1) * norm_w


def _ssd_chunk(gz, xs, b_bf, c_bf, dt, a_neg, dskip, norm_w, e_mat, state):
    xs_bf = xs.astype(BF16)

    a = dt * a_neg
    li = lax.broadcasted_iota(jnp.int32, (CHUNK, CHUNK), 0)
    si = lax.broadcasted_iota(jnp.int32, (CHUNK, CHUNK), 1)
    causal = li >= si
    a_cum = _dot_sel3(jnp.where(causal, 1.0, 0.0).astype(BF16), a)
    a2 = a_cum * LOG2E
    row_term = a2.T - jnp.log2(dt.T)
    ea_full = _dot_sel(jnp.exp(a_cum), e_mat)
    w_full = _dot((dt * jnp.exp(a_cum[CHUNK - 1:CHUNK, :] - a_cum)).astype(BF16), e_mat)

    cb = [_dot_nt(c_bf[:, g * SSM_STATE:(g + 1) * SSM_STATE], b_bf[:, g * SSM_STATE:(g + 1) * SSM_STATE])
          for g in range(SSM_GROUPS)]
    half = lax.broadcasted_iota(jnp.int32, (CHUNK, LANES), 1) < SSM_HEAD_DIM
    heads_per_group = SSM_HEADS // SSM_GROUPS
    y_parts = []
    for pair in range(SSM_HEADS // 2):
        blocks = []
        for h in (2 * pair, 2 * pair + 1):
            seg = a2[:, h:h + 1] - row_term[h:h + 1, :]
            decay_dt = jnp.exp2(jnp.where(causal, seg, -jnp.inf))
            blocks.append((cb[h // heads_per_group] * decay_dt).astype(BF16))
        lhs = jnp.concatenate(blocks, axis=1)
        xp = xs_bf[:, pair * LANES:(pair + 1) * LANES]
        zero = jnp.zeros_like(xp)
        rhs = jnp.concatenate([jnp.where(half, xp, zero), jnp.where(half, zero, xp)], axis=0)
        y_parts.append(_dot(lhs, rhs))
    y_diag = jnp.concatenate(y_parts, axis=1)

    state_bf = state.astype(BF16)
    xw_bf = (xs * w_full).astype(BF16)
    y_off, upd = [], []
    for g in range(SSM_GROUPS):
        cols = slice(g * GROUP_WIDTH, (g + 1) * GROUP_WIDTH)
        ns = slice(g * SSM_STATE, (g + 1) * SSM_STATE)
        y_off.append(_dot(c_bf[:, ns], state_bf[:, cols]))
        upd.append(_dot_tn(b_bf[:, ns], xw_bf[:, cols]))
    y = y_diag + jnp.concatenate(y_off, axis=1) * ea_full + dskip * xs
    new_state = state * ea_full[CHUNK - 1:CHUNK, :] + jnp.concatenate(upd, axis=1)
    return _group_rmsnorm(y * gz, norm_w), new_state


def _rel_bucket_np(dist):
    max_exact = REL_BUCKETS // 2
    d_f = np.maximum(dist, 1).astype(np.float32)
    large = max_exact + (np.log(d_f / np.float32(max_exact)) / np.float32(math.log(REL_MAX_DIST / max_exact))
                         * np.float32(REL_BUCKETS - max_exact)).astype(np.int32)
    return np.where(dist < max_exact, dist, np.minimum(large, REL_BUCKETS - 1)).astype(np.int32)


def _bucket_or_masked(dist, extra_mask=None):
    ok = (dist >= 0) & (dist <= WINDOW)
    if extra_mask is not None:
        ok = ok & extra_mask
    return np.where(ok, _rel_bucket_np(np.clip(dist, 0, WINDOW)), -1).astype(np.int32)


def _bias_body(rel_ref, bucket_ref, o_ref):
    bucket = bucket_ref[0]

    def per_head(h, carry):
        acc = jnp.full(bucket.shape, NEG, F32)
        for bkt in range(REL_BUCKETS):
            acc = jnp.where(bucket == bkt, rel_ref[bkt * ATTN_HEADS + h], acc)
        o_ref[0, h] = acc
        return carry

    lax.fori_loop(0, ATTN_HEADS, per_head, 0)


def _bias_tables(rel_flat, buckets):
    nv, lq, lk = buckets.shape
    return pl.pallas_call(
        _bias_body, grid=(nv,),
        in_specs=[pl.BlockSpec(memory_space=pltpu.SMEM), pl.BlockSpec((1, lq, lk), lambda v: (v, 0, 0))],
        out_specs=pl.BlockSpec((1, ATTN_HEADS, lq, lk), lambda v: (v, 0, 0, 0)),
        out_shape=jax.ShapeDtypeStruct((nv, ATTN_HEADS, lq, lk), F32),
        compiler_params=_params("arbitrary"), name="rel_bias")(rel_flat, buckets)


def _bias_t_body(rel_ref, bucket_ref, o_ref):
    variants = [bucket_ref[v] for v in range(bucket_ref.shape[0])]
    union = functools.reduce(jnp.maximum, variants)
    lq = union.shape[1]

    def per_kv_head(n, carry):
        for g in range(Q_PER_KV):
            acc = jnp.full(union.shape, NEG, F32)
            for bkt in range(REL_BUCKETS):
                acc = jnp.where(union == bkt, rel_ref[bkt * ATTN_HEADS + n * Q_PER_KV + g], acc)
            for v, bucket in enumerate(variants):
                o_ref[v, n, :, g * lq:(g + 1) * lq] = jnp.where(bucket >= 0, acc, NEG)
        return carry

    lax.fori_loop(0, ATTN_KV_HEADS, per_kv_head, 0)


def _bias_tables_t(rel_flat, buckets_t):
    nv, lk, lq = buckets_t.shape
    out_dims = (nv, ATTN_KV_HEADS, lk, Q_PER_KV * lq)
    return pl.pallas_call(
        _bias_t_body,
        in_specs=[pl.BlockSpec(memory_space=pltpu.SMEM), pl.BlockSpec(memory_space=pltpu.VMEM)],
        out_specs=pl.BlockSpec(memory_space=pltpu.VMEM),
        out_shape=jax.ShapeDtypeStruct(out_dims, F32),
        compiler_params=pltpu.CompilerParams(vmem_limit_bytes=VMEM_LIMIT), name="rel_bias_t")(rel_flat, buckets_t)


def _head_rmsnorm(x, g_mat, e_mat, w):
    ms = _dot_sel(x * x, g_mat) * (1.0 / ATTN_HEAD_DIM)
    return x * _dot_sel(lax.rsqrt(ms + EPS), e_mat) * w


def _lane_head(shape):
    return lax.broadcasted_iota(jnp.int32, shape, 1) // ATTN_HEAD_DIM


def _attn_block(q_blk, kcat, vcat_t, bias_at, sink_ref):
    T = CHUNK
    lane_head = _lane_head((2 * T, KV_WIDTH))
    zero = jnp.zeros((2 * T, KV_WIDTH), BF16)
    head = lambda h: q_blk[h * ATTN_HEAD_DIM:(h + 1) * ATTN_HEAD_DIM]
    q_cols = jnp.concatenate(
        [jnp.concatenate([head(n * Q_PER_KV + g) for n in range(ATTN_KV_HEADS)], axis=0)
         for g in range(Q_PER_KV)], axis=1)
    row_head = lax.broadcasted_iota(jnp.int32, (KV_WIDTH, 2 * T), 0) // ATTN_HEAD_DIM
    probs, vals, inv = [], [], {}
    for n in range(ATTN_KV_HEADS):
        s = _dot(jnp.where(lane_head == n, kcat, zero), q_cols)
        sink = sink_ref[n]
        cols = []
        for g in range(Q_PER_KV):
            c = slice(g * T, (g + 1) * T)
            sg = s[:, c] + bias_at(n, c)
            m = jnp.maximum(jnp.max(sg, axis=0, keepdims=True), sink[:, c])
            p = jnp.exp2(sg - m)
            inv[n, g] = 1.0 / (jnp.sum(p, axis=0, keepdims=True) + jnp.exp2(sink[:, c] - m))
            cols.append(p.astype(BF16))
        probs.append(jnp.concatenate(cols, axis=1))
        vals.append(jnp.where(row_head == n, vcat_t, zero.T))
    o_t = _dot(jnp.concatenate(vals, axis=1), jnp.concatenate(probs, axis=0))
    return jnp.concatenate(
        [o_t[n * ATTN_HEAD_DIM:(n + 1) * ATTN_HEAD_DIM, g * T:(g + 1) * T] * inv[n, g]
         for n in range(ATTN_KV_HEADS) for g in range(Q_PER_KV)], axis=0)


def _mixer_body(gz_ref, xs_ref, b_ref, c_ref, dt_ref, alog_ref, dskip_ref, nw_ref, e_ref,
                qt_ref, kt_ref, vt_ref, gzt_ref, qwt_ref, kwt_ref, bias_ref, sink_ref, x_ref, wo_ref,
                y_ref, st_ref, knt_ref, vnt_ref,
                state_sc, kcat_sc, vcat_t_sc, yssm_sc, yattn_t_sc):
    T = CHUNK
    step = pl.program_id(1)
    cols_step = CHUNKS_PER_STEP * T

    @pl.when(step == 0)
    def _():
        state_sc[...] = jnp.zeros_like(state_sc)
        kcat_sc[0:T, :] = jnp.zeros((T, KV_WIDTH), BF16)
        vcat_t_sc[:, 0:T] = jnp.zeros((KV_WIDTH, T), BF16)

    def head_norm(t_ref, w_ref):
        x3 = t_ref[...].reshape(-1, ATTN_HEAD_DIM, cols_step)
        ms = jnp.mean(x3 * x3, axis=1, keepdims=True)
        return (x3 * lax.rsqrt(ms + EPS)).reshape(t_ref.shape) * w_ref[...]

    qn = head_norm(qt_ref, qwt_ref).astype(BF16)
    kn_t = head_norm(kt_ref, kwt_ref)
    v_t = vt_ref[...]
    knt_ref[0] = kn_t[:, cols_step - T:]
    vnt_ref[0] = v_t[:, cols_step - T:]
    kcat_sc[T:, :] = kn_t.T.astype(BF16)
    vcat_t_sc[:, T:] = v_t.astype(BF16)
    e_mat = e_ref[...]

    a_neg = -jnp.exp(alog_ref[...])
    state = state_sc[...]
    first_variant = jnp.minimum(step, 1)
    for j in range(CHUNKS_PER_STEP):
        r = slice(j * T, (j + 1) * T)
        y, state = _ssd_chunk(gz_ref[r, :], xs_ref[r, :], b_ref[r, :], c_ref[r, :], dt_ref[r, :], a_neg,
                              dskip_ref[...], nw_ref[...], e_mat, state)
        yssm_sc[r, :] = y.astype(BF16)
        variant = first_variant if j == 0 else 1
        y_t = _attn_block(qn[:, r], kcat_sc[j * T:(j + 2) * T, :], vcat_t_sc[:, j * T:(j + 2) * T],
                          lambda n, c, variant=variant: bias_ref[variant, n, :, c], sink_ref)
        yattn_t_sc[:, r] = (y_t * gzt_ref[:, r]).astype(BF16)
    state_sc[...] = state
    kcat_sc[0:T, :] = kcat_sc[cols_step:, :]
    vcat_t_sc[:, 0:T] = vcat_t_sc[:, cols_step:]

    y_ref[...] = (x_ref[...] + _dot(yssm_sc[...], wo_ref[:SSM_WIDTH, :])
                  + _dot_tn(yattn_t_sc[...], wo_ref[SSM_WIDTH:, :]))

    @pl.when(step == pl.num_programs(1) - 1)
    def _():
        st_ref[0] = state.T


def _mixer(gz, xs, b, c, dt, alog, dskip, norm_w, e_mat, q_t, k_t, v_t, gz_t, qw_t, kw_t, bias_t, sink_rows,
           x2d, w_out, batch, seq):
    step_rows = CHUNKS_PER_STEP * CHUNK
    ns = seq // step_rows
    row = lambda w: pl.BlockSpec((step_rows, w), lambda b, i: (b * ns + i, 0))
    col = lambda w: pl.BlockSpec((w, step_rows), lambda b, i: (0, b * ns + i))
    resident = lambda a: pl.BlockSpec(a.shape, lambda b, i: (0,) * a.ndim, pipeline_mode=pl.Buffered(1))
    in_specs = [row(SSM_WIDTH), row(SSM_WIDTH), row(BC_WIDTH), row(BC_WIDTH), row(LANES),
                _const_spec((1, LANES)), _const_spec((1, SSM_WIDTH)), _const_spec((1, SSM_WIDTH)), resident(e_mat),
                col(ATTN_WIDTH), col(KV_WIDTH), col(KV_WIDTH), col(ATTN_WIDTH), resident(qw_t), resident(kw_t),
                resident(bias_t), _const_spec(sink_rows.shape), row(D_MODEL), resident(w_out)]
    kv_out = pl.BlockSpec((1, KV_WIDTH, CHUNK), lambda b, i: (b, 0, 0))
    out_specs = [row(D_MODEL), pl.BlockSpec((1, SSM_WIDTH, SSM_STATE), lambda b, i: (b, 0, 0)), kv_out, kv_out]
    out_shape = [jax.ShapeDtypeStruct((batch * seq, D_MODEL), F32),
                 jax.ShapeDtypeStruct((batch, SSM_WIDTH, SSM_STATE), F32),
                 jax.ShapeDtypeStruct((batch, KV_WIDTH, CHUNK), F32),
                 jax.ShapeDtypeStruct((batch, KV_WIDTH, CHUNK), F32)]
    scratch = [pltpu.VMEM((SSM_STATE, SSM_WIDTH), F32),
               pltpu.VMEM((step_rows + CHUNK, KV_WIDTH), BF16), pltpu.VMEM((KV_WIDTH, step_rows + CHUNK), BF16),
               pltpu.VMEM((step_rows, SSM_WIDTH), BF16), pltpu.VMEM((ATTN_WIDTH, step_rows), BF16)]
    return pl.pallas_call(
        _mixer_body, grid=(batch, ns), in_specs=in_specs, out_specs=out_specs, out_shape=out_shape,
        scratch_shapes=scratch, compiler_params=_params("arbitrary", "arbitrary"), name="mixer")(
            gz, xs, b, c, dt, alog, dskip, norm_w, e_mat, q_t, k_t, v_t, gz_t, qw_t, kw_t, bias_t, sink_rows,
            x2d, w_out)


def _sample_front_body(x_ref, nw_ref, wt_ref, sconv_ref, cw_ref, cb_ref, dtb_ref, alog_ref, dskip_ref,
                       gh_ref, e_ref, qw_ref, kw_ref, g_ref,
                       ypart_ref, ea_ref, xw_ref, b_ref, c_ref, cdec_ref, convnew_ref,
                       gz_ref, qn_ref, kn_ref, v_ref, za_ref):
    steps, nb = ypart_ref.shape[0], ypart_ref.shape[1]
    tail = CONV_WIDTH - 1
    h = _normed_input(x_ref, nw_ref)
    proj = lambda name: _dot_nt(h, wt_ref[IN_ROWS[name], :])
    slab = lambda a, l: a[l * nb:(l + 1) * nb]
    xbc = proj("xbc")
    dt_raw = _dt_projection(h, wt_ref)
    gz_ref[...] = _silu(proj("z"))
    v_ref[...] = proj("v")
    za_ref[...] = proj("za")
    g_mat = g_ref[...]
    e_mat = e_ref[...]
    qn = _head_rmsnorm(proj("q"), g_mat, e_mat, qw_ref[...]) * ATTN_SCALE
    head = lambda hd: qn[:, hd * ATTN_HEAD_DIM:(hd + 1) * ATTN_HEAD_DIM]
    qn_ref[...] = jnp.concatenate(
        [head(n * Q_PER_KV + g) for g in range(Q_PER_KV) for n in range(ATTN_KV_HEADS)], axis=1)
    kn_ref[...] = _head_rmsnorm(proj("k"), g_mat[:KV_WIDTH], e_mat[:, :KV_WIDTH], kw_ref[...])

    full = [sconv_ref[j] for j in range(tail)] + [slab(xbc, l) for l in range(steps)]
    for j in range(tail):
        convnew_ref[j] = full[steps + j]
    gh = gh_ref[...]
    a_neg = -jnp.exp(alog_ref[...])
    xs, bm, cm, dts, acum = [], [], [], [], []
    run = None
    for l in range(steps):
        conv = cb_ref[...]
        for tap in range(CONV_WIDTH):
            conv = conv + full[l + tap] * cw_ref[tap:tap + 1, :]
        act = _silu(conv)
        xs.append(act[:, :SSM_WIDTH])
        bm.append(act[:, SSM_WIDTH:SSM_WIDTH + BC_WIDTH])
        cm.append(act[:, SSM_WIDTH + BC_WIDTH:])
        d = _softplus(slab(dt_raw, l) + dtb_ref[...])
        dts.append(d)
        run = d * a_neg if run is None else run + d * a_neg
        acum.append(run)
        b_ref[l] = bm[l]
        c_ref[l] = cm[l]
    for l in range(steps):
        y = dskip_ref[...] * xs[l]
        for s in range(l + 1):
            cb_h = _dot_sel(cm[l] * bm[s], gh)
            coef = cb_h * jnp.exp(acum[l] - acum[s]) * dts[s]
            y = y + _dot_sel(coef, e_mat) * xs[s]
        ypart_ref[l] = y
        ea_ref[l] = _dot_sel(jnp.exp(acum[l]), e_mat)
        xw_ref[l] = xs[l] * _dot_sel(dts[l] * jnp.exp(acum[steps - 1] - acum[l]), e_mat)
    cdec_ref[...] = jnp.exp(acum[steps - 1])


def _sample_front(x2d, norm_w, w_t, sconv3, conv_w, conv_b, dtb, alog, dskip, gh_mat, e_mat, qw, kw, g_mat,
                  steps, nb):
    rows = steps * nb
    f = lambda *s: jax.ShapeDtypeStruct(s, F32)
    out_shape = [f(steps, nb, SSM_WIDTH), f(steps, nb, SSM_WIDTH), f(steps, nb, SSM_WIDTH),
                 f(steps, nb, BC_WIDTH), f(steps, nb, BC_WIDTH), f(nb, LANES), f(CONV_WIDTH - 1, nb, CONV_DIM),
                 f(rows, SSM_WIDTH), f(rows, ATTN_WIDTH), f(rows, KV_WIDTH), f(rows, KV_WIDTH), f(rows, ATTN_WIDTH)]
    return pl.pallas_call(_sample_front_body, out_shape=out_shape,
                          compiler_params=pltpu.CompilerParams(vmem_limit_bytes=VMEM_LIMIT),
                          name="sample_front")(
        x2d, norm_w, w_t, sconv3, conv_w, conv_b, dtb, alog, dskip, gh_mat, e_mat, qw, kw, g_mat)


def _ssd_sample_state_block(i, cdec_ref, st_ref, c_ref, b_ref, xw_ref, new_ref, yoff_ref):
    heads_per_group = SSM_HEADS // SSM_GROUPS
    for j in range(SAMPLE_BATCH_BLOCK):
        st = st_ref[j]
        cb_bf = c_ref[:, j, :].astype(BF16)
        bb_bf = b_ref[:, j, :].astype(BF16)
        xw_bf = xw_ref[:, j, :].astype(BF16)
        y_parts = []
        for g in range(SSM_GROUPS):
            rows = slice(g * GROUP_WIDTH, (g + 1) * GROUP_WIDTH)
            ns = slice(g * SSM_STATE, (g + 1) * SSM_STATE)
            y_parts.append(_dot_nt(cb_bf[:, ns], st[rows].astype(BF16)))
            upd = _dot_tn(xw_bf[:, rows], bb_bf[:, ns])
            for hh in range(heads_per_group):
                h = g * heads_per_group + hh
                r = slice(h * SSM_HEAD_DIM, (h + 1) * SSM_HEAD_DIM)
                dec = cdec_ref[(i * SAMPLE_BATCH_BLOCK + j) * SSM_HEADS + h]
                new_ref[j, r, :] = st[r] * dec + upd[hh * SSM_HEAD_DIM:(hh + 1) * SSM_HEAD_DIM]
        yoff_ref[:, j, :] = jnp.concatenate(y_parts, axis=1)


def _sample_back_body(ypart_ref, yoff_ref, ea_ref, gz_ref, nw_ref, ya_ref, x_ref, wo_ref, o_ref):
    y = ypart_ref[...] + yoff_ref[...] * ea_ref[...]
    y_ssm = _group_rmsnorm(y * gz_ref[...], nw_ref[...])
    o_ref[...] = (x_ref[...] + _dot(y_ssm.astype(BF16), wo_ref[:SSM_WIDTH, :])
                  + _dot(ya_ref[...].astype(BF16), wo_ref[SSM_WIDTH:, :]))


def _sample_back(ypart, yoff, ea, gz, norm_w, y_attn, x2d, w_out):
    return pl.pallas_call(_sample_back_body, out_shape=jax.ShapeDtypeStruct(x2d.shape, F32),
                          compiler_params=pltpu.CompilerParams(vmem_limit_bytes=VMEM_LIMIT),
                          name="sample_back")(ypart, yoff, ea, gz, norm_w, y_attn, x2d, w_out)


def _attn_sample_body(q_ref, kn_ref, vn_ref, z_ref, ckt_ref, cvt_ref, biasc_ref, biasn_ref,
                      y_ref, kot_ref, vot_ref):
    steps = q_ref.shape[0]
    bb = SAMPLE_BATCH_BLOCK
    blk = Q_PER_KV * steps
    rows = ATTN_KV_HEADS * blk
    pad = jnp.zeros((SUBLANES - steps, KV_WIDTH), F32)
    lane_head = _lane_head((blk, KV_WIDTH))
    zero = jnp.zeros((blk, KV_WIDTH), F32)

    s_c, s_n, k8, v8 = [], [], [], []
    for j in range(bb):
        q = q_ref[:, j, :]
        qg = jnp.concatenate([q[:, g * KV_WIDTH:(g + 1) * KV_WIDTH] for g in range(Q_PER_KV)], axis=0)
        qx = jnp.concatenate([jnp.where(lane_head == n, qg, zero) for n in range(ATTN_KV_HEADS)], axis=0)
        qx = qx.astype(BF16)
        k8.append(jnp.concatenate([kn_ref[:, j, :], pad], axis=0))
        v8.append(jnp.concatenate([vn_ref[:, j, :], pad], axis=0))
        s_c.append(_dot(qx, ckt_ref[j].astype(BF16)))
        s_n.append(_dot_nt(qx, k8[j].astype(BF16)))
    s_c = jnp.concatenate(s_c, axis=0) + biasc_ref[...]
    s_n = jnp.concatenate(s_n, axis=0) + biasn_ref[...]
    m = jnp.maximum(jnp.max(s_c, axis=-1, keepdims=True), jnp.max(s_n, axis=-1, keepdims=True))
    p_c = jnp.exp(s_c - m)
    p_n = jnp.exp(s_n - m)
    inv = 1.0 / (jnp.sum(p_c, axis=-1, keepdims=True) + jnp.sum(p_n, axis=-1, keepdims=True))
    p_c = (p_c * inv).astype(BF16)
    p_n = (p_n * inv).astype(BF16)

    lane = lax.broadcasted_iota(jnp.int32, (KV_WIDTH, WINDOW), 1)
    for j in range(bb):
        r = slice(j * rows, (j + 1) * rows)
        o = _dot_nt(p_c[r], cvt_ref[j].astype(BF16)) + _dot(p_n[r], v8[j].astype(BF16))
        og = zero
        for n in range(ATTN_KV_HEADS):
            og = og + jnp.where(lane_head == n, o[n * blk:(n + 1) * blk], zero)
        y = jnp.concatenate(
            [og[g * steps:(g + 1) * steps, n * ATTN_HEAD_DIM:(n + 1) * ATTN_HEAD_DIM]
             for n in range(ATTN_KV_HEADS) for g in range(Q_PER_KV)], axis=1)
        y_ref[:, j, :] = y * _silu(z_ref[:, j, :])

        for new8, old_ref, out_ref in ((k8[j], ckt_ref, kot_ref), (v8[j], cvt_ref, vot_ref)):
            tail_rows = jnp.concatenate([new8[steps:], new8[:steps]], axis=0)
            block = jnp.concatenate([jnp.zeros((WINDOW - SUBLANES, KV_WIDTH), F32), tail_rows], axis=0)
            shifted = pltpu.roll(old_ref[j], WINDOW - steps, axis=1)
            out_ref[j] = jnp.where(lane >= WINDOW - steps, block.T, shifted)


def _attn_sample(q3, kn3, vn3, z3, cache_kt, cache_vt, bias_c, bias_n):
    steps, nb = q3.shape[0], q3.shape[1]
    bb = SAMPLE_BATCH_BLOCK
    tok = lambda w: pl.BlockSpec((steps, bb, w), lambda i: (0, i, 0))
    cache_spec = pl.BlockSpec((bb, KV_WIDTH, WINDOW), lambda i: (i, 0, 0))
    return pl.pallas_call(
        _attn_sample_body, grid=(nb // bb,),
        in_specs=[tok(ATTN_WIDTH), tok(KV_WIDTH), tok(KV_WIDTH), tok(ATTN_WIDTH), cache_spec, cache_spec,
                  _const_spec(bias_c.shape), _const_spec(bias_n.shape)],
        out_specs=[tok(ATTN_WIDTH), cache_spec, cache_spec],
        out_shape=[jax.ShapeDtypeStruct((steps, nb, ATTN_WIDTH), F32),
                   jax.ShapeDtypeStruct(cache_kt.shape, F32), jax.ShapeDtypeStruct(cache_vt.shape, F32)],
        compiler_params=_params("parallel"), name="attn_sample")(
            q3, kn3, vn3, z3, cache_kt, cache_vt, bias_c, bias_n)


def _static_tables(steps):
    lanes = np.arange(ATTN_WIDTH)
    g_mat = np.zeros((ATTN_WIDTH, LANES), np.float32)
    g_mat[lanes, lanes // ATTN_HEAD_DIM] = 1.0
    e_mat = g_mat.T.copy()
    bc = np.arange(BC_WIDTH)
    gh_mat = np.zeros((BC_WIDTH, LANES), np.float32)
    for h in range(SSM_HEADS):
        gh_mat[bc // SSM_STATE == h // (SSM_HEADS // SSM_GROUPS), h] = 1.0
    T = CHUNK
    dist = np.arange(T)[:, None] - (np.arange(2 * T) - T)[None, :]
    first = np.broadcast_to((np.arange(2 * T) >= T)[None, :], dist.shape)
    prompt_buckets = np.stack([_bucket_or_masked(dist, first), _bucket_or_masked(dist)])
    dist_c = (np.arange(steps) + WINDOW)[:, None] - np.arange(WINDOW)[None, :]
    dist_n = np.arange(steps)[:, None] - np.arange(SUBLANES)[None, :]
    real = np.broadcast_to((np.arange(SUBLANES) < steps)[None, :], dist_n.shape)
    return dict(g=g_mat, e=e_mat, gh=gh_mat, prompt_buckets=prompt_buckets,
                cache_buckets=_bucket_or_masked(dist_c)[None], new_buckets=_bucket_or_masked(dist_n, real)[None])


def kernel(x_prompt, x_sample, cache_k, cache_v, state_ssm, state_conv, norm_w, w_in, conv_w, conv_b, dt_bias,
           a_log, d_skip, ssm_norm_w, q_norm_w, k_norm_w, sinks, rel_table, w_out):
    assert w_in.shape[0] == 1, "single-layer kernel"
    batch, seq, _ = x_prompt.shape
    nb, steps, _ = x_sample.shape
    tab = _static_tables(steps)
    g_mat = jnp.asarray(tab["g"], BF16)
    e_mat = jnp.asarray(tab["e"], BF16)
    gh_mat = jnp.asarray(tab["gh"], BF16)

    w_t = jnp.transpose(w_in[0]).astype(BF16)
    wo_all = w_out[0].astype(BF16)

    row = lambda v, width: jnp.pad(v.reshape(1, -1), ((0, 0), (0, width - v.size)))
    nw = row(norm_w[0], D_MODEL)
    cw = conv_w[0]
    cb = row(conv_b[0], CONV_DIM)
    dtb = row(dt_bias[0], LANES)
    alog = row(a_log[0], LANES)
    dskip = jnp.repeat(d_skip[0], SSM_HEAD_DIM).reshape(1, SSM_WIDTH)
    snw = row(ssm_norm_w[0], SSM_WIDTH)
    qw = jnp.tile(q_norm_w[0], ATTN_HEADS).reshape(1, ATTN_WIDTH)
    kw = jnp.tile(k_norm_w[0], ATTN_KV_HEADS).reshape(1, KV_WIDTH)
    sink = sinks[0]
    rel_flat = rel_table.reshape(-1)

    xs = jnp.swapaxes(x_sample, 0, 1).reshape(steps * nb, D_MODEL)
    t3 = lambda a: a.reshape(steps, nb, a.shape[-1])
    sconv3 = jnp.swapaxes(state_conv[0], 0, 1)
    ypart, ea, xw, b3, c3, cdec, conv_s3, gz_smp, qn, kn, v_smp, za = _sample_front(
        xs, nw, w_t, sconv3, cw, cb, dtb, alog, dskip, gh_mat, e_mat, qw, kw, g_mat, steps, nb)
    state_in = state_ssm[0].reshape(nb, SSM_WIDTH, SSM_STATE)

    xp = x_prompt.reshape(batch * seq, D_MODEL)
    gz, xs_p, b_p, c_p, dt_p, q_t, k_t, v_t, gza_t, tail_p, st_s, yoff = _inproj_prompt(
        xp, nw, w_t, cw, cb, dtb, batch, seq, cdec[:, :SSM_HEADS].reshape(-1), state_in, c3, b3, xw)
    bias_t = _bias_tables_t(rel_flat * LOG2E, jnp.asarray(tab["prompt_buckets"].transpose(0, 2, 1)))
    step_cols = CHUNKS_PER_STEP * CHUNK
    qw_t = jnp.broadcast_to((qw * (ATTN_SCALE * LOG2E)).reshape(ATTN_WIDTH, 1), (ATTN_WIDTH, step_cols))
    kw_t = jnp.broadcast_to(kw.reshape(KV_WIDTH, 1), (KV_WIDTH, step_cols))
    sink_rows = jnp.repeat(sink.reshape(ATTN_KV_HEADS, Q_PER_KV) * LOG2E, CHUNK, axis=1)
    sink_rows = sink_rows.reshape(ATTN_KV_HEADS, 1, -1)
    y_p, st_p, k_pt, v_pt = _mixer(gz, xs_p, b_p, c_p, dt_p, alog, dskip, snw, e_mat, q_t, k_t, v_t, gza_t, qw_t,
                                   kw_t, bias_t, sink_rows, xp, wo_all, batch, seq)
    y_p = y_p.reshape(batch, seq, D_MODEL)
    conv_p = tail_p[:, SUBLANES - (CONV_WIDTH - 1):, :]

    f2 = lambda a: a.reshape(steps * nb, a.shape[-1])
    bias_c = _bias_tables(rel_flat, jnp.asarray(tab["cache_buckets"])).reshape(ATTN_HEADS * steps, WINDOW)
    bias_n = _bias_tables(rel_flat, jnp.asarray(tab["new_buckets"])).reshape(ATTN_HEADS * steps, SUBLANES)
    bias_n = bias_n.at[:, steps].set(jnp.repeat(sink, steps))
    bias_c = jnp.tile(bias_c, (SAMPLE_BATCH_BLOCK, 1))
    bias_n = jnp.tile(bias_n, (SAMPLE_BATCH_BLOCK, 1))
    to_t = lambda a: jnp.transpose(a[0], (0, 2, 3, 1)).reshape(nb, KV_WIDTH, WINDOW)
    from_t = lambda a: jnp.transpose(
        a.reshape(a.shape[0], ATTN_KV_HEADS, ATTN_HEAD_DIM, WINDOW), (0, 3, 1, 2))[None]
    y_attn3, k_st, v_st = _attn_sample(t3(qn), t3(kn), t3(v_smp), t3(za), to_t(cache_k), to_t(cache_v),
                                       bias_c, bias_n)
    k_s, v_s = from_t(k_st), from_t(v_st)
    y_s = _sample_back(f2(ypart), f2(yoff), f2(ea), gz_smp, snw, f2(y_attn3), xs, wo_all)
    y_s = jnp.swapaxes(y_s.reshape(steps, nb, D_MODEL), 0, 1)

    st5 = lambda a: a.reshape(1, a.shape[0], SSM_HEADS, SSM_HEAD_DIM, SSM_STATE)
    return (y_p, y_s, from_t(k_pt), from_t(v_pt), st5(st_p), conv_p[None],
            k_s, v_s, st5(st_s), jnp.swapaxes(conv_s3, 0, 1)[None])
```

```python
import functools
import math

import numpy as np
import jax
import jax.numpy as jnp
from jax import lax
from jax.experimental import pallas as pl
from jax.experimental.pallas import tpu as pltpu

F32 = jnp.float32
BF16 = jnp.bfloat16

D_MODEL = 1024
SSM_HEADS = 16
SSM_HEAD_DIM = 64
SSM_WIDTH = SSM_HEADS * SSM_HEAD_DIM
SSM_GROUPS = 2
SSM_STATE = 128
GROUP_WIDTH = SSM_WIDTH // SSM_GROUPS
BC_WIDTH = SSM_GROUPS * SSM_STATE
CONV_WIDTH = 4
CONV_DIM = SSM_WIDTH + 2 * BC_WIDTH
CHUNK = 128
ATTN_HEADS = 16
ATTN_KV_HEADS = 4
Q_PER_KV = ATTN_HEADS // ATTN_KV_HEADS
ATTN_HEAD_DIM = 64
ATTN_WIDTH = ATTN_HEADS * ATTN_HEAD_DIM
KV_WIDTH = ATTN_KV_HEADS * ATTN_HEAD_DIM
WINDOW = 128
ATTN_SCALE = ATTN_HEAD_DIM ** -0.5
REL_BUCKETS = 32
REL_MAX_DIST = 128
EPS = 1e-6
LOG2E = 1.0 / math.log(2.0)
NEG = -1e30

LANES = 128
SUBLANES = 8
MXU_WIDTH = 256
VMEM_LIMIT = 56 * 1024 * 1024


def _in_proj_rows():
    widths = (("z", SSM_WIDTH), ("xbc", CONV_DIM), ("dt", SSM_HEADS), ("q", ATTN_WIDTH), ("k", KV_WIDTH),
              ("v", KV_WIDTH), ("za", ATTN_WIDTH))
    rows, start = {}, 0
    for name, width in widths:
        rows[name] = slice(start, start + width)
        start += width
    return rows


IN_ROWS = _in_proj_rows()
DT_ROWS = slice(IN_ROWS["dt"].start, IN_ROWS["dt"].start + LANES)

PROJ_ROWS = 512
CHUNKS_PER_STEP = 4
SAMPLE_BATCH_BLOCK = 8


def _dot(a, b):
    return jnp.dot(a, b, preferred_element_type=F32)


def _dot_nt(a, b):
    return lax.dot_general(a, b, (((1,), (1,)), ((), ())), preferred_element_type=F32)


def _dot_tn(a, b):
    return lax.dot_general(a, b, (((0,), (0,)), ((), ())), preferred_element_type=F32)


def _split2(v):
    hi = v.astype(BF16)
    lo = (v - hi.astype(F32)).astype(BF16)
    return hi, lo


def _dot_sel(v, m):
    hi, lo = _split2(v)
    if 2 * v.shape[1] <= MXU_WIDTH:
        return _dot(jnp.concatenate([hi, lo], axis=1), jnp.concatenate([m, m], axis=0))
    return _dot(hi, m) + _dot(lo, m)


def _dot_sel3(m, v):
    hi = v.astype(BF16)
    r1 = v - hi.astype(F32)
    mid = r1.astype(BF16)
    lo = (r1 - mid.astype(F32)).astype(BF16)
    return _dot(m, hi) + _dot(m, mid) + _dot(m, lo)


def _silu(x):
    return x / (1.0 + jnp.exp(-x))


def _softplus(x):
    return jnp.maximum(x, 0.0) + jnp.log1p(jnp.exp(-jnp.abs(x)))


def _params(*sem):
    return pltpu.CompilerParams(dimension_semantics=sem, vmem_limit_bytes=VMEM_LIMIT)


def _const_spec(shape):
    nd = len(shape)
    return pl.BlockSpec(shape, lambda *_: (0,) * nd)


def _normed_input(x_ref, nw_ref):
    x = x_ref[...]
    ms = jnp.mean(x * x, axis=-1, keepdims=True)
    return (x * lax.rsqrt(ms + EPS) * nw_ref[...]).astype(BF16)


def _dt_projection(h, wt_ref):
    raw = _dot_nt(h, wt_ref[DT_ROWS, :])
    return jnp.where(lax.broadcasted_iota(jnp.int32, raw.shape, 1) < SSM_HEADS, raw, 0.0)


def _shift_rows(u, prev_tail, k):
    rows, width = u.shape
    tiles = jnp.concatenate([prev_tail, u], axis=0).reshape(rows // SUBLANES + 1, SUBLANES, width)
    rot = jnp.concatenate([tiles[:, SUBLANES - k:], tiles[:, :SUBLANES - k]], axis=1)
    first = lax.broadcasted_iota(jnp.int32, (1, SUBLANES, width), 1) < k
    return jnp.where(first, rot[:-1], rot[1:]).reshape(rows, width)


def _inproj_prompt_body(steps_per_seq, n_state_blocks, x_ref, nw_ref, wt_ref, cw_ref, cb_ref, dtb_ref,
                        qwt_ref, kwt_ref, cdec_ref, st_ref, sc_ref, sb_ref, sxw_ref,
                        gz_ref, xs_ref, b_ref, c_ref, dt_ref, qt_ref, kt_ref, vt_ref, gzat_ref, tail_ref,
                        newst_ref, yoff_ref, tail_sc):
    step = pl.program_id(0)

    @pl.when(step < n_state_blocks)
    def _():
        _ssd_sample_state_block(step, cdec_ref, st_ref, sc_ref, sb_ref, sxw_ref, newst_ref, yoff_ref)

    @pl.when(step % steps_per_seq == 0)
    def _():
        tail_sc[...] = jnp.zeros_like(tail_sc)

    h = _normed_input(x_ref, nw_ref)
    rows = h.shape[0]
    w_tile = lambda name, j: wt_ref[IN_ROWS[name].start + j * MXU_WIDTH:IN_ROWS[name].start + (j + 1) * MXU_WIDTH, :]
    n_side = SSM_WIDTH // MXU_WIDTH
    for j in range(CONV_DIM // MXU_WIDTH):
        cols = slice(j * MXU_WIDTH, (j + 1) * MXU_WIDTH)
        if j < n_side:
            both = _dot_nt(h, jnp.concatenate([w_tile("xbc", j), w_tile("z", j)], axis=0))
            u = both[:, :MXU_WIDTH]
            gz_ref[:, cols] = _silu(both[:, MXU_WIDTH:])
        else:
            u = _dot_nt(h, w_tile("xbc", j))
        prev_tail = tail_sc[:, cols]
        conv = cb_ref[:, cols] + u * cw_ref[CONV_WIDTH - 1:CONV_WIDTH, cols]
        for k in range(1, CONV_WIDTH):
            tap = CONV_WIDTH - 1 - k
            conv = conv + _shift_rows(u, prev_tail, k) * cw_ref[tap:tap + 1, cols]
        new_tail = u[rows - SUBLANES:, :]
        tail_sc[:, cols] = new_tail
        tail_ref[0, :, cols] = new_tail
        act = _silu(conv)
        if j < SSM_WIDTH // MXU_WIDTH:
            xs_ref[:, cols] = act
        elif j == SSM_WIDTH // MXU_WIDTH:
            b_ref[...] = act.astype(BF16)
        else:
            c_ref[...] = act.astype(BF16)

    for j in range(ATTN_WIDTH // MXU_WIDTH):
        feats = slice(j * MXU_WIDTH, (j + 1) * MXU_WIDTH)
        gzat_ref[feats, :] = _silu(_dot_nt(w_tile("za", j), h))
    dt_ref[...] = _softplus(_dt_projection(h, wt_ref) + dtb_ref[...])

    def head_norm(t, w):
        x3 = t.reshape(-1, ATTN_HEAD_DIM, rows)
        ms = jnp.mean(x3 * x3, axis=1, keepdims=True)
        return (x3 * lax.rsqrt(ms + EPS)).reshape(t.shape) * w

    for j in range(ATTN_WIDTH // MXU_WIDTH):
        feats = slice(j * MXU_WIDTH, (j + 1) * MXU_WIDTH)
        qt_ref[feats, :] = head_norm(_dot_nt(w_tile("q", j), h), qwt_ref[feats, :]).astype(BF16)
    kt_ref[...] = head_norm(_dot_nt(wt_ref[IN_ROWS["k"], :], h), kwt_ref[...])
    vt_ref[...] = _dot_nt(wt_ref[IN_ROWS["v"], :], h)


def _inproj_prompt(x2d, norm_w, w_t, conv_w, conv_b, dtb, qw_t, kw_t, batch, seq, cdec_flat, state, c3, b3, xw3):
    rows = x2d.shape[0]
    tm = PROJ_ROWS
    steps_per_seq = seq // tm
    steps, nb = c3.shape[0], c3.shape[1]
    bb = SAMPLE_BATCH_BLOCK
    n_state_blocks = nb // bb
    assert n_state_blocks <= rows // tm, "state blocks ride on the in-proj grid steps"
    last = n_state_blocks - 1
    resident = lambda a: pl.BlockSpec(a.shape, lambda i: (0, 0), pipeline_mode=pl.Buffered(1))
    rowblk = lambda w: pl.BlockSpec((tm, w), lambda i: (i, 0))
    colblk = pl.BlockSpec((ATTN_WIDTH, tm), lambda i: (0, i))
    kvblk = pl.BlockSpec((KV_WIDTH, tm), lambda i: (0, i))
    tok = lambda w: pl.BlockSpec((steps, bb, w), lambda i: (0, jnp.minimum(i, last), 0))
    st_spec = pl.BlockSpec((bb, SSM_WIDTH, SSM_STATE), lambda i: (jnp.minimum(i, last), 0, 0))
    in_specs = ([rowblk(D_MODEL), _const_spec((1, D_MODEL)), resident(w_t)]
                + [_const_spec(conv_w.shape), _const_spec(conv_b.shape), _const_spec(dtb.shape)]
                + [resident(qw_t), resident(kw_t)]
                + [pl.BlockSpec(memory_space=pltpu.SMEM), st_spec, tok(BC_WIDTH), tok(BC_WIDTH), tok(SSM_WIDTH)])
    out_specs = [rowblk(SSM_WIDTH), rowblk(SSM_WIDTH), rowblk(BC_WIDTH), rowblk(BC_WIDTH), rowblk(LANES),
                 colblk, kvblk, kvblk, colblk,
                 pl.BlockSpec((1, SUBLANES, CONV_DIM), lambda i: (i // steps_per_seq, 0, 0)),
                 st_spec, tok(SSM_WIDTH)]
    f = lambda r, c, dt=F32: jax.ShapeDtypeStruct((r, c), dt)
    out_shape = [f(rows, SSM_WIDTH), f(rows, SSM_WIDTH), f(rows, BC_WIDTH, BF16), f(rows, BC_WIDTH, BF16),
                 f(rows, LANES), f(ATTN_WIDTH, rows, BF16), f(KV_WIDTH, rows), f(KV_WIDTH, rows), f(ATTN_WIDTH, rows),
                 jax.ShapeDtypeStruct((batch, SUBLANES, CONV_DIM), F32),
                 jax.ShapeDtypeStruct(state.shape, F32), jax.ShapeDtypeStruct((steps, nb, SSM_WIDTH), F32)]
    return pl.pallas_call(
        functools.partial(_inproj_prompt_body, steps_per_seq, n_state_blocks), grid=(rows // tm,),
        in_specs=in_specs, out_specs=out_specs, out_shape=out_shape,
        scratch_shapes=[pltpu.VMEM((SUBLANES, CONV_DIM), F32)],
        compiler_params=_params("arbitrary"), name="inproj_prompt")(
            x2d, norm_w, w_t, conv_w, conv_b, dtb, qw_t, kw_t, cdec_flat, state, c3, b3, xw3)


def _group_rmsnorm(gy, norm_w):
    parts = []
    for g in range(SSM_GROUPS):
        blk = gy[:, g * GROUP_WIDTH:(g + 1) * GROUP_WIDTH]
        ms = jnp.mean(blk * blk, axis=-1, keepdims=True)
        parts.append(blk * lax.rsqrt(ms + EPS))
    return jnp.concatenate(parts, axis=1) * norm_w


def _ssd_chunk(gz, xs, b_bf, c_bf, dt, a_neg, dskip, norm_w, e_mat, state):
    xs_bf = xs.astype(BF16)

    a = dt * a_neg
    li = lax.broadcasted_iota(jnp.int32, (CHUNK, CHUNK), 0)
    si = lax.broadcasted_iota(jnp.int32, (CHUNK, CHUNK), 1)
    causal = li >= si
    a_cum = _dot_sel3(jnp.where(causal, 1.0, 0.0).astype(BF16), a)
    a2 = a_cum * LOG2E
    row_term = a2.T - jnp.log2(dt.T)
    ea_full = _dot_sel(jnp.exp(a_cum), e_mat)
    w_full = _dot((dt * jnp.exp(a_cum[CHUNK - 1:CHUNK, :] - a_cum)).astype(BF16), e_mat)

    cb = [_dot_nt(c_bf[:, g * SSM_STATE:(g + 1) * SSM_STATE], b_bf[:, g * SSM_STATE:(g + 1) * SSM_STATE])
          for g in range(SSM_GROUPS)]
    half = lax.broadcasted_iota(jnp.int32, (CHUNK, LANES), 1) < SSM_HEAD_DIM
    heads_per_group = SSM_HEADS // SSM_GROUPS
    y_parts = []
    for pair in range(SSM_HEADS // 2):
        blocks = []
        for h in (2 * pair, 2 * pair + 1):
            seg = a2[:, h:h + 1] - row_term[h:h + 1, :]
            decay_dt = jnp.exp2(jnp.where(causal, seg, -jnp.inf))
            blocks.append((cb[h // heads_per_group] * decay_dt).astype(BF16))
        lhs = jnp.concatenate(blocks, axis=1)
        xp = xs_bf[:, pair * LANES:(pair + 1) * LANES]
        zero = jnp.zeros_like(xp)
        rhs = jnp.concatenate([jnp.where(half, xp, zero), jnp.where(half, zero, xp)], axis=0)
        y_parts.append(_dot(lhs, rhs))
    y_diag = jnp.concatenate(y_parts, axis=1)

    state_bf = state.astype(BF16)
    xw_bf = (xs * w_full).astype(BF16)
    y_off, upd = [], []
    for g in range(SSM_GROUPS):
        cols = slice(g * GROUP_WIDTH, (g + 1) * GROUP_WIDTH)
        ns = slice(g * SSM_STATE, (g + 1) * SSM_STATE)
        y_off.append(_dot(c_bf[:, ns], state_bf[:, cols]))
        upd.append(_dot_tn(b_bf[:, ns], xw_bf[:, cols]))
    y = y_diag + jnp.concatenate(y_off, axis=1) * ea_full + dskip * xs
    new_state = state * ea_full[CHUNK - 1:CHUNK, :] + jnp.concatenate(upd, axis=1)
    return _group_rmsnorm(y * gz, norm_w), new_state


def _rel_bucket_np(dist):
    max_exact = REL_BUCKETS // 2
    d_f = np.maximum(dist, 1).astype(np.float32)
    large = max_exact + (np.log(d_f / np.float32(max_exact)) / np.float32(math.log(REL_MAX_DIST / max_exact))
                         * np.float32(REL_BUCKETS - max_exact)).astype(np.int32)
    return np.where(dist < max_exact, dist, np.minimum(large, REL_BUCKETS - 1)).astype(np.int32)


def _bucket_or_masked(dist, extra_mask=None):
    ok = (dist >= 0) & (dist <= WINDOW)
    if extra_mask is not None:
        ok = ok & extra_mask
    return np.where(ok, _rel_bucket_np(np.clip(dist, 0, WINDOW)), -1).astype(np.int32)


def _bias_body(rel_ref, bucket_ref, o_ref):
    bucket = bucket_ref[0]

    def per_head(h, carry):
        acc = jnp.full(bucket.shape, NEG, F32)
        for bkt in range(REL_BUCKETS):
            acc = jnp.where(bucket == bkt, rel_ref[bkt * ATTN_HEADS + h], acc)
        o_ref[0, h] = acc
        return carry

    lax.fori_loop(0, ATTN_HEADS, per_head, 0)


def _bias_tables(rel_flat, buckets):
    nv, lq, lk = buckets.shape
    return pl.pallas_call(
        _bias_body, grid=(nv,),
        in_specs=[pl.BlockSpec(memory_space=pltpu.SMEM), pl.BlockSpec((1, lq, lk), lambda v: (v, 0, 0))],
        out_specs=pl.BlockSpec((1, ATTN_HEADS, lq, lk), lambda v: (v, 0, 0, 0)),
        out_shape=jax.ShapeDtypeStruct((nv, ATTN_HEADS, lq, lk), F32),
        compiler_params=_params("arbitrary"), name="rel_bias")(rel_flat, buckets)


def _bias_t_body(rel_ref, bucket_ref, o_ref):
    variants = [bucket_ref[v] for v in range(bucket_ref.shape[0])]
    union = functools.reduce(jnp.maximum, variants)
    lq = union.shape[1]

    def per_kv_head(n, carry):
        for g in range(Q_PER_KV):
            acc = jnp.full(union.shape, NEG, F32)
            for bkt in range(REL_BUCKETS):
                acc = jnp.where(union == bkt, rel_ref[bkt * ATTN_HEADS + n * Q_PER_KV + g], acc)
            for v, bucket in enumerate(variants):
                o_ref[v, n, :, g * lq:(g + 1) * lq] = jnp.where(bucket >= 0, acc, NEG)
        return carry

    lax.fori_loop(0, ATTN_KV_HEADS, per_kv_head, 0)


def _bias_tables_t(rel_flat, buckets_t):
    nv, lk, lq = buckets_t.shape
    out_dims = (nv, ATTN_KV_HEADS, lk, Q_PER_KV * lq)
    return pl.pallas_call(
        _bias_t_body,
        in_specs=[pl.BlockSpec(memory_space=pltpu.SMEM), pl.BlockSpec(memory_space=pltpu.VMEM)],
        out_specs=pl.BlockSpec(memory_space=pltpu.VMEM),
        out_shape=jax.ShapeDtypeStruct(out_dims, F32),
        compiler_params=pltpu.CompilerParams(vmem_limit_bytes=VMEM_LIMIT), name="rel_bias_t")(rel_flat, buckets_t)


def _head_rmsnorm(x, g_mat, e_mat, w):
    ms = _dot_sel(x * x, g_mat) * (1.0 / ATTN_HEAD_DIM)
    return x * _dot_sel(lax.rsqrt(ms + EPS), e_mat) * w


def _lane_head(shape):
    return lax.broadcasted_iota(jnp.int32, shape, 1) // ATTN_HEAD_DIM


def _attn_block(q_blk, kcat, vcat_t, bias_at, sink_ref):
    T = CHUNK
    lane_head = _lane_head((2 * T, KV_WIDTH))
    zero = jnp.zeros((2 * T, KV_WIDTH), BF16)
    head = lambda h: q_blk[h * ATTN_HEAD_DIM:(h + 1) * ATTN_HEAD_DIM]
    q_cols = jnp.concatenate(
        [jnp.concatenate([head(n * Q_PER_KV + g) for n in range(ATTN_KV_HEADS)], axis=0)
         for g in range(Q_PER_KV)], axis=1)
    row_head = lax.broadcasted_iota(jnp.int32, (KV_WIDTH, 2 * T), 0) // ATTN_HEAD_DIM
    probs, vals, inv = [], [], {}
    for n in range(ATTN_KV_HEADS):
        s = _dot(jnp.where(lane_head == n, kcat, zero), q_cols)
        sink = sink_ref[n]
        cols = []
        for g in range(Q_PER_KV):
            c = slice(g * T, (g + 1) * T)
            sg = s[:, c] + bias_at(n, c)
            m = jnp.maximum(jnp.max(sg, axis=0, keepdims=True), sink[:, c])
            p = jnp.exp2(sg - m)
            inv[n, g] = 1.0 / (jnp.sum(p, axis=0, keepdims=True) + jnp.exp2(sink[:, c] - m))
            cols.append(p.astype(BF16))
        probs.append(jnp.concatenate(cols, axis=1))
        vals.append(jnp.where(row_head == n, vcat_t, zero.T))
    o_t = _dot(jnp.concatenate(vals, axis=1), jnp.concatenate(probs, axis=0))
    return jnp.concatenate(
        [o_t[n * ATTN_HEAD_DIM:(n + 1) * ATTN_HEAD_DIM, g * T:(g + 1) * T] * inv[n, g]
         for n in range(ATTN_KV_HEADS) for g in range(Q_PER_KV)], axis=0)


def _mixer_body(gz_ref, xs_ref, b_ref, c_ref, dt_ref, alog_ref, dskip_ref, nw_ref, e_ref,
                qt_ref, kt_ref, vt_ref, gzt_ref, bias_ref, sink_ref, x_ref, wo_ref,
                y_ref, st_ref, knt_ref, vnt_ref,
                state_sc, kcat_sc, vcat_t_sc, yssm_sc, yattn_t_sc):
    T = CHUNK
    step = pl.program_id(1)
    cols_step = CHUNKS_PER_STEP * T

    @pl.when(step == 0)
    def _():
        state_sc[...] = jnp.zeros_like(state_sc)
        kcat_sc[0:T, :] = jnp.zeros((T, KV_WIDTH), BF16)
        vcat_t_sc[:, 0:T] = jnp.zeros((KV_WIDTH, T), BF16)

    qn = qt_ref[...]
    kn_t = kt_ref[...]
    v_t = vt_ref[...]
    knt_ref[0] = kn_t[:, cols_step - T:]
    vnt_ref[0] = v_t[:, cols_step - T:]
    kcat_sc[T:, :] = kn_t.T.astype(BF16)
    vcat_t_sc[:, T:] = v_t.astype(BF16)
    e_mat = e_ref[...]

    a_neg = -jnp.exp(alog_ref[...])
    state = state_sc[...]
    first_variant = jnp.minimum(step, 1)
    for j in range(CHUNKS_PER_STEP):
        r = slice(j * T, (j + 1) * T)
        y, state = _ssd_chunk(gz_ref[r, :], xs_ref[r, :], b_ref[r, :], c_ref[r, :], dt_ref[r, :], a_neg,
                              dskip_ref[...], nw_ref[...], e_mat, state)
        yssm_sc[r, :] = y.astype(BF16)
        variant = first_variant if j == 0 else 1
        y_t = _attn_block(qn[:, r], kcat_sc[j * T:(j + 2) * T, :], vcat_t_sc[:, j * T:(j + 2) * T],
                          lambda n, c, variant=variant: bias_ref[variant, n, :, c], sink_ref)
        yattn_t_sc[:, r] = (y_t * gzt_ref[:, r]).astype(BF16)
    state_sc[...] = state
    kcat_sc[0:T, :] = kcat_sc[cols_step:, :]
    vcat_t_sc[:, 0:T] = vcat_t_sc[:, cols_step:]

    y_ref[...] = (x_ref[...] + _dot(yssm_sc[...], wo_ref[:SSM_WIDTH, :])
                  + _dot_tn(yattn_t_sc[...], wo_ref[SSM_WIDTH:, :]))

    @pl.when(step == pl.num_programs(1) - 1)
    def _():
        st_ref[0] = state.T


def _mixer(gz, xs, b, c, dt, alog, dskip, norm_w, e_mat, q_t, k_t, v_t, gz_t, bias_t, sink_rows,
           x2d, w_out, batch, seq):
    step_rows = CHUNKS_PER_STEP * CHUNK
    ns = seq // step_rows
    row = lambda w: pl.BlockSpec((step_rows, w), lambda b, i: (b * ns + i, 0))
    col = lambda w: pl.BlockSpec((w, step_rows), lambda b, i: (0, b * ns + i))
    resident = lambda a: pl.BlockSpec(a.shape, lambda b, i: (0,) * a.ndim, pipeline_mode=pl.Buffered(1))
    in_specs = [row(SSM_WIDTH), row(SSM_WIDTH), row(BC_WIDTH), row(BC_WIDTH), row(LANES),
                _const_spec((1, LANES)), _const_spec((1, SSM_WIDTH)), _const_spec((1, SSM_WIDTH)), resident(e_mat),
                col(ATTN_WIDTH), col(KV_WIDTH), col(KV_WIDTH), col(ATTN_WIDTH),
                resident(bias_t), _const_spec(sink_rows.shape), row(D_MODEL), resident(w_out)]
    kv_out = pl.BlockSpec((1, KV_WIDTH, CHUNK), lambda b, i: (b, 0, 0))
    out_specs = [row(D_MODEL), pl.BlockSpec((1, SSM_WIDTH, SSM_STATE), lambda b, i: (b, 0, 0)), kv_out, kv_out]
    out_shape = [jax.ShapeDtypeStruct((batch * seq, D_MODEL), F32),
                 jax.ShapeDtypeStruct((batch, SSM_WIDTH, SSM_STATE), F32),
                 jax.ShapeDtypeStruct((batch, KV_WIDTH, CHUNK), F32),
                 jax.ShapeDtypeStruct((batch, KV_WIDTH, CHUNK), F32)]
    scratch = [pltpu.VMEM((SSM_STATE, SSM_WIDTH), F32),
               pltpu.VMEM((step_rows + CHUNK, KV_WIDTH), BF16), pltpu.VMEM((KV_WIDTH, step_rows + CHUNK), BF16),
               pltpu.VMEM((step_rows, SSM_WIDTH), BF16), pltpu.VMEM((ATTN_WIDTH, step_rows), BF16)]
    return pl.pallas_call(
        _mixer_body, grid=(batch, ns), in_specs=in_specs, out_specs=out_specs, out_shape=out_shape,
        scratch_shapes=scratch, compiler_params=_params("arbitrary", "arbitrary"), name="mixer")(
            gz, xs, b, c, dt, alog, dskip, norm_w, e_mat, q_t, k_t, v_t, gz_t, bias_t, sink_rows, x2d, w_out)


def _sample_front_body(x_ref, nw_ref, wt_ref, sconv_ref, cw_ref, cb_ref, dtb_ref, alog_ref, dskip_ref,
                       gh_ref, e_ref, qw_ref, kw_ref, g_ref,
                       ypart_ref, ea_ref, xw_ref, b_ref, c_ref, cdec_ref, convnew_ref,
                       gz_ref, qn_ref, kn_ref, v_ref, za_ref):
    steps, nb = ypart_ref.shape[0], ypart_ref.shape[1]
    tail = CONV_WIDTH - 1
    h = _normed_input(x_ref, nw_ref)
    proj = lambda name: _dot_nt(h, wt_ref[IN_ROWS[name], :])
    slab = lambda a, l: a[l * nb:(l + 1) * nb]
    xbc = proj("xbc")
    dt_raw = _dt_projection(h, wt_ref)
    gz_ref[...] = _silu(proj("z"))
    v_ref[...] = proj("v")
    za_ref[...] = proj("za")
    g_mat = g_ref[...]
    e_mat = e_ref[...]
    qn = _head_rmsnorm(proj("q"), g_mat, e_mat, qw_ref[...]) * ATTN_SCALE
    head = lambda hd: qn[:, hd * ATTN_HEAD_DIM:(hd + 1) * ATTN_HEAD_DIM]
    qn_ref[...] = jnp.concatenate(
        [head(n * Q_PER_KV + g) for g in range(Q_PER_KV) for n in range(ATTN_KV_HEADS)], axis=1)
    kn_ref[...] = _head_rmsnorm(proj("k"), g_mat[:KV_WIDTH], e_mat[:, :KV_WIDTH], kw_ref[...])

    full = [sconv_ref[j] for j in range(tail)] + [slab(xbc, l) for l in range(steps)]
    for j in range(tail):
        convnew_ref[j] = full[steps + j]
    gh = gh_ref[...]
    a_neg = -jnp.exp(alog_ref[...])
    xs, bm, cm, dts, acum = [], [], [], [], []
    run = None
    for l in range(steps):
        conv = cb_ref[...]
        for tap in range(CONV_WIDTH):
            conv = conv + full[l + tap] * cw_ref[tap:tap + 1, :]
        act = _silu(conv)
        xs.append(act[:, :SSM_WIDTH])
        bm.append(act[:, SSM_WIDTH:SSM_WIDTH + BC_WIDTH])
        cm.append(act[:, SSM_WIDTH + BC_WIDTH:])
        d = _softplus(slab(dt_raw, l) + dtb_ref[...])
        dts.append(d)
        run = d * a_neg if run is None else run + d * a_neg
        acum.append(run)
        b_ref[l] = bm[l]
        c_ref[l] = cm[l]
    for l in range(steps):
        y = dskip_ref[...] * xs[l]
        for s in range(l + 1):
            cb_h = _dot_sel(cm[l] * bm[s], gh)
            coef = cb_h * jnp.exp(acum[l] - acum[s]) * dts[s]
            y = y + _dot_sel(coef, e_mat) * xs[s]
        ypart_ref[l] = y
        ea_ref[l] = _dot_sel(jnp.exp(acum[l]), e_mat)
        xw_ref[l] = xs[l] * _dot_sel(dts[l] * jnp.exp(acum[steps - 1] - acum[l]), e_mat)
    cdec_ref[...] = jnp.exp(acum[steps - 1])


def _sample_front(x2d, norm_w, w_t, sconv3, conv_w, conv_b, dtb, alog, dskip, gh_mat, e_mat, qw, kw, g_mat,
                  steps, nb):
    rows = steps * nb
    f = lambda *s: jax.ShapeDtypeStruct(s, F32)
    out_shape = [f(steps, nb, SSM_WIDTH), f(steps, nb, SSM_WIDTH), f(steps, nb, SSM_WIDTH),
                 f(steps, nb, BC_WIDTH), f(steps, nb, BC_WIDTH), f(nb, LANES), f(CONV_WIDTH - 1, nb, CONV_DIM),
                 f(rows, SSM_WIDTH), f(rows, ATTN_WIDTH), f(rows, KV_WIDTH), f(rows, KV_WIDTH), f(rows, ATTN_WIDTH)]
    return pl.pallas_call(_sample_front_body, out_shape=out_shape,
                          compiler_params=pltpu.CompilerParams(vmem_limit_bytes=VMEM_LIMIT),
                          name="sample_front")(
        x2d, norm_w, w_t, sconv3, conv_w, conv_b, dtb, alog, dskip, gh_mat, e_mat, qw, kw, g_mat)


def _ssd_sample_state_block(i, cdec_ref, st_ref, c_ref, b_ref, xw_ref, new_ref, yoff_ref):
    heads_per_group = SSM_HEADS // SSM_GROUPS
    for j in range(SAMPLE_BATCH_BLOCK):
        st = st_ref[j]
        cb_bf = c_ref[:, j, :].astype(BF16)
        bb_bf = b_ref[:, j, :].astype(BF16)
        xw_bf = xw_ref[:, j, :].astype(BF16)
        y_parts = []
        for g in range(SSM_GROUPS):
            rows = slice(g * GROUP_WIDTH, (g + 1) * GROUP_WIDTH)
            ns = slice(g * SSM_STATE, (g + 1) * SSM_STATE)
            y_parts.append(_dot_nt(cb_bf[:, ns], st[rows].astype(BF16)))
            upd = _dot_tn(xw_bf[:, rows], bb_bf[:, ns])
            for hh in range(heads_per_group):
                h = g * heads_per_group + hh
                r = slice(h * SSM_HEAD_DIM, (h + 1) * SSM_HEAD_DIM)
                dec = cdec_ref[(i * SAMPLE_BATCH_BLOCK + j) * SSM_HEADS + h]
                new_ref[j, r, :] = st[r] * dec + upd[hh * SSM_HEAD_DIM:(hh + 1) * SSM_HEAD_DIM]
        yoff_ref[:, j, :] = jnp.concatenate(y_parts, axis=1)


def _sample_back_body(ypart_ref, yoff_ref, ea_ref, gz_ref, nw_ref, ya_ref, x_ref, wo_ref, o_ref):
    y = ypart_ref[...] + yoff_ref[...] * ea_ref[...]
    y_ssm = _group_rmsnorm(y * gz_ref[...], nw_ref[...])
    o_ref[...] = (x_ref[...] + _dot(y_ssm.astype(BF16), wo_ref[:SSM_WIDTH, :])
                  + _dot(ya_ref[...].astype(BF16), wo_ref[SSM_WIDTH:, :]))


def _sample_back(ypart, yoff, ea, gz, norm_w, y_attn, x2d, w_out):
    return pl.pallas_call(_sample_back_body, out_shape=jax.ShapeDtypeStruct(x2d.shape, F32),
                          compiler_params=pltpu.CompilerParams(vmem_limit_bytes=VMEM_LIMIT),
                          name="sample_back")(ypart, yoff, ea, gz, norm_w, y_attn, x2d, w_out)


def _attn_sample_body(q_ref, kn_ref, vn_ref, z_ref, ckt_ref, cvt_ref, biasc_ref, biasn_ref,
                      y_ref, kot_ref, vot_ref):
    steps = q_ref.shape[0]
    bb = SAMPLE_BATCH_BLOCK
    blk = Q_PER_KV * steps
    rows = ATTN_KV_HEADS * blk
    pad = jnp.zeros((SUBLANES - steps, KV_WIDTH), F32)
    lane_head = _lane_head((blk, KV_WIDTH))
    zero = jnp.zeros((blk, KV_WIDTH), F32)

    s_c, s_n, k8, v8 = [], [], [], []
    for j in range(bb):
        q = q_ref[:, j, :]
        qg = jnp.concatenate([q[:, g * KV_WIDTH:(g + 1) * KV_WIDTH] for g in range(Q_PER_KV)], axis=0)
        qx = jnp.concatenate([jnp.where(lane_head == n, qg, zero) for n in range(ATTN_KV_HEADS)], axis=0)
        qx = qx.astype(BF16)
        k8.append(jnp.concatenate([kn_ref[:, j, :], pad], axis=0))
        v8.append(jnp.concatenate([vn_ref[:, j, :], pad], axis=0))
        s_c.append(_dot(qx, ckt_ref[j].astype(BF16)))
        s_n.append(_dot_nt(qx, k8[j].astype(BF16)))
    s_c = jnp.concatenate(s_c, axis=0) + biasc_ref[...]
    s_n = jnp.concatenate(s_n, axis=0) + biasn_ref[...]
    m = jnp.maximum(jnp.max(s_c, axis=-1, keepdims=True), jnp.max(s_n, axis=-1, keepdims=True))
    p_c = jnp.exp(s_c - m)
    p_n = jnp.exp(s_n - m)
    inv = 1.0 / (jnp.sum(p_c, axis=-1, keepdims=True) + jnp.sum(p_n, axis=-1, keepdims=True))
    p_c = (p_c * inv).astype(BF16)
    p_n = (p_n * inv).astype(BF16)

    lane = lax.broadcasted_iota(jnp.int32, (KV_WIDTH, WINDOW), 1)
    for j in range(bb):
        r = slice(j * rows, (j + 1) * rows)
        o = _dot_nt(p_c[r], cvt_ref[j].astype(BF16)) + _dot(p_n[r], v8[j].astype(BF16))
        og = zero
        for n in range(ATTN_KV_HEADS):
            og = og + jnp.where(lane_head == n, o[n * blk:(n + 1) * blk], zero)
        y = jnp.concatenate(
            [og[g * steps:(g + 1) * steps, n * ATTN_HEAD_DIM:(n + 1) * ATTN_HEAD_DIM]
             for n in range(ATTN_KV_HEADS) for g in range(Q_PER_KV)], axis=1)
        y_ref[:, j, :] = y * _silu(z_ref[:, j, :])

        for new8, old_ref, out_ref in ((k8[j], ckt_ref, kot_ref), (v8[j], cvt_ref, vot_ref)):
            tail_rows = jnp.concatenate([new8[steps:], new8[:steps]], axis=0)
            block = jnp.concatenate([jnp.zeros((WINDOW - SUBLANES, KV_WIDTH), F32), tail_rows], axis=0)
            shifted = pltpu.roll(old_ref[j], WINDOW - steps, axis=1)
            out_ref[j] = jnp.where(lane >= WINDOW - steps, block.T, shifted)


def _attn_sample(q3, kn3, vn3, z3, cache_kt, cache_vt, bias_c, bias_n):
    steps, nb = q3.shape[0], q3.shape[1]
    bb = SAMPLE_BATCH_BLOCK
    tok = lambda w: pl.BlockSpec((steps, bb, w), lambda i: (0, i, 0))
    cache_spec = pl.BlockSpec((bb, KV_WIDTH, WINDOW), lambda i: (i, 0, 0))
    return pl.pallas_call(
        _attn_sample_body, grid=(nb // bb,),
        in_specs=[tok(ATTN_WIDTH), tok(KV_WIDTH), tok(KV_WIDTH), tok(ATTN_WIDTH), cache_spec, cache_spec,
                  _const_spec(bias_c.shape), _const_spec(bias_n.shape)],
        out_specs=[tok(ATTN_WIDTH), cache_spec, cache_spec],
        out_shape=[jax.ShapeDtypeStruct((steps, nb, ATTN_WIDTH), F32),
                   jax.ShapeDtypeStruct(cache_kt.shape, F32), jax.ShapeDtypeStruct(cache_vt.shape, F32)],
        compiler_params=_params("parallel"), name="attn_sample")(
            q3, kn3, vn3, z3, cache_kt, cache_vt, bias_c, bias_n)


def _static_tables(steps):
    lanes = np.arange(ATTN_WIDTH)
    g_mat = np.zeros((ATTN_WIDTH, LANES), np.float32)
    g_mat[lanes, lanes // ATTN_HEAD_DIM] = 1.0
    e_mat = g_mat.T.copy()
    bc = np.arange(BC_WIDTH)
    gh_mat = np.zeros((BC_WIDTH, LANES), np.float32)
    for h in range(SSM_HEADS):
        gh_mat[bc // SSM_STATE == h // (SSM_HEADS // SSM_GROUPS), h] = 1.0
    T = CHUNK
    dist = np.arange(T)[:, None] - (np.arange(2 * T) - T)[None, :]
    first = np.broadcast_to((np.arange(2 * T) >= T)[None, :], dist.shape)
    prompt_buckets = np.stack([_bucket_or_masked(dist, first), _bucket_or_masked(dist)])
    dist_c = (np.arange(steps) + WINDOW)[:, None] - np.arange(WINDOW)[None, :]
    dist_n = np.arange(steps)[:, None] - np.arange(SUBLANES)[None, :]
    real = np.broadcast_to((np.arange(SUBLANES) < steps)[None, :], dist_n.shape)
    return dict(g=g_mat, e=e_mat, gh=gh_mat, prompt_buckets=prompt_buckets,
                cache_buckets=_bucket_or_masked(dist_c)[None], new_buckets=_bucket_or_masked(dist_n, real)[None])


def kernel(x_prompt, x_sample, cache_k, cache_v, state_ssm, state_conv, norm_w, w_in, conv_w, conv_b, dt_bias,
           a_log, d_skip, ssm_norm_w, q_norm_w, k_norm_w, sinks, rel_table, w_out):
    assert w_in.shape[0] == 1, "single-layer kernel"
    batch, seq, _ = x_prompt.shape
    nb, steps, _ = x_sample.shape
    tab = _static_tables(steps)
    g_mat = jnp.asarray(tab["g"], BF16)
    e_mat = jnp.asarray(tab["e"], BF16)
    gh_mat = jnp.asarray(tab["gh"], BF16)

    w_t = jnp.transpose(w_in[0]).astype(BF16)
    wo_all = w_out[0].astype(BF16)

    row = lambda v, width: jnp.pad(v.reshape(1, -1), ((0, 0), (0, width - v.size)))
    nw = row(norm_w[0], D_MODEL)
    cw = conv_w[0]
    cb = row(conv_b[0], CONV_DIM)
    dtb = row(dt_bias[0], LANES)
    alog = row(a_log[0], LANES)
    dskip = jnp.repeat(d_skip[0], SSM_HEAD_DIM).reshape(1, SSM_WIDTH)
    snw = row(ssm_norm_w[0], SSM_WIDTH)
    qw = jnp.tile(q_norm_w[0], ATTN_HEADS).reshape(1, ATTN_WIDTH)
    kw = jnp.tile(k_norm_w[0], ATTN_KV_HEADS).reshape(1, KV_WIDTH)
    sink = sinks[0]
    rel_flat = rel_table.reshape(-1)

    xs = jnp.swapaxes(x_sample, 0, 1).reshape(steps * nb, D_MODEL)
    t3 = lambda a: a.reshape(steps, nb, a.shape[-1])
    sconv3 = jnp.swapaxes(state_conv[0], 0, 1)
    ypart, ea, xw, b3, c3, cdec, conv_s3, gz_smp, qn, kn, v_smp, za = _sample_front(
        xs, nw, w_t, sconv3, cw, cb, dtb, alog, dskip, gh_mat, e_mat, qw, kw, g_mat, steps, nb)
    state_in = state_ssm[0].reshape(nb, SSM_WIDTH, SSM_STATE)

    xp = x_prompt.reshape(batch * seq, D_MODEL)
    qw_t = jnp.broadcast_to((qw * (ATTN_SCALE * LOG2E)).reshape(ATTN_WIDTH, 1), (ATTN_WIDTH, PROJ_ROWS))
    kw_t = jnp.broadcast_to(kw.reshape(KV_WIDTH, 1), (KV_WIDTH, PROJ_ROWS))
    gz, xs_p, b_p, c_p, dt_p, q_t, k_t, v_t, gza_t, tail_p, st_s, yoff = _inproj_prompt(
        xp, nw, w_t, cw, cb, dtb, qw_t, kw_t, batch, seq, cdec[:, :SSM_HEADS].reshape(-1), state_in, c3, b3, xw)
    bias_t = _bias_tables_t(rel_flat * LOG2E, jnp.asarray(tab["prompt_buckets"].transpose(0, 2, 1)))
    sink_rows = jnp.repeat(sink.reshape(ATTN_KV_HEADS, Q_PER_KV) * LOG2E, CHUNK, axis=1)
    sink_rows = sink_rows.reshape(ATTN_KV_HEADS, 1, -1)
    y_p, st_p, k_pt, v_pt = _mixer(gz, xs_p, b_p, c_p, dt_p, alog, dskip, snw, e_mat, q_t, k_t, v_t, gza_t,
                                   bias_t, sink_rows, xp, wo_all, batch, seq)
    y_p = y_p.reshape(batch, seq, D_MODEL)
    conv_p = tail_p[:, SUBLANES - (CONV_WIDTH - 1):, :]

    f2 = lambda a: a.reshape(steps * nb, a.shape[-1])
    bias_c = _bias_tables(rel_flat, jnp.asarray(tab["cache_buckets"])).reshape(ATTN_HEADS * steps, WINDOW)
    bias_n = _bias_tables(rel_flat, jnp.asarray(tab["new_buckets"])).reshape(ATTN_HEADS * steps, SUBLANES)
    bias_n = bias_n.at[:, steps].set(jnp.repeat(sink, steps))
    bias_c = jnp.tile(bias_c, (SAMPLE_BATCH_BLOCK, 1))
    bias_n = jnp.tile(bias_n, (SAMPLE_BATCH_BLOCK, 1))
    to_t = lambda a: jnp.transpose(a[0], (0, 2, 3, 1)).reshape(nb, KV_WIDTH, WINDOW)
    from_t = lambda a: jnp.transpose(
        a.reshape(a.shape[0], ATTN_KV_HEADS, ATTN_HEAD_DIM, WINDOW), (0, 3, 1, 2))[None]
    y_attn3, k_st, v_st = _attn_sample(t3(qn), t3(kn), t3(v_smp), t3(za), to_t(cache_k), to_t(cache_v),
                                       bias_c, bias_n)
    k_s, v_s = from_t(k_st), from_t(v_st)
    y_s = _sample_back(f2(ypart), f2(yoff), f2(ea), gz_smp, snw, f2(y_attn3), xs, wo_all)
    y_s = jnp.swapaxes(y_s.reshape(steps, nb, D_MODEL), 0, 1)

    st5 = lambda a: a.reshape(1, a.shape[0], SSM_HEADS, SSM_HEAD_DIM, SSM_STATE)
    return (y_p, y_s, from_t(k_pt), from_t(v_pt), st5(st_p), conv_p[None],
            k_s, v_s, st5(st_s), jnp.swapaxes(conv_s3, 0, 1)[None])
```

```python
import functools
import math

import numpy as np
import jax
import jax.numpy as jnp
from jax import lax
from jax.experimental import pallas as pl
from jax.experimental.pallas import tpu as pltpu

F32 = jnp.float32
BF16 = jnp.bfloat16

D_MODEL = 1024
SSM_HEADS = 16
SSM_HEAD_DIM = 64
SSM_WIDTH = SSM_HEADS * SSM_HEAD_DIM
SSM_GROUPS = 2
SSM_STATE = 128
GROUP_WIDTH = SSM_WIDTH // SSM_GROUPS
BC_WIDTH = SSM_GROUPS * SSM_STATE
CONV_WIDTH = 4
CONV_DIM = SSM_WIDTH + 2 * BC_WIDTH
CHUNK = 128
ATTN_HEADS = 16
ATTN_KV_HEADS = 4
Q_PER_KV = ATTN_HEADS // ATTN_KV_HEADS
ATTN_HEAD_DIM = 64
ATTN_WIDTH = ATTN_HEADS * ATTN_HEAD_DIM
KV_WIDTH = ATTN_KV_HEADS * ATTN_HEAD_DIM
WINDOW = 128
ATTN_SCALE = ATTN_HEAD_DIM ** -0.5
REL_BUCKETS = 32
REL_MAX_DIST = 128
EPS = 1e-6
LOG2E = 1.0 / math.log(2.0)
NEG = -1e30

LANES = 128
SUBLANES = 8
MXU_WIDTH = 256
VMEM_LIMIT = 56 * 1024 * 1024


def _in_proj_rows():
    widths = (("z", SSM_WIDTH), ("xbc", CONV_DIM), ("dt", SSM_HEADS), ("q", ATTN_WIDTH), ("k", KV_WIDTH),
              ("v", KV_WIDTH), ("za", ATTN_WIDTH))
    rows, start = {}, 0
    for name, width in widths:
        rows[name] = slice(start, start + width)
        start += width
    return rows


IN_ROWS = _in_proj_rows()
DT_ROWS = slice(IN_ROWS["dt"].start, IN_ROWS["dt"].start + LANES)

PROJ_ROWS = 512
CHUNKS_PER_STEP = 4
SAMPLE_BATCH_BLOCK = 8


def _dot(a, b):
    return jnp.dot(a, b, preferred_element_type=F32)


def _dot_nt(a, b):
    return lax.dot_general(a, b, (((1,), (1,)), ((), ())), preferred_element_type=F32)


def _dot_tn(a, b):
    return lax.dot_general(a, b, (((0,), (0,)), ((), ())), preferred_element_type=F32)


def _split2(v):
    hi = v.astype(BF16)
    lo = (v - hi.astype(F32)).astype(BF16)
    return hi, lo


def _dot_sel(v, m):
    hi, lo = _split2(v)
    if 2 * v.shape[1] <= MXU_WIDTH:
        return _dot(jnp.concatenate([hi, lo], axis=1), jnp.concatenate([m, m], axis=0))
    return _dot(hi, m) + _dot(lo, m)


def _dot_sel3(m, v):
    hi = v.astype(BF16)
    r1 = v - hi.astype(F32)
    mid = r1.astype(BF16)
    lo = (r1 - mid.astype(F32)).astype(BF16)
    return _dot(m, hi) + _dot(m, mid) + _dot(m, lo)


def _silu(x):
    return x / (1.0 + jnp.exp(-x))


def _softplus(x):
    return jnp.maximum(x, 0.0) + jnp.log1p(jnp.exp(-jnp.abs(x)))


def _params(*sem):
    return pltpu.CompilerParams(dimension_semantics=sem, vmem_limit_bytes=VMEM_LIMIT)


def _const_spec(shape):
    nd = len(shape)
    return pl.BlockSpec(shape, lambda *_: (0,) * nd)


def _normed_input(x_ref, nw_ref):
    x = x_ref[...]
    ms = jnp.mean(x * x, axis=-1, keepdims=True)
    return (x * lax.rsqrt(ms + EPS) * nw_ref[...]).astype(BF16)


def _dt_projection(h, wt_ref):
    raw = _dot_nt(h, wt_ref[DT_ROWS, :])
    return jnp.where(lax.broadcasted_iota(jnp.int32, raw.shape, 1) < SSM_HEADS, raw, 0.0)


def _shift_rows(u, prev_tail, k):
    rows, width = u.shape
    tiles = jnp.concatenate([prev_tail, u], axis=0).reshape(rows // SUBLANES + 1, SUBLANES, width)
    rot = jnp.concatenate([tiles[:, SUBLANES - k:], tiles[:, :SUBLANES - k]], axis=1)
    first = lax.broadcasted_iota(jnp.int32, (1, SUBLANES, width), 1) < k
    return jnp.where(first, rot[:-1], rot[1:]).reshape(rows, width)


def _inproj_prompt_body(steps_per_seq, n_state_blocks, x_ref, nw_ref, wt_ref, cw_ref, cb_ref, dtb_ref,
                        qwt_ref, kwt_ref, cdec_ref, st_ref, sc_ref, sb_ref, sxw_ref, wo_ref,
                        gz_ref, xs_ref, b_ref, c_ref, dt_ref, qt_ref, kt_ref, vt_ref, gzat_ref, tail_ref,
                        newst_ref, yoff_ref, wo16_ref, tail_sc):
    step = pl.program_id(0)

    @pl.when(step < n_state_blocks)
    def _():
        _ssd_sample_state_block(step, cdec_ref, st_ref, sc_ref, sb_ref, sxw_ref, newst_ref, yoff_ref)

    @pl.when(step >= n_state_blocks)
    def _():
        wo16_ref[...] = wo_ref[...].astype(BF16)

    @pl.when(step % steps_per_seq == 0)
    def _():
        tail_sc[...] = jnp.zeros_like(tail_sc)

    h = _normed_input(x_ref, nw_ref)
    rows = h.shape[0]
    w_tile = lambda name, j: wt_ref[IN_ROWS[name].start + j * MXU_WIDTH:IN_ROWS[name].start + (j + 1) * MXU_WIDTH, :]
    n_side = SSM_WIDTH // MXU_WIDTH
    for j in range(CONV_DIM // MXU_WIDTH):
        cols = slice(j * MXU_WIDTH, (j + 1) * MXU_WIDTH)
        if j < n_side:
            both = _dot_nt(h, jnp.concatenate([w_tile("xbc", j), w_tile("z", j)], axis=0))
            u = both[:, :MXU_WIDTH]
            gz_ref[:, cols] = _silu(both[:, MXU_WIDTH:])
        else:
            u = _dot_nt(h, w_tile("xbc", j))
        prev_tail = tail_sc[:, cols]
        conv = cb_ref[:, cols] + u * cw_ref[CONV_WIDTH - 1:CONV_WIDTH, cols]
        for k in range(1, CONV_WIDTH):
            tap = CONV_WIDTH - 1 - k
            conv = conv + _shift_rows(u, prev_tail, k) * cw_ref[tap:tap + 1, cols]
        new_tail = u[rows - SUBLANES:, :]
        tail_sc[:, cols] = new_tail
        tail_ref[0, :, cols] = new_tail
        act = _silu(conv)
        if j < SSM_WIDTH // MXU_WIDTH:
            xs_ref[:, cols] = act
        elif j == SSM_WIDTH // MXU_WIDTH:
            b_ref[...] = act.astype(BF16)
        else:
            c_ref[...] = act.astype(BF16)

    for j in range(ATTN_WIDTH // MXU_WIDTH):
        feats = slice(j * MXU_WIDTH, (j + 1) * MXU_WIDTH)
        gzat_ref[feats, :] = _silu(_dot_nt(w_tile("za", j), h))
    dt_ref[...] = _softplus(_dt_projection(h, wt_ref) + dtb_ref[...])

    def head_norm(t, w):
        x3 = t.reshape(-1, ATTN_HEAD_DIM, rows)
        ms = jnp.mean(x3 * x3, axis=1, keepdims=True)
        return (x3 * lax.rsqrt(ms + EPS)).reshape(t.shape) * w

    for j in range(ATTN_WIDTH // MXU_WIDTH):
        feats = slice(j * MXU_WIDTH, (j + 1) * MXU_WIDTH)
        qt_ref[feats, :] = head_norm(_dot_nt(w_tile("q", j), h), qwt_ref[feats, :]).astype(BF16)
    kt_ref[...] = head_norm(_dot_nt(wt_ref[IN_ROWS["k"], :], h), kwt_ref[...])
    vt_ref[...] = _dot_nt(wt_ref[IN_ROWS["v"], :], h)


def _inproj_prompt(x2d, norm_w, w_t, conv_w, conv_b, dtb, qw_t, kw_t, batch, seq, cdec_flat, state, c3, b3, xw3,
                   w_out):
    rows = x2d.shape[0]
    tm = PROJ_ROWS
    steps_per_seq = seq // tm
    steps, nb = c3.shape[0], c3.shape[1]
    bb = SAMPLE_BATCH_BLOCK
    n_state_blocks = nb // bb
    n_steps = rows // tm
    assert n_state_blocks < n_steps, "state blocks and weight-cast blocks share the in-proj grid steps"
    last = n_state_blocks - 1
    cast_rows = w_out.shape[0] // (n_steps - n_state_blocks)
    assert cast_rows * (n_steps - n_state_blocks) == w_out.shape[0] and cast_rows % (2 * SUBLANES) == 0
    wo_spec = pl.BlockSpec((cast_rows, D_MODEL), lambda i: (jnp.maximum(i - n_state_blocks, 0), 0))
    resident = lambda a: pl.BlockSpec(a.shape, lambda i: (0, 0), pipeline_mode=pl.Buffered(1))
    rowblk = lambda w: pl.BlockSpec((tm, w), lambda i: (i, 0))
    colblk = pl.BlockSpec((ATTN_WIDTH, tm), lambda i: (0, i))
    kvblk = pl.BlockSpec((KV_WIDTH, tm), lambda i: (0, i))
    tok = lambda w: pl.BlockSpec((steps, bb, w), lambda i: (0, jnp.minimum(i, last), 0))
    st_spec = pl.BlockSpec((bb, SSM_WIDTH, SSM_STATE), lambda i: (jnp.minimum(i, last), 0, 0))
    in_specs = ([rowblk(D_MODEL), _const_spec((1, D_MODEL)), resident(w_t)]
                + [_const_spec(conv_w.shape), _const_spec(conv_b.shape), _const_spec(dtb.shape)]
                + [resident(qw_t), resident(kw_t)]
                + [pl.BlockSpec(memory_space=pltpu.SMEM), st_spec, tok(BC_WIDTH), tok(BC_WIDTH), tok(SSM_WIDTH),
                   wo_spec])
    out_specs = [rowblk(SSM_WIDTH), rowblk(SSM_WIDTH), rowblk(BC_WIDTH), rowblk(BC_WIDTH), rowblk(LANES),
                 colblk, kvblk, kvblk, colblk,
                 pl.BlockSpec((1, SUBLANES, CONV_DIM), lambda i: (i // steps_per_seq, 0, 0)),
                 st_spec, tok(SSM_WIDTH), wo_spec]
    f = lambda r, c, dt=F32: jax.ShapeDtypeStruct((r, c), dt)
    out_shape = [f(rows, SSM_WIDTH), f(rows, SSM_WIDTH), f(rows, BC_WIDTH, BF16), f(rows, BC_WIDTH, BF16),
                 f(rows, LANES), f(ATTN_WIDTH, rows, BF16), f(KV_WIDTH, rows), f(KV_WIDTH, rows), f(ATTN_WIDTH, rows),
                 jax.ShapeDtypeStruct((batch, SUBLANES, CONV_DIM), F32),
                 jax.ShapeDtypeStruct(state.shape, F32), jax.ShapeDtypeStruct((steps, nb, SSM_WIDTH), F32),
                 jax.ShapeDtypeStruct(w_out.shape, BF16)]
    return pl.pallas_call(
        functools.partial(_inproj_prompt_body, steps_per_seq, n_state_blocks), grid=(n_steps,),
        in_specs=in_specs, out_specs=out_specs, out_shape=out_shape,
        scratch_shapes=[pltpu.VMEM((SUBLANES, CONV_DIM), F32)],
        compiler_params=_params("arbitrary"), name="inproj_prompt")(
            x2d, norm_w, w_t, conv_w, conv_b, dtb, qw_t, kw_t, cdec_flat, state, c3, b3, xw3, w_out)


def _group_rmsnorm(gy, norm_w):
    parts = []
    for g in range(SSM_GROUPS):
        blk = gy[:, g * GROUP_WIDTH:(g + 1) * GROUP_WIDTH]
        ms = jnp.mean(blk * blk, axis=-1, keepdims=True)
        parts.append(blk * lax.rsqrt(ms + EPS))
    return jnp.concatenate(parts, axis=1) * norm_w


def _ssd_chunk(gz, xs, b_bf, c_bf, dt, a_neg, dskip, norm_w, e_mat, state):
    xs_bf = xs.astype(BF16)

    a = dt * a_neg
    li = lax.broadcasted_iota(jnp.int32, (CHUNK, CHUNK), 0)
    si = lax.broadcasted_iota(jnp.int32, (CHUNK, CHUNK), 1)
    causal = li >= si
    a_cum = _dot_sel3(jnp.where(causal, 1.0, 0.0).astype(BF16), a)
    a2 = a_cum * LOG2E
    row_term = a2.T - jnp.log2(dt.T)
    ea_full = _dot_sel(jnp.exp(a_cum), e_mat)
    w_full = _dot((dt * jnp.exp(a_cum[CHUNK - 1:CHUNK, :] - a_cum)).astype(BF16), e_mat)

    cb = [_dot_nt(c_bf[:, g * SSM_STATE:(g + 1) * SSM_STATE], b_bf[:, g * SSM_STATE:(g + 1) * SSM_STATE])
          for g in range(SSM_GROUPS)]
    half = lax.broadcasted_iota(jnp.int32, (CHUNK, LANES), 1) < SSM_HEAD_DIM
    heads_per_group = SSM_HEADS // SSM_GROUPS
    y_parts = []
    for pair in range(SSM_HEADS // 2):
        blocks = []
        for h in (2 * pair, 2 * pair + 1):
            seg = a2[:, h:h + 1] - row_term[h:h + 1, :]
            decay_dt = jnp.exp2(jnp.where(causal, seg, -jnp.inf))
            blocks.append((cb[h // heads_per_group] * decay_dt).astype(BF16))
        lhs = jnp.concatenate(blocks, axis=1)
        xp = xs_bf[:, pair * LANES:(pair + 1) * LANES]
        zero = jnp.zeros_like(xp)
        rhs = jnp.concatenate([jnp.where(half, xp, zero), jnp.where(half, zero, xp)], axis=0)
        y_parts.append(_dot(lhs, rhs))
    y_diag = jnp.concatenate(y_parts, axis=1)

    state_bf = state.astype(BF16)
    xw_bf = (xs * w_full).astype(BF16)
    y_off, upd = [], []
    for g in range(SSM_GROUPS):
        cols = slice(g * GROUP_WIDTH, (g + 1) * GROUP_WIDTH)
        ns = slice(g * SSM_STATE, (g + 1) * SSM_STATE)
        y_off.append(_dot(c_bf[:, ns], state_bf[:, cols]))
        upd.append(_dot_tn(b_bf[:, ns], xw_bf[:, cols]))
    y = y_diag + jnp.concatenate(y_off, axis=1) * ea_full + dskip * xs
    new_state = state * ea_full[CHUNK - 1:CHUNK, :] + jnp.concatenate(upd, axis=1)
    return _group_rmsnorm(y * gz, norm_w), new_state


def _rel_bucket_np(dist):
    max_exact = REL_BUCKETS // 2
    d_f = np.maximum(dist, 1).astype(np.float32)
    large = max_exact + (np.log(d_f / np.float32(max_exact)) / np.float32(math.log(REL_MAX_DIST / max_exact))
                         * np.float32(REL_BUCKETS - max_exact)).astype(np.int32)
    return np.where(dist < max_exact, dist, np.minimum(large, REL_BUCKETS - 1)).astype(np.int32)


def _bucket_or_masked(dist, extra_mask=None):
    ok = (dist >= 0) & (dist <= WINDOW)
    if extra_mask is not None:
        ok = ok & extra_mask
    return np.where(ok, _rel_bucket_np(np.clip(dist, 0, WINDOW)), -1).astype(np.int32)


def _bias_body(rel_ref, bucket_ref, o_ref):
    bucket = bucket_ref[0]

    def per_head(h, carry):
        acc = jnp.full(bucket.shape, NEG, F32)
        for bkt in range(REL_BUCKETS):
            acc = jnp.where(bucket == bkt, rel_ref[bkt * ATTN_HEADS + h], acc)
        o_ref[0, h] = acc
        return carry

    lax.fori_loop(0, ATTN_HEADS, per_head, 0)


def _bias_tables(rel_flat, buckets):
    nv, lq, lk = buckets.shape
    return pl.pallas_call(
        _bias_body, grid=(nv,),
        in_specs=[pl.BlockSpec(memory_space=pltpu.SMEM), pl.BlockSpec((1, lq, lk), lambda v: (v, 0, 0))],
        out_specs=pl.BlockSpec((1, ATTN_HEADS, lq, lk), lambda v: (v, 0, 0, 0)),
        out_shape=jax.ShapeDtypeStruct((nv, ATTN_HEADS, lq, lk), F32),
        compiler_params=_params("arbitrary"), name="rel_bias")(rel_flat, buckets)


def _bias_t_body(rel_ref, bucket_ref, o_ref):
    variants = [bucket_ref[v] for v in range(bucket_ref.shape[0])]
    union = functools.reduce(jnp.maximum, variants)
    lq = union.shape[1]

    def per_kv_head(n, carry):
        for g in range(Q_PER_KV):
            acc = jnp.full(union.shape, NEG, F32)
            for bkt in range(REL_BUCKETS):
                acc = jnp.where(union == bkt, rel_ref[bkt * ATTN_HEADS + n * Q_PER_KV + g], acc)
            for v, bucket in enumerate(variants):
                o_ref[v, n, :, g * lq:(g + 1) * lq] = jnp.where(bucket >= 0, acc, NEG)
        return carry

    lax.fori_loop(0, ATTN_KV_HEADS, per_kv_head, 0)


def _bias_tables_t(rel_flat, buckets_t):
    nv, lk, lq = buckets_t.shape
    out_dims = (nv, ATTN_KV_HEADS, lk, Q_PER_KV * lq)
    return pl.pallas_call(
        _bias_t_body,
        in_specs=[pl.BlockSpec(memory_space=pltpu.SMEM), pl.BlockSpec(memory_space=pltpu.VMEM)],
        out_specs=pl.BlockSpec(memory_space=pltpu.VMEM),
        out_shape=jax.ShapeDtypeStruct(out_dims, F32),
        compiler_params=pltpu.CompilerParams(vmem_limit_bytes=VMEM_LIMIT), name="rel_bias_t")(rel_flat, buckets_t)


def _head_rmsnorm(x, g_mat, e_mat, w):
    ms = _dot_sel(x * x, g_mat) * (1.0 / ATTN_HEAD_DIM)
    return x * _dot_sel(lax.rsqrt(ms + EPS), e_mat) * w


def _lane_head(shape):
    return lax.broadcasted_iota(jnp.int32, shape, 1) // ATTN_HEAD_DIM


def _attn_block(q_blk, kcat, vcat_t, bias_at, sink_ref):
    T = CHUNK
    lane_head = _lane_head((2 * T, KV_WIDTH))
    zero = jnp.zeros((2 * T, KV_WIDTH), BF16)
    head = lambda h: q_blk[h * ATTN_HEAD_DIM:(h + 1) * ATTN_HEAD_DIM]
    q_cols = jnp.concatenate(
        [jnp.concatenate([head(n * Q_PER_KV + g) for n in range(ATTN_KV_HEADS)], axis=0)
         for g in range(Q_PER_KV)], axis=1)
    row_head = lax.broadcasted_iota(jnp.int32, (KV_WIDTH, 2 * T), 0) // ATTN_HEAD_DIM
    probs, vals, inv = [], [], {}
    for n in range(ATTN_KV_HEADS):
        s = _dot(jnp.where(lane_head == n, kcat, zero), q_cols)
        sink = sink_ref[n]
        cols = []
        for g in range(Q_PER_KV):
            c = slice(g * T, (g + 1) * T)
            sg = s[:, c] + bias_at(n, c)
            m = jnp.maximum(jnp.max(sg, axis=0, keepdims=True), sink[:, c])
            p = jnp.exp2(sg - m)
            inv[n, g] = 1.0 / (jnp.sum(p, axis=0, keepdims=True) + jnp.exp2(sink[:, c] - m))
            cols.append(p.astype(BF16))
        probs.append(jnp.concatenate(cols, axis=1))
        vals.append(jnp.where(row_head == n, vcat_t, zero.T))
    o_t = _dot(jnp.concatenate(vals, axis=1), jnp.concatenate(probs, axis=0))
    return jnp.concatenate(
        [o_t[n * ATTN_HEAD_DIM:(n + 1) * ATTN_HEAD_DIM, g * T:(g + 1) * T] * inv[n, g]
         for n in range(ATTN_KV_HEADS) for g in range(Q_PER_KV)], axis=0)


def _mixer_body(n_sample_blocks, gz_ref, xs_ref, b_ref, c_ref, dt_ref, alog_ref, dskip_ref, nw_ref, e_ref,
                qt_ref, kt_ref, vt_ref, gzt_ref, bias_ref, sink_ref, x_ref, wo_ref,
                sq_ref, skn_ref, svn_ref, sz_ref, ckt_ref, cvt_ref, sbc_ref, sbn_ref,
                y_ref, st_ref, knt_ref, vnt_ref, sy_ref, kot_ref, vot_ref,
                state_sc, kcat_sc, vcat_t_sc, yssm_sc, yattn_t_sc):
    T = CHUNK
    step = pl.program_id(1)
    cols_step = CHUNKS_PER_STEP * T

    @pl.when(pl.program_id(0) * pl.num_programs(1) + step < n_sample_blocks)
    def _():
        _attn_sample_body(sq_ref, skn_ref, svn_ref, sz_ref, ckt_ref, cvt_ref, sbc_ref, sbn_ref,
                          sy_ref, kot_ref, vot_ref)

    @pl.when(step == 0)
    def _():
        state_sc[...] = jnp.zeros_like(state_sc)
        kcat_sc[0:T, :] = jnp.zeros((T, KV_WIDTH), BF16)
        vcat_t_sc[:, 0:T] = jnp.zeros((KV_WIDTH, T), BF16)

    qn = qt_ref[...]
    kn_t = kt_ref[...]
    v_t = vt_ref[...]
    knt_ref[0] = kn_t[:, cols_step - T:]
    vnt_ref[0] = v_t[:, cols_step - T:]
    kcat_sc[T:, :] = kn_t.T.astype(BF16)
    vcat_t_sc[:, T:] = v_t.astype(BF16)
    e_mat = e_ref[...]

    a_neg = -jnp.exp(alog_ref[...])
    state = state_sc[...]
    first_variant = jnp.minimum(step, 1)
    for j in range(CHUNKS_PER_STEP):
        r = slice(j * T, (j + 1) * T)
        y, state = _ssd_chunk(gz_ref[r, :], xs_ref[r, :], b_ref[r, :], c_ref[r, :], dt_ref[r, :], a_neg,
                              dskip_ref[...], nw_ref[...], e_mat, state)
        yssm_sc[r, :] = y.astype(BF16)
        variant = first_variant if j == 0 else 1
        y_t = _attn_block(qn[:, r], kcat_sc[j * T:(j + 2) * T, :], vcat_t_sc[:, j * T:(j + 2) * T],
                          lambda n, c, variant=variant: bias_ref[variant, n, :, c], sink_ref)
        yattn_t_sc[:, r] = (y_t * gzt_ref[:, r]).astype(BF16)
    state_sc[...] = state
    kcat_sc[0:T, :] = kcat_sc[cols_step:, :]
    vcat_t_sc[:, 0:T] = vcat_t_sc[:, cols_step:]

    y_ref[...] = (x_ref[...] + _dot(yssm_sc[...], wo_ref[:SSM_WIDTH, :])
                  + _dot_tn(yattn_t_sc[...], wo_ref[SSM_WIDTH:, :]))

    @pl.when(step == pl.num_programs(1) - 1)
    def _():
        st_ref[0] = state.T


def _mixer(gz, xs, b, c, dt, alog, dskip, norm_w, e_mat, q_t, k_t, v_t, gz_t, bias_t, sink_rows,
           x2d, w_out, batch, seq, q3, kn3, vn3, z3, cache_kt, cache_vt, bias_c, bias_n):
    step_rows = CHUNKS_PER_STEP * CHUNK
    ns = seq // step_rows
    steps, nb = q3.shape[0], q3.shape[1]
    bb = SAMPLE_BATCH_BLOCK
    n_sample_blocks = nb // bb
    assert n_sample_blocks <= batch * ns, "decode attention blocks ride on the mixer grid steps"
    blk = lambda b, i: jnp.minimum(b * ns + i, n_sample_blocks - 1)
    row = lambda w: pl.BlockSpec((step_rows, w), lambda b, i: (b * ns + i, 0))
    col = lambda w: pl.BlockSpec((w, step_rows), lambda b, i: (0, b * ns + i))
    resident = lambda a: pl.BlockSpec(a.shape, lambda b, i: (0,) * a.ndim, pipeline_mode=pl.Buffered(1))
    tok = lambda w: pl.BlockSpec((steps, bb, w), lambda b, i: (0, blk(b, i), 0))
    cache_spec = pl.BlockSpec((bb, KV_WIDTH, WINDOW), lambda b, i: (blk(b, i), 0, 0))
    in_specs = [row(SSM_WIDTH), row(SSM_WIDTH), row(BC_WIDTH), row(BC_WIDTH), row(LANES),
                _const_spec((1, LANES)), _const_spec((1, SSM_WIDTH)), _const_spec((1, SSM_WIDTH)), resident(e_mat),
                col(ATTN_WIDTH), col(KV_WIDTH), col(KV_WIDTH), col(ATTN_WIDTH),
                resident(bias_t), _const_spec(sink_rows.shape), row(D_MODEL), resident(w_out),
                tok(ATTN_WIDTH), tok(KV_WIDTH), tok(KV_WIDTH), tok(ATTN_WIDTH), cache_spec, cache_spec,
                resident(bias_c), resident(bias_n)]
    kv_out = pl.BlockSpec((1, KV_WIDTH, CHUNK), lambda b, i: (b, 0, 0))
    out_specs = [row(D_MODEL), pl.BlockSpec((1, SSM_WIDTH, SSM_STATE), lambda b, i: (b, 0, 0)), kv_out, kv_out,
                 tok(ATTN_WIDTH), cache_spec, cache_spec]
    out_shape = [jax.ShapeDtypeStruct((batch * seq, D_MODEL), F32),
                 jax.ShapeDtypeStruct((batch, SSM_WIDTH, SSM_STATE), F32),
                 jax.ShapeDtypeStruct((batch, KV_WIDTH, CHUNK), F32),
                 jax.ShapeDtypeStruct((batch, KV_WIDTH, CHUNK), F32),
                 jax.ShapeDtypeStruct((steps, nb, ATTN_WIDTH), F32),
                 jax.ShapeDtypeStruct(cache_kt.shape, F32), jax.ShapeDtypeStruct(cache_vt.shape, F32)]
    scratch = [pltpu.VMEM((SSM_STATE, SSM_WIDTH), F32),
               pltpu.VMEM((step_rows + CHUNK, KV_WIDTH), BF16), pltpu.VMEM((KV_WIDTH, step_rows + CHUNK), BF16),
               pltpu.VMEM((step_rows, SSM_WIDTH), BF16), pltpu.VMEM((ATTN_WIDTH, step_rows), BF16)]
    return pl.pallas_call(
        functools.partial(_mixer_body, n_sample_blocks), grid=(batch, ns), in_specs=in_specs, out_specs=out_specs,
        out_shape=out_shape, scratch_shapes=scratch, compiler_params=_params("arbitrary", "arbitrary"),
        name="mixer")(
            gz, xs, b, c, dt, alog, dskip, norm_w, e_mat, q_t, k_t, v_t, gz_t, bias_t, sink_rows, x2d, w_out,
            q3, kn3, vn3, z3, cache_kt, cache_vt, bias_c, bias_n)


def _sample_front_body(x_ref, nw_ref, wt_ref, sconv_ref, cw_ref, cb_ref, dtb_ref, alog_ref, dskip_ref,
                       gh_ref, e_ref, qw_ref, kw_ref, g_ref,
                       ypart_ref, ea_ref, xw_ref, b_ref, c_ref, cdec_ref, convnew_ref,
                       gz_ref, qn_ref, kn_ref, v_ref, za_ref):
    steps, nb = ypart_ref.shape[0], ypart_ref.shape[1]
    tail = CONV_WIDTH - 1
    h = _normed_input(x_ref, nw_ref)
    proj = lambda name: _dot_nt(h, wt_ref[IN_ROWS[name], :])
    slab = lambda a, l: a[l * nb:(l + 1) * nb]
    xbc = proj("xbc")
    dt_raw = _dt_projection(h, wt_ref)
    gz_ref[...] = _silu(proj("z"))
    v_ref[...] = proj("v")
    za_ref[...] = proj("za")
    g_mat = g_ref[...]
    e_mat = e_ref[...]
    qn = _head_rmsnorm(proj("q"), g_mat, e_mat, qw_ref[...]) * ATTN_SCALE
    head = lambda hd: qn[:, hd * ATTN_HEAD_DIM:(hd + 1) * ATTN_HEAD_DIM]
    qn_ref[...] = jnp.concatenate(
        [head(n * Q_PER_KV + g) for g in range(Q_PER_KV) for n in range(ATTN_KV_HEADS)], axis=1)
    kn_ref[...] = _head_rmsnorm(proj("k"), g_mat[:KV_WIDTH], e_mat[:, :KV_WIDTH], kw_ref[...])

    full = [sconv_ref[j] for j in range(tail)] + [slab(xbc, l) for l in range(steps)]
    for j in range(tail):
        convnew_ref[j] = full[steps + j]
    gh = gh_ref[...]
    a_neg = -jnp.exp(alog_ref[...])
    xs, bm, cm, dts, acum = [], [], [], [], []
    run = None
    for l in range(steps):
        conv = cb_ref[...]
        for tap in range(CONV_WIDTH):
            conv = conv + full[l + tap] * cw_ref[tap:tap + 1, :]
        act = _silu(conv)
        xs.append(act[:, :SSM_WIDTH])
        bm.append(act[:, SSM_WIDTH:SSM_WIDTH + BC_WIDTH])
        cm.append(act[:, SSM_WIDTH + BC_WIDTH:])
        d = _softplus(slab(dt_raw, l) + dtb_ref[...])
        dts.append(d)
        run = d * a_neg if run is None else run + d * a_neg
        acum.append(run)
        b_ref[l] = bm[l]
        c_ref[l] = cm[l]
    for l in range(steps):
        y = dskip_ref[...] * xs[l]
        for s in range(l + 1):
            cb_h = _dot_sel(cm[l] * bm[s], gh)
            coef = cb_h * jnp.exp(acum[l] - acum[s]) * dts[s]
            y = y + _dot_sel(coef, e_mat) * xs[s]
        ypart_ref[l] = y
        ea_ref[l] = _dot_sel(jnp.exp(acum[l]), e_mat)
        xw_ref[l] = xs[l] * _dot_sel(dts[l] * jnp.exp(acum[steps - 1] - acum[l]), e_mat)
    cdec_ref[...] = jnp.exp(acum[steps - 1])


def _sample_front(x2d, norm_w, w_t, sconv3, conv_w, conv_b, dtb, alog, dskip, gh_mat, e_mat, qw, kw, g_mat,
                  steps, nb):
    rows = steps * nb
    f = lambda *s: jax.ShapeDtypeStruct(s, F32)
    out_shape = [f(steps, nb, SSM_WIDTH), f(steps, nb, SSM_WIDTH), f(steps, nb, SSM_WIDTH),
                 f(steps, nb, BC_WIDTH), f(steps, nb, BC_WIDTH), f(nb, LANES), f(CONV_WIDTH - 1, nb, CONV_DIM),
                 f(rows, SSM_WIDTH), f(rows, ATTN_WIDTH), f(rows, KV_WIDTH), f(rows, KV_WIDTH), f(rows, ATTN_WIDTH)]
    return pl.pallas_call(_sample_front_body, out_shape=out_shape,
                          compiler_params=pltpu.CompilerParams(vmem_limit_bytes=VMEM_LIMIT),
                          name="sample_front")(
        x2d, norm_w, w_t, sconv3, conv_w, conv_b, dtb, alog, dskip, gh_mat, e_mat, qw, kw, g_mat)


def _ssd_sample_state_block(i, cdec_ref, st_ref, c_ref, b_ref, xw_ref, new_ref, yoff_ref):
    heads_per_group = SSM_HEADS // SSM_GROUPS
    for j in range(SAMPLE_BATCH_BLOCK):
        st = st_ref[j]
        cb_bf = c_ref[:, j, :].astype(BF16)
        bb_bf = b_ref[:, j, :].astype(BF16)
        xw_bf = xw_ref[:, j, :].astype(BF16)
        y_parts = []
        for g in range(SSM_GROUPS):
            rows = slice(g * GROUP_WIDTH, (g + 1) * GROUP_WIDTH)
            ns = slice(g * SSM_STATE, (g + 1) * SSM_STATE)
            y_parts.append(_dot_nt(cb_bf[:, ns], st[rows].astype(BF16)))
            upd = _dot_tn(xw_bf[:, rows], bb_bf[:, ns])
            for hh in range(heads_per_group):
                h = g * heads_per_group + hh
                r = slice(h * SSM_HEAD_DIM, (h + 1) * SSM_HEAD_DIM)
                dec = cdec_ref[(i * SAMPLE_BATCH_BLOCK + j) * SSM_HEADS + h]
                new_ref[j, r, :] = st[r] * dec + upd[hh * SSM_HEAD_DIM:(hh + 1) * SSM_HEAD_DIM]
        yoff_ref[:, j, :] = jnp.concatenate(y_parts, axis=1)


def _sample_back_body(ypart_ref, yoff_ref, ea_ref, gz_ref, nw_ref, ya_ref, x_ref, wo_ref, o_ref):
    y = ypart_ref[...] + yoff_ref[...] * ea_ref[...]
    y_ssm = _group_rmsnorm(y * gz_ref[...], nw_ref[...])
    o_ref[...] = (x_ref[...] + _dot(y_ssm.astype(BF16), wo_ref[:SSM_WIDTH, :])
                  + _dot(ya_ref[...].astype(BF16), wo_ref[SSM_WIDTH:, :]))


def _sample_back(ypart, yoff, ea, gz, norm_w, y_attn, x2d, w_out):
    return pl.pallas_call(_sample_back_body, out_shape=jax.ShapeDtypeStruct(x2d.shape, F32),
                          compiler_params=pltpu.CompilerParams(vmem_limit_bytes=VMEM_LIMIT),
                          name="sample_back")(ypart, yoff, ea, gz, norm_w, y_attn, x2d, w_out)


def _attn_sample_body(q_ref, kn_ref, vn_ref, z_ref, ckt_ref, cvt_ref, biasc_ref, biasn_ref,
                      y_ref, kot_ref, vot_ref):
    steps = q_ref.shape[0]
    bb = SAMPLE_BATCH_BLOCK
    blk = Q_PER_KV * steps
    rows = ATTN_KV_HEADS * blk
    pad = jnp.zeros((SUBLANES - steps, KV_WIDTH), F32)
    lane_head = _lane_head((blk, KV_WIDTH))
    zero = jnp.zeros((blk, KV_WIDTH), F32)

    s_c, s_n, k8, v8 = [], [], [], []
    for j in range(bb):
        q = q_ref[:, j, :]
        qg = jnp.concatenate([q[:, g * KV_WIDTH:(g + 1) * KV_WIDTH] for g in range(Q_PER_KV)], axis=0)
        qx = jnp.concatenate([jnp.where(lane_head == n, qg, zero) for n in range(ATTN_KV_HEADS)], axis=0)
        qx = qx.astype(BF16)
        k8.append(jnp.concatenate([kn_ref[:, j, :], pad], axis=0))
        v8.append(jnp.concatenate([vn_ref[:, j, :], pad], axis=0))
        s_c.append(_dot(qx, ckt_ref[j].astype(BF16)))
        s_n.append(_dot_nt(qx, k8[j].astype(BF16)))
    s_c = jnp.concatenate(s_c, axis=0) + biasc_ref[...]
    s_n = jnp.concatenate(s_n, axis=0) + biasn_ref[...]
    m = jnp.maximum(jnp.max(s_c, axis=-1, keepdims=True), jnp.max(s_n, axis=-1, keepdims=True))
    p_c = jnp.exp(s_c - m)
    p_n = jnp.exp(s_n - m)
    inv = 1.0 / (jnp.sum(p_c, axis=-1, keepdims=True) + jnp.sum(p_n, axis=-1, keepdims=True))
    p_c = (p_c * inv).astype(BF16)
    p_n = (p_n * inv).astype(BF16)

    lane = lax.broadcasted_iota(jnp.int32, (KV_WIDTH, WINDOW), 1)
    for j in range(bb):
        r = slice(j * rows, (j + 1) * rows)
        o = _dot_nt(p_c[r], cvt_ref[j].astype(BF16)) + _dot(p_n[r], v8[j].astype(BF16))
        og = zero
        for n in range(ATTN_KV_HEADS):
            og = og + jnp.where(lane_head == n, o[n * blk:(n + 1) * blk], zero)
        y = jnp.concatenate(
            [og[g * steps:(g + 1) * steps, n * ATTN_HEAD_DIM:(n + 1) * ATTN_HEAD_DIM]
             for n in range(ATTN_KV_HEADS) for g in range(Q_PER_KV)], axis=1)
        y_ref[:, j, :] = y * _silu(z_ref[:, j, :])

        for new8, old_ref, out_ref in ((k8[j], ckt_ref, kot_ref), (v8[j], cvt_ref, vot_ref)):
            tail_rows = jnp.concatenate([new8[steps:], new8[:steps]], axis=0)
            block = jnp.concatenate([jnp.zeros((WINDOW - SUBLANES, KV_WIDTH), F32), tail_rows], axis=0)
            shifted = pltpu.roll(old_ref[j], WINDOW - steps, axis=1)
            out_ref[j] = jnp.where(lane >= WINDOW - steps, block.T, shifted)


def _static_tables(steps):
    lanes = np.arange(ATTN_WIDTH)
    g_mat = np.zeros((ATTN_WIDTH, LANES), np.float32)
    g_mat[lanes, lanes // ATTN_HEAD_DIM] = 1.0
    e_mat = g_mat.T.copy()
    bc = np.arange(BC_WIDTH)
    gh_mat = np.zeros((BC_WIDTH, LANES), np.float32)
    for h in range(SSM_HEADS):
        gh_mat[bc // SSM_STATE == h // (SSM_HEADS // SSM_GROUPS), h] = 1.0
    T = CHUNK
    dist = np.arange(T)[:, None] - (np.arange(2 * T) - T)[None, :]
    first = np.broadcast_to((np.arange(2 * T) >= T)[None, :], dist.shape)
    prompt_buckets = np.stack([_bucket_or_masked(dist, first), _bucket_or_masked(dist)])
    dist_c = (np.arange(steps) + WINDOW)[:, None] - np.arange(WINDOW)[None, :]
    dist_n = np.arange(steps)[:, None] - np.arange(SUBLANES)[None, :]
    real = np.broadcast_to((np.arange(SUBLANES) < steps)[None, :], dist_n.shape)
    return dict(g=g_mat, e=e_mat, gh=gh_mat, prompt_buckets=prompt_buckets,
                cache_buckets=_bucket_or_masked(dist_c)[None], new_buckets=_bucket_or_masked(dist_n, real)[None])


def kernel(x_prompt, x_sample, cache_k, cache_v, state_ssm, state_conv, norm_w, w_in, conv_w, conv_b, dt_bias,
           a_log, d_skip, ssm_norm_w, q_norm_w, k_norm_w, sinks, rel_table, w_out):
    assert w_in.shape[0] == 1, "single-layer kernel"
    batch, seq, _ = x_prompt.shape
    nb, steps, _ = x_sample.shape
    tab = _static_tables(steps)
    g_mat = jnp.asarray(tab["g"], BF16)
    e_mat = jnp.asarray(tab["e"], BF16)
    gh_mat = jnp.asarray(tab["gh"], BF16)

    w_t = jnp.transpose(w_in[0]).astype(BF16)

    row = lambda v, width: jnp.pad(v.reshape(1, -1), ((0, 0), (0, width - v.size)))
    nw = row(norm_w[0], D_MODEL)
    cw = conv_w[0]
    cb = row(conv_b[0], CONV_DIM)
    dtb = row(dt_bias[0], LANES)
    alog = row(a_log[0], LANES)
    dskip = jnp.repeat(d_skip[0], SSM_HEAD_DIM).reshape(1, SSM_WIDTH)
    snw = row(ssm_norm_w[0], SSM_WIDTH)
    qw = jnp.tile(q_norm_w[0], ATTN_HEADS).reshape(1, ATTN_WIDTH)
    kw = jnp.tile(k_norm_w[0], ATTN_KV_HEADS).reshape(1, KV_WIDTH)
    sink = sinks[0]
    rel_flat = rel_table.reshape(-1)

    xs = jnp.swapaxes(x_sample, 0, 1).reshape(steps * nb, D_MODEL)
    t3 = lambda a: a.reshape(steps, nb, a.shape[-1])
    sconv3 = jnp.swapaxes(state_conv[0], 0, 1)
    ypart, ea, xw, b3, c3, cdec, conv_s3, gz_smp, qn, kn, v_smp, za = _sample_front(
        xs, nw, w_t, sconv3, cw, cb, dtb, alog, dskip, gh_mat, e_mat, qw, kw, g_mat, steps, nb)
    state_in = state_ssm[0].reshape(nb, SSM_WIDTH, SSM_STATE)

    xp = x_prompt.reshape(batch * seq, D_MODEL)
    qw_t = jnp.broadcast_to((qw * (ATTN_SCALE * LOG2E)).reshape(ATTN_WIDTH, 1), (ATTN_WIDTH, PROJ_ROWS))
    kw_t = jnp.broadcast_to(kw.reshape(KV_WIDTH, 1), (KV_WIDTH, PROJ_ROWS))
    gz, xs_p, b_p, c_p, dt_p, q_t, k_t, v_t, gza_t, tail_p, st_s, yoff, wo_all = _inproj_prompt(
        xp, nw, w_t, cw, cb, dtb, qw_t, kw_t, batch, seq, cdec[:, :SSM_HEADS].reshape(-1), state_in, c3, b3, xw,
        w_out[0])
    bias_t = _bias_tables_t(rel_flat * LOG2E, jnp.asarray(tab["prompt_buckets"].transpose(0, 2, 1)))
    sink_rows = jnp.repeat(sink.reshape(ATTN_KV_HEADS, Q_PER_KV) * LOG2E, CHUNK, axis=1)
    sink_rows = sink_rows.reshape(ATTN_KV_HEADS, 1, -1)
    bias_c = _bias_tables(rel_flat, jnp.asarray(tab["cache_buckets"])).reshape(ATTN_HEADS * steps, WINDOW)
    bias_n = _bias_tables(rel_flat, jnp.asarray(tab["new_buckets"])).reshape(ATTN_HEADS * steps, SUBLANES)
    bias_n = bias_n.at[:, steps].set(jnp.repeat(sink, steps))
    bias_c = jnp.tile(bias_c, (SAMPLE_BATCH_BLOCK, 1))
    bias_n = jnp.tile(bias_n, (SAMPLE_BATCH_BLOCK, 1))
    to_t = lambda a: jnp.transpose(a[0], (0, 2, 3, 1)).reshape(nb, KV_WIDTH, WINDOW)
    from_t = lambda a: jnp.transpose(
        a.reshape(a.shape[0], ATTN_KV_HEADS, ATTN_HEAD_DIM, WINDOW), (0, 3, 1, 2))[None]
    y_p, st_p, k_pt, v_pt, y_attn3, k_st, v_st = _mixer(
        gz, xs_p, b_p, c_p, dt_p, alog, dskip, snw, e_mat, q_t, k_t, v_t, gza_t, bias_t, sink_rows, xp, wo_all,
        batch, seq, t3(qn), t3(kn), t3(v_smp), t3(za), to_t(cache_k), to_t(cache_v), bias_c, bias_n)
    y_p = y_p.reshape(batch, seq, D_MODEL)
    conv_p = tail_p[:, SUBLANES - (CONV_WIDTH - 1):, :]

    f2 = lambda a: a.reshape(steps * nb, a.shape[-1])
    k_s, v_s = from_t(k_st), from_t(v_st)
    y_s = _sample_back(f2(ypart), f2(yoff), f2(ea), gz_smp, snw, f2(y_attn3), xs, wo_all)
    y_s = jnp.swapaxes(y_s.reshape(steps, nb, D_MODEL), 0, 1)

    st5 = lambda a: a.reshape(1, a.shape[0], SSM_HEADS, SSM_HEAD_DIM, SSM_STATE)
    return (y_p, y_s, from_t(k_pt), from_t(v_pt), st5(st_p), conv_p[None],
            k_s, v_s, st5(st_s), jnp.swapaxes(conv_s3, 0, 1)[None])
```

```python
import functools
import math

import numpy as np
import jax
import jax.numpy as jnp
from jax import lax
from jax.experimental import pallas as pl
from jax.experimental.pallas import tpu as pltpu

F32 = jnp.float32
BF16 = jnp.bfloat16

D_MODEL = 1024
SSM_HEADS = 16
SSM_HEAD_DIM = 64
SSM_WIDTH = SSM_HEADS * SSM_HEAD_DIM
SSM_GROUPS = 2
SSM_STATE = 128
GROUP_WIDTH = SSM_WIDTH // SSM_GROUPS
BC_WIDTH = SSM_GROUPS * SSM_STATE
CONV_WIDTH = 4
CONV_DIM = SSM_WIDTH + 2 * BC_WIDTH
CHUNK = 128
ATTN_HEADS = 16
ATTN_KV_HEADS = 4
Q_PER_KV = ATTN_HEADS // ATTN_KV_HEADS
ATTN_HEAD_DIM = 64
ATTN_WIDTH = ATTN_HEADS * ATTN_HEAD_DIM
KV_WIDTH = ATTN_KV_HEADS * ATTN_HEAD_DIM
WINDOW = 128
ATTN_SCALE = ATTN_HEAD_DIM ** -0.5
REL_BUCKETS = 32
REL_MAX_DIST = 128
EPS = 1e-6
LOG2E = 1.0 / math.log(2.0)
NEG = -1e30

LANES = 128
SUBLANES = 8
MXU_WIDTH = 256
VMEM_LIMIT = 56 * 1024 * 1024


def _in_proj_rows():
    widths = (("z", SSM_WIDTH), ("xbc", CONV_DIM), ("dt", SSM_HEADS), ("q", ATTN_WIDTH), ("k", KV_WIDTH),
              ("v", KV_WIDTH), ("za", ATTN_WIDTH))
    rows, start = {}, 0
    for name, width in widths:
        rows[name] = slice(start, start + width)
        start += width
    return rows


IN_ROWS = _in_proj_rows()
DT_ROWS = slice(IN_ROWS["dt"].start, IN_ROWS["dt"].start + LANES)

PROJ_ROWS = 512
CHUNKS_PER_STEP = 4
SAMPLE_BATCH_BLOCK = 8


def _dot(a, b):
    return jnp.dot(a, b, preferred_element_type=F32)


def _dot_nt(a, b):
    return lax.dot_general(a, b, (((1,), (1,)), ((), ())), preferred_element_type=F32)


def _dot_tn(a, b):
    return lax.dot_general(a, b, (((0,), (0,)), ((), ())), preferred_element_type=F32)


def _split2(v):
    hi = v.astype(BF16)
    lo = (v - hi.astype(F32)).astype(BF16)
    return hi, lo


def _dot_sel(v, m):
    hi, lo = _split2(v)
    if 2 * v.shape[1] <= MXU_WIDTH:
        return _dot(jnp.concatenate([hi, lo], axis=1), jnp.concatenate([m, m], axis=0))
    return _dot(hi, m) + _dot(lo, m)


def _dot_sel3(m, v):
    hi = v.astype(BF16)
    r1 = v - hi.astype(F32)
    mid = r1.astype(BF16)
    lo = (r1 - mid.astype(F32)).astype(BF16)
    return _dot(m, hi) + _dot(m, mid) + _dot(m, lo)


def _silu(x):
    return x / (1.0 + jnp.exp(-x))


def _softplus(x):
    return jnp.maximum(x, 0.0) + jnp.log1p(jnp.exp(-jnp.abs(x)))


def _params(*sem):
    return pltpu.CompilerParams(dimension_semantics=sem, vmem_limit_bytes=VMEM_LIMIT)


def _const_spec(shape):
    nd = len(shape)
    return pl.BlockSpec(shape, lambda *_: (0,) * nd)


def _normed_input(x_ref, nw_ref):
    x = x_ref[...]
    ms = jnp.mean(x * x, axis=-1, keepdims=True)
    return (x * lax.rsqrt(ms + EPS) * nw_ref[...]).astype(BF16)


def _dt_projection(h, wt_ref):
    raw = _dot_nt(h, wt_ref[DT_ROWS, :])
    return jnp.where(lax.broadcasted_iota(jnp.int32, raw.shape, 1) < SSM_HEADS, raw, 0.0)


def _shift_rows(u, prev_tail, k):
    rows, width = u.shape
    tiles = jnp.concatenate([prev_tail, u], axis=0).reshape(rows // SUBLANES + 1, SUBLANES, width)
    rot = jnp.concatenate([tiles[:, SUBLANES - k:], tiles[:, :SUBLANES - k]], axis=1)
    first = lax.broadcasted_iota(jnp.int32, (1, SUBLANES, width), 1) < k
    return jnp.where(first, rot[:-1], rot[1:]).reshape(rows, width)


def _inproj_prompt_body(steps_per_seq, n_state_blocks, x_ref, nw_ref, wt_ref, cw_ref, cb_ref, dtb_ref,
                        qwt_ref, kwt_ref, cdec_ref, st_ref, sc_ref, sb_ref, sxw_ref, wo_ref,
                        gz_ref, xs_ref, b_ref, c_ref, dt_ref, qt_ref, kt_ref, vt_ref, gzat_ref, tail_ref,
                        newst_ref, yoff_ref, wo16_ref, tail_sc):
    step = pl.program_id(0)

    @pl.when(step < n_state_blocks)
    def _():
        _ssd_sample_state_block(step, cdec_ref, st_ref, sc_ref, sb_ref, sxw_ref, newst_ref, yoff_ref)

    @pl.when(step >= n_state_blocks)
    def _():
        wo16_ref[...] = wo_ref[...].astype(BF16)

    @pl.when(step % steps_per_seq == 0)
    def _():
        tail_sc[...] = jnp.zeros_like(tail_sc)

    h = _normed_input(x_ref, nw_ref)
    rows = h.shape[0]
    w_tile = lambda name, j: wt_ref[IN_ROWS[name].start + j * MXU_WIDTH:IN_ROWS[name].start + (j + 1) * MXU_WIDTH, :]
    n_side = SSM_WIDTH // MXU_WIDTH
    for j in range(CONV_DIM // MXU_WIDTH):
        cols = slice(j * MXU_WIDTH, (j + 1) * MXU_WIDTH)
        if j < n_side:
            both = _dot_nt(h, jnp.concatenate([w_tile("xbc", j), w_tile("z", j)], axis=0))
            u = both[:, :MXU_WIDTH]
            gz_ref[:, cols] = _silu(both[:, MXU_WIDTH:])
        else:
            u = _dot_nt(h, w_tile("xbc", j))
        prev_tail = tail_sc[:, cols]
        conv = cb_ref[:, cols] + u * cw_ref[CONV_WIDTH - 1:CONV_WIDTH, cols]
        for k in range(1, CONV_WIDTH):
            tap = CONV_WIDTH - 1 - k
            conv = conv + _shift_rows(u, prev_tail, k) * cw_ref[tap:tap + 1, cols]
        new_tail = u[rows - SUBLANES:, :]
        tail_sc[:, cols] = new_tail
        tail_ref[0, :, cols] = new_tail
        act = _silu(conv)
        if j < SSM_WIDTH // MXU_WIDTH:
            xs_ref[:, cols] = act
        elif j == SSM_WIDTH // MXU_WIDTH:
            b_ref[...] = act.astype(BF16)
        else:
            c_ref[...] = act.astype(BF16)

    for j in range(ATTN_WIDTH // MXU_WIDTH):
        feats = slice(j * MXU_WIDTH, (j + 1) * MXU_WIDTH)
        gzat_ref[feats, :] = _silu(_dot_nt(w_tile("za", j), h))
    dt_ref[...] = _softplus(_dt_projection(h, wt_ref) + dtb_ref[...])

    def head_norm(t, w):
        x3 = t.reshape(-1, ATTN_HEAD_DIM, rows)
        ms = jnp.mean(x3 * x3, axis=1, keepdims=True)
        return (x3 * lax.rsqrt(ms + EPS)).reshape(t.shape) * w

    for j in range(ATTN_WIDTH // MXU_WIDTH):
        feats = slice(j * MXU_WIDTH, (j + 1) * MXU_WIDTH)
        qt_ref[feats, :] = head_norm(_dot_nt(w_tile("q", j), h), qwt_ref[feats, :]).astype(BF16)
    kt_ref[...] = head_norm(_dot_nt(wt_ref[IN_ROWS["k"], :], h), kwt_ref[...])
    vt_ref[...] = _dot_nt(wt_ref[IN_ROWS["v"], :], h)


def _inproj_prompt(x2d, norm_w, w_t, conv_w, conv_b, dtb, qw_t, kw_t, batch, seq, cdec_flat, state, c3, b3, xw3,
                   w_out):
    rows = x2d.shape[0]
    tm = PROJ_ROWS
    steps_per_seq = seq // tm
    steps, nb = c3.shape[0], c3.shape[1]
    bb = SAMPLE_BATCH_BLOCK
    n_state_blocks = nb // bb
    n_steps = rows // tm
    assert n_state_blocks < n_steps, "state blocks and weight-cast blocks share the in-proj grid steps"
    last = n_state_blocks - 1
    cast_rows = w_out.shape[0] // (n_steps - n_state_blocks)
    assert cast_rows * (n_steps - n_state_blocks) == w_out.shape[0] and cast_rows % (2 * SUBLANES) == 0
    wo_spec = pl.BlockSpec((cast_rows, D_MODEL), lambda i: (jnp.maximum(i - n_state_blocks, 0), 0))
    resident = lambda a: pl.BlockSpec(a.shape, lambda i: (0, 0), pipeline_mode=pl.Buffered(1))
    rowblk = lambda w: pl.BlockSpec((tm, w), lambda i: (i, 0))
    colblk = pl.BlockSpec((ATTN_WIDTH, tm), lambda i: (0, i))
    kvblk = pl.BlockSpec((KV_WIDTH, tm), lambda i: (0, i))
    tok = lambda w: pl.BlockSpec((steps, bb, w), lambda i: (0, jnp.minimum(i, last), 0))
    st_spec = pl.BlockSpec((bb, SSM_WIDTH, SSM_STATE), lambda i: (jnp.minimum(i, last), 0, 0))
    in_specs = ([rowblk(D_MODEL), _const_spec((1, D_MODEL)), resident(w_t)]
                + [_const_spec(conv_w.shape), _const_spec(conv_b.shape), _const_spec(dtb.shape)]
                + [resident(qw_t), resident(kw_t)]
                + [pl.BlockSpec(memory_space=pltpu.SMEM), st_spec, tok(BC_WIDTH), tok(BC_WIDTH), tok(SSM_WIDTH),
                   wo_spec])
    out_specs = [rowblk(SSM_WIDTH), rowblk(SSM_WIDTH), rowblk(BC_WIDTH), rowblk(BC_WIDTH), rowblk(LANES),
                 colblk, kvblk, kvblk, colblk,
                 pl.BlockSpec((1, SUBLANES, CONV_DIM), lambda i: (i // steps_per_seq, 0, 0)),
                 st_spec, tok(SSM_WIDTH), wo_spec]
    f = lambda r, c, dt=F32: jax.ShapeDtypeStruct((r, c), dt)
    out_shape = [f(rows, SSM_WIDTH), f(rows, SSM_WIDTH), f(rows, BC_WIDTH, BF16), f(rows, BC_WIDTH, BF16),
                 f(rows, LANES), f(ATTN_WIDTH, rows, BF16), f(KV_WIDTH, rows), f(KV_WIDTH, rows), f(ATTN_WIDTH, rows),
                 jax.ShapeDtypeStruct((batch, SUBLANES, CONV_DIM), F32),
                 jax.ShapeDtypeStruct(state.shape, F32), jax.ShapeDtypeStruct((steps, nb, SSM_WIDTH), F32),
                 jax.ShapeDtypeStruct(w_out.shape, BF16)]
    return pl.pallas_call(
        functools.partial(_inproj_prompt_body, steps_per_seq, n_state_blocks), grid=(n_steps,),
        in_specs=in_specs, out_specs=out_specs, out_shape=out_shape,
        scratch_shapes=[pltpu.VMEM((SUBLANES, CONV_DIM), F32)],
        compiler_params=_params("arbitrary"), name="inproj_prompt")(
            x2d, norm_w, w_t, conv_w, conv_b, dtb, qw_t, kw_t, cdec_flat, state, c3, b3, xw3, w_out)


def _group_rmsnorm(gy, norm_w):
    parts = []
    for g in range(SSM_GROUPS):
        blk = gy[:, g * GROUP_WIDTH:(g + 1) * GROUP_WIDTH]
        ms = jnp.mean(blk * blk, axis=-1, keepdims=True)
        parts.append(blk * lax.rsqrt(ms + EPS))
    return jnp.concatenate(parts, axis=1) * norm_w


def _ssd_chunk(gz, xs, b_bf, c_bf, dt, a_neg, dskip, norm_w, e_mat, state):
    xs_bf = xs.astype(BF16)

    a = dt * a_neg
    li = lax.broadcasted_iota(jnp.int32, (CHUNK, CHUNK), 0)
    si = lax.broadcasted_iota(jnp.int32, (CHUNK, CHUNK), 1)
    causal = li >= si
    a_cum = _dot_sel3(jnp.where(causal, 1.0, 0.0).astype(BF16), a)
    a2 = a_cum * LOG2E
    row_term = a2.T - jnp.log2(dt.T)
    ea_full = _dot_sel(jnp.exp(a_cum), e_mat)
    w_full = _dot((dt * jnp.exp(a_cum[CHUNK - 1:CHUNK, :] - a_cum)).astype(BF16), e_mat)

    cb = [_dot_nt(c_bf[:, g * SSM_STATE:(g + 1) * SSM_STATE], b_bf[:, g * SSM_STATE:(g + 1) * SSM_STATE])
          for g in range(SSM_GROUPS)]
    half = lax.broadcasted_iota(jnp.int32, (CHUNK, LANES), 1) < SSM_HEAD_DIM
    heads_per_group = SSM_HEADS // SSM_GROUPS
    y_parts = []
    for pair in range(SSM_HEADS // 2):
        blocks = []
        for h in (2 * pair, 2 * pair + 1):
            seg = a2[:, h:h + 1] - row_term[h:h + 1, :]
            decay_dt = jnp.exp2(jnp.where(causal, seg, -jnp.inf))
            blocks.append((cb[h // heads_per_group] * decay_dt).astype(BF16))
        lhs = jnp.concatenate(blocks, axis=1)
        xp = xs_bf[:, pair * LANES:(pair + 1) * LANES]
        zero = jnp.zeros_like(xp)
        rhs = jnp.concatenate([jnp.where(half, xp, zero), jnp.where(half, zero, xp)], axis=0)
        y_parts.append(_dot(lhs, rhs))
    y_diag = jnp.concatenate(y_parts, axis=1)

    state_bf = state.astype(BF16)
    xw_bf = (xs * w_full).astype(BF16)
    y_off, upd = [], []
    for g in range(SSM_GROUPS):
        cols = slice(g * GROUP_WIDTH, (g + 1) * GROUP_WIDTH)
        ns = slice(g * SSM_STATE, (g + 1) * SSM_STATE)
        y_off.append(_dot(c_bf[:, ns], state_bf[:, cols]))
        upd.append(_dot_tn(b_bf[:, ns], xw_bf[:, cols]))
    y = y_diag + jnp.concatenate(y_off, axis=1) * ea_full + dskip * xs
    new_state = state * ea_full[CHUNK - 1:CHUNK, :] + jnp.concatenate(upd, axis=1)
    return _group_rmsnorm(y * gz, norm_w), new_state


def _rel_bucket_np(dist):
    max_exact = REL_BUCKETS // 2
    d_f = np.maximum(dist, 1).astype(np.float32)
    large = max_exact + (np.log(d_f / np.float32(max_exact)) / np.float32(math.log(REL_MAX_DIST / max_exact))
                         * np.float32(REL_BUCKETS - max_exact)).astype(np.int32)
    return np.where(dist < max_exact, dist, np.minimum(large, REL_BUCKETS - 1)).astype(np.int32)


def _bucket_or_masked(dist, extra_mask=None):
    ok = (dist >= 0) & (dist <= WINDOW)
    if extra_mask is not None:
        ok = ok & extra_mask
    return np.where(ok, _rel_bucket_np(np.clip(dist, 0, WINDOW)), -1).astype(np.int32)


def _bias_body(rel_ref, bucket_ref, o_ref):
    bucket = bucket_ref[0]

    def per_head(h, carry):
        acc = jnp.full(bucket.shape, NEG, F32)
        for bkt in range(REL_BUCKETS):
            acc = jnp.where(bucket == bkt, rel_ref[bkt * ATTN_HEADS + h], acc)
        o_ref[0, h] = acc
        return carry

    lax.fori_loop(0, ATTN_HEADS, per_head, 0)


def _bias_tables(rel_flat, buckets):
    nv, lq, lk = buckets.shape
    return pl.pallas_call(
        _bias_body, grid=(nv,),
        in_specs=[pl.BlockSpec(memory_space=pltpu.SMEM), pl.BlockSpec((1, lq, lk), lambda v: (v, 0, 0))],
        out_specs=pl.BlockSpec((1, ATTN_HEADS, lq, lk), lambda v: (v, 0, 0, 0)),
        out_shape=jax.ShapeDtypeStruct((nv, ATTN_HEADS, lq, lk), F32),
        compiler_params=_params("arbitrary"), name="rel_bias")(rel_flat, buckets)


def _bias_t_body(rel_ref, dist_bucket_ref, o_ref):
    T = CHUNK
    width = dist_bucket_ref.shape[1]
    bucket = dist_bucket_ref[...]
    own_block = lax.broadcasted_iota(jnp.int32, (2 * T, T), 0) >= T

    def per_kv_head(n, carry):
        for g in range(Q_PER_KV):
            profile = jnp.full(bucket.shape, NEG, F32)
            for bkt in range(REL_BUCKETS):
                profile = jnp.where(bucket == bkt, rel_ref[bkt * ATTN_HEADS + n * Q_PER_KV + g], profile)
            rows = jnp.broadcast_to(profile[0:1, :], (2 * T, width))
            tile = pltpu.roll(rows, width - T, axis=1, stride=1, stride_axis=0)[:, :T]
            o_ref[1, n, :, g * T:(g + 1) * T] = tile
            o_ref[0, n, :, g * T:(g + 1) * T] = jnp.where(own_block, tile, NEG)
        return carry

    lax.fori_loop(0, ATTN_KV_HEADS, per_kv_head, 0)


def _bias_tables_t(rel_flat, dist_bucket):
    out_dims = (2, ATTN_KV_HEADS, 2 * CHUNK, Q_PER_KV * CHUNK)
    return pl.pallas_call(
        _bias_t_body,
        in_specs=[pl.BlockSpec(memory_space=pltpu.SMEM), pl.BlockSpec(memory_space=pltpu.VMEM)],
        out_specs=pl.BlockSpec(memory_space=pltpu.VMEM),
        out_shape=jax.ShapeDtypeStruct(out_dims, F32),
        compiler_params=pltpu.CompilerParams(vmem_limit_bytes=VMEM_LIMIT), name="rel_bias_t")(rel_flat, dist_bucket)


def _head_rmsnorm(x, g_mat, e_mat, w):
    ms = _dot_sel(x * x, g_mat) * (1.0 / ATTN_HEAD_DIM)
    return x * _dot_sel(lax.rsqrt(ms + EPS), e_mat) * w


def _lane_head(shape):
    return lax.broadcasted_iota(jnp.int32, shape, 1) // ATTN_HEAD_DIM


def _attn_block(q_blk, kcat, vcat_t, bias_at, sink_ref):
    T = CHUNK
    lane_head = _lane_head((2 * T, KV_WIDTH))
    zero = jnp.zeros((2 * T, KV_WIDTH), BF16)
    head = lambda h: q_blk[h * ATTN_HEAD_DIM:(h + 1) * ATTN_HEAD_DIM]
    q_cols = jnp.concatenate(
        [jnp.concatenate([head(n * Q_PER_KV + g) for n in range(ATTN_KV_HEADS)], axis=0)
         for g in range(Q_PER_KV)], axis=1)
    row_head = lax.broadcasted_iota(jnp.int32, (KV_WIDTH, 2 * T), 0) // ATTN_HEAD_DIM
    probs, vals, inv = [], [], {}
    for n in range(ATTN_KV_HEADS):
        s = _dot(jnp.where(lane_head == n, kcat, zero), q_cols)
        sink = sink_ref[n]
        cols = []
        for g in range(Q_PER_KV):
            c = slice(g * T, (g + 1) * T)
            sg = s[:, c] + bias_at(n, c)
            m = jnp.maximum(jnp.max(sg, axis=0, keepdims=True), sink[:, c])
            p = jnp.exp2(sg - m)
            inv[n, g] = 1.0 / (jnp.sum(p, axis=0, keepdims=True) + jnp.exp2(sink[:, c] - m))
            cols.append(p.astype(BF16))
        probs.append(jnp.concatenate(cols, axis=1))
        vals.append(jnp.where(row_head == n, vcat_t, zero.T))
    o_t = _dot(jnp.concatenate(vals, axis=1), jnp.concatenate(probs, axis=0))
    return jnp.concatenate(
        [o_t[n * ATTN_HEAD_DIM:(n + 1) * ATTN_HEAD_DIM, g * T:(g + 1) * T] * inv[n, g]
         for n in range(ATTN_KV_HEADS) for g in range(Q_PER_KV)], axis=0)


def _mixer_body(gz_ref, xs_ref, b_ref, c_ref, dt_ref, alog_ref, dskip_ref, nw_ref, e_ref,
                qt_ref, kt_ref, vt_ref, gzt_ref, bias_ref, sink_ref, x_ref, wo_ref,
                y_ref, st_ref, knt_ref, vnt_ref,
                state_sc, kcat_sc, vcat_t_sc, yssm_sc, yattn_t_sc):
    T = CHUNK
    step = pl.program_id(1)
    cols_step = CHUNKS_PER_STEP * T

    @pl.when(step == 0)
    def _():
        state_sc[...] = jnp.zeros_like(state_sc)
        kcat_sc[0:T, :] = jnp.zeros((T, KV_WIDTH), BF16)
        vcat_t_sc[:, 0:T] = jnp.zeros((KV_WIDTH, T), BF16)

    qn = qt_ref[...]
    kn_t = kt_ref[...]
    v_t = vt_ref[...]
    knt_ref[0] = kn_t[:, cols_step - T:]
    vnt_ref[0] = v_t[:, cols_step - T:]
    kcat_sc[T:, :] = kn_t.T.astype(BF16)
    vcat_t_sc[:, T:] = v_t.astype(BF16)
    e_mat = e_ref[...]

    a_neg = -jnp.exp(alog_ref[...])
    state = state_sc[...]
    first_variant = jnp.minimum(step, 1)
    for j in range(CHUNKS_PER_STEP):
        r = slice(j * T, (j + 1) * T)
        y, state = _ssd_chunk(gz_ref[r, :], xs_ref[r, :], b_ref[r, :], c_ref[r, :], dt_ref[r, :], a_neg,
                              dskip_ref[...], nw_ref[...], e_mat, state)
        yssm_sc[r, :] = y.astype(BF16)
        variant = first_variant if j == 0 else 1
        y_t = _attn_block(qn[:, r], kcat_sc[j * T:(j + 2) * T, :], vcat_t_sc[:, j * T:(j + 2) * T],
                          lambda n, c, variant=variant: bias_ref[variant, n, :, c], sink_ref)
        yattn_t_sc[:, r] = (y_t * gzt_ref[:, r]).astype(BF16)
    state_sc[...] = state
    kcat_sc[0:T, :] = kcat_sc[cols_step:, :]
    vcat_t_sc[:, 0:T] = vcat_t_sc[:, cols_step:]

    y_ref[...] = (x_ref[...] + _dot(yssm_sc[...], wo_ref[:SSM_WIDTH, :])
                  + _dot_tn(yattn_t_sc[...], wo_ref[SSM_WIDTH:, :]))

    @pl.when(step == pl.num_programs(1) - 1)
    def _():
        st_ref[0] = state.T


def _mixer(gz, xs, b, c, dt, alog, dskip, norm_w, e_mat, q_t, k_t, v_t, gz_t, bias_t, sink_rows,
           x2d, w_out, batch, seq):
    step_rows = CHUNKS_PER_STEP * CHUNK
    ns = seq // step_rows
    row = lambda w: pl.BlockSpec((step_rows, w), lambda b, i: (b * ns + i, 0))
    col = lambda w: pl.BlockSpec((w, step_rows), lambda b, i: (0, b * ns + i))
    resident = lambda a: pl.BlockSpec(a.shape, lambda b, i: (0,) * a.ndim, pipeline_mode=pl.Buffered(1))
    in_specs = [row(SSM_WIDTH), row(SSM_WIDTH), row(BC_WIDTH), row(BC_WIDTH), row(LANES),
                _const_spec((1, LANES)), _const_spec((1, SSM_WIDTH)), _const_spec((1, SSM_WIDTH)), resident(e_mat),
                col(ATTN_WIDTH), col(KV_WIDTH), col(KV_WIDTH), col(ATTN_WIDTH),
                resident(bias_t), _const_spec(sink_rows.shape), row(D_MODEL), resident(w_out)]
    kv_out = pl.BlockSpec((1, KV_WIDTH, CHUNK), lambda b, i: (b, 0, 0))
    out_specs = [row(D_MODEL), pl.BlockSpec((1, SSM_WIDTH, SSM_STATE), lambda b, i: (b, 0, 0)), kv_out, kv_out]
    out_shape = [jax.ShapeDtypeStruct((batch * seq, D_MODEL), F32),
                 jax.ShapeDtypeStruct((batch, SSM_WIDTH, SSM_STATE), F32),
                 jax.ShapeDtypeStruct((batch, KV_WIDTH, CHUNK), F32),
                 jax.ShapeDtypeStruct((batch, KV_WIDTH, CHUNK), F32)]
    scratch = [pltpu.VMEM((SSM_STATE, SSM_WIDTH), F32),
               pltpu.VMEM((step_rows + CHUNK, KV_WIDTH), BF16), pltpu.VMEM((KV_WIDTH, step_rows + CHUNK), BF16),
               pltpu.VMEM((step_rows, SSM_WIDTH), BF16), pltpu.VMEM((ATTN_WIDTH, step_rows), BF16)]
    return pl.pallas_call(
        _mixer_body, grid=(batch, ns), in_specs=in_specs, out_specs=out_specs, out_shape=out_shape,
        scratch_shapes=scratch, compiler_params=_params("arbitrary", "arbitrary"), name="mixer")(
            gz, xs, b, c, dt, alog, dskip, norm_w, e_mat, q_t, k_t, v_t, gz_t, bias_t, sink_rows, x2d, w_out)


def _sample_front_body(x_ref, nw_ref, wt_ref, sconv_ref, cw_ref, cb_ref, dtb_ref, alog_ref, dskip_ref,
                       gh_ref, e_ref, qw_ref, kw_ref, g_ref,
                       ypart_ref, ea_ref, xw_ref, b_ref, c_ref, cdec_ref, convnew_ref,
                       gz_ref, qn_ref, kn_ref, v_ref, za_ref):
    steps, nb = ypart_ref.shape[0], ypart_ref.shape[1]
    tail = CONV_WIDTH - 1
    h = _normed_input(x_ref, nw_ref)
    proj = lambda name: _dot_nt(h, wt_ref[IN_ROWS[name], :])
    slab = lambda a, l: a[l * nb:(l + 1) * nb]
    xbc = proj("xbc")
    dt_raw = _dt_projection(h, wt_ref)
    gz_ref[...] = _silu(proj("z"))
    v_ref[...] = proj("v")
    za_ref[...] = proj("za")
    g_mat = g_ref[...]
    e_mat = e_ref[...]
    qn = _head_rmsnorm(proj("q"), g_mat, e_mat, qw_ref[...]) * ATTN_SCALE
    head = lambda hd: qn[:, hd * ATTN_HEAD_DIM:(hd + 1) * ATTN_HEAD_DIM]
    qn_ref[...] = jnp.concatenate(
        [head(n * Q_PER_KV + g) for g in range(Q_PER_KV) for n in range(ATTN_KV_HEADS)], axis=1)
    kn_ref[...] = _head_rmsnorm(proj("k"), g_mat[:KV_WIDTH], e_mat[:, :KV_WIDTH], kw_ref[...])

    full = [sconv_ref[j] for j in range(tail)] + [slab(xbc, l) for l in range(steps)]
    for j in range(tail):
        convnew_ref[j] = full[steps + j]
    gh = gh_ref[...]
    a_neg = -jnp.exp(alog_ref[...])
    xs, bm, cm, dts, acum = [], [], [], [], []
    run = None
    for l in range(steps):
        conv = cb_ref[...]
        for tap in range(CONV_WIDTH):
            conv = conv + full[l + tap] * cw_ref[tap:tap + 1, :]
        act = _silu(conv)
        xs.append(act[:, :SSM_WIDTH])
        bm.append(act[:, SSM_WIDTH:SSM_WIDTH + BC_WIDTH])
        cm.append(act[:, SSM_WIDTH + BC_WIDTH:])
        d = _softplus(slab(dt_raw, l) + dtb_ref[...])
        dts.append(d)
        run = d * a_neg if run is None else run + d * a_neg
        acum.append(run)
        b_ref[l] = bm[l]
        c_ref[l] = cm[l]
    for l in range(steps):
        y = dskip_ref[...] * xs[l]
        for s in range(l + 1):
            cb_h = _dot_sel(cm[l] * bm[s], gh)
            coef = cb_h * jnp.exp(acum[l] - acum[s]) * dts[s]
            y = y + _dot_sel(coef, e_mat) * xs[s]
        ypart_ref[l] = y
        ea_ref[l] = _dot_sel(jnp.exp(acum[l]), e_mat)
        xw_ref[l] = xs[l] * _dot_sel(dts[l] * jnp.exp(acum[steps - 1] - acum[l]), e_mat)
    cdec_ref[...] = jnp.exp(acum[steps - 1])


def _sample_front(x2d, norm_w, w_t, sconv3, conv_w, conv_b, dtb, alog, dskip, gh_mat, e_mat, qw, kw, g_mat,
                  steps, nb):
    rows = steps * nb
    f = lambda *s: jax.ShapeDtypeStruct(s, F32)
    out_shape = [f(steps, nb, SSM_WIDTH), f(steps, nb, SSM_WIDTH), f(steps, nb, SSM_WIDTH),
                 f(steps, nb, BC_WIDTH), f(steps, nb, BC_WIDTH), f(nb, LANES), f(CONV_WIDTH - 1, nb, CONV_DIM),
                 f(rows, SSM_WIDTH), f(rows, ATTN_WIDTH), f(rows, KV_WIDTH), f(rows, KV_WIDTH), f(rows, ATTN_WIDTH)]
    return pl.pallas_call(_sample_front_body, out_shape=out_shape,
                          compiler_params=pltpu.CompilerParams(vmem_limit_bytes=VMEM_LIMIT),
                          name="sample_front")(
        x2d, norm_w, w_t, sconv3, conv_w, conv_b, dtb, alog, dskip, gh_mat, e_mat, qw, kw, g_mat)


def _ssd_sample_state_block(i, cdec_ref, st_ref, c_ref, b_ref, xw_ref, new_ref, yoff_ref):
    heads_per_group = SSM_HEADS // SSM_GROUPS
    for j in range(SAMPLE_BATCH_BLOCK):
        st = st_ref[j]
        cb_bf = c_ref[:, j, :].astype(BF16)
        bb_bf = b_ref[:, j, :].astype(BF16)
        xw_bf = xw_ref[:, j, :].astype(BF16)
        y_parts = []
        for g in range(SSM_GROUPS):
            rows = slice(g * GROUP_WIDTH, (g + 1) * GROUP_WIDTH)
            ns = slice(g * SSM_STATE, (g + 1) * SSM_STATE)
            y_parts.append(_dot_nt(cb_bf[:, ns], st[rows].astype(BF16)))
            upd = _dot_tn(xw_bf[:, rows], bb_bf[:, ns])
            for hh in range(heads_per_group):
                h = g * heads_per_group + hh
                r = slice(h * SSM_HEAD_DIM, (h + 1) * SSM_HEAD_DIM)
                dec = cdec_ref[(i * SAMPLE_BATCH_BLOCK + j) * SSM_HEADS + h]
                new_ref[j, r, :] = st[r] * dec + upd[hh * SSM_HEAD_DIM:(hh + 1) * SSM_HEAD_DIM]
        yoff_ref[:, j, :] = jnp.concatenate(y_parts, axis=1)


def _sample_back_body(ypart_ref, yoff_ref, ea_ref, gz_ref, nw_ref, ya_ref, x_ref, wo_ref, o_ref):
    y = ypart_ref[...] + yoff_ref[...] * ea_ref[...]
    y_ssm = _group_rmsnorm(y * gz_ref[...], nw_ref[...])
    o_ref[...] = (x_ref[...] + _dot(y_ssm.astype(BF16), wo_ref[:SSM_WIDTH, :])
                  + _dot(ya_ref[...].astype(BF16), wo_ref[SSM_WIDTH:, :]))


def _sample_back(ypart, yoff, ea, gz, norm_w, y_attn, x2d, w_out):
    return pl.pallas_call(_sample_back_body, out_shape=jax.ShapeDtypeStruct(x2d.shape, F32),
                          compiler_params=pltpu.CompilerParams(vmem_limit_bytes=VMEM_LIMIT),
                          name="sample_back")(ypart, yoff, ea, gz, norm_w, y_attn, x2d, w_out)


def _attn_sample_body(q_ref, kn_ref, vn_ref, z_ref, ckt_ref, cvt_ref, biasc_ref, biasn_ref,
                      y_ref, kot_ref, vot_ref):
    steps = q_ref.shape[0]
    bb = SAMPLE_BATCH_BLOCK
    blk = Q_PER_KV * steps
    rows = ATTN_KV_HEADS * blk
    pad = jnp.zeros((SUBLANES - steps, KV_WIDTH), F32)
    lane_head = _lane_head((blk, KV_WIDTH))
    zero = jnp.zeros((blk, KV_WIDTH), F32)

    s_c, s_n, k8, v8 = [], [], [], []
    for j in range(bb):
        q = q_ref[:, j, :]
        qg = jnp.concatenate([q[:, g * KV_WIDTH:(g + 1) * KV_WIDTH] for g in range(Q_PER_KV)], axis=0)
        qx = jnp.concatenate([jnp.where(lane_head == n, qg, zero) for n in range(ATTN_KV_HEADS)], axis=0)
        qx = qx.astype(BF16)
        k8.append(jnp.concatenate([kn_ref[:, j, :], pad], axis=0))
        v8.append(jnp.concatenate([vn_ref[:, j, :], pad], axis=0))
        s_c.append(_dot(qx, ckt_ref[j].astype(BF16)))
        s_n.append(_dot_nt(qx, k8[j].astype(BF16)))
    s_c = jnp.concatenate(s_c, axis=0) + biasc_ref[...]
    s_n = jnp.concatenate(s_n, axis=0) + biasn_ref[...]
    m = jnp.maximum(jnp.max(s_c, axis=-1, keepdims=True), jnp.max(s_n, axis=-1, keepdims=True))
    p_c = jnp.exp(s_c - m)
    p_n = jnp.exp(s_n - m)
    inv = 1.0 / (jnp.sum(p_c, axis=-1, keepdims=True) + jnp.sum(p_n, axis=-1, keepdims=True))
    p_c = (p_c * inv).astype(BF16)
    p_n = (p_n * inv).astype(BF16)

    lane = lax.broadcasted_iota(jnp.int32, (KV_WIDTH, WINDOW), 1)
    for j in range(bb):
        r = slice(j * rows, (j + 1) * rows)
        o = _dot_nt(p_c[r], cvt_ref[j].astype(BF16)) + _dot(p_n[r], v8[j].astype(BF16))
        og = zero
        for n in range(ATTN_KV_HEADS):
            og = og + jnp.where(lane_head == n, o[n * blk:(n + 1) * blk], zero)
        y = jnp.concatenate(
            [og[g * steps:(g + 1) * steps, n * ATTN_HEAD_DIM:(n + 1) * ATTN_HEAD_DIM]
             for n in range(ATTN_KV_HEADS) for g in range(Q_PER_KV)], axis=1)
        y_ref[:, j, :] = y * _silu(z_ref[:, j, :])

        for new8, old_ref, out_ref in ((k8[j], ckt_ref, kot_ref), (v8[j], cvt_ref, vot_ref)):
            tail_rows = jnp.concatenate([new8[steps:], new8[:steps]], axis=0)
            block = jnp.concatenate([jnp.zeros((WINDOW - SUBLANES, KV_WIDTH), F32), tail_rows], axis=0)
            shifted = pltpu.roll(old_ref[j], WINDOW - steps, axis=1)
            out_ref[j] = jnp.where(lane >= WINDOW - steps, block.T, shifted)


def _attn_sample(q3, kn3, vn3, z3, cache_kt, cache_vt, bias_c, bias_n):
    steps, nb = q3.shape[0], q3.shape[1]
    bb = SAMPLE_BATCH_BLOCK
    tok = lambda w: pl.BlockSpec((steps, bb, w), lambda i: (0, i, 0))
    cache_spec = pl.BlockSpec((bb, KV_WIDTH, WINDOW), lambda i: (i, 0, 0))
    return pl.pallas_call(
        _attn_sample_body, grid=(nb // bb,),
        in_specs=[tok(ATTN_WIDTH), tok(KV_WIDTH), tok(KV_WIDTH), tok(ATTN_WIDTH), cache_spec, cache_spec,
                  _const_spec(bias_c.shape), _const_spec(bias_n.shape)],
        out_specs=[tok(ATTN_WIDTH), cache_spec, cache_spec],
        out_shape=[jax.ShapeDtypeStruct((steps, nb, ATTN_WIDTH), F32),
                   jax.ShapeDtypeStruct(cache_kt.shape, F32), jax.ShapeDtypeStruct(cache_vt.shape, F32)],
        compiler_params=_params("parallel"), name="attn_sample")(
            q3, kn3, vn3, z3, cache_kt, cache_vt, bias_c, bias_n)


def _static_tables(steps):
    lanes = np.arange(ATTN_WIDTH)
    g_mat = np.zeros((ATTN_WIDTH, LANES), np.float32)
    g_mat[lanes, lanes // ATTN_HEAD_DIM] = 1.0
    e_mat = g_mat.T.copy()
    bc = np.arange(BC_WIDTH)
    gh_mat = np.zeros((BC_WIDTH, LANES), np.float32)
    for h in range(SSM_HEADS):
        gh_mat[bc // SSM_STATE == h // (SSM_HEADS // SSM_GROUPS), h] = 1.0
    assert WINDOW <= CHUNK
    prompt_buckets = np.broadcast_to(_bucket_or_masked(np.arange(2 * CHUNK))[None, :], (SUBLANES, 2 * CHUNK))
    dist_c = (np.arange(steps) + WINDOW)[:, None] - np.arange(WINDOW)[None, :]
    dist_n = np.arange(steps)[:, None] - np.arange(SUBLANES)[None, :]
    real = np.broadcast_to((np.arange(SUBLANES) < steps)[None, :], dist_n.shape)
    return dict(g=g_mat, e=e_mat, gh=gh_mat, prompt_buckets=prompt_buckets,
                cache_buckets=_bucket_or_masked(dist_c)[None], new_buckets=_bucket_or_masked(dist_n, real)[None])


def kernel(x_prompt, x_sample, cache_k, cache_v, state_ssm, state_conv, norm_w, w_in, conv_w, conv_b, dt_bias,
           a_log, d_skip, ssm_norm_w, q_norm_w, k_norm_w, sinks, rel_table, w_out):
    assert w_in.shape[0] == 1, "single-layer kernel"
    batch, seq, _ = x_prompt.shape
    nb, steps, _ = x_sample.shape
    tab = _static_tables(steps)
    g_mat = jnp.asarray(tab["g"], BF16)
    e_mat = jnp.asarray(tab["e"], BF16)
    gh_mat = jnp.asarray(tab["gh"], BF16)

    w_t = jnp.transpose(w_in[0]).astype(BF16)

    row = lambda v, width: jnp.pad(v.reshape(1, -1), ((0, 0), (0, width - v.size)))
    nw = row(norm_w[0], D_MODEL)
    cw = conv_w[0]
    cb = row(conv_b[0], CONV_DIM)
    dtb = row(dt_bias[0], LANES)
    alog = row(a_log[0], LANES)
    dskip = jnp.repeat(d_skip[0], SSM_HEAD_DIM).reshape(1, SSM_WIDTH)
    snw = row(ssm_norm_w[0], SSM_WIDTH)
    qw = jnp.tile(q_norm_w[0], ATTN_HEADS).reshape(1, ATTN_WIDTH)
    kw = jnp.tile(k_norm_w[0], ATTN_KV_HEADS).reshape(1, KV_WIDTH)
    sink = sinks[0]
    rel_flat = rel_table.reshape(-1)

    xs = jnp.swapaxes(x_sample, 0, 1).reshape(steps * nb, D_MODEL)
    t3 = lambda a: a.reshape(steps, nb, a.shape[-1])
    sconv3 = jnp.swapaxes(state_conv[0], 0, 1)
    ypart, ea, xw, b3, c3, cdec, conv_s3, gz_smp, qn, kn, v_smp, za = _sample_front(
        xs, nw, w_t, sconv3, cw, cb, dtb, alog, dskip, gh_mat, e_mat, qw, kw, g_mat, steps, nb)
    state_in = state_ssm[0].reshape(nb, SSM_WIDTH, SSM_STATE)

    xp = x_prompt.reshape(batch * seq, D_MODEL)
    qw_t = jnp.broadcast_to((qw * (ATTN_SCALE * LOG2E)).reshape(ATTN_WIDTH, 1), (ATTN_WIDTH, PROJ_ROWS))
    kw_t = jnp.broadcast_to(kw.reshape(KV_WIDTH, 1), (KV_WIDTH, PROJ_ROWS))
    gz, xs_p, b_p, c_p, dt_p, q_t, k_t, v_t, gza_t, tail_p, st_s, yoff, wo_all = _inproj_prompt(
        xp, nw, w_t, cw, cb, dtb, qw_t, kw_t, batch, seq, cdec[:, :SSM_HEADS].reshape(-1), state_in, c3, b3, xw,
        w_out[0])
    bias_t = _bias_tables_t(rel_flat * LOG2E, jnp.asarray(tab["prompt_buckets"]))
    sink_rows = jnp.repeat(sink.reshape(ATTN_KV_HEADS, Q_PER_KV) * LOG2E, CHUNK, axis=1)
    sink_rows = sink_rows.reshape(ATTN_KV_HEADS, 1, -1)
    y_p, st_p, k_pt, v_pt = _mixer(gz, xs_p, b_p, c_p, dt_p, alog, dskip, snw, e_mat, q_t, k_t, v_t, gza_t,
                                   bias_t, sink_rows, xp, wo_all, batch, seq)
    y_p = y_p.reshape(batch, seq, D_MODEL)
    conv_p = tail_p[:, SUBLANES - (CONV_WIDTH - 1):, :]

    f2 = lambda a: a.reshape(steps * nb, a.shape[-1])
    bias_c = _bias_tables(rel_flat, jnp.asarray(tab["cache_buckets"])).reshape(ATTN_HEADS * steps, WINDOW)
    bias_n = _bias_tables(rel_flat, jnp.asarray(tab["new_buckets"])).reshape(ATTN_HEADS * steps, SUBLANES)
    bias_n = bias_n.at[:, steps].set(jnp.repeat(sink, steps))
    bias_c = jnp.tile(bias_c, (SAMPLE_BATCH_BLOCK, 1))
    bias_n = jnp.tile(bias_n, (SAMPLE_BATCH_BLOCK, 1))
    to_t = lambda a: jnp.transpose(a[0], (0, 2, 3, 1)).reshape(nb, KV_WIDTH, WINDOW)
    from_t = lambda a: jnp.transpose(
        a.reshape(a.shape[0], ATTN_KV_HEADS, ATTN_HEAD_DIM, WINDOW), (0, 3, 1, 2))[None]
    y_attn3, k_st, v_st = _attn_sample(t3(qn), t3(kn), t3(v_smp), t3(za), to_t(cache_k), to_t(cache_v),
                                       bias_c, bias_n)
    k_s, v_s = from_t(k_st), from_t(v_st)
    y_s = _sample_back(f2(ypart), f2(yoff), f2(ea), gz_smp, snw, f2(y_attn3), xs, wo_all)
    y_s = jnp.swapaxes(y_s.reshape(steps, nb, D_MODEL), 0, 1)

    st5 = lambda a: a.reshape(1, a.shape[0], SSM_HEADS, SSM_HEAD_DIM, SSM_STATE)
    return (y_p, y_s, from_t(k_pt), from_t(v_pt), st5(st_p), conv_p[None],
            k_s, v_s, st5(st_s), jnp.swapaxes(conv_s3, 0, 1)[None])
```

```python
import functools
import math

import numpy as np
import jax
import jax.numpy as jnp
from jax import lax
from jax.experimental import pallas as pl
from jax.experimental.pallas import tpu as pltpu

F32 = jnp.float32
BF16 = jnp.bfloat16

D_MODEL = 1024
SSM_HEADS = 16
SSM_HEAD_DIM = 64
SSM_WIDTH = SSM_HEADS * SSM_HEAD_DIM
SSM_GROUPS = 2
SSM_STATE = 128
GROUP_WIDTH = SSM_WIDTH // SSM_GROUPS
BC_WIDTH = SSM_GROUPS * SSM_STATE
CONV_WIDTH = 4
CONV_DIM = SSM_WIDTH + 2 * BC_WIDTH
CHUNK = 128
ATTN_HEADS = 16
ATTN_KV_HEADS = 4
Q_PER_KV = ATTN_HEADS // ATTN_KV_HEADS
ATTN_HEAD_DIM = 64
ATTN_WIDTH = ATTN_HEADS * ATTN_HEAD_DIM
KV_WIDTH = ATTN_KV_HEADS * ATTN_HEAD_DIM
WINDOW = 128
ATTN_SCALE = ATTN_HEAD_DIM ** -0.5
REL_BUCKETS = 32
REL_MAX_DIST = 128
EPS = 1e-6
LOG2E = 1.0 / math.log(2.0)
NEG = -1e30

LANES = 128
SUBLANES = 8
MXU_WIDTH = 256
VMEM_LIMIT = 56 * 1024 * 1024


def _in_proj_rows():
    widths = (("z", SSM_WIDTH), ("xbc", CONV_DIM), ("dt", SSM_HEADS), ("q", ATTN_WIDTH), ("k", KV_WIDTH),
              ("v", KV_WIDTH), ("za", ATTN_WIDTH))
    rows, start = {}, 0
    for name, width in widths:
        rows[name] = slice(start, start + width)
        start += width
    return rows


IN_ROWS = _in_proj_rows()
DT_ROWS = slice(IN_ROWS["dt"].start, IN_ROWS["dt"].start + LANES)

PROJ_ROWS = 512
CHUNKS_PER_STEP = 4
SAMPLE_BATCH_BLOCK = 8


def _dot(a, b):
    return jnp.dot(a, b, preferred_element_type=F32)


def _dot_nt(a, b):
    return lax.dot_general(a, b, (((1,), (1,)), ((), ())), preferred_element_type=F32)


def _dot_tn(a, b):
    return lax.dot_general(a, b, (((0,), (0,)), ((), ())), preferred_element_type=F32)


def _split2(v):
    hi = v.astype(BF16)
    lo = (v - hi.astype(F32)).astype(BF16)
    return hi, lo


def _dot_sel(v, m):
    hi, lo = _split2(v)
    if 2 * v.shape[1] <= MXU_WIDTH:
        return _dot(jnp.concatenate([hi, lo], axis=1), jnp.concatenate([m, m], axis=0))
    return _dot(hi, m) + _dot(lo, m)


def _dot_sel3(m, v):
    hi = v.astype(BF16)
    r1 = v - hi.astype(F32)
    mid = r1.astype(BF16)
    lo = (r1 - mid.astype(F32)).astype(BF16)
    return _dot(m, hi) + _dot(m, mid) + _dot(m, lo)


def _silu(x):
    return x / (1.0 + jnp.exp(-x))


def _softplus(x):
    return jnp.maximum(x, 0.0) + jnp.log1p(jnp.exp(-jnp.abs(x)))


def _params(*sem):
    return pltpu.CompilerParams(dimension_semantics=sem, vmem_limit_bytes=VMEM_LIMIT)


def _const_spec(shape):
    nd = len(shape)
    return pl.BlockSpec(shape, lambda *_: (0,) * nd)


def _normed_input(x_ref, nw_ref):
    x = x_ref[...]
    ms = jnp.mean(x * x, axis=-1, keepdims=True)
    return (x * lax.rsqrt(ms + EPS) * nw_ref[...]).astype(BF16)


def _dt_projection(h, wt_ref):
    raw = _dot_nt(h, wt_ref[DT_ROWS, :])
    return jnp.where(lax.broadcasted_iota(jnp.int32, raw.shape, 1) < SSM_HEADS, raw, 0.0)


def _shift_rows(u, prev_tail, k):
    rows, width = u.shape
    tiles = jnp.concatenate([prev_tail, u], axis=0).reshape(rows // SUBLANES + 1, SUBLANES, width)
    rot = jnp.concatenate([tiles[:, SUBLANES - k:], tiles[:, :SUBLANES - k]], axis=1)
    first = lax.broadcasted_iota(jnp.int32, (1, SUBLANES, width), 1) < k
    return jnp.where(first, rot[:-1], rot[1:]).reshape(rows, width)


def _inproj_prompt_body(steps_per_seq, n_state_blocks, x_ref, nw_ref, wt_ref, cw_ref, cb_ref, dtb_ref,
                        qwt_ref, kwt_ref, cdec_ref, st_ref, sc_ref, sb_ref, sxw_ref, wo_ref,
                        gz_ref, xs_ref, b_ref, c_ref, dt_ref, qt_ref, kt_ref, vt_ref, gzat_ref, tail_ref,
                        newst_ref, yoff_ref, wo16_ref, tail_sc):
    step = pl.program_id(0)

    @pl.when(step < n_state_blocks)
    def _():
        _ssd_sample_state_block(step, cdec_ref, st_ref, sc_ref, sb_ref, sxw_ref, newst_ref, yoff_ref)

    @pl.when(step >= n_state_blocks)
    def _():
        wo16_ref[...] = wo_ref[...].astype(BF16)

    @pl.when(step % steps_per_seq == 0)
    def _():
        tail_sc[...] = jnp.zeros_like(tail_sc)

    h = _normed_input(x_ref, nw_ref)
    rows = h.shape[0]
    w_tile = lambda name, j: wt_ref[IN_ROWS[name].start + j * MXU_WIDTH:IN_ROWS[name].start + (j + 1) * MXU_WIDTH, :]
    n_side = SSM_WIDTH // MXU_WIDTH
    for j in range(CONV_DIM // MXU_WIDTH):
        cols = slice(j * MXU_WIDTH, (j + 1) * MXU_WIDTH)
        if j < n_side:
            both = _dot_nt(h, jnp.concatenate([w_tile("xbc", j), w_tile("z", j)], axis=0))
            u = both[:, :MXU_WIDTH]
            gz_ref[:, cols] = _silu(both[:, MXU_WIDTH:])
        else:
            u = _dot_nt(h, w_tile("xbc", j))
        prev_tail = tail_sc[:, cols]
        conv = cb_ref[:, cols] + u * cw_ref[CONV_WIDTH - 1:CONV_WIDTH, cols]
        for k in range(1, CONV_WIDTH):
            tap = CONV_WIDTH - 1 - k
            conv = conv + _shift_rows(u, prev_tail, k) * cw_ref[tap:tap + 1, cols]
        new_tail = u[rows - SUBLANES:, :]
        tail_sc[:, cols] = new_tail
        tail_ref[0, :, cols] = new_tail
        act = _silu(conv)
        if j < SSM_WIDTH // MXU_WIDTH:
            xs_ref[:, cols] = act
        elif j == SSM_WIDTH // MXU_WIDTH:
            b_ref[...] = act.astype(BF16)
        else:
            c_ref[...] = act.astype(BF16)

    for j in range(ATTN_WIDTH // MXU_WIDTH):
        feats = slice(j * MXU_WIDTH, (j + 1) * MXU_WIDTH)
        gzat_ref[feats, :] = _silu(_dot_nt(w_tile("za", j), h))
    dt_ref[...] = _softplus(_dt_projection(h, wt_ref) + dtb_ref[...])

    def head_norm(t, w):
        x3 = t.reshape(-1, ATTN_HEAD_DIM, rows)
        ms = jnp.mean(x3 * x3, axis=1, keepdims=True)
        return (x3 * lax.rsqrt(ms + EPS)).reshape(t.shape) * w

    for j in range(ATTN_WIDTH // MXU_WIDTH):
        feats = slice(j * MXU_WIDTH, (j + 1) * MXU_WIDTH)
        qt_ref[feats, :] = head_norm(_dot_nt(w_tile("q", j), h), qwt_ref[feats, :]).astype(BF16)
    kt_ref[...] = head_norm(_dot_nt(wt_ref[IN_ROWS["k"], :], h), kwt_ref[...])
    vt_ref[...] = _dot_nt(wt_ref[IN_ROWS["v"], :], h)


def _inproj_prompt(x2d, norm_w, w_t, conv_w, conv_b, dtb, qw_t, kw_t, batch, seq, cdec_flat, state, c3, b3, xw3,
                   w_out):
    rows = x2d.shape[0]
    tm = PROJ_ROWS
    steps_per_seq = seq // tm
    steps, nb = c3.shape[0], c3.shape[1]
    bb = SAMPLE_BATCH_BLOCK
    n_state_blocks = nb // bb
    n_steps = rows // tm
    assert n_state_blocks < n_steps, "state blocks and weight-cast blocks share the in-proj grid steps"
    last = n_state_blocks - 1
    cast_rows = w_out.shape[0] // (n_steps - n_state_blocks)
    assert cast_rows * (n_steps - n_state_blocks) == w_out.shape[0] and cast_rows % (2 * SUBLANES) == 0
    wo_spec = pl.BlockSpec((cast_rows, D_MODEL), lambda i: (jnp.maximum(i - n_state_blocks, 0), 0))
    resident = lambda a: pl.BlockSpec(a.shape, lambda i: (0, 0), pipeline_mode=pl.Buffered(1))
    rowblk = lambda w: pl.BlockSpec((tm, w), lambda i: (i, 0))
    colblk = pl.BlockSpec((ATTN_WIDTH, tm), lambda i: (0, i))
    kvblk = pl.BlockSpec((KV_WIDTH, tm), lambda i: (0, i))
    tok = lambda w: pl.BlockSpec((steps, bb, w), lambda i: (0, jnp.minimum(i, last), 0))
    st_spec = pl.BlockSpec((bb, SSM_WIDTH, SSM_STATE), lambda i: (jnp.minimum(i, last), 0, 0))
    in_specs = ([rowblk(D_MODEL), _const_spec((1, D_MODEL)), resident(w_t)]
                + [_const_spec(conv_w.shape), _const_spec(conv_b.shape), _const_spec(dtb.shape)]
                + [resident(qw_t), resident(kw_t)]
                + [pl.BlockSpec(memory_space=pltpu.SMEM), st_spec, tok(BC_WIDTH), tok(BC_WIDTH), tok(SSM_WIDTH),
                   wo_spec])
    out_specs = [rowblk(SSM_WIDTH), rowblk(SSM_WIDTH), rowblk(BC_WIDTH), rowblk(BC_WIDTH), rowblk(LANES),
                 colblk, kvblk, kvblk, colblk,
                 pl.BlockSpec((1, SUBLANES, CONV_DIM), lambda i: (i // steps_per_seq, 0, 0)),
                 st_spec, tok(SSM_WIDTH), wo_spec]
    f = lambda r, c, dt=F32: jax.ShapeDtypeStruct((r, c), dt)
    out_shape = [f(rows, SSM_WIDTH), f(rows, SSM_WIDTH), f(rows, BC_WIDTH, BF16), f(rows, BC_WIDTH, BF16),
                 f(rows, LANES), f(ATTN_WIDTH, rows, BF16), f(KV_WIDTH, rows), f(KV_WIDTH, rows), f(ATTN_WIDTH, rows),
                 jax.ShapeDtypeStruct((batch, SUBLANES, CONV_DIM), F32),
                 jax.ShapeDtypeStruct(state.shape, F32), jax.ShapeDtypeStruct((steps, nb, SSM_WIDTH), F32),
                 jax.ShapeDtypeStruct(w_out.shape, BF16)]
    return pl.pallas_call(
        functools.partial(_inproj_prompt_body, steps_per_seq, n_state_blocks), grid=(n_steps,),
        in_specs=in_specs, out_specs=out_specs, out_shape=out_shape,
        scratch_shapes=[pltpu.VMEM((SUBLANES, CONV_DIM), F32)],
        compiler_params=_params("arbitrary"), name="inproj_prompt")(
            x2d, norm_w, w_t, conv_w, conv_b, dtb, qw_t, kw_t, cdec_flat, state, c3, b3, xw3, w_out)


def _group_rmsnorm(gy, norm_w):
    parts = []
    for g in range(SSM_GROUPS):
        blk = gy[:, g * GROUP_WIDTH:(g + 1) * GROUP_WIDTH]
        ms = jnp.mean(blk * blk, axis=-1, keepdims=True)
        parts.append(blk * lax.rsqrt(ms + EPS))
    return jnp.concatenate(parts, axis=1) * norm_w


def _ssd_chunk(gz, xs, b_bf, c_bf, dt, a_neg, dskip, norm_w, e_mat, state):
    xs_bf = xs.astype(BF16)

    a = dt * a_neg
    li = lax.broadcasted_iota(jnp.int32, (CHUNK, CHUNK), 0)
    si = lax.broadcasted_iota(jnp.int32, (CHUNK, CHUNK), 1)
    causal = li >= si
    a_cum = _dot_sel3(jnp.where(causal, 1.0, 0.0).astype(BF16), a)
    a2 = a_cum * LOG2E
    row_term = a2.T - jnp.log2(dt.T)
    ea_full = _dot_sel(jnp.exp(a_cum), e_mat)
    w_full = _dot((dt * jnp.exp(a_cum[CHUNK - 1:CHUNK, :] - a_cum)).astype(BF16), e_mat)

    cb = [_dot_nt(c_bf[:, g * SSM_STATE:(g + 1) * SSM_STATE], b_bf[:, g * SSM_STATE:(g + 1) * SSM_STATE])
          for g in range(SSM_GROUPS)]
    half = lax.broadcasted_iota(jnp.int32, (CHUNK, LANES), 1) < SSM_HEAD_DIM
    heads_per_group = SSM_HEADS // SSM_GROUPS
    y_parts = []
    for pair in range(SSM_HEADS // 2):
        blocks = []
        for h in (2 * pair, 2 * pair + 1):
            seg = a2[:, h:h + 1] - row_term[h:h + 1, :]
            decay_dt = jnp.exp2(jnp.where(causal, seg, -jnp.inf))
            blocks.append((cb[h // heads_per_group] * decay_dt).astype(BF16))
        lhs = jnp.concatenate(blocks, axis=1)
        xp = xs_bf[:, pair * LANES:(pair + 1) * LANES]
        zero = jnp.zeros_like(xp)
        rhs = jnp.concatenate([jnp.where(half, xp, zero), jnp.where(half, zero, xp)], axis=0)
        y_parts.append(_dot(lhs, rhs))
    y_diag = jnp.concatenate(y_parts, axis=1)

    state_bf = state.astype(BF16)
    xw_bf = (xs * w_full).astype(BF16)
    y_off, upd = [], []
    for g in range(SSM_GROUPS):
        cols = slice(g * GROUP_WIDTH, (g + 1) * GROUP_WIDTH)
        ns = slice(g * SSM_STATE, (g + 1) * SSM_STATE)
        y_off.append(_dot(c_bf[:, ns], state_bf[:, cols]))
        upd.append(_dot_tn(b_bf[:, ns], xw_bf[:, cols]))
    y = y_diag + jnp.concatenate(y_off, axis=1) * ea_full + dskip * xs
    new_state = state * ea_full[CHUNK - 1:CHUNK, :] + jnp.concatenate(upd, axis=1)
    return _group_rmsnorm(y * gz, norm_w), new_state


def _rel_bucket_np(dist):
    max_exact = REL_BUCKETS // 2
    d_f = np.maximum(dist, 1).astype(np.float32)
    large = max_exact + (np.log(d_f / np.float32(max_exact)) / np.float32(math.log(REL_MAX_DIST / max_exact))
                         * np.float32(REL_BUCKETS - max_exact)).astype(np.int32)
    return np.where(dist < max_exact, dist, np.minimum(large, REL_BUCKETS - 1)).astype(np.int32)


def _bucket_or_masked(dist, extra_mask=None):
    ok = (dist >= 0) & (dist <= WINDOW)
    if extra_mask is not None:
        ok = ok & extra_mask
    return np.where(ok, _rel_bucket_np(np.clip(dist, 0, WINDOW)), -1).astype(np.int32)


def _sample_bias_tables(rel_ref, sink_ref, cache_bucket_ref, new_bucket_ref, cache_out_ref, new_out_ref):
    steps = cache_bucket_ref.shape[0]
    cache_bucket = cache_bucket_ref[...]
    new_bucket = new_bucket_ref[...]
    sink_column = lax.broadcasted_iota(jnp.int32, new_bucket.shape, 1) == steps
    for h in range(ATTN_HEADS):
        cache_bias = jnp.full(cache_bucket.shape, NEG, F32)
        new_bias = jnp.full(new_bucket.shape, NEG, F32)
        for bkt in range(REL_BUCKETS):
            entry = rel_ref[bkt * ATTN_HEADS + h]
            cache_bias = jnp.where(cache_bucket == bkt, entry, cache_bias)
            new_bias = jnp.where(new_bucket == bkt, entry, new_bias)
        new_bias = jnp.where(sink_column, sink_ref[h], new_bias)
        for bb in range(SAMPLE_BATCH_BLOCK):
            row = (bb * ATTN_HEADS + h) * steps
            cache_out_ref[row:row + steps, :] = cache_bias
            new_out_ref[row:row + steps, :] = new_bias


def _bias_body(rel_log2_ref, rel_ref, sink_ref, dist_bucket_ref, cache_bucket_ref, new_bucket_ref,
               o_ref, cache_out_ref, new_out_ref):
    _sample_bias_tables(rel_ref, sink_ref, cache_bucket_ref, new_bucket_ref, cache_out_ref, new_out_ref)
    T = CHUNK
    width = dist_bucket_ref.shape[1]
    bucket = dist_bucket_ref[...]
    own_block = lax.broadcasted_iota(jnp.int32, (2 * T, T), 0) >= T

    def per_kv_head(n, carry):
        for g in range(Q_PER_KV):
            profile = jnp.full(bucket.shape, NEG, F32)
            for bkt in range(REL_BUCKETS):
                profile = jnp.where(bucket == bkt, rel_log2_ref[bkt * ATTN_HEADS + n * Q_PER_KV + g], profile)
            rows = jnp.broadcast_to(profile[0:1, :], (2 * T, width))
            tile = pltpu.roll(rows, width - T, axis=1, stride=1, stride_axis=0)[:, :T]
            o_ref[1, n, :, g * T:(g + 1) * T] = tile
            o_ref[0, n, :, g * T:(g + 1) * T] = jnp.where(own_block, tile, NEG)
        return carry

    lax.fori_loop(0, ATTN_KV_HEADS, per_kv_head, 0)


def _bias_tables(rel_flat, sink, dist_bucket, cache_bucket, new_bucket):
    steps = cache_bucket.shape[0]
    rows = SAMPLE_BATCH_BLOCK * ATTN_HEADS * steps
    f = lambda *s: jax.ShapeDtypeStruct(s, F32)
    smem, vmem = pl.BlockSpec(memory_space=pltpu.SMEM), pl.BlockSpec(memory_space=pltpu.VMEM)
    return pl.pallas_call(
        _bias_body,
        in_specs=[smem, smem, smem, vmem, vmem, vmem],
        out_specs=[vmem, vmem, vmem],
        out_shape=[f(2, ATTN_KV_HEADS, 2 * CHUNK, Q_PER_KV * CHUNK), f(rows, cache_bucket.shape[1]),
                   f(rows, new_bucket.shape[1])],
        compiler_params=pltpu.CompilerParams(vmem_limit_bytes=VMEM_LIMIT), name="rel_bias")(
        rel_flat * LOG2E, rel_flat, sink, dist_bucket, cache_bucket, new_bucket)


def _head_rmsnorm(x, g_mat, e_mat, w):
    ms = _dot_sel(x * x, g_mat) * (1.0 / ATTN_HEAD_DIM)
    return x * _dot_sel(lax.rsqrt(ms + EPS), e_mat) * w


def _lane_head(shape):
    return lax.broadcasted_iota(jnp.int32, shape, 1) // ATTN_HEAD_DIM


def _attn_block(q_blk, kcat, vcat_t, bias_at, sink_ref):
    T = CHUNK
    lane_head = _lane_head((2 * T, KV_WIDTH))
    zero = jnp.zeros((2 * T, KV_WIDTH), BF16)
    head = lambda h: q_blk[h * ATTN_HEAD_DIM:(h + 1) * ATTN_HEAD_DIM]
    q_cols = jnp.concatenate(
        [jnp.concatenate([head(n * Q_PER_KV + g) for n in range(ATTN_KV_HEADS)], axis=0)
         for g in range(Q_PER_KV)], axis=1)
    row_head = lax.broadcasted_iota(jnp.int32, (KV_WIDTH, 2 * T), 0) // ATTN_HEAD_DIM
    probs, vals, inv = [], [], {}
    for n in range(ATTN_KV_HEADS):
        s = _dot(jnp.where(lane_head == n, kcat, zero), q_cols)
        sink = sink_ref[n]
        cols = []
        for g in range(Q_PER_KV):
            c = slice(g * T, (g + 1) * T)
            sg = s[:, c] + bias_at(n, c)
            m = jnp.maximum(jnp.max(sg, axis=0, keepdims=True), sink[:, c])
            p = jnp.exp2(sg - m)
            inv[n, g] = 1.0 / (jnp.sum(p, axis=0, keepdims=True) + jnp.exp2(sink[:, c] - m))
            cols.append(p.astype(BF16))
        probs.append(jnp.concatenate(cols, axis=1))
        vals.append(jnp.where(row_head == n, vcat_t, zero.T))
    o_t = _dot(jnp.concatenate(vals, axis=1), jnp.concatenate(probs, axis=0))
    return jnp.concatenate(
        [o_t[n * ATTN_HEAD_DIM:(n + 1) * ATTN_HEAD_DIM, g * T:(g + 1) * T] * inv[n, g]
         for n in range(ATTN_KV_HEADS) for g in range(Q_PER_KV)], axis=0)


def _mixer_body(gz_ref, xs_ref, b_ref, c_ref, dt_ref, alog_ref, dskip_ref, nw_ref, e_ref,
                qt_ref, kt_ref, vt_ref, gzt_ref, bias_ref, sink_ref, x_ref, wo_ref,
                y_ref, st_ref, knt_ref, vnt_ref,
                state_sc, kcat_sc, vcat_t_sc, yssm_sc, yattn_t_sc):
    T = CHUNK
    step = pl.program_id(1)
    cols_step = CHUNKS_PER_STEP * T

    @pl.when(step == 0)
    def _():
        state_sc[...] = jnp.zeros_like(state_sc)
        kcat_sc[0:T, :] = jnp.zeros((T, KV_WIDTH), BF16)
        vcat_t_sc[:, 0:T] = jnp.zeros((KV_WIDTH, T), BF16)

    qn = qt_ref[...]
    kn_t = kt_ref[...]
    v_t = vt_ref[...]
    knt_ref[0] = kn_t[:, cols_step - T:]
    vnt_ref[0] = v_t[:, cols_step - T:]
    kcat_sc[T:, :] = kn_t.T.astype(BF16)
    vcat_t_sc[:, T:] = v_t.astype(BF16)
    e_mat = e_ref[...]

    a_neg = -jnp.exp(alog_ref[...])
    state = state_sc[...]
    first_variant = jnp.minimum(step, 1)
    for j in range(CHUNKS_PER_STEP):
        r = slice(j * T, (j + 1) * T)
        y, state = _ssd_chunk(gz_ref[r, :], xs_ref[r, :], b_ref[r, :], c_ref[r, :], dt_ref[r, :], a_neg,
                              dskip_ref[...], nw_ref[...], e_mat, state)
        yssm_sc[r, :] = y.astype(BF16)
        variant = first_variant if j == 0 else 1
        y_t = _attn_block(qn[:, r], kcat_sc[j * T:(j + 2) * T, :], vcat_t_sc[:, j * T:(j + 2) * T],
                          lambda n, c, variant=variant: bias_ref[variant, n, :, c], sink_ref)
        yattn_t_sc[:, r] = (y_t * gzt_ref[:, r]).astype(BF16)
    state_sc[...] = state
    kcat_sc[0:T, :] = kcat_sc[cols_step:, :]
    vcat_t_sc[:, 0:T] = vcat_t_sc[:, cols_step:]

    y_ref[...] = (x_ref[...] + _dot(yssm_sc[...], wo_ref[:SSM_WIDTH, :])
                  + _dot_tn(yattn_t_sc[...], wo_ref[SSM_WIDTH:, :]))

    @pl.when(step == pl.num_programs(1) - 1)
    def _():
        st_ref[0] = state.T


def _mixer(gz, xs, b, c, dt, alog, dskip, norm_w, e_mat, q_t, k_t, v_t, gz_t, bias_t, sink_rows,
           x2d, w_out, batch, seq):
    step_rows = CHUNKS_PER_STEP * CHUNK
    ns = seq // step_rows
    row = lambda w: pl.BlockSpec((step_rows, w), lambda b, i: (b * ns + i, 0))
    col = lambda w: pl.BlockSpec((w, step_rows), lambda b, i: (0, b * ns + i))
    resident = lambda a: pl.BlockSpec(a.shape, lambda b, i: (0,) * a.ndim, pipeline_mode=pl.Buffered(1))
    in_specs = [row(SSM_WIDTH), row(SSM_WIDTH), row(BC_WIDTH), row(BC_WIDTH), row(LANES),
                _const_spec((1, LANES)), _const_spec((1, SSM_WIDTH)), _const_spec((1, SSM_WIDTH)), resident(e_mat),
                col(ATTN_WIDTH), col(KV_WIDTH), col(KV_WIDTH), col(ATTN_WIDTH),
                resident(bias_t), _const_spec(sink_rows.shape), row(D_MODEL), resident(w_out)]
    kv_out = pl.BlockSpec((1, KV_WIDTH, CHUNK), lambda b, i: (b, 0, 0))
    out_specs = [row(D_MODEL), pl.BlockSpec((1, SSM_WIDTH, SSM_STATE), lambda b, i: (b, 0, 0)), kv_out, kv_out]
    out_shape = [jax.ShapeDtypeStruct((batch * seq, D_MODEL), F32),
                 jax.ShapeDtypeStruct((batch, SSM_WIDTH, SSM_STATE), F32),
                 jax.ShapeDtypeStruct((batch, KV_WIDTH, CHUNK), F32),
                 jax.ShapeDtypeStruct((batch, KV_WIDTH, CHUNK), F32)]
    scratch = [pltpu.VMEM((SSM_STATE, SSM_WIDTH), F32),
               pltpu.VMEM((step_rows + CHUNK, KV_WIDTH), BF16), pltpu.VMEM((KV_WIDTH, step_rows + CHUNK), BF16),
               pltpu.VMEM((step_rows, SSM_WIDTH), BF16), pltpu.VMEM((ATTN_WIDTH, step_rows), BF16)]
    return pl.pallas_call(
        _mixer_body, grid=(batch, ns), in_specs=in_specs, out_specs=out_specs, out_shape=out_shape,
        scratch_shapes=scratch, compiler_params=_params("arbitrary", "arbitrary"), name="mixer")(
            gz, xs, b, c, dt, alog, dskip, norm_w, e_mat, q_t, k_t, v_t, gz_t, bias_t, sink_rows, x2d, w_out)


def _sample_front_body(x_ref, nw_ref, wt_ref, sconv_ref, cw_ref, cb_ref, dtb_ref, alog_ref, dskip_ref,
                       gh_ref, e_ref, qw_ref, kw_ref, g_ref,
                       ypart_ref, ea_ref, xw_ref, b_ref, c_ref, cdec_ref, convnew_ref,
                       gz_ref, qn_ref, kn_ref, v_ref, za_ref):
    steps, nb = ypart_ref.shape[0], ypart_ref.shape[1]
    tail = CONV_WIDTH - 1
    h = _normed_input(x_ref, nw_ref)
    proj = lambda name: _dot_nt(h, wt_ref[IN_ROWS[name], :])
    slab = lambda a, l: a[l * nb:(l + 1) * nb]
    xbc = proj("xbc")
    dt_raw = _dt_projection(h, wt_ref)
    gz_ref[...] = _silu(proj("z"))
    v_ref[...] = proj("v")
    za_ref[...] = proj("za")
    g_mat = g_ref[...]
    e_mat = e_ref[...]
    qn = _head_rmsnorm(proj("q"), g_mat, e_mat, qw_ref[...]) * ATTN_SCALE
    head = lambda hd: qn[:, hd * ATTN_HEAD_DIM:(hd + 1) * ATTN_HEAD_DIM]
    qn_ref[...] = jnp.concatenate(
        [head(n * Q_PER_KV + g) for g in range(Q_PER_KV) for n in range(ATTN_KV_HEADS)], axis=1)
    kn_ref[...] = _head_rmsnorm(proj("k"), g_mat[:KV_WIDTH], e_mat[:, :KV_WIDTH], kw_ref[...])

    full = [sconv_ref[j] for j in range(tail)] + [slab(xbc, l) for l in range(steps)]
    for j in range(tail):
        convnew_ref[j] = full[steps + j]
    gh = gh_ref[...]
    a_neg = -jnp.exp(alog_ref[...])
    xs, bm, cm, dts, acum = [], [], [], [], []
    run = None
    for l in range(steps):
        conv = cb_ref[...]
        for tap in range(CONV_WIDTH):
            conv = conv + full[l + tap] * cw_ref[tap:tap + 1, :]
        act = _silu(conv)
        xs.append(act[:, :SSM_WIDTH])
        bm.append(act[:, SSM_WIDTH:SSM_WIDTH + BC_WIDTH])
        cm.append(act[:, SSM_WIDTH + BC_WIDTH:])
        d = _softplus(slab(dt_raw, l) + dtb_ref[...])
        dts.append(d)
        run = d * a_neg if run is None else run + d * a_neg
        acum.append(run)
        b_ref[l] = bm[l]
        c_ref[l] = cm[l]
    for l in range(steps):
        y = dskip_ref[...] * xs[l]
        for s in range(l + 1):
            cb_h = _dot_sel(cm[l] * bm[s], gh)
            coef = cb_h * jnp.exp(acum[l] - acum[s]) * dts[s]
            y = y + _dot_sel(coef, e_mat) * xs[s]
        ypart_ref[l] = y
        ea_ref[l] = _dot_sel(jnp.exp(acum[l]), e_mat)
        xw_ref[l] = xs[l] * _dot_sel(dts[l] * jnp.exp(acum[steps - 1] - acum[l]), e_mat)
    cdec_ref[...] = jnp.exp(acum[steps - 1])


def _sample_front(x2d, norm_w, w_t, sconv3, conv_w, conv_b, dtb, alog, dskip, gh_mat, e_mat, qw, kw, g_mat,
                  steps, nb):
    rows = steps * nb
    f = lambda *s: jax.ShapeDtypeStruct(s, F32)
    out_shape = [f(steps, nb, SSM_WIDTH), f(steps, nb, SSM_WIDTH), f(steps, nb, SSM_WIDTH),
                 f(steps, nb, BC_WIDTH), f(steps, nb, BC_WIDTH), f(nb, LANES), f(CONV_WIDTH - 1, nb, CONV_DIM),
                 f(rows, SSM_WIDTH), f(rows, ATTN_WIDTH), f(rows, KV_WIDTH), f(rows, KV_WIDTH), f(rows, ATTN_WIDTH)]
    return pl.pallas_call(_sample_front_body, out_shape=out_shape,
                          compiler_params=pltpu.CompilerParams(vmem_limit_bytes=VMEM_LIMIT),
                          name="sample_front")(
        x2d, norm_w, w_t, sconv3, conv_w, conv_b, dtb, alog, dskip, gh_mat, e_mat, qw, kw, g_mat)


def _ssd_sample_state_block(i, cdec_ref, st_ref, c_ref, b_ref, xw_ref, new_ref, yoff_ref):
    heads_per_group = SSM_HEADS // SSM_GROUPS
    for j in range(SAMPLE_BATCH_BLOCK):
        st = st_ref[j]
        cb_bf = c_ref[:, j, :].astype(BF16)
        bb_bf = b_ref[:, j, :].astype(BF16)
        xw_bf = xw_ref[:, j, :].astype(BF16)
        y_parts = []
        for g in range(SSM_GROUPS):
            rows = slice(g * GROUP_WIDTH, (g + 1) * GROUP_WIDTH)
            ns = slice(g * SSM_STATE, (g + 1) * SSM_STATE)
            y_parts.append(_dot_nt(cb_bf[:, ns], st[rows].astype(BF16)))
            upd = _dot_tn(xw_bf[:, rows], bb_bf[:, ns])
            for hh in range(heads_per_group):
                h = g * heads_per_group + hh
                r = slice(h * SSM_HEAD_DIM, (h + 1) * SSM_HEAD_DIM)
                dec = cdec_ref[(i * SAMPLE_BATCH_BLOCK + j) * SSM_HEADS + h]
                new_ref[j, r, :] = st[r] * dec + upd[hh * SSM_HEAD_DIM:(hh + 1) * SSM_HEAD_DIM]
        yoff_ref[:, j, :] = jnp.concatenate(y_parts, axis=1)


def _sample_back_body(ypart_ref, yoff_ref, ea_ref, gz_ref, nw_ref, ya_ref, x_ref, wo_ref, o_ref):
    y = ypart_ref[...] + yoff_ref[...] * ea_ref[...]
    y_ssm = _group_rmsnorm(y * gz_ref[...], nw_ref[...])
    o_ref[...] = (x_ref[...] + _dot(y_ssm.astype(BF16), wo_ref[:SSM_WIDTH, :])
                  + _dot(ya_ref[...].astype(BF16), wo_ref[SSM_WIDTH:, :]))


def _sample_back(ypart, yoff, ea, gz, norm_w, y_attn, x2d, w_out):
    return pl.pallas_call(_sample_back_body, out_shape=jax.ShapeDtypeStruct(x2d.shape, F32),
                          compiler_params=pltpu.CompilerParams(vmem_limit_bytes=VMEM_LIMIT),
                          name="sample_back")(ypart, yoff, ea, gz, norm_w, y_attn, x2d, w_out)


def _attn_sample_body(q_ref, kn_ref, vn_ref, z_ref, ckt_ref, cvt_ref, biasc_ref, biasn_ref,
                      y_ref, kot_ref, vot_ref):
    steps = q_ref.shape[0]
    bb = SAMPLE_BATCH_BLOCK
    blk = Q_PER_KV * steps
    rows = ATTN_KV_HEADS * blk
    pad = jnp.zeros((SUBLANES - steps, KV_WIDTH), F32)
    lane_head = _lane_head((blk, KV_WIDTH))
    zero = jnp.zeros((blk, KV_WIDTH), F32)

    s_c, s_n, k8, v8 = [], [], [], []
    for j in range(bb):
        q = q_ref[:, j, :]
        qg = jnp.concatenate([q[:, g * KV_WIDTH:(g + 1) * KV_WIDTH] for g in range(Q_PER_KV)], axis=0)
        qx = jnp.concatenate([jnp.where(lane_head == n, qg, zero) for n in range(ATTN_KV_HEADS)], axis=0)
        qx = qx.astype(BF16)
        k8.append(jnp.concatenate([kn_ref[:, j, :], pad], axis=0))
        v8.append(jnp.concatenate([vn_ref[:, j, :], pad], axis=0))
        s_c.append(_dot(qx, ckt_ref[j].astype(BF16)))
        s_n.append(_dot_nt(qx, k8[j].astype(BF16)))
    s_c = jnp.concatenate(s_c, axis=0) + biasc_ref[...]
    s_n = jnp.concatenate(s_n, axis=0) + biasn_ref[...]
    m = jnp.maximum(jnp.max(s_c, axis=-1, keepdims=True), jnp.max(s_n, axis=-1, keepdims=True))
    p_c = jnp.exp(s_c - m)
    p_n = jnp.exp(s_n - m)
    inv = 1.0 / (jnp.sum(p_c, axis=-1, keepdims=True) + jnp.sum(p_n, axis=-1, keepdims=True))
    p_c = (p_c * inv).astype(BF16)
    p_n = (p_n * inv).astype(BF16)

    lane = lax.broadcasted_iota(jnp.int32, (KV_WIDTH, WINDOW), 1)
    for j in range(bb):
        r = slice(j * rows, (j + 1) * rows)
        o = _dot_nt(p_c[r], cvt_ref[j].astype(BF16)) + _dot(p_n[r], v8[j].astype(BF16))
        og = zero
        for n in range(ATTN_KV_HEADS):
            og = og + jnp.where(lane_head == n, o[n * blk:(n + 1) * blk], zero)
        y = jnp.concatenate(
            [og[g * steps:(g + 1) * steps, n * ATTN_HEAD_DIM:(n + 1) * ATTN_HEAD_DIM]
             for n in range(ATTN_KV_HEADS) for g in range(Q_PER_KV)], axis=1)
        y_ref[:, j, :] = y * _silu(z_ref[:, j, :])

        for new8, old_ref, out_ref in ((k8[j], ckt_ref, kot_ref), (v8[j], cvt_ref, vot_ref)):
            tail_rows = jnp.concatenate([new8[steps:], new8[:steps]], axis=0)
            block = jnp.concatenate([jnp.zeros((WINDOW - SUBLANES, KV_WIDTH), F32), tail_rows], axis=0)
            shifted = pltpu.roll(old_ref[j], WINDOW - steps, axis=1)
            out_ref[j] = jnp.where(lane >= WINDOW - steps, block.T, shifted)


def _attn_sample(q3, kn3, vn3, z3, cache_kt, cache_vt, bias_c, bias_n):
    steps, nb = q3.shape[0], q3.shape[1]
    bb = SAMPLE_BATCH_BLOCK
    tok = lambda w: pl.BlockSpec((steps, bb, w), lambda i: (0, i, 0))
    cache_spec = pl.BlockSpec((bb, KV_WIDTH, WINDOW), lambda i: (i, 0, 0))
    return pl.pallas_call(
        _attn_sample_body, grid=(nb // bb,),
        in_specs=[tok(ATTN_WIDTH), tok(KV_WIDTH), tok(KV_WIDTH), tok(ATTN_WIDTH), cache_spec, cache_spec,
                  _const_spec(bias_c.shape), _const_spec(bias_n.shape)],
        out_specs=[tok(ATTN_WIDTH), cache_spec, cache_spec],
        out_shape=[jax.ShapeDtypeStruct((steps, nb, ATTN_WIDTH), F32),
                   jax.ShapeDtypeStruct(cache_kt.shape, F32), jax.ShapeDtypeStruct(cache_vt.shape, F32)],
        compiler_params=_params("parallel"), name="attn_sample")(
            q3, kn3, vn3, z3, cache_kt, cache_vt, bias_c, bias_n)


def _static_tables(steps):
    lanes = np.arange(ATTN_WIDTH)
    g_mat = np.zeros((ATTN_WIDTH, LANES), np.float32)
    g_mat[lanes, lanes // ATTN_HEAD_DIM] = 1.0
    e_mat = g_mat.T.copy()
    bc = np.arange(BC_WIDTH)
    gh_mat = np.zeros((BC_WIDTH, LANES), np.float32)
    for h in range(SSM_HEADS):
        gh_mat[bc // SSM_STATE == h // (SSM_HEADS // SSM_GROUPS), h] = 1.0
    assert WINDOW <= CHUNK
    prompt_buckets = np.broadcast_to(_bucket_or_masked(np.arange(2 * CHUNK))[None, :], (SUBLANES, 2 * CHUNK))
    dist_c = (np.arange(steps) + WINDOW)[:, None] - np.arange(WINDOW)[None, :]
    dist_n = np.arange(steps)[:, None] - np.arange(SUBLANES)[None, :]
    real = np.broadcast_to((np.arange(SUBLANES) < steps)[None, :], dist_n.shape)
    return dict(g=g_mat, e=e_mat, gh=gh_mat, prompt_buckets=prompt_buckets,
                cache_buckets=_bucket_or_masked(dist_c), new_buckets=_bucket_or_masked(dist_n, real))


def kernel(x_prompt, x_sample, cache_k, cache_v, state_ssm, state_conv, norm_w, w_in, conv_w, conv_b, dt_bias,
           a_log, d_skip, ssm_norm_w, q_norm_w, k_norm_w, sinks, rel_table, w_out):
    assert w_in.shape[0] == 1, "single-layer kernel"
    batch, seq, _ = x_prompt.shape
    nb, steps, _ = x_sample.shape
    tab = _static_tables(steps)
    g_mat = jnp.asarray(tab["g"], BF16)
    e_mat = jnp.asarray(tab["e"], BF16)
    gh_mat = jnp.asarray(tab["gh"], BF16)

    w_t = jnp.transpose(w_in[0]).astype(BF16)

    row = lambda v, width: jnp.pad(v.reshape(1, -1), ((0, 0), (0, width - v.size)))
    nw = row(norm_w[0], D_MODEL)
    cw = conv_w[0]
    cb = row(conv_b[0], CONV_DIM)
    dtb = row(dt_bias[0], LANES)
    alog = row(a_log[0], LANES)
    dskip = jnp.repeat(d_skip[0], SSM_HEAD_DIM).reshape(1, SSM_WIDTH)
    snw = row(ssm_norm_w[0], SSM_WIDTH)
    qw = jnp.tile(q_norm_w[0], ATTN_HEADS).reshape(1, ATTN_WIDTH)
    kw = jnp.tile(k_norm_w[0], ATTN_KV_HEADS).reshape(1, KV_WIDTH)
    sink = sinks[0]
    rel_flat = rel_table.reshape(-1)

    xs = jnp.swapaxes(x_sample, 0, 1).reshape(steps * nb, D_MODEL)
    t3 = lambda a: a.reshape(steps, nb, a.shape[-1])
    sconv3 = jnp.swapaxes(state_conv[0], 0, 1)
    ypart, ea, xw, b3, c3, cdec, conv_s3, gz_smp, qn, kn, v_smp, za = _sample_front(
        xs, nw, w_t, sconv3, cw, cb, dtb, alog, dskip, gh_mat, e_mat, qw, kw, g_mat, steps, nb)
    state_in = state_ssm[0].reshape(nb, SSM_WIDTH, SSM_STATE)

    xp = x_prompt.reshape(batch * seq, D_MODEL)
    qw_t = jnp.broadcast_to((qw * (ATTN_SCALE * LOG2E)).reshape(ATTN_WIDTH, 1), (ATTN_WIDTH, PROJ_ROWS))
    kw_t = jnp.broadcast_to(kw.reshape(KV_WIDTH, 1), (KV_WIDTH, PROJ_ROWS))
    gz, xs_p, b_p, c_p, dt_p, q_t, k_t, v_t, gza_t, tail_p, st_s, yoff, wo_all = _inproj_prompt(
        xp, nw, w_t, cw, cb, dtb, qw_t, kw_t, batch, seq, cdec[:, :SSM_HEADS].reshape(-1), state_in, c3, b3, xw,
        w_out[0])
    bias_t, bias_c, bias_n = _bias_tables(rel_flat, sink, jnp.asarray(tab["prompt_buckets"]),
                                          jnp.asarray(tab["cache_buckets"]), jnp.asarray(tab["new_buckets"]))
    sink_rows = jnp.repeat(sink.reshape(ATTN_KV_HEADS, Q_PER_KV) * LOG2E, CHUNK, axis=1)
    sink_rows = sink_rows.reshape(ATTN_KV_HEADS, 1, -1)
    y_p, st_p, k_pt, v_pt = _mixer(gz, xs_p, b_p, c_p, dt_p, alog, dskip, snw, e_mat, q_t, k_t, v_t, gza_t,
                                   bias_t, sink_rows, xp, wo_all, batch, seq)
    y_p = y_p.reshape(batch, seq, D_MODEL)
    conv_p = tail_p[:, SUBLANES - (CONV_WIDTH - 1):, :]

    f2 = lambda a: a.reshape(steps * nb, a.shape[-1])
    to_t = lambda a: jnp.transpose(a[0], (0, 2, 3, 1)).reshape(nb, KV_WIDTH, WINDOW)
    from_t = lambda a: jnp.transpose(
        a.reshape(a.shape[0], ATTN_KV_HEADS, ATTN_HEAD_DIM, WINDOW), (0, 3, 1, 2))[None]
    y_attn3, k_st, v_st = _attn_sample(t3(qn), t3(kn), t3(v_smp), t3(za), to_t(cache_k), to_t(cache_v),
                                       bias_c, bias_n)
    k_s, v_s = from_t(k_st), from_t(v_st)
    y_s = _sample_back(f2(ypart), f2(yoff), f2(ea), gz_smp, snw, f2(y_attn3), xs, wo_all)
    y_s = jnp.swapaxes(y_s.reshape(steps, nb, D_MODEL), 0, 1)

    st5 = lambda a: a.reshape(1, a.shape[0], SSM_HEADS, SSM_HEAD_DIM, SSM_STATE)
    return (y_p, y_s, from_t(k_pt), from_t(v_pt), st5(st_p), conv_p[None],
            k_s, v_s, st5(st_s), jnp.swapaxes(conv_s3, 0, 1)[None])
```

```python
import functools
import math

import numpy as np
import jax
import jax.numpy as jnp
from jax import lax
from jax.experimental import pallas as pl
from jax.experimental.pallas import tpu as pltpu

F32 = jnp.float32
BF16 = jnp.bfloat16

D_MODEL = 1024
SSM_HEADS = 16
SSM_HEAD_DIM = 64
SSM_WIDTH = SSM_HEADS * SSM_HEAD_DIM
SSM_GROUPS = 2
SSM_STATE = 128
GROUP_WIDTH = SSM_WIDTH // SSM_GROUPS
BC_WIDTH = SSM_GROUPS * SSM_STATE
CONV_WIDTH = 4
CONV_DIM = SSM_WIDTH + 2 * BC_WIDTH
CHUNK = 128
ATTN_HEADS = 16
ATTN_KV_HEADS = 4
Q_PER_KV = ATTN_HEADS // ATTN_KV_HEADS
ATTN_HEAD_DIM = 64
ATTN_WIDTH = ATTN_HEADS * ATTN_HEAD_DIM
KV_WIDTH = ATTN_KV_HEADS * ATTN_HEAD_DIM
WINDOW = 128
ATTN_SCALE = ATTN_HEAD_DIM ** -0.5
REL_BUCKETS = 32
REL_MAX_DIST = 128
EPS = 1e-6
LOG2E = 1.0 / math.log(2.0)
NEG = -1e30

LANES = 128
SUBLANES = 8
MXU_WIDTH = 256
VMEM_LIMIT = 56 * 1024 * 1024


def _in_proj_rows():
    widths = (("z", SSM_WIDTH), ("xbc", CONV_DIM), ("dt", SSM_HEADS), ("q", ATTN_WIDTH), ("k", KV_WIDTH),
              ("v", KV_WIDTH), ("za", ATTN_WIDTH))
    rows, start = {}, 0
    for name, width in widths:
        rows[name] = slice(start, start + width)
        start += width
    return rows


IN_ROWS = _in_proj_rows()
DT_ROWS = slice(IN_ROWS["dt"].start, IN_ROWS["dt"].start + LANES)

PROJ_ROWS = 512
CHUNKS_PER_STEP = 4
SAMPLE_BATCH_BLOCK = 8


def _dot(a, b):
    return jnp.dot(a, b, preferred_element_type=F32)


def _dot_nt(a, b):
    return lax.dot_general(a, b, (((1,), (1,)), ((), ())), preferred_element_type=F32)


def _dot_tn(a, b):
    return lax.dot_general(a, b, (((0,), (0,)), ((), ())), preferred_element_type=F32)


def _split2(v):
    hi = v.astype(BF16)
    lo = (v - hi.astype(F32)).astype(BF16)
    return hi, lo


def _dot_sel(v, m):
    hi, lo = _split2(v)
    if 2 * v.shape[1] <= MXU_WIDTH:
        return _dot(jnp.concatenate([hi, lo], axis=1), jnp.concatenate([m, m], axis=0))
    return _dot(hi, m) + _dot(lo, m)


def _dot_sel3(m, v):
    hi = v.astype(BF16)
    r1 = v - hi.astype(F32)
    mid = r1.astype(BF16)
    lo = (r1 - mid.astype(F32)).astype(BF16)
    return _dot(m, hi) + _dot(m, mid) + _dot(m, lo)


def _silu(x):
    return x / (1.0 + jnp.exp(-x))


def _softplus(x):
    return jnp.maximum(x, 0.0) + jnp.log1p(jnp.exp(-jnp.abs(x)))


def _params(*sem):
    return pltpu.CompilerParams(dimension_semantics=sem, vmem_limit_bytes=VMEM_LIMIT)


def _const_spec(shape):
    nd = len(shape)
    return pl.BlockSpec(shape, lambda *_: (0,) * nd)


def _normed_input(x_ref, nw_ref):
    x = x_ref[...]
    ms = jnp.mean(x * x, axis=-1, keepdims=True)
    return (x * lax.rsqrt(ms + EPS) * nw_ref[...]).astype(BF16)


def _dt_projection(h, wt_ref):
    raw = _dot_nt(h, wt_ref[DT_ROWS, :])
    return jnp.where(lax.broadcasted_iota(jnp.int32, raw.shape, 1) < SSM_HEADS, raw, 0.0)


def _shift_rows(u, prev_tail, k):
    rows, width = u.shape
    tiles = jnp.concatenate([prev_tail, u], axis=0).reshape(rows // SUBLANES + 1, SUBLANES, width)
    rot = jnp.concatenate([tiles[:, SUBLANES - k:], tiles[:, :SUBLANES - k]], axis=1)
    first = lax.broadcasted_iota(jnp.int32, (1, SUBLANES, width), 1) < k
    return jnp.where(first, rot[:-1], rot[1:]).reshape(rows, width)


def _inproj_prompt_body(steps_per_seq, n_state_blocks, x_ref, nw_ref, wt_ref, cw_ref, cb_ref, dtb_ref,
                        qwt_ref, kwt_ref, cdec_ref, st_ref, sc_ref, sb_ref, sxw_ref, wo_ref,
                        gz_ref, xs_ref, b_ref, c_ref, dt_ref, qt_ref, kt_ref, vt_ref, gzat_ref, tail_ref,
                        newst_ref, yoff_ref, wo16_ref, tail_sc):
    step = pl.program_id(0)

    @pl.when(step < n_state_blocks)
    def _():
        _ssd_sample_state_block(step, cdec_ref, st_ref, sc_ref, sb_ref, sxw_ref, newst_ref, yoff_ref)

    @pl.when(step >= n_state_blocks)
    def _():
        wo16_ref[...] = wo_ref[...].astype(BF16)

    @pl.when(step % steps_per_seq == 0)
    def _():
        tail_sc[...] = jnp.zeros_like(tail_sc)

    h = _normed_input(x_ref, nw_ref)
    rows = h.shape[0]
    w_tile = lambda name, j: wt_ref[IN_ROWS[name].start + j * MXU_WIDTH:IN_ROWS[name].start + (j + 1) * MXU_WIDTH, :]
    n_side = SSM_WIDTH // MXU_WIDTH
    for j in range(CONV_DIM // MXU_WIDTH):
        cols = slice(j * MXU_WIDTH, (j + 1) * MXU_WIDTH)
        if j < n_side:
            both = _dot_nt(h, jnp.concatenate([w_tile("xbc", j), w_tile("z", j)], axis=0))
            u = both[:, :MXU_WIDTH]
            gz_ref[:, cols] = _silu(both[:, MXU_WIDTH:])
        else:
            u = _dot_nt(h, w_tile("xbc", j))
        prev_tail = tail_sc[:, cols]
        conv = cb_ref[:, cols] + u * cw_ref[CONV_WIDTH - 1:CONV_WIDTH, cols]
        for k in range(1, CONV_WIDTH):
            tap = CONV_WIDTH - 1 - k
            conv = conv + _shift_rows(u, prev_tail, k) * cw_ref[tap:tap + 1, cols]
        new_tail = u[rows - SUBLANES:, :]
        tail_sc[:, cols] = new_tail
        tail_ref[0, :, cols] = new_tail
        act = _silu(conv)
        if j < SSM_WIDTH // MXU_WIDTH:
            xs_ref[:, cols] = act
        elif j == SSM_WIDTH // MXU_WIDTH:
            b_ref[...] = act.astype(BF16)
        else:
            c_ref[...] = act.astype(BF16)

    for j in range(ATTN_WIDTH // MXU_WIDTH):
        feats = slice(j * MXU_WIDTH, (j + 1) * MXU_WIDTH)
        gzat_ref[feats, :] = _silu(_dot_nt(w_tile("za", j), h))
    dt_ref[...] = _softplus(_dt_projection(h, wt_ref) + dtb_ref[...])

    def head_norm(t, w):
        x3 = t.reshape(-1, ATTN_HEAD_DIM, rows)
        ms = jnp.mean(x3 * x3, axis=1, keepdims=True)
        return (x3 * lax.rsqrt(ms + EPS)).reshape(t.shape) * w

    for j in range(ATTN_WIDTH // MXU_WIDTH):
        feats = slice(j * MXU_WIDTH, (j + 1) * MXU_WIDTH)
        qt_ref[feats, :] = head_norm(_dot_nt(w_tile("q", j), h), qwt_ref[feats, :]).astype(BF16)
    kt_ref[...] = head_norm(_dot_nt(wt_ref[IN_ROWS["k"], :], h), kwt_ref[...])
    vt_ref[...] = _dot_nt(wt_ref[IN_ROWS["v"], :], h)


def _inproj_prompt(x2d, norm_w, w_t, conv_w, conv_b, dtb, qw_t, kw_t, batch, seq, cdec, state, c3, b3, xw3,
                   w_out):
    rows = x2d.shape[0]
    tm = PROJ_ROWS
    steps_per_seq = seq // tm
    steps, nb = c3.shape[0], c3.shape[1]
    bb = SAMPLE_BATCH_BLOCK
    n_state_blocks = nb // bb
    n_steps = rows // tm
    assert n_state_blocks < n_steps, "state blocks and weight-cast blocks share the in-proj grid steps"
    last = n_state_blocks - 1
    cast_rows = w_out.shape[0] // (n_steps - n_state_blocks)
    assert cast_rows * (n_steps - n_state_blocks) == w_out.shape[0] and cast_rows % (2 * SUBLANES) == 0
    wo_spec = pl.BlockSpec((cast_rows, D_MODEL), lambda i: (jnp.maximum(i - n_state_blocks, 0), 0))
    resident = lambda a: pl.BlockSpec(a.shape, lambda i: (0, 0), pipeline_mode=pl.Buffered(1))
    rowblk = lambda w: pl.BlockSpec((tm, w), lambda i: (i, 0))
    colblk = pl.BlockSpec((ATTN_WIDTH, tm), lambda i: (0, i))
    kvblk = pl.BlockSpec((KV_WIDTH, tm), lambda i: (0, i))
    tok = lambda w: pl.BlockSpec((steps, bb, w), lambda i: (0, jnp.minimum(i, last), 0))
    st_spec = pl.BlockSpec((bb, SSM_WIDTH, SSM_STATE), lambda i: (jnp.minimum(i, last), 0, 0))
    in_specs = ([rowblk(D_MODEL), _const_spec((1, D_MODEL)), resident(w_t)]
                + [_const_spec(conv_w.shape), _const_spec(conv_b.shape), _const_spec(dtb.shape)]
                + [resident(qw_t), resident(kw_t)]
                + [pl.BlockSpec(memory_space=pltpu.SMEM), st_spec, tok(BC_WIDTH), tok(BC_WIDTH), tok(SSM_WIDTH),
                   wo_spec])
    out_specs = [rowblk(SSM_WIDTH), rowblk(SSM_WIDTH), rowblk(BC_WIDTH), rowblk(BC_WIDTH), rowblk(LANES),
                 colblk, kvblk, kvblk, colblk,
                 pl.BlockSpec((1, SUBLANES, CONV_DIM), lambda i: (i // steps_per_seq, 0, 0)),
                 st_spec, tok(SSM_WIDTH), wo_spec]
    f = lambda r, c, dt=F32: jax.ShapeDtypeStruct((r, c), dt)
    out_shape = [f(rows, SSM_WIDTH), f(rows, SSM_WIDTH), f(rows, BC_WIDTH, BF16), f(rows, BC_WIDTH, BF16),
                 f(rows, LANES), f(ATTN_WIDTH, rows, BF16), f(KV_WIDTH, rows), f(KV_WIDTH, rows), f(ATTN_WIDTH, rows),
                 jax.ShapeDtypeStruct((batch, SUBLANES, CONV_DIM), F32),
                 jax.ShapeDtypeStruct(state.shape, F32), jax.ShapeDtypeStruct((steps, nb, SSM_WIDTH), F32),
                 jax.ShapeDtypeStruct(w_out.shape, BF16)]
    return pl.pallas_call(
        functools.partial(_inproj_prompt_body, steps_per_seq, n_state_blocks), grid=(n_steps,),
        in_specs=in_specs, out_specs=out_specs, out_shape=out_shape,
        scratch_shapes=[pltpu.VMEM((SUBLANES, CONV_DIM), F32)],
        compiler_params=_params("arbitrary"), name="inproj_prompt")(
            x2d, norm_w, w_t, conv_w, conv_b, dtb, qw_t, kw_t, cdec, state, c3, b3, xw3, w_out)


def _group_rmsnorm(gy, norm_w):
    parts = []
    for g in range(SSM_GROUPS):
        blk = gy[:, g * GROUP_WIDTH:(g + 1) * GROUP_WIDTH]
        ms = jnp.mean(blk * blk, axis=-1, keepdims=True)
        parts.append(blk * lax.rsqrt(ms + EPS))
    return jnp.concatenate(parts, axis=1) * norm_w


def _ssd_chunk(gz, xs, b_bf, c_bf, dt, a_neg, dskip, norm_w, e_mat, state):
    xs_bf = xs.astype(BF16)

    a = dt * a_neg
    li = lax.broadcasted_iota(jnp.int32, (CHUNK, CHUNK), 0)
    si = lax.broadcasted_iota(jnp.int32, (CHUNK, CHUNK), 1)
    causal = li >= si
    a_cum = _dot_sel3(jnp.where(causal, 1.0, 0.0).astype(BF16), a)
    a2 = a_cum * LOG2E
    row_term = a2.T - jnp.log2(dt.T)
    ea_full = _dot_sel(jnp.exp(a_cum), e_mat)
    w_full = _dot((dt * jnp.exp(a_cum[CHUNK - 1:CHUNK, :] - a_cum)).astype(BF16), e_mat)

    cb = [_dot_nt(c_bf[:, g * SSM_STATE:(g + 1) * SSM_STATE], b_bf[:, g * SSM_STATE:(g + 1) * SSM_STATE])
          for g in range(SSM_GROUPS)]
    half = lax.broadcasted_iota(jnp.int32, (CHUNK, LANES), 1) < SSM_HEAD_DIM
    heads_per_group = SSM_HEADS // SSM_GROUPS
    y_parts = []
    for pair in range(SSM_HEADS // 2):
        blocks = []
        for h in (2 * pair, 2 * pair + 1):
            seg = a2[:, h:h + 1] - row_term[h:h + 1, :]
            decay_dt = jnp.exp2(jnp.where(causal, seg, -jnp.inf))
            blocks.append((cb[h // heads_per_group] * decay_dt).astype(BF16))
        lhs = jnp.concatenate(blocks, axis=1)
        xp = xs_bf[:, pair * LANES:(pair + 1) * LANES]
        zero = jnp.zeros_like(xp)
        rhs = jnp.concatenate([jnp.where(half, xp, zero), jnp.where(half, zero, xp)], axis=0)
        y_parts.append(_dot(lhs, rhs))
    y_diag = jnp.concatenate(y_parts, axis=1)

    state_bf = state.astype(BF16)
    xw_bf = (xs * w_full).astype(BF16)
    y_off, upd = [], []
    for g in range(SSM_GROUPS):
        cols = slice(g * GROUP_WIDTH, (g + 1) * GROUP_WIDTH)
        ns = slice(g * SSM_STATE, (g + 1) * SSM_STATE)
        y_off.append(_dot(c_bf[:, ns], state_bf[:, cols]))
        upd.append(_dot_tn(b_bf[:, ns], xw_bf[:, cols]))
    y = y_diag + jnp.concatenate(y_off, axis=1) * ea_full + dskip * xs
    new_state = state * ea_full[CHUNK - 1:CHUNK, :] + jnp.concatenate(upd, axis=1)
    return _group_rmsnorm(y * gz, norm_w), new_state


def _rel_bucket_np(dist):
    max_exact = REL_BUCKETS // 2
    d_f = np.maximum(dist, 1).astype(np.float32)
    large = max_exact + (np.log(d_f / np.float32(max_exact)) / np.float32(math.log(REL_MAX_DIST / max_exact))
                         * np.float32(REL_BUCKETS - max_exact)).astype(np.int32)
    return np.where(dist < max_exact, dist, np.minimum(large, REL_BUCKETS - 1)).astype(np.int32)


def _bucket_or_masked(dist, extra_mask=None):
    ok = (dist >= 0) & (dist <= WINDOW)
    if extra_mask is not None:
        ok = ok & extra_mask
    return np.where(ok, _rel_bucket_np(np.clip(dist, 0, WINDOW)), -1).astype(np.int32)


def _sample_bias_tables(rel_ref, sink_ref, cache_bucket_ref, new_bucket_ref, cache_out_ref, new_out_ref):
    steps = cache_bucket_ref.shape[0]
    cache_bucket = cache_bucket_ref[...]
    new_bucket = new_bucket_ref[...]
    sink_column = lax.broadcasted_iota(jnp.int32, new_bucket.shape, 1) == steps
    for h in range(ATTN_HEADS):
        cache_bias = jnp.full(cache_bucket.shape, NEG, F32)
        new_bias = jnp.full(new_bucket.shape, NEG, F32)
        for bkt in range(REL_BUCKETS):
            entry = rel_ref[bkt * ATTN_HEADS + h]
            cache_bias = jnp.where(cache_bucket == bkt, entry, cache_bias)
            new_bias = jnp.where(new_bucket == bkt, entry, new_bias)
        new_bias = jnp.where(sink_column, sink_ref[h], new_bias)
        for bb in range(SAMPLE_BATCH_BLOCK):
            row = (bb * ATTN_HEADS + h) * steps
            cache_out_ref[row:row + steps, :] = cache_bias
            new_out_ref[row:row + steps, :] = new_bias


def _bias_body(rel_ref, sink_ref, dist_bucket_ref, cache_bucket_ref, new_bucket_ref,
               o_ref, cache_out_ref, new_out_ref):
    _sample_bias_tables(rel_ref, sink_ref, cache_bucket_ref, new_bucket_ref, cache_out_ref, new_out_ref)
    T = CHUNK
    width = dist_bucket_ref.shape[1]
    bucket = dist_bucket_ref[...]
    own_block = lax.broadcasted_iota(jnp.int32, (2 * T, T), 0) >= T

    def per_kv_head(n, carry):
        for g in range(Q_PER_KV):
            profile = jnp.full(bucket.shape, NEG, F32)
            for bkt in range(REL_BUCKETS):
                profile = jnp.where(bucket == bkt, rel_ref[bkt * ATTN_HEADS + n * Q_PER_KV + g], profile)
            rows = jnp.broadcast_to(profile[0:1, :] * LOG2E, (2 * T, width))
            tile = pltpu.roll(rows, width - T, axis=1, stride=1, stride_axis=0)[:, :T]
            o_ref[1, n, :, g * T:(g + 1) * T] = tile
            o_ref[0, n, :, g * T:(g + 1) * T] = jnp.where(own_block, tile, NEG)
        return carry

    lax.fori_loop(0, ATTN_KV_HEADS, per_kv_head, 0)


def _bias_tables(rel_flat, sink, dist_bucket, cache_bucket, new_bucket):
    steps = cache_bucket.shape[0]
    rows = SAMPLE_BATCH_BLOCK * ATTN_HEADS * steps
    f = lambda *s: jax.ShapeDtypeStruct(s, F32)
    smem, vmem = pl.BlockSpec(memory_space=pltpu.SMEM), pl.BlockSpec(memory_space=pltpu.VMEM)
    return pl.pallas_call(
        _bias_body,
        in_specs=[smem, smem, vmem, vmem, vmem],
        out_specs=[vmem, vmem, vmem],
        out_shape=[f(2, ATTN_KV_HEADS, 2 * CHUNK, Q_PER_KV * CHUNK), f(rows, cache_bucket.shape[1]),
                   f(rows, new_bucket.shape[1])],
        compiler_params=pltpu.CompilerParams(vmem_limit_bytes=VMEM_LIMIT), name="rel_bias")(
        rel_flat, sink, dist_bucket, cache_bucket, new_bucket)


def _head_rmsnorm(x, g_mat, e_mat, w):
    ms = _dot_sel(x * x, g_mat) * (1.0 / ATTN_HEAD_DIM)
    return x * _dot_sel(lax.rsqrt(ms + EPS), e_mat) * w


def _lane_head(shape):
    return lax.broadcasted_iota(jnp.int32, shape, 1) // ATTN_HEAD_DIM


def _attn_block(q_blk, kcat, vcat_t, bias_at, sink_ref):
    T = CHUNK
    lane_head = _lane_head((2 * T, KV_WIDTH))
    zero = jnp.zeros((2 * T, KV_WIDTH), BF16)
    head = lambda h: q_blk[h * ATTN_HEAD_DIM:(h + 1) * ATTN_HEAD_DIM]
    q_cols = jnp.concatenate(
        [jnp.concatenate([head(n * Q_PER_KV + g) for n in range(ATTN_KV_HEADS)], axis=0)
         for g in range(Q_PER_KV)], axis=1)
    row_head = lax.broadcasted_iota(jnp.int32, (KV_WIDTH, 2 * T), 0) // ATTN_HEAD_DIM
    probs, vals, inv = [], [], {}
    for n in range(ATTN_KV_HEADS):
        s = _dot(jnp.where(lane_head == n, kcat, zero), q_cols)
        sink = sink_ref[n]
        cols = []
        for g in range(Q_PER_KV):
            c = slice(g * T, (g + 1) * T)
            sg = s[:, c] + bias_at(n, c)
            m = jnp.maximum(jnp.max(sg, axis=0, keepdims=True), sink[:, c])
            p = jnp.exp2(sg - m)
            inv[n, g] = 1.0 / (jnp.sum(p, axis=0, keepdims=True) + jnp.exp2(sink[:, c] - m))
            cols.append(p.astype(BF16))
        probs.append(jnp.concatenate(cols, axis=1))
        vals.append(jnp.where(row_head == n, vcat_t, zero.T))
    o_t = _dot(jnp.concatenate(vals, axis=1), jnp.concatenate(probs, axis=0))
    return jnp.concatenate(
        [o_t[n * ATTN_HEAD_DIM:(n + 1) * ATTN_HEAD_DIM, g * T:(g + 1) * T] * inv[n, g]
         for n in range(ATTN_KV_HEADS) for g in range(Q_PER_KV)], axis=0)


def _mixer_body(gz_ref, xs_ref, b_ref, c_ref, dt_ref, alog_ref, dskip_ref, nw_ref, e_ref,
                qt_ref, kt_ref, vt_ref, gzt_ref, bias_ref, sink_ref, x_ref, wo_ref,
                y_ref, st_ref, knt_ref, vnt_ref,
                state_sc, kcat_sc, vcat_t_sc, yssm_sc, yattn_t_sc):
    T = CHUNK
    step = pl.program_id(1)
    cols_step = CHUNKS_PER_STEP * T

    @pl.when(step == 0)
    def _():
        state_sc[...] = jnp.zeros_like(state_sc)
        kcat_sc[0:T, :] = jnp.zeros((T, KV_WIDTH), BF16)
        vcat_t_sc[:, 0:T] = jnp.zeros((KV_WIDTH, T), BF16)

    qn = qt_ref[...]
    kn_t = kt_ref[...]
    v_t = vt_ref[...]
    knt_ref[0] = kn_t[:, cols_step - T:]
    vnt_ref[0] = v_t[:, cols_step - T:]
    kcat_sc[T:, :] = kn_t.T.astype(BF16)
    vcat_t_sc[:, T:] = v_t.astype(BF16)
    e_mat = e_ref[...]

    a_neg = -jnp.exp(alog_ref[...])
    state = state_sc[...]
    first_variant = jnp.minimum(step, 1)
    for j in range(CHUNKS_PER_STEP):
        r = slice(j * T, (j + 1) * T)
        y, state = _ssd_chunk(gz_ref[r, :], xs_ref[r, :], b_ref[r, :], c_ref[r, :], dt_ref[r, :], a_neg,
                              dskip_ref[...], nw_ref[...], e_mat, state)
        yssm_sc[r, :] = y.astype(BF16)
        variant = first_variant if j == 0 else 1
        y_t = _attn_block(qn[:, r], kcat_sc[j * T:(j + 2) * T, :], vcat_t_sc[:, j * T:(j + 2) * T],
                          lambda n, c, variant=variant: bias_ref[variant, n, :, c], sink_ref)
        yattn_t_sc[:, r] = (y_t * gzt_ref[:, r]).astype(BF16)
    state_sc[...] = state
    kcat_sc[0:T, :] = kcat_sc[cols_step:, :]
    vcat_t_sc[:, 0:T] = vcat_t_sc[:, cols_step:]

    y_ref[...] = (x_ref[...] + _dot(yssm_sc[...], wo_ref[:SSM_WIDTH, :])
                  + _dot_tn(yattn_t_sc[...], wo_ref[SSM_WIDTH:, :]))

    @pl.when(step == pl.num_programs(1) - 1)
    def _():
        st_ref[0] = state.T


def _mixer(gz, xs, b, c, dt, alog, dskip, norm_w, e_mat, q_t, k_t, v_t, gz_t, bias_t, sink_rows,
           x2d, w_out, batch, seq):
    step_rows = CHUNKS_PER_STEP * CHUNK
    ns = seq // step_rows
    row = lambda w: pl.BlockSpec((step_rows, w), lambda b, i: (b * ns + i, 0))
    col = lambda w: pl.BlockSpec((w, step_rows), lambda b, i: (0, b * ns + i))
    resident = lambda a: pl.BlockSpec(a.shape, lambda b, i: (0,) * a.ndim, pipeline_mode=pl.Buffered(1))
    in_specs = [row(SSM_WIDTH), row(SSM_WIDTH), row(BC_WIDTH), row(BC_WIDTH), row(LANES),
                _const_spec((1, LANES)), _const_spec((1, SSM_WIDTH)), _const_spec((1, SSM_WIDTH)), resident(e_mat),
                col(ATTN_WIDTH), col(KV_WIDTH), col(KV_WIDTH), col(ATTN_WIDTH),
                resident(bias_t), _const_spec(sink_rows.shape), row(D_MODEL), resident(w_out)]
    kv_out = pl.BlockSpec((1, KV_WIDTH, CHUNK), lambda b, i: (b, 0, 0))
    out_specs = [row(D_MODEL), pl.BlockSpec((1, SSM_WIDTH, SSM_STATE), lambda b, i: (b, 0, 0)), kv_out, kv_out]
    out_shape = [jax.ShapeDtypeStruct((batch * seq, D_MODEL), F32),
                 jax.ShapeDtypeStruct((batch, SSM_WIDTH, SSM_STATE), F32),
                 jax.ShapeDtypeStruct((batch, KV_WIDTH, CHUNK), F32),
                 jax.ShapeDtypeStruct((batch, KV_WIDTH, CHUNK), F32)]
    scratch = [pltpu.VMEM((SSM_STATE, SSM_WIDTH), F32),
               pltpu.VMEM((step_rows + CHUNK, KV_WIDTH), BF16), pltpu.VMEM((KV_WIDTH, step_rows + CHUNK), BF16),
               pltpu.VMEM((step_rows, SSM_WIDTH), BF16), pltpu.VMEM((ATTN_WIDTH, step_rows), BF16)]
    return pl.pallas_call(
        _mixer_body, grid=(batch, ns), in_specs=in_specs, out_specs=out_specs, out_shape=out_shape,
        scratch_shapes=scratch, compiler_params=_params("arbitrary", "arbitrary"), name="mixer")(
            gz, xs, b, c, dt, alog, dskip, norm_w, e_mat, q_t, k_t, v_t, gz_t, bias_t, sink_rows, x2d, w_out)


def _sample_front_body(x_ref, nw_ref, wt_ref, sconv_ref, cw_ref, cb_ref, dtb_ref, alog_ref, dskip_ref,
                       gh_ref, e_ref, qw_ref, kw_ref, g_ref,
                       ypart_ref, ea_ref, xw_ref, b_ref, c_ref, cdec_ref, convnew_ref,
                       gz_ref, qn_ref, kn_ref, v_ref, za_ref):
    steps, nb = ypart_ref.shape[0], ypart_ref.shape[1]
    tail = CONV_WIDTH - 1
    h = _normed_input(x_ref, nw_ref)
    proj = lambda name: _dot_nt(h, wt_ref[IN_ROWS[name], :])
    slab = lambda a, l: a[l * nb:(l + 1) * nb]
    xbc = proj("xbc")
    dt_raw = _dt_projection(h, wt_ref)
    gz_ref[...] = _silu(proj("z"))
    v_ref[...] = proj("v")
    za_ref[...] = proj("za")
    g_mat = g_ref[...]
    e_mat = e_ref[...]
    qn = _head_rmsnorm(proj("q"), g_mat, e_mat, qw_ref[...]) * ATTN_SCALE
    head = lambda hd: qn[:, hd * ATTN_HEAD_DIM:(hd + 1) * ATTN_HEAD_DIM]
    qn_ref[...] = jnp.concatenate(
        [head(n * Q_PER_KV + g) for g in range(Q_PER_KV) for n in range(ATTN_KV_HEADS)], axis=1)
    kn_ref[...] = _head_rmsnorm(proj("k"), g_mat[:KV_WIDTH], e_mat[:, :KV_WIDTH], kw_ref[...])

    full = [sconv_ref[j] for j in range(tail)] + [slab(xbc, l) for l in range(steps)]
    for j in range(tail):
        convnew_ref[j] = full[steps + j]
    gh = gh_ref[...]
    a_neg = -jnp.exp(alog_ref[...])
    xs, bm, cm, dts, acum = [], [], [], [], []
    run = None
    for l in range(steps):
        conv = cb_ref[...]
        for tap in range(CONV_WIDTH):
            conv = conv + full[l + tap] * cw_ref[tap:tap + 1, :]
        act = _silu(conv)
        xs.append(act[:, :SSM_WIDTH])
        bm.append(act[:, SSM_WIDTH:SSM_WIDTH + BC_WIDTH])
        cm.append(act[:, SSM_WIDTH + BC_WIDTH:])
        d = _softplus(slab(dt_raw, l) + dtb_ref[...])
        dts.append(d)
        run = d * a_neg if run is None else run + d * a_neg
        acum.append(run)
        b_ref[l] = bm[l]
        c_ref[l] = cm[l]
    for l in range(steps):
        y = dskip_ref[...] * xs[l]
        for s in range(l + 1):
            cb_h = _dot_sel(cm[l] * bm[s], gh)
            coef = cb_h * jnp.exp(acum[l] - acum[s]) * dts[s]
            y = y + _dot_sel(coef, e_mat) * xs[s]
        ypart_ref[l] = y
        ea_ref[l] = _dot_sel(jnp.exp(acum[l]), e_mat)
        xw_ref[l] = xs[l] * _dot_sel(dts[l] * jnp.exp(acum[steps - 1] - acum[l]), e_mat)
    cdec_ref[...] = jnp.exp(acum[steps - 1])


def _sample_front(x2d, norm_w, w_t, sconv3, conv_w, conv_b, dtb, alog, dskip, gh_mat, e_mat, qw, kw, g_mat,
                  steps, nb):
    rows = steps * nb
    f = lambda *s: jax.ShapeDtypeStruct(s, F32)
    out_shape = [f(steps, nb, SSM_WIDTH), f(steps, nb, SSM_WIDTH), f(steps, nb, SSM_WIDTH),
                 f(steps, nb, BC_WIDTH), f(steps, nb, BC_WIDTH), f(nb, LANES), f(CONV_WIDTH - 1, nb, CONV_DIM),
                 f(rows, SSM_WIDTH), f(rows, ATTN_WIDTH), f(rows, KV_WIDTH), f(rows, KV_WIDTH), f(rows, ATTN_WIDTH)]
    return pl.pallas_call(_sample_front_body, out_shape=out_shape,
                          compiler_params=pltpu.CompilerParams(vmem_limit_bytes=VMEM_LIMIT),
                          name="sample_front")(
        x2d, norm_w, w_t, sconv3, conv_w, conv_b, dtb, alog, dskip, gh_mat, e_mat, qw, kw, g_mat)


def _ssd_sample_state_block(i, cdec_ref, st_ref, c_ref, b_ref, xw_ref, new_ref, yoff_ref):
    heads_per_group = SSM_HEADS // SSM_GROUPS
    for j in range(SAMPLE_BATCH_BLOCK):
        st = st_ref[j]
        cb_bf = c_ref[:, j, :].astype(BF16)
        bb_bf = b_ref[:, j, :].astype(BF16)
        xw_bf = xw_ref[:, j, :].astype(BF16)
        y_parts = []
        for g in range(SSM_GROUPS):
            rows = slice(g * GROUP_WIDTH, (g + 1) * GROUP_WIDTH)
            ns = slice(g * SSM_STATE, (g + 1) * SSM_STATE)
            y_parts.append(_dot_nt(cb_bf[:, ns], st[rows].astype(BF16)))
            upd = _dot_tn(xw_bf[:, rows], bb_bf[:, ns])
            for hh in range(heads_per_group):
                h = g * heads_per_group + hh
                r = slice(h * SSM_HEAD_DIM, (h + 1) * SSM_HEAD_DIM)
                dec = cdec_ref[i * SAMPLE_BATCH_BLOCK + j, h]
                new_ref[j, r, :] = st[r] * dec + upd[hh * SSM_HEAD_DIM:(hh + 1) * SSM_HEAD_DIM]
        yoff_ref[:, j, :] = jnp.concatenate(y_parts, axis=1)


def _sample_back_body(ypart_ref, yoff_ref, ea_ref, gz_ref, nw_ref, ya_ref, x_ref, wo_ref, o_ref):
    y = ypart_ref[...] + yoff_ref[...] * ea_ref[...]
    y_ssm = _group_rmsnorm(y * gz_ref[...], nw_ref[...])
    o_ref[...] = (x_ref[...] + _dot(y_ssm.astype(BF16), wo_ref[:SSM_WIDTH, :])
                  + _dot(ya_ref[...].astype(BF16), wo_ref[SSM_WIDTH:, :]))


def _sample_back(ypart, yoff, ea, gz, norm_w, y_attn, x2d, w_out):
    return pl.pallas_call(_sample_back_body, out_shape=jax.ShapeDtypeStruct(x2d.shape, F32),
                          compiler_params=pltpu.CompilerParams(vmem_limit_bytes=VMEM_LIMIT),
                          name="sample_back")(ypart, yoff, ea, gz, norm_w, y_attn, x2d, w_out)


def _attn_sample_body(q_ref, kn_ref, vn_ref, z_ref, ckt_ref, cvt_ref, biasc_ref, biasn_ref,
                      y_ref, kot_ref, vot_ref):
    steps = q_ref.shape[0]
    bb = SAMPLE_BATCH_BLOCK
    blk = Q_PER_KV * steps
    rows = ATTN_KV_HEADS * blk
    pad = jnp.zeros((SUBLANES - steps, KV_WIDTH), F32)
    lane_head = _lane_head((blk, KV_WIDTH))
    zero = jnp.zeros((blk, KV_WIDTH), F32)

    s_c, s_n, k8, v8 = [], [], [], []
    for j in range(bb):
        q = q_ref[:, j, :]
        qg = jnp.concatenate([q[:, g * KV_WIDTH:(g + 1) * KV_WIDTH] for g in range(Q_PER_KV)], axis=0)
        qx = jnp.concatenate([jnp.where(lane_head == n, qg, zero) for n in range(ATTN_KV_HEADS)], axis=0)
        qx = qx.astype(BF16)
        k8.append(jnp.concatenate([kn_ref[:, j, :], pad], axis=0))
        v8.append(jnp.concatenate([vn_ref[:, j, :], pad], axis=0))
        s_c.append(_dot(qx, ckt_ref[j].astype(BF16)))
        s_n.append(_dot_nt(qx, k8[j].astype(BF16)))
    s_c = jnp.concatenate(s_c, axis=0) + biasc_ref[...]
    s_n = jnp.concatenate(s_n, axis=0) + biasn_ref[...]
    m = jnp.maximum(jnp.max(s_c, axis=-1, keepdims=True), jnp.max(s_n, axis=-1, keepdims=True))
    p_c = jnp.exp(s_c - m)
    p_n = jnp.exp(s_n - m)
    inv = 1.0 / (jnp.sum(p_c, axis=-1, keepdims=True) + jnp.sum(p_n, axis=-1, keepdims=True))
    p_c = (p_c * inv).astype(BF16)
    p_n = (p_n * inv).astype(BF16)

    lane = lax.broadcasted_iota(jnp.int32, (KV_WIDTH, WINDOW), 1)
    for j in range(bb):
        r = slice(j * rows, (j + 1) * rows)
        o = _dot_nt(p_c[r], cvt_ref[j].astype(BF16)) + _dot(p_n[r], v8[j].astype(BF16))
        og = zero
        for n in range(ATTN_KV_HEADS):
            og = og + jnp.where(lane_head == n, o[n * blk:(n + 1) * blk], zero)
        y = jnp.concatenate(
            [og[g * steps:(g + 1) * steps, n * ATTN_HEAD_DIM:(n + 1) * ATTN_HEAD_DIM]
             for n in range(ATTN_KV_HEADS) for g in range(Q_PER_KV)], axis=1)
        y_ref[:, j, :] = y * _silu(z_ref[:, j, :])

        for new8, old_ref, out_ref in ((k8[j], ckt_ref, kot_ref), (v8[j], cvt_ref, vot_ref)):
            tail_rows = jnp.concatenate([new8[steps:], new8[:steps]], axis=0)
            block = jnp.concatenate([jnp.zeros((WINDOW - SUBLANES, KV_WIDTH), F32), tail_rows], axis=0)
            shifted = pltpu.roll(old_ref[j], WINDOW - steps, axis=1)
            out_ref[j] = jnp.where(lane >= WINDOW - steps, block.T, shifted)


def _attn_sample(q3, kn3, vn3, z3, cache_kt, cache_vt, bias_c, bias_n):
    steps, nb = q3.shape[0], q3.shape[1]
    bb = SAMPLE_BATCH_BLOCK
    tok = lambda w: pl.BlockSpec((steps, bb, w), lambda i: (0, i, 0))
    cache_spec = pl.BlockSpec((bb, KV_WIDTH, WINDOW), lambda i: (i, 0, 0))
    return pl.pallas_call(
        _attn_sample_body, grid=(nb // bb,),
        in_specs=[tok(ATTN_WIDTH), tok(KV_WIDTH), tok(KV_WIDTH), tok(ATTN_WIDTH), cache_spec, cache_spec,
                  _const_spec(bias_c.shape), _const_spec(bias_n.shape)],
        out_specs=[tok(ATTN_WIDTH), cache_spec, cache_spec],
        out_shape=[jax.ShapeDtypeStruct((steps, nb, ATTN_WIDTH), F32),
                   jax.ShapeDtypeStruct(cache_kt.shape, F32), jax.ShapeDtypeStruct(cache_vt.shape, F32)],
        compiler_params=_params("parallel"), name="attn_sample")(
            q3, kn3, vn3, z3, cache_kt, cache_vt, bias_c, bias_n)


def _static_tables(steps):
    lanes = np.arange(ATTN_WIDTH)
    g_mat = np.zeros((ATTN_WIDTH, LANES), np.float32)
    g_mat[lanes, lanes // ATTN_HEAD_DIM] = 1.0
    e_mat = g_mat.T.copy()
    bc = np.arange(BC_WIDTH)
    gh_mat = np.zeros((BC_WIDTH, LANES), np.float32)
    for h in range(SSM_HEADS):
        gh_mat[bc // SSM_STATE == h // (SSM_HEADS // SSM_GROUPS), h] = 1.0
    assert WINDOW <= CHUNK
    prompt_buckets = np.tile(_bucket_or_masked(np.arange(2 * CHUNK))[None, :], (SUBLANES, 1))
    dist_c = (np.arange(steps) + WINDOW)[:, None] - np.arange(WINDOW)[None, :]
    dist_n = np.arange(steps)[:, None] - np.arange(SUBLANES)[None, :]
    real = np.broadcast_to((np.arange(SUBLANES) < steps)[None, :], dist_n.shape)
    return dict(g=g_mat, e=e_mat, gh=gh_mat, prompt_buckets=prompt_buckets,
                cache_buckets=_bucket_or_masked(dist_c), new_buckets=_bucket_or_masked(dist_n, real))


def kernel(x_prompt, x_sample, cache_k, cache_v, state_ssm, state_conv, norm_w, w_in, conv_w, conv_b, dt_bias,
           a_log, d_skip, ssm_norm_w, q_norm_w, k_norm_w, sinks, rel_table, w_out):
    assert w_in.shape[0] == 1, "single-layer kernel"
    batch, seq, _ = x_prompt.shape
    nb, steps, _ = x_sample.shape
    tab = _static_tables(steps)
    g_mat = jnp.asarray(tab["g"], BF16)
    e_mat = jnp.asarray(tab["e"], BF16)
    gh_mat = jnp.asarray(tab["gh"], BF16)

    w_t = jnp.transpose(w_in[0]).astype(BF16)

    row = lambda v, width: jnp.pad(v.reshape(1, -1), ((0, 0), (0, width - v.size)))
    nw = row(norm_w[0], D_MODEL)
    cw = conv_w[0]
    cb = row(conv_b[0], CONV_DIM)
    dtb = row(dt_bias[0], LANES)
    alog = row(a_log[0], LANES)
    dskip = jnp.repeat(d_skip[0], SSM_HEAD_DIM).reshape(1, SSM_WIDTH)
    snw = row(ssm_norm_w[0], SSM_WIDTH)
    qw = jnp.tile(q_norm_w[0], ATTN_HEADS).reshape(1, ATTN_WIDTH)
    kw = jnp.tile(k_norm_w[0], ATTN_KV_HEADS).reshape(1, KV_WIDTH)
    sink = sinks[0]
    rel_flat = rel_table.reshape(-1)

    xs = jnp.swapaxes(x_sample, 0, 1).reshape(steps * nb, D_MODEL)
    t3 = lambda a: a.reshape(steps, nb, a.shape[-1])
    sconv3 = jnp.swapaxes(state_conv[0], 0, 1)
    ypart, ea, xw, b3, c3, cdec, conv_s3, gz_smp, qn, kn, v_smp, za = _sample_front(
        xs, nw, w_t, sconv3, cw, cb, dtb, alog, dskip, gh_mat, e_mat, qw, kw, g_mat, steps, nb)
    state_in = state_ssm[0].reshape(nb, SSM_WIDTH, SSM_STATE)

    xp = x_prompt.reshape(batch * seq, D_MODEL)
    qw_t = jnp.broadcast_to((qw * (ATTN_SCALE * LOG2E)).reshape(ATTN_WIDTH, 1), (ATTN_WIDTH, PROJ_ROWS))
    kw_t = jnp.broadcast_to(kw.reshape(KV_WIDTH, 1), (KV_WIDTH, PROJ_ROWS))
    gz, xs_p, b_p, c_p, dt_p, q_t, k_t, v_t, gza_t, tail_p, st_s, yoff, wo_all = _inproj_prompt(
        xp, nw, w_t, cw, cb, dtb, qw_t, kw_t, batch, seq, cdec, state_in, c3, b3, xw,
        w_out[0])
    bias_t, bias_c, bias_n = _bias_tables(rel_flat, sink, jnp.asarray(tab["prompt_buckets"]),
                                          jnp.asarray(tab["cache_buckets"]), jnp.asarray(tab["new_buckets"]))
    sink_rows = jnp.repeat(sink.reshape(ATTN_KV_HEADS, Q_PER_KV) * LOG2E, CHUNK, axis=1)
    sink_rows = sink_rows.reshape(ATTN_KV_HEADS, 1, -1)
    y_p, st_p, k_pt, v_pt = _mixer(gz, xs_p, b_p, c_p, dt_p, alog, dskip, snw, e_mat, q_t, k_t, v_t, gza_t,
                                   bias_t, sink_rows, xp, wo_all, batch, seq)
    y_p = y_p.reshape(batch, seq, D_MODEL)
    conv_p = tail_p[:, SUBLANES - (CONV_WIDTH - 1):, :]

    f2 = lambda a: a.reshape(steps * nb, a.shape[-1])
    to_t = lambda a: jnp.transpose(a[0], (0, 2, 3, 1)).reshape(nb, KV_WIDTH, WINDOW)
    from_t = lambda a: jnp.transpose(
        a.reshape(a.shape[0], ATTN_KV_HEADS, ATTN_HEAD_DIM, WINDOW), (0, 3, 1, 2))[None]
    y_attn3, k_st, v_st = _attn_sample(t3(qn), t3(kn), t3(v_smp), t3(za), to_t(cache_k), to_t(cache_v),
                                       bias_c, bias_n)
    k_s, v_s = from_t(k_st), from_t(v_st)
    y_s = _sample_back(f2(ypart), f2(yoff), f2(ea), gz_smp, snw, f2(y_attn3), xs, wo_all)
    y_s = jnp.swapaxes(y_s.reshape(steps, nb, D_MODEL), 0, 1)

    st5 = lambda a: a.reshape(1, a.shape[0], SSM_HEADS, SSM_HEAD_DIM, SSM_STATE)
    return (y_p, y_s, from_t(k_pt), from_t(v_pt), st5(st_p), conv_p[None],
            k_s, v_s, st5(st_s), jnp.swapaxes(conv_s3, 0, 1)[None])
```

```python
import functools
import math

import numpy as np
import jax
import jax.numpy as jnp
from jax import lax
from jax.experimental import pallas as pl
from jax.experimental.pallas import tpu as pltpu

F32 = jnp.float32
BF16 = jnp.bfloat16

D_MODEL = 1024
SSM_HEADS = 16
SSM_HEAD_DIM = 64
SSM_WIDTH = SSM_HEADS * SSM_HEAD_DIM
SSM_GROUPS = 2
SSM_STATE = 128
GROUP_WIDTH = SSM_WIDTH // SSM_GROUPS
BC_WIDTH = SSM_GROUPS * SSM_STATE
CONV_WIDTH = 4
CONV_DIM = SSM_WIDTH + 2 * BC_WIDTH
CHUNK = 128
ATTN_HEADS = 16
ATTN_KV_HEADS = 4
Q_PER_KV = ATTN_HEADS // ATTN_KV_HEADS
ATTN_HEAD_DIM = 64
ATTN_WIDTH = ATTN_HEADS * ATTN_HEAD_DIM
KV_WIDTH = ATTN_KV_HEADS * ATTN_HEAD_DIM
WINDOW = 128
ATTN_SCALE = ATTN_HEAD_DIM ** -0.5
REL_BUCKETS = 32
REL_MAX_DIST = 128
EPS = 1e-6
LOG2E = 1.0 / math.log(2.0)
NEG = -1e30

LANES = 128
SUBLANES = 8
MXU_WIDTH = 256
VMEM_LIMIT = 56 * 1024 * 1024


def _in_proj_rows():
    widths = (("z", SSM_WIDTH), ("xbc", CONV_DIM), ("dt", SSM_HEADS), ("q", ATTN_WIDTH), ("k", KV_WIDTH),
              ("v", KV_WIDTH), ("za", ATTN_WIDTH))
    rows, start = {}, 0
    for name, width in widths:
        rows[name] = slice(start, start + width)
        start += width
    return rows


IN_ROWS = _in_proj_rows()
DT_ROWS = slice(IN_ROWS["dt"].start, IN_ROWS["dt"].start + LANES)

PROJ_ROWS = 512
CHUNKS_PER_STEP = 4
SAMPLE_BATCH_BLOCK = 8


def _dot(a, b):
    return jnp.dot(a, b, preferred_element_type=F32)


def _dot_nt(a, b):
    return lax.dot_general(a, b, (((1,), (1,)), ((), ())), preferred_element_type=F32)


def _dot_tn(a, b):
    return lax.dot_general(a, b, (((0,), (0,)), ((), ())), preferred_element_type=F32)


def _split2(v):
    hi = v.astype(BF16)
    lo = (v - hi.astype(F32)).astype(BF16)
    return hi, lo


def _dot_sel(v, m):
    hi, lo = _split2(v)
    if 2 * v.shape[1] <= MXU_WIDTH:
        return _dot(jnp.concatenate([hi, lo], axis=1), jnp.concatenate([m, m], axis=0))
    return _dot(hi, m) + _dot(lo, m)


def _dot_sel3(m, v):
    hi = v.astype(BF16)
    r1 = v - hi.astype(F32)
    mid = r1.astype(BF16)
    lo = (r1 - mid.astype(F32)).astype(BF16)
    return _dot(m, hi) + _dot(m, mid) + _dot(m, lo)


def _silu(x):
    return x / (1.0 + jnp.exp(-x))


def _softplus(x):
    return jnp.maximum(x, 0.0) + jnp.log1p(jnp.exp(-jnp.abs(x)))


def _params(*sem):
    return pltpu.CompilerParams(dimension_semantics=sem, vmem_limit_bytes=VMEM_LIMIT)


def _const_spec(shape):
    nd = len(shape)
    return pl.BlockSpec(shape, lambda *_: (0,) * nd)


def _normed_input(x_ref, nw_ref):
    return _normed_rows(x_ref[...], nw_ref)


def _normed_rows(x, nw_ref):
    ms = jnp.mean(x * x, axis=-1, keepdims=True)
    return (x * lax.rsqrt(ms + EPS) * nw_ref[...]).astype(BF16)


def _dt_projection(h, wt_ref):
    raw = _dot_nt(h, wt_ref[DT_ROWS, :])
    return jnp.where(lax.broadcasted_iota(jnp.int32, raw.shape, 1) < SSM_HEADS, raw, 0.0)


def _shift_rows(u, prev_tail, k):
    rows, width = u.shape
    tiles = jnp.concatenate([prev_tail, u], axis=0).reshape(rows // SUBLANES + 1, SUBLANES, width)
    rot = jnp.concatenate([tiles[:, SUBLANES - k:], tiles[:, :SUBLANES - k]], axis=1)
    first = lax.broadcasted_iota(jnp.int32, (1, SUBLANES, width), 1) < k
    return jnp.where(first, rot[:-1], rot[1:]).reshape(rows, width)


def _inproj_prompt_body(steps_per_seq, n_state_blocks, x_ref, nw_ref, wt_ref, cw_ref, cb_ref, dtb_ref,
                        qwt_ref, kwt_ref, cdec_ref, st_ref, sc_ref, sb_ref, sxw_ref, wo_ref,
                        gz_ref, xs_ref, b_ref, c_ref, dt_ref, qt_ref, kt_ref, vt_ref, gzat_ref, tail_ref,
                        newst_ref, yoff_ref, wo16_ref, tail_sc):
    step = pl.program_id(0)

    @pl.when(step < n_state_blocks)
    def _():
        _ssd_sample_state_block(step, cdec_ref, st_ref, sc_ref, sb_ref, sxw_ref, newst_ref, yoff_ref)

    @pl.when(step >= n_state_blocks)
    def _():
        wo16_ref[...] = wo_ref[...].astype(BF16)

    @pl.when(step % steps_per_seq == 0)
    def _():
        tail_sc[...] = jnp.zeros_like(tail_sc)

    h = _normed_input(x_ref, nw_ref)
    rows = h.shape[0]
    w_tile = lambda name, j: wt_ref[IN_ROWS[name].start + j * MXU_WIDTH:IN_ROWS[name].start + (j + 1) * MXU_WIDTH, :]
    n_side = SSM_WIDTH // MXU_WIDTH
    for j in range(CONV_DIM // MXU_WIDTH):
        cols = slice(j * MXU_WIDTH, (j + 1) * MXU_WIDTH)
        if j < n_side:
            both = _dot_nt(h, jnp.concatenate([w_tile("xbc", j), w_tile("z", j)], axis=0))
            u = both[:, :MXU_WIDTH]
            gz_ref[:, cols] = _silu(both[:, MXU_WIDTH:])
        else:
            u = _dot_nt(h, w_tile("xbc", j))
        prev_tail = tail_sc[:, cols]
        conv = cb_ref[:, cols] + u * cw_ref[CONV_WIDTH - 1:CONV_WIDTH, cols]
        for k in range(1, CONV_WIDTH):
            tap = CONV_WIDTH - 1 - k
            conv = conv + _shift_rows(u, prev_tail, k) * cw_ref[tap:tap + 1, cols]
        new_tail = u[rows - SUBLANES:, :]
        tail_sc[:, cols] = new_tail
        tail_ref[0, :, cols] = new_tail
        act = _silu(conv)
        if j < SSM_WIDTH // MXU_WIDTH:
            xs_ref[:, cols] = act
        elif j == SSM_WIDTH // MXU_WIDTH:
            b_ref[...] = act.astype(BF16)
        else:
            c_ref[...] = act.astype(BF16)

    for j in range(ATTN_WIDTH // MXU_WIDTH):
        feats = slice(j * MXU_WIDTH, (j + 1) * MXU_WIDTH)
        gzat_ref[feats, :] = _silu(_dot_nt(w_tile("za", j), h))
    dt_ref[...] = _softplus(_dt_projection(h, wt_ref) + dtb_ref[...])

    def head_norm(t, w):
        x3 = t.reshape(-1, ATTN_HEAD_DIM, rows)
        ms = jnp.mean(x3 * x3, axis=1, keepdims=True)
        return (x3 * lax.rsqrt(ms + EPS)).reshape(t.shape) * w

    for j in range(ATTN_WIDTH // MXU_WIDTH):
        feats = slice(j * MXU_WIDTH, (j + 1) * MXU_WIDTH)
        qt_ref[feats, :] = head_norm(_dot_nt(w_tile("q", j), h), qwt_ref[feats, :]).astype(BF16)
    kt_ref[...] = head_norm(_dot_nt(wt_ref[IN_ROWS["k"], :], h), kwt_ref[...])
    vt_ref[...] = _dot_nt(wt_ref[IN_ROWS["v"], :], h)


def _inproj_prompt(x2d, norm_w, w_t, conv_w, conv_b, dtb, qw_t, kw_t, batch, seq, cdec, state, c3, b3, xw3,
                   w_out):
    rows = x2d.shape[0]
    tm = PROJ_ROWS
    steps_per_seq = seq // tm
    steps, nb = c3.shape[0], c3.shape[1]
    bb = SAMPLE_BATCH_BLOCK
    n_state_blocks = nb // bb
    n_steps = rows // tm
    assert n_state_blocks < n_steps, "state blocks and weight-cast blocks share the in-proj grid steps"
    last = n_state_blocks - 1
    cast_rows = w_out.shape[0] // (n_steps - n_state_blocks)
    assert cast_rows * (n_steps - n_state_blocks) == w_out.shape[0] and cast_rows % (2 * SUBLANES) == 0
    wo_spec = pl.BlockSpec((cast_rows, D_MODEL), lambda i: (jnp.maximum(i - n_state_blocks, 0), 0))
    resident = lambda a: pl.BlockSpec(a.shape, lambda i: (0, 0), pipeline_mode=pl.Buffered(1))
    rowblk = lambda w: pl.BlockSpec((tm, w), lambda i: (i, 0))
    colblk = pl.BlockSpec((ATTN_WIDTH, tm), lambda i: (0, i))
    kvblk = pl.BlockSpec((KV_WIDTH, tm), lambda i: (0, i))
    tok = lambda w: pl.BlockSpec((steps, bb, w), lambda i: (0, jnp.minimum(i, last), 0))
    st_spec = pl.BlockSpec((bb, SSM_WIDTH, SSM_STATE), lambda i: (jnp.minimum(i, last), 0, 0))
    in_specs = ([rowblk(D_MODEL), _const_spec((1, D_MODEL)), resident(w_t)]
                + [_const_spec(conv_w.shape), _const_spec(conv_b.shape), _const_spec(dtb.shape)]
                + [resident(qw_t), resident(kw_t)]
                + [pl.BlockSpec(memory_space=pltpu.SMEM), st_spec, tok(BC_WIDTH), tok(BC_WIDTH), tok(SSM_WIDTH),
                   wo_spec])
    out_specs = [rowblk(SSM_WIDTH), rowblk(SSM_WIDTH), rowblk(BC_WIDTH), rowblk(BC_WIDTH), rowblk(LANES),
                 colblk, kvblk, kvblk, colblk,
                 pl.BlockSpec((1, SUBLANES, CONV_DIM), lambda i: (i // steps_per_seq, 0, 0)),
                 st_spec, tok(SSM_WIDTH), wo_spec]
    f = lambda r, c, dt=F32: jax.ShapeDtypeStruct((r, c), dt)
    out_shape = [f(rows, SSM_WIDTH), f(rows, SSM_WIDTH), f(rows, BC_WIDTH, BF16), f(rows, BC_WIDTH, BF16),
                 f(rows, LANES), f(ATTN_WIDTH, rows, BF16), f(KV_WIDTH, rows), f(KV_WIDTH, rows), f(ATTN_WIDTH, rows),
                 jax.ShapeDtypeStruct((batch, SUBLANES, CONV_DIM), F32),
                 jax.ShapeDtypeStruct(state.shape, F32), jax.ShapeDtypeStruct((steps, nb, SSM_WIDTH), F32),
                 jax.ShapeDtypeStruct(w_out.shape, BF16)]
    return pl.pallas_call(
        functools.partial(_inproj_prompt_body, steps_per_seq, n_state_blocks), grid=(n_steps,),
        in_specs=in_specs, out_specs=out_specs, out_shape=out_shape,
        scratch_shapes=[pltpu.VMEM((SUBLANES, CONV_DIM), F32)],
        compiler_params=_params("arbitrary"), name="inproj_prompt")(
            x2d, norm_w, w_t, conv_w, conv_b, dtb, qw_t, kw_t, cdec, state, c3, b3, xw3, w_out)


def _group_rmsnorm(gy, norm_w):
    parts = []
    for g in range(SSM_GROUPS):
        blk = gy[:, g * GROUP_WIDTH:(g + 1) * GROUP_WIDTH]
        ms = jnp.mean(blk * blk, axis=-1, keepdims=True)
        parts.append(blk * lax.rsqrt(ms + EPS))
    return jnp.concatenate(parts, axis=1) * norm_w


def _ssd_chunk(gz, xs, b_bf, c_bf, dt, a_neg, dskip, norm_w, e_mat, state):
    xs_bf = xs.astype(BF16)

    a = dt * a_neg
    li = lax.broadcasted_iota(jnp.int32, (CHUNK, CHUNK), 0)
    si = lax.broadcasted_iota(jnp.int32, (CHUNK, CHUNK), 1)
    causal = li >= si
    a_cum = _dot_sel3(jnp.where(causal, 1.0, 0.0).astype(BF16), a)
    a2 = a_cum * LOG2E
    row_term = a2.T - jnp.log2(dt.T)
    ea_full = _dot_sel(jnp.exp(a_cum), e_mat)
    w_full = _dot((dt * jnp.exp(a_cum[CHUNK - 1:CHUNK, :] - a_cum)).astype(BF16), e_mat)

    cb = [_dot_nt(c_bf[:, g * SSM_STATE:(g + 1) * SSM_STATE], b_bf[:, g * SSM_STATE:(g + 1) * SSM_STATE])
          for g in range(SSM_GROUPS)]
    half = lax.broadcasted_iota(jnp.int32, (CHUNK, LANES), 1) < SSM_HEAD_DIM
    heads_per_group = SSM_HEADS // SSM_GROUPS
    y_parts = []
    for pair in range(SSM_HEADS // 2):
        blocks = []
        for h in (2 * pair, 2 * pair + 1):
            seg = a2[:, h:h + 1] - row_term[h:h + 1, :]
            decay_dt = jnp.exp2(jnp.where(causal, seg, -jnp.inf))
            blocks.append((cb[h // heads_per_group] * decay_dt).astype(BF16))
        lhs = jnp.concatenate(blocks, axis=1)
        xp = xs_bf[:, pair * LANES:(pair + 1) * LANES]
        zero = jnp.zeros_like(xp)
        rhs = jnp.concatenate([jnp.where(half, xp, zero), jnp.where(half, zero, xp)], axis=0)
        y_parts.append(_dot(lhs, rhs))
    y_diag = jnp.concatenate(y_parts, axis=1)

    state_bf = state.astype(BF16)
    xw_bf = (xs * w_full).astype(BF16)
    y_off, upd = [], []
    for g in range(SSM_GROUPS):
        cols = slice(g * GROUP_WIDTH, (g + 1) * GROUP_WIDTH)
        ns = slice(g * SSM_STATE, (g + 1) * SSM_STATE)
        y_off.append(_dot(c_bf[:, ns], state_bf[:, cols]))
        upd.append(_dot_tn(b_bf[:, ns], xw_bf[:, cols]))
    y = y_diag + jnp.concatenate(y_off, axis=1) * ea_full + dskip * xs
    new_state = state * ea_full[CHUNK - 1:CHUNK, :] + jnp.concatenate(upd, axis=1)
    return _group_rmsnorm(y * gz, norm_w), new_state


def _rel_bucket_np(dist):
    max_exact = REL_BUCKETS // 2
    d_f = np.maximum(dist, 1).astype(np.float32)
    large = max_exact + (np.log(d_f / np.float32(max_exact)) / np.float32(math.log(REL_MAX_DIST / max_exact))
                         * np.float32(REL_BUCKETS - max_exact)).astype(np.int32)
    return np.where(dist < max_exact, dist, np.minimum(large, REL_BUCKETS - 1)).astype(np.int32)


def _bucket_or_masked(dist, extra_mask=None):
    ok = (dist >= 0) & (dist <= WINDOW)
    if extra_mask is not None:
        ok = ok & extra_mask
    return np.where(ok, _rel_bucket_np(np.clip(dist, 0, WINDOW)), -1).astype(np.int32)


def _sample_bias_tables(rel_ref, sink_ref, cache_bucket_ref, new_bucket_ref, cache_out_ref, new_out_ref):
    steps = cache_bucket_ref.shape[0]
    cache_bucket = cache_bucket_ref[...]
    new_bucket = new_bucket_ref[...]
    sink_column = lax.broadcasted_iota(jnp.int32, new_bucket.shape, 1) == steps
    for h in range(ATTN_HEADS):
        cache_bias = jnp.full(cache_bucket.shape, NEG, F32)
        new_bias = jnp.full(new_bucket.shape, NEG, F32)
        for bkt in range(REL_BUCKETS):
            entry = rel_ref[bkt * ATTN_HEADS + h]
            cache_bias = jnp.where(cache_bucket == bkt, entry, cache_bias)
            new_bias = jnp.where(new_bucket == bkt, entry, new_bias)
        new_bias = jnp.where(sink_column, sink_ref[h], new_bias)
        for bb in range(SAMPLE_BATCH_BLOCK):
            row = (bb * ATTN_HEADS + h) * steps
            cache_out_ref[row:row + steps, :] = cache_bias
            new_out_ref[row:row + steps, :] = new_bias


def _bias_body(rel_ref, sink_ref, dist_bucket_ref, cache_bucket_ref, new_bucket_ref,
               o_ref, cache_out_ref, new_out_ref):
    _sample_bias_tables(rel_ref, sink_ref, cache_bucket_ref, new_bucket_ref, cache_out_ref, new_out_ref)
    T = CHUNK
    width = dist_bucket_ref.shape[1]
    bucket = dist_bucket_ref[...]
    own_block = lax.broadcasted_iota(jnp.int32, (2 * T, T), 0) >= T

    def per_kv_head(n, carry):
        for g in range(Q_PER_KV):
            profile = jnp.full(bucket.shape, NEG, F32)
            for bkt in range(REL_BUCKETS):
                profile = jnp.where(bucket == bkt, rel_ref[bkt * ATTN_HEADS + n * Q_PER_KV + g], profile)
            rows = jnp.broadcast_to(profile[0:1, :] * LOG2E, (2 * T, width))
            tile = pltpu.roll(rows, width - T, axis=1, stride=1, stride_axis=0)[:, :T]
            o_ref[1, n, :, g * T:(g + 1) * T] = tile
            o_ref[0, n, :, g * T:(g + 1) * T] = jnp.where(own_block, tile, NEG)
        return carry

    lax.fori_loop(0, ATTN_KV_HEADS, per_kv_head, 0)


def _bias_tables(rel_flat, sink, dist_bucket, cache_bucket, new_bucket):
    steps = cache_bucket.shape[0]
    rows = SAMPLE_BATCH_BLOCK * ATTN_HEADS * steps
    f = lambda *s: jax.ShapeDtypeStruct(s, F32)
    smem, vmem = pl.BlockSpec(memory_space=pltpu.SMEM), pl.BlockSpec(memory_space=pltpu.VMEM)
    return pl.pallas_call(
        _bias_body,
        in_specs=[smem, smem, vmem, vmem, vmem],
        out_specs=[vmem, vmem, vmem],
        out_shape=[f(2, ATTN_KV_HEADS, 2 * CHUNK, Q_PER_KV * CHUNK), f(rows, cache_bucket.shape[1]),
                   f(rows, new_bucket.shape[1])],
        compiler_params=pltpu.CompilerParams(vmem_limit_bytes=VMEM_LIMIT), name="rel_bias")(
        rel_flat, sink, dist_bucket, cache_bucket, new_bucket)


def _head_rmsnorm(x, g_mat, e_mat, w):
    ms = _dot_sel(x * x, g_mat) * (1.0 / ATTN_HEAD_DIM)
    return x * _dot_sel(lax.rsqrt(ms + EPS), e_mat) * w


def _lane_head(shape):
    return lax.broadcasted_iota(jnp.int32, shape, 1) // ATTN_HEAD_DIM


def _attn_block(q_blk, kcat, vcat_t, bias_at, sink_ref):
    T = CHUNK
    lane_head = _lane_head((2 * T, KV_WIDTH))
    zero = jnp.zeros((2 * T, KV_WIDTH), BF16)
    head = lambda h: q_blk[h * ATTN_HEAD_DIM:(h + 1) * ATTN_HEAD_DIM]
    q_cols = jnp.concatenate(
        [jnp.concatenate([head(n * Q_PER_KV + g) for n in range(ATTN_KV_HEADS)], axis=0)
         for g in range(Q_PER_KV)], axis=1)
    row_head = lax.broadcasted_iota(jnp.int32, (KV_WIDTH, 2 * T), 0) // ATTN_HEAD_DIM
    probs, vals, inv = [], [], {}
    for n in range(ATTN_KV_HEADS):
        s = _dot(jnp.where(lane_head == n, kcat, zero), q_cols)
        sink = sink_ref[n]
        cols = []
        for g in range(Q_PER_KV):
            c = slice(g * T, (g + 1) * T)
            sg = s[:, c] + bias_at(n, c)
            m = jnp.maximum(jnp.max(sg, axis=0, keepdims=True), sink[:, c])
            p = jnp.exp2(sg - m)
            inv[n, g] = 1.0 / (jnp.sum(p, axis=0, keepdims=True) + jnp.exp2(sink[:, c] - m))
            cols.append(p.astype(BF16))
        probs.append(jnp.concatenate(cols, axis=1))
        vals.append(jnp.where(row_head == n, vcat_t, zero.T))
    o_t = _dot(jnp.concatenate(vals, axis=1), jnp.concatenate(probs, axis=0))
    return jnp.concatenate(
        [o_t[n * ATTN_HEAD_DIM:(n + 1) * ATTN_HEAD_DIM, g * T:(g + 1) * T] * inv[n, g]
         for n in range(ATTN_KV_HEADS) for g in range(Q_PER_KV)], axis=0)


def _mixer_body(gz_ref, xs_ref, b_ref, c_ref, dt_ref, alog_ref, dskip_ref, nw_ref, e_ref,
                qt_ref, kt_ref, vt_ref, gzt_ref, bias_ref, sink_ref, x_ref, wo_ref,
                y_ref, st_ref, knt_ref, vnt_ref,
                state_sc, kcat_sc, vcat_t_sc, yssm_sc, yattn_t_sc):
    T = CHUNK
    step = pl.program_id(1)
    cols_step = CHUNKS_PER_STEP * T

    @pl.when(step == 0)
    def _():
        state_sc[...] = jnp.zeros_like(state_sc)
        kcat_sc[0:T, :] = jnp.zeros((T, KV_WIDTH), BF16)
        vcat_t_sc[:, 0:T] = jnp.zeros((KV_WIDTH, T), BF16)

    qn = qt_ref[...]
    kn_t = kt_ref[...]
    v_t = vt_ref[...]
    knt_ref[0] = kn_t[:, cols_step - T:]
    vnt_ref[0] = v_t[:, cols_step - T:]
    kcat_sc[T:, :] = kn_t.T.astype(BF16)
    vcat_t_sc[:, T:] = v_t.astype(BF16)
    e_mat = e_ref[...]

    a_neg = -jnp.exp(alog_ref[...])
    state = state_sc[...]
    first_variant = jnp.minimum(step, 1)
    for j in range(CHUNKS_PER_STEP):
        r = slice(j * T, (j + 1) * T)
        y, state = _ssd_chunk(gz_ref[r, :], xs_ref[r, :], b_ref[r, :], c_ref[r, :], dt_ref[r, :], a_neg,
                              dskip_ref[...], nw_ref[...], e_mat, state)
        yssm_sc[r, :] = y.astype(BF16)
        variant = first_variant if j == 0 else 1
        y_t = _attn_block(qn[:, r], kcat_sc[j * T:(j + 2) * T, :], vcat_t_sc[:, j * T:(j + 2) * T],
                          lambda n, c, variant=variant: bias_ref[variant, n, :, c], sink_ref)
        yattn_t_sc[:, r] = (y_t * gzt_ref[:, r]).astype(BF16)
    state_sc[...] = state
    kcat_sc[0:T, :] = kcat_sc[cols_step:, :]
    vcat_t_sc[:, 0:T] = vcat_t_sc[:, cols_step:]

    y_ref[...] = (x_ref[...] + _dot(yssm_sc[...], wo_ref[:SSM_WIDTH, :])
                  + _dot_tn(yattn_t_sc[...], wo_ref[SSM_WIDTH:, :]))

    @pl.when(step == pl.num_programs(1) - 1)
    def _():
        st_ref[0] = state.T


def _mixer(gz, xs, b, c, dt, alog, dskip, norm_w, e_mat, q_t, k_t, v_t, gz_t, bias_t, sink_rows,
           x2d, w_out, batch, seq):
    step_rows = CHUNKS_PER_STEP * CHUNK
    ns = seq // step_rows
    row = lambda w: pl.BlockSpec((step_rows, w), lambda b, i: (b * ns + i, 0))
    col = lambda w: pl.BlockSpec((w, step_rows), lambda b, i: (0, b * ns + i))
    resident = lambda a: pl.BlockSpec(a.shape, lambda b, i: (0,) * a.ndim, pipeline_mode=pl.Buffered(1))
    in_specs = [row(SSM_WIDTH), row(SSM_WIDTH), row(BC_WIDTH), row(BC_WIDTH), row(LANES),
                _const_spec((1, LANES)), _const_spec((1, SSM_WIDTH)), _const_spec((1, SSM_WIDTH)), resident(e_mat),
                col(ATTN_WIDTH), col(KV_WIDTH), col(KV_WIDTH), col(ATTN_WIDTH),
                resident(bias_t), _const_spec(sink_rows.shape), row(D_MODEL), resident(w_out)]
    kv_out = pl.BlockSpec((1, KV_WIDTH, CHUNK), lambda b, i: (b, 0, 0))
    out_specs = [row(D_MODEL), pl.BlockSpec((1, SSM_WIDTH, SSM_STATE), lambda b, i: (b, 0, 0)), kv_out, kv_out]
    out_shape = [jax.ShapeDtypeStruct((batch * seq, D_MODEL), F32),
                 jax.ShapeDtypeStruct((batch, SSM_WIDTH, SSM_STATE), F32),
                 jax.ShapeDtypeStruct((batch, KV_WIDTH, CHUNK), F32),
                 jax.ShapeDtypeStruct((batch, KV_WIDTH, CHUNK), F32)]
    scratch = [pltpu.VMEM((SSM_STATE, SSM_WIDTH), F32),
               pltpu.VMEM((step_rows + CHUNK, KV_WIDTH), BF16), pltpu.VMEM((KV_WIDTH, step_rows + CHUNK), BF16),
               pltpu.VMEM((step_rows, SSM_WIDTH), BF16), pltpu.VMEM((ATTN_WIDTH, step_rows), BF16)]
    return pl.pallas_call(
        _mixer_body, grid=(batch, ns), in_specs=in_specs, out_specs=out_specs, out_shape=out_shape,
        scratch_shapes=scratch, compiler_params=_params("arbitrary", "arbitrary"), name="mixer")(
            gz, xs, b, c, dt, alog, dskip, norm_w, e_mat, q_t, k_t, v_t, gz_t, bias_t, sink_rows, x2d, w_out)


def _sample_front_body(x_ref, nw_ref, wt_ref, sconv_ref, cw_ref, cb_ref, dtb_ref, alog_ref, dskip_ref,
                       gh_ref, e_ref, qw_ref, kw_ref, g_ref,
                       ypart_ref, ea_ref, xw_ref, b_ref, c_ref, cdec_ref, convnew_ref,
                       gz_ref, qn_ref, kn_ref, v_ref, za_ref):
    steps, nb = ypart_ref.shape[0], ypart_ref.shape[1]
    tail = CONV_WIDTH - 1
    h = jnp.concatenate([_normed_rows(x_ref[:, l, :], nw_ref) for l in range(steps)], axis=0)
    proj = lambda name: _dot_nt(h, wt_ref[IN_ROWS[name], :])
    slab = lambda a, l: a[l * nb:(l + 1) * nb]
    xbc = proj("xbc")
    dt_raw = _dt_projection(h, wt_ref)
    gz_ref[...] = _silu(proj("z"))
    v_ref[...] = proj("v")
    za_ref[...] = proj("za")
    g_mat = g_ref[...]
    e_mat = e_ref[...]
    qn = _head_rmsnorm(proj("q"), g_mat, e_mat, qw_ref[...]) * ATTN_SCALE
    head = lambda hd: qn[:, hd * ATTN_HEAD_DIM:(hd + 1) * ATTN_HEAD_DIM]
    qn_ref[...] = jnp.concatenate(
        [head(n * Q_PER_KV + g) for g in range(Q_PER_KV) for n in range(ATTN_KV_HEADS)], axis=1)
    kn_ref[...] = _head_rmsnorm(proj("k"), g_mat[:KV_WIDTH], e_mat[:, :KV_WIDTH], kw_ref[...])

    full = [sconv_ref[j] for j in range(tail)] + [slab(xbc, l) for l in range(steps)]
    for j in range(tail):
        convnew_ref[j] = full[steps + j]
    gh = gh_ref[...]
    a_neg = -jnp.exp(alog_ref[...])
    xs, bm, cm, dts, acum = [], [], [], [], []
    run = None
    for l in range(steps):
        conv = cb_ref[...]
        for tap in range(CONV_WIDTH):
            conv = conv + full[l + tap] * cw_ref[tap:tap + 1, :]
        act = _silu(conv)
        xs.append(act[:, :SSM_WIDTH])
        bm.append(act[:, SSM_WIDTH:SSM_WIDTH + BC_WIDTH])
        cm.append(act[:, SSM_WIDTH + BC_WIDTH:])
        d = _softplus(slab(dt_raw, l) + dtb_ref[...])
        dts.append(d)
        run = d * a_neg if run is None else run + d * a_neg
        acum.append(run)
        b_ref[l] = bm[l]
        c_ref[l] = cm[l]
    for l in range(steps):
        y = dskip_ref[...] * xs[l]
        for s in range(l + 1):
            cb_h = _dot_sel(cm[l] * bm[s], gh)
            coef = cb_h * jnp.exp(acum[l] - acum[s]) * dts[s]
            y = y + _dot_sel(coef, e_mat) * xs[s]
        ypart_ref[l] = y
        ea_ref[l] = _dot_sel(jnp.exp(acum[l]), e_mat)
        xw_ref[l] = xs[l] * _dot_sel(dts[l] * jnp.exp(acum[steps - 1] - acum[l]), e_mat)
    cdec_ref[...] = jnp.exp(acum[steps - 1])


def _sample_front(x3, norm_w, w_t, sconv3, conv_w, conv_b, dtb, alog, dskip, gh_mat, e_mat, qw, kw, g_mat,
                  steps, nb):
    rows = steps * nb
    f = lambda *s: jax.ShapeDtypeStruct(s, F32)
    out_shape = [f(steps, nb, SSM_WIDTH), f(steps, nb, SSM_WIDTH), f(steps, nb, SSM_WIDTH),
                 f(steps, nb, BC_WIDTH), f(steps, nb, BC_WIDTH), f(nb, LANES), f(CONV_WIDTH - 1, nb, CONV_DIM),
                 f(rows, SSM_WIDTH), f(rows, ATTN_WIDTH), f(rows, KV_WIDTH), f(rows, KV_WIDTH), f(rows, ATTN_WIDTH)]
    return pl.pallas_call(_sample_front_body, out_shape=out_shape,
                          compiler_params=pltpu.CompilerParams(vmem_limit_bytes=VMEM_LIMIT),
                          name="sample_front")(
        x3, norm_w, w_t, sconv3, conv_w, conv_b, dtb, alog, dskip, gh_mat, e_mat, qw, kw, g_mat)


def _ssd_sample_state_block(i, cdec_ref, st_ref, c_ref, b_ref, xw_ref, new_ref, yoff_ref):
    heads_per_group = SSM_HEADS // SSM_GROUPS
    for j in range(SAMPLE_BATCH_BLOCK):
        st = st_ref[j]
        cb_bf = c_ref[:, j, :].astype(BF16)
        bb_bf = b_ref[:, j, :].astype(BF16)
        xw_bf = xw_ref[:, j, :].astype(BF16)
        y_parts = []
        for g in range(SSM_GROUPS):
            rows = slice(g * GROUP_WIDTH, (g + 1) * GROUP_WIDTH)
            ns = slice(g * SSM_STATE, (g + 1) * SSM_STATE)
            y_parts.append(_dot_nt(cb_bf[:, ns], st[rows].astype(BF16)))
            upd = _dot_tn(xw_bf[:, rows], bb_bf[:, ns])
            for hh in range(heads_per_group):
                h = g * heads_per_group + hh
                r = slice(h * SSM_HEAD_DIM, (h + 1) * SSM_HEAD_DIM)
                dec = cdec_ref[i * SAMPLE_BATCH_BLOCK + j, h]
                new_ref[j, r, :] = st[r] * dec + upd[hh * SSM_HEAD_DIM:(hh + 1) * SSM_HEAD_DIM]
        yoff_ref[:, j, :] = jnp.concatenate(y_parts, axis=1)


def _sample_back_body(ypart_ref, yoff_ref, ea_ref, gz_ref, nw_ref, ya_ref, x_ref, wo_ref, o_ref):
    y = ypart_ref[...] + yoff_ref[...] * ea_ref[...]
    y_ssm = _group_rmsnorm(y * gz_ref[...], nw_ref[...])
    mixed = _dot(y_ssm.astype(BF16), wo_ref[:SSM_WIDTH, :]) + _dot(ya_ref[...].astype(BF16), wo_ref[SSM_WIDTH:, :])
    nb, steps = o_ref.shape[0], o_ref.shape[1]
    for l in range(steps):
        o_ref[:, l, :] = x_ref[:, l, :] + mixed[l * nb:(l + 1) * nb]


def _sample_back(ypart, yoff, ea, gz, norm_w, y_attn, x3, w_out):
    return pl.pallas_call(_sample_back_body, out_shape=jax.ShapeDtypeStruct(x3.shape, F32),
                          compiler_params=pltpu.CompilerParams(vmem_limit_bytes=VMEM_LIMIT),
                          name="sample_back")(ypart, yoff, ea, gz, norm_w, y_attn, x3, w_out)


def _attn_sample_body(q_ref, kn_ref, vn_ref, z_ref, ckt_ref, cvt_ref, biasc_ref, biasn_ref,
                      y_ref, kot_ref, vot_ref):
    steps = q_ref.shape[0]
    bb = SAMPLE_BATCH_BLOCK
    blk = Q_PER_KV * steps
    rows = ATTN_KV_HEADS * blk
    pad = jnp.zeros((SUBLANES - steps, KV_WIDTH), F32)
    lane_head = _lane_head((blk, KV_WIDTH))
    zero = jnp.zeros((blk, KV_WIDTH), F32)

    s_c, s_n, k8, v8 = [], [], [], []
    for j in range(bb):
        q = q_ref[:, j, :]
        qg = jnp.concatenate([q[:, g * KV_WIDTH:(g + 1) * KV_WIDTH] for g in range(Q_PER_KV)], axis=0)
        qx = jnp.concatenate([jnp.where(lane_head == n, qg, zero) for n in range(ATTN_KV_HEADS)], axis=0)
        qx = qx.astype(BF16)
        k8.append(jnp.concatenate([kn_ref[:, j, :], pad], axis=0))
        v8.append(jnp.concatenate([vn_ref[:, j, :], pad], axis=0))
        s_c.append(_dot(qx, ckt_ref[j].astype(BF16)))
        s_n.append(_dot_nt(qx, k8[j].astype(BF16)))
    s_c = jnp.concatenate(s_c, axis=0) + biasc_ref[...]
    s_n = jnp.concatenate(s_n, axis=0) + biasn_ref[...]
    m = jnp.maximum(jnp.max(s_c, axis=-1, keepdims=True), jnp.max(s_n, axis=-1, keepdims=True))
    p_c = jnp.exp(s_c - m)
    p_n = jnp.exp(s_n - m)
    inv = 1.0 / (jnp.sum(p_c, axis=-1, keepdims=True) + jnp.sum(p_n, axis=-1, keepdims=True))
    p_c = (p_c * inv).astype(BF16)
    p_n = (p_n * inv).astype(BF16)

    lane = lax.broadcasted_iota(jnp.int32, (KV_WIDTH, WINDOW), 1)
    for j in range(bb):
        r = slice(j * rows, (j + 1) * rows)
        o = _dot_nt(p_c[r], cvt_ref[j].astype(BF16)) + _dot(p_n[r], v8[j].astype(BF16))
        og = zero
        for n in range(ATTN_KV_HEADS):
            og = og + jnp.where(lane_head == n, o[n * blk:(n + 1) * blk], zero)
        y = jnp.concatenate(
            [og[g * steps:(g + 1) * steps, n * ATTN_HEAD_DIM:(n + 1) * ATTN_HEAD_DIM]
             for n in range(ATTN_KV_HEADS) for g in range(Q_PER_KV)], axis=1)
        y_ref[:, j, :] = y * _silu(z_ref[:, j, :])

        for new8, old_ref, out_ref in ((k8[j], ckt_ref, kot_ref), (v8[j], cvt_ref, vot_ref)):
            tail_rows = jnp.concatenate([new8[steps:], new8[:steps]], axis=0)
            block = jnp.concatenate([jnp.zeros((WINDOW - SUBLANES, KV_WIDTH), F32), tail_rows], axis=0)
            shifted = pltpu.roll(old_ref[j], WINDOW - steps, axis=1)
            out_ref[j] = jnp.where(lane >= WINDOW - steps, block.T, shifted)


def _attn_sample(q3, kn3, vn3, z3, cache_kt, cache_vt, bias_c, bias_n):
    steps, nb = q3.shape[0], q3.shape[1]
    bb = SAMPLE_BATCH_BLOCK
    tok = lambda w: pl.BlockSpec((steps, bb, w), lambda i: (0, i, 0))
    cache_spec = pl.BlockSpec((bb, KV_WIDTH, WINDOW), lambda i: (i, 0, 0))
    return pl.pallas_call(
        _attn_sample_body, grid=(nb // bb,),
        in_specs=[tok(ATTN_WIDTH), tok(KV_WIDTH), tok(KV_WIDTH), tok(ATTN_WIDTH), cache_spec, cache_spec,
                  _const_spec(bias_c.shape), _const_spec(bias_n.shape)],
        out_specs=[tok(ATTN_WIDTH), cache_spec, cache_spec],
        out_shape=[jax.ShapeDtypeStruct((steps, nb, ATTN_WIDTH), F32),
                   jax.ShapeDtypeStruct(cache_kt.shape, F32), jax.ShapeDtypeStruct(cache_vt.shape, F32)],
        compiler_params=_params("parallel"), name="attn_sample")(
            q3, kn3, vn3, z3, cache_kt, cache_vt, bias_c, bias_n)


def _static_tables(steps):
    lanes = np.arange(ATTN_WIDTH)
    g_mat = np.zeros((ATTN_WIDTH, LANES), np.float32)
    g_mat[lanes, lanes // ATTN_HEAD_DIM] = 1.0
    e_mat = g_mat.T.copy()
    bc = np.arange(BC_WIDTH)
    gh_mat = np.zeros((BC_WIDTH, LANES), np.float32)
    for h in range(SSM_HEADS):
        gh_mat[bc // SSM_STATE == h // (SSM_HEADS // SSM_GROUPS), h] = 1.0
    assert WINDOW <= CHUNK
    prompt_buckets = np.tile(_bucket_or_masked(np.arange(2 * CHUNK))[None, :], (SUBLANES, 1))
    dist_c = (np.arange(steps) + WINDOW)[:, None] - np.arange(WINDOW)[None, :]
    dist_n = np.arange(steps)[:, None] - np.arange(SUBLANES)[None, :]
    real = np.broadcast_to((np.arange(SUBLANES) < steps)[None, :], dist_n.shape)
    return dict(g=g_mat, e=e_mat, gh=gh_mat, prompt_buckets=prompt_buckets,
                cache_buckets=_bucket_or_masked(dist_c), new_buckets=_bucket_or_masked(dist_n, real))


def kernel(x_prompt, x_sample, cache_k, cache_v, state_ssm, state_conv, norm_w, w_in, conv_w, conv_b, dt_bias,
           a_log, d_skip, ssm_norm_w, q_norm_w, k_norm_w, sinks, rel_table, w_out):
    assert w_in.shape[0] == 1, "single-layer kernel"
    batch, seq, _ = x_prompt.shape
    nb, steps, _ = x_sample.shape
    tab = _static_tables(steps)
    g_mat = jnp.asarray(tab["g"], BF16)
    e_mat = jnp.asarray(tab["e"], BF16)
    gh_mat = jnp.asarray(tab["gh"], BF16)

    w_t = jnp.transpose(w_in[0]).astype(BF16)

    row = lambda v, width: jnp.pad(v.reshape(1, -1), ((0, 0), (0, width - v.size)))
    nw = row(norm_w[0], D_MODEL)
    cw = conv_w[0]
    cb = row(conv_b[0], CONV_DIM)
    dtb = row(dt_bias[0], LANES)
    alog = row(a_log[0], LANES)
    dskip = jnp.repeat(d_skip[0], SSM_HEAD_DIM).reshape(1, SSM_WIDTH)
    snw = row(ssm_norm_w[0], SSM_WIDTH)
    qw = jnp.tile(q_norm_w[0], ATTN_HEADS).reshape(1, ATTN_WIDTH)
    kw = jnp.tile(k_norm_w[0], ATTN_KV_HEADS).reshape(1, KV_WIDTH)
    sink = sinks[0]
    rel_flat = rel_table.reshape(-1)

    t3 =lambda a: a.reshape(steps, nb, a.shape[-1])
    sconv3 = jnp.swapaxes(state_conv[0], 0, 1)
    ypart, ea, xw, b3, c3, cdec, conv_s3, gz_smp, qn, kn, v_smp, za = _sample_front(
        x_sample, nw, w_t, sconv3, cw, cb, dtb, alog, dskip, gh_mat, e_mat, qw, kw, g_mat, steps, nb)
    state_in = state_ssm[0].reshape(nb, SSM_WIDTH, SSM_STATE)

    xp = x_prompt.reshape(batch * seq, D_MODEL)
    qw_t = jnp.broadcast_to((qw * (ATTN_SCALE * LOG2E)).reshape(ATTN_WIDTH, 1), (ATTN_WIDTH, PROJ_ROWS))
    kw_t = jnp.broadcast_to(kw.reshape(KV_WIDTH, 1), (KV_WIDTH, PROJ_ROWS))
    gz, xs_p, b_p, c_p, dt_p, q_t, k_t, v_t, gza_t, tail_p, st_s, yoff, wo_all = _inproj_prompt(
        xp, nw, w_t, cw, cb, dtb, qw_t, kw_t, batch, seq, cdec, state_in, c3, b3, xw,
        w_out[0])
    bias_t, bias_c, bias_n = _bias_tables(rel_flat, sink, jnp.asarray(tab["prompt_buckets"]),
                                          jnp.asarray(tab["cache_buckets"]), jnp.asarray(tab["new_buckets"]))
    sink_rows = jnp.repeat(sink.reshape(ATTN_KV_HEADS, Q_PER_KV) * LOG2E, CHUNK, axis=1)
    sink_rows = sink_rows.reshape(ATTN_KV_HEADS, 1, -1)
    y_p, st_p, k_pt, v_pt = _mixer(gz, xs_p, b_p, c_p, dt_p, alog, dskip, snw, e_mat, q_t, k_t, v_t, gza_t,
                                   bias_t, sink_rows, xp, wo_all, batch, seq)
    y_p = y_p.reshape(batch, seq, D_MODEL)
    conv_p = tail_p[:, SUBLANES - (CONV_WIDTH - 1):, :]

    f2 = lambda a: a.reshape(steps * nb, a.shape[-1])
    to_t = lambda a: jnp.transpose(a[0], (0, 2, 3, 1)).reshape(nb, KV_WIDTH, WINDOW)
    from_t = lambda a: jnp.transpose(
        a.reshape(a.shape[0], ATTN_KV_HEADS, ATTN_HEAD_DIM, WINDOW), (0, 3, 1, 2))[None]
    y_attn3, k_st, v_st = _attn_sample(t3(qn), t3(kn), t3(v_smp), t3(za), to_t(cache_k), to_t(cache_v),
                                       bias_c, bias_n)
    k_s, v_s = from_t(k_st), from_t(v_st)
    y_s = _sample_back(f2(ypart), f2(yoff), f2(ea), gz_smp, snw, f2(y_attn3), x_sample, wo_all)

    st5 = lambda a: a.reshape(1, a.shape[0], SSM_HEADS, SSM_HEAD_DIM, SSM_STATE)
    return (y_p, y_s, from_t(k_pt), from_t(v_pt), st5(st_p), conv_p[None],
            k_s, v_s, st5(st_s), jnp.swapaxes(conv_s3, 0, 1)[None])
```

```python
import functools
import math

import numpy as np
import jax
import jax.numpy as jnp
from jax import lax
from jax.experimental import pallas as pl
from jax.experimental.pallas import tpu as pltpu

F32 = jnp.float32
BF16 = jnp.bfloat16

D_MODEL = 1024
SSM_HEADS = 16
SSM_HEAD_DIM = 64
SSM_WIDTH = SSM_HEADS * SSM_HEAD_DIM
SSM_GROUPS = 2
SSM_STATE = 128
GROUP_WIDTH = SSM_WIDTH // SSM_GROUPS
BC_WIDTH = SSM_GROUPS * SSM_STATE
CONV_WIDTH = 4
CONV_DIM = SSM_WIDTH + 2 * BC_WIDTH
CHUNK = 128
ATTN_HEADS = 16
ATTN_KV_HEADS = 4
Q_PER_KV = ATTN_HEADS // ATTN_KV_HEADS
ATTN_HEAD_DIM = 64
ATTN_WIDTH = ATTN_HEADS * ATTN_HEAD_DIM
KV_WIDTH = ATTN_KV_HEADS * ATTN_HEAD_DIM
WINDOW = 128
ATTN_SCALE = ATTN_HEAD_DIM ** -0.5
REL_BUCKETS = 32
REL_MAX_DIST = 128
EPS = 1e-6
LOG2E = 1.0 / math.log(2.0)
NEG = -1e30

LANES = 128
SUBLANES = 8
MXU_WIDTH = 256
VMEM_LIMIT = 56 * 1024 * 1024


def _in_proj_rows():
    widths = (("z", SSM_WIDTH), ("xbc", CONV_DIM), ("dt", SSM_HEADS), ("q", ATTN_WIDTH), ("k", KV_WIDTH),
              ("v", KV_WIDTH), ("za", ATTN_WIDTH))
    rows, start = {}, 0
    for name, width in widths:
        rows[name] = slice(start, start + width)
        start += width
    return rows


IN_ROWS = _in_proj_rows()
DT_ROWS = slice(IN_ROWS["dt"].start, IN_ROWS["dt"].start + LANES)

PROJ_ROWS = 512
CHUNKS_PER_STEP = 4
SAMPLE_BATCH_BLOCK = 8


def _dot(a, b):
    return jnp.dot(a, b, preferred_element_type=F32)


def _dot_nt(a, b):
    return lax.dot_general(a, b, (((1,), (1,)), ((), ())), preferred_element_type=F32)


def _dot_tn(a, b):
    return lax.dot_general(a, b, (((0,), (0,)), ((), ())), preferred_element_type=F32)


def _split2(v):
    hi = v.astype(BF16)
    lo = (v - hi.astype(F32)).astype(BF16)
    return hi, lo


def _dot_sel(v, m):
    hi, lo = _split2(v)
    if 2 * v.shape[1] <= MXU_WIDTH:
        return _dot(jnp.concatenate([hi, lo], axis=1), jnp.concatenate([m, m], axis=0))
    return _dot(hi, m) + _dot(lo, m)


def _dot_sel3(m, v):
    hi = v.astype(BF16)
    r1 = v - hi.astype(F32)
    mid = r1.astype(BF16)
    lo = (r1 - mid.astype(F32)).astype(BF16)
    return _dot(m, hi) + _dot(m, mid) + _dot(m, lo)


def _silu(x):
    return x / (1.0 + jnp.exp(-x))


def _softplus(x):
    return jnp.maximum(x, 0.0) + jnp.log1p(jnp.exp(-jnp.abs(x)))


def _params(*sem):
    return pltpu.CompilerParams(dimension_semantics=sem, vmem_limit_bytes=VMEM_LIMIT)


def _const_spec(shape):
    nd = len(shape)
    return pl.BlockSpec(shape, lambda *_: (0,) * nd)


def _normed_input(x_ref, nw_ref):
    x = x_ref[...]
    ms = jnp.mean(x * x, axis=-1, keepdims=True)
    return (x * lax.rsqrt(ms + EPS) * nw_ref[...]).astype(BF16)


def _dt_projection(h, wt_ref):
    raw = _dot_nt(h, wt_ref[DT_ROWS, :])
    return jnp.where(lax.broadcasted_iota(jnp.int32, raw.shape, 1) < SSM_HEADS, raw, 0.0)


def _shift_rows(u, prev_tail, k):
    rows, width = u.shape
    tiles = jnp.concatenate([prev_tail, u], axis=0).reshape(rows // SUBLANES + 1, SUBLANES, width)
    rot = jnp.concatenate([tiles[:, SUBLANES - k:], tiles[:, :SUBLANES - k]], axis=1)
    first = lax.broadcasted_iota(jnp.int32, (1, SUBLANES, width), 1) < k
    return jnp.where(first, rot[:-1], rot[1:]).reshape(rows, width)


def _inproj_prompt_body(steps_per_seq, n_state_blocks, x_ref, nw_ref, wt_ref, cw_ref, cb_ref, dtb_ref,
                        qwt_ref, kwt_ref, cdec_ref, st_ref, sc_ref, sb_ref, sxw_ref, wo_ref,
                        gz_ref, xs_ref, b_ref, c_ref, dt_ref, qt_ref, kt_ref, vt_ref, gzat_ref, tail_ref,
                        newst_ref, yoff_ref, wo16_ref, tail_sc):
    step = pl.program_id(0)

    @pl.when(step < n_state_blocks)
    def _():
        _ssd_sample_state_block(step, cdec_ref, st_ref, sc_ref, sb_ref, sxw_ref, newst_ref, yoff_ref)

    @pl.when(step >= n_state_blocks)
    def _():
        wo16_ref[...] = wo_ref[...].astype(BF16)

    @pl.when(step % steps_per_seq == 0)
    def _():
        tail_sc[...] = jnp.zeros_like(tail_sc)

    h = _normed_input(x_ref, nw_ref)
    rows = h.shape[0]
    w_tile = lambda name, j: wt_ref[IN_ROWS[name].start + j * MXU_WIDTH:IN_ROWS[name].start + (j + 1) * MXU_WIDTH, :]
    n_side = SSM_WIDTH // MXU_WIDTH
    for j in range(CONV_DIM // MXU_WIDTH):
        cols = slice(j * MXU_WIDTH, (j + 1) * MXU_WIDTH)
        if j < n_side:
            both = _dot_nt(h, jnp.concatenate([w_tile("xbc", j), w_tile("z", j)], axis=0))
            u = both[:, :MXU_WIDTH]
            gz_ref[:, cols] = _silu(both[:, MXU_WIDTH:])
        else:
            u = _dot_nt(h, w_tile("xbc", j))
        prev_tail = tail_sc[:, cols]
        conv = cb_ref[:, cols] + u * cw_ref[CONV_WIDTH - 1:CONV_WIDTH, cols]
        for k in range(1, CONV_WIDTH):
            tap = CONV_WIDTH - 1 - k
            conv = conv + _shift_rows(u, prev_tail, k) * cw_ref[tap:tap + 1, cols]
        new_tail = u[rows - SUBLANES:, :]
        tail_sc[:, cols] = new_tail
        tail_ref[0, :, cols] = new_tail
        act = _silu(conv)
        if j < SSM_WIDTH // MXU_WIDTH:
            xs_ref[:, cols] = act
        elif j == SSM_WIDTH // MXU_WIDTH:
            b_ref[...] = act.astype(BF16)
        else:
            c_ref[...] = act.astype(BF16)

    for j in range(ATTN_WIDTH // MXU_WIDTH):
        feats = slice(j * MXU_WIDTH, (j + 1) * MXU_WIDTH)
        gzat_ref[feats, :] = _silu(_dot_nt(w_tile("za", j), h))
    dt_ref[...] = _softplus(_dt_projection(h, wt_ref) + dtb_ref[...])

    def head_norm(t, w):
        x3 = t.reshape(-1, ATTN_HEAD_DIM, rows)
        ms = jnp.mean(x3 * x3, axis=1, keepdims=True)
        return (x3 * lax.rsqrt(ms + EPS)).reshape(t.shape) * w

    for j in range(ATTN_WIDTH // MXU_WIDTH):
        feats = slice(j * MXU_WIDTH, (j + 1) * MXU_WIDTH)
        qt_ref[feats, :] = head_norm(_dot_nt(w_tile("q", j), h), qwt_ref[feats, :]).astype(BF16)
    kt_ref[...] = head_norm(_dot_nt(wt_ref[IN_ROWS["k"], :], h), kwt_ref[...])
    vt_ref[...] = _dot_nt(wt_ref[IN_ROWS["v"], :], h)


def _inproj_prompt(x2d, norm_w, w_t, conv_w, conv_b, dtb, qw_t, kw_t, batch, seq, cdec, state, c3, b3, xw3,
                   w_out):
    rows = x2d.shape[0]
    tm = PROJ_ROWS
    steps_per_seq = seq // tm
    steps, nb = c3.shape[0], c3.shape[1]
    bb = SAMPLE_BATCH_BLOCK
    n_state_blocks = nb // bb
    n_steps = rows // tm
    assert n_state_blocks < n_steps, "state blocks and weight-cast blocks share the in-proj grid steps"
    last = n_state_blocks - 1
    cast_rows = w_out.shape[0] // (n_steps - n_state_blocks)
    assert cast_rows * (n_steps - n_state_blocks) == w_out.shape[0] and cast_rows % (2 * SUBLANES) == 0
    wo_spec = pl.BlockSpec((cast_rows, D_MODEL), lambda i: (jnp.maximum(i - n_state_blocks, 0), 0))
    resident = lambda a: pl.BlockSpec(a.shape, lambda i: (0, 0), pipeline_mode=pl.Buffered(1))
    rowblk = lambda w: pl.BlockSpec((tm, w), lambda i: (i, 0))
    colblk = pl.BlockSpec((ATTN_WIDTH, tm), lambda i: (0, i))
    kvblk = pl.BlockSpec((KV_WIDTH, tm), lambda i: (0, i))
    tok = lambda w: pl.BlockSpec((steps, bb, w), lambda i: (0, jnp.minimum(i, last), 0))
    st_spec = pl.BlockSpec((bb, SSM_WIDTH, SSM_STATE), lambda i: (jnp.minimum(i, last), 0, 0))
    in_specs = ([rowblk(D_MODEL), _const_spec((1, D_MODEL)), resident(w_t)]
                + [_const_spec(conv_w.shape), _const_spec(conv_b.shape), _const_spec(dtb.shape)]
                + [resident(qw_t), resident(kw_t)]
                + [pl.BlockSpec(memory_space=pltpu.SMEM), st_spec, tok(BC_WIDTH), tok(BC_WIDTH), tok(SSM_WIDTH),
                   wo_spec])
    out_specs = [rowblk(SSM_WIDTH), rowblk(SSM_WIDTH), rowblk(BC_WIDTH), rowblk(BC_WIDTH), rowblk(LANES),
                 colblk, kvblk, kvblk, colblk,
                 pl.BlockSpec((1, SUBLANES, CONV_DIM), lambda i: (i // steps_per_seq, 0, 0)),
                 st_spec, tok(SSM_WIDTH), wo_spec]
    f = lambda r, c, dt=F32: jax.ShapeDtypeStruct((r, c), dt)
    out_shape = [f(rows, SSM_WIDTH), f(rows, SSM_WIDTH), f(rows, BC_WIDTH, BF16), f(rows, BC_WIDTH, BF16),
                 f(rows, LANES), f(ATTN_WIDTH, rows, BF16), f(KV_WIDTH, rows), f(KV_WIDTH, rows), f(ATTN_WIDTH, rows),
                 jax.ShapeDtypeStruct((batch, SUBLANES, CONV_DIM), F32),
                 jax.ShapeDtypeStruct(state.shape, F32), jax.ShapeDtypeStruct((steps, nb, SSM_WIDTH), F32),
                 jax.ShapeDtypeStruct(w_out.shape, BF16)]
    return pl.pallas_call(
        functools.partial(_inproj_prompt_body, steps_per_seq, n_state_blocks), grid=(n_steps,),
        in_specs=in_specs, out_specs=out_specs, out_shape=out_shape,
        scratch_shapes=[pltpu.VMEM((SUBLANES, CONV_DIM), F32)],
        compiler_params=_params("arbitrary"), name="inproj_prompt")(
            x2d, norm_w, w_t, conv_w, conv_b, dtb, qw_t, kw_t, cdec, state, c3, b3, xw3, w_out)


def _group_rmsnorm(gy, norm_w):
    parts = []
    for g in range(SSM_GROUPS):
        blk = gy[:, g * GROUP_WIDTH:(g + 1) * GROUP_WIDTH]
        ms = jnp.mean(blk * blk, axis=-1, keepdims=True)
        parts.append(blk * lax.rsqrt(ms + EPS))
    return jnp.concatenate(parts, axis=1) * norm_w


def _ssd_chunk(gz, xs, b_bf, c_bf, dt, a_neg, dskip, norm_w, e_mat, state):
    xs_bf = xs.astype(BF16)

    a = dt * a_neg
    li = lax.broadcasted_iota(jnp.int32, (CHUNK, CHUNK), 0)
    si = lax.broadcasted_iota(jnp.int32, (CHUNK, CHUNK), 1)
    causal = li >= si
    a_cum = _dot_sel3(jnp.where(causal, 1.0, 0.0).astype(BF16), a)
    a2 = a_cum * LOG2E
    row_term = a2.T - jnp.log2(dt.T)
    ea_full = _dot_sel(jnp.exp(a_cum), e_mat)
    w_full = _dot((dt * jnp.exp(a_cum[CHUNK - 1:CHUNK, :] - a_cum)).astype(BF16), e_mat)

    cb = [_dot_nt(c_bf[:, g * SSM_STATE:(g + 1) * SSM_STATE], b_bf[:, g * SSM_STATE:(g + 1) * SSM_STATE])
          for g in range(SSM_GROUPS)]
    half = lax.broadcasted_iota(jnp.int32, (CHUNK, LANES), 1) < SSM_HEAD_DIM
    heads_per_group = SSM_HEADS // SSM_GROUPS
    y_parts = []
    for pair in range(SSM_HEADS // 2):
        blocks = []
        for h in (2 * pair, 2 * pair + 1):
            seg = a2[:, h:h + 1] - row_term[h:h + 1, :]
            decay_dt = jnp.exp2(jnp.where(causal, seg, -jnp.inf))
            blocks.append((cb[h // heads_per_group] * decay_dt).astype(BF16))
        lhs = jnp.concatenate(blocks, axis=1)
        xp = xs_bf[:, pair * LANES:(pair + 1) * LANES]
        zero = jnp.zeros_like(xp)
        rhs = jnp.concatenate([jnp.where(half, xp, zero), jnp.where(half, zero, xp)], axis=0)
        y_parts.append(_dot(lhs, rhs))
    y_diag = jnp.concatenate(y_parts, axis=1)

    state_bf = state.astype(BF16)
    xw_bf = (xs * w_full).astype(BF16)
    y_off, upd = [], []
    for g in range(SSM_GROUPS):
        cols = slice(g * GROUP_WIDTH, (g + 1) * GROUP_WIDTH)
        ns = slice(g * SSM_STATE, (g + 1) * SSM_STATE)
        y_off.append(_dot(c_bf[:, ns], state_bf[:, cols]))
        upd.append(_dot_tn(b_bf[:, ns], xw_bf[:, cols]))
    y = y_diag + jnp.concatenate(y_off, axis=1) * ea_full + dskip * xs
    new_state = state * ea_full[CHUNK - 1:CHUNK, :] + jnp.concatenate(upd, axis=1)
    return _group_rmsnorm(y * gz, norm_w), new_state


def _rel_bucket_np(dist):
    max_exact = REL_BUCKETS // 2
    d_f = np.maximum(dist, 1).astype(np.float32)
    large = max_exact + (np.log(d_f / np.float32(max_exact)) / np.float32(math.log(REL_MAX_DIST / max_exact))
                         * np.float32(REL_BUCKETS - max_exact)).astype(np.int32)
    return np.where(dist < max_exact, dist, np.minimum(large, REL_BUCKETS - 1)).astype(np.int32)


def _bucket_or_masked(dist, extra_mask=None):
    ok = (dist >= 0) & (dist <= WINDOW)
    if extra_mask is not None:
        ok = ok & extra_mask
    return np.where(ok, _rel_bucket_np(np.clip(dist, 0, WINDOW)), -1).astype(np.int32)


def _sample_bias_tables(rel_ref, sink_ref, cache_bucket_ref, new_bucket_ref, cache_out_ref, new_out_ref):
    steps = cache_bucket_ref.shape[0]
    cache_bucket = cache_bucket_ref[...]
    new_bucket = new_bucket_ref[...]
    sink_column = lax.broadcasted_iota(jnp.int32, new_bucket.shape, 1) == steps
    for h in range(ATTN_HEADS):
        cache_bias = jnp.full(cache_bucket.shape, NEG, F32)
        new_bias = jnp.full(new_bucket.shape, NEG, F32)
        for bkt in range(REL_BUCKETS):
            entry = rel_ref[bkt * ATTN_HEADS + h]
            cache_bias = jnp.where(cache_bucket == bkt, entry, cache_bias)
            new_bias = jnp.where(new_bucket == bkt, entry, new_bias)
        new_bias = jnp.where(sink_column, sink_ref[h], new_bias)
        for bb in range(SAMPLE_BATCH_BLOCK):
            row = (bb * ATTN_HEADS + h) * steps
            cache_out_ref[row:row + steps, :] = cache_bias
            new_out_ref[row:row + steps, :] = new_bias


def _bias_body(rel_ref, sink_ref, dist_bucket_ref, cache_bucket_ref, new_bucket_ref,
               o_ref, cache_out_ref, new_out_ref):
    _sample_bias_tables(rel_ref, sink_ref, cache_bucket_ref, new_bucket_ref, cache_out_ref, new_out_ref)
    T = CHUNK
    width = dist_bucket_ref.shape[1]
    bucket = dist_bucket_ref[...]
    own_block = lax.broadcasted_iota(jnp.int32, (2 * T, T), 0) >= T

    def per_kv_head(n, carry):
        for g in range(Q_PER_KV):
            profile = jnp.full(bucket.shape, NEG, F32)
            for bkt in range(REL_BUCKETS):
                profile = jnp.where(bucket == bkt, rel_ref[bkt * ATTN_HEADS + n * Q_PER_KV + g], profile)
            rows = jnp.broadcast_to(profile[0:1, :] * LOG2E, (2 * T, width))
            tile = pltpu.roll(rows, width - T, axis=1, stride=1, stride_axis=0)[:, :T]
            o_ref[1, n, :, g * T:(g + 1) * T] = tile
            o_ref[0, n, :, g * T:(g + 1) * T] = jnp.where(own_block, tile, NEG)
        return carry

    lax.fori_loop(0, ATTN_KV_HEADS, per_kv_head, 0)


def _bias_tables(rel_flat, sink, dist_bucket, cache_bucket, new_bucket):
    steps = cache_bucket.shape[0]
    rows = SAMPLE_BATCH_BLOCK * ATTN_HEADS * steps
    f = lambda *s: jax.ShapeDtypeStruct(s, F32)
    smem, vmem = pl.BlockSpec(memory_space=pltpu.SMEM), pl.BlockSpec(memory_space=pltpu.VMEM)
    return pl.pallas_call(
        _bias_body,
        in_specs=[smem, smem, vmem, vmem, vmem],
        out_specs=[vmem, vmem, vmem],
        out_shape=[f(2, ATTN_KV_HEADS, 2 * CHUNK, Q_PER_KV * CHUNK), f(rows, cache_bucket.shape[1]),
                   f(rows, new_bucket.shape[1])],
        compiler_params=pltpu.CompilerParams(vmem_limit_bytes=VMEM_LIMIT), name="rel_bias")(
        rel_flat, sink, dist_bucket, cache_bucket, new_bucket)


def _head_rmsnorm(x, g_mat, e_mat, w):
    ms = _dot_sel(x * x, g_mat) * (1.0 / ATTN_HEAD_DIM)
    return x * _dot_sel(lax.rsqrt(ms + EPS), e_mat) * w


def _lane_head(shape):
    return lax.broadcasted_iota(jnp.int32, shape, 1) // ATTN_HEAD_DIM


def _attn_block(q_blk, kcat, vcat_t, bias_at, sink_ref):
    T = CHUNK
    lane_head = _lane_head((2 * T, KV_WIDTH))
    zero = jnp.zeros((2 * T, KV_WIDTH), BF16)
    head = lambda h: q_blk[h * ATTN_HEAD_DIM:(h + 1) * ATTN_HEAD_DIM]
    q_cols = jnp.concatenate(
        [jnp.concatenate([head(n * Q_PER_KV + g) for n in range(ATTN_KV_HEADS)], axis=0)
         for g in range(Q_PER_KV)], axis=1)
    row_head = lax.broadcasted_iota(jnp.int32, (KV_WIDTH, 2 * T), 0) // ATTN_HEAD_DIM
    probs, vals, inv = [], [], {}
    for n in range(ATTN_KV_HEADS):
        s = _dot(jnp.where(lane_head == n, kcat, zero), q_cols)
        sink = sink_ref[n]
        cols = []
        for g in range(Q_PER_KV):
            c = slice(g * T, (g + 1) * T)
            sg = s[:, c] + bias_at(n, c)
            m = jnp.maximum(jnp.max(sg, axis=0, keepdims=True), sink[:, c])
            p = jnp.exp2(sg - m)
            inv[n, g] = 1.0 / (jnp.sum(p, axis=0, keepdims=True) + jnp.exp2(sink[:, c] - m))
            cols.append(p.astype(BF16))
        probs.append(jnp.concatenate(cols, axis=1))
        vals.append(jnp.where(row_head == n, vcat_t, zero.T))
    o_t = _dot(jnp.concatenate(vals, axis=1), jnp.concatenate(probs, axis=0))
    return jnp.concatenate(
        [o_t[n * ATTN_HEAD_DIM:(n + 1) * ATTN_HEAD_DIM, g * T:(g + 1) * T] * inv[n, g]
         for n in range(ATTN_KV_HEADS) for g in range(Q_PER_KV)], axis=0)


def _mixer_body(gz_ref, xs_ref, b_ref, c_ref, dt_ref, alog_ref, dskip_ref, nw_ref, e_ref,
                qt_ref, kt_ref, vt_ref, gzt_ref, bias_ref, sink_ref, x_ref, wo_ref,
                y_ref, st_ref, knt_ref, vnt_ref,
                state_sc, kcat_sc, vcat_t_sc, yssm_sc, yattn_t_sc):
    T = CHUNK
    step = pl.program_id(1)
    cols_step = CHUNKS_PER_STEP * T

    @pl.when(step == 0)
    def _():
        state_sc[...] = jnp.zeros_like(state_sc)
        kcat_sc[0:T, :] = jnp.zeros((T, KV_WIDTH), BF16)
        vcat_t_sc[:, 0:T] = jnp.zeros((KV_WIDTH, T), BF16)

    qn = qt_ref[...]
    kn_t = kt_ref[...]
    v_t = vt_ref[...]
    knt_ref[0] = kn_t[:, cols_step - T:]
    vnt_ref[0] = v_t[:, cols_step - T:]
    kcat_sc[T:, :] = kn_t.T.astype(BF16)
    vcat_t_sc[:, T:] = v_t.astype(BF16)
    e_mat = e_ref[...]

    a_neg = -jnp.exp(alog_ref[...])
    state = state_sc[...]
    first_variant = jnp.minimum(step, 1)
    for j in range(CHUNKS_PER_STEP):
        r = slice(j * T, (j + 1) * T)
        y, state = _ssd_chunk(gz_ref[r, :], xs_ref[r, :], b_ref[r, :], c_ref[r, :], dt_ref[r, :], a_neg,
                              dskip_ref[...], nw_ref[...], e_mat, state)
        yssm_sc[r, :] = y.astype(BF16)
        variant = first_variant if j == 0 else 1
        y_t = _attn_block(qn[:, r], kcat_sc[j * T:(j + 2) * T, :], vcat_t_sc[:, j * T:(j + 2) * T],
                          lambda n, c, variant=variant: bias_ref[variant, n, :, c], sink_ref)
        yattn_t_sc[:, r] = (y_t * gzt_ref[:, r]).astype(BF16)
    state_sc[...] = state
    kcat_sc[0:T, :] = kcat_sc[cols_step:, :]
    vcat_t_sc[:, 0:T] = vcat_t_sc[:, cols_step:]

    y_ref[...] = (x_ref[...] + _dot(yssm_sc[...], wo_ref[:SSM_WIDTH, :])
                  + _dot_tn(yattn_t_sc[...], wo_ref[SSM_WIDTH:, :]))

    @pl.when(step == pl.num_programs(1) - 1)
    def _():
        st_ref[0] = state.T


def _mixer(gz, xs, b, c, dt, alog, dskip, norm_w, e_mat, q_t, k_t, v_t, gz_t, bias_t, sink_rows,
           x2d, w_out, batch, seq):
    step_rows = CHUNKS_PER_STEP * CHUNK
    ns = seq // step_rows
    row = lambda w: pl.BlockSpec((step_rows, w), lambda b, i: (b * ns + i, 0))
    col = lambda w: pl.BlockSpec((w, step_rows), lambda b, i: (0, b * ns + i))
    resident = lambda a: pl.BlockSpec(a.shape, lambda b, i: (0,) * a.ndim, pipeline_mode=pl.Buffered(1))
    in_specs = [row(SSM_WIDTH), row(SSM_WIDTH), row(BC_WIDTH), row(BC_WIDTH), row(LANES),
                _const_spec((1, LANES)), _const_spec((1, SSM_WIDTH)), _const_spec((1, SSM_WIDTH)), resident(e_mat),
                col(ATTN_WIDTH), col(KV_WIDTH), col(KV_WIDTH), col(ATTN_WIDTH),
                resident(bias_t), _const_spec(sink_rows.shape), row(D_MODEL), resident(w_out)]
    kv_out = pl.BlockSpec((1, KV_WIDTH, CHUNK), lambda b, i: (b, 0, 0))
    out_specs = [row(D_MODEL), pl.BlockSpec((1, SSM_WIDTH, SSM_STATE), lambda b, i: (b, 0, 0)), kv_out, kv_out]
    out_shape = [jax.ShapeDtypeStruct((batch * seq, D_MODEL), F32),
                 jax.ShapeDtypeStruct((batch, SSM_WIDTH, SSM_STATE), F32),
                 jax.ShapeDtypeStruct((batch, KV_WIDTH, CHUNK), F32),
                 jax.ShapeDtypeStruct((batch, KV_WIDTH, CHUNK), F32)]
    scratch = [pltpu.VMEM((SSM_STATE, SSM_WIDTH), F32),
               pltpu.VMEM((step_rows + CHUNK, KV_WIDTH), BF16), pltpu.VMEM((KV_WIDTH, step_rows + CHUNK), BF16),
               pltpu.VMEM((step_rows, SSM_WIDTH), BF16), pltpu.VMEM((ATTN_WIDTH, step_rows), BF16)]
    return pl.pallas_call(
        _mixer_body, grid=(batch, ns), in_specs=in_specs, out_specs=out_specs, out_shape=out_shape,
        scratch_shapes=scratch, compiler_params=_params("arbitrary", "arbitrary"), name="mixer")(
            gz, xs, b, c, dt, alog, dskip, norm_w, e_mat, q_t, k_t, v_t, gz_t, bias_t, sink_rows, x2d, w_out)


def _sample_front_body(x_ref, nw_ref, wt_ref, sconv_ref, cw_ref, cb_ref, dtb_ref, alog_ref, dskip_ref,
                       gh_ref, e_ref, qw_ref, kw_ref, g_ref,
                       ypart_ref, ea_ref, xw_ref, b_ref, c_ref, cdec_ref, convnew_ref,
                       gz_ref, qn_ref, kn_ref, v_ref, za_ref):
    steps, nb = ypart_ref.shape[0], ypart_ref.shape[1]
    tail = CONV_WIDTH - 1
    h = _normed_input(x_ref, nw_ref)
    proj = lambda name: _dot_nt(h, wt_ref[IN_ROWS[name], :])
    slab = lambda a, l: a[l * nb:(l + 1) * nb]
    xbc = proj("xbc")
    dt_raw = _dt_projection(h, wt_ref)
    gz_ref[...] = _silu(proj("z"))
    v_ref[...] = proj("v")
    za_ref[...] = proj("za")
    g_mat = g_ref[...]
    e_mat = e_ref[...]
    qn = _head_rmsnorm(proj("q"), g_mat, e_mat, qw_ref[...]) * ATTN_SCALE
    head = lambda hd: qn[:, hd * ATTN_HEAD_DIM:(hd + 1) * ATTN_HEAD_DIM]
    qn_ref[...] = jnp.concatenate(
        [head(n * Q_PER_KV + g) for g in range(Q_PER_KV) for n in range(ATTN_KV_HEADS)], axis=1)
    kn_ref[...] = _head_rmsnorm(proj("k"), g_mat[:KV_WIDTH], e_mat[:, :KV_WIDTH], kw_ref[...])

    full = [sconv_ref[j] for j in range(tail)] + [slab(xbc, l) for l in range(steps)]
    for j in range(tail):
        convnew_ref[j] = full[steps + j]
    gh = gh_ref[...]
    a_neg = -jnp.exp(alog_ref[...])
    xs, bm, cm, dts, acum = [], [], [], [], []
    run = None
    for l in range(steps):
        conv = cb_ref[...]
        for tap in range(CONV_WIDTH):
            conv = conv + full[l + tap] * cw_ref[tap:tap + 1, :]
        act = _silu(conv)
        xs.append(act[:, :SSM_WIDTH])
        bm.append(act[:, SSM_WIDTH:SSM_WIDTH + BC_WIDTH])
        cm.append(act[:, SSM_WIDTH + BC_WIDTH:])
        d = _softplus(slab(dt_raw, l) + dtb_ref[...])
        dts.append(d)
        run = d * a_neg if run is None else run + d * a_neg
        acum.append(run)
        b_ref[l] = bm[l]
        c_ref[l] = cm[l]
    for l in range(steps):
        y = dskip_ref[...] * xs[l]
        for s in range(l + 1):
            cb_h = _dot_sel(cm[l] * bm[s], gh)
            coef = cb_h * jnp.exp(acum[l] - acum[s]) * dts[s]
            y = y + _dot_sel(coef, e_mat) * xs[s]
        ypart_ref[l] = y
        ea_ref[l] = _dot_sel(jnp.exp(acum[l]), e_mat)
        xw_ref[l] = xs[l] * _dot_sel(dts[l] * jnp.exp(acum[steps - 1] - acum[l]), e_mat)
    cdec_ref[...] = jnp.exp(acum[steps - 1])


def _sample_front(x2d, norm_w, w_t, sconv3, conv_w, conv_b, dtb, alog, dskip, gh_mat, e_mat, qw, kw, g_mat,
                  steps, nb):
    rows = steps * nb
    f = lambda *s: jax.ShapeDtypeStruct(s, F32)
    out_shape = [f(steps, nb, SSM_WIDTH), f(steps, nb, SSM_WIDTH), f(steps, nb, SSM_WIDTH),
                 f(steps, nb, BC_WIDTH), f(steps, nb, BC_WIDTH), f(nb, LANES), f(CONV_WIDTH - 1, nb, CONV_DIM),
                 f(rows, SSM_WIDTH), f(rows, ATTN_WIDTH), f(rows, KV_WIDTH), f(rows, KV_WIDTH), f(rows, ATTN_WIDTH)]
    return pl.pallas_call(_sample_front_body, out_shape=out_shape,
                          compiler_params=pltpu.CompilerParams(vmem_limit_bytes=VMEM_LIMIT),
                          name="sample_front")(
        x2d, norm_w, w_t, sconv3, conv_w, conv_b, dtb, alog, dskip, gh_mat, e_mat, qw, kw, g_mat)


def _ssd_sample_state_block(i, cdec_ref, st_ref, c_ref, b_ref, xw_ref, new_ref, yoff_ref):
    heads_per_group = SSM_HEADS // SSM_GROUPS
    for j in range(SAMPLE_BATCH_BLOCK):
        st = st_ref[j]
        cb_bf = c_ref[:, j, :].astype(BF16)
        bb_bf = b_ref[:, j, :].astype(BF16)
        xw_bf = xw_ref[:, j, :].astype(BF16)
        y_parts = []
        for g in range(SSM_GROUPS):
            rows = slice(g * GROUP_WIDTH, (g + 1) * GROUP_WIDTH)
            ns = slice(g * SSM_STATE, (g + 1) * SSM_STATE)
            y_parts.append(_dot_nt(cb_bf[:, ns], st[rows].astype(BF16)))
            upd = _dot_tn(xw_bf[:, rows], bb_bf[:, ns])
            for hh in range(heads_per_group):
                h = g * heads_per_group + hh
                r = slice(h * SSM_HEAD_DIM, (h + 1) * SSM_HEAD_DIM)
                dec = cdec_ref[i * SAMPLE_BATCH_BLOCK + j, h]
                new_ref[j, r, :] = st[r] * dec + upd[hh * SSM_HEAD_DIM:(hh + 1) * SSM_HEAD_DIM]
        yoff_ref[:, j, :] = jnp.concatenate(y_parts, axis=1)


def _sample_back_body(ypart_ref, yoff_ref, ea_ref, gz_ref, nw_ref, ya_ref, x_ref, wo_ref, o_ref):
    y = ypart_ref[...] + yoff_ref[...] * ea_ref[...]
    y_ssm = _group_rmsnorm(y * gz_ref[...], nw_ref[...])
    mixed = _dot(y_ssm.astype(BF16), wo_ref[:SSM_WIDTH, :]) + _dot(ya_ref[...].astype(BF16), wo_ref[SSM_WIDTH:, :])
    nb, steps = o_ref.shape[0], o_ref.shape[1]
    for l in range(steps):
        o_ref[:, l, :] = x_ref[:, l, :] + mixed[l * nb:(l + 1) * nb]


def _sample_back(ypart, yoff, ea, gz, norm_w, y_attn, x3, w_out):
    return pl.pallas_call(_sample_back_body, out_shape=jax.ShapeDtypeStruct(x3.shape, F32),
                          compiler_params=pltpu.CompilerParams(vmem_limit_bytes=VMEM_LIMIT),
                          name="sample_back")(ypart, yoff, ea, gz, norm_w, y_attn, x3, w_out)


def _attn_sample_body(q_ref, kn_ref, vn_ref, z_ref, ckt_ref, cvt_ref, biasc_ref, biasn_ref,
                      y_ref, kot_ref, vot_ref):
    steps = q_ref.shape[0]
    bb = SAMPLE_BATCH_BLOCK
    blk = Q_PER_KV * steps
    rows = ATTN_KV_HEADS * blk
    pad = jnp.zeros((SUBLANES - steps, KV_WIDTH), F32)
    lane_head = _lane_head((blk, KV_WIDTH))
    zero = jnp.zeros((blk, KV_WIDTH), F32)

    s_c, s_n, k8, v8 = [], [], [], []
    for j in range(bb):
        q = q_ref[:, j, :]
        qg = jnp.concatenate([q[:, g * KV_WIDTH:(g + 1) * KV_WIDTH] for g in range(Q_PER_KV)], axis=0)
        qx = jnp.concatenate([jnp.where(lane_head == n, qg, zero) for n in range(ATTN_KV_HEADS)], axis=0)
        qx = qx.astype(BF16)
        k8.append(jnp.concatenate([kn_ref[:, j, :], pad], axis=0))
        v8.append(jnp.concatenate([vn_ref[:, j, :], pad], axis=0))
        s_c.append(_dot(qx, ckt_ref[j].astype(BF16)))
        s_n.append(_dot_nt(qx, k8[j].astype(BF16)))
    s_c = jnp.concatenate(s_c, axis=0) + biasc_ref[...]
    s_n = jnp.concatenate(s_n, axis=0) + biasn_ref[...]
    m = jnp.maximum(jnp.max(s_c, axis=-1, keepdims=True), jnp.max(s_n, axis=-1, keepdims=True))
    p_c = jnp.exp(s_c - m)
    p_n = jnp.exp(s_n - m)
    inv = 1.0 / (jnp.sum(p_c, axis=-1, keepdims=True) + jnp.sum(p_n, axis=-1, keepdims=True))
    p_c = (p_c * inv).astype(BF16)
    p_n = (p_n * inv).astype(BF16)

    lane = lax.broadcasted_iota(jnp.int32, (KV_WIDTH, WINDOW), 1)
    for j in range(bb):
        r = slice(j * rows, (j + 1) * rows)
        o = _dot_nt(p_c[r], cvt_ref[j].astype(BF16)) + _dot(p_n[r], v8[j].astype(BF16))
        og = zero
        for n in range(ATTN_KV_HEADS):
            og = og + jnp.where(lane_head == n, o[n * blk:(n + 1) * blk], zero)
        y = jnp.concatenate(
            [og[g * steps:(g + 1) * steps, n * ATTN_HEAD_DIM:(n + 1) * ATTN_HEAD_DIM]
             for n in range(ATTN_KV_HEADS) for g in range(Q_PER_KV)], axis=1)
        y_ref[:, j, :] = y * _silu(z_ref[:, j, :])

        for new8, old_ref, out_ref in ((k8[j], ckt_ref, kot_ref), (v8[j], cvt_ref, vot_ref)):
            tail_rows = jnp.concatenate([new8[steps:], new8[:steps]], axis=0)
            block = jnp.concatenate([jnp.zeros((WINDOW - SUBLANES, KV_WIDTH), F32), tail_rows], axis=0)
            shifted = pltpu.roll(old_ref[j], WINDOW - steps, axis=1)
            out_ref[j] = jnp.where(lane >= WINDOW - steps, block.T, shifted)


def _attn_sample(q3, kn3, vn3, z3, cache_kt, cache_vt, bias_c, bias_n):
    steps, nb = q3.shape[0], q3.shape[1]
    bb = SAMPLE_BATCH_BLOCK
    tok = lambda w: pl.BlockSpec((steps, bb, w), lambda i: (0, i, 0))
    cache_spec = pl.BlockSpec((bb, KV_WIDTH, WINDOW), lambda i: (i, 0, 0))
    return pl.pallas_call(
        _attn_sample_body, grid=(nb // bb,),
        in_specs=[tok(ATTN_WIDTH), tok(KV_WIDTH), tok(KV_WIDTH), tok(ATTN_WIDTH), cache_spec, cache_spec,
                  _const_spec(bias_c.shape), _const_spec(bias_n.shape)],
        out_specs=[tok(ATTN_WIDTH), cache_spec, cache_spec],
        out_shape=[jax.ShapeDtypeStruct((steps, nb, ATTN_WIDTH), F32),
                   jax.ShapeDtypeStruct(cache_kt.shape, F32), jax.ShapeDtypeStruct(cache_vt.shape, F32)],
        compiler_params=_params("parallel"), name="attn_sample")(
            q3, kn3, vn3, z3, cache_kt, cache_vt, bias_c, bias_n)


def _static_tables(steps):
    lanes = np.arange(ATTN_WIDTH)
    g_mat = np.zeros((ATTN_WIDTH, LANES), np.float32)
    g_mat[lanes, lanes // ATTN_HEAD_DIM] = 1.0
    e_mat = g_mat.T.copy()
    bc = np.arange(BC_WIDTH)
    gh_mat = np.zeros((BC_WIDTH, LANES), np.float32)
    for h in range(SSM_HEADS):
        gh_mat[bc // SSM_STATE == h // (SSM_HEADS // SSM_GROUPS), h] = 1.0
    assert WINDOW <= CHUNK
    prompt_buckets = np.tile(_bucket_or_masked(np.arange(2 * CHUNK))[None, :], (SUBLANES, 1))
    dist_c = (np.arange(steps) + WINDOW)[:, None] - np.arange(WINDOW)[None, :]
    dist_n = np.arange(steps)[:, None] - np.arange(SUBLANES)[None, :]
    real = np.broadcast_to((np.arange(SUBLANES) < steps)[None, :], dist_n.shape)
    return dict(g=g_mat, e=e_mat, gh=gh_mat, prompt_buckets=prompt_buckets,
                cache_buckets=_bucket_or_masked(dist_c), new_buckets=_bucket_or_masked(dist_n, real))


def kernel(x_prompt, x_sample, cache_k, cache_v, state_ssm, state_conv, norm_w, w_in, conv_w, conv_b, dt_bias,
           a_log, d_skip, ssm_norm_w, q_norm_w, k_norm_w, sinks, rel_table, w_out):
    assert w_in.shape[0] == 1, "single-layer kernel"
    batch, seq, _ = x_prompt.shape
    nb, steps, _ = x_sample.shape
    tab = _static_tables(steps)
    g_mat = jnp.asarray(tab["g"], BF16)
    e_mat = jnp.asarray(tab["e"], BF16)
    gh_mat = jnp.asarray(tab["gh"], BF16)

    w_t = jnp.transpose(w_in[0]).astype(BF16)

    row = lambda v, width: jnp.pad(v.reshape(1, -1), ((0, 0), (0, width - v.size)))
    nw = row(norm_w[0], D_MODEL)
    cw = conv_w[0]
    cb = row(conv_b[0], CONV_DIM)
    dtb = row(dt_bias[0], LANES)
    alog = row(a_log[0], LANES)
    dskip = jnp.repeat(d_skip[0], SSM_HEAD_DIM).reshape(1, SSM_WIDTH)
    snw = row(ssm_norm_w[0], SSM_WIDTH)
    qw = jnp.tile(q_norm_w[0], ATTN_HEADS).reshape(1, ATTN_WIDTH)
    kw = jnp.tile(k_norm_w[0], ATTN_KV_HEADS).reshape(1, KV_WIDTH)
    sink = sinks[0]
    rel_flat = rel_table.reshape(-1)

    xs = jnp.swapaxes(x_sample, 0, 1).reshape(steps * nb, D_MODEL)
    t3 = lambda a: a.reshape(steps, nb, a.shape[-1])
    sconv3 = jnp.swapaxes(state_conv[0], 0, 1)
    ypart, ea, xw, b3, c3, cdec, conv_s3, gz_smp, qn, kn, v_smp, za = _sample_front(
        xs, nw, w_t, sconv3, cw, cb, dtb, alog, dskip, gh_mat, e_mat, qw, kw, g_mat, steps, nb)
    state_in = state_ssm[0].reshape(nb, SSM_WIDTH, SSM_STATE)

    xp = x_prompt.reshape(batch * seq, D_MODEL)
    qw_t = jnp.broadcast_to((qw * (ATTN_SCALE * LOG2E)).reshape(ATTN_WIDTH, 1), (ATTN_WIDTH, PROJ_ROWS))
    kw_t = jnp.broadcast_to(kw.reshape(KV_WIDTH, 1), (KV_WIDTH, PROJ_ROWS))
    gz, xs_p, b_p, c_p, dt_p, q_t, k_t, v_t, gza_t, tail_p, st_s, yoff, wo_all = _inproj_prompt(
        xp, nw, w_t, cw, cb, dtb, qw_t, kw_t, batch, seq, cdec, state_in, c3, b3, xw,
        w_out[0])
    bias_t, bias_c, bias_n = _bias_tables(rel_flat, sink, jnp.asarray(tab["prompt_buckets"]),
                                          jnp.asarray(tab["cache_buckets"]), jnp.asarray(tab["new_buckets"]))
    sink_rows = jnp.repeat(sink.reshape(ATTN_KV_HEADS, Q_PER_KV) * LOG2E, CHUNK, axis=1)
    sink_rows = sink_rows.reshape(ATTN_KV_HEADS, 1, -1)
    y_p, st_p, k_pt, v_pt = _mixer(gz, xs_p, b_p, c_p, dt_p, alog, dskip, snw, e_mat, q_t, k_t, v_t, gza_t,
                                   bias_t, sink_rows, xp, wo_all, batch, seq)
    y_p = y_p.reshape(batch, seq, D_MODEL)
    conv_p = tail_p[:, SUBLANES - (CONV_WIDTH - 1):, :]

    f2 = lambda a: a.reshape(steps * nb, a.shape[-1])
    to_t = lambda a: jnp.transpose(a[0], (0, 2, 3, 1)).reshape(nb, KV_WIDTH, WINDOW)
    from_t = lambda a: jnp.transpose(
        a.reshape(a.shape[0], ATTN_KV_HEADS, ATTN_HEAD_DIM, WINDOW), (0, 3, 1, 2))[None]
    y_attn3, k_st, v_st = _attn_sample(t3(qn), t3(kn), t3(v_smp), t3(za), to_t(cache_k), to_t(cache_v),
                                       bias_c, bias_n)
    k_s, v_s = from_t(k_st), from_t(v_st)
    y_s = _sample_back(f2(ypart), f2(yoff), f2(ea), gz_smp, snw, f2(y_attn3), x_sample, wo_all)

    st5 = lambda a: a.reshape(1, a.shape[0], SSM_HEADS, SSM_HEAD_DIM, SSM_STATE)
    return (y_p, y_s, from_t(k_pt), from_t(v_pt), st5(st_p), conv_p[None],
            k_s, v_s, st5(st_s), jnp.swapaxes(conv_s3, 0, 1)[None])
```
